```python
import math
import jax
import jax.numpy as jnp
from jax import lax
import numpy as np

D_MODEL = 2048
BATCH = 4
SEQ = 2048
DEPTH = 2
DEC_BATCH = 128
DEC_SEQ = 1
PAST_LEN = 16384
PAGE_SIZE = 128

RET_W = D_MODEL // 2
RET_H = 4
RET_DK = RET_W // RET_H
RET_DV = RET_W // RET_H
RET_CHUNK = 128
RET_GN_EPS = 1e-6
ROPE_BASE = 10000.0
RWKV_W = D_MODEL // 2
RWKV_N = 64
RWKV_H = RWKV_W // RWKV_N
DECAY_LORA = 64
AAA_LORA = 64
GATE_LORA = 160
RWKV_PROJ = 3 * RWKV_W + DECAY_LORA + AAA_LORA + GATE_LORA
RWKV_SPLITS = (RWKV_W, 2 * RWKV_W, 3 * RWKV_W, 3 * RWKV_W + DECAY_LORA, 3 * RWKV_W + DECAY_LORA + AAA_LORA)
RWKV_LN_EPS = 64e-5
S5_W = D_MODEL // 2
S5_GC = 16
S5_G = S5_W // S5_GC
S5_P = 64
N_BRANCH = 3
IN_SPLITS = (RET_W, 2 * RET_W, 3 * RET_W, 4 * RET_W, 4 * RET_W + RWKV_PROJ, 4 * RET_W + RWKV_PROJ + S5_W)
IN_W = 4 * RET_W + RWKV_PROJ + S5_W + N_BRANCH * D_MODEL
N_GROUPS = 4
EXPERTS_PER_GROUP = 8
N_EXPERTS = N_GROUPS * EXPERTS_PER_GROUP
TOP_K = 2
D_EXPERT = D_MODEL // 4
MOE_BLOCK = 128
N_MOD = 6
RMS_EPS = 1e-6

kernel_name = 'hybrid_retention_rwkv7_s5_hmoe_adaln_step'

LAYER_KEYS = ('norm_mix', 'norm_ffn', 'w_ada', 'b_ada', 'w_in', 'ret_w_o', 'rwkv_mu', 'rwkv_w0', 'rwkv_w2',
              'rwkv_a0', 'rwkv_a2', 'rwkv_g2', 'rwkv_k_k', 'rwkv_k_a', 'rwkv_r_k', 'rwkv_ln_w', 'rwkv_ln_b',
              'rwkv_w_o', 's5_a_re', 's5_a_im', 's5_b_re', 's5_b_im', 's5_c_re', 's5_c_im', 's5_d', 's5_log_dt',
              's5_w_glu', 's5_w_o', 'w_out', 'moe_w_group', 'moe_b_group', 'moe_w_router', 'moe_b_router',
              'moe_w1', 'moe_w3', 'moe_w2')


def _rmsnorm(x, g):
    xf = x.astype(jnp.float32)
    y = xf * lax.rsqrt(jnp.mean(xf * xf, axis=-1, keepdims=True) + RMS_EPS)
    return (y * g.astype(jnp.float32)).astype(x.dtype)


def _head_norm(y, eps):
    mu = jnp.mean(y, axis=-1, keepdims=True)
    yc = y - mu
    return yc * lax.rsqrt(jnp.mean(yc * yc, axis=-1, keepdims=True) + eps)


def _rotary(x, pos):
    half = x.shape[-1] // 2
    inv = ROPE_BASE ** (-jnp.arange(half, dtype=jnp.float32) / half)
    ang = pos[:, None] * inv[None, :]
    cos = jnp.cos(ang)[None, :, None, :]
    sin = jnp.sin(ang)[None, :, None, :]
    x1, x2 = x[..., :half], x[..., half:]
    return jnp.concatenate([x1 * cos - x2 * sin, x1 * sin + x2 * cos], axis=-1)


def _retention(q, k, v, s0):
    Bsz, L, H, dk = q.shape
    dv = v.shape[-1]
    C = RET_CHUNK if L % RET_CHUNK == 0 else L
    n = L // C
    log_g = jnp.log1p(-jnp.exp2(-5.0 - jnp.arange(H, dtype=jnp.float32)))
    i = jnp.arange(C, dtype=jnp.float32)
    diff = i[:, None] - i[None, :]
    causal = diff >= 0
    dmask = jnp.where(causal, jnp.exp(jnp.where(causal, diff, 0.0)[None] * log_g[:, None, None]), 0.0)
    qc = q.reshape(Bsz, n, C, H, dk)
    kc = k.reshape(Bsz, n, C, H, dk)
    vc = v.reshape(Bsz, n, C, H, dv)
    scores = jnp.einsum('bnihd,bnjhd->bnhij', qc, kc) * dmask
    intra = jnp.einsum('bnhij,bnjhv->bnihv', scores, vc)
    kdec = jnp.exp((C - 1.0 - i)[:, None] * log_g[None, :])
    chunk_kv = jnp.einsum('bnjhd,bnjhv,jh->nbhdv', kc, vc, kdec)
    g_chunk = jnp.exp(C * log_g)[None, :, None, None]

    def step(S, kv):
        return S * g_chunk + kv, S

    s_last, s_prev = lax.scan(step, s0, chunk_kv)
    qdec = jnp.exp((i + 1.0)[:, None] * log_g[None, :])
    cross = jnp.einsum('bnihd,ih,nbhdv->bnihv', qc, qdec, s_prev)
    return (intra + cross).reshape(Bsz, L, H, dv), s_last


def _rwkv7(rw, s_shift, s_wkv, p):
    Bsz, L, _ = rw.shape
    dt = rw.dtype
    f32 = jnp.float32
    prev = jnp.concatenate([s_shift[:, None].astype(dt), rw[:, :-1]], axis=1)
    m = rw + (prev - rw) * p['rwkv_mu']
    r, k, v, xw, xa, xg = jnp.split(m, RWKV_SPLITS, axis=-1)

    def heads(t):
        return t.astype(f32).reshape(Bsz, L, RWKV_H, RWKV_N)

    w_log = -jax.nn.softplus(-(p['rwkv_w0'] + jnp.tanh(xw) @ p['rwkv_w2']).astype(f32)) - 0.5
    decay = jnp.exp(-jnp.exp(w_log))
    a = jax.nn.sigmoid((p['rwkv_a0'] + xa @ p['rwkv_a2']).astype(f32))
    g = jax.nn.sigmoid(xg) @ p['rwkv_g2']
    kf = k.astype(f32)
    kk = heads(kf * p['rwkv_k_k'].astype(f32))
    kk = kk / jnp.maximum(jnp.sqrt(jnp.sum(kk * kk, axis=-1, keepdims=True)), 1e-12)
    kf = kf * (1.0 + (a - 1.0) * p['rwkv_k_a'].astype(f32))
    rh, wh, kh, vh, ah = heads(r), heads(decay), heads(kf), heads(v), heads(a)

    def step(S, inp):
        r_t, w_t, k_t, v_t, kk_t, a_t = inp
        sa = jnp.einsum('bhvk,bhk->bhv', S, -kk_t)
        S = S * w_t[:, :, None, :] + sa[..., None] * (kk_t * a_t)[:, :, None, :] + v_t[..., None] * k_t[:, :, None, :]
        return S, jnp.einsum('bhvk,bhk->bhv', S, r_t)

    seq = tuple(jnp.swapaxes(t, 0, 1) for t in (rh, wh, kh, vh, kk, ah))
    s_last, ys = lax.scan(step, s_wkv.astype(f32), seq)
    y = jnp.swapaxes(ys, 0, 1)
    y = _head_norm(y, RWKV_LN_EPS) * p['rwkv_ln_w'].astype(f32).reshape(RWKV_H, RWKV_N) \
        + p['rwkv_ln_b'].astype(f32).reshape(RWKV_H, RWKV_N)
    y = y + jnp.sum(rh * kh * p['rwkv_r_k'].astype(f32), axis=-1, keepdims=True) * vh
    y = y.reshape(Bsz, L, RWKV_W).astype(dt) * g
    return y, s_last, rw[:, -1]


def _complex_affine_combine(e1, e2):
    a1r, a1i, b1r, b1i = e1
    a2r, a2i, b2r, b2i = e2
    return (a2r * a1r - a2i * a1i, a2r * a1i + a2i * a1r,
            a2r * b1r - a2i * b1i + b2r, a2r * b1i + a2i * b1r + b2i)


def _s5(u, x0_re, x0_im, p):
    Bsz, L, _ = u.shape
    f32 = jnp.float32
    uf = u.astype(f32)
    ug = uf.reshape(Bsz, L, S5_G, S5_GC)
    a_re = p['s5_a_re'].astype(f32)
    a_im = p['s5_a_im'].astype(f32)
    dstep = jnp.exp(p['s5_log_dt'].astype(f32))[:, None]
    mag = jnp.exp(a_re * dstep)
    ab_re = mag * jnp.cos(a_im * dstep)
    ab_im = mag * jnp.sin(a_im * dstep)
    den = a_re * a_re + a_im * a_im
    n_re = ab_re - 1.0
    f_re = (n_re * a_re + ab_im * a_im) / den
    f_im = (ab_im * a_re - n_re * a_im) / den
    b_re = p['s5_b_re'].astype(f32)
    b_im = p['s5_b_im'].astype(f32)
    bb_re = f_re[..., None] * b_re - f_im[..., None] * b_im
    bb_im = f_re[..., None] * b_im + f_im[..., None] * b_re
    bu_re = jnp.einsum('blgc,gpc->blgp', ug, bb_re)
    bu_im = jnp.einsum('blgc,gpc->blgp', ug, bb_im)
    x0r = x0_re.astype(f32)
    x0i = x0_im.astype(f32)
    bu_re = bu_re.at[:, 0].add(ab_re * x0r - ab_im * x0i)
    bu_im = bu_im.at[:, 0].add(ab_re * x0i + ab_im * x0r)
    aa_re = jnp.broadcast_to(ab_re, bu_re.shape)
    aa_im = jnp.broadcast_to(ab_im, bu_im.shape)
    _, _, xs_re, xs_im = lax.associative_scan(_complex_affine_combine, (aa_re, aa_im, bu_re, bu_im), axis=1)
    y = jnp.einsum('blgp,gcp->blgc', xs_re, p['s5_c_re'].astype(f32)) \
        - jnp.einsum('blgp,gcp->blgc', xs_im, p['s5_c_im'].astype(f32))
    y = y.reshape(Bsz, L, S5_W) + p['s5_d'].astype(f32) * uf
    return y.astype(u.dtype), xs_re[:, -1], xs_im[:, -1]


def _moe(h, p):
    Bsz, L, D = h.shape
    f32 = jnp.float32
    t = h.reshape(-1, D)
    T = t.shape[0]
    g_prob = jax.nn.softmax((t @ p['moe_w_group']).astype(f32) + p['moe_b_group'].astype(f32), axis=-1)
    g_p, g_idx = lax.top_k(g_prob, 1)
    e_logits = ((t @ p['moe_w_router']).astype(f32) + p['moe_b_router'].astype(f32)).reshape(T, N_GROUPS, EXPERTS_PER_GROUP)
    e_prob = jax.nn.softmax(e_logits[jnp.arange(T), g_idx[:, 0]], axis=-1)
    e_p, e_loc = lax.top_k(e_prob, TOP_K)
    weight = g_p * e_p / jnp.sum(e_p, axis=-1, keepdims=True)
    expert = (g_idx * EXPERTS_PER_GROUP + e_loc).astype(jnp.int32)
    A = T * TOP_K
    flat_e = expert.reshape(-1)
    order = jnp.argsort(flat_e)
    sorted_e = flat_e[order]
    counts = jnp.bincount(flat_e, length=N_EXPERTS)
    padded = (counts + MOE_BLOCK - 1) // MOE_BLOCK * MOE_BLOCK
    pad_end = jnp.cumsum(padded)
    pad_start = pad_end - padded
    start = jnp.cumsum(counts) - counts
    dest = (pad_start[sorted_e] + jnp.arange(A) - start[sorted_e]).astype(jnp.int32)
    n_blocks = (A + MOE_BLOCK - 1) // MOE_BLOCK + N_EXPERTS
    M = n_blocks * MOE_BLOCK
    row_token = jnp.full((M,), T, jnp.int32).at[dest].set((order // TOP_K).astype(jnp.int32))
    t_pad = jnp.concatenate([t, jnp.zeros((1, D), t.dtype)], axis=0)
    rows = t_pad[row_token].reshape(n_blocks, MOE_BLOCK, D)
    block_e = jnp.minimum(jnp.searchsorted(pad_end, jnp.arange(n_blocks) * MOE_BLOCK, side='right'), N_EXPERTS - 1)
    w1, w3, w2 = p['moe_w1'], p['moe_w3'], p['moe_w2']

    def expert_block(args):
        xb, e = args
        return (jax.nn.silu(xb @ w1[e]) * (xb @ w3[e])) @ w2[e]

    out = lax.map(expert_block, (rows, block_e)).reshape(M, D)
    slot = jnp.zeros((A,), jnp.int32).at[order].set(dest)
    y = jnp.sum(out[slot].reshape(T, TOP_K, D) * weight[..., None], axis=1)
    return y.reshape(Bsz, L, D).astype(h.dtype)


def _layer(x, c, s_ret, s_rwkv, s_shift, s_re, s_im, pos0, p):
    Bsz, L, _ = x.shape
    dt = x.dtype
    f32 = jnp.float32
    mod = (jax.nn.silu(c) @ p['w_ada'] + p['b_ada']).reshape(Bsz, N_MOD, 1, D_MODEL).astype(dt)
    h = _rmsnorm(x, p['norm_mix']) * (1 + mod[:, 1]) + mod[:, 0]
    proj = h @ p['w_in']
    q, k, v, g_ret, rw, u, gate_logits = jnp.split(proj, IN_SPLITS, axis=-1)
    pos = pos0 + jnp.arange(L, dtype=f32)
    qh = _rotary(q.astype(f32).reshape(Bsz, L, RET_H, RET_DK), pos)
    kh = _rotary(k.astype(f32).reshape(Bsz, L, RET_H, RET_DK), pos) * (RET_DK ** -0.5)
    vh = v.astype(f32).reshape(Bsz, L, RET_H, RET_DV)
    o, s_ret_new = _retention(qh, kh, vh, s_ret.astype(f32))
    o = _head_norm(o, RET_GN_EPS).reshape(Bsz, L, RET_W).astype(dt)
    br_ret = (jax.nn.silu(g_ret) * o) @ p['ret_w_o']
    y_rw, s_rwkv_new, shift_new = _rwkv7(rw, s_shift, s_rwkv, p)
    br_rwkv = y_rw @ p['rwkv_w_o']
    y_s5, s_re_new, s_im_new = _s5(u, s_re, s_im, p)
    z = jax.nn.gelu(y_s5) @ p['s5_w_glu']
    br_s5 = (z[..., :S5_W] * jax.nn.sigmoid(z[..., S5_W:])) @ p['s5_w_o']
    gates = jax.nn.sigmoid(gate_logits).reshape(Bsz, L, N_BRANCH, D_MODEL)
    merged = gates[:, :, 0] * br_ret + gates[:, :, 1] * br_rwkv + gates[:, :, 2] * br_s5
    x = x + mod[:, 2] * (merged @ p['w_out'])
    h2 = _rmsnorm(x, p['norm_ffn']) * (1 + mod[:, 4]) + mod[:, 3]
    x = x + mod[:, 5] * _moe(h2, p)
    return x, (s_ret_new.astype(dt), s_rwkv_new.astype(dt), shift_new.astype(dt),
               s_re_new.astype(dt), s_im_new.astype(dt))


def _trunk(x, c, s_ret, s_rwkv, s_shift, s_re, s_im, pos0, layers, norm_final):
    outs = ([], [], [], [], [])
    for l in range(DEPTH):
        p = {name: arr[l] for name, arr in layers.items()}
        x, new = _layer(x, c, s_ret[l], s_rwkv[l], s_shift[l], s_re[l], s_im[l], pos0, p)
        for lst, val in zip(outs, new):
            lst.append(val)
    return _rmsnorm(x, norm_final), [jnp.stack(o) for o in outs]


def setup_inputs(seed: int = 0) -> dict:
    key = jax.random.key(seed)
    ks = iter(jax.random.split(key, 64))
    f32 = jnp.float32
    Dd = DEPTH

    def nrm(shape, scale):
        return jax.random.normal(next(ks), shape, f32) * scale

    def unif(shape, lo, hi):
        return jax.random.uniform(next(ks), shape, f32, lo, hi)

    return {
        'x_prompt': nrm((BATCH, SEQ, D_MODEL), 1.0),
        'x_sample': nrm((DEC_BATCH, DEC_SEQ, D_MODEL), 1.0),
        'state_ret': nrm((Dd, DEC_BATCH, RET_H, RET_DK, RET_DV), 0.5),
        'state_rwkv': nrm((Dd, DEC_BATCH, RWKV_H, RWKV_N, RWKV_N), 0.5),
        'state_shift': nrm((Dd, DEC_BATCH, RWKV_PROJ), 1.0),
        'state_s5_re': nrm((Dd, DEC_BATCH, S5_G, S5_P), 0.1),
        'state_s5_im': nrm((Dd, DEC_BATCH, S5_G, S5_P), 0.1),
        'c_prompt': nrm((BATCH, D_MODEL), 1.0),
        'c_sample': nrm((DEC_BATCH, D_MODEL), 1.0),
        'norm_mix': 1.0 + nrm((Dd, D_MODEL), 0.02),
        'norm_ffn': 1.0 + nrm((Dd, D_MODEL), 0.02),
        'w_ada': nrm((Dd, D_MODEL, N_MOD * D_MODEL), 0.5 * D_MODEL ** -0.5),
        'b_ada': nrm((Dd, N_MOD * D_MODEL), 0.02),
        'w_in': nrm((Dd, D_MODEL, IN_W), D_MODEL ** -0.5),
        'ret_w_o': nrm((Dd, RET_W, D_MODEL), RET_W ** -0.5),
        'rwkv_mu': unif((Dd, RWKV_PROJ), 0.0, 1.0),
        'rwkv_w0': unif((Dd, RWKV_W), -5.0, -1.0),
        'rwkv_w2': nrm((Dd, DECAY_LORA, RWKV_W), 0.5 * DECAY_LORA ** -0.5),
        'rwkv_a0': nrm((Dd, RWKV_W), 0.1),
        'rwkv_a2': nrm((Dd, AAA_LORA, RWKV_W), 0.5 * AAA_LORA ** -0.5),
        'rwkv_g2': nrm((Dd, GATE_LORA, RWKV_W), GATE_LORA ** -0.5),
        'rwkv_k_k': 0.85 + nrm((Dd, RWKV_W), 0.02),
        'rwkv_k_a': 1.0 + nrm((Dd, RWKV_W), 0.02),
        'rwkv_r_k': nrm((Dd, RWKV_H, RWKV_N), 0.1),
        'rwkv_ln_w': 1.0 + nrm((Dd, RWKV_W), 0.02),
        'rwkv_ln_b': nrm((Dd, RWKV_W), 0.02),
        'rwkv_w_o': nrm((Dd, RWKV_W, D_MODEL), RWKV_W ** -0.5),
        's5_a_re': -0.5 + nrm((Dd, S5_G, S5_P), 0.01),
        's5_a_im': math.pi * jnp.arange(S5_P, dtype=f32) + nrm((Dd, S5_G, S5_P), 0.01),
        's5_b_re': nrm((Dd, S5_G, S5_P, S5_GC), (2 * S5_GC) ** -0.5),
        's5_b_im': nrm((Dd, S5_G, S5_P, S5_GC), (2 * S5_GC) ** -0.5),
        's5_c_re': nrm((Dd, S5_G, S5_GC, S5_P), S5_P ** -0.5),
        's5_c_im': nrm((Dd, S5_G, S5_GC, S5_P), S5_P ** -0.5),
        's5_d': nrm((Dd, S5_W), 1.0),
        's5_log_dt': unif((Dd, S5_G), math.log(1e-3), math.log(1e-1)),
        's5_w_glu': nrm((Dd, S5_W, 2 * S5_W), S5_W ** -0.5),
        's5_w_o': nrm((Dd, S5_W, D_MODEL), S5_W ** -0.5),
        'w_out': nrm((Dd, D_MODEL, D_MODEL), D_MODEL ** -0.5),
        'moe_w_group': nrm((Dd, D_MODEL, N_GROUPS), D_MODEL ** -0.5),
        'moe_b_group': nrm((Dd, N_GROUPS), 0.01),
        'moe_w_router': nrm((Dd, D_MODEL, N_EXPERTS), D_MODEL ** -0.5),
        'moe_b_router': nrm((Dd, N_EXPERTS), 0.01),
        'moe_w1': nrm((Dd, N_EXPERTS, D_MODEL, D_EXPERT), D_MODEL ** -0.5),
        'moe_w3': nrm((Dd, N_EXPERTS, D_MODEL, D_EXPERT), D_MODEL ** -0.5),
        'moe_w2': nrm((Dd, N_EXPERTS, D_EXPERT, D_MODEL), D_EXPERT ** -0.5),
        'norm_final': 1.0 + nrm((D_MODEL,), 0.02),
    }


def reference(x_prompt, x_sample, state_ret, state_rwkv, state_shift, state_s5_re, state_s5_im,
              c_prompt, c_sample, norm_mix, norm_ffn, w_ada, b_ada, w_in, ret_w_o, rwkv_mu, rwkv_w0,
              rwkv_w2, rwkv_a0, rwkv_a2, rwkv_g2, rwkv_k_k, rwkv_k_a, rwkv_r_k, rwkv_ln_w, rwkv_ln_b,
              rwkv_w_o, s5_a_re, s5_a_im, s5_b_re, s5_b_im, s5_c_re, s5_c_im, s5_d, s5_log_dt, s5_w_glu,
              s5_w_o, w_out, moe_w_group, moe_b_group, moe_w_router, moe_b_router, moe_w1, moe_w3, moe_w2,
              norm_final):
    layers = {
        'norm_mix': norm_mix, 'norm_ffn': norm_ffn, 'w_ada': w_ada, 'b_ada': b_ada, 'w_in': w_in,
        'ret_w_o': ret_w_o, 'rwkv_mu': rwkv_mu, 'rwkv_w0': rwkv_w0, 'rwkv_w2': rwkv_w2,
        'rwkv_a0': rwkv_a0, 'rwkv_a2': rwkv_a2, 'rwkv_g2': rwkv_g2, 'rwkv_k_k': rwkv_k_k,
        'rwkv_k_a': rwkv_k_a, 'rwkv_r_k': rwkv_r_k, 'rwkv_ln_w': rwkv_ln_w, 'rwkv_ln_b': rwkv_ln_b,
        'rwkv_w_o': rwkv_w_o, 's5_a_re': s5_a_re, 's5_a_im': s5_a_im, 's5_b_re': s5_b_re,
        's5_b_im': s5_b_im, 's5_c_re': s5_c_re, 's5_c_im': s5_c_im, 's5_d': s5_d,
        's5_log_dt': s5_log_dt, 's5_w_glu': s5_w_glu, 's5_w_o': s5_w_o, 'w_out': w_out,
        'moe_w_group': moe_w_group, 'moe_b_group': moe_b_group, 'moe_w_router': moe_w_router,
        'moe_b_router': moe_b_router, 'moe_w1': moe_w1, 'moe_w3': moe_w3, 'moe_w2': moe_w2,
    }
    dt = x_prompt.dtype
    z_ret = jnp.zeros((DEPTH, BATCH, RET_H, RET_DK, RET_DV), dt)
    z_rwkv = jnp.zeros((DEPTH, BATCH, RWKV_H, RWKV_N, RWKV_N), dt)
    z_shift = jnp.zeros((DEPTH, BATCH, RWKV_PROJ), dt)
    z_s5 = jnp.zeros((DEPTH, BATCH, S5_G, S5_P), dt)
    y_prompt, (ret_p, rwkv_p, shift_p, s5re_p, s5im_p) = _trunk(
        x_prompt, c_prompt, z_ret, z_rwkv, z_shift, z_s5, z_s5, 0.0, layers, norm_final)
    y_sample, (ret_s, rwkv_s, shift_s, s5re_s, s5im_s) = _trunk(
        x_sample, c_sample, state_ret, state_rwkv, state_shift, state_s5_re, state_s5_im,
        float(PAST_LEN), layers, norm_final)
    return (y_prompt, y_sample, ret_p, ret_s, rwkv_p, rwkv_s, shift_p, shift_s, s5re_p, s5re_s, s5im_p, s5im_s)
```

```python
import jax
import jax.numpy as jnp
from jax import lax
from jax.experimental import pallas as pl
from jax.experimental.pallas import tpu as pltpu

f32 = jnp.float32
bf16 = jnp.bfloat16
i32 = jnp.int32

D = 2048
DEPTH = 2
PAST_LEN = 16384
RET_W, RET_H, RET_DK, RET_DV, RET_CHUNK = 1024, 4, 256, 256, 128
RET_GN_EPS = 1e-6
ROPE_BASE = 10000.0
RWKV_W, RWKV_N, RWKV_H = 1024, 64, 16
DECAY_LORA, AAA_LORA, GATE_LORA = 64, 64, 160
RWKV_PROJ = 3 * RWKV_W + DECAY_LORA + AAA_LORA + GATE_LORA
RWKV_LN_EPS = 64e-5
S5_W, S5_GC, S5_G, S5_P = 1024, 16, 64, 64
N_MOD = 6
RMS_EPS = 1e-6
N_GROUPS, EPG, N_EXPERTS, TOP_K, D_EXPERT = 4, 8, 32, 2, 512
IN_W = 4 * RET_W + RWKV_PROJ + S5_W + 3 * D
OFF_Q, OFF_K, OFF_V, OFF_G, OFF_RW = 0, 1024, 2048, 3072, 4096
OFF_U = OFF_RW + RWKV_PROJ
OFF_GATE = OFF_U + S5_W
LANE = 128
PROJ_W = ((IN_W + LANE - 1) // LANE) * LANE
VMEM_LIMIT = 56 * 1024 * 1024


def _cparams(sem):
    return pltpu.CompilerParams(dimension_semantics=sem, vmem_limit_bytes=VMEM_LIMIT)


def _dot(a, b):
    return jnp.dot(a, b, preferred_element_type=f32)


def _dot_nt(a, b):
    return lax.dot_general(a, b, (((1,), (1,)), ((), ())), preferred_element_type=f32)


def _dot_tn(a, b):
    return lax.dot_general(a, b, (((0,), (0,)), ((), ())), preferred_element_type=f32)


def _rms(x, g):
    return x * lax.rsqrt(jnp.mean(x * x, axis=-1, keepdims=True) + RMS_EPS) * g


def _head_norm(y, eps):
    mu = jnp.mean(y, axis=-1, keepdims=True)
    yc = y - mu
    return yc * lax.rsqrt(jnp.mean(yc * yc, axis=-1, keepdims=True) + eps)


def _row_tile(T):
    return 1024 if T >= 1024 else T


class _Mod:
    def __init__(self, mod, L, tm):
        self.L, self.tm = L, tm
        self.per_token = L == 1
        self.arr = mod if self.per_token else mod.reshape(mod.shape[0], 1, N_MOD * D)

    def spec(self, j, tn, col_of):
        nb = D // tn
        if self.per_token:
            return pl.BlockSpec((self.tm, tn), lambda m, n: (m, j * nb + col_of(n)))
        L, tm = self.L, self.tm
        return pl.BlockSpec((None, 1, tn), lambda m, n: ((m * tm) // L, 0, j * nb + col_of(n)))

    def row_spec(self, j, tm):
        if self.per_token:
            return pl.BlockSpec((tm, D), lambda m: (m, j))
        L = self.L
        return pl.BlockSpec((None, 1, D), lambda m: ((m * tm) // L, 0, j))


def _fused_mm(x_ops, w_ops, e_ops, pre, post, *, grid, out_specs, out_shape, cache_shapes):
    nx, nw, ne = len(x_ops), len(w_ops), len(e_ops)
    n_out = len(out_shape)

    def body(*refs):
        x_refs = refs[:nx]
        w_refs = refs[nx:nx + nw]
        e_refs = refs[nx + nw:nx + nw + ne]
        o_refs = refs[nx + nw + ne:nx + nw + ne + n_out]
        caches = refs[nx + nw + ne + n_out:]
        if cache_shapes:
            @pl.when(pl.program_id(1) == 0)
            def _():
                for i in range(nx):
                    caches[i][...] = pre(i, x_refs[i], e_refs).astype(bf16)
            lhs = [c[...] for c in caches]
        else:
            lhs = [x[...] for x in x_refs]
        prods = [_dot(lhs[w_ops[j][2]], w_refs[j][...].astype(bf16)) for j in range(nw)]
        for o_ref, o in zip(o_refs, post(prods, e_refs)):
            o_ref[...] = o.astype(o_ref.dtype)

    return pl.pallas_call(
        body,
        grid=grid,
        in_specs=[s for _, s in x_ops] + [s for _, s, _ in w_ops] + [s for _, s in e_ops],
        out_specs=out_specs,
        out_shape=out_shape,
        scratch_shapes=[pltpu.VMEM(s, bf16) for s in cache_shapes],
        compiler_params=_cparams(("parallel", "arbitrary")),
    )(*[a for a, _ in x_ops], *[a for a, _, _ in w_ops], *[a for a, _ in e_ops])


def _adaln(c_all, w_ada, b_ada):
    R = c_all.shape[0]
    tn = 1024

    def pre(i, x_ref, e_refs):
        c = x_ref[...]
        return c * jax.nn.sigmoid(c)

    def post(prods, e_refs):
        return (prods[0] + e_refs[0][...],)

    (out,) = _fused_mm(
        [(c_all, pl.BlockSpec((R, D), lambda l, n: (0, 0)))],
        [(w_ada, pl.BlockSpec((None, D, tn), lambda l, n: (l, 0, n)), 0)],
        [(b_ada.reshape(DEPTH, 1, N_MOD * D), pl.BlockSpec((None, 1, tn), lambda l, n: (l, 0, n)))],
        pre, post,
        grid=(DEPTH, N_MOD * D // tn),
        out_specs=[pl.BlockSpec((None, R, tn), lambda l, n: (l, 0, n))],
        out_shape=[jax.ShapeDtypeStruct((DEPTH, R, N_MOD * D), f32)],
        cache_shapes=[(R, D)],
    )
    return out


def _in_proj(x, g, mod, w_in):
    T = x.shape[0]
    tm, tn = mod.tm, 512

    def pre(i, x_ref, e_refs):
        g_ref, shift_ref, scale_ref = e_refs
        return _rms(x_ref[...], g_ref[...]) * (1.0 + scale_ref[...]) + shift_ref[...]

    def post(prods, e_refs):
        col = pl.program_id(1) * tn + lax.broadcasted_iota(i32, (tm, tn), 1)
        return (jnp.where(col < IN_W, prods[0], 0.0),)

    (out,) = _fused_mm(
        [(x, pl.BlockSpec((tm, D), lambda m, n: (m, 0)))],
        [(w_in, pl.BlockSpec((D, tn), lambda m, n: (0, n)), 0)],
        [(g.reshape(1, D), pl.BlockSpec((1, D), lambda m, n: (0, 0))),
         (mod.arr, mod.spec(0, D, lambda n: 0)),
         (mod.arr, mod.spec(1, D, lambda n: 0))],
        pre, post,
        grid=(T // tm, pl.cdiv(PROJ_W, tn)),
        out_specs=[pl.BlockSpec((tm, tn), lambda m, n: (m, n))],
        out_shape=[jax.ShapeDtypeStruct((T, PROJ_W), f32)],
        cache_shapes=[(tm, D)],
    )
    return out


def _glu_proj(yg, w_glu):
    T = yg.shape[0]
    tm, tn = _row_tile(T), 512
    nb = S5_W // tn

    def post(prods, e_refs):
        return (prods[0] * jax.nn.sigmoid(prods[1]),)

    (out,) = _fused_mm(
        [(yg, pl.BlockSpec((tm, S5_W), lambda m, n: (m, 0)))],
        [(w_glu, pl.BlockSpec((S5_W, tn), lambda m, n: (0, n)), 0),
         (w_glu, pl.BlockSpec((S5_W, tn), lambda m, n: (0, nb + n)), 0)],
        [], None, post,
        grid=(T // tm, nb),
        out_specs=[pl.BlockSpec((tm, tn), lambda m, n: (m, n))],
        out_shape=[jax.ShapeDtypeStruct((T, S5_W), bf16)],
        cache_shapes=[],
    )
    return out


def _merge_proj(y_ret, y_rw, y_s5, w_ret, w_rw, w_s5, proj):
    T = y_ret.shape[0]
    tm, tn = _row_tile(T), 512
    lead = OFF_GATE % LANE
    base = OFF_GATE - lead

    def gate_spec(i):
        return pl.BlockSpec((pl.Element(tm), pl.Element(tn + LANE)),
                            lambda m, n: (pl.multiple_of(m * tm, tm), pl.multiple_of(base + i * D + n * tn, LANE)))

    def post(prods, e_refs):
        acc = None
        for p, e in zip(prods, e_refs):
            t = jax.nn.sigmoid(e[:, lead:lead + tn]) * p
            acc = t if acc is None else acc + t
        return (acc,)

    xspec = pl.BlockSpec((tm, RET_W), lambda m, n: (m, 0))
    wspec = pl.BlockSpec((RET_W, tn), lambda m, n: (0, n))
    (out,) = _fused_mm(
        [(y_ret, xspec), (y_rw, xspec), (y_s5, xspec)],
        [(w_ret, wspec, 0), (w_rw, wspec, 1), (w_s5, wspec, 2)],
        [(proj, gate_spec(0)), (proj, gate_spec(1)), (proj, gate_spec(2))],
        None, post,
        grid=(T // tm, D // tn),
        out_specs=[pl.BlockSpec((tm, tn), lambda m, n: (m, n))],
        out_shape=[jax.ShapeDtypeStruct((T, D), bf16)],
        cache_shapes=[],
    )
    return out


def _out_proj(merged, w_out, x, mod):
    T = x.shape[0]
    tm, tn = mod.tm, 512

    def post(prods, e_refs):
        x_ref, m_ref = e_refs
        return (x_ref[...] + m_ref[...] * prods[0],)

    (out,) = _fused_mm(
        [(merged, pl.BlockSpec((tm, D), lambda m, n: (m, 0)))],
        [(w_out, pl.BlockSpec((D, tn), lambda m, n: (0, n)), 0)],
        [(x, pl.BlockSpec((tm, tn), lambda m, n: (m, n))), (mod.arr, mod.spec(2, tn, lambda n: n))],
        None, post,
        grid=(T // tm, D // tn),
        out_specs=[pl.BlockSpec((tm, tn), lambda m, n: (m, n))],
        out_shape=[jax.ShapeDtypeStruct((T, D), f32)],
        cache_shapes=[],
    )
    return out


def _ret_consts(L, pos0):
    C = RET_CHUNK if L % RET_CHUNK == 0 else L
    H = RET_H
    log_g = jnp.log1p(-jnp.exp2(-5.0 - jnp.arange(H, dtype=f32)))
    i = jnp.arange(C, dtype=f32)
    diff = i[:, None] - i[None, :]
    causal = diff >= 0
    dmask = jnp.where(causal, jnp.exp(jnp.where(causal, diff, 0.0)[None] * log_g[:, None, None]), 0.0)
    kdec = jnp.exp((C - 1.0 - i)[:, None] * log_g[None, :])
    qdec = jnp.exp((i + 1.0)[:, None] * log_g[None, :])
    g_chunk = jnp.exp(C * log_g)
    half = RET_DK // 2
    inv = ROPE_BASE ** (-jnp.arange(half, dtype=f32) / half)
    pos = pos0 + jnp.arange(L, dtype=f32)
    ang = pos[:, None] * inv[None, :]
    return C, dmask, kdec, qdec, g_chunk, jnp.cos(ang), jnp.sin(ang)


def _rotary(x, cos, sin):
    half = RET_DK // 2
    x1, x2 = x[..., :half], x[..., half:]
    return jnp.concatenate([x1 * cos - x2 * sin, x1 * sin + x2 * cos], axis=-1)


def _retention_seq(proj, B, L):
    C, dmask, kdec, qdec, g_chunk, cos, sin = _ret_consts(L, 0.0)
    H, dk = RET_H, RET_DK
    n = L // C
    kdec_f = jnp.broadcast_to(kdec.T[:, :, None], (H, C, dk))
    qdec_f = jnp.broadcast_to(qdec.T[:, :, None], (H, C, dk))
    gch_f = jnp.broadcast_to(g_chunk[:, None, None], (H, 8, dk))
    proj3 = proj.reshape(B, L, PROJ_W)

    def body(q_ref, k_ref, v_ref, g_ref, cos_ref, sin_ref, dm_ref, kd_ref, qd_ref, gc_ref, y_ref, s_ref, st):
        c = pl.program_id(1)

        @pl.when(c == 0)
        def _():
            st[...] = jnp.zeros_like(st)

        cs, sn = cos_ref[...], sin_ref[...]
        for h in range(H):
            sl = slice(h * dk, (h + 1) * dk)
            q = _rotary(q_ref[:, sl], cs, sn)
            k = _rotary(k_ref[:, sl], cs, sn) * (dk ** -0.5)
            vb = v_ref[:, sl].astype(bf16)
            s0 = st[h]
            scores = _dot_nt(q.astype(bf16), k.astype(bf16)) * dm_ref[h]
            o = _dot(scores.astype(bf16), vb)
            o = o + _dot((q * qd_ref[h]).astype(bf16), s0.astype(bf16))
            kv = _dot_tn((k * kd_ref[h]).astype(bf16), vb)
            st[h] = s0 * gc_ref[h, 0:1, :] + kv
            g = g_ref[:, sl]
            y_ref[:, sl] = (g * jax.nn.sigmoid(g) * _head_norm(o, RET_GN_EPS)).astype(bf16)

        @pl.when(c == n - 1)
        def _():
            s_ref[...] = st[...]

    def seg(off):
        return pl.BlockSpec((None, C, RET_W), lambda b, c: (b, c, off // RET_W))

    const3 = lambda shp: pl.BlockSpec(shp, lambda b, c: (0, 0, 0))
    y, s = pl.pallas_call(
        body,
        grid=(B, n),
        in_specs=[seg(OFF_Q), seg(OFF_K), seg(OFF_V), seg(OFF_G),
                  pl.BlockSpec((C, dk // 2), lambda b, c: (c, 0)),
                  pl.BlockSpec((C, dk // 2), lambda b, c: (c, 0)),
                  const3((H, C, C)), const3((H, C, dk)), const3((H, C, dk)), const3((H, 8, dk))],
        out_specs=[pl.BlockSpec((None, C, RET_W), lambda b, c: (b, c, 0)),
                   pl.BlockSpec((None, H, dk, RET_DV), lambda b, c: (b, 0, 0, 0))],
        out_shape=[jax.ShapeDtypeStruct((B, L, RET_W), bf16),
                   jax.ShapeDtypeStruct((B, H, dk, RET_DV), f32)],
        scratch_shapes=[pltpu.VMEM((H, dk, RET_DV), f32)],
        compiler_params=_cparams(("parallel", "arbitrary")),
    )(proj3, proj3, proj3, proj3, cos, sin, dmask, kdec_f, qdec_f, gch_f)
    return y.reshape(B * L, RET_W), s


STEP_TB = 16


def _retention_step(proj, s0, pos0):
    B = proj.shape[0]
    _, _, _, _, g_chunk, cos, sin = _ret_consts(1, pos0)
    H, dk = RET_H, RET_DK
    gch = jnp.broadcast_to(g_chunk[:, None, None], (H, 8, dk))
    tb = STEP_TB

    def body(q_ref, k_ref, v_ref, g_ref, cos_ref, sin_ref, gc_ref, s_ref, y_ref, so_ref):
        cs, sn = cos_ref[...], sin_ref[...]
        q = _rotary(q_ref[...], cs, sn)
        k = _rotary(k_ref[...], cs, sn) * (dk ** -0.5)
        v = v_ref[...]
        s1 = s_ref[...] * gc_ref[0:1, :][None] + k[:, :, None] * v[:, None, :]
        so_ref[...] = s1
        o = jnp.sum(q[:, :, None] * s1, axis=1)
        g = g_ref[...]
        y_ref[...] = (g * jax.nn.sigmoid(g) * _head_norm(o, RET_GN_EPS)).astype(bf16)

    def seg(off):
        return pl.BlockSpec((tb, dk), lambda i, h: (i, off // dk + h))

    return pl.pallas_call(
        body,
        grid=(B // tb, H),
        in_specs=[seg(OFF_Q), seg(OFF_K), seg(OFF_V), seg(OFF_G),
                  pl.BlockSpec((1, dk // 2), lambda i, h: (0, 0)),
                  pl.BlockSpec((1, dk // 2), lambda i, h: (0, 0)),
                  pl.BlockSpec((None, 8, dk), lambda i, h: (h, 0, 0)),
                  pl.BlockSpec((tb, None, dk, RET_DV), lambda i, h: (i, h, 0, 0))],
        out_specs=[pl.BlockSpec((tb, dk), lambda i, h: (i, h)),
                   pl.BlockSpec((tb, None, dk, RET_DV), lambda i, h: (i, h, 0, 0))],
        out_shape=[jax.ShapeDtypeStruct((B, RET_W), bf16),
                   jax.ShapeDtypeStruct((B, H, dk, RET_DV), f32)],
        compiler_params=_cparams(("parallel", "parallel")),
    )(proj, proj, proj, proj, cos, sin, gch, s0)


RW_C = 64
RW_Q = 4
RW_NQ = RWKV_H // RW_Q
RW_BLK = ((RWKV_PROJ + LANE - 1) // LANE) * LANE
_RW_PKEYS = ('mu', 'w0', 'w2', 'a0', 'a2', 'g2', 'k_k', 'k_a', 'r_k', 'ln_w', 'ln_b')


def _rwkv_params(p):
    return dict(
        mu=jnp.pad(p['rwkv_mu'], (0, RW_BLK - RWKV_PROJ)).reshape(1, RW_BLK),
        w0=p['rwkv_w0'].reshape(1, RWKV_W), w2=p['rwkv_w2'],
        a0=p['rwkv_a0'].reshape(1, RWKV_W), a2=p['rwkv_a2'], g2=p['rwkv_g2'],
        k_k=p['rwkv_k_k'].reshape(1, RWKV_W), k_a=p['rwkv_k_a'].reshape(1, RWKV_W),
        r_k=p['rwkv_r_k'].reshape(1, RWKV_W),
        ln_w=p['rwkv_ln_w'].reshape(1, RWKV_W), ln_b=p['rwkv_ln_b'].reshape(1, RWKV_W))


def _head_sum(x):
    R = x.shape[0]
    s = jnp.sum(x.reshape(R, RWKV_H, RWKV_N), axis=-1, keepdims=True)
    return jnp.broadcast_to(s, (R, RWKV_H, RWKV_N)).reshape(R, RWKV_W)


def _rwkv_mix(rw, prev, pr):
    m = rw + (prev - rw) * pr['mu'][...]
    r = m[:, 0:RWKV_W]
    k = m[:, RWKV_W:2 * RWKV_W]
    v = m[:, 2 * RWKV_W:3 * RWKV_W]
    o = 3 * RWKV_W
    xw = m[:, o:o + DECAY_LORA]
    xa = m[:, o + DECAY_LORA:o + DECAY_LORA + AAA_LORA]
    xg = m[:, o + DECAY_LORA + AAA_LORA:o + DECAY_LORA + AAA_LORA + GATE_LORA]
    w_log = -jax.nn.softplus(-(pr['w0'][...] + _dot(jnp.tanh(xw).astype(bf16), pr['w2'][...].astype(bf16)))) - 0.5
    lw = -jnp.exp(w_log)
    a = jax.nn.sigmoid(pr['a0'][...] + _dot(xa.astype(bf16), pr['a2'][...].astype(bf16)))
    g = _dot(jax.nn.sigmoid(xg).astype(bf16), pr['g2'][...].astype(bf16))
    kk = k * pr['k_k'][...]
    kk = kk / jnp.maximum(jnp.sqrt(_head_sum(kk * kk)), 1e-12)
    kf = k * (1.0 + (a - 1.0) * pr['k_a'][...])
    return r, lw, kf, v, kk, a, g


def _rwkv_out(y, r, kf, v, g, pr):
    R = y.shape[0]
    yn = _head_norm(y.reshape(R, RWKV_H, RWKV_N), RWKV_LN_EPS).reshape(R, RWKV_W)
    yn = yn * pr['ln_w'][...] + pr['ln_b'][...]
    yn = yn + _head_sum(r * kf * pr['r_k'][...]) * v
    return yn * g


def _rwkv_seq(proj, rp, B, L):
    C = RW_C
    assert L % C == 0 and C == RWKV_N
    n = L // C
    QW = RW_Q * RWKV_N

    def body(rw_ref, *refs):
        pr = dict(zip(_RW_PKEYS, refs[:len(_RW_PKEYS)]))
        y_ref, s_ref, sh_ref, st, carry = refs[len(_RW_PKEYS):]
        c = pl.program_id(1)

        @pl.when(c == 0)
        def _():
            st[...] = jnp.zeros_like(st)
            carry[...] = jnp.zeros_like(carry)

        rw = rw_ref[...]
        row = lax.broadcasted_iota(i32, (C, RW_BLK), 0)
        prev = jnp.where(row == 0, carry[0:1, :], pltpu.roll(rw, 1, 0))
        carry[0:1, :] = rw[C - 1:C, :]
        r, lw, kf, v, kk, a, g = _rwkv_mix(rw, prev, pr)

        ti = lax.broadcasted_iota(i32, (C, C), 0)
        si = lax.broadcasted_iota(i32, (C, C), 1)
        tril = (si <= ti).astype(f32)
        lg = jnp.dot(tril, lw, precision=lax.Precision.HIGHEST, preferred_element_type=f32)
        lgc = lg[C - 1:C, :]
        e_neg = jnp.exp(-lg)
        e_rem = jnp.exp(lgc - lg)
        at = kk * jnp.exp(lg - lw)
        ka = kk * a
        bt = ka * e_neg
        kt = kf * e_neg
        rt = r * jnp.exp(lg)
        bh = ka * e_rem
        kh = kf * e_rem
        gcr = jnp.exp(lgc)

        rr = lax.broadcasted_iota(i32, (RW_Q * C, QW), 0)
        ll = lax.broadcasted_iota(i32, (RW_Q * C, QW), 1)
        blockmask = (rr // C) == (ll // RWKV_N)
        tt = lax.broadcasted_iota(i32, (C, QW), 0)
        ss = lax.broadcasted_iota(i32, (C, QW), 1) % C
        strict = ss < tt
        incl = ss <= tt
        eye = (ss == tt).astype(f32)

        def bd(x):
            return jnp.where(blockmask, jnp.concatenate([x] * RW_Q, axis=0), 0.0).astype(bf16)

        ys = []
        for q in range(RW_NQ):
            sl = slice(q * QW, (q + 1) * QW)
            a_, r_, b_, k_, v_ = at[:, sl], rt[:, sl], bt[:, sl], kt[:, sl], v[:, sl]
            ar = jnp.concatenate([a_, r_], axis=0).astype(bf16)
            big = _dot_nt(ar, jnp.concatenate([bd(b_), bd(k_)], axis=0))
            nmat = jnp.where(strict, big[:C, :QW], 0.0)
            akm = jnp.where(strict, big[:C, QW:], 0.0)
            rbm = jnp.where(incl, big[C:, :QW], 0.0)
            rkm = jnp.where(incl, big[C:, QW:], 0.0)
            tm = eye - nmat
            pw = _dot(nmat.astype(bf16), bd(nmat))
            lvl = 2
            while lvl < C:
                res = _dot(jnp.concatenate([tm, pw], axis=0).astype(bf16), bd(pw))
                tm = tm + res[:C]
                pw = res[C:]
                lvl *= 2
            s0 = st[q]
            asrs = _dot_nt(ar, s0.astype(bf16))
            vbd = bd(v_)
            rhs = -(asrs[:C] + _dot(akm.astype(bf16), vbd))
            u = _dot(tm.astype(bf16), bd(rhs))
            y = asrs[C:] + _dot(jnp.concatenate([rbm, rkm], axis=1).astype(bf16),
                                jnp.concatenate([bd(u), vbd], axis=0))
            upd = _dot_tn(jnp.concatenate([u, v_], axis=0).astype(bf16),
                          jnp.concatenate([bh[:, sl], kh[:, sl]], axis=0).astype(bf16))
            st[q] = s0 * gcr[:, sl] + jnp.where(blockmask, upd, 0.0)
            ys.append(y)

        y = jnp.concatenate(ys, axis=1)
        y_ref[...] = _rwkv_out(y, r, kf, v, g, pr).astype(bf16)

        @pl.when(c == n - 1)
        def _():
            for q in range(RW_NQ):
                for h in range(RW_Q):
                    s_ref[q * RW_Q + h] = st[q, h * RWKV_N:(h + 1) * RWKV_N, h * RWKV_N:(h + 1) * RWKV_N]
            sh_ref[...] = rw[C - 1:C, 0:RWKV_PROJ]

    pspecs = [pl.BlockSpec(rp[k].shape, lambda b, c: (0, 0)) for k in _RW_PKEYS]
    y, s, sh = pl.pallas_call(
        body,
        grid=(B, n),
        in_specs=[pl.BlockSpec((pl.Element(C), pl.Element(RW_BLK)),
                               lambda b, c: (pl.multiple_of((b * n + c) * C, C), OFF_RW))] + pspecs,
        out_specs=[pl.BlockSpec((C, RWKV_W), lambda b, c: (b * n + c, 0)),
                   pl.BlockSpec((None, RWKV_H, RWKV_N, RWKV_N), lambda b, c: (b, 0, 0, 0)),
                   pl.BlockSpec((None, 1, RWKV_PROJ), lambda b, c: (b, 0, 0))],
        out_shape=[jax.ShapeDtypeStruct((B * L, RWKV_W), bf16),
                   jax.ShapeDtypeStruct((B, RWKV_H, RWKV_N, RWKV_N), f32),
                   jax.ShapeDtypeStruct((B, 1, RWKV_PROJ), f32)],
        scratch_shapes=[pltpu.VMEM((RW_NQ, QW, QW), f32), pltpu.VMEM((8, RW_BLK), f32)],
        compiler_params=_cparams(("parallel", "arbitrary")),
    )(proj, *[rp[k] for k in _RW_PKEYS])
    return y, s, sh.reshape(B, RWKV_PROJ)


def _rwkv_step(proj, shift, s0, rp):
    B = proj.shape[0]
    tb = STEP_TB
    shift_p = jnp.pad(shift, ((0, 0), (0, RW_BLK - RWKV_PROJ)))

    def body(rw_ref, sh_ref, s_ref, *refs):
        pr = dict(zip(_RW_PKEYS, refs[:len(_RW_PKEYS)]))
        y_ref, so_ref = refs[len(_RW_PKEYS):]
        r, lw, kf, v, kk, a, g = _rwkv_mix(rw_ref[...], sh_ref[...], pr)
        hd = lambda x: x.reshape(tb, RWKV_H, RWKV_N)
        s = s_ref[...]
        kk3 = hd(kk)
        sa = jnp.sum(s * (-kk3)[:, :, None, :], axis=-1, keepdims=True)
        s1 = (s * hd(jnp.exp(lw))[:, :, None, :] + sa * (kk3 * hd(a))[:, :, None, :]
              + hd(v)[:, :, :, None] * hd(kf)[:, :, None, :])
        so_ref[...] = s1
        y = jnp.sum(s1 * hd(r)[:, :, None, :], axis=-1).reshape(tb, RWKV_W)
        y_ref[...] = _rwkv_out(y, r, kf, v, g, pr).astype(bf16)

    pspecs = [pl.BlockSpec(rp[k].shape, lambda i: (0, 0)) for k in _RW_PKEYS]
    return pl.pallas_call(
        body,
        grid=(B // tb,),
        in_specs=[pl.BlockSpec((pl.Element(tb), pl.Element(RW_BLK)),
                               lambda i: (pl.multiple_of(i * tb, tb), OFF_RW)),
                  pl.BlockSpec((tb, RW_BLK), lambda i: (i, 0)),
                  pl.BlockSpec((tb, RWKV_H, RWKV_N, RWKV_N), lambda i: (i, 0, 0, 0))] + pspecs,
        out_specs=[pl.BlockSpec((tb, RWKV_W), lambda i: (i, 0)),
                   pl.BlockSpec((tb, RWKV_H, RWKV_N, RWKV_N), lambda i: (i, 0, 0, 0))],
        out_shape=[jax.ShapeDtypeStruct((B, RWKV_W), bf16),
                   jax.ShapeDtypeStruct((B, RWKV_H, RWKV_N, RWKV_N), f32)],
        compiler_params=_cparams(("parallel",)),
    )(proj, shift_p, s0, *[rp[k] for k in _RW_PKEYS])


S5_N = S5_G * S5_P
S5_KC = 256
S5_NKC = S5_W // S5_KC
S5_TILES = S5_N // LANE
S5_GB = S5_KC // S5_GC


def _s5_params(p):
    a_re, a_im = p['s5_a_re'], p['s5_a_im']
    dstep = jnp.exp(p['s5_log_dt'])[:, None]
    mag = jnp.exp(a_re * dstep)
    ab_re = mag * jnp.cos(a_im * dstep)
    ab_im = mag * jnp.sin(a_im * dstep)
    den = a_re * a_re + a_im * a_im
    n_re = ab_re - 1.0
    f_re = (n_re * a_re + ab_im * a_im) / den
    f_im = (ab_im * a_re - n_re * a_im) / den
    b_re, b_im = p['s5_b_re'], p['s5_b_im']
    bb_re = f_re[..., None] * b_re - f_im[..., None] * b_im
    bb_im = f_re[..., None] * b_im + f_im[..., None] * b_re
    eye = jnp.eye(S5_GB, dtype=f32)

    def in_map(bb):
        t = bb.reshape(S5_NKC, S5_GB, S5_P, S5_GC)
        return jnp.einsum('kgpc,gh->kgchp', t, eye).reshape(S5_NKC, S5_KC, S5_GB * S5_P)

    def out_map(cc):
        t = cc.reshape(S5_NKC, S5_GB, S5_GC, S5_P)
        return jnp.einsum('qgcp,gh->qgphc', t, eye).reshape(S5_NKC, S5_GB * S5_P, S5_KC)

    return dict(wb=jnp.concatenate([in_map(bb_re), in_map(bb_im)], axis=-1),
                wc_re=out_map(p['s5_c_re']), wc_im=out_map(p['s5_c_im']),
                ab_re_t=ab_re.reshape(S5_TILES // 8, 8, LANE), ab_im_t=ab_im.reshape(S5_TILES // 8, 8, LANE),
                ab_re=ab_re.reshape(1, S5_N), ab_im=ab_im.reshape(1, S5_N), d=p['s5_d'].reshape(1, S5_W))


def _s5_seq(proj, sp, B, L):
    Lc = min(L, 256)
    n = L // Lc
    pitch = Lc + 4
    lead = OFF_U % LANE
    base = OFF_U - lead
    width = S5_W + LANE
    nt4 = S5_TILES // 8
    tiles_kc = S5_TILES // S5_NKC

    def body(u_ref, wb_ref, wcr_ref, wci_ref, abr_ref, abi_ref, d_ref, y_ref, sr_ref, si_ref, xr, xi, cr, ci):
        c = pl.program_id(1)

        @pl.when(c == 0)
        def _():
            cr[...] = jnp.zeros_like(cr)
            ci[...] = jnp.zeros_like(ci)

        u = u_ref[:, lead:lead + S5_W]
        ub = u.astype(bf16)
        for kc in range(S5_NKC):
            bu = _dot(ub[:, kc * S5_KC:(kc + 1) * S5_KC], wb_ref[kc].astype(bf16))
            for j in range(tiles_kc):
                t = kc * tiles_kc + j
                xr[pl.ds(t * pitch, Lc), :] = bu[:, j * LANE:(j + 1) * LANE]
                xi[pl.ds(t * pitch, Lc), :] = bu[:, (tiles_kc + j) * LANE:(tiles_kc + j + 1) * LANE]

        abr = [abr_ref[g] for g in range(nt4)]
        abi = [abi_ref[g] for g in range(nt4)]

        def step(t, carry):
            out = []
            for g in range(nt4):
                s_r, s_i = carry[2 * g], carry[2 * g + 1]
                idx = pl.ds(g * 8 * pitch + t, 8, stride=pitch)
                n_r = abr[g] * s_r - abi[g] * s_i + xr[idx, :]
                n_i = abr[g] * s_i + abi[g] * s_r + xi[idx, :]
                xr[idx, :] = n_r
                xi[idx, :] = n_i
                out += [n_r, n_i]
            return tuple(out)

        init = []
        for g in range(nt4):
            init += [cr[g], ci[g]]
        fin = lax.fori_loop(0, Lc, step, tuple(init))
        for g in range(nt4):
            cr[g] = fin[2 * g]
            ci[g] = fin[2 * g + 1]

        for q in range(S5_NKC):
            lr = jnp.concatenate([xr[pl.ds((q * tiles_kc + j) * pitch, Lc), :] for j in range(tiles_kc)], axis=1)
            li = jnp.concatenate([xi[pl.ds((q * tiles_kc + j) * pitch, Lc), :] for j in range(tiles_kc)], axis=1)
            y = _dot(lr.astype(bf16), wcr_ref[q].astype(bf16)) - _dot(li.astype(bf16), wci_ref[q].astype(bf16))
            cs = slice(q * S5_KC, (q + 1) * S5_KC)
            y = y + d_ref[:, cs] * u[:, cs]
            y_ref[:, cs] = jax.nn.gelu(y).astype(bf16)

        @pl.when(c == n - 1)
        def _():
            sr_ref[...] = cr[...]
            si_ref[...] = ci[...]

    full = lambda shp: pl.BlockSpec(shp, lambda b, c: (0,) * len(shp))
    y, sr, si = pl.pallas_call(
        body,
        grid=(B, n),
        in_specs=[pl.BlockSpec((pl.Element(Lc), pl.Element(width)),
                               lambda b, c: (pl.multiple_of((b * n + c) * Lc, Lc), base)),
                  full(sp['wb'].shape), full(sp['wc_re'].shape), full(sp['wc_im'].shape),
                  full(sp['ab_re_t'].shape), full(sp['ab_im_t'].shape), full((1, S5_W))],
        out_specs=[pl.BlockSpec((Lc, S5_W), lambda b, c: (b * n + c, 0)),
                   pl.BlockSpec((None, nt4, 8, LANE), lambda b, c: (b, 0, 0, 0)),
                   pl.BlockSpec((None, nt4, 8, LANE), lambda b, c: (b, 0, 0, 0))],
        out_shape=[jax.ShapeDtypeStruct((B * L, S5_W), bf16),
                   jax.ShapeDtypeStruct((B, nt4, 8, LANE), f32),
                   jax.ShapeDtypeStruct((B, nt4, 8, LANE), f32)],
        scratch_shapes=[pltpu.VMEM((S5_TILES * pitch, LANE), f32), pltpu.VMEM((S5_TILES * pitch, LANE), f32),
                        pltpu.VMEM((nt4, 8, LANE), f32), pltpu.VMEM((nt4, 8, LANE), f32)],
        compiler_params=_cparams(("parallel", "arbitrary")),
    )(proj, sp['wb'], sp['wc_re'], sp['wc_im'], sp['ab_re_t'], sp['ab_im_t'], sp['d'])
    return y, sr.reshape(B, S5_G, S5_P), si.reshape(B, S5_G, S5_P)


def _s5_step(proj, x_re, x_im, sp):
    B = proj.shape[0]
    lead = OFF_U % LANE
    base = OFF_U - lead
    width = S5_W + LANE
    kw = S5_N // S5_NKC

    def body(u_ref, xr_ref, xi_ref, wb_ref, wcr_ref, wci_ref, abr_ref, abi_ref, d_ref, y_ref, sr_ref, si_ref):
        u = u_ref[:, lead:lead + S5_W]
        ub = u.astype(bf16)
        abr, abi = abr_ref[...], abi_ref[...]
        xr, xi = xr_ref[...], xi_ref[...]
        for kc in range(S5_NKC):
            bu = _dot(ub[:, kc * S5_KC:(kc + 1) * S5_KC], wb_ref[kc].astype(bf16))
            sl = slice(kc * kw, (kc + 1) * kw)
            n_r = abr[:, sl] * xr[:, sl] - abi[:, sl] * xi[:, sl] + bu[:, :kw]
            n_i = abr[:, sl] * xi[:, sl] + abi[:, sl] * xr[:, sl] + bu[:, kw:]
            sr_ref[:, sl] = n_r
            si_ref[:, sl] = n_i
            y = _dot(n_r.astype(bf16), wcr_ref[kc].astype(bf16)) - _dot(n_i.astype(bf16), wci_ref[kc].astype(bf16))
            cs = slice(kc * S5_KC, (kc + 1) * S5_KC)
            y = y + d_ref[:, cs] * u[:, cs]
            y_ref[:, cs] = jax.nn.gelu(y).astype(bf16)

    full = lambda shp: pl.BlockSpec(shp, lambda i: (0,) * len(shp))
    y, sr, si = pl.pallas_call(
        body,
        grid=(1,),
        in_specs=[pl.BlockSpec((pl.Element(B), pl.Element(width)), lambda i: (0, base)),
                  full((B, S5_N)), full((B, S5_N)),
                  full(sp['wb'].shape), full(sp['wc_re'].shape), full(sp['wc_im'].shape),
                  full((1, S5_N)), full((1, S5_N)), full((1, S5_W))],
        out_specs=[full((B, S5_W)), full((B, S5_N)), full((B, S5_N))],
        out_shape=[jax.ShapeDtypeStruct((B, S5_W), bf16),
                   jax.ShapeDtypeStruct((B, S5_N), f32), jax.ShapeDtypeStruct((B, S5_N), f32)],
        compiler_params=_cparams(("arbitrary",)),
    )(proj, x_re.reshape(B, S5_N), x_im.reshape(B, S5_N), sp['wb'], sp['wc_re'], sp['wc_im'],
      sp['ab_re'], sp['ab_im'], sp['d'])
    return y, sr.reshape(B, S5_G, S5_P), si.reshape(B, S5_G, S5_P)


MOE_RT = 256
ROUTE_W = LANE


def _router(x, g, mod, w_r, b_r):
    T = x.shape[0]
    tm = min(mod.tm, 512)

    def body(x_ref, g_ref, sh_ref, sc_ref, w_ref, b_ref, h_ref, e_ref, p_ref):
        h2 = _rms(x_ref[...], g_ref[...]) * (1.0 + sc_ref[...]) + sh_ref[...]
        h_ref[...] = h2
        logits = jnp.dot(h2, w_ref[...], precision=lax.Precision.HIGHEST, preferred_element_type=f32) + b_ref[...]
        lane = lax.broadcasted_iota(i32, (tm, ROUTE_W), 1)
        ninf = jnp.float32(-jnp.inf)
        gl = jnp.where(lane < N_GROUPS, logits, ninf)
        gm = jnp.max(gl, axis=-1, keepdims=True)
        g_p = 1.0 / jnp.sum(jnp.exp(gl - gm), axis=-1, keepdims=True)
        g_idx = jnp.min(jnp.where(gl == gm, lane, ROUTE_W), axis=-1, keepdims=True)
        valid = (lane >= N_GROUPS) & (lane < N_GROUPS + N_EXPERTS) & (((lane - N_GROUPS) // EPG) == g_idx)
        el = jnp.where(valid, logits, ninf)
        ee = jnp.exp(el - jnp.max(el, axis=-1, keepdims=True))
        prob = jnp.where(valid, ee / jnp.sum(ee, axis=-1, keepdims=True), -1.0)
        p1 = jnp.max(prob, axis=-1, keepdims=True)
        i1 = jnp.min(jnp.where(prob == p1, lane, ROUTE_W), axis=-1, keepdims=True)
        prob2 = jnp.where(lane == i1, -1.0, prob)
        p2 = jnp.max(prob2, axis=-1, keepdims=True)
        i2 = jnp.min(jnp.where(prob2 == p2, lane, ROUTE_W), axis=-1, keepdims=True)
        den = p1 + p2
        e_ref[...] = jnp.where(lane == 0, i1 - N_GROUPS, jnp.where(lane == 1, i2 - N_GROUPS, 0))
        p_ref[...] = jnp.where(lane == 0, g_p * p1 / den, jnp.where(lane == 1, g_p * p2 / den, 0.0))

    return pl.pallas_call(
        body,
        grid=(T // tm,),
        in_specs=[pl.BlockSpec((tm, D), lambda m: (m, 0)),
                  pl.BlockSpec((1, D), lambda m: (0, 0)),
                  mod.row_spec(3, tm), mod.row_spec(4, tm),
                  pl.BlockSpec((D, ROUTE_W), lambda m: (0, 0)),
                  pl.BlockSpec((1, ROUTE_W), lambda m: (0, 0))],
        out_specs=[pl.BlockSpec((tm, D), lambda m: (m, 0)),
                   pl.BlockSpec((tm, ROUTE_W), lambda m: (m, 0)),
                   pl.BlockSpec((tm, ROUTE_W), lambda m: (m, 0))],
        out_shape=[jax.ShapeDtypeStruct((T, D), f32), jax.ShapeDtypeStruct((T, ROUTE_W), i32),
                   jax.ShapeDtypeStruct((T, ROUTE_W), f32)],
        compiler_params=_cparams(("parallel",)),
    )(x, g.reshape(1, D), mod.arr, mod.arr, w_r, b_r)


def _experts(h2, order, start, counts, w1, w3, w2):
    T = h2.shape[0]
    assert TOP_K == 2
    A = T * TOP_K
    RT = MOE_RT

    def body(order_ref, start_ref, cnt_ref, h_hbm, w1_ref, w3_ref, w2_ref, o_hbm,
             xbuf, obuf, w1b, w3b, w2b, gsem, ssem):
        e = pl.program_id(0)
        cnt = cnt_ref[e]
        s0 = start_ref[e]
        ntile = (cnt + RT - 1) // RT

        @pl.when(e == 0)
        def _():
            xbuf[...] = jnp.zeros_like(xbuf)

        w1b[...] = w1_ref[...].astype(bf16)
        w3b[...] = w3_ref[...].astype(bf16)
        w2b[...] = w2_ref[...].astype(bf16)

        def rows_in(i):
            return jnp.minimum(RT, cnt - i * RT)

        def gather_copy(slot, r, tok):
            return pltpu.make_async_copy(h_hbm.at[pl.ds(tok, 1), :], xbuf.at[slot, pl.ds(r, 1), :], gsem.at[slot])

        def scatter_copy(slot, r, a):
            return pltpu.make_async_copy(obuf.at[slot, pl.ds(r, 1), :], o_hbm.at[pl.ds(a, 1), :], ssem.at[slot])

        def start_gather(i, slot):
            def one(r, _):
                a = order_ref[s0 + i * RT + r]
                gather_copy(slot, r, lax.shift_right_logical(a, 1)).start()
                return 0
            lax.fori_loop(0, rows_in(i), one, 0)

        def wait_gather(i, slot):
            def one(r, _):
                gather_copy(slot, 0, 0).wait()
                return 0
            lax.fori_loop(0, rows_in(i), one, 0)

        def start_scatter(i, slot):
            def one(r, _):
                scatter_copy(slot, r, order_ref[s0 + i * RT + r]).start()
                return 0
            lax.fori_loop(0, rows_in(i), one, 0)

        def wait_scatter(i, slot):
            def one(r, _):
                scatter_copy(slot, 0, 0).wait()
                return 0
            lax.fori_loop(0, rows_in(i), one, 0)

        @pl.when(ntile > 0)
        def _():
            start_gather(0, 0)

        def tile(i, _):
            slot = i % 2

            @pl.when(i + 1 < ntile)
            def _():
                start_gather(i + 1, 1 - slot)

            wait_gather(i, slot)
            xb = xbuf[slot].astype(bf16)
            hmid = _dot(xb, w1b[...])
            hmid = (hmid * jax.nn.sigmoid(hmid)) * _dot(xb, w3b[...])
            out = _dot(hmid.astype(bf16), w2b[...])

            @pl.when(i >= 2)
            def _():
                wait_scatter(i - 2, slot)

            obuf[slot] = out
            start_scatter(i, slot)
            return 0

        lax.fori_loop(0, ntile, tile, 0)

        @pl.when(ntile >= 2)
        def _():
            wait_scatter(ntile - 2, ntile % 2)

        @pl.when(ntile >= 1)
        def _():
            wait_scatter(ntile - 1, (ntile - 1) % 2)

    grid_spec = pltpu.PrefetchScalarGridSpec(
        num_scalar_prefetch=3,
        grid=(N_EXPERTS,),
        in_specs=[pl.BlockSpec(memory_space=pl.ANY),
                  pl.BlockSpec((None, D, D_EXPERT), lambda e, *_: (e, 0, 0)),
                  pl.BlockSpec((None, D, D_EXPERT), lambda e, *_: (e, 0, 0)),
                  pl.BlockSpec((None, D_EXPERT, D), lambda e, *_: (e, 0, 0))],
        out_specs=pl.BlockSpec(memory_space=pl.ANY),
        scratch_shapes=[pltpu.VMEM((2, RT, D), f32), pltpu.VMEM((2, RT, D), f32),
                        pltpu.VMEM((D, D_EXPERT), bf16), pltpu.VMEM((D, D_EXPERT), bf16),
                        pltpu.VMEM((D_EXPERT, D), bf16),
                        pltpu.SemaphoreType.DMA((2,)), pltpu.SemaphoreType.DMA((2,))],
    )
    return pl.pallas_call(
        body,
        grid_spec=grid_spec,
        out_shape=jax.ShapeDtypeStruct((A, D), f32),
        compiler_params=_cparams(("arbitrary",)),
    )(order, start, counts, h2, w1, w3, w2)


def _moe_combine(x, o2, wgt, mod, norm_final):
    T = x.shape[0]
    tm = min(mod.tm, 512)
    o2 = o2.reshape(T, TOP_K * D)

    def body(x_ref, o_ref, p_ref, m_ref, *rest):
        p = p_ref[...]
        y = p[:, 0:1] * o_ref[:, 0:D]
        for k in range(1, TOP_K):
            y = y + p[:, k:k + 1] * o_ref[:, k * D:(k + 1) * D]
        xo = x_ref[...] + m_ref[...] * y
        if norm_final is None:
            rest[0][...] = xo
        else:
            rest[1][...] = _rms(xo, rest[0][...])

    ins = [x, o2, wgt, mod.arr]
    specs = [pl.BlockSpec((tm, D), lambda m: (m, 0)), pl.BlockSpec((tm, TOP_K * D), lambda m: (m, 0)),
             pl.BlockSpec((tm, ROUTE_W), lambda m: (m, 0)), mod.row_spec(5, tm)]
    if norm_final is not None:
        ins.append(norm_final.reshape(1, D))
        specs.append(pl.BlockSpec((1, D), lambda m: (0, 0)))
    return pl.pallas_call(
        body,
        grid=(T // tm,),
        in_specs=specs,
        out_specs=pl.BlockSpec((tm, D), lambda m: (m, 0)),
        out_shape=jax.ShapeDtypeStruct((T, D), f32),
        compiler_params=_cparams(("parallel",)),
    )(*ins)


def _moe(x, mod, p, norm_final):
    pad = ROUTE_W - N_GROUPS - N_EXPERTS
    w_r = jnp.pad(jnp.concatenate([p['moe_w_group'], p['moe_w_router']], axis=1), ((0, 0), (0, pad)))
    b_r = jnp.pad(jnp.concatenate([p['moe_b_group'], p['moe_b_router']]), (0, pad)).reshape(1, ROUTE_W)
    h2, eid, wgt = _router(x, p['norm_ffn'], mod, w_r, b_r)
    flat_e = eid[:, :TOP_K].reshape(-1)
    order = jnp.argsort(flat_e).astype(i32)
    counts = jnp.bincount(flat_e, length=N_EXPERTS).astype(i32)
    start = (jnp.cumsum(counts) - counts).astype(i32)
    o2 = _experts(h2, order, start, counts, p['moe_w1'], p['moe_w3'], p['moe_w2'])
    return _moe_combine(x, o2, wgt, mod, norm_final)


def _trunk(x, mods, states, pos0, layers, norm_final):
    B, L, _ = x.shape
    T = B * L
    tm = _row_tile(T)
    x = x.reshape(T, D)
    outs = ([], [], [], [], [])
    for l in range(DEPTH):
        p = {name: arr[l] for name, arr in layers.items()}
        mod = _Mod(mods[l], L, tm)
        rp, sp = _rwkv_params(p), _s5_params(p)
        proj = _in_proj(x, p['norm_mix'], mod, p['w_in'])
        if states is None:
            y_ret, s_ret = _retention_seq(proj, B, L)
            y_rw, s_rw, shift = _rwkv_seq(proj, rp, B, L)
            y_s5, s_re, s_im = _s5_seq(proj, sp, B, L)
        else:
            st_ret, st_rw, st_shift, st_re, st_im = (s[l] for s in states)
            y_ret, s_ret = _retention_step(proj, st_ret, pos0)
            y_rw, s_rw = _rwkv_step(proj, st_shift, st_rw, rp)
            shift = proj[:, OFF_RW:OFF_RW + RWKV_PROJ]
            y_s5, s_re, s_im = _s5_step(proj, st_re, st_im, sp)
        z = _glu_proj(y_s5, p['s5_w_glu'])
        merged = _merge_proj(y_ret, y_rw, z, p['ret_w_o'], p['rwkv_w_o'], p['s5_w_o'], proj)
        x = _out_proj(merged, p['w_out'], x, mod)
        x = _moe(x, mod, p, norm_final if l == DEPTH - 1 else None)
        for lst, val in zip(outs, (s_ret, s_rw, shift, s_re, s_im)):
            lst.append(val)
    return x.reshape(B, L, D), [jnp.stack(o) for o in outs]


def kernel(x_prompt, x_sample, state_ret, state_rwkv, state_shift, state_s5_re, state_s5_im,
           c_prompt, c_sample, norm_mix, norm_ffn, w_ada, b_ada, w_in, ret_w_o, rwkv_mu, rwkv_w0,
           rwkv_w2, rwkv_a0, rwkv_a2, rwkv_g2, rwkv_k_k, rwkv_k_a, rwkv_r_k, rwkv_ln_w, rwkv_ln_b,
           rwkv_w_o, s5_a_re, s5_a_im, s5_b_re, s5_b_im, s5_c_re, s5_c_im, s5_d, s5_log_dt, s5_w_glu,
           s5_w_o, w_out, moe_w_group, moe_b_group, moe_w_router, moe_b_router, moe_w1, moe_w3, moe_w2,
           norm_final):
    layers = {
        'norm_mix': norm_mix, 'norm_ffn': norm_ffn, 'w_in': w_in,
        'ret_w_o': ret_w_o, 'rwkv_mu': rwkv_mu, 'rwkv_w0': rwkv_w0, 'rwkv_w2': rwkv_w2,
        'rwkv_a0': rwkv_a0, 'rwkv_a2': rwkv_a2, 'rwkv_g2': rwkv_g2, 'rwkv_k_k': rwkv_k_k,
        'rwkv_k_a': rwkv_k_a, 'rwkv_r_k': rwkv_r_k, 'rwkv_ln_w': rwkv_ln_w, 'rwkv_ln_b': rwkv_ln_b,
        'rwkv_w_o': rwkv_w_o, 's5_a_re': s5_a_re, 's5_a_im': s5_a_im, 's5_b_re': s5_b_re,
        's5_b_im': s5_b_im, 's5_c_re': s5_c_re, 's5_c_im': s5_c_im, 's5_d': s5_d,
        's5_log_dt': s5_log_dt, 's5_w_glu': s5_w_glu, 's5_w_o': s5_w_o, 'w_out': w_out,
        'moe_w_group': moe_w_group, 'moe_b_group': moe_b_group, 'moe_w_router': moe_w_router,
        'moe_b_router': moe_b_router, 'moe_w1': moe_w1, 'moe_w3': moe_w3, 'moe_w2': moe_w2,
    }
    Bp, Bs = x_prompt.shape[0], x_sample.shape[0]
    s_off = -(-Bp // 8) * 8
    c_all = jnp.concatenate([c_prompt, jnp.zeros((s_off - Bp, D), f32), c_sample], axis=0)
    mod_all = _adaln(c_all, w_ada, b_ada)
    y_prompt, (ret_p, rwkv_p, shift_p, s5re_p, s5im_p) = _trunk(
        x_prompt, mod_all[:, :Bp], None, 0.0, layers, norm_final)
    y_sample, (ret_s, rwkv_s, shift_s, s5re_s, s5im_s) = _trunk(
        x_sample, mod_all[:, s_off:s_off + Bs], (state_ret, state_rwkv, state_shift, state_s5_re, state_s5_im),
        float(PAST_LEN), layers, norm_final)
    return (y_prompt, y_sample, ret_p, ret_s, rwkv_p, rwkv_s, shift_p, shift_s, s5re_p, s5re_s, s5im_p, s5im_s)
```

```python
import jax
import jax.numpy as jnp
from jax import lax
from jax.experimental import pallas as pl
from jax.experimental.pallas import tpu as pltpu

f32 = jnp.float32
bf16 = jnp.bfloat16
i32 = jnp.int32

D = 2048
DEPTH = 2
PAST_LEN = 16384
RET_W, RET_H, RET_DK, RET_DV, RET_CHUNK = 1024, 4, 256, 256, 128
RET_GN_EPS = 1e-6
ROPE_BASE = 10000.0
RWKV_W, RWKV_N, RWKV_H = 1024, 64, 16
DECAY_LORA, AAA_LORA, GATE_LORA = 64, 64, 160
RWKV_PROJ = 3 * RWKV_W + DECAY_LORA + AAA_LORA + GATE_LORA
RWKV_LN_EPS = 64e-5
S5_W, S5_GC, S5_G, S5_P = 1024, 16, 64, 64
N_MOD = 6
RMS_EPS = 1e-6
N_GROUPS, EPG, N_EXPERTS, TOP_K, D_EXPERT = 4, 8, 32, 2, 512
IN_W = 4 * RET_W + RWKV_PROJ + S5_W + 3 * D
OFF_Q, OFF_K, OFF_V, OFF_G, OFF_RW = 0, 1024, 2048, 3072, 4096
OFF_U = OFF_RW + RWKV_PROJ
OFF_GATE = OFF_U + S5_W
LANE = 128
PROJ_W = ((IN_W + LANE - 1) // LANE) * LANE
VMEM_LIMIT = 56 * 1024 * 1024


def _cparams(sem):
    return pltpu.CompilerParams(dimension_semantics=sem, vmem_limit_bytes=VMEM_LIMIT)


def _dot(a, b):
    return jnp.dot(a, b, preferred_element_type=f32)


def _dot_nt(a, b):
    return lax.dot_general(a, b, (((1,), (1,)), ((), ())), preferred_element_type=f32)


def _dot_tn(a, b):
    return lax.dot_general(a, b, (((0,), (0,)), ((), ())), preferred_element_type=f32)


def _rms(x, g):
    return x * lax.rsqrt(jnp.mean(x * x, axis=-1, keepdims=True) + RMS_EPS) * g


def _head_norm(y, eps):
    mu = jnp.mean(y, axis=-1, keepdims=True)
    yc = y - mu
    return yc * lax.rsqrt(jnp.mean(yc * yc, axis=-1, keepdims=True) + eps)


def _row_tile(T):
    return 1024 if T >= 1024 else T


class _Mod:
    def __init__(self, mod, L, tm):
        self.L, self.tm = L, tm
        self.per_token = L == 1
        self.arr = mod if self.per_token else mod.reshape(mod.shape[0], 1, N_MOD * D)

    def spec(self, j, tn, col_of):
        nb = D // tn
        if self.per_token:
            return pl.BlockSpec((self.tm, tn), lambda m, n: (m, j * nb + col_of(n)))
        L, tm = self.L, self.tm
        return pl.BlockSpec((None, 1, tn), lambda m, n: ((m * tm) // L, 0, j * nb + col_of(n)))

    def row_spec(self, j, tm):
        if self.per_token:
            return pl.BlockSpec((tm, D), lambda m: (m, j))
        L = self.L
        return pl.BlockSpec((None, 1, D), lambda m: ((m * tm) // L, 0, j))


def _fused_mm(x_ops, w_ops, e_ops, pre, post, *, grid, out_specs, out_shape, cache_shapes):
    nx, nw, ne = len(x_ops), len(w_ops), len(e_ops)
    n_out = len(out_shape)

    def body(*refs):
        x_refs = refs[:nx]
        w_refs = refs[nx:nx + nw]
        e_refs = refs[nx + nw:nx + nw + ne]
        o_refs = refs[nx + nw + ne:nx + nw + ne + n_out]
        caches = refs[nx + nw + ne + n_out:]
        if cache_shapes:
            @pl.when(pl.program_id(1) == 0)
            def _():
                for i in range(nx):
                    caches[i][...] = pre(i, x_refs[i], e_refs).astype(bf16)
            lhs = [c[...] for c in caches]
        else:
            lhs = [x[...] for x in x_refs]
        prods = [_dot(lhs[w_ops[j][2]], w_refs[j][...].astype(bf16)) for j in range(nw)]
        for o_ref, o in zip(o_refs, post(prods, e_refs)):
            o_ref[...] = o.astype(o_ref.dtype)

    return pl.pallas_call(
        body,
        grid=grid,
        in_specs=[s for _, s in x_ops] + [s for _, s, _ in w_ops] + [s for _, s in e_ops],
        out_specs=out_specs,
        out_shape=out_shape,
        scratch_shapes=[pltpu.VMEM(s, bf16) for s in cache_shapes],
        compiler_params=_cparams(("parallel", "arbitrary")),
    )(*[a for a, _ in x_ops], *[a for a, _, _ in w_ops], *[a for a, _ in e_ops])


def _adaln(c_all, w_ada, b_ada):
    R = c_all.shape[0]
    tn = 1024

    def pre(i, x_ref, e_refs):
        c = x_ref[...]
        return c * jax.nn.sigmoid(c)

    def post(prods, e_refs):
        return (prods[0] + e_refs[0][...],)

    (out,) = _fused_mm(
        [(c_all, pl.BlockSpec((R, D), lambda l, n: (0, 0)))],
        [(w_ada, pl.BlockSpec((None, D, tn), lambda l, n: (l, 0, n)), 0)],
        [(b_ada.reshape(DEPTH, 1, N_MOD * D), pl.BlockSpec((None, 1, tn), lambda l, n: (l, 0, n)))],
        pre, post,
        grid=(DEPTH, N_MOD * D // tn),
        out_specs=[pl.BlockSpec((None, R, tn), lambda l, n: (l, 0, n))],
        out_shape=[jax.ShapeDtypeStruct((DEPTH, R, N_MOD * D), f32)],
        cache_shapes=[(R, D)],
    )
    return out


def _in_proj(x, g, l, mod, w_in):
    T = x.shape[0]
    tm, tn = mod.tm, 512

    def pre(i, x_ref, e_refs):
        g_ref, shift_ref, scale_ref = e_refs
        return _rms(x_ref[...], g_ref[...]) * (1.0 + scale_ref[...]) + shift_ref[...]

    def post(prods, e_refs):
        col = pl.program_id(1) * tn + lax.broadcasted_iota(i32, (tm, tn), 1)
        return (jnp.where(col < IN_W, prods[0], 0.0),)

    (out,) = _fused_mm(
        [(x, pl.BlockSpec((tm, D), lambda m, n: (m, 0)))],
        [(w_in, pl.BlockSpec((None, D, tn), lambda m, n: (l, 0, n)), 0)],
        [(g.reshape(DEPTH, 1, D), pl.BlockSpec((None, 1, D), lambda m, n: (l, 0, 0))),
         (mod.arr, mod.spec(0, D, lambda n: 0)),
         (mod.arr, mod.spec(1, D, lambda n: 0))],
        pre, post,
        grid=(T // tm, pl.cdiv(PROJ_W, tn)),
        out_specs=[pl.BlockSpec((tm, tn), lambda m, n: (m, n))],
        out_shape=[jax.ShapeDtypeStruct((T, PROJ_W), f32)],
        cache_shapes=[(tm, D)],
    )
    return out


def _glu_proj(yg, l, w_glu):
    T = yg.shape[0]
    tm, tn = _row_tile(T), 512
    nb = S5_W // tn

    def post(prods, e_refs):
        return (prods[0] * jax.nn.sigmoid(prods[1]),)

    (out,) = _fused_mm(
        [(yg, pl.BlockSpec((tm, S5_W), lambda m, n: (m, 0)))],
        [(w_glu, pl.BlockSpec((None, S5_W, tn), lambda m, n: (l, 0, n)), 0),
         (w_glu, pl.BlockSpec((None, S5_W, tn), lambda m, n: (l, 0, nb + n)), 0)],
        [], None, post,
        grid=(T // tm, nb),
        out_specs=[pl.BlockSpec((tm, tn), lambda m, n: (m, n))],
        out_shape=[jax.ShapeDtypeStruct((T, S5_W), bf16)],
        cache_shapes=[],
    )
    return out


def _merge_proj(y_ret, y_rw, y_s5, l, w_ret, w_rw, w_s5, proj):
    T = y_ret.shape[0]
    tm, tn = _row_tile(T), 512
    lead = OFF_GATE % LANE
    base = OFF_GATE - lead

    def gate_spec(i):
        return pl.BlockSpec((pl.Element(tm), pl.Element(tn + LANE)),
                            lambda m, n: (pl.multiple_of(m * tm, tm), pl.multiple_of(base + i * D + n * tn, LANE)))

    def post(prods, e_refs):
        acc = None
        for p, e in zip(prods, e_refs):
            t = jax.nn.sigmoid(e[:, lead:lead + tn]) * p
            acc = t if acc is None else acc + t
        return (acc,)

    xspec = pl.BlockSpec((tm, RET_W), lambda m, n: (m, 0))
    wspec = pl.BlockSpec((None, RET_W, tn), lambda m, n: (l, 0, n))
    (out,) = _fused_mm(
        [(y_ret, xspec), (y_rw, xspec), (y_s5, xspec)],
        [(w_ret, wspec, 0), (w_rw, wspec, 1), (w_s5, wspec, 2)],
        [(proj, gate_spec(0)), (proj, gate_spec(1)), (proj, gate_spec(2))],
        None, post,
        grid=(T // tm, D // tn),
        out_specs=[pl.BlockSpec((tm, tn), lambda m, n: (m, n))],
        out_shape=[jax.ShapeDtypeStruct((T, D), bf16)],
        cache_shapes=[],
    )
    return out


def _out_proj(merged, l, w_out, x, mod):
    T = x.shape[0]
    tm, tn = mod.tm, 512

    def post(prods, e_refs):
        x_ref, m_ref = e_refs
        return (x_ref[...] + m_ref[...] * prods[0],)

    (out,) = _fused_mm(
        [(merged, pl.BlockSpec((tm, D), lambda m, n: (m, 0)))],
        [(w_out, pl.BlockSpec((None, D, tn), lambda m, n: (l, 0, n)), 0)],
        [(x, pl.BlockSpec((tm, tn), lambda m, n: (m, n))), (mod.arr, mod.spec(2, tn, lambda n: n))],
        None, post,
        grid=(T // tm, D // tn),
        out_specs=[pl.BlockSpec((tm, tn), lambda m, n: (m, n))],
        out_shape=[jax.ShapeDtypeStruct((T, D), f32)],
        cache_shapes=[],
    )
    return out


def _ret_consts(L, pos0):
    C = RET_CHUNK if L % RET_CHUNK == 0 else L
    H = RET_H
    log_g = jnp.log1p(-jnp.exp2(-5.0 - jnp.arange(H, dtype=f32)))
    i = jnp.arange(C, dtype=f32)
    diff = i[:, None] - i[None, :]
    causal = diff >= 0
    dmask = jnp.where(causal, jnp.exp(jnp.where(causal, diff, 0.0)[None] * log_g[:, None, None]), 0.0)
    kdec = jnp.exp((C - 1.0 - i)[:, None] * log_g[None, :])
    qdec = jnp.exp((i + 1.0)[:, None] * log_g[None, :])
    g_chunk = jnp.exp(C * log_g)
    half = RET_DK // 2
    inv = ROPE_BASE ** (-jnp.arange(half, dtype=f32) / half)
    pos = pos0 + jnp.arange(L, dtype=f32)
    ang = pos[:, None] * inv[None, :]
    return C, dmask, kdec, qdec, g_chunk, jnp.cos(ang), jnp.sin(ang)


def _rotary(x, cos, sin):
    half = RET_DK // 2
    x1, x2 = x[..., :half], x[..., half:]
    return jnp.concatenate([x1 * cos - x2 * sin, x1 * sin + x2 * cos], axis=-1)


def _retention_seq(proj, B, L):
    C, dmask, kdec, qdec, g_chunk, cos, sin = _ret_consts(L, 0.0)
    H, dk = RET_H, RET_DK
    n = L // C
    kdec_f = jnp.broadcast_to(kdec.T[:, :, None], (H, C, dk))
    qdec_f = jnp.broadcast_to(qdec.T[:, :, None], (H, C, dk))
    gch_f = jnp.broadcast_to(g_chunk[:, None, None], (H, 8, dk))
    proj3 = proj.reshape(B, L, PROJ_W)

    def body(q_ref, k_ref, v_ref, g_ref, cos_ref, sin_ref, dm_ref, kd_ref, qd_ref, gc_ref, y_ref, s_ref, st):
        c = pl.program_id(1)

        @pl.when(c == 0)
        def _():
            st[...] = jnp.zeros_like(st)

        cs, sn = cos_ref[...], sin_ref[...]
        for h in range(H):
            sl = slice(h * dk, (h + 1) * dk)
            q = _rotary(q_ref[:, sl], cs, sn)
            k = _rotary(k_ref[:, sl], cs, sn) * (dk ** -0.5)
            vb = v_ref[:, sl].astype(bf16)
            s0 = st[h]
            scores = _dot_nt(q.astype(bf16), k.astype(bf16)) * dm_ref[h]
            o = _dot(scores.astype(bf16), vb)
            o = o + _dot((q * qd_ref[h]).astype(bf16), s0.astype(bf16))
            kv = _dot_tn((k * kd_ref[h]).astype(bf16), vb)
            st[h] = s0 * gc_ref[h, 0:1, :] + kv
            g = g_ref[:, sl]
            y_ref[:, sl] = (g * jax.nn.sigmoid(g) * _head_norm(o, RET_GN_EPS)).astype(bf16)

        @pl.when(c == n - 1)
        def _():
            s_ref[...] = st[...]

    def seg(off):
        return pl.BlockSpec((None, C, RET_W), lambda b, c: (b, c, off // RET_W))

    const3 = lambda shp: pl.BlockSpec(shp, lambda b, c: (0, 0, 0))
    y, s = pl.pallas_call(
        body,
        grid=(B, n),
        in_specs=[seg(OFF_Q), seg(OFF_K), seg(OFF_V), seg(OFF_G),
                  pl.BlockSpec((C, dk // 2), lambda b, c: (c, 0)),
                  pl.BlockSpec((C, dk // 2), lambda b, c: (c, 0)),
                  const3((H, C, C)), const3((H, C, dk)), const3((H, C, dk)), const3((H, 8, dk))],
        out_specs=[pl.BlockSpec((None, C, RET_W), lambda b, c: (b, c, 0)),
                   pl.BlockSpec((None, H, dk, RET_DV), lambda b, c: (b, 0, 0, 0))],
        out_shape=[jax.ShapeDtypeStruct((B, L, RET_W), bf16),
                   jax.ShapeDtypeStruct((B, H, dk, RET_DV), f32)],
        scratch_shapes=[pltpu.VMEM((H, dk, RET_DV), f32)],
        compiler_params=_cparams(("parallel", "arbitrary")),
    )(proj3, proj3, proj3, proj3, cos, sin, dmask, kdec_f, qdec_f, gch_f)
    return y.reshape(B * L, RET_W), s


STEP_TB = 16


def _layer_grid(l, buf, inner):
    if buf is not None:
        return inner, (lambda fn: (lambda *ix: fn(l, ix, ix)))
    assert l == 0
    last = tuple(n - 1 for n in inner)

    def wrap(fn):
        def index_map(d, *ix):
            parked = tuple(jnp.where(d == l, i, z) for i, z in zip(ix, last))
            return fn(d, ix, parked)
        return index_map
    return (DEPTH,) + inner, wrap


def _retention_step(proj, s_all, l, buf, pos0):
    B = proj.shape[0]
    _, _, _, _, g_chunk, cos, sin = _ret_consts(1, pos0)
    H, dk = RET_H, RET_DK
    gch = jnp.broadcast_to(g_chunk[:, None, None], (H, 8, dk))
    tb = STEP_TB
    grid, wrap = _layer_grid(l, buf, (B // tb, H))

    def body(q_ref, k_ref, v_ref, g_ref, cos_ref, sin_ref, gc_ref, s_ref, *rest):
        y_ref, so_ref = rest[-2:]

        def update():
            cs, sn = cos_ref[...], sin_ref[...]
            q = _rotary(q_ref[...], cs, sn)
            k = _rotary(k_ref[...], cs, sn) * (dk ** -0.5)
            v = v_ref[...]
            s1 = s_ref[...] * gc_ref[0:1, :][None] + k[:, :, None] * v[:, None, :]
            so_ref[...] = s1
            o = jnp.sum(q[:, :, None] * s1, axis=1)
            g = g_ref[...]
            y_ref[...] = (g * jax.nn.sigmoid(g) * _head_norm(o, RET_GN_EPS)).astype(bf16)

        if buf is not None:
            update()
        else:
            pl.when(pl.program_id(0) == l)(update)

            @pl.when(pl.program_id(0) != l)
            def _():
                so_ref[...] = jnp.zeros_like(so_ref)

    def seg(off):
        return pl.BlockSpec((tb, dk), wrap(lambda d, ix, pk: (pk[0], off // dk + pk[1])))

    const = lambda d, ix, pk: (0, 0)
    in_specs = [seg(OFF_Q), seg(OFF_K), seg(OFF_V), seg(OFF_G),
                pl.BlockSpec((1, dk // 2), wrap(const)), pl.BlockSpec((1, dk // 2), wrap(const)),
                pl.BlockSpec((None, 8, dk), wrap(lambda d, ix, pk: (pk[1], 0, 0))),
                pl.BlockSpec((None, tb, None, dk, RET_DV), wrap(lambda d, ix, pk: (l, pk[0], pk[1], 0, 0)))]
    args = [proj, proj, proj, proj, cos, sin, gch, s_all]
    aliases = {}
    if buf is not None:
        in_specs.append(pl.BlockSpec(memory_space=pl.ANY))
        args.append(buf)
        aliases = {len(args) - 1: 1}
    return pl.pallas_call(
        body,
        grid=grid,
        in_specs=in_specs,
        out_specs=[pl.BlockSpec((tb, dk), wrap(lambda d, ix, pk: pk)),
                   pl.BlockSpec((None, tb, None, dk, RET_DV), wrap(lambda d, ix, pk: (d, ix[0], ix[1], 0, 0)))],
        out_shape=[jax.ShapeDtypeStruct((B, RET_W), bf16),
                   jax.ShapeDtypeStruct((DEPTH, B, H, dk, RET_DV), f32)],
        input_output_aliases=aliases,
        compiler_params=_cparams(("arbitrary",) * len(grid)),
    )(*args)


RW_C = 64
RW_Q = 4
RW_NQ = RWKV_H // RW_Q
RW_GROUP = 8
RW_BLK =((RWKV_PROJ + LANE - 1) // LANE) * LANE
_RW_PKEYS = ('mu', 'w0', 'w2', 'a0', 'a2', 'g2', 'k_k', 'k_a', 'r_k', 'ln_w', 'ln_b')


def _rwkv_params(p):
    return dict(
        mu=jnp.pad(p['rwkv_mu'], (0, RW_BLK - RWKV_PROJ)).reshape(1, RW_BLK),
        w0=p['rwkv_w0'].reshape(1, RWKV_W), w2=p['rwkv_w2'],
        a0=p['rwkv_a0'].reshape(1, RWKV_W), a2=p['rwkv_a2'], g2=p['rwkv_g2'],
        k_k=p['rwkv_k_k'].reshape(1, RWKV_W), k_a=p['rwkv_k_a'].reshape(1, RWKV_W),
        r_k=p['rwkv_r_k'].reshape(1, RWKV_W),
        ln_w=p['rwkv_ln_w'].reshape(1, RWKV_W), ln_b=p['rwkv_ln_b'].reshape(1, RWKV_W))


def _split_bf16(x, terms):
    out = []
    for _ in range(terms - 1):
        hi = x.astype(bf16)
        out.append(hi)
        x = x - hi.astype(f32)
    out.append(x.astype(bf16))
    return out


def _head_sum(x):
    QW = RW_Q * RWKV_N
    r = lax.broadcasted_iota(i32, (QW, QW), 0) // RWKV_N
    c = lax.broadcasted_iota(i32, (QW, QW), 1) // RWKV_N
    ones = (r == c).astype(bf16)
    parts = _split_bf16(x, 2)
    outs = []
    for q in range(RW_NQ):
        sl = slice(q * QW, (q + 1) * QW)
        outs.append(_dot(parts[0][:, sl], ones) + _dot(parts[1][:, sl], ones))
    return jnp.concatenate(outs, axis=1)


def _rwkv_mix(rw, prev, pr):
    m = rw + (prev - rw) * pr['mu'][...]
    r = m[:, 0:RWKV_W]
    k = m[:, RWKV_W:2 * RWKV_W]
    v = m[:, 2 * RWKV_W:3 * RWKV_W]
    o = 3 * RWKV_W
    xw = m[:, o:o + DECAY_LORA]
    xa = m[:, o + DECAY_LORA:o + DECAY_LORA + AAA_LORA]
    xg = m[:, o + DECAY_LORA + AAA_LORA:o + DECAY_LORA + AAA_LORA + GATE_LORA]
    w_log = -jax.nn.softplus(-(pr['w0'][...] + _dot(jnp.tanh(xw).astype(bf16), pr['w2'][...].astype(bf16)))) - 0.5
    lw = -jnp.exp(w_log)
    a = jax.nn.sigmoid(pr['a0'][...] + _dot(xa.astype(bf16), pr['a2'][...].astype(bf16)))
    g = _dot(jax.nn.sigmoid(xg).astype(bf16), pr['g2'][...].astype(bf16))
    kk = k * pr['k_k'][...]
    kk = kk / jnp.maximum(jnp.sqrt(_head_sum(kk * kk)), 1e-12)
    kf = k * (1.0 + (a - 1.0) * pr['k_a'][...])
    return r, lw, kf, v, kk, a, g


def _rwkv_out(y, r, kf, v, g, pr):
    yc = y - _head_sum(y) * (1.0 / RWKV_N)
    yn = yc * lax.rsqrt(_head_sum(yc * yc) * (1.0 / RWKV_N) + RWKV_LN_EPS)
    yn = yn * pr['ln_w'][...] + pr['ln_b'][...]
    yn = yn + _head_sum(r * kf * pr['r_k'][...]) * v
    return yn * g


def _rwkv_seq(proj, rp, B, L):
    C = RW_C
    assert L % C == 0 and C == RWKV_N
    n = L // C
    QW = RW_Q * RWKV_N
    BC = B * C
    proj3 = proj.reshape(B, L, PROJ_W)

    def body(rw_ref, *refs):
        pr = dict(zip(_RW_PKEYS, refs[:len(_RW_PKEYS)]))
        y_ref, s_ref, sh_ref, st, carry = refs[len(_RW_PKEYS):]
        c = pl.program_id(0)

        @pl.when(c == 0)
        def _():
            st[...] = jnp.zeros_like(st)
            carry[...] = jnp.zeros_like(carry)

        rw = rw_ref[...].reshape(BC, RW_BLK)
        rolled = pltpu.roll(rw, 1, 0)
        row = lax.broadcasted_iota(i32, (C, RW_BLK), 0)
        prev = jnp.concatenate(
            [jnp.where(row == 0, carry[b, 0:1, :], rolled[b * C:(b + 1) * C]) for b in range(B)], axis=0)
        for b in range(B):
            carry[b, 0:1, :] = rw[(b + 1) * C - 1:(b + 1) * C, :]
        r, lw, kf, v, kk, a, g = _rwkv_mix(rw, prev, pr)

        ti = lax.broadcasted_iota(i32, (BC, BC), 0)
        si = lax.broadcasted_iota(i32, (BC, BC), 1)
        tril = ((si <= ti) & ((si // C) == (ti // C))).astype(bf16)
        lg = sum(_dot(tril, part) for part in _split_bf16(lw, 3))
        lgc = jnp.concatenate(
            [jnp.broadcast_to(lg[(b + 1) * C - 1:(b + 1) * C, :], (C, RWKV_W)) for b in range(B)], axis=0)
        e_neg = jnp.exp(-lg)
        e_rem = jnp.exp(lgc - lg)
        at = kk * jnp.exp(lg - lw)
        ka = kk * a
        bt = ka * e_neg
        kt = kf * e_neg
        rt = r * jnp.exp(lg)
        bh = ka * e_rem
        kh = kf * e_rem
        gcr = jnp.exp(lgc)

        rr = lax.broadcasted_iota(i32, (RW_Q * C, QW), 0)
        ll = lax.broadcasted_iota(i32, (RW_Q * C, QW), 1)
        blockmask = (rr // C) == (ll // RWKV_N)
        tt = lax.broadcasted_iota(i32, (C, QW), 0)
        ss = lax.broadcasted_iota(i32, (C, QW), 1) % C
        strict = ss < tt
        incl = ss <= tt
        eye = (ss == tt).astype(f32)

        def bd(x):
            return jnp.where(blockmask, jnp.concatenate([x] * RW_Q, axis=0), 0.0).astype(bf16)

        ys = [[None] * RW_NQ for _ in range(B)]
        chains = [(b, q) for b in range(B) for q in range(RW_NQ)]
        for g0 in range(0, len(chains), RW_GROUP):
            grp = chains[g0:g0 + RW_GROUP]
            idx = [(slice(b * C, (b + 1) * C), slice(q * QW, (q + 1) * QW)) for b, q in grp]
            each = lambda fn: [fn(i) for i in range(len(grp))]
            vq = each(lambda i: v[idx[i]])
            ar = each(lambda i: jnp.concatenate([at[idx[i]], rt[idx[i]]], axis=0).astype(bf16))
            big = each(lambda i: _dot_nt(ar[i], jnp.concatenate([bd(bt[idx[i]]), bd(kt[idx[i]])], axis=0)))
            s0 = each(lambda i: st[grp[i]])
            asrs = each(lambda i: _dot_nt(ar[i], s0[i].astype(bf16)))
            nmat = each(lambda i: jnp.where(strict, big[i][:C, :QW], 0.0))
            akm = each(lambda i: jnp.where(strict, big[i][:C, QW:], 0.0).astype(bf16))
            rbk = each(lambda i: jnp.concatenate([jnp.where(incl, big[i][C:, :QW], 0.0),
                                                  jnp.where(incl, big[i][C:, QW:], 0.0)], axis=1).astype(bf16))
            tm = each(lambda i: eye - nmat[i])
            pw = each(lambda i: _dot(nmat[i].astype(bf16), bd(nmat[i])))
            lvl = 2
            while lvl < C:
                res = each(lambda i: _dot(jnp.concatenate([tm[i], pw[i]], axis=0).astype(bf16), bd(pw[i])))
                tm = each(lambda i: tm[i] + res[i][:C])
                pw = each(lambda i: res[i][C:])
                lvl *= 2
            vbd = each(lambda i: bd(vq[i]))
            rhs = each(lambda i: -(asrs[i][:C] + _dot(akm[i], vbd[i])))
            u = each(lambda i: _dot(tm[i].astype(bf16), bd(rhs[i])))
            y = each(lambda i: asrs[i][C:] + _dot(rbk[i], jnp.concatenate([bd(u[i]), vbd[i]], axis=0)))
            upd = each(lambda i: _dot_tn(jnp.concatenate([u[i], vq[i]], axis=0).astype(bf16),
                                         jnp.concatenate([bh[idx[i]], kh[idx[i]]], axis=0).astype(bf16)))
            for i, (b, q) in enumerate(grp):
                st[b, q] = s0[i] * gcr[b * C:b * C + 1, idx[i][1]] + jnp.where(blockmask, upd[i], 0.0)
                ys[b][q] = y[i]

        y = jnp.concatenate([jnp.concatenate(yb, axis=1) for yb in ys], axis=0)
        y_ref[...] = _rwkv_out(y, r, kf, v, g, pr).astype(bf16).reshape(B, C, RWKV_W)

        @pl.when(c == n - 1)
        def _():
            for b in range(B):
                for q in range(RW_NQ):
                    for h in range(RW_Q):
                        hs = slice(h * RWKV_N, (h + 1) * RWKV_N)
                        s_ref[b, q * RW_Q + h] = st[b, q, hs, hs]
                sh_ref[b] = rw[(b + 1) * C - 1:(b + 1) * C, 0:RWKV_PROJ]

    pspecs = [pl.BlockSpec(rp[k].shape, lambda c: (0, 0)) for k in _RW_PKEYS]
    y, s, sh = pl.pallas_call(
        body,
        grid=(n,),
        in_specs=[pl.BlockSpec((pl.Element(B), pl.Element(C), pl.Element(RW_BLK)),
                               lambda c: (0, pl.multiple_of(c * C, C), OFF_RW))] + pspecs,
        out_specs=[pl.BlockSpec((B, C, RWKV_W), lambda c: (0, c, 0)),
                   pl.BlockSpec((B, RWKV_H, RWKV_N, RWKV_N), lambda c: (0, 0, 0, 0)),
                   pl.BlockSpec((B, 1, RWKV_PROJ), lambda c: (0, 0, 0))],
        out_shape=[jax.ShapeDtypeStruct((B, L, RWKV_W), bf16),
                   jax.ShapeDtypeStruct((B, RWKV_H, RWKV_N, RWKV_N), f32),
                   jax.ShapeDtypeStruct((B, 1, RWKV_PROJ), f32)],
        scratch_shapes=[pltpu.VMEM((B, RW_NQ, QW, QW), f32), pltpu.VMEM((B, 8, RW_BLK), f32)],
        compiler_params=_cparams(("arbitrary",)),
    )(proj3, *[rp[k] for k in _RW_PKEYS])
    return y.reshape(B * L, RWKV_W), s, sh.reshape(B, RWKV_PROJ)


def _rwkv_step(proj, shift, s_all, l, buf, rp):
    B = proj.shape[0]
    tb = STEP_TB
    shift_p = jnp.pad(shift, ((0, 0), (0, RW_BLK - RWKV_PROJ)))
    grid, wrap = _layer_grid(l, buf, (B // tb,))
    npk = len(_RW_PKEYS)

    def body(rw_ref, sh_ref, s_ref, *refs):
        pr = dict(zip(_RW_PKEYS, refs[:npk]))
        y_ref, so_ref = refs[-2:]

        def update():
            r, lw, kf, v, kk, a, g = _rwkv_mix(rw_ref[...], sh_ref[...], pr)
            hd = lambda x: x.reshape(tb, RWKV_H, RWKV_N)
            s = s_ref[...]
            kk3 = hd(kk)
            sa = jnp.sum(s * (-kk3)[:, :, None, :], axis=-1, keepdims=True)
            s1 = (s * hd(jnp.exp(lw))[:, :, None, :] + sa * (kk3 * hd(a))[:, :, None, :]
                  + hd(v)[:, :, :, None] * hd(kf)[:, :, None, :])
            so_ref[...] = s1
            y = jnp.sum(s1 * hd(r)[:, :, None, :], axis=-1).reshape(tb, RWKV_W)
            y_ref[...] = _rwkv_out(y, r, kf, v, g, pr).astype(bf16)

        if buf is not None:
            update()
        else:
            pl.when(pl.program_id(0) == l)(update)

            @pl.when(pl.program_id(0) != l)
            def _():
                so_ref[...] = jnp.zeros_like(so_ref)

    const = lambda d, ix, pk: (0, 0)
    in_specs = [pl.BlockSpec((pl.Element(tb), pl.Element(RW_BLK)),
                             wrap(lambda d, ix, pk: (pl.multiple_of(pk[0] * tb, tb), OFF_RW))),
                pl.BlockSpec((tb, RW_BLK), wrap(lambda d, ix, pk: (pk[0], 0))),
                pl.BlockSpec((None, tb, RWKV_H, RWKV_N, RWKV_N), wrap(lambda d, ix, pk: (l, pk[0], 0, 0, 0)))]
    in_specs += [pl.BlockSpec(rp[k].shape, wrap(const)) for k in _RW_PKEYS]
    args = [proj, shift_p, s_all] + [rp[k] for k in _RW_PKEYS]
    aliases = {}
    if buf is not None:
        in_specs.append(pl.BlockSpec(memory_space=pl.ANY))
        args.append(buf)
        aliases = {len(args) - 1: 1}
    return pl.pallas_call(
        body,
        grid=grid,
        in_specs=in_specs,
        out_specs=[pl.BlockSpec((tb, RWKV_W), wrap(lambda d, ix, pk: (pk[0], 0))),
                   pl.BlockSpec((None, tb, RWKV_H, RWKV_N, RWKV_N), wrap(lambda d, ix, pk: (d, ix[0], 0, 0, 0)))],
        out_shape=[jax.ShapeDtypeStruct((B, RWKV_W), bf16),
                   jax.ShapeDtypeStruct((DEPTH, B, RWKV_H, RWKV_N, RWKV_N), f32)],
        input_output_aliases=aliases,
        compiler_params=_cparams(("arbitrary",) * len(grid)),
    )(*args)


S5_N = S5_G * S5_P
S5_KC = 256
S5_NKC = S5_W // S5_KC
S5_TILES = S5_N // LANE
S5_GB = S5_KC // S5_GC


def _s5_params(p):
    a_re, a_im = p['s5_a_re'], p['s5_a_im']
    dstep = jnp.exp(p['s5_log_dt'])[:, None]
    mag = jnp.exp(a_re * dstep)
    ab_re = mag * jnp.cos(a_im * dstep)
    ab_im = mag * jnp.sin(a_im * dstep)
    den = a_re * a_re + a_im * a_im
    n_re = ab_re - 1.0
    f_re = (n_re * a_re + ab_im * a_im) / den
    f_im = (ab_im * a_re - n_re * a_im) / den
    b_re, b_im = p['s5_b_re'], p['s5_b_im']
    bb_re = f_re[..., None] * b_re - f_im[..., None] * b_im
    bb_im = f_re[..., None] * b_im + f_im[..., None] * b_re
    eye = jnp.eye(S5_GB, dtype=f32)

    def in_map(bb):
        t = bb.reshape(S5_NKC, S5_GB, S5_P, S5_GC)
        return jnp.einsum('kgpc,gh->kgchp', t, eye).reshape(S5_NKC, S5_KC, S5_GB * S5_P)

    def out_map(cc):
        t = cc.reshape(S5_NKC, S5_GB, S5_GC, S5_P)
        return jnp.einsum('qgcp,gh->qgphc', t, eye).reshape(S5_NKC, S5_GB * S5_P, S5_KC)

    return dict(wb=jnp.concatenate([in_map(bb_re), in_map(bb_im)], axis=-1),
                wc_re=out_map(p['s5_c_re']), wc_im=out_map(p['s5_c_im']),
                ab_re_t=ab_re.reshape(S5_TILES // 8, 8, LANE), ab_im_t=ab_im.reshape(S5_TILES // 8, 8, LANE),
                ab_re=ab_re.reshape(1, S5_N), ab_im=ab_im.reshape(1, S5_N), d=p['s5_d'].reshape(1, S5_W))


def _s5_seq(proj, sp, B, L):
    Lc = min(L, 256)
    n = L // Lc
    pitch = Lc + 4
    lead = OFF_U % LANE
    base = OFF_U - lead
    width = S5_W + LANE
    nt4 = S5_TILES // 8
    tiles_kc = S5_TILES // S5_NKC

    def body(u_ref, wb_ref, wcr_ref, wci_ref, abr_ref, abi_ref, d_ref, y_ref, sr_ref, si_ref, xr, xi, cr, ci):
        c = pl.program_id(1)

        @pl.when(c == 0)
        def _():
            cr[...] = jnp.zeros_like(cr)
            ci[...] = jnp.zeros_like(ci)

        u = u_ref[:, lead:lead + S5_W]
        ub = u.astype(bf16)
        for kc in range(S5_NKC):
            bu = _dot(ub[:, kc * S5_KC:(kc + 1) * S5_KC], wb_ref[kc].astype(bf16))
            for j in range(tiles_kc):
                t = kc * tiles_kc + j
                xr[pl.ds(t * pitch, Lc), :] = bu[:, j * LANE:(j + 1) * LANE]
                xi[pl.ds(t * pitch, Lc), :] = bu[:, (tiles_kc + j) * LANE:(tiles_kc + j + 1) * LANE]

        abr = [abr_ref[g] for g in range(nt4)]
        abi = [abi_ref[g] for g in range(nt4)]

        def step(t, carry):
            out = []
            for g in range(nt4):
                s_r, s_i = carry[2 * g], carry[2 * g + 1]
                idx = pl.ds(g * 8 * pitch + t, 8, stride=pitch)
                n_r = abr[g] * s_r - abi[g] * s_i + xr[idx, :]
                n_i = abr[g] * s_i + abi[g] * s_r + xi[idx, :]
                xr[idx, :] = n_r
                xi[idx, :] = n_i
                out += [n_r, n_i]
            return tuple(out)

        init = []
        for g in range(nt4):
            init += [cr[g], ci[g]]
        fin = lax.fori_loop(0, Lc, step, tuple(init))
        for g in range(nt4):
            cr[g] = fin[2 * g]
            ci[g] = fin[2 * g + 1]

        for q in range(S5_NKC):
            lr = jnp.concatenate([xr[pl.ds((q * tiles_kc + j) * pitch, Lc), :] for j in range(tiles_kc)], axis=1)
            li = jnp.concatenate([xi[pl.ds((q * tiles_kc + j) * pitch, Lc), :] for j in range(tiles_kc)], axis=1)
            y = _dot(lr.astype(bf16), wcr_ref[q].astype(bf16)) - _dot(li.astype(bf16), wci_ref[q].astype(bf16))
            cs = slice(q * S5_KC, (q + 1) * S5_KC)
            y = y + d_ref[:, cs] * u[:, cs]
            y_ref[:, cs] = jax.nn.gelu(y).astype(bf16)

        @pl.when(c == n - 1)
        def _():
            sr_ref[...] = cr[...]
            si_ref[...] = ci[...]

    full = lambda shp: pl.BlockSpec(shp, lambda b, c: (0,) * len(shp))
    y, sr, si = pl.pallas_call(
        body,
        grid=(B, n),
        in_specs=[pl.BlockSpec((pl.Element(Lc), pl.Element(width)),
                               lambda b, c: (pl.multiple_of((b * n + c) * Lc, Lc), base)),
                  full(sp['wb'].shape), full(sp['wc_re'].shape), full(sp['wc_im'].shape),
                  full(sp['ab_re_t'].shape), full(sp['ab_im_t'].shape), full((1, S5_W))],
        out_specs=[pl.BlockSpec((Lc, S5_W), lambda b, c: (b * n + c, 0)),
                   pl.BlockSpec((None, nt4, 8, LANE), lambda b, c: (b, 0, 0, 0)),
                   pl.BlockSpec((None, nt4, 8, LANE), lambda b, c: (b, 0, 0, 0))],
        out_shape=[jax.ShapeDtypeStruct((B * L, S5_W), bf16),
                   jax.ShapeDtypeStruct((B, nt4, 8, LANE), f32),
                   jax.ShapeDtypeStruct((B, nt4, 8, LANE), f32)],
        scratch_shapes=[pltpu.VMEM((S5_TILES * pitch, LANE), f32), pltpu.VMEM((S5_TILES * pitch, LANE), f32),
                        pltpu.VMEM((nt4, 8, LANE), f32), pltpu.VMEM((nt4, 8, LANE), f32)],
        compiler_params=_cparams(("parallel", "arbitrary")),
    )(proj, sp['wb'], sp['wc_re'], sp['wc_im'], sp['ab_re_t'], sp['ab_im_t'], sp['d'])
    return y, sr.reshape(B, S5_G, S5_P), si.reshape(B, S5_G, S5_P)


def _s5_step(proj, x_re, x_im, sp):
    B = proj.shape[0]
    lead = OFF_U % LANE
    base = OFF_U - lead
    width = S5_W + LANE
    kw = S5_N // S5_NKC

    def body(u_ref, xr_ref, xi_ref, wb_ref, wcr_ref, wci_ref, abr_ref, abi_ref, d_ref, y_ref, sr_ref, si_ref):
        u = u_ref[:, lead:lead + S5_W]
        ub = u.astype(bf16)
        abr, abi = abr_ref[...], abi_ref[...]
        xr, xi = xr_ref[...], xi_ref[...]
        for kc in range(S5_NKC):
            bu = _dot(ub[:, kc * S5_KC:(kc + 1) * S5_KC], wb_ref[kc].astype(bf16))
            sl = slice(kc * kw, (kc + 1) * kw)
            n_r = abr[:, sl] * xr[:, sl] - abi[:, sl] * xi[:, sl] + bu[:, :kw]
            n_i = abr[:, sl] * xi[:, sl] + abi[:, sl] * xr[:, sl] + bu[:, kw:]
            sr_ref[:, sl] = n_r
            si_ref[:, sl] = n_i
            y = _dot(n_r.astype(bf16), wcr_ref[kc].astype(bf16)) - _dot(n_i.astype(bf16), wci_ref[kc].astype(bf16))
            cs = slice(kc * S5_KC, (kc + 1) * S5_KC)
            y = y + d_ref[:, cs] * u[:, cs]
            y_ref[:, cs] = jax.nn.gelu(y).astype(bf16)

    full = lambda shp: pl.BlockSpec(shp, lambda i: (0,) * len(shp))
    y, sr, si = pl.pallas_call(
        body,
        grid=(1,),
        in_specs=[pl.BlockSpec((pl.Element(B), pl.Element(width)), lambda i: (0, base)),
                  full((B, S5_N)), full((B, S5_N)),
                  full(sp['wb'].shape), full(sp['wc_re'].shape), full(sp['wc_im'].shape),
                  full((1, S5_N)), full((1, S5_N)), full((1, S5_W))],
        out_specs=[full((B, S5_W)), full((B, S5_N)), full((B, S5_N))],
        out_shape=[jax.ShapeDtypeStruct((B, S5_W), bf16),
                   jax.ShapeDtypeStruct((B, S5_N), f32), jax.ShapeDtypeStruct((B, S5_N), f32)],
        compiler_params=_cparams(("arbitrary",)),
    )(proj, x_re.reshape(B, S5_N), x_im.reshape(B, S5_N), sp['wb'], sp['wc_re'], sp['wc_im'],
      sp['ab_re'], sp['ab_im'], sp['d'])
    return y, sr.reshape(B, S5_G, S5_P), si.reshape(B, S5_G, S5_P)


MOE_RT = 256
ROUTE_W = LANE


def _router(x, g, mod, w_r, b_r):
    T = x.shape[0]
    tm = min(mod.tm, 512)

    def body(x_ref, g_ref, sh_ref, sc_ref, w_ref, b_ref, h_ref, e_ref, p_ref):
        h2 = _rms(x_ref[...], g_ref[...]) * (1.0 + sc_ref[...]) + sh_ref[...]
        h_ref[...] = h2
        logits = jnp.dot(h2, w_ref[...], precision=lax.Precision.HIGHEST, preferred_element_type=f32) + b_ref[...]
        lane = lax.broadcasted_iota(i32, (tm, ROUTE_W), 1)
        ninf = jnp.float32(-jnp.inf)
        gl = jnp.where(lane < N_GROUPS, logits, ninf)
        gm = jnp.max(gl, axis=-1, keepdims=True)
        g_p = 1.0 / jnp.sum(jnp.exp(gl - gm), axis=-1, keepdims=True)
        g_idx = jnp.min(jnp.where(gl == gm, lane, ROUTE_W), axis=-1, keepdims=True)
        valid = (lane >= N_GROUPS) & (lane < N_GROUPS + N_EXPERTS) & (((lane - N_GROUPS) // EPG) == g_idx)
        el = jnp.where(valid, logits, ninf)
        ee = jnp.exp(el - jnp.max(el, axis=-1, keepdims=True))
        prob = jnp.where(valid, ee / jnp.sum(ee, axis=-1, keepdims=True), -1.0)
        p1 = jnp.max(prob, axis=-1, keepdims=True)
        i1 = jnp.min(jnp.where(prob == p1, lane, ROUTE_W), axis=-1, keepdims=True)
        prob2 = jnp.where(lane == i1, -1.0, prob)
        p2 = jnp.max(prob2, axis=-1, keepdims=True)
        i2 = jnp.min(jnp.where(prob2 == p2, lane, ROUTE_W), axis=-1, keepdims=True)
        den = p1 + p2
        e_ref[...] = jnp.where(lane == 0, i1 - N_GROUPS, jnp.where(lane == 1, i2 - N_GROUPS, 0))
        p_ref[...] = jnp.where(lane == 0, g_p * p1 / den, jnp.where(lane == 1, g_p * p2 / den, 0.0))

    return pl.pallas_call(
        body,
        grid=(T // tm,),
        in_specs=[pl.BlockSpec((tm, D), lambda m: (m, 0)),
                  pl.BlockSpec((1, D), lambda m: (0, 0)),
                  mod.row_spec(3, tm), mod.row_spec(4, tm),
                  pl.BlockSpec((D, ROUTE_W), lambda m: (0, 0)),
                  pl.BlockSpec((1, ROUTE_W), lambda m: (0, 0))],
        out_specs=[pl.BlockSpec((tm, D), lambda m: (m, 0)),
                   pl.BlockSpec((tm, ROUTE_W), lambda m: (m, 0)),
                   pl.BlockSpec((tm, ROUTE_W), lambda m: (m, 0))],
        out_shape=[jax.ShapeDtypeStruct((T, D), f32), jax.ShapeDtypeStruct((T, ROUTE_W), i32),
                   jax.ShapeDtypeStruct((T, ROUTE_W), f32)],
        compiler_params=_cparams(("parallel",)),
    )(x, g.reshape(1, D), mod.arr, mod.arr, w_r, b_r)


def _experts(h2, order, start, counts, l, w1, w3, w2):
    T = h2.shape[0]
    assert TOP_K == 2
    A = T * TOP_K
    RT = MOE_RT

    def body(order_ref, start_ref, cnt_ref, h_hbm, w1_ref, w3_ref, w2_ref, o_hbm,
             xbuf, obuf, w1b, w3b, w2b, gsem, ssem):
        e = pl.program_id(0)
        cnt = cnt_ref[e]
        s0 = start_ref[e]
        ntile = (cnt + RT - 1) // RT

        @pl.when(e == 0)
        def _():
            xbuf[...] = jnp.zeros_like(xbuf)

        w1b[...] = w1_ref[...].astype(bf16)
        w3b[...] = w3_ref[...].astype(bf16)
        w2b[...] = w2_ref[...].astype(bf16)

        def rows_in(i):
            return jnp.minimum(RT, cnt - i * RT)

        def gather_copy(slot, r, tok):
            return pltpu.make_async_copy(h_hbm.at[pl.ds(tok, 1), :], xbuf.at[slot, pl.ds(r, 1), :], gsem.at[slot])

        def scatter_copy(slot, r, a):
            col = pl.multiple_of((a & (TOP_K - 1)) * D, D)
            return pltpu.make_async_copy(obuf.at[slot, pl.ds(r, 1), :],
                                         o_hbm.at[pl.ds(lax.shift_right_logical(a, 1), 1), pl.ds(col, D)],
                                         ssem.at[slot])

        def start_gather(i, slot):
            def one(r, _):
                a = order_ref[s0 + i * RT + r]
                gather_copy(slot, r, lax.shift_right_logical(a, 1)).start()
                return 0
            lax.fori_loop(0, rows_in(i), one, 0)

        def wait_gather(i, slot):
            def one(r, _):
                gather_copy(slot, 0, 0).wait()
                return 0
            lax.fori_loop(0, rows_in(i), one, 0)

        def start_scatter(i, slot):
            def one(r, _):
                scatter_copy(slot, r, order_ref[s0 + i * RT + r]).start()
                return 0
            lax.fori_loop(0, rows_in(i), one, 0)

        def wait_scatter(i, slot):
            def one(r, _):
                scatter_copy(slot, 0, 0).wait()
                return 0
            lax.fori_loop(0, rows_in(i), one, 0)

        @pl.when(ntile > 0)
        def _():
            start_gather(0, 0)

        def tile(i, _):
            slot = i % 2

            @pl.when(i + 1 < ntile)
            def _():
                start_gather(i + 1, 1 - slot)

            wait_gather(i, slot)
            xb = xbuf[slot].astype(bf16)
            hmid = _dot(xb, w1b[...])
            hmid = (hmid * jax.nn.sigmoid(hmid)) * _dot(xb, w3b[...])
            out = _dot(hmid.astype(bf16), w2b[...])

            @pl.when(i >= 2)
            def _():
                wait_scatter(i - 2, slot)

            obuf[slot] = out
            start_scatter(i, slot)
            return 0

        lax.fori_loop(0, ntile, tile, 0)

        @pl.when(ntile >= 2)
        def _():
            wait_scatter(ntile - 2, ntile % 2)

        @pl.when(ntile >= 1)
        def _():
            wait_scatter(ntile - 1, (ntile - 1) % 2)

    grid_spec = pltpu.PrefetchScalarGridSpec(
        num_scalar_prefetch=3,
        grid=(N_EXPERTS,),
        in_specs=[pl.BlockSpec(memory_space=pl.ANY),
                  pl.BlockSpec((None, None, D, D_EXPERT), lambda e, *_: (l, e, 0, 0)),
                  pl.BlockSpec((None, None, D, D_EXPERT), lambda e, *_: (l, e, 0, 0)),
                  pl.BlockSpec((None, None, D_EXPERT, D), lambda e, *_: (l, e, 0, 0))],
        out_specs=pl.BlockSpec(memory_space=pl.ANY),
        scratch_shapes=[pltpu.VMEM((2, RT, D), f32), pltpu.VMEM((2, RT, D), f32),
                        pltpu.VMEM((D, D_EXPERT), bf16), pltpu.VMEM((D, D_EXPERT), bf16),
                        pltpu.VMEM((D_EXPERT, D), bf16),
                        pltpu.SemaphoreType.DMA((2,)), pltpu.SemaphoreType.DMA((2,))],
    )
    return pl.pallas_call(
        body,
        grid_spec=grid_spec,
        out_shape=jax.ShapeDtypeStruct((T, TOP_K * D), f32),
        compiler_params=_cparams(("arbitrary",)),
    )(order, start, counts, h2, w1, w3, w2)


def _moe_combine(x, o2, wgt, mod, norm_final):
    T = x.shape[0]
    tm = min(mod.tm, 512)

    def body(x_ref, o_ref, p_ref, m_ref, *rest):
        p = p_ref[...]
        y = p[:, 0:1] * o_ref[:, 0:D]
        for k in range(1, TOP_K):
            y = y + p[:, k:k + 1] * o_ref[:, k * D:(k + 1) * D]
        xo = x_ref[...] + m_ref[...] * y
        if norm_final is None:
            rest[0][...] = xo
        else:
            rest[1][...] = _rms(xo, rest[0][...])

    ins = [x, o2, wgt, mod.arr]
    specs = [pl.BlockSpec((tm, D), lambda m: (m, 0)), pl.BlockSpec((tm, TOP_K * D), lambda m: (m, 0)),
             pl.BlockSpec((tm, ROUTE_W), lambda m: (m, 0)), mod.row_spec(5, tm)]
    if norm_final is not None:
        ins.append(norm_final.reshape(1, D))
        specs.append(pl.BlockSpec((1, D), lambda m: (0, 0)))
    return pl.pallas_call(
        body,
        grid=(T // tm,),
        in_specs=specs,
        out_specs=pl.BlockSpec((tm, D), lambda m: (m, 0)),
        out_shape=jax.ShapeDtypeStruct((T, D), f32),
        compiler_params=_cparams(("parallel",)),
    )(*ins)


def _moe(x, mod, p, l, big, norm_final):
    pad = ROUTE_W - N_GROUPS - N_EXPERTS
    w_r = jnp.pad(jnp.concatenate([p['moe_w_group'], p['moe_w_router']], axis=1), ((0, 0), (0, pad)))
    b_r = jnp.pad(jnp.concatenate([p['moe_b_group'], p['moe_b_router']]), (0, pad)).reshape(1, ROUTE_W)
    h2, eid, wgt = _router(x, p['norm_ffn'], mod, w_r, b_r)
    flat_e = eid[:, :TOP_K].reshape(-1)
    order = jnp.argsort(flat_e).astype(i32)
    counts = jnp.bincount(flat_e, length=N_EXPERTS).astype(i32)
    start = (jnp.cumsum(counts) - counts).astype(i32)
    o2 = _experts(h2, order, start, counts, l, big['moe_w1'], big['moe_w3'], big['moe_w2'])
    return _moe_combine(x, o2, wgt, mod, norm_final)


def _trunk(x, mods, states, pos0, layers, norm_final):
    B, L, _ = x.shape
    T = B * L
    tm = _row_tile(T)
    x = x.reshape(T, D)
    outs = ([], [], [], [], [])
    big = {k: layers[k] for k in _BIG}
    ret_all = rw_all = None
    for l in range(DEPTH):
        p = {name: arr[l] for name, arr in layers.items() if name not in _BIG}
        mod = _Mod(mods[l], L, tm)
        rp, sp = _rwkv_params(p), _s5_params(p)
        proj = _in_proj(x, layers['norm_mix'], l, mod, big['w_in'])
        if states is None:
            y_ret, s_ret = _retention_seq(proj, B, L)
            y_rw, s_rw, shift = _rwkv_seq(proj, rp, B, L)
            y_s5, s_re, s_im = _s5_seq(proj, sp, B, L)
        else:
            st_ret, st_rw, st_shift, st_re, st_im = states
            y_ret, ret_all = _retention_step(proj, st_ret, l, ret_all, pos0)
            y_rw, rw_all = _rwkv_step(proj, st_shift[l], st_rw, l, rw_all, rp)
            s_ret = s_rw = None
            shift = proj[:, OFF_RW:OFF_RW + RWKV_PROJ]
            y_s5, s_re, s_im = _s5_step(proj, st_re[l], st_im[l], sp)
        z = _glu_proj(y_s5, l, big['s5_w_glu'])
        merged = _merge_proj(y_ret, y_rw, z, l, big['ret_w_o'], big['rwkv_w_o'], big['s5_w_o'], proj)
        x = _out_proj(merged, l, big['w_out'], x, mod)
        x = _moe(x, mod, p, l, big, norm_final if l == DEPTH - 1 else None)
        for lst, val in zip(outs, (s_ret, s_rw, shift, s_re, s_im)):
            lst.append(val)
    stacked = [jnp.stack(o) if o[0] is not None else None for o in outs]
    if states is not None:
        stacked[0], stacked[1] = ret_all, rw_all
    return x.reshape(B, L, D), stacked


_BIG = ('w_in', 'ret_w_o', 'rwkv_w_o', 's5_w_glu', 's5_w_o', 'w_out', 'moe_w1', 'moe_w3', 'moe_w2')


def kernel(x_prompt, x_sample, state_ret, state_rwkv, state_shift, state_s5_re, state_s5_im,
           c_prompt, c_sample, norm_mix, norm_ffn, w_ada, b_ada, w_in, ret_w_o, rwkv_mu, rwkv_w0,
           rwkv_w2, rwkv_a0, rwkv_a2, rwkv_g2, rwkv_k_k, rwkv_k_a, rwkv_r_k, rwkv_ln_w, rwkv_ln_b,
           rwkv_w_o, s5_a_re, s5_a_im, s5_b_re, s5_b_im, s5_c_re, s5_c_im, s5_d, s5_log_dt, s5_w_glu,
           s5_w_o, w_out, moe_w_group, moe_b_group, moe_w_router, moe_b_router, moe_w1, moe_w3, moe_w2,
           norm_final):
    layers = {
        'norm_mix': norm_mix, 'norm_ffn': norm_ffn, 'w_in': w_in,
        'ret_w_o': ret_w_o, 'rwkv_mu': rwkv_mu, 'rwkv_w0': rwkv_w0, 'rwkv_w2': rwkv_w2,
        'rwkv_a0': rwkv_a0, 'rwkv_a2': rwkv_a2, 'rwkv_g2': rwkv_g2, 'rwkv_k_k': rwkv_k_k,
        'rwkv_k_a': rwkv_k_a, 'rwkv_r_k': rwkv_r_k, 'rwkv_ln_w': rwkv_ln_w, 'rwkv_ln_b': rwkv_ln_b,
        'rwkv_w_o': rwkv_w_o, 's5_a_re': s5_a_re, 's5_a_im': s5_a_im, 's5_b_re': s5_b_re,
        's5_b_im': s5_b_im, 's5_c_re': s5_c_re, 's5_c_im': s5_c_im, 's5_d': s5_d,
        's5_log_dt': s5_log_dt, 's5_w_glu': s5_w_glu, 's5_w_o': s5_w_o, 'w_out': w_out,
        'moe_w_group': moe_w_group, 'moe_b_group': moe_b_group, 'moe_w_router': moe_w_router,
        'moe_b_router': moe_b_router, 'moe_w1': moe_w1, 'moe_w3': moe_w3, 'moe_w2': moe_w2,
    }
    Bp, Bs = x_prompt.shape[0], x_sample.shape[0]
    s_off = -(-Bp // 8) * 8
    c_all = jnp.concatenate([c_prompt, jnp.zeros((s_off - Bp, D), f32), c_sample], axis=0)
    mod_all = _adaln(c_all, w_ada, b_ada)
    y_prompt, (ret_p, rwkv_p, shift_p, s5re_p, s5im_p) = _trunk(
        x_prompt, mod_all[:, :Bp], None, 0.0, layers, norm_final)
    y_sample, (ret_s, rwkv_s, shift_s, s5re_s, s5im_s) = _trunk(
        x_sample, mod_all[:, s_off:s_off + Bs], (state_ret, state_rwkv, state_shift, state_s5_re, state_s5_im),
        float(PAST_LEN), layers, norm_final)
    return (y_prompt, y_sample, ret_p, ret_s, rwkv_p, rwkv_s, shift_p, shift_s, s5re_p, s5re_s, s5im_p, s5im_s)
```

```python
import jax
import jax.numpy as jnp
from jax import lax
from jax.experimental import pallas as pl
from jax.experimental.pallas import tpu as pltpu

f32 = jnp.float32
bf16 = jnp.bfloat16
i32 = jnp.int32

D = 2048
DEPTH = 2
PAST_LEN = 16384
RET_W, RET_H, RET_DK, RET_DV, RET_CHUNK = 1024, 4, 256, 256, 128
RET_GN_EPS = 1e-6
ROPE_BASE = 10000.0
RWKV_W, RWKV_N, RWKV_H = 1024, 64, 16
DECAY_LORA, AAA_LORA, GATE_LORA = 64, 64, 160
RWKV_PROJ = 3 * RWKV_W + DECAY_LORA + AAA_LORA + GATE_LORA
RWKV_LN_EPS = 64e-5
S5_W, S5_GC, S5_G, S5_P = 1024, 16, 64, 64
N_MOD = 6
RMS_EPS = 1e-6
N_GROUPS, EPG, N_EXPERTS, TOP_K, D_EXPERT = 4, 8, 32, 2, 512
IN_W = 4 * RET_W + RWKV_PROJ + S5_W + 3 * D
OFF_Q, OFF_K, OFF_V, OFF_G, OFF_RW = 0, 1024, 2048, 3072, 4096
OFF_U = OFF_RW + RWKV_PROJ
OFF_GATE = OFF_U + S5_W
LANE = 128
PROJ_W = ((IN_W + LANE - 1) // LANE) * LANE
VMEM_LIMIT = 56 * 1024 * 1024


def _cparams(sem):
    return pltpu.CompilerParams(dimension_semantics=sem, vmem_limit_bytes=VMEM_LIMIT)


def _dot(a, b):
    return jnp.dot(a, b, preferred_element_type=f32)


def _dot_nt(a, b):
    return lax.dot_general(a, b, (((1,), (1,)), ((), ())), preferred_element_type=f32)


def _dot_tn(a, b):
    return lax.dot_general(a, b, (((0,), (0,)), ((), ())), preferred_element_type=f32)


def _rms(x, g):
    return x * lax.rsqrt(jnp.mean(x * x, axis=-1, keepdims=True) + RMS_EPS) * g


def _head_norm(y, eps):
    mu = jnp.mean(y, axis=-1, keepdims=True)
    yc = y - mu
    return yc * lax.rsqrt(jnp.mean(yc * yc, axis=-1, keepdims=True) + eps)


def _row_tile(T):
    return 1024 if T >= 1024 else T


class _Mod:
    def __init__(self, mod, L, tm):
        self.L, self.tm = L, tm
        self.per_token = L == 1
        self.arr = mod if self.per_token else mod.reshape(mod.shape[0], 1, N_MOD * D)

    def spec(self, j, tn, col_of):
        nb = D // tn
        if self.per_token:
            return pl.BlockSpec((self.tm, tn), lambda m, n: (m, j * nb + col_of(n)))
        L, tm = self.L, self.tm
        return pl.BlockSpec((None, 1, tn), lambda m, n: ((m * tm) // L, 0, j * nb + col_of(n)))

    def row_spec(self, j, tm):
        if self.per_token:
            return pl.BlockSpec((tm, D), lambda m: (m, j))
        L = self.L
        return pl.BlockSpec((None, 1, D), lambda m: ((m * tm) // L, 0, j))


def _fused_mm(x_ops, w_ops, e_ops, pre, post, *, grid, out_specs, out_shape, cache_shapes):
    nx, nw, ne = len(x_ops), len(w_ops), len(e_ops)
    n_out = len(out_shape)

    def body(*refs):
        x_refs = refs[:nx]
        w_refs = refs[nx:nx + nw]
        e_refs = refs[nx + nw:nx + nw + ne]
        o_refs = refs[nx + nw + ne:nx + nw + ne + n_out]
        caches = refs[nx + nw + ne + n_out:]
        if cache_shapes:
            @pl.when(pl.program_id(1) == 0)
            def _():
                for i in range(nx):
                    caches[i][...] = pre(i, x_refs[i], e_refs).astype(bf16)
            lhs = [c[...] for c in caches]
        else:
            lhs = [x[...] for x in x_refs]
        prods = [(_dot_nt if len(w_ops[j]) > 3 and w_ops[j][3] else _dot)(
            lhs[w_ops[j][2]], w_refs[j][...].astype(bf16)) for j in range(nw)]
        for o_ref, o in zip(o_refs, post(prods, e_refs)):
            o_ref[...] = o.astype(o_ref.dtype)

    return pl.pallas_call(
        body,
        grid=grid,
        in_specs=[s for _, s in x_ops] + [w[1] for w in w_ops] + [s for _, s in e_ops],
        out_specs=out_specs,
        out_shape=out_shape,
        scratch_shapes=[pltpu.VMEM(s, bf16) for s in cache_shapes],
        compiler_params=_cparams(("parallel", "arbitrary")),
    )(*[a for a, _ in x_ops], *[w[0] for w in w_ops], *[a for a, _ in e_ops])


def _adaln(c_all, w_ada, b_ada):
    R = c_all.shape[0]
    tn = 1024

    def pre(i, x_ref, e_refs):
        c = x_ref[...]
        return c * jax.nn.sigmoid(c)

    def post(prods, e_refs):
        return (prods[0] + e_refs[0][...],)

    (out,) = _fused_mm(
        [(c_all, pl.BlockSpec((R, D), lambda l, n: (0, 0)))],
        [(w_ada, pl.BlockSpec((None, D, tn), lambda l, n: (l, 0, n)), 0)],
        [(b_ada.reshape(DEPTH, 1, N_MOD * D), pl.BlockSpec((None, 1, tn), lambda l, n: (l, 0, n)))],
        pre, post,
        grid=(DEPTH, N_MOD * D // tn),
        out_specs=[pl.BlockSpec((None, R, tn), lambda l, n: (l, 0, n))],
        out_shape=[jax.ShapeDtypeStruct((DEPTH, R, N_MOD * D), f32)],
        cache_shapes=[(R, D)],
    )
    return out


def _in_proj(x, g, l, mod, w_in_t):
    T = x.shape[0]
    tm, tn = mod.tm, 512

    def pre(i, x_ref, e_refs):
        g_ref, shift_ref, scale_ref = e_refs
        return _rms(x_ref[...], g_ref[...]) * (1.0 + scale_ref[...]) + shift_ref[...]

    def post(prods, e_refs):
        col = pl.program_id(1) * tn + lax.broadcasted_iota(i32, (tm, tn), 1)
        return (jnp.where(col < IN_W, prods[0], 0.0),)

    (out,) = _fused_mm(
        [(x, pl.BlockSpec((tm, D), lambda m, n: (m, 0)))],
        [(w_in_t, pl.BlockSpec((None, tn, D), lambda m, n: (l, n, 0)), 0, True)],
        [(g.reshape(DEPTH, 1, D), pl.BlockSpec((None, 1, D), lambda m, n: (l, 0, 0))),
         (mod.arr, mod.spec(0, D, lambda n: 0)),
         (mod.arr, mod.spec(1, D, lambda n: 0))],
        pre, post,
        grid=(T // tm, pl.cdiv(PROJ_W, tn)),
        out_specs=[pl.BlockSpec((tm, tn), lambda m, n: (m, n))],
        out_shape=[jax.ShapeDtypeStruct((T, PROJ_W), f32)],
        cache_shapes=[(tm, D)],
    )
    return out


def _glu_proj(yg, l, w_glu):
    T = yg.shape[0]
    tm, tn = _row_tile(T), 512
    nb = S5_W // tn

    def post(prods, e_refs):
        return (prods[0] * jax.nn.sigmoid(prods[1]),)

    (out,) = _fused_mm(
        [(yg, pl.BlockSpec((tm, S5_W), lambda m, n: (m, 0)))],
        [(w_glu, pl.BlockSpec((None, S5_W, tn), lambda m, n: (l, 0, n)), 0),
         (w_glu, pl.BlockSpec((None, S5_W, tn), lambda m, n: (l, 0, nb + n)), 0)],
        [], None, post,
        grid=(T // tm, nb),
        out_specs=[pl.BlockSpec((tm, tn), lambda m, n: (m, n))],
        out_shape=[jax.ShapeDtypeStruct((T, S5_W), bf16)],
        cache_shapes=[],
    )
    return out


def _merge_proj(y_ret, y_rw, y_s5, l, w_ret, w_rw, w_s5, proj):
    T = y_ret.shape[0]
    tm, tn = _row_tile(T), 512
    lead = OFF_GATE % LANE
    base = OFF_GATE - lead

    def gate_spec(i):
        return pl.BlockSpec((pl.Element(tm), pl.Element(tn + LANE)),
                            lambda m, n: (pl.multiple_of(m * tm, tm), pl.multiple_of(base + i * D + n * tn, LANE)))

    def post(prods, e_refs):
        acc = None
        for p, e in zip(prods, e_refs):
            t = jax.nn.sigmoid(e[:, lead:lead + tn]) * p
            acc = t if acc is None else acc + t
        return (acc,)

    xspec = pl.BlockSpec((tm, RET_W), lambda m, n: (m, 0))
    wspec = pl.BlockSpec((None, RET_W, tn), lambda m, n: (l, 0, n))
    (out,) = _fused_mm(
        [(y_ret, xspec), (y_rw, xspec), (y_s5, xspec)],
        [(w_ret, wspec, 0), (w_rw, wspec, 1), (w_s5, wspec, 2)],
        [(proj, gate_spec(0)), (proj, gate_spec(1)), (proj, gate_spec(2))],
        None, post,
        grid=(T // tm, D // tn),
        out_specs=[pl.BlockSpec((tm, tn), lambda m, n: (m, n))],
        out_shape=[jax.ShapeDtypeStruct((T, D), bf16)],
        cache_shapes=[],
    )
    return out


def _out_proj(merged, l, w_out, x, mod):
    T = x.shape[0]
    tm, tn = mod.tm, 512

    def post(prods, e_refs):
        x_ref, m_ref = e_refs
        return (x_ref[...] + m_ref[...] * prods[0],)

    (out,) = _fused_mm(
        [(merged, pl.BlockSpec((tm, D), lambda m, n: (m, 0)))],
        [(w_out, pl.BlockSpec((None, D, tn), lambda m, n: (l, 0, n)), 0)],
        [(x, pl.BlockSpec((tm, tn), lambda m, n: (m, n))), (mod.arr, mod.spec(2, tn, lambda n: n))],
        None, post,
        grid=(T // tm, D // tn),
        out_specs=[pl.BlockSpec((tm, tn), lambda m, n: (m, n))],
        out_shape=[jax.ShapeDtypeStruct((T, D), f32)],
        cache_shapes=[],
    )
    return out


def _ret_consts(L, pos0):
    C = RET_CHUNK if L % RET_CHUNK == 0 else L
    H = RET_H
    log_g = jnp.log1p(-jnp.exp2(-5.0 - jnp.arange(H, dtype=f32)))
    i = jnp.arange(C, dtype=f32)
    diff = i[:, None] - i[None, :]
    causal = diff >= 0
    dmask = jnp.where(causal, jnp.exp(jnp.where(causal, diff, 0.0)[None] * log_g[:, None, None]), 0.0)
    kdec = jnp.exp((C - 1.0 - i)[:, None] * log_g[None, :])
    qdec = jnp.exp((i + 1.0)[:, None] * log_g[None, :])
    g_chunk = jnp.exp(C * log_g)
    half = RET_DK // 2
    inv = ROPE_BASE ** (-jnp.arange(half, dtype=f32) / half)
    pos = pos0 + jnp.arange(L, dtype=f32)
    ang = pos[:, None] * inv[None, :]
    return C, dmask, kdec, qdec, g_chunk, jnp.cos(ang), jnp.sin(ang)


def _rotary(x, cos, sin):
    half = RET_DK // 2
    x1, x2 = x[..., :half], x[..., half:]
    return jnp.concatenate([x1 * cos - x2 * sin, x1 * sin + x2 * cos], axis=-1)


def _retention_seq(proj, B, L):
    C, dmask, kdec, qdec, g_chunk, cos, sin = _ret_consts(L, 0.0)
    H, dk = RET_H, RET_DK
    n = L // C
    kdec_f = jnp.broadcast_to(kdec.T[:, :, None], (H, C, dk))
    qdec_f = jnp.broadcast_to(qdec.T[:, :, None], (H, C, dk))
    gch_f = jnp.broadcast_to(g_chunk[:, None, None], (H, 8, dk))
    proj3 = proj.reshape(B, L, PROJ_W)

    def body(q_ref, k_ref, v_ref, g_ref, cos_ref, sin_ref, dm_ref, kd_ref, qd_ref, gc_ref, y_ref, s_ref, st):
        c = pl.program_id(1)

        @pl.when(c == 0)
        def _():
            st[...] = jnp.zeros_like(st)

        cs, sn = cos_ref[...], sin_ref[...]
        for h in range(H):
            sl = slice(h * dk, (h + 1) * dk)
            q = _rotary(q_ref[:, sl], cs, sn)
            k = _rotary(k_ref[:, sl], cs, sn) * (dk ** -0.5)
            vb = v_ref[:, sl].astype(bf16)
            s0 = st[h]
            scores = _dot_nt(q.astype(bf16), k.astype(bf16)) * dm_ref[h]
            o = _dot(scores.astype(bf16), vb)
            o = o + _dot((q * qd_ref[h]).astype(bf16), s0.astype(bf16))
            kv = _dot_tn((k * kd_ref[h]).astype(bf16), vb)
            st[h] = s0 * gc_ref[h, 0:1, :] + kv
            g = g_ref[:, sl]
            y_ref[:, sl] = (g * jax.nn.sigmoid(g) * _head_norm(o, RET_GN_EPS)).astype(bf16)

        @pl.when(c == n - 1)
        def _():
            s_ref[...] = st[...]

    def seg(off):
        return pl.BlockSpec((None, C, RET_W), lambda b, c: (b, c, off // RET_W))

    const3 = lambda shp: pl.BlockSpec(shp, lambda b, c: (0, 0, 0))
    y, s = pl.pallas_call(
        body,
        grid=(B, n),
        in_specs=[seg(OFF_Q), seg(OFF_K), seg(OFF_V), seg(OFF_G),
                  pl.BlockSpec((C, dk // 2), lambda b, c: (c, 0)),
                  pl.BlockSpec((C, dk // 2), lambda b, c: (c, 0)),
                  const3((H, C, C)), const3((H, C, dk)), const3((H, C, dk)), const3((H, 8, dk))],
        out_specs=[pl.BlockSpec((None, C, RET_W), lambda b, c: (b, c, 0)),
                   pl.BlockSpec((None, H, dk, RET_DV), lambda b, c: (b, 0, 0, 0))],
        out_shape=[jax.ShapeDtypeStruct((B, L, RET_W), bf16),
                   jax.ShapeDtypeStruct((B, H, dk, RET_DV), f32)],
        scratch_shapes=[pltpu.VMEM((H, dk, RET_DV), f32)],
        compiler_params=_cparams(("parallel", "arbitrary")),
    )(proj3, proj3, proj3, proj3, cos, sin, dmask, kdec_f, qdec_f, gch_f)
    return y.reshape(B * L, RET_W), s


STEP_TB = 16


def _layer_grid(l, buf, inner):
    if buf is not None:
        return inner, (lambda fn: (lambda *ix: fn(l, ix, ix)))
    assert l == 0
    last = tuple(n - 1 for n in inner)

    def wrap(fn):
        def index_map(d, *ix):
            parked = tuple(jnp.where(d == l, i, z) for i, z in zip(ix, last))
            return fn(d, ix, parked)
        return index_map
    return (DEPTH,) + inner, wrap


def _retention_step(proj, s_all, l, buf, pos0):
    B = proj.shape[0]
    _, _, _, _, g_chunk, cos, sin = _ret_consts(1, pos0)
    H, dk = RET_H, RET_DK
    gch = jnp.broadcast_to(g_chunk[:, None, None], (H, 8, dk))
    tb = STEP_TB
    grid, wrap = _layer_grid(l, buf, (B // tb, H))

    def body(q_ref, k_ref, v_ref, g_ref, cos_ref, sin_ref, gc_ref, s_ref, *rest):
        y_ref, so_ref = rest[-2:]

        def update():
            cs, sn = cos_ref[...], sin_ref[...]
            q = _rotary(q_ref[...], cs, sn)
            k = _rotary(k_ref[...], cs, sn) * (dk ** -0.5)
            v = v_ref[...]
            s1 = s_ref[...] * gc_ref[0:1, :][None] + k[:, :, None] * v[:, None, :]
            so_ref[...] = s1
            o = jnp.sum(q[:, :, None] * s1, axis=1)
            g = g_ref[...]
            y_ref[...] = (g * jax.nn.sigmoid(g) * _head_norm(o, RET_GN_EPS)).astype(bf16)

        if buf is not None:
            update()
        else:
            pl.when(pl.program_id(0) == l)(update)

            @pl.when(pl.program_id(0) != l)
            def _():
                so_ref[...] = jnp.zeros_like(so_ref)

    def seg(off):
        return pl.BlockSpec((tb, dk), wrap(lambda d, ix, pk: (pk[0], off // dk + pk[1])))

    const = lambda d, ix, pk: (0, 0)
    in_specs = [seg(OFF_Q), seg(OFF_K), seg(OFF_V), seg(OFF_G),
                pl.BlockSpec((1, dk // 2), wrap(const)), pl.BlockSpec((1, dk // 2), wrap(const)),
                pl.BlockSpec((None, 8, dk), wrap(lambda d, ix, pk: (pk[1], 0, 0))),
                pl.BlockSpec((None, tb, None, dk, RET_DV), wrap(lambda d, ix, pk: (l, pk[0], pk[1], 0, 0)))]
    args = [proj, proj, proj, proj, cos, sin, gch, s_all]
    aliases = {}
    if buf is not None:
        in_specs.append(pl.BlockSpec(memory_space=pl.ANY))
        args.append(buf)
        aliases = {len(args) - 1: 1}
    return pl.pallas_call(
        body,
        grid=grid,
        in_specs=in_specs,
        out_specs=[pl.BlockSpec((tb, dk), wrap(lambda d, ix, pk: pk)),
                   pl.BlockSpec((None, tb, None, dk, RET_DV), wrap(lambda d, ix, pk: (d, ix[0], ix[1], 0, 0)))],
        out_shape=[jax.ShapeDtypeStruct((B, RET_W), bf16),
                   jax.ShapeDtypeStruct((DEPTH, B, H, dk, RET_DV), f32)],
        input_output_aliases=aliases,
        compiler_params=_cparams(("arbitrary",) * len(grid)),
    )(*args)


RW_C = 64
RW_Q = 4
RW_NQ = RWKV_H // RW_Q
RW_GROUP = 8
RW_BLK =((RWKV_PROJ + LANE - 1) // LANE) * LANE
_RW_PKEYS = ('mu', 'w0', 'w2', 'a0', 'a2', 'g2', 'k_k', 'k_a', 'r_k', 'ln_w', 'ln_b')


def _rwkv_params(p):
    return dict(
        mu=jnp.pad(p['rwkv_mu'], (0, RW_BLK - RWKV_PROJ)).reshape(1, RW_BLK),
        w0=p['rwkv_w0'].reshape(1, RWKV_W), w2=p['rwkv_w2'],
        a0=p['rwkv_a0'].reshape(1, RWKV_W), a2=p['rwkv_a2'], g2=p['rwkv_g2'],
        k_k=p['rwkv_k_k'].reshape(1, RWKV_W), k_a=p['rwkv_k_a'].reshape(1, RWKV_W),
        r_k=p['rwkv_r_k'].reshape(1, RWKV_W),
        ln_w=p['rwkv_ln_w'].reshape(1, RWKV_W), ln_b=p['rwkv_ln_b'].reshape(1, RWKV_W))


def _split_bf16(x, terms):
    out = []
    for _ in range(terms - 1):
        hi = x.astype(bf16)
        out.append(hi)
        x = x - hi.astype(f32)
    out.append(x.astype(bf16))
    return out


def _head_sum(x):
    QW = RW_Q * RWKV_N
    r = lax.broadcasted_iota(i32, (QW, QW), 0) // RWKV_N
    c = lax.broadcasted_iota(i32, (QW, QW), 1) // RWKV_N
    ones = (r == c).astype(bf16)
    parts = _split_bf16(x, 2)
    outs = []
    for q in range(RW_NQ):
        sl = slice(q * QW, (q + 1) * QW)
        outs.append(_dot(parts[0][:, sl], ones) + _dot(parts[1][:, sl], ones))
    return jnp.concatenate(outs, axis=1)


def _rwkv_mix(rw, prev, pr):
    m = rw + (prev - rw) * pr['mu'][...]
    r = m[:, 0:RWKV_W]
    k = m[:, RWKV_W:2 * RWKV_W]
    v = m[:, 2 * RWKV_W:3 * RWKV_W]
    o = 3 * RWKV_W
    xw = m[:, o:o + DECAY_LORA]
    xa = m[:, o + DECAY_LORA:o + DECAY_LORA + AAA_LORA]
    xg = m[:, o + DECAY_LORA + AAA_LORA:o + DECAY_LORA + AAA_LORA + GATE_LORA]
    w_log = -jax.nn.softplus(-(pr['w0'][...] + _dot(jnp.tanh(xw).astype(bf16), pr['w2'][...].astype(bf16)))) - 0.5
    lw = -jnp.exp(w_log)
    a = jax.nn.sigmoid(pr['a0'][...] + _dot(xa.astype(bf16), pr['a2'][...].astype(bf16)))
    g = _dot(jax.nn.sigmoid(xg).astype(bf16), pr['g2'][...].astype(bf16))
    kk = k * pr['k_k'][...]
    kk = kk / jnp.maximum(jnp.sqrt(_head_sum(kk * kk)), 1e-12)
    kf = k * (1.0 + (a - 1.0) * pr['k_a'][...])
    return r, lw, kf, v, kk, a, g


def _rwkv_out(y, r, kf, v, g, pr):
    yc = y - _head_sum(y) * (1.0 / RWKV_N)
    yn = yc * lax.rsqrt(_head_sum(yc * yc) * (1.0 / RWKV_N) + RWKV_LN_EPS)
    yn = yn * pr['ln_w'][...] + pr['ln_b'][...]
    yn = yn + _head_sum(r * kf * pr['r_k'][...]) * v
    return yn * g


def _rwkv_seq(proj, rp, B, L):
    C = RW_C
    assert L % C == 0 and C == RWKV_N
    n = L // C
    QW = RW_Q * RWKV_N
    BC = B * C
    proj3 = proj.reshape(B, L, PROJ_W)

    def body(rw_ref, *refs):
        pr = dict(zip(_RW_PKEYS, refs[:len(_RW_PKEYS)]))
        y_ref, s_ref, sh_ref, st, carry = refs[len(_RW_PKEYS):]
        c = pl.program_id(0)

        @pl.when(c == 0)
        def _():
            st[...] = jnp.zeros_like(st)
            carry[...] = jnp.zeros_like(carry)

        rw = rw_ref[...].reshape(BC, RW_BLK)
        rolled = pltpu.roll(rw, 1, 0)
        row = lax.broadcasted_iota(i32, (C, RW_BLK), 0)
        prev = jnp.concatenate(
            [jnp.where(row == 0, carry[b, 0:1, :], rolled[b * C:(b + 1) * C]) for b in range(B)], axis=0)
        for b in range(B):
            carry[b, 0:1, :] = rw[(b + 1) * C - 1:(b + 1) * C, :]
        r, lw, kf, v, kk, a, g = _rwkv_mix(rw, prev, pr)

        ti = lax.broadcasted_iota(i32, (BC, BC), 0)
        si = lax.broadcasted_iota(i32, (BC, BC), 1)
        tril = ((si <= ti) & ((si // C) == (ti // C))).astype(bf16)
        lg = sum(_dot(tril, part) for part in _split_bf16(lw, 3))
        lgc = jnp.concatenate(
            [jnp.broadcast_to(lg[(b + 1) * C - 1:(b + 1) * C, :], (C, RWKV_W)) for b in range(B)], axis=0)
        e_neg = jnp.exp(-lg)
        e_rem = jnp.exp(lgc - lg)
        at = kk * jnp.exp(lg - lw)
        ka = kk * a
        bt = ka * e_neg
        kt = kf * e_neg
        rt = r * jnp.exp(lg)
        bh = ka * e_rem
        kh = kf * e_rem
        gcr = jnp.exp(lgc)

        rr = lax.broadcasted_iota(i32, (RW_Q * C, QW), 0)
        ll = lax.broadcasted_iota(i32, (RW_Q * C, QW), 1)
        blockmask = (rr // C) == (ll // RWKV_N)
        tt = lax.broadcasted_iota(i32, (C, QW), 0)
        ss = lax.broadcasted_iota(i32, (C, QW), 1) % C
        strict = ss < tt
        incl = ss <= tt
        eye = (ss == tt).astype(f32)

        def bd(x):
            return jnp.where(blockmask, jnp.concatenate([x] * RW_Q, axis=0), 0.0).astype(bf16)

        ys = [[None] * RW_NQ for _ in range(B)]
        chains = [(b, q) for b in range(B) for q in range(RW_NQ)]
        for g0 in range(0, len(chains), RW_GROUP):
            grp = chains[g0:g0 + RW_GROUP]
            idx = [(slice(b * C, (b + 1) * C), slice(q * QW, (q + 1) * QW)) for b, q in grp]
            each = lambda fn: [fn(i) for i in range(len(grp))]
            vq = each(lambda i: v[idx[i]])
            ar = each(lambda i: jnp.concatenate([at[idx[i]], rt[idx[i]]], axis=0).astype(bf16))
            big = each(lambda i: _dot_nt(ar[i], jnp.concatenate([bd(bt[idx[i]]), bd(kt[idx[i]])], axis=0)))
            s0 = each(lambda i: st[grp[i]])
            asrs = each(lambda i: _dot_nt(ar[i], s0[i].astype(bf16)))
            nmat = each(lambda i: jnp.where(strict, big[i][:C, :QW], 0.0))
            akm = each(lambda i: jnp.where(strict, big[i][:C, QW:], 0.0).astype(bf16))
            rbk = each(lambda i: jnp.concatenate([jnp.where(incl, big[i][C:, :QW], 0.0),
                                                  jnp.where(incl, big[i][C:, QW:], 0.0)], axis=1).astype(bf16))
            tm = each(lambda i: eye - nmat[i])
            pw = each(lambda i: _dot(nmat[i].astype(bf16), bd(nmat[i])))
            lvl = 2
            while lvl < C:
                res = each(lambda i: _dot(jnp.concatenate([tm[i], pw[i]], axis=0).astype(bf16), bd(pw[i])))
                tm = each(lambda i: tm[i] + res[i][:C])
                pw = each(lambda i: res[i][C:])
                lvl *= 2
            vbd = each(lambda i: bd(vq[i]))
            rhs = each(lambda i: -(asrs[i][:C] + _dot(akm[i], vbd[i])))
            u = each(lambda i: _dot(tm[i].astype(bf16), bd(rhs[i])))
            y = each(lambda i: asrs[i][C:] + _dot(rbk[i], jnp.concatenate([bd(u[i]), vbd[i]], axis=0)))
            upd = each(lambda i: _dot_tn(jnp.concatenate([u[i], vq[i]], axis=0).astype(bf16),
                                         jnp.concatenate([bh[idx[i]], kh[idx[i]]], axis=0).astype(bf16)))
            for i, (b, q) in enumerate(grp):
                st[b, q] = s0[i] * gcr[b * C:b * C + 1, idx[i][1]] + jnp.where(blockmask, upd[i], 0.0)
                ys[b][q] = y[i]

        y = jnp.concatenate([jnp.concatenate(yb, axis=1) for yb in ys], axis=0)
        y_ref[...] = _rwkv_out(y, r, kf, v, g, pr).astype(bf16).reshape(B, C, RWKV_W)

        @pl.when(c == n - 1)
        def _():
            for b in range(B):
                for q in range(RW_NQ):
                    for h in range(RW_Q):
                        hs = slice(h * RWKV_N, (h + 1) * RWKV_N)
                        s_ref[b, q * RW_Q + h] = st[b, q, hs, hs]
                sh_ref[b] = rw[(b + 1) * C - 1:(b + 1) * C, 0:RWKV_PROJ]

    pspecs = [pl.BlockSpec(rp[k].shape, lambda c: (0, 0)) for k in _RW_PKEYS]
    y, s, sh = pl.pallas_call(
        body,
        grid=(n,),
        in_specs=[pl.BlockSpec((pl.Element(B), pl.Element(C), pl.Element(RW_BLK)),
                               lambda c: (0, pl.multiple_of(c * C, C), OFF_RW))] + pspecs,
        out_specs=[pl.BlockSpec((B, C, RWKV_W), lambda c: (0, c, 0)),
                   pl.BlockSpec((B, RWKV_H, RWKV_N, RWKV_N), lambda c: (0, 0, 0, 0)),
                   pl.BlockSpec((B, 1, RWKV_PROJ), lambda c: (0, 0, 0))],
        out_shape=[jax.ShapeDtypeStruct((B, L, RWKV_W), bf16),
                   jax.ShapeDtypeStruct((B, RWKV_H, RWKV_N, RWKV_N), f32),
                   jax.ShapeDtypeStruct((B, 1, RWKV_PROJ), f32)],
        scratch_shapes=[pltpu.VMEM((B, RW_NQ, QW, QW), f32), pltpu.VMEM((B, 8, RW_BLK), f32)],
        compiler_params=_cparams(("arbitrary",)),
    )(proj3, *[rp[k] for k in _RW_PKEYS])
    return y.reshape(B * L, RWKV_W), s, sh.reshape(B, RWKV_PROJ)


def _rwkv_step(proj, shift, s_all, l, buf, rp):
    B = proj.shape[0]
    tb = STEP_TB
    shift_p = jnp.pad(shift, ((0, 0), (0, RW_BLK - RWKV_PROJ)))
    grid, wrap = _layer_grid(l, buf, (B // tb,))
    npk = len(_RW_PKEYS)

    def body(rw_ref, sh_ref, s_ref, *refs):
        pr = dict(zip(_RW_PKEYS, refs[:npk]))
        y_ref, so_ref = refs[-2:]

        def update():
            r, lw, kf, v, kk, a, g = _rwkv_mix(rw_ref[...], sh_ref[...], pr)
            hd = lambda x: x.reshape(tb, RWKV_H, RWKV_N)
            s = s_ref[...]
            kk3 = hd(kk)
            sa = jnp.sum(s * (-kk3)[:, :, None, :], axis=-1, keepdims=True)
            s1 = (s * hd(jnp.exp(lw))[:, :, None, :] + sa * (kk3 * hd(a))[:, :, None, :]
                  + hd(v)[:, :, :, None] * hd(kf)[:, :, None, :])
            so_ref[...] = s1
            y = jnp.sum(s1 * hd(r)[:, :, None, :], axis=-1).reshape(tb, RWKV_W)
            y_ref[...] = _rwkv_out(y, r, kf, v, g, pr).astype(bf16)

        if buf is not None:
            update()
        else:
            pl.when(pl.program_id(0) == l)(update)

            @pl.when(pl.program_id(0) != l)
            def _():
                so_ref[...] = jnp.zeros_like(so_ref)

    const = lambda d, ix, pk: (0, 0)
    in_specs = [pl.BlockSpec((pl.Element(tb), pl.Element(RW_BLK)),
                             wrap(lambda d, ix, pk: (pl.multiple_of(pk[0] * tb, tb), OFF_RW))),
                pl.BlockSpec((tb, RW_BLK), wrap(lambda d, ix, pk: (pk[0], 0))),
                pl.BlockSpec((None, tb, RWKV_H, RWKV_N, RWKV_N), wrap(lambda d, ix, pk: (l, pk[0], 0, 0, 0)))]
    in_specs += [pl.BlockSpec(rp[k].shape, wrap(const)) for k in _RW_PKEYS]
    args = [proj, shift_p, s_all] + [rp[k] for k in _RW_PKEYS]
    aliases = {}
    if buf is not None:
        in_specs.append(pl.BlockSpec(memory_space=pl.ANY))
        args.append(buf)
        aliases = {len(args) - 1: 1}
    return pl.pallas_call(
        body,
        grid=grid,
        in_specs=in_specs,
        out_specs=[pl.BlockSpec((tb, RWKV_W), wrap(lambda d, ix, pk: (pk[0], 0))),
                   pl.BlockSpec((None, tb, RWKV_H, RWKV_N, RWKV_N), wrap(lambda d, ix, pk: (d, ix[0], 0, 0, 0)))],
        out_shape=[jax.ShapeDtypeStruct((B, RWKV_W), bf16),
                   jax.ShapeDtypeStruct((DEPTH, B, RWKV_H, RWKV_N, RWKV_N), f32)],
        input_output_aliases=aliases,
        compiler_params=_cparams(("arbitrary",) * len(grid)),
    )(*args)


S5_N = S5_G * S5_P
S5_KC = 256
S5_NKC = S5_W // S5_KC
S5_TILES = S5_N // LANE
S5_GB = S5_KC // S5_GC


def _s5_params(p):
    a_re, a_im = p['s5_a_re'], p['s5_a_im']
    dstep = jnp.exp(p['s5_log_dt'])[:, None]
    mag = jnp.exp(a_re * dstep)
    ab_re = mag * jnp.cos(a_im * dstep)
    ab_im = mag * jnp.sin(a_im * dstep)
    den = a_re * a_re + a_im * a_im
    n_re = ab_re - 1.0
    f_re = (n_re * a_re + ab_im * a_im) / den
    f_im = (ab_im * a_re - n_re * a_im) / den
    b_re, b_im = p['s5_b_re'], p['s5_b_im']
    bb_re = f_re[..., None] * b_re - f_im[..., None] * b_im
    bb_im = f_re[..., None] * b_im + f_im[..., None] * b_re
    eye = jnp.eye(S5_GB, dtype=f32)

    def in_map(bb):
        t = bb.reshape(S5_NKC, S5_GB, S5_P, S5_GC)
        return jnp.einsum('kgpc,gh->kgchp', t, eye).reshape(S5_NKC, S5_KC, S5_GB * S5_P)

    def out_map(cc):
        t = cc.reshape(S5_NKC, S5_GB, S5_GC, S5_P)
        return jnp.einsum('qgcp,gh->qgphc', t, eye).reshape(S5_NKC, S5_GB * S5_P, S5_KC)

    return dict(wb=jnp.concatenate([in_map(bb_re), in_map(bb_im)], axis=-1),
                wc_re=out_map(p['s5_c_re']), wc_im=out_map(p['s5_c_im']),
                ab_re_t=ab_re.reshape(S5_TILES // 8, 8, LANE), ab_im_t=ab_im.reshape(S5_TILES // 8, 8, LANE),
                ab_re=ab_re.reshape(1, S5_N), ab_im=ab_im.reshape(1, S5_N), d=p['s5_d'].reshape(1, S5_W))


def _s5_seq(proj, sp, B, L):
    Lc = min(L, 256)
    n = L // Lc
    pitch = Lc + 4
    lead = OFF_U % LANE
    base = OFF_U - lead
    width = S5_W + LANE
    nt4 = S5_TILES // 8
    tiles_kc = S5_TILES // S5_NKC

    def body(u_ref, wb_ref, wcr_ref, wci_ref, abr_ref, abi_ref, d_ref, y_ref, sr_ref, si_ref, xr, xi, cr, ci):
        c = pl.program_id(1)

        @pl.when(c == 0)
        def _():
            cr[...] = jnp.zeros_like(cr)
            ci[...] = jnp.zeros_like(ci)

        u = u_ref[:, lead:lead + S5_W]
        ub = u.astype(bf16)
        for kc in range(S5_NKC):
            bu = _dot(ub[:, kc * S5_KC:(kc + 1) * S5_KC], wb_ref[kc].astype(bf16))
            for j in range(tiles_kc):
                t = kc * tiles_kc + j
                xr[pl.ds(t * pitch, Lc), :] = bu[:, j * LANE:(j + 1) * LANE]
                xi[pl.ds(t * pitch, Lc), :] = bu[:, (tiles_kc + j) * LANE:(tiles_kc + j + 1) * LANE]

        abr = [abr_ref[g] for g in range(nt4)]
        abi = [abi_ref[g] for g in range(nt4)]

        def step(t, carry):
            out = []
            for g in range(nt4):
                s_r, s_i = carry[2 * g], carry[2 * g + 1]
                idx = pl.ds(g * 8 * pitch + t, 8, stride=pitch)
                n_r = abr[g] * s_r - abi[g] * s_i + xr[idx, :]
                n_i = abr[g] * s_i + abi[g] * s_r + xi[idx, :]
                xr[idx, :] = n_r
                xi[idx, :] = n_i
                out += [n_r, n_i]
            return tuple(out)

        init = []
        for g in range(nt4):
            init += [cr[g], ci[g]]
        fin = lax.fori_loop(0, Lc, step, tuple(init))
        for g in range(nt4):
            cr[g] = fin[2 * g]
            ci[g] = fin[2 * g + 1]

        for q in range(S5_NKC):
            lr = jnp.concatenate([xr[pl.ds((q * tiles_kc + j) * pitch, Lc), :] for j in range(tiles_kc)], axis=1)
            li = jnp.concatenate([xi[pl.ds((q * tiles_kc + j) * pitch, Lc), :] for j in range(tiles_kc)], axis=1)
            y = _dot(lr.astype(bf16), wcr_ref[q].astype(bf16)) - _dot(li.astype(bf16), wci_ref[q].astype(bf16))
            cs = slice(q * S5_KC, (q + 1) * S5_KC)
            y = y + d_ref[:, cs] * u[:, cs]
            y_ref[:, cs] = jax.nn.gelu(y).astype(bf16)

        @pl.when(c == n - 1)
        def _():
            sr_ref[...] = cr[...]
            si_ref[...] = ci[...]

    full = lambda shp: pl.BlockSpec(shp, lambda b, c: (0,) * len(shp))
    y, sr, si = pl.pallas_call(
        body,
        grid=(B, n),
        in_specs=[pl.BlockSpec((pl.Element(Lc), pl.Element(width)),
                               lambda b, c: (pl.multiple_of((b * n + c) * Lc, Lc), base)),
                  full(sp['wb'].shape), full(sp['wc_re'].shape), full(sp['wc_im'].shape),
                  full(sp['ab_re_t'].shape), full(sp['ab_im_t'].shape), full((1, S5_W))],
        out_specs=[pl.BlockSpec((Lc, S5_W), lambda b, c: (b * n + c, 0)),
                   pl.BlockSpec((None, nt4, 8, LANE), lambda b, c: (b, 0, 0, 0)),
                   pl.BlockSpec((None, nt4, 8, LANE), lambda b, c: (b, 0, 0, 0))],
        out_shape=[jax.ShapeDtypeStruct((B * L, S5_W), bf16),
                   jax.ShapeDtypeStruct((B, nt4, 8, LANE), f32),
                   jax.ShapeDtypeStruct((B, nt4, 8, LANE), f32)],
        scratch_shapes=[pltpu.VMEM((S5_TILES * pitch, LANE), f32), pltpu.VMEM((S5_TILES * pitch, LANE), f32),
                        pltpu.VMEM((nt4, 8, LANE), f32), pltpu.VMEM((nt4, 8, LANE), f32)],
        compiler_params=_cparams(("parallel", "arbitrary")),
    )(proj, sp['wb'], sp['wc_re'], sp['wc_im'], sp['ab_re_t'], sp['ab_im_t'], sp['d'])
    return y, sr.reshape(B, S5_G, S5_P), si.reshape(B, S5_G, S5_P)


def _s5_step(proj, x_re, x_im, sp):
    B = proj.shape[0]
    lead = OFF_U % LANE
    base = OFF_U - lead
    width = S5_W + LANE
    kw = S5_N // S5_NKC

    def body(u_ref, xr_ref, xi_ref, wb_ref, wcr_ref, wci_ref, abr_ref, abi_ref, d_ref, y_ref, sr_ref, si_ref):
        u = u_ref[:, lead:lead + S5_W]
        ub = u.astype(bf16)
        abr, abi = abr_ref[...], abi_ref[...]
        xr, xi = xr_ref[...], xi_ref[...]
        for kc in range(S5_NKC):
            bu = _dot(ub[:, kc * S5_KC:(kc + 1) * S5_KC], wb_ref[kc].astype(bf16))
            sl = slice(kc * kw, (kc + 1) * kw)
            n_r = abr[:, sl] * xr[:, sl] - abi[:, sl] * xi[:, sl] + bu[:, :kw]
            n_i = abr[:, sl] * xi[:, sl] + abi[:, sl] * xr[:, sl] + bu[:, kw:]
            sr_ref[:, sl] = n_r
            si_ref[:, sl] = n_i
            y = _dot(n_r.astype(bf16), wcr_ref[kc].astype(bf16)) - _dot(n_i.astype(bf16), wci_ref[kc].astype(bf16))
            cs = slice(kc * S5_KC, (kc + 1) * S5_KC)
            y = y + d_ref[:, cs] * u[:, cs]
            y_ref[:, cs] = jax.nn.gelu(y).astype(bf16)

    full = lambda shp: pl.BlockSpec(shp, lambda i: (0,) * len(shp))
    y, sr, si = pl.pallas_call(
        body,
        grid=(1,),
        in_specs=[pl.BlockSpec((pl.Element(B), pl.Element(width)), lambda i: (0, base)),
                  full((B, S5_N)), full((B, S5_N)),
                  full(sp['wb'].shape), full(sp['wc_re'].shape), full(sp['wc_im'].shape),
                  full((1, S5_N)), full((1, S5_N)), full((1, S5_W))],
        out_specs=[full((B, S5_W)), full((B, S5_N)), full((B, S5_N))],
        out_shape=[jax.ShapeDtypeStruct((B, S5_W), bf16),
                   jax.ShapeDtypeStruct((B, S5_N), f32), jax.ShapeDtypeStruct((B, S5_N), f32)],
        compiler_params=_cparams(("arbitrary",)),
    )(proj, x_re.reshape(B, S5_N), x_im.reshape(B, S5_N), sp['wb'], sp['wc_re'], sp['wc_im'],
      sp['ab_re'], sp['ab_im'], sp['d'])
    return y, sr.reshape(B, S5_G, S5_P), si.reshape(B, S5_G, S5_P)


MOE_RT = 256
ROUTE_W = LANE


def _router(x, g, mod, w_r, b_r):
    T = x.shape[0]
    tm = min(mod.tm, 512)

    def body(x_ref, g_ref, sh_ref, sc_ref, w_ref, b_ref, h_ref, e_ref, p_ref):
        h2 = _rms(x_ref[...], g_ref[...]) * (1.0 + sc_ref[...]) + sh_ref[...]
        h_ref[...] = h2
        logits = jnp.dot(h2, w_ref[...], precision=lax.Precision.HIGHEST, preferred_element_type=f32) + b_ref[...]
        lane = lax.broadcasted_iota(i32, (tm, ROUTE_W), 1)
        ninf = jnp.float32(-jnp.inf)
        gl = jnp.where(lane < N_GROUPS, logits, ninf)
        gm = jnp.max(gl, axis=-1, keepdims=True)
        g_p = 1.0 / jnp.sum(jnp.exp(gl - gm), axis=-1, keepdims=True)
        g_idx = jnp.min(jnp.where(gl == gm, lane, ROUTE_W), axis=-1, keepdims=True)
        valid = (lane >= N_GROUPS) & (lane < N_GROUPS + N_EXPERTS) & (((lane - N_GROUPS) // EPG) == g_idx)
        el = jnp.where(valid, logits, ninf)
        ee = jnp.exp(el - jnp.max(el, axis=-1, keepdims=True))
        prob = jnp.where(valid, ee / jnp.sum(ee, axis=-1, keepdims=True), -1.0)
        p1 = jnp.max(prob, axis=-1, keepdims=True)
        i1 = jnp.min(jnp.where(prob == p1, lane, ROUTE_W), axis=-1, keepdims=True)
        prob2 = jnp.where(lane == i1, -1.0, prob)
        p2 = jnp.max(prob2, axis=-1, keepdims=True)
        i2 = jnp.min(jnp.where(prob2 == p2, lane, ROUTE_W), axis=-1, keepdims=True)
        den = p1 + p2
        e_ref[...] = jnp.where(lane == 0, i1 - N_GROUPS, jnp.where(lane == 1, i2 - N_GROUPS, 0))
        p_ref[...] = jnp.where(lane == 0, g_p * p1 / den, jnp.where(lane == 1, g_p * p2 / den, 0.0))

    return pl.pallas_call(
        body,
        grid=(T // tm,),
        in_specs=[pl.BlockSpec((tm, D), lambda m: (m, 0)),
                  pl.BlockSpec((1, D), lambda m: (0, 0)),
                  mod.row_spec(3, tm), mod.row_spec(4, tm),
                  pl.BlockSpec((D, ROUTE_W), lambda m: (0, 0)),
                  pl.BlockSpec((1, ROUTE_W), lambda m: (0, 0))],
        out_specs=[pl.BlockSpec((tm, D), lambda m: (m, 0)),
                   pl.BlockSpec((tm, ROUTE_W), lambda m: (m, 0)),
                   pl.BlockSpec((tm, ROUTE_W), lambda m: (m, 0))],
        out_shape=[jax.ShapeDtypeStruct((T, D), f32), jax.ShapeDtypeStruct((T, ROUTE_W), i32),
                   jax.ShapeDtypeStruct((T, ROUTE_W), f32)],
        compiler_params=_cparams(("parallel",)),
    )(x, g.reshape(1, D), mod.arr, mod.arr, w_r, b_r)


TOK_BITS = 14


def _moe_plan(eid, T):
    A = T * TOP_K
    RT = MOE_RT
    nt = A // RT + N_EXPERTS
    M = nt * RT
    flat_e = eid[:, :TOP_K].reshape(-1)
    order = jnp.argsort(flat_e).astype(i32)
    counts = jnp.bincount(flat_e, length=N_EXPERTS).astype(i32)
    start = jnp.cumsum(counts) - counts
    pcnt = (counts + RT - 1) // RT * RT
    pend = jnp.cumsum(pcnt)
    pstart = pend - pcnt
    sorted_e = flat_e[order]
    pos = pstart[sorted_e] + (jnp.arange(A, dtype=i32) - start[sorted_e])
    is_real = jnp.zeros((M,), jnp.bool_).at[pos].set(True)
    tok = jnp.zeros((M,), i32).at[pos].set(order // TOP_K)
    spare = A + jnp.cumsum(~is_real).astype(i32) - 1
    dst = jnp.where(is_real, jnp.zeros((M,), i32).at[pos].set(order), spare)
    tile_e = jnp.minimum(jnp.searchsorted(pend, jnp.arange(nt, dtype=i32) * RT, side='right'),
                         N_EXPERTS - 1).astype(i32)
    n_used = (pend[-1] // RT).astype(i32).reshape(1)
    return tok | (dst << TOK_BITS), tile_e, n_used


def _experts(h2, packed, tile_e, n_used, l, w1, w3, w2):
    T = h2.shape[0]
    assert TOP_K == 2 and T < (1 << TOK_BITS)
    A = T * TOP_K
    RT = MOE_RT
    nt = tile_e.shape[0]
    prime_id = nt * RT
    out_rows = (prime_id + RT) // TOP_K
    HC, OC = 128, 256
    n_hc, n_oc = D_EXPERT // HC, D // OC
    g_per, s_per = RT // n_hc, RT // n_oc

    def body(tile_ref, nused_ref, slot_ref, h_hbm, w1_ref, w3_ref, w2_ref, o_hbm,
             x0, x1, o0, o1, w1b, w3b, w2b, gsem, ssem):
        j = pl.program_id(0)
        n_used = nused_ref[0]
        active = j < n_used

        def gather_row(x_ref, r, tok):
            return pltpu.make_async_copy(h_hbm.at[pl.ds(tok, 1), :], x_ref.at[pl.ds(r, 1), :], gsem)

        def scatter_row(o_ref, r, d):
            col = pl.multiple_of((d & (TOP_K - 1)) * D, D)
            return pltpu.make_async_copy(o_ref.at[pl.ds(r, 1), :],
                                         o_hbm.at[pl.ds(lax.shift_right_logical(d, 1), 1), pl.ds(col, D)], ssem)

        def wait_gather():
            pltpu.make_async_copy(h_hbm.at[pl.ds(0, RT), :], x0, gsem).wait()

        def wait_scatter():
            pltpu.make_async_copy(o0, o_hbm.at[pl.ds(0, RT), pl.ds(0, D)], ssem).wait()

        tok_of = lambda s: s & ((1 << TOK_BITS) - 1)
        dst_of = lambda s: lax.shift_right_logical(s, TOK_BITS)

        @pl.when(j == 0)
        def _():
            o1[...] = jnp.zeros_like(o1)

            def one(r, _):
                gather_row(x0, r, tok_of(slot_ref[r])).start()
                return 0
            lax.fori_loop(0, RT, one, 0)

        @pl.when(active & ((j == 0) | (tile_ref[j] != tile_ref[jnp.maximum(j - 1, 0)])))
        def _():
            w1b[...] = w1_ref[...].astype(bf16)
            w3b[...] = w3_ref[...].astype(bf16)
            w2b[...] = w2_ref[...].astype(bf16)

        @pl.when(active & (j >= 1))
        def _():
            wait_scatter()

        def tile(x_cur, x_nxt, o_cur, o_prv):
            wait_gather()
            xb = x_cur[...].astype(bf16)
            nxt = jnp.minimum(j + 1, n_used - 1) * RT
            prv = jnp.maximum(j - 1, 0) * RT
            parts = []
            for c in range(n_hc):
                h1 = _dot(xb, w1b[:, c * HC:(c + 1) * HC])
                h3 = _dot(xb, w3b[:, c * HC:(c + 1) * HC])
                parts.append(((h1 * jax.nn.sigmoid(h1)) * h3).astype(bf16))
                for r in range(c * g_per, (c + 1) * g_per):
                    gather_row(x_nxt, r, tok_of(slot_ref[nxt + r])).start()
            hm = jnp.concatenate(parts, axis=1)
            for c in range(n_oc):
                o_cur[:, c * OC:(c + 1) * OC] = _dot(hm, w2b[:, c * OC:(c + 1) * OC])
                for r in range(c * s_per, (c + 1) * s_per):
                    d = jnp.where(j == 0, prime_id + r, dst_of(slot_ref[prv + r]))
                    scatter_row(o_prv, r, d).start()

        def last(o_cur):
            wait_scatter()

            def one(r, _):
                scatter_row(o_cur, r, dst_of(slot_ref[j * RT + r])).start()
                return 0
            lax.fori_loop(0, RT, one, 0)
            wait_scatter()
            wait_gather()

        for par, (xc, xn, oc, op) in enumerate(((x0, x1, o0, o1), (x1, x0, o1, o0))):
            pl.when(active & (j % 2 == par))(lambda xc=xc, xn=xn, oc=oc, op=op: tile(xc, xn, oc, op))
        for par, oc in enumerate((o0, o1)):
            pl.when((j == n_used - 1) & (j % 2 == par))(lambda oc=oc: last(oc))

    grid_spec = pltpu.PrefetchScalarGridSpec(
        num_scalar_prefetch=3,
        grid=(nt,),
        in_specs=[pl.BlockSpec(memory_space=pl.ANY),
                  pl.BlockSpec((None, None, D, D_EXPERT), lambda j, te, *_: (l, te[j], 0, 0)),
                  pl.BlockSpec((None, None, D, D_EXPERT), lambda j, te, *_: (l, te[j], 0, 0)),
                  pl.BlockSpec((None, None, D_EXPERT, D), lambda j, te, *_: (l, te[j], 0, 0))],
        out_specs=pl.BlockSpec(memory_space=pl.ANY),
        scratch_shapes=[pltpu.VMEM((RT, D), f32), pltpu.VMEM((RT, D), f32),
                        pltpu.VMEM((RT, D), f32), pltpu.VMEM((RT, D), f32),
                        pltpu.VMEM((D, D_EXPERT), bf16), pltpu.VMEM((D, D_EXPERT), bf16),
                        pltpu.VMEM((D_EXPERT, D), bf16),
                        pltpu.SemaphoreType.DMA(()), pltpu.SemaphoreType.DMA(())],
    )
    return pl.pallas_call(
        body,
        grid_spec=grid_spec,
        out_shape=jax.ShapeDtypeStruct((out_rows, TOP_K * D), f32),
        compiler_params=_cparams(("arbitrary",)),
    )(tile_e, n_used, packed, h2, w1, w3, w2)


def _moe_combine(x, o2, wgt, mod, norm_final):
    T = x.shape[0]
    tm = min(mod.tm, 512)

    def body(x_ref, o_ref, p_ref, m_ref, *rest):
        p = p_ref[...]
        y = p[:, 0:1] * o_ref[:, 0:D]
        for k in range(1, TOP_K):
            y = y + p[:, k:k + 1] * o_ref[:, k * D:(k + 1) * D]
        xo = x_ref[...] + m_ref[...] * y
        if norm_final is None:
            rest[0][...] = xo
        else:
            rest[1][...] = _rms(xo, rest[0][...])

    ins = [x, o2, wgt, mod.arr]
    specs = [pl.BlockSpec((tm, D), lambda m: (m, 0)), pl.BlockSpec((tm, TOP_K * D), lambda m: (m, 0)),
             pl.BlockSpec((tm, ROUTE_W), lambda m: (m, 0)), mod.row_spec(5, tm)]
    if norm_final is not None:
        ins.append(norm_final.reshape(1, D))
        specs.append(pl.BlockSpec((1, D), lambda m: (0, 0)))
    return pl.pallas_call(
        body,
        grid=(T // tm,),
        in_specs=specs,
        out_specs=pl.BlockSpec((tm, D), lambda m: (m, 0)),
        out_shape=jax.ShapeDtypeStruct((T, D), f32),
        compiler_params=_cparams(("parallel",)),
    )(*ins)


def _moe(x, mod, p, l, big, norm_final):
    pad = ROUTE_W - N_GROUPS - N_EXPERTS
    w_r = jnp.pad(jnp.concatenate([p['moe_w_group'], p['moe_w_router']], axis=1), ((0, 0), (0, pad)))
    b_r = jnp.pad(jnp.concatenate([p['moe_b_group'], p['moe_b_router']]), (0, pad)).reshape(1, ROUTE_W)
    h2, eid, wgt = _router(x, p['norm_ffn'], mod, w_r, b_r)
    packed, tile_e, n_used = _moe_plan(eid, x.shape[0])
    o2 = _experts(h2, packed, tile_e, n_used, l, big['moe_w1'], big['moe_w3'], big['moe_w2'])
    return _moe_combine(x, o2, wgt, mod, norm_final)


def _trunk(x, mods, states, pos0, layers, norm_final):
    B, L, _ = x.shape
    T = B * L
    tm = _row_tile(T)
    x = x.reshape(T, D)
    outs = ([], [], [], [], [])
    big = {k: layers[k] for k in _BIG}
    w_in_t = jnp.swapaxes(layers['w_in'], 1, 2)
    ret_all = rw_all = None
    for l in range(DEPTH):
        p = {name: arr[l] for name, arr in layers.items() if name not in _BIG + ('w_in',)}
        mod = _Mod(mods[l], L, tm)
        rp, sp = _rwkv_params(p), _s5_params(p)
        proj = _in_proj(x, layers['norm_mix'], l, mod, w_in_t)
        if states is None:
            y_ret, s_ret = _retention_seq(proj, B, L)
            y_rw, s_rw, shift = _rwkv_seq(proj, rp, B, L)
            y_s5, s_re, s_im = _s5_seq(proj, sp, B, L)
        else:
            st_ret, st_rw, st_shift, st_re, st_im = states
            y_ret, ret_all = _retention_step(proj, st_ret, l, ret_all, pos0)
            y_rw, rw_all = _rwkv_step(proj, st_shift[l], st_rw, l, rw_all, rp)
            s_ret = s_rw = None
            shift = proj[:, OFF_RW:OFF_RW + RWKV_PROJ]
            y_s5, s_re, s_im = _s5_step(proj, st_re[l], st_im[l], sp)
        z = _glu_proj(y_s5, l, big['s5_w_glu'])
        merged = _merge_proj(y_ret, y_rw, z, l, big['ret_w_o'], big['rwkv_w_o'], big['s5_w_o'], proj)
        x = _out_proj(merged, l, big['w_out'], x, mod)
        x = _moe(x, mod, p, l, big, norm_final if l == DEPTH - 1 else None)
        for lst, val in zip(outs, (s_ret, s_rw, shift, s_re, s_im)):
            lst.append(val)
    stacked = [jnp.stack(o) if o[0] is not None else None for o in outs]
    if states is not None:
        stacked[0], stacked[1] = ret_all, rw_all
    return x.reshape(B, L, D), stacked


_BIG = ('ret_w_o', 'rwkv_w_o', 's5_w_glu', 's5_w_o', 'w_out', 'moe_w1', 'moe_w3', 'moe_w2')


def kernel(x_prompt, x_sample, state_ret, state_rwkv, state_shift, state_s5_re, state_s5_im,
           c_prompt, c_sample, norm_mix, norm_ffn, w_ada, b_ada, w_in, ret_w_o, rwkv_mu, rwkv_w0,
           rwkv_w2, rwkv_a0, rwkv_a2, rwkv_g2, rwkv_k_k, rwkv_k_a, rwkv_r_k, rwkv_ln_w, rwkv_ln_b,
           rwkv_w_o, s5_a_re, s5_a_im, s5_b_re, s5_b_im, s5_c_re, s5_c_im, s5_d, s5_log_dt, s5_w_glu,
           s5_w_o, w_out, moe_w_group, moe_b_group, moe_w_router, moe_b_router, moe_w1, moe_w3, moe_w2,
           norm_final):
    layers = {
        'norm_mix': norm_mix, 'norm_ffn': norm_ffn, 'w_in': w_in,
        'ret_w_o': ret_w_o, 'rwkv_mu': rwkv_mu, 'rwkv_w0': rwkv_w0, 'rwkv_w2': rwkv_w2,
        'rwkv_a0': rwkv_a0, 'rwkv_a2': rwkv_a2, 'rwkv_g2': rwkv_g2, 'rwkv_k_k': rwkv_k_k,
        'rwkv_k_a': rwkv_k_a, 'rwkv_r_k': rwkv_r_k, 'rwkv_ln_w': rwkv_ln_w, 'rwkv_ln_b': rwkv_ln_b,
        'rwkv_w_o': rwkv_w_o, 's5_a_re': s5_a_re, 's5_a_im': s5_a_im, 's5_b_re': s5_b_re,
        's5_b_im': s5_b_im, 's5_c_re': s5_c_re, 's5_c_im': s5_c_im, 's5_d': s5_d,
        's5_log_dt': s5_log_dt, 's5_w_glu': s5_w_glu, 's5_w_o': s5_w_o, 'w_out': w_out,
        'moe_w_group': moe_w_group, 'moe_b_group': moe_b_group, 'moe_w_router': moe_w_router,
        'moe_b_router': moe_b_router, 'moe_w1': moe_w1, 'moe_w3': moe_w3, 'moe_w2': moe_w2,
    }
    Bp, Bs = x_prompt.shape[0], x_sample.shape[0]
    s_off = -(-Bp // 8) * 8
    c_all = jnp.concatenate([c_prompt, jnp.zeros((s_off - Bp, D), f32), c_sample], axis=0)
    mod_all = _adaln(c_all, w_ada, b_ada)
    y_prompt, (ret_p, rwkv_p, shift_p, s5re_p, s5im_p) = _trunk(
        x_prompt, mod_all[:, :Bp], None, 0.0, layers, norm_final)
    y_sample, (ret_s, rwkv_s, shift_s, s5re_s, s5im_s) = _trunk(
        x_sample, mod_all[:, s_off:s_off + Bs], (state_ret, state_rwkv, state_shift, state_s5_re, state_s5_im),
        float(PAST_LEN), layers, norm_final)
    return (y_prompt, y_sample, ret_p, ret_s, rwkv_p, rwkv_s, shift_p, shift_s, s5re_p, s5re_s, s5im_p, s5im_s)
```

```python
import jax
import jax.numpy as jnp
from jax import lax
from jax.experimental import pallas as pl
from jax.experimental.pallas import tpu as pltpu

f32 = jnp.float32
bf16 = jnp.bfloat16
i32 = jnp.int32

D = 2048
DEPTH = 2
PAST_LEN = 16384
RET_W, RET_H, RET_DK, RET_DV, RET_CHUNK = 1024, 4, 256, 256, 128
RET_GN_EPS = 1e-6
ROPE_BASE = 10000.0
RWKV_W, RWKV_N, RWKV_H = 1024, 64, 16
DECAY_LORA, AAA_LORA, GATE_LORA = 64, 64, 160
RWKV_PROJ = 3 * RWKV_W + DECAY_LORA + AAA_LORA + GATE_LORA
RWKV_LN_EPS = 64e-5
S5_W, S5_GC, S5_G, S5_P = 1024, 16, 64, 64
N_MOD = 6
RMS_EPS = 1e-6
N_GROUPS, EPG, N_EXPERTS, TOP_K, D_EXPERT = 4, 8, 32, 2, 512
IN_W = 4 * RET_W + RWKV_PROJ + S5_W + 3 * D
OFF_Q, OFF_K, OFF_V, OFF_G, OFF_RW = 0, 1024, 2048, 3072, 4096
OFF_U = OFF_RW + RWKV_PROJ
OFF_GATE = OFF_U + S5_W
LANE = 128
PROJ_W = ((IN_W + LANE - 1) // LANE) * LANE
VMEM_LIMIT = 56 * 1024 * 1024


def _cparams(sem):
    return pltpu.CompilerParams(dimension_semantics=sem, vmem_limit_bytes=VMEM_LIMIT)


def _dot(a, b):
    return jnp.dot(a, b, preferred_element_type=f32)


def _dot_nt(a, b):
    return lax.dot_general(a, b, (((1,), (1,)), ((), ())), preferred_element_type=f32)


def _dot_tn(a, b):
    return lax.dot_general(a, b, (((0,), (0,)), ((), ())), preferred_element_type=f32)


def _rms(x, g):
    return x * lax.rsqrt(jnp.mean(x * x, axis=-1, keepdims=True) + RMS_EPS) * g


def _head_norm(y, eps):
    mu = jnp.mean(y, axis=-1, keepdims=True)
    yc = y - mu
    return yc * lax.rsqrt(jnp.mean(yc * yc, axis=-1, keepdims=True) + eps)


def _row_tile(T):
    return 1024 if T >= 1024 else T


class _Mod:
    def __init__(self, mod, L, tm):
        self.L, self.tm = L, tm
        self.per_token = L == 1
        self.arr = mod if self.per_token else mod.reshape(mod.shape[0], 1, N_MOD * D)

    def spec(self, j, tn, col_of):
        nb = D // tn
        if self.per_token:
            return pl.BlockSpec((self.tm, tn), lambda m, n: (m, j * nb + col_of(n)))
        L, tm = self.L, self.tm
        return pl.BlockSpec((None, 1, tn), lambda m, n: ((m * tm) // L, 0, j * nb + col_of(n)))

    def row_spec(self, j, tm):
        if self.per_token:
            return pl.BlockSpec((tm, D), lambda m: (m, j))
        L = self.L
        return pl.BlockSpec((None, 1, D), lambda m: ((m * tm) // L, 0, j))


def _fused_mm(x_ops, w_ops, e_ops, pre, post, *, grid, out_specs, out_shape, cache_shapes):
    nx, nw, ne = len(x_ops), len(w_ops), len(e_ops)
    n_out = len(out_shape)

    def body(*refs):
        x_refs = refs[:nx]
        w_refs = refs[nx:nx + nw]
        e_refs = refs[nx + nw:nx + nw + ne]
        o_refs = refs[nx + nw + ne:nx + nw + ne + n_out]
        caches = refs[nx + nw + ne + n_out:]
        if cache_shapes:
            @pl.when(pl.program_id(1) == 0)
            def _():
                for i in range(nx):
                    caches[i][...] = pre(i, x_refs[i], e_refs).astype(bf16)
            lhs = [c[...] for c in caches]
        else:
            lhs = [x[...] for x in x_refs]
        prods = [(_dot_nt if len(w_ops[j]) > 3 and w_ops[j][3] else _dot)(
            lhs[w_ops[j][2]], w_refs[j][...].astype(bf16)) for j in range(nw)]
        for o_ref, o in zip(o_refs, post(prods, e_refs)):
            o_ref[...] = o.astype(o_ref.dtype)

    return pl.pallas_call(
        body,
        grid=grid,
        in_specs=[s for _, s in x_ops] + [w[1] for w in w_ops] + [s for _, s in e_ops],
        out_specs=out_specs,
        out_shape=out_shape,
        scratch_shapes=[pltpu.VMEM(s, bf16) for s in cache_shapes],
        compiler_params=_cparams(("parallel", "arbitrary")),
    )(*[a for a, _ in x_ops], *[w[0] for w in w_ops], *[a for a, _ in e_ops])


def _adaln(c_all, w_ada, b_ada):
    R = c_all.shape[0]
    tn = 1024

    def pre(i, x_ref, e_refs):
        c = x_ref[...]
        return c * jax.nn.sigmoid(c)

    def post(prods, e_refs):
        return (prods[0] + e_refs[0][...],)

    (out,) = _fused_mm(
        [(c_all, pl.BlockSpec((R, D), lambda l, n: (0, 0)))],
        [(w_ada, pl.BlockSpec((None, D, tn), lambda l, n: (l, 0, n)), 0)],
        [(b_ada.reshape(DEPTH, 1, N_MOD * D), pl.BlockSpec((None, 1, tn), lambda l, n: (l, 0, n)))],
        pre, post,
        grid=(DEPTH, N_MOD * D // tn),
        out_specs=[pl.BlockSpec((None, R, tn), lambda l, n: (l, 0, n))],
        out_shape=[jax.ShapeDtypeStruct((DEPTH, R, N_MOD * D), f32)],
        cache_shapes=[(R, D)],
    )
    return out


def _in_proj(x, g, l, mod, w_in_t):
    T = x.shape[0]
    tm, tn = mod.tm, 512

    def pre(i, x_ref, e_refs):
        g_ref, shift_ref, scale_ref = e_refs
        return _rms(x_ref[...], g_ref[...]) * (1.0 + scale_ref[...]) + shift_ref[...]

    def post(prods, e_refs):
        col = pl.program_id(1) * tn + lax.broadcasted_iota(i32, (tm, tn), 1)
        return (jnp.where(col < IN_W, prods[0], 0.0),)

    (out,) = _fused_mm(
        [(x, pl.BlockSpec((tm, D), lambda m, n: (m, 0)))],
        [(w_in_t, pl.BlockSpec((None, tn, D), lambda m, n: (l, n, 0)), 0, True)],
        [(g.reshape(DEPTH, 1, D), pl.BlockSpec((None, 1, D), lambda m, n: (l, 0, 0))),
         (mod.arr, mod.spec(0, D, lambda n: 0)),
         (mod.arr, mod.spec(1, D, lambda n: 0))],
        pre, post,
        grid=(T // tm, pl.cdiv(PROJ_W, tn)),
        out_specs=[pl.BlockSpec((tm, tn), lambda m, n: (m, n))],
        out_shape=[jax.ShapeDtypeStruct((T, PROJ_W), f32)],
        cache_shapes=[(tm, D)],
    )
    return out


def _glu_proj(yg, l, w_glu):
    T = yg.shape[0]
    tm, tn = _row_tile(T), 512
    nb = S5_W // tn

    def post(prods, e_refs):
        return (prods[0] * jax.nn.sigmoid(prods[1]),)

    (out,) = _fused_mm(
        [(yg, pl.BlockSpec((tm, S5_W), lambda m, n: (m, 0)))],
        [(w_glu, pl.BlockSpec((None, S5_W, tn), lambda m, n: (l, 0, n)), 0),
         (w_glu, pl.BlockSpec((None, S5_W, tn), lambda m, n: (l, 0, nb + n)), 0)],
        [], None, post,
        grid=(T // tm, nb),
        out_specs=[pl.BlockSpec((tm, tn), lambda m, n: (m, n))],
        out_shape=[jax.ShapeDtypeStruct((T, S5_W), bf16)],
        cache_shapes=[],
    )
    return out


def _merge_proj(y_ret, y_rw, y_s5, l, w_ret, w_rw, w_s5, proj):
    T = y_ret.shape[0]
    tm, tn = _row_tile(T), 512
    lead = OFF_GATE % LANE
    base = OFF_GATE - lead

    def gate_spec(i):
        return pl.BlockSpec((pl.Element(tm), pl.Element(tn + LANE)),
                            lambda m, n: (pl.multiple_of(m * tm, tm), pl.multiple_of(base + i * D + n * tn, LANE)))

    def post(prods, e_refs):
        acc = None
        for p, e in zip(prods, e_refs):
            t = jax.nn.sigmoid(e[:, lead:lead + tn]) * p
            acc = t if acc is None else acc + t
        return (acc,)

    xspec = pl.BlockSpec((tm, RET_W), lambda m, n: (m, 0))
    wspec = pl.BlockSpec((None, RET_W, tn), lambda m, n: (l, 0, n))
    (out,) = _fused_mm(
        [(y_ret, xspec), (y_rw, xspec), (y_s5, xspec)],
        [(w_ret, wspec, 0), (w_rw, wspec, 1), (w_s5, wspec, 2)],
        [(proj, gate_spec(0)), (proj, gate_spec(1)), (proj, gate_spec(2))],
        None, post,
        grid=(T // tm, D // tn),
        out_specs=[pl.BlockSpec((tm, tn), lambda m, n: (m, n))],
        out_shape=[jax.ShapeDtypeStruct((T, D), bf16)],
        cache_shapes=[],
    )
    return out


def _out_proj(merged, l, w_out, x, mod):
    T = x.shape[0]
    tm, tn = mod.tm, 512

    def post(prods, e_refs):
        x_ref, m_ref = e_refs
        return (x_ref[...] + m_ref[...] * prods[0],)

    (out,) = _fused_mm(
        [(merged, pl.BlockSpec((tm, D), lambda m, n: (m, 0)))],
        [(w_out, pl.BlockSpec((None, D, tn), lambda m, n: (l, 0, n)), 0)],
        [(x, pl.BlockSpec((tm, tn), lambda m, n: (m, n))), (mod.arr, mod.spec(2, tn, lambda n: n))],
        None, post,
        grid=(T // tm, D // tn),
        out_specs=[pl.BlockSpec((tm, tn), lambda m, n: (m, n))],
        out_shape=[jax.ShapeDtypeStruct((T, D), f32)],
        cache_shapes=[],
    )
    return out


def _ret_consts(L, pos0):
    C = RET_CHUNK if L % RET_CHUNK == 0 else L
    H = RET_H
    log_g = jnp.log1p(-jnp.exp2(-5.0 - jnp.arange(H, dtype=f32)))
    i = jnp.arange(C, dtype=f32)
    diff = i[:, None] - i[None, :]
    causal = diff >= 0
    dmask = jnp.where(causal, jnp.exp(jnp.where(causal, diff, 0.0)[None] * log_g[:, None, None]), 0.0)
    kdec = jnp.exp((C - 1.0 - i)[:, None] * log_g[None, :])
    qdec = jnp.exp((i + 1.0)[:, None] * log_g[None, :])
    g_chunk = jnp.exp(C * log_g)
    half = RET_DK // 2
    inv = ROPE_BASE ** (-jnp.arange(half, dtype=f32) / half)
    pos = pos0 + jnp.arange(L, dtype=f32)
    ang = pos[:, None] * inv[None, :]
    return C, dmask, kdec, qdec, g_chunk, jnp.cos(ang), jnp.sin(ang)


def _rotary(x, cos, sin):
    half = RET_DK // 2
    x1, x2 = x[..., :half], x[..., half:]
    return jnp.concatenate([x1 * cos - x2 * sin, x1 * sin + x2 * cos], axis=-1)


def _retention_seq(proj, B, L):
    C, dmask, kdec, qdec, g_chunk, cos, sin = _ret_consts(L, 0.0)
    H, dk = RET_H, RET_DK
    n = L // C
    kdec_f = jnp.broadcast_to(kdec.T[:, :, None], (H, C, dk))
    qdec_f = jnp.broadcast_to(qdec.T[:, :, None], (H, C, dk))
    gch_f = jnp.broadcast_to(g_chunk[:, None, None], (H, 8, dk))
    proj3 = proj.reshape(B, L, PROJ_W)

    def body(q_ref, k_ref, v_ref, g_ref, cos_ref, sin_ref, dm_ref, kd_ref, qd_ref, gc_ref, y_ref, s_ref, st):
        c = pl.program_id(1)

        @pl.when(c == 0)
        def _():
            st[...] = jnp.zeros_like(st)

        cs, sn = cos_ref[...], sin_ref[...]
        for h in range(H):
            sl = slice(h * dk, (h + 1) * dk)
            q = _rotary(q_ref[:, sl], cs, sn)
            k = _rotary(k_ref[:, sl], cs, sn) * (dk ** -0.5)
            vb = v_ref[:, sl].astype(bf16)
            s0 = st[h]
            scores = _dot_nt(q.astype(bf16), k.astype(bf16)) * dm_ref[h]
            o = _dot(scores.astype(bf16), vb)
            o = o + _dot((q * qd_ref[h]).astype(bf16), s0.astype(bf16))
            kv = _dot_tn((k * kd_ref[h]).astype(bf16), vb)
            st[h] = s0 * gc_ref[h, 0:1, :] + kv
            g = g_ref[:, sl]
            y_ref[:, sl] = (g * jax.nn.sigmoid(g) * _head_norm(o, RET_GN_EPS)).astype(bf16)

        @pl.when(c == n - 1)
        def _():
            s_ref[...] = st[...]

    def seg(off):
        return pl.BlockSpec((None, C, RET_W), lambda b, c: (b, c, off // RET_W))

    const3 = lambda shp: pl.BlockSpec(shp, lambda b, c: (0, 0, 0))
    y, s = pl.pallas_call(
        body,
        grid=(B, n),
        in_specs=[seg(OFF_Q), seg(OFF_K), seg(OFF_V), seg(OFF_G),
                  pl.BlockSpec((C, dk // 2), lambda b, c: (c, 0)),
                  pl.BlockSpec((C, dk // 2), lambda b, c: (c, 0)),
                  const3((H, C, C)), const3((H, C, dk)), const3((H, C, dk)), const3((H, 8, dk))],
        out_specs=[pl.BlockSpec((None, C, RET_W), lambda b, c: (b, c, 0)),
                   pl.BlockSpec((None, H, dk, RET_DV), lambda b, c: (b, 0, 0, 0))],
        out_shape=[jax.ShapeDtypeStruct((B, L, RET_W), bf16),
                   jax.ShapeDtypeStruct((B, H, dk, RET_DV), f32)],
        scratch_shapes=[pltpu.VMEM((H, dk, RET_DV), f32)],
        compiler_params=_cparams(("parallel", "arbitrary")),
    )(proj3, proj3, proj3, proj3, cos, sin, dmask, kdec_f, qdec_f, gch_f)
    return y.reshape(B * L, RET_W), s


STEP_TB = 16


def _layer_grid(l, buf, inner):
    if buf is not None:
        return inner, (lambda fn: (lambda *ix: fn(l, ix, ix)))
    assert l == 0
    last = tuple(n - 1 for n in inner)

    def wrap(fn):
        def index_map(d, *ix):
            parked = tuple(jnp.where(d == l, i, z) for i, z in zip(ix, last))
            return fn(d, ix, parked)
        return index_map
    return (DEPTH,) + inner, wrap


def _retention_step(proj, s_all, l, buf, pos0):
    B = proj.shape[0]
    _, _, _, _, g_chunk, cos, sin = _ret_consts(1, pos0)
    H, dk = RET_H, RET_DK
    gch = jnp.broadcast_to(g_chunk[:, None, None], (H, 8, dk))
    tb = STEP_TB
    grid, wrap = _layer_grid(l, buf, (B // tb, H))

    def body(q_ref, k_ref, v_ref, g_ref, cos_ref, sin_ref, gc_ref, s_ref, *rest):
        y_ref, so_ref = rest[-2:]

        def update():
            cs, sn = cos_ref[...], sin_ref[...]
            q = _rotary(q_ref[...], cs, sn)
            k = _rotary(k_ref[...], cs, sn) * (dk ** -0.5)
            v = v_ref[...]
            s1 = s_ref[...] * gc_ref[0:1, :][None] + k[:, :, None] * v[:, None, :]
            so_ref[...] = s1
            o = jnp.sum(q[:, :, None] * s1, axis=1)
            g = g_ref[...]
            y_ref[...] = (g * jax.nn.sigmoid(g) * _head_norm(o, RET_GN_EPS)).astype(bf16)

        if buf is not None:
            update()
        else:
            pl.when(pl.program_id(0) == l)(update)

            @pl.when(pl.program_id(0) != l)
            def _():
                so_ref[...] = jnp.zeros_like(so_ref)

    def seg(off):
        return pl.BlockSpec((tb, dk), wrap(lambda d, ix, pk: (pk[0], off // dk + pk[1])))

    const = lambda d, ix, pk: (0, 0)
    in_specs = [seg(OFF_Q), seg(OFF_K), seg(OFF_V), seg(OFF_G),
                pl.BlockSpec((1, dk // 2), wrap(const)), pl.BlockSpec((1, dk // 2), wrap(const)),
                pl.BlockSpec((None, 8, dk), wrap(lambda d, ix, pk: (pk[1], 0, 0))),
                pl.BlockSpec((None, tb, None, dk, RET_DV), wrap(lambda d, ix, pk: (l, pk[0], pk[1], 0, 0)))]
    args = [proj, proj, proj, proj, cos, sin, gch, s_all]
    aliases = {}
    if buf is not None:
        in_specs.append(pl.BlockSpec(memory_space=pl.ANY))
        args.append(buf)
        aliases = {len(args) - 1: 1}
    return pl.pallas_call(
        body,
        grid=grid,
        in_specs=in_specs,
        out_specs=[pl.BlockSpec((tb, dk), wrap(lambda d, ix, pk: pk)),
                   pl.BlockSpec((None, tb, None, dk, RET_DV), wrap(lambda d, ix, pk: (d, ix[0], ix[1], 0, 0)))],
        out_shape=[jax.ShapeDtypeStruct((B, RET_W), bf16),
                   jax.ShapeDtypeStruct((DEPTH, B, H, dk, RET_DV), f32)],
        input_output_aliases=aliases,
        compiler_params=_cparams(("arbitrary",) * len(grid)),
    )(*args)


RW_C = 64
RW_Q = 4
RW_NQ = RWKV_H // RW_Q
RW_GROUP = 8
RW_BLK =((RWKV_PROJ + LANE - 1) // LANE) * LANE
_RW_PKEYS = ('mu', 'w0', 'w2', 'a0', 'a2', 'g2', 'k_k', 'k_a', 'r_k', 'ln_w', 'ln_b')


def _rwkv_params(p):
    return dict(
        mu=jnp.pad(p['rwkv_mu'], (0, RW_BLK - RWKV_PROJ)).reshape(1, RW_BLK),
        w0=p['rwkv_w0'].reshape(1, RWKV_W), w2=p['rwkv_w2'],
        a0=p['rwkv_a0'].reshape(1, RWKV_W), a2=p['rwkv_a2'], g2=p['rwkv_g2'],
        k_k=p['rwkv_k_k'].reshape(1, RWKV_W), k_a=p['rwkv_k_a'].reshape(1, RWKV_W),
        r_k=p['rwkv_r_k'].reshape(1, RWKV_W),
        ln_w=p['rwkv_ln_w'].reshape(1, RWKV_W), ln_b=p['rwkv_ln_b'].reshape(1, RWKV_W))


def _split_bf16(x, terms):
    out = []
    for _ in range(terms - 1):
        hi = x.astype(bf16)
        out.append(hi)
        x = x - hi.astype(f32)
    out.append(x.astype(bf16))
    return out


def _head_sum(x):
    QW = RW_Q * RWKV_N
    r = lax.broadcasted_iota(i32, (QW, QW), 0) // RWKV_N
    c = lax.broadcasted_iota(i32, (QW, QW), 1) // RWKV_N
    ones = (r == c).astype(bf16)
    parts = _split_bf16(x, 2)
    outs = []
    for q in range(RW_NQ):
        sl = slice(q * QW, (q + 1) * QW)
        outs.append(_dot(parts[0][:, sl], ones) + _dot(parts[1][:, sl], ones))
    return jnp.concatenate(outs, axis=1)


def _rwkv_mix(rw, prev, pr):
    m = rw + (prev - rw) * pr['mu'][...]
    r = m[:, 0:RWKV_W]
    k = m[:, RWKV_W:2 * RWKV_W]
    v = m[:, 2 * RWKV_W:3 * RWKV_W]
    o = 3 * RWKV_W
    xw = m[:, o:o + DECAY_LORA]
    xa = m[:, o + DECAY_LORA:o + DECAY_LORA + AAA_LORA]
    xg = m[:, o + DECAY_LORA + AAA_LORA:o + DECAY_LORA + AAA_LORA + GATE_LORA]
    w_log = -jax.nn.softplus(-(pr['w0'][...] + _dot(jnp.tanh(xw).astype(bf16), pr['w2'][...].astype(bf16)))) - 0.5
    lw = -jnp.exp(w_log)
    a = jax.nn.sigmoid(pr['a0'][...] + _dot(xa.astype(bf16), pr['a2'][...].astype(bf16)))
    g = _dot(jax.nn.sigmoid(xg).astype(bf16), pr['g2'][...].astype(bf16))
    kk = k * pr['k_k'][...]
    kk = kk / jnp.maximum(jnp.sqrt(_head_sum(kk * kk)), 1e-12)
    kf = k * (1.0 + (a - 1.0) * pr['k_a'][...])
    return r, lw, kf, v, kk, a, g


def _rwkv_out(y, r, kf, v, g, pr):
    yc = y - _head_sum(y) * (1.0 / RWKV_N)
    yn = yc * lax.rsqrt(_head_sum(yc * yc) * (1.0 / RWKV_N) + RWKV_LN_EPS)
    yn = yn * pr['ln_w'][...] + pr['ln_b'][...]
    yn = yn + _head_sum(r * kf * pr['r_k'][...]) * v
    return yn * g


def _rwkv_seq(proj, rp, B, L):
    C = RW_C
    assert L % C == 0 and C == RWKV_N
    n = L // C
    QW = RW_Q * RWKV_N
    BC = B * C
    proj3 = proj.reshape(B, L, PROJ_W)

    def body(rw_ref, *refs):
        pr = dict(zip(_RW_PKEYS, refs[:len(_RW_PKEYS)]))
        y_ref, s_ref, sh_ref, st, carry = refs[len(_RW_PKEYS):]
        c = pl.program_id(0)

        @pl.when(c == 0)
        def _():
            st[...] = jnp.zeros_like(st)
            carry[...] = jnp.zeros_like(carry)

        rw = rw_ref[...].reshape(BC, RW_BLK)
        rolled = pltpu.roll(rw, 1, 0)
        row = lax.broadcasted_iota(i32, (C, RW_BLK), 0)
        prev = jnp.concatenate(
            [jnp.where(row == 0, carry[b, 0:1, :], rolled[b * C:(b + 1) * C]) for b in range(B)], axis=0)
        for b in range(B):
            carry[b, 0:1, :] = rw[(b + 1) * C - 1:(b + 1) * C, :]
        r, lw, kf, v, kk, a, g = _rwkv_mix(rw, prev, pr)

        ti = lax.broadcasted_iota(i32, (BC, BC), 0)
        si = lax.broadcasted_iota(i32, (BC, BC), 1)
        tril = ((si <= ti) & ((si // C) == (ti // C))).astype(bf16)
        lg = sum(_dot(tril, part) for part in _split_bf16(lw, 3))
        lgc = jnp.concatenate(
            [jnp.broadcast_to(lg[(b + 1) * C - 1:(b + 1) * C, :], (C, RWKV_W)) for b in range(B)], axis=0)
        e_neg = jnp.exp(-lg)
        e_rem = jnp.exp(lgc - lg)
        at = kk * jnp.exp(lg - lw)
        ka = kk * a
        bt = ka * e_neg
        kt = kf * e_neg
        rt = r * jnp.exp(lg)
        bh = ka * e_rem
        kh = kf * e_rem
        gcr = jnp.exp(lgc)

        rr = lax.broadcasted_iota(i32, (RW_Q * C, QW), 0)
        ll = lax.broadcasted_iota(i32, (RW_Q * C, QW), 1)
        blockmask = (rr // C) == (ll // RWKV_N)
        tt = lax.broadcasted_iota(i32, (C, QW), 0)
        ss = lax.broadcasted_iota(i32, (C, QW), 1) % C
        strict = ss < tt
        incl = ss <= tt
        eye = (ss == tt).astype(f32)

        def bd(x):
            return jnp.where(blockmask, jnp.concatenate([x] * RW_Q, axis=0), 0.0).astype(bf16)

        ys = [[None] * RW_NQ for _ in range(B)]
        chains = [(b, q) for b in range(B) for q in range(RW_NQ)]
        for g0 in range(0, len(chains), RW_GROUP):
            grp = chains[g0:g0 + RW_GROUP]
            idx = [(slice(b * C, (b + 1) * C), slice(q * QW, (q + 1) * QW)) for b, q in grp]
            each = lambda fn: [fn(i) for i in range(len(grp))]
            vq = each(lambda i: v[idx[i]])
            ar = each(lambda i: jnp.concatenate([at[idx[i]], rt[idx[i]]], axis=0).astype(bf16))
            big = each(lambda i: _dot_nt(ar[i], jnp.concatenate([bd(bt[idx[i]]), bd(kt[idx[i]])], axis=0)))
            s0 = each(lambda i: st[grp[i]])
            asrs = each(lambda i: _dot_nt(ar[i], s0[i].astype(bf16)))
            nmat = each(lambda i: jnp.where(strict, big[i][:C, :QW], 0.0))
            akm = each(lambda i: jnp.where(strict, big[i][:C, QW:], 0.0).astype(bf16))
            rbk = each(lambda i: jnp.concatenate([jnp.where(incl, big[i][C:, :QW], 0.0),
                                                  jnp.where(incl, big[i][C:, QW:], 0.0)], axis=1).astype(bf16))
            tm = each(lambda i: eye - nmat[i])
            pw = each(lambda i: _dot(nmat[i].astype(bf16), bd(nmat[i])))
            lvl = 2
            while lvl < C:
                res = each(lambda i: _dot(jnp.concatenate([tm[i], pw[i]], axis=0).astype(bf16), bd(pw[i])))
                tm = each(lambda i: tm[i] + res[i][:C])
                pw = each(lambda i: res[i][C:])
                lvl *= 2
            vbd = each(lambda i: bd(vq[i]))
            rhs = each(lambda i: -(asrs[i][:C] + _dot(akm[i], vbd[i])))
            u = each(lambda i: _dot(tm[i].astype(bf16), bd(rhs[i])))
            y = each(lambda i: asrs[i][C:] + _dot(rbk[i], jnp.concatenate([bd(u[i]), vbd[i]], axis=0)))
            upd = each(lambda i: _dot_tn(jnp.concatenate([u[i], vq[i]], axis=0).astype(bf16),
                                         jnp.concatenate([bh[idx[i]], kh[idx[i]]], axis=0).astype(bf16)))
            for i, (b, q) in enumerate(grp):
                st[b, q] = s0[i] * gcr[b * C:b * C + 1, idx[i][1]] + jnp.where(blockmask, upd[i], 0.0)
                ys[b][q] = y[i]

        y = jnp.concatenate([jnp.concatenate(yb, axis=1) for yb in ys], axis=0)
        y_ref[...] = _rwkv_out(y, r, kf, v, g, pr).astype(bf16).reshape(B, C, RWKV_W)

        @pl.when(c == n - 1)
        def _():
            for b in range(B):
                for q in range(RW_NQ):
                    for h in range(RW_Q):
                        hs = slice(h * RWKV_N, (h + 1) * RWKV_N)
                        s_ref[b, q * RW_Q + h] = st[b, q, hs, hs]
                sh_ref[b] = rw[(b + 1) * C - 1:(b + 1) * C, 0:RWKV_PROJ]

    pspecs = [pl.BlockSpec(rp[k].shape, lambda c: (0, 0)) for k in _RW_PKEYS]
    y, s, sh = pl.pallas_call(
        body,
        grid=(n,),
        in_specs=[pl.BlockSpec((pl.Element(B), pl.Element(C), pl.Element(RW_BLK)),
                               lambda c: (0, pl.multiple_of(c * C, C), OFF_RW))] + pspecs,
        out_specs=[pl.BlockSpec((B, C, RWKV_W), lambda c: (0, c, 0)),
                   pl.BlockSpec((B, RWKV_H, RWKV_N, RWKV_N), lambda c: (0, 0, 0, 0)),
                   pl.BlockSpec((B, 1, RWKV_PROJ), lambda c: (0, 0, 0))],
        out_shape=[jax.ShapeDtypeStruct((B, L, RWKV_W), bf16),
                   jax.ShapeDtypeStruct((B, RWKV_H, RWKV_N, RWKV_N), f32),
                   jax.ShapeDtypeStruct((B, 1, RWKV_PROJ), f32)],
        scratch_shapes=[pltpu.VMEM((B, RW_NQ, QW, QW), f32), pltpu.VMEM((B, 8, RW_BLK), f32)],
        compiler_params=_cparams(("arbitrary",)),
    )(proj3, *[rp[k] for k in _RW_PKEYS])
    return y.reshape(B * L, RWKV_W), s, sh.reshape(B, RWKV_PROJ)


def _rwkv_step(proj, shift, s_all, l, buf, rp):
    B = proj.shape[0]
    tb = STEP_TB
    shift_p = jnp.pad(shift, ((0, 0), (0, RW_BLK - RWKV_PROJ)))
    grid, wrap = _layer_grid(l, buf, (B // tb,))
    npk = len(_RW_PKEYS)

    def body(rw_ref, sh_ref, s_ref, *refs):
        pr = dict(zip(_RW_PKEYS, refs[:npk]))
        y_ref, so_ref = refs[-2:]

        def update():
            r, lw, kf, v, kk, a, g = _rwkv_mix(rw_ref[...], sh_ref[...], pr)
            hd = lambda x: x.reshape(tb, RWKV_H, RWKV_N)
            s = s_ref[...]
            kk3 = hd(kk)
            sa = jnp.sum(s * (-kk3)[:, :, None, :], axis=-1, keepdims=True)
            s1 = (s * hd(jnp.exp(lw))[:, :, None, :] + sa * (kk3 * hd(a))[:, :, None, :]
                  + hd(v)[:, :, :, None] * hd(kf)[:, :, None, :])
            so_ref[...] = s1
            y = jnp.sum(s1 * hd(r)[:, :, None, :], axis=-1).reshape(tb, RWKV_W)
            y_ref[...] = _rwkv_out(y, r, kf, v, g, pr).astype(bf16)

        if buf is not None:
            update()
        else:
            pl.when(pl.program_id(0) == l)(update)

            @pl.when(pl.program_id(0) != l)
            def _():
                so_ref[...] = jnp.zeros_like(so_ref)

    const = lambda d, ix, pk: (0, 0)
    in_specs = [pl.BlockSpec((pl.Element(tb), pl.Element(RW_BLK)),
                             wrap(lambda d, ix, pk: (pl.multiple_of(pk[0] * tb, tb), OFF_RW))),
                pl.BlockSpec((tb, RW_BLK), wrap(lambda d, ix, pk: (pk[0], 0))),
                pl.BlockSpec((None, tb, RWKV_H, RWKV_N, RWKV_N), wrap(lambda d, ix, pk: (l, pk[0], 0, 0, 0)))]
    in_specs += [pl.BlockSpec(rp[k].shape, wrap(const)) for k in _RW_PKEYS]
    args = [proj, shift_p, s_all] + [rp[k] for k in _RW_PKEYS]
    aliases = {}
    if buf is not None:
        in_specs.append(pl.BlockSpec(memory_space=pl.ANY))
        args.append(buf)
        aliases = {len(args) - 1: 1}
    return pl.pallas_call(
        body,
        grid=grid,
        in_specs=in_specs,
        out_specs=[pl.BlockSpec((tb, RWKV_W), wrap(lambda d, ix, pk: (pk[0], 0))),
                   pl.BlockSpec((None, tb, RWKV_H, RWKV_N, RWKV_N), wrap(lambda d, ix, pk: (d, ix[0], 0, 0, 0)))],
        out_shape=[jax.ShapeDtypeStruct((B, RWKV_W), bf16),
                   jax.ShapeDtypeStruct((DEPTH, B, RWKV_H, RWKV_N, RWKV_N), f32)],
        input_output_aliases=aliases,
        compiler_params=_cparams(("arbitrary",) * len(grid)),
    )(*args)


S5_N = S5_G * S5_P
S5_KC = 256
S5_NKC = S5_W // S5_KC
S5_TILES = S5_N // LANE
S5_GB = S5_KC // S5_GC


def _s5_params(p):
    a_re, a_im = p['s5_a_re'], p['s5_a_im']
    dstep = jnp.exp(p['s5_log_dt'])[:, None]
    mag = jnp.exp(a_re * dstep)
    ab_re = mag * jnp.cos(a_im * dstep)
    ab_im = mag * jnp.sin(a_im * dstep)
    den = a_re * a_re + a_im * a_im
    n_re = ab_re - 1.0
    f_re = (n_re * a_re + ab_im * a_im) / den
    f_im = (ab_im * a_re - n_re * a_im) / den
    b_re, b_im = p['s5_b_re'], p['s5_b_im']
    bb_re = f_re[..., None] * b_re - f_im[..., None] * b_im
    bb_im = f_re[..., None] * b_im + f_im[..., None] * b_re
    eye = jnp.eye(S5_GB, dtype=f32)

    def in_map(bb):
        t = bb.reshape(S5_NKC, S5_GB, S5_P, S5_GC)
        return jnp.einsum('kgpc,gh->kgchp', t, eye).reshape(S5_NKC, S5_KC, S5_GB * S5_P)

    def out_map(cc):
        t = cc.reshape(S5_NKC, S5_GB, S5_GC, S5_P)
        return jnp.einsum('qgcp,gh->qgphc', t, eye).reshape(S5_NKC, S5_GB * S5_P, S5_KC)

    return dict(wb=jnp.concatenate([in_map(bb_re), in_map(bb_im)], axis=-1),
                wc_re=out_map(p['s5_c_re']), wc_im=out_map(p['s5_c_im']),
                ab_re_t=ab_re.reshape(S5_TILES // 8, 8, LANE), ab_im_t=ab_im.reshape(S5_TILES // 8, 8, LANE),
                ab_re=ab_re.reshape(1, S5_N), ab_im=ab_im.reshape(1, S5_N), d=p['s5_d'].reshape(1, S5_W))


def _s5_seq(proj, sp, B, L):
    Lc = min(L, 256)
    n = L // Lc
    pitch = Lc + 4
    lead = OFF_U % LANE
    base = OFF_U - lead
    width = S5_W + LANE
    nt4 = S5_TILES // 8
    tiles_kc = S5_TILES // S5_NKC

    def body(u_ref, wb_ref, wcr_ref, wci_ref, abr_ref, abi_ref, d_ref, y_ref, sr_ref, si_ref, xr, xi, cr, ci):
        c = pl.program_id(1)

        @pl.when(c == 0)
        def _():
            cr[...] = jnp.zeros_like(cr)
            ci[...] = jnp.zeros_like(ci)

        u = u_ref[:, lead:lead + S5_W]
        ub = u.astype(bf16)
        for kc in range(S5_NKC):
            bu = _dot(ub[:, kc * S5_KC:(kc + 1) * S5_KC], wb_ref[kc].astype(bf16))
            for j in range(tiles_kc):
                t = kc * tiles_kc + j
                xr[pl.ds(t * pitch, Lc), :] = bu[:, j * LANE:(j + 1) * LANE]
                xi[pl.ds(t * pitch, Lc), :] = bu[:, (tiles_kc + j) * LANE:(tiles_kc + j + 1) * LANE]

        abr = [abr_ref[g] for g in range(nt4)]
        abi = [abi_ref[g] for g in range(nt4)]

        def step(t, carry):
            out = []
            for g in range(nt4):
                s_r, s_i = carry[2 * g], carry[2 * g + 1]
                idx = pl.ds(g * 8 * pitch + t, 8, stride=pitch)
                n_r = abr[g] * s_r - abi[g] * s_i + xr[idx, :]
                n_i = abr[g] * s_i + abi[g] * s_r + xi[idx, :]
                xr[idx, :] = n_r
                xi[idx, :] = n_i
                out += [n_r, n_i]
            return tuple(out)

        init = []
        for g in range(nt4):
            init += [cr[g], ci[g]]
        fin = lax.fori_loop(0, Lc, step, tuple(init))
        for g in range(nt4):
            cr[g] = fin[2 * g]
            ci[g] = fin[2 * g + 1]

        for q in range(S5_NKC):
            lr = jnp.concatenate([xr[pl.ds((q * tiles_kc + j) * pitch, Lc), :] for j in range(tiles_kc)], axis=1)
            li = jnp.concatenate([xi[pl.ds((q * tiles_kc + j) * pitch, Lc), :] for j in range(tiles_kc)], axis=1)
            y = _dot(lr.astype(bf16), wcr_ref[q].astype(bf16)) - _dot(li.astype(bf16), wci_ref[q].astype(bf16))
            cs = slice(q * S5_KC, (q + 1) * S5_KC)
            y = y + d_ref[:, cs] * u[:, cs]
            y_ref[:, cs] = jax.nn.gelu(y).astype(bf16)

        @pl.when(c == n - 1)
        def _():
            sr_ref[...] = cr[...]
            si_ref[...] = ci[...]

    full = lambda shp: pl.BlockSpec(shp, lambda b, c: (0,) * len(shp))
    y, sr, si = pl.pallas_call(
        body,
        grid=(B, n),
        in_specs=[pl.BlockSpec((pl.Element(Lc), pl.Element(width)),
                               lambda b, c: (pl.multiple_of((b * n + c) * Lc, Lc), base)),
                  full(sp['wb'].shape), full(sp['wc_re'].shape), full(sp['wc_im'].shape),
                  full(sp['ab_re_t'].shape), full(sp['ab_im_t'].shape), full((1, S5_W))],
        out_specs=[pl.BlockSpec((Lc, S5_W), lambda b, c: (b * n + c, 0)),
                   pl.BlockSpec((None, nt4, 8, LANE), lambda b, c: (b, 0, 0, 0)),
                   pl.BlockSpec((None, nt4, 8, LANE), lambda b, c: (b, 0, 0, 0))],
        out_shape=[jax.ShapeDtypeStruct((B * L, S5_W), bf16),
                   jax.ShapeDtypeStruct((B, nt4, 8, LANE), f32),
                   jax.ShapeDtypeStruct((B, nt4, 8, LANE), f32)],
        scratch_shapes=[pltpu.VMEM((S5_TILES * pitch, LANE), f32), pltpu.VMEM((S5_TILES * pitch, LANE), f32),
                        pltpu.VMEM((nt4, 8, LANE), f32), pltpu.VMEM((nt4, 8, LANE), f32)],
        compiler_params=_cparams(("parallel", "arbitrary")),
    )(proj, sp['wb'], sp['wc_re'], sp['wc_im'], sp['ab_re_t'], sp['ab_im_t'], sp['d'])
    return y, sr.reshape(B, S5_G, S5_P), si.reshape(B, S5_G, S5_P)


def _s5_step(proj, x_re, x_im, sp):
    B = proj.shape[0]
    lead = OFF_U % LANE
    base = OFF_U - lead
    width = S5_W + LANE
    kw = S5_N // S5_NKC

    def body(u_ref, xr_ref, xi_ref, wb_ref, wcr_ref, wci_ref, abr_ref, abi_ref, d_ref, y_ref, sr_ref, si_ref):
        u = u_ref[:, lead:lead + S5_W]
        ub = u.astype(bf16)
        abr, abi = abr_ref[...], abi_ref[...]
        xr, xi = xr_ref[...], xi_ref[...]
        for kc in range(S5_NKC):
            bu = _dot(ub[:, kc * S5_KC:(kc + 1) * S5_KC], wb_ref[kc].astype(bf16))
            sl = slice(kc * kw, (kc + 1) * kw)
            n_r = abr[:, sl] * xr[:, sl] - abi[:, sl] * xi[:, sl] + bu[:, :kw]
            n_i = abr[:, sl] * xi[:, sl] + abi[:, sl] * xr[:, sl] + bu[:, kw:]
            sr_ref[:, sl] = n_r
            si_ref[:, sl] = n_i
            y = _dot(n_r.astype(bf16), wcr_ref[kc].astype(bf16)) - _dot(n_i.astype(bf16), wci_ref[kc].astype(bf16))
            cs = slice(kc * S5_KC, (kc + 1) * S5_KC)
            y = y + d_ref[:, cs] * u[:, cs]
            y_ref[:, cs] = jax.nn.gelu(y).astype(bf16)

    full = lambda shp: pl.BlockSpec(shp, lambda i: (0,) * len(shp))
    y, sr, si = pl.pallas_call(
        body,
        grid=(1,),
        in_specs=[pl.BlockSpec((pl.Element(B), pl.Element(width)), lambda i: (0, base)),
                  full((B, S5_N)), full((B, S5_N)),
                  full(sp['wb'].shape), full(sp['wc_re'].shape), full(sp['wc_im'].shape),
                  full((1, S5_N)), full((1, S5_N)), full((1, S5_W))],
        out_specs=[full((B, S5_W)), full((B, S5_N)), full((B, S5_N))],
        out_shape=[jax.ShapeDtypeStruct((B, S5_W), bf16),
                   jax.ShapeDtypeStruct((B, S5_N), f32), jax.ShapeDtypeStruct((B, S5_N), f32)],
        compiler_params=_cparams(("arbitrary",)),
    )(proj, x_re.reshape(B, S5_N), x_im.reshape(B, S5_N), sp['wb'], sp['wc_re'], sp['wc_im'],
      sp['ab_re'], sp['ab_im'], sp['d'])
    return y, sr.reshape(B, S5_G, S5_P), si.reshape(B, S5_G, S5_P)


MOE_RT = 256
MOE_RT_SMALL = 32
SLAB = D // LANE
SLAB_PITCH = SLAB + 4


def _moe_rt(T):
    return MOE_RT if T * TOP_K >= N_EXPERTS * MOE_RT else MOE_RT_SMALL
ROUTE_W = LANE


def _router(x, g, mod, w_r, b_r):
    T = x.shape[0]
    tm = min(mod.tm, 512)

    def body(x_ref, g_ref, sh_ref, sc_ref, w_ref, b_ref, h_ref, e_ref, p_ref):
        h2 = _rms(x_ref[...], g_ref[...]) * (1.0 + sc_ref[...]) + sh_ref[...]
        for s in range(SLAB):
            h_ref[:, s, :] = h2[:, s * LANE:(s + 1) * LANE]
        logits = jnp.dot(h2, w_ref[...], precision=lax.Precision.HIGHEST, preferred_element_type=f32) + b_ref[...]
        lane = lax.broadcasted_iota(i32, (tm, ROUTE_W), 1)
        ninf = jnp.float32(-jnp.inf)
        gl = jnp.where(lane < N_GROUPS, logits, ninf)
        gm = jnp.max(gl, axis=-1, keepdims=True)
        g_p = 1.0 / jnp.sum(jnp.exp(gl - gm), axis=-1, keepdims=True)
        g_idx = jnp.min(jnp.where(gl == gm, lane, ROUTE_W), axis=-1, keepdims=True)
        valid = (lane >= N_GROUPS) & (lane < N_GROUPS + N_EXPERTS) & (((lane - N_GROUPS) // EPG) == g_idx)
        el = jnp.where(valid, logits, ninf)
        ee = jnp.exp(el - jnp.max(el, axis=-1, keepdims=True))
        prob = jnp.where(valid, ee / jnp.sum(ee, axis=-1, keepdims=True), -1.0)
        p1 = jnp.max(prob, axis=-1, keepdims=True)
        i1 = jnp.min(jnp.where(prob == p1, lane, ROUTE_W), axis=-1, keepdims=True)
        prob2 = jnp.where(lane == i1, -1.0, prob)
        p2 = jnp.max(prob2, axis=-1, keepdims=True)
        i2 = jnp.min(jnp.where(prob2 == p2, lane, ROUTE_W), axis=-1, keepdims=True)
        den = p1 + p2
        e_ref[...] = jnp.where(lane == 0, i1 - N_GROUPS, jnp.where(lane == 1, i2 - N_GROUPS, 0))
        p_ref[...] = jnp.where(lane == 0, g_p * p1 / den, jnp.where(lane == 1, g_p * p2 / den, 0.0))

    return pl.pallas_call(
        body,
        grid=(T // tm,),
        in_specs=[pl.BlockSpec((tm, D), lambda m: (m, 0)),
                  pl.BlockSpec((1, D), lambda m: (0, 0)),
                  mod.row_spec(3, tm), mod.row_spec(4, tm),
                  pl.BlockSpec((D, ROUTE_W), lambda m: (0, 0)),
                  pl.BlockSpec((1, ROUTE_W), lambda m: (0, 0))],
        out_specs=[pl.BlockSpec((tm, SLAB, LANE), lambda m: (m, 0, 0)),
                   pl.BlockSpec((tm, ROUTE_W), lambda m: (m, 0)),
                   pl.BlockSpec((tm, ROUTE_W), lambda m: (m, 0))],
        out_shape=[jax.ShapeDtypeStruct((T, SLAB, LANE), f32), jax.ShapeDtypeStruct((T, ROUTE_W), i32),
                   jax.ShapeDtypeStruct((T, ROUTE_W), f32)],
        compiler_params=_cparams(("parallel",)),
    )(x, g.reshape(1, D), mod.arr, mod.arr, w_r, b_r)


TOK_BITS = 14


def _moe_plan(eid, T):
    A = T * TOP_K
    RT = _moe_rt(T)
    nt = A // RT + N_EXPERTS
    flat_e = eid[:, :TOP_K].reshape(-1)
    order = jnp.argsort(flat_e).astype(i32)
    counts = jnp.sum((flat_e[:, None] == jnp.arange(N_EXPERTS, dtype=i32)[None, :]).astype(i32), axis=0)
    start = jnp.cumsum(counts) - counts
    pcnt = (counts + RT - 1) // RT * RT
    pend = jnp.cumsum(pcnt)
    tile_e = jnp.minimum(jnp.sum((pend[None, :] <= (jnp.arange(nt, dtype=i32) * RT)[:, None]).astype(i32), axis=1),
                         N_EXPERTS - 1)
    off = (jnp.arange(nt, dtype=i32) * RT - (pend - pcnt)[tile_e])[:, None] + jnp.arange(RT, dtype=i32)[None, :]
    real = off < counts[tile_e][:, None]
    srt = start[tile_e][:, None] + jnp.minimum(off, counts[tile_e][:, None])
    a = order[jnp.minimum(srt, A - 1)]
    slot = jnp.arange(nt * RT, dtype=i32).reshape(nt, RT)
    tok = jnp.where(real, a // TOP_K, 0)
    dst = jnp.where(real, a, A + slot - srt)
    n_used = (pend[-1] // RT).astype(i32).reshape(1)
    return (tok | (dst << TOK_BITS)).reshape(-1), tile_e, n_used


def _experts(h2, packed, tile_e, n_used, l, w1, w3, w2):
    T = h2.shape[0]
    assert TOP_K == 2 and T < (1 << TOK_BITS)
    RT = _moe_rt(T)
    PITCH = SLAB_PITCH
    nt = tile_e.shape[0]
    prime_id = nt * RT
    out_rows = (prime_id + RT) // TOP_K
    HC, OC = 128, 256
    n_hc, n_oc = D_EXPERT // HC, D // OC
    g_per, s_per = RT // n_hc, RT // n_oc

    def body(tile_ref, nused_ref, slot_ref, h_hbm, h2d_hbm, w1_ref, w3_ref, w2_ref, o_hbm,
             x0, x1, o0, o1, w1b, w3b, w2b, gsem, ssem):
        j = pl.program_id(0)
        n_used = nused_ref[0]
        active = j < n_used

        def gather_row(x_ref, r, tok):
            return pltpu.make_async_copy(h_hbm.at[tok], x_ref.at[pl.ds(r * PITCH, SLAB), :], gsem)

        def scatter_row(o_ref, r, d):
            return pltpu.make_async_copy(o_ref.at[pl.ds(r * PITCH, SLAB), :],
                                         o_hbm.at[lax.shift_right_logical(d, 1), d & (TOP_K - 1)], ssem)

        def wait_rows(sem):
            pltpu.make_async_copy(h2d_hbm.at[pl.ds(0, RT * SLAB), :], x0.at[pl.ds(0, RT * SLAB), :], sem).wait()

        wait_gather = lambda: wait_rows(gsem)
        wait_scatter = lambda: wait_rows(ssem)

        tok_of = lambda s: s & ((1 << TOK_BITS) - 1)
        dst_of = lambda s: lax.shift_right_logical(s, TOK_BITS)

        @pl.when(j == 0)
        def _():
            o1[...] = jnp.zeros_like(o1)

            def one(r, _):
                gather_row(x0, r, tok_of(slot_ref[r])).start()
                return 0
            lax.fori_loop(0, RT, one, 0)

        @pl.when(active & ((j == 0) | (tile_ref[j] != tile_ref[jnp.maximum(j - 1, 0)])))
        def _():
            w1b[...] = w1_ref[...].astype(bf16)
            w3b[...] = w3_ref[...].astype(bf16)
            w2b[...] = w2_ref[...].astype(bf16)

        @pl.when(active & (j >= 1))
        def _():
            wait_scatter()

        def tile(x_cur, x_nxt, o_cur, o_prv):
            wait_gather()
            xb = jnp.concatenate([x_cur[pl.ds(s, RT, stride=PITCH), :] for s in range(SLAB)], axis=1).astype(bf16)
            nxt = jnp.minimum(j + 1, n_used - 1) * RT
            prv = jnp.maximum(j - 1, 0) * RT
            parts = []
            for c in range(n_hc):
                h1 = _dot(xb, w1b[:, c * HC:(c + 1) * HC])
                h3 = _dot(xb, w3b[:, c * HC:(c + 1) * HC])
                parts.append(((h1 * jax.nn.sigmoid(h1)) * h3).astype(bf16))
                for r in range(c * g_per, (c + 1) * g_per):
                    gather_row(x_nxt, r, tok_of(slot_ref[nxt + r])).start(priority=r % 2)
            hm = jnp.concatenate(parts, axis=1)
            for c in range(n_oc):
                res = _dot(hm, w2b[:, c * OC:(c + 1) * OC])
                for t in range(OC // LANE):
                    o_cur[pl.ds(c * (OC // LANE) + t, RT, stride=PITCH), :] = res[:, t * LANE:(t + 1) * LANE]
                for r in range(c * s_per, (c + 1) * s_per):
                    d = jnp.where(j == 0, prime_id + r, dst_of(slot_ref[prv + r]))
                    scatter_row(o_prv, r, d).start(priority=r % 2)

        def last(o_cur):
            wait_scatter()

            def one(r, _):
                scatter_row(o_cur, r, dst_of(slot_ref[j * RT + r])).start()
                return 0
            lax.fori_loop(0, RT, one, 0)
            wait_scatter()
            wait_gather()

        for par, (xc, xn, oc, op) in enumerate(((x0, x1, o0, o1), (x1, x0, o1, o0))):
            pl.when(active & (j % 2 == par))(lambda xc=xc, xn=xn, oc=oc, op=op: tile(xc, xn, oc, op))
        for par, oc in enumerate((o0, o1)):
            pl.when((j == n_used - 1) & (j % 2 == par))(lambda oc=oc: last(oc))

    grid_spec = pltpu.PrefetchScalarGridSpec(
        num_scalar_prefetch=3,
        grid=(nt,),
        in_specs=[pl.BlockSpec(memory_space=pl.ANY), pl.BlockSpec(memory_space=pl.ANY),
                  pl.BlockSpec((None, None, D, D_EXPERT), lambda j, te, *_: (l, te[j], 0, 0)),
                  pl.BlockSpec((None, None, D, D_EXPERT), lambda j, te, *_: (l, te[j], 0, 0)),
                  pl.BlockSpec((None, None, D_EXPERT, D), lambda j, te, *_: (l, te[j], 0, 0))],
        out_specs=pl.BlockSpec(memory_space=pl.ANY),
        scratch_shapes=[pltpu.VMEM((RT * PITCH, LANE), f32), pltpu.VMEM((RT * PITCH, LANE), f32),
                        pltpu.VMEM((RT * PITCH, LANE), f32), pltpu.VMEM((RT * PITCH, LANE), f32),
                        pltpu.VMEM((D, D_EXPERT), bf16), pltpu.VMEM((D, D_EXPERT), bf16),
                        pltpu.VMEM((D_EXPERT, D), bf16),
                        pltpu.SemaphoreType.DMA(()), pltpu.SemaphoreType.DMA(())],
    )
    return pl.pallas_call(
        body,
        grid_spec=grid_spec,
        out_shape=jax.ShapeDtypeStruct((out_rows, TOP_K, SLAB, LANE), f32),
        compiler_params=_cparams(("arbitrary",)),
    )(tile_e, n_used, packed, h2, h2.reshape(T * SLAB, LANE), w1, w3, w2)


def _moe_combine(x, o2, wgt, mod, norm_final):
    T = x.shape[0]
    tm = min(mod.tm, 512)

    def body(x_ref, o_ref, p_ref, m_ref, *rest):
        p = p_ref[...]
        row = lambda k: jnp.concatenate([o_ref[:, k, s, :] for s in range(SLAB)], axis=1)
        y = p[:, 0:1] * row(0)
        for k in range(1, TOP_K):
            y = y + p[:, k:k + 1] * row(k)
        xo = x_ref[...] + m_ref[...] * y
        if norm_final is None:
            rest[0][...] = xo
        else:
            rest[1][...] = _rms(xo, rest[0][...])

    ins = [x, o2, wgt, mod.arr]
    specs = [pl.BlockSpec((tm, D), lambda m: (m, 0)),
             pl.BlockSpec((tm, TOP_K, SLAB, LANE), lambda m: (m, 0, 0, 0)),
             pl.BlockSpec((tm, ROUTE_W), lambda m: (m, 0)), mod.row_spec(5, tm)]
    if norm_final is not None:
        ins.append(norm_final.reshape(1, D))
        specs.append(pl.BlockSpec((1, D), lambda m: (0, 0)))
    return pl.pallas_call(
        body,
        grid=(T // tm,),
        in_specs=specs,
        out_specs=pl.BlockSpec((tm, D), lambda m: (m, 0)),
        out_shape=jax.ShapeDtypeStruct((T, D), f32),
        compiler_params=_cparams(("parallel",)),
    )(*ins)


def _moe(x, mod, p, l, big, norm_final):
    pad = ROUTE_W - N_GROUPS - N_EXPERTS
    w_r = jnp.pad(jnp.concatenate([p['moe_w_group'], p['moe_w_router']], axis=1), ((0, 0), (0, pad)))
    b_r = jnp.pad(jnp.concatenate([p['moe_b_group'], p['moe_b_router']]), (0, pad)).reshape(1, ROUTE_W)
    h2, eid, wgt = _router(x, p['norm_ffn'], mod, w_r, b_r)
    packed, tile_e, n_used = _moe_plan(eid, x.shape[0])
    o2 = _experts(h2, packed, tile_e, n_used, l, big['moe_w1'], big['moe_w3'], big['moe_w2'])
    return _moe_combine(x, o2, wgt, mod, norm_final)


def _trunk(x, mods, states, pos0, layers, norm_final):
    B, L, _ = x.shape
    T = B * L
    tm = _row_tile(T)
    x = x.reshape(T, D)
    outs = ([], [], [], [], [])
    big = {k: layers[k] for k in _BIG}
    w_in_t = jnp.swapaxes(layers['w_in'], 1, 2)
    ret_all = rw_all = None
    for l in range(DEPTH):
        p = {name: arr[l] for name, arr in layers.items() if name not in _BIG + ('w_in',)}
        mod = _Mod(mods[l], L, tm)
        rp, sp = _rwkv_params(p), _s5_params(p)
        proj = _in_proj(x, layers['norm_mix'], l, mod, w_in_t)
        if states is None:
            y_ret, s_ret = _retention_seq(proj, B, L)
            y_rw, s_rw, shift = _rwkv_seq(proj, rp, B, L)
            y_s5, s_re, s_im = _s5_seq(proj, sp, B, L)
        else:
            st_ret, st_rw, st_shift, st_re, st_im = states
            y_ret, ret_all = _retention_step(proj, st_ret, l, ret_all, pos0)
            y_rw, rw_all = _rwkv_step(proj, st_shift[l], st_rw, l, rw_all, rp)
            s_ret = s_rw = None
            shift = proj[:, OFF_RW:OFF_RW + RWKV_PROJ]
            y_s5, s_re, s_im = _s5_step(proj, st_re[l], st_im[l], sp)
        z = _glu_proj(y_s5, l, big['s5_w_glu'])
        merged = _merge_proj(y_ret, y_rw, z, l, big['ret_w_o'], big['rwkv_w_o'], big['s5_w_o'], proj)
        x = _out_proj(merged, l, big['w_out'], x, mod)
        x = _moe(x, mod, p, l, big, norm_final if l == DEPTH - 1 else None)
        for lst, val in zip(outs, (s_ret, s_rw, shift, s_re, s_im)):
            lst.append(val)
    stacked = [jnp.stack(o) if o[0] is not None else None for o in outs]
    if states is not None:
        stacked[0], stacked[1] = ret_all, rw_all
    return x.reshape(B, L, D), stacked


_BIG = ('ret_w_o', 'rwkv_w_o', 's5_w_glu', 's5_w_o', 'w_out', 'moe_w1', 'moe_w3', 'moe_w2')


def kernel(x_prompt, x_sample, state_ret, state_rwkv, state_shift, state_s5_re, state_s5_im,
           c_prompt, c_sample, norm_mix, norm_ffn, w_ada, b_ada, w_in, ret_w_o, rwkv_mu, rwkv_w0,
           rwkv_w2, rwkv_a0, rwkv_a2, rwkv_g2, rwkv_k_k, rwkv_k_a, rwkv_r_k, rwkv_ln_w, rwkv_ln_b,
           rwkv_w_o, s5_a_re, s5_a_im, s5_b_re, s5_b_im, s5_c_re, s5_c_im, s5_d, s5_log_dt, s5_w_glu,
           s5_w_o, w_out, moe_w_group, moe_b_group, moe_w_router, moe_b_router, moe_w1, moe_w3, moe_w2,
           norm_final):
    layers = {
        'norm_mix': norm_mix, 'norm_ffn': norm_ffn, 'w_in': w_in,
        'ret_w_o': ret_w_o, 'rwkv_mu': rwkv_mu, 'rwkv_w0': rwkv_w0, 'rwkv_w2': rwkv_w2,
        'rwkv_a0': rwkv_a0, 'rwkv_a2': rwkv_a2, 'rwkv_g2': rwkv_g2, 'rwkv_k_k': rwkv_k_k,
        'rwkv_k_a': rwkv_k_a, 'rwkv_r_k': rwkv_r_k, 'rwkv_ln_w': rwkv_ln_w, 'rwkv_ln_b': rwkv_ln_b,
        'rwkv_w_o': rwkv_w_o, 's5_a_re': s5_a_re, 's5_a_im': s5_a_im, 's5_b_re': s5_b_re,
        's5_b_im': s5_b_im, 's5_c_re': s5_c_re, 's5_c_im': s5_c_im, 's5_d': s5_d,
        's5_log_dt': s5_log_dt, 's5_w_glu': s5_w_glu, 's5_w_o': s5_w_o, 'w_out': w_out,
        'moe_w_group': moe_w_group, 'moe_b_group': moe_b_group, 'moe_w_router': moe_w_router,
        'moe_b_router': moe_b_router, 'moe_w1': moe_w1, 'moe_w3': moe_w3, 'moe_w2': moe_w2,
    }
    Bp, Bs = x_prompt.shape[0], x_sample.shape[0]
    s_off = -(-Bp // 8) * 8
    c_all = jnp.concatenate([c_prompt, jnp.zeros((s_off - Bp, D), f32), c_sample], axis=0)
    mod_all = _adaln(c_all, w_ada, b_ada)
    y_prompt, (ret_p, rwkv_p, shift_p, s5re_p, s5im_p) = _trunk(
        x_prompt, mod_all[:, :Bp], None, 0.0, layers, norm_final)
    y_sample, (ret_s, rwkv_s, shift_s, s5re_s, s5im_s) = _trunk(
        x_sample, mod_all[:, s_off:s_off + Bs], (state_ret, state_rwkv, state_shift, state_s5_re, state_s5_im),
        float(PAST_LEN), layers, norm_final)
    return (y_prompt, y_sample, ret_p, ret_s, rwkv_p, rwkv_s, shift_p, shift_s, s5re_p, s5re_s, s5im_p, s5im_s)
```

```python
import jax
import jax.numpy as jnp
from jax import lax
from jax.experimental import pallas as pl
from jax.experimental.pallas import tpu as pltpu

f32 = jnp.float32
bf16 = jnp.bfloat16
i32 = jnp.int32

D = 2048
DEPTH = 2
PAST_LEN = 16384
RET_W, RET_H, RET_DK, RET_DV, RET_CHUNK = 1024, 4, 256, 256, 128
RET_GN_EPS = 1e-6
ROPE_BASE = 10000.0
RWKV_W, RWKV_N, RWKV_H = 1024, 64, 16
DECAY_LORA, AAA_LORA, GATE_LORA = 64, 64, 160
RWKV_PROJ = 3 * RWKV_W + DECAY_LORA + AAA_LORA + GATE_LORA
RWKV_LN_EPS = 64e-5
S5_W, S5_GC, S5_G, S5_P = 1024, 16, 64, 64
N_MOD = 6
RMS_EPS = 1e-6
N_GROUPS, EPG, N_EXPERTS, TOP_K, D_EXPERT = 4, 8, 32, 2, 512
IN_W = 4 * RET_W + RWKV_PROJ + S5_W + 3 * D
OFF_Q, OFF_K, OFF_V, OFF_G, OFF_RW = 0, 1024, 2048, 3072, 4096
OFF_U = OFF_RW + RWKV_PROJ
OFF_GATE = OFF_U + S5_W
LANE = 128
PROJ_W = ((IN_W + LANE - 1) // LANE) * LANE
VMEM_LIMIT = 56 * 1024 * 1024


def _cparams(sem):
    return pltpu.CompilerParams(dimension_semantics=sem, vmem_limit_bytes=VMEM_LIMIT)


def _dot(a, b):
    return jnp.dot(a, b, preferred_element_type=f32)


def _dot_nt(a, b):
    return lax.dot_general(a, b, (((1,), (1,)), ((), ())), preferred_element_type=f32)


def _dot_tn(a, b):
    return lax.dot_general(a, b, (((0,), (0,)), ((), ())), preferred_element_type=f32)


def _rms(x, g):
    return x * lax.rsqrt(jnp.mean(x * x, axis=-1, keepdims=True) + RMS_EPS) * g


def _head_norm(y, eps):
    mu = jnp.mean(y, axis=-1, keepdims=True)
    yc = y - mu
    return yc * lax.rsqrt(jnp.mean(yc * yc, axis=-1, keepdims=True) + eps)


def _row_tile(T):
    return 1024 if T >= 1024 else T


class _Mod:
    def __init__(self, mod, L, tm):
        self.L, self.tm = L, tm
        self.per_token = L == 1
        self.arr = mod if self.per_token else mod.reshape(mod.shape[0], 1, N_MOD * D)

    def spec(self, j, tn, col_of):
        nb = D // tn
        if self.per_token:
            return pl.BlockSpec((self.tm, tn), lambda m, n: (m, j * nb + col_of(n)))
        L, tm = self.L, self.tm
        return pl.BlockSpec((None, 1, tn), lambda m, n: ((m * tm) // L, 0, j * nb + col_of(n)))

    def row_spec(self, j, tm):
        if self.per_token:
            return pl.BlockSpec((tm, D), lambda m: (m, j))
        L = self.L
        return pl.BlockSpec((None, 1, D), lambda m: ((m * tm) // L, 0, j))


def _fused_mm(x_ops, w_ops, e_ops, pre, post, *, grid, out_specs, out_shape, cache_shapes):
    nx, nw, ne = len(x_ops), len(w_ops), len(e_ops)
    n_out = len(out_shape)

    def body(*refs):
        x_refs = refs[:nx]
        w_refs = refs[nx:nx + nw]
        e_refs = refs[nx + nw:nx + nw + ne]
        o_refs = refs[nx + nw + ne:nx + nw + ne + n_out]
        caches = refs[nx + nw + ne + n_out:]
        if cache_shapes:
            @pl.when(pl.program_id(1) == 0)
            def _():
                for i in range(nx):
                    caches[i][...] = pre(i, x_refs[i], e_refs).astype(bf16)
            lhs = [c[...] for c in caches]
        else:
            lhs = [x[...] for x in x_refs]
        prods = [(_dot_nt if len(w_ops[j]) > 3 and w_ops[j][3] else _dot)(
            lhs[w_ops[j][2]], w_refs[j][...].astype(bf16)) for j in range(nw)]
        for o_ref, o in zip(o_refs, post(prods, e_refs)):
            o_ref[...] = o.astype(o_ref.dtype)

    return pl.pallas_call(
        body,
        grid=grid,
        in_specs=[s for _, s in x_ops] + [w[1] for w in w_ops] + [s for _, s in e_ops],
        out_specs=out_specs,
        out_shape=out_shape,
        scratch_shapes=[pltpu.VMEM(s, bf16) for s in cache_shapes],
        compiler_params=_cparams(("parallel", "arbitrary")),
    )(*[a for a, _ in x_ops], *[w[0] for w in w_ops], *[a for a, _ in e_ops])


def _adaln(c_all, w_ada, b_ada):
    R = c_all.shape[0]
    tn = 1024

    def pre(i, x_ref, e_refs):
        c = x_ref[...]
        return c * jax.nn.sigmoid(c)

    def post(prods, e_refs):
        return (prods[0] + e_refs[0][...],)

    (out,) = _fused_mm(
        [(c_all, pl.BlockSpec((R, D), lambda l, n: (0, 0)))],
        [(w_ada, pl.BlockSpec((None, D, tn), lambda l, n: (l, 0, n)), 0)],
        [(b_ada.reshape(DEPTH, 1, N_MOD * D), pl.BlockSpec((None, 1, tn), lambda l, n: (l, 0, n)))],
        pre, post,
        grid=(DEPTH, N_MOD * D // tn),
        out_specs=[pl.BlockSpec((None, R, tn), lambda l, n: (l, 0, n))],
        out_shape=[jax.ShapeDtypeStruct((DEPTH, R, N_MOD * D), f32)],
        cache_shapes=[(R, D)],
    )
    return out


def _in_proj(x, g, l, mod, w_in_t):
    T = x.shape[0]
    tm, tn = mod.tm, 1024
    tr = min(tm, 512)

    def norm_body(x_ref, g_ref, shift_ref, scale_ref, h_ref):
        h_ref[...] = (_rms(x_ref[...], g_ref[...]) * (1.0 + scale_ref[...]) + shift_ref[...]).astype(bf16)

    h = pl.pallas_call(
        norm_body,
        grid=(T // tr,),
        in_specs=[pl.BlockSpec((tr, D), lambda m: (m, 0)),
                  pl.BlockSpec((None, 1, D), lambda m: (l, 0, 0)),
                  mod.row_spec(0, tr), mod.row_spec(1, tr)],
        out_specs=pl.BlockSpec((tr, D), lambda m: (m, 0)),
        out_shape=jax.ShapeDtypeStruct((T, D), bf16),
        compiler_params=_cparams(("parallel",)),
    )(x, g.reshape(DEPTH, 1, D), mod.arr, mod.arr)

    def mm_body(h_ref, w_ref, o_ref, wb):
        @pl.when(pl.program_id(1) == 0)
        def _():
            wb[...] = w_ref[...].astype(bf16)

        col = pl.program_id(0) * tn + lax.broadcasted_iota(i32, (tm, tn), 1)
        o_ref[...] = jnp.where(col < IN_W, _dot_nt(h_ref[...], wb[...]), 0.0)

    return pl.pallas_call(
        mm_body,
        grid=(pl.cdiv(PROJ_W, tn), T // tm),
        in_specs=[pl.BlockSpec((tm, D), lambda n, m: (m, 0)),
                  pl.BlockSpec((None, tn, D), lambda n, m: (l, n, 0))],
        out_specs=pl.BlockSpec((tm, tn), lambda n, m: (m, n)),
        out_shape=jax.ShapeDtypeStruct((T, PROJ_W), f32),
        scratch_shapes=[pltpu.VMEM((tn, D), bf16)],
        compiler_params=_cparams(("parallel", "arbitrary")),
    )(h, w_in_t)


def _glu_proj(yg, l, w_glu):
    T = yg.shape[0]
    tm, tn = _row_tile(T), 512
    nb = S5_W // tn

    def post(prods, e_refs):
        return (prods[0] * jax.nn.sigmoid(prods[1]),)

    (out,) = _fused_mm(
        [(yg, pl.BlockSpec((tm, S5_W), lambda m, n: (m, 0)))],
        [(w_glu, pl.BlockSpec((None, S5_W, tn), lambda m, n: (l, 0, n)), 0),
         (w_glu, pl.BlockSpec((None, S5_W, tn), lambda m, n: (l, 0, nb + n)), 0)],
        [], None, post,
        grid=(T // tm, nb),
        out_specs=[pl.BlockSpec((tm, tn), lambda m, n: (m, n))],
        out_shape=[jax.ShapeDtypeStruct((T, S5_W), bf16)],
        cache_shapes=[],
    )
    return out


def _merge_proj(y_ret, y_rw, y_s5, l, w_ret, w_rw, w_s5, proj):
    T = y_ret.shape[0]
    tm, tn = _row_tile(T), 512
    lead = OFF_GATE % LANE
    base = OFF_GATE - lead

    def gate_spec(i):
        return pl.BlockSpec((pl.Element(tm), pl.Element(tn + LANE)),
                            lambda m, n: (pl.multiple_of(m * tm, tm), pl.multiple_of(base + i * D + n * tn, LANE)))

    def post(prods, e_refs):
        acc = None
        for p, e in zip(prods, e_refs):
            t = jax.nn.sigmoid(e[:, lead:lead + tn]) * p
            acc = t if acc is None else acc + t
        return (acc,)

    xspec = pl.BlockSpec((tm, RET_W), lambda m, n: (m, 0))
    wspec = pl.BlockSpec((None, RET_W, tn), lambda m, n: (l, 0, n))
    (out,) = _fused_mm(
        [(y_ret, xspec), (y_rw, xspec), (y_s5, xspec)],
        [(w_ret, wspec, 0), (w_rw, wspec, 1), (w_s5, wspec, 2)],
        [(proj, gate_spec(0)), (proj, gate_spec(1)), (proj, gate_spec(2))],
        None, post,
        grid=(T // tm, D // tn),
        out_specs=[pl.BlockSpec((tm, tn), lambda m, n: (m, n))],
        out_shape=[jax.ShapeDtypeStruct((T, D), bf16)],
        cache_shapes=[],
    )
    return out


def _out_proj(merged, l, w_out, x, mod):
    T = x.shape[0]
    tm, tn = mod.tm, 512

    def post(prods, e_refs):
        x_ref, m_ref = e_refs
        return (x_ref[...] + m_ref[...] * prods[0],)

    (out,) = _fused_mm(
        [(merged, pl.BlockSpec((tm, D), lambda m, n: (m, 0)))],
        [(w_out, pl.BlockSpec((None, D, tn), lambda m, n: (l, 0, n)), 0)],
        [(x, pl.BlockSpec((tm, tn), lambda m, n: (m, n))), (mod.arr, mod.spec(2, tn, lambda n: n))],
        None, post,
        grid=(T // tm, D // tn),
        out_specs=[pl.BlockSpec((tm, tn), lambda m, n: (m, n))],
        out_shape=[jax.ShapeDtypeStruct((T, D), f32)],
        cache_shapes=[],
    )
    return out


def _ret_consts(L, pos0):
    C = RET_CHUNK if L % RET_CHUNK == 0 else L
    H = RET_H
    log_g = jnp.log1p(-jnp.exp2(-5.0 - jnp.arange(H, dtype=f32)))
    i = jnp.arange(C, dtype=f32)
    diff = i[:, None] - i[None, :]
    causal = diff >= 0
    dmask = jnp.where(causal, jnp.exp(jnp.where(causal, diff, 0.0)[None] * log_g[:, None, None]), 0.0)
    kdec = jnp.exp((C - 1.0 - i)[:, None] * log_g[None, :])
    qdec = jnp.exp((i + 1.0)[:, None] * log_g[None, :])
    g_chunk = jnp.exp(C * log_g)
    half = RET_DK // 2
    inv = ROPE_BASE ** (-jnp.arange(half, dtype=f32) / half)
    pos = pos0 + jnp.arange(L, dtype=f32)
    ang = pos[:, None] * inv[None, :]
    return C, dmask, kdec, qdec, g_chunk, jnp.cos(ang), jnp.sin(ang)


def _rotary(x, cos, sin):
    half = RET_DK // 2
    x1, x2 = x[..., :half], x[..., half:]
    return jnp.concatenate([x1 * cos - x2 * sin, x1 * sin + x2 * cos], axis=-1)


def _retention_seq(proj, B, L):
    C, dmask, kdec, qdec, g_chunk, cos, sin = _ret_consts(L, 0.0)
    H, dk = RET_H, RET_DK
    n = L // C
    kdec_f = jnp.broadcast_to(kdec.T[:, :, None], (H, C, dk))
    qdec_f = jnp.broadcast_to(qdec.T[:, :, None], (H, C, dk))
    gch_f = jnp.broadcast_to(g_chunk[:, None, None], (H, 8, dk))
    proj3 = proj.reshape(B, L, PROJ_W)

    def body(q_ref, k_ref, v_ref, g_ref, cos_ref, sin_ref, dm_ref, kd_ref, qd_ref, gc_ref, y_ref, s_ref, st):
        c = pl.program_id(1)

        @pl.when(c == 0)
        def _():
            st[...] = jnp.zeros_like(st)

        cs, sn = cos_ref[...], sin_ref[...]
        for h in range(H):
            sl = slice(h * dk, (h + 1) * dk)
            q = _rotary(q_ref[:, sl], cs, sn)
            k = _rotary(k_ref[:, sl], cs, sn) * (dk ** -0.5)
            vb = v_ref[:, sl].astype(bf16)
            s0 = st[h]
            scores = _dot_nt(q.astype(bf16), k.astype(bf16)) * dm_ref[h]
            o = _dot(scores.astype(bf16), vb)
            o = o + _dot((q * qd_ref[h]).astype(bf16), s0.astype(bf16))
            kv = _dot_tn((k * kd_ref[h]).astype(bf16), vb)
            st[h] = s0 * gc_ref[h, 0:1, :] + kv
            g = g_ref[:, sl]
            y_ref[:, sl] = (g * jax.nn.sigmoid(g) * _head_norm(o, RET_GN_EPS)).astype(bf16)

        @pl.when(c == n - 1)
        def _():
            s_ref[...] = st[...]

    def seg(off):
        return pl.BlockSpec((None, C, RET_W), lambda b, c: (b, c, off // RET_W))

    const3 = lambda shp: pl.BlockSpec(shp, lambda b, c: (0, 0, 0))
    y, s = pl.pallas_call(
        body,
        grid=(B, n),
        in_specs=[seg(OFF_Q), seg(OFF_K), seg(OFF_V), seg(OFF_G),
                  pl.BlockSpec((C, dk // 2), lambda b, c: (c, 0)),
                  pl.BlockSpec((C, dk // 2), lambda b, c: (c, 0)),
                  const3((H, C, C)), const3((H, C, dk)), const3((H, C, dk)), const3((H, 8, dk))],
        out_specs=[pl.BlockSpec((None, C, RET_W), lambda b, c: (b, c, 0)),
                   pl.BlockSpec((None, H, dk, RET_DV), lambda b, c: (b, 0, 0, 0))],
        out_shape=[jax.ShapeDtypeStruct((B, L, RET_W), bf16),
                   jax.ShapeDtypeStruct((B, H, dk, RET_DV), f32)],
        scratch_shapes=[pltpu.VMEM((H, dk, RET_DV), f32)],
        compiler_params=_cparams(("parallel", "arbitrary")),
    )(proj3, proj3, proj3, proj3, cos, sin, dmask, kdec_f, qdec_f, gch_f)
    return y.reshape(B * L, RET_W), s


STEP_TB = 16


def _layer_grid(l, buf, inner):
    if buf is not None:
        return inner, (lambda fn: (lambda *ix: fn(l, ix, ix)))
    assert l == 0
    last = tuple(n - 1 for n in inner)

    def wrap(fn):
        def index_map(d, *ix):
            parked = tuple(jnp.where(d == l, i, z) for i, z in zip(ix, last))
            return fn(d, ix, parked)
        return index_map
    return (DEPTH,) + inner, wrap


def _retention_step(proj, s_all, l, buf, pos0):
    B = proj.shape[0]
    _, _, _, _, g_chunk, cos, sin = _ret_consts(1, pos0)
    H, dk = RET_H, RET_DK
    gch = jnp.broadcast_to(g_chunk[:, None, None], (H, 8, dk))
    tb = STEP_TB
    grid, wrap = _layer_grid(l, buf, (B // tb, H))

    def body(q_ref, k_ref, v_ref, g_ref, cos_ref, sin_ref, gc_ref, s_ref, *rest):
        y_ref, so_ref = rest[-2:]

        def update():
            cs, sn = cos_ref[...], sin_ref[...]
            q = _rotary(q_ref[...], cs, sn)
            k = _rotary(k_ref[...], cs, sn) * (dk ** -0.5)
            v = v_ref[...]
            s1 = s_ref[...] * gc_ref[0:1, :][None] + k[:, :, None] * v[:, None, :]
            so_ref[...] = s1
            o = jnp.sum(q[:, :, None] * s1, axis=1)
            g = g_ref[...]
            y_ref[...] = (g * jax.nn.sigmoid(g) * _head_norm(o, RET_GN_EPS)).astype(bf16)

        if buf is not None:
            update()
        else:
            pl.when(pl.program_id(0) == l)(update)

            @pl.when(pl.program_id(0) != l)
            def _():
                so_ref[...] = jnp.zeros_like(so_ref)

    def seg(off):
        return pl.BlockSpec((tb, dk), wrap(lambda d, ix, pk: (pk[0], off // dk + pk[1])))

    const = lambda d, ix, pk: (0, 0)
    in_specs = [seg(OFF_Q), seg(OFF_K), seg(OFF_V), seg(OFF_G),
                pl.BlockSpec((1, dk // 2), wrap(const)), pl.BlockSpec((1, dk // 2), wrap(const)),
                pl.BlockSpec((None, 8, dk), wrap(lambda d, ix, pk: (pk[1], 0, 0))),
                pl.BlockSpec((None, tb, None, dk, RET_DV), wrap(lambda d, ix, pk: (l, pk[0], pk[1], 0, 0)))]
    args = [proj, proj, proj, proj, cos, sin, gch, s_all]
    aliases = {}
    if buf is not None:
        in_specs.append(pl.BlockSpec(memory_space=pl.ANY))
        args.append(buf)
        aliases = {len(args) - 1: 1}
    return pl.pallas_call(
        body,
        grid=grid,
        in_specs=in_specs,
        out_specs=[pl.BlockSpec((tb, dk), wrap(lambda d, ix, pk: pk)),
                   pl.BlockSpec((None, tb, None, dk, RET_DV), wrap(lambda d, ix, pk: (d, ix[0], ix[1], 0, 0)))],
        out_shape=[jax.ShapeDtypeStruct((B, RET_W), bf16),
                   jax.ShapeDtypeStruct((DEPTH, B, H, dk, RET_DV), f32)],
        input_output_aliases=aliases,
        compiler_params=_cparams(("arbitrary",) * len(grid)),
    )(*args)


RW_C = 64
RW_Q = 4
RW_NQ = RWKV_H // RW_Q
RW_GROUP = 8
RW_BLK =((RWKV_PROJ + LANE - 1) // LANE) * LANE
_RW_PKEYS = ('mu', 'w0', 'w2', 'a0', 'a2', 'g2', 'k_k', 'k_a', 'r_k', 'ln_w', 'ln_b')


def _rwkv_params(p):
    return dict(
        mu=jnp.pad(p['rwkv_mu'], (0, RW_BLK - RWKV_PROJ)).reshape(1, RW_BLK),
        w0=p['rwkv_w0'].reshape(1, RWKV_W), w2=p['rwkv_w2'],
        a0=p['rwkv_a0'].reshape(1, RWKV_W), a2=p['rwkv_a2'], g2=p['rwkv_g2'],
        k_k=p['rwkv_k_k'].reshape(1, RWKV_W), k_a=p['rwkv_k_a'].reshape(1, RWKV_W),
        r_k=p['rwkv_r_k'].reshape(1, RWKV_W),
        ln_w=p['rwkv_ln_w'].reshape(1, RWKV_W), ln_b=p['rwkv_ln_b'].reshape(1, RWKV_W))


def _split_bf16(x, terms):
    out = []
    for _ in range(terms - 1):
        hi = x.astype(bf16)
        out.append(hi)
        x = x - hi.astype(f32)
    out.append(x.astype(bf16))
    return out


def _head_sum(x):
    QW = RW_Q * RWKV_N
    r = lax.broadcasted_iota(i32, (QW, QW), 0) // RWKV_N
    c = lax.broadcasted_iota(i32, (QW, QW), 1) // RWKV_N
    ones = (r == c).astype(bf16)
    parts = _split_bf16(x, 2)
    outs = []
    for q in range(RW_NQ):
        sl = slice(q * QW, (q + 1) * QW)
        outs.append(_dot(parts[0][:, sl], ones) + _dot(parts[1][:, sl], ones))
    return jnp.concatenate(outs, axis=1)


def _rwkv_mix(rw, prev, pr):
    m = rw + (prev - rw) * pr['mu'][...]
    r = m[:, 0:RWKV_W]
    k = m[:, RWKV_W:2 * RWKV_W]
    v = m[:, 2 * RWKV_W:3 * RWKV_W]
    o = 3 * RWKV_W
    xw = m[:, o:o + DECAY_LORA]
    xa = m[:, o + DECAY_LORA:o + DECAY_LORA + AAA_LORA]
    xg = m[:, o + DECAY_LORA + AAA_LORA:o + DECAY_LORA + AAA_LORA + GATE_LORA]
    w_log = -jax.nn.softplus(-(pr['w0'][...] + _dot(jnp.tanh(xw).astype(bf16), pr['w2'][...].astype(bf16)))) - 0.5
    lw = -jnp.exp(w_log)
    a = jax.nn.sigmoid(pr['a0'][...] + _dot(xa.astype(bf16), pr['a2'][...].astype(bf16)))
    g = _dot(jax.nn.sigmoid(xg).astype(bf16), pr['g2'][...].astype(bf16))
    kk = k * pr['k_k'][...]
    kk = kk / jnp.maximum(jnp.sqrt(_head_sum(kk * kk)), 1e-12)
    kf = k * (1.0 + (a - 1.0) * pr['k_a'][...])
    return r, lw, kf, v, kk, a, g


def _rwkv_out(y, r, kf, v, g, pr):
    yc = y - _head_sum(y) * (1.0 / RWKV_N)
    yn = yc * lax.rsqrt(_head_sum(yc * yc) * (1.0 / RWKV_N) + RWKV_LN_EPS)
    yn = yn * pr['ln_w'][...] + pr['ln_b'][...]
    yn = yn + _head_sum(r * kf * pr['r_k'][...]) * v
    return yn * g


def _rwkv_seq(proj, rp, B, L):
    C = RW_C
    assert L % C == 0 and C == RWKV_N
    n = L // C
    QW = RW_Q * RWKV_N
    BC = B * C
    proj3 = proj.reshape(B, L, PROJ_W)

    def body(rw_ref, *refs):
        pr = dict(zip(_RW_PKEYS, refs[:len(_RW_PKEYS)]))
        y_ref, s_ref, sh_ref, st, carry = refs[len(_RW_PKEYS):]
        c = pl.program_id(0)

        @pl.when(c == 0)
        def _():
            st[...] = jnp.zeros_like(st)
            carry[...] = jnp.zeros_like(carry)

        rw = rw_ref[...].reshape(BC, RW_BLK)
        rolled = pltpu.roll(rw, 1, 0)
        row = lax.broadcasted_iota(i32, (C, RW_BLK), 0)
        prev = jnp.concatenate(
            [jnp.where(row == 0, carry[b, 0:1, :], rolled[b * C:(b + 1) * C]) for b in range(B)], axis=0)
        for b in range(B):
            carry[b, 0:1, :] = rw[(b + 1) * C - 1:(b + 1) * C, :]
        r, lw, kf, v, kk, a, g = _rwkv_mix(rw, prev, pr)

        ti = lax.broadcasted_iota(i32, (BC, BC), 0)
        si = lax.broadcasted_iota(i32, (BC, BC), 1)
        tril = ((si <= ti) & ((si // C) == (ti // C))).astype(bf16)
        lg = sum(_dot(tril, part) for part in _split_bf16(lw, 3))
        lgc = jnp.concatenate(
            [jnp.broadcast_to(lg[(b + 1) * C - 1:(b + 1) * C, :], (C, RWKV_W)) for b in range(B)], axis=0)
        e_neg = jnp.exp(-lg)
        e_rem = jnp.exp(lgc - lg)
        at = kk * jnp.exp(lg - lw)
        ka = kk * a
        bt = ka * e_neg
        kt = kf * e_neg
        rt = r * jnp.exp(lg)
        bh = ka * e_rem
        kh = kf * e_rem
        gcr = jnp.exp(lgc)

        rr = lax.broadcasted_iota(i32, (RW_Q * C, QW), 0)
        ll = lax.broadcasted_iota(i32, (RW_Q * C, QW), 1)
        blockmask = (rr // C) == (ll // RWKV_N)
        tt = lax.broadcasted_iota(i32, (C, QW), 0)
        ss = lax.broadcasted_iota(i32, (C, QW), 1) % C
        strict = ss < tt
        incl = ss <= tt
        eye = (ss == tt).astype(f32)

        def bd(x):
            return jnp.where(blockmask, jnp.concatenate([x] * RW_Q, axis=0), 0.0).astype(bf16)

        ys = [[None] * RW_NQ for _ in range(B)]
        chains = [(b, q) for b in range(B) for q in range(RW_NQ)]
        for g0 in range(0, len(chains), RW_GROUP):
            grp = chains[g0:g0 + RW_GROUP]
            idx = [(slice(b * C, (b + 1) * C), slice(q * QW, (q + 1) * QW)) for b, q in grp]
            each = lambda fn: [fn(i) for i in range(len(grp))]
            vq = each(lambda i: v[idx[i]])
            ar = each(lambda i: jnp.concatenate([at[idx[i]], rt[idx[i]]], axis=0).astype(bf16))
            big = each(lambda i: _dot_nt(ar[i], jnp.concatenate([bd(bt[idx[i]]), bd(kt[idx[i]])], axis=0)))
            s0 = each(lambda i: st[grp[i]])
            asrs = each(lambda i: _dot_nt(ar[i], s0[i].astype(bf16)))
            nmat = each(lambda i: jnp.where(strict, big[i][:C, :QW], 0.0))
            akm = each(lambda i: jnp.where(strict, big[i][:C, QW:], 0.0).astype(bf16))
            rbk = each(lambda i: jnp.concatenate([jnp.where(incl, big[i][C:, :QW], 0.0),
                                                  jnp.where(incl, big[i][C:, QW:], 0.0)], axis=1).astype(bf16))
            tm = each(lambda i: eye - nmat[i])
            pw = each(lambda i: _dot(nmat[i].astype(bf16), bd(nmat[i])))
            lvl = 2
            while lvl < C:
                res = each(lambda i: _dot(jnp.concatenate([tm[i], pw[i]], axis=0).astype(bf16), bd(pw[i])))
                tm = each(lambda i: tm[i] + res[i][:C])
                pw = each(lambda i: res[i][C:])
                lvl *= 2
            vbd = each(lambda i: bd(vq[i]))
            rhs = each(lambda i: -(asrs[i][:C] + _dot(akm[i], vbd[i])))
            u = each(lambda i: _dot(tm[i].astype(bf16), bd(rhs[i])))
            y = each(lambda i: asrs[i][C:] + _dot(rbk[i], jnp.concatenate([bd(u[i]), vbd[i]], axis=0)))
            upd = each(lambda i: _dot_tn(jnp.concatenate([u[i], vq[i]], axis=0).astype(bf16),
                                         jnp.concatenate([bh[idx[i]], kh[idx[i]]], axis=0).astype(bf16)))
            for i, (b, q) in enumerate(grp):
                st[b, q] = s0[i] * gcr[b * C:b * C + 1, idx[i][1]] + jnp.where(blockmask, upd[i], 0.0)
                ys[b][q] = y[i]

        y = jnp.concatenate([jnp.concatenate(yb, axis=1) for yb in ys], axis=0)
        y_ref[...] = _rwkv_out(y, r, kf, v, g, pr).astype(bf16).reshape(B, C, RWKV_W)

        @pl.when(c == n - 1)
        def _():
            for b in range(B):
                for q in range(RW_NQ):
                    for h in range(RW_Q):
                        hs = slice(h * RWKV_N, (h + 1) * RWKV_N)
                        s_ref[b, q * RW_Q + h] = st[b, q, hs, hs]
                sh_ref[b] = rw[(b + 1) * C - 1:(b + 1) * C, 0:RWKV_PROJ]

    pspecs = [pl.BlockSpec(rp[k].shape, lambda c: (0, 0)) for k in _RW_PKEYS]
    y, s, sh = pl.pallas_call(
        body,
        grid=(n,),
        in_specs=[pl.BlockSpec((pl.Element(B), pl.Element(C), pl.Element(RW_BLK)),
                               lambda c: (0, pl.multiple_of(c * C, C), OFF_RW))] + pspecs,
        out_specs=[pl.BlockSpec((B, C, RWKV_W), lambda c: (0, c, 0)),
                   pl.BlockSpec((B, RWKV_H, RWKV_N, RWKV_N), lambda c: (0, 0, 0, 0)),
                   pl.BlockSpec((B, 1, RWKV_PROJ), lambda c: (0, 0, 0))],
        out_shape=[jax.ShapeDtypeStruct((B, L, RWKV_W), bf16),
                   jax.ShapeDtypeStruct((B, RWKV_H, RWKV_N, RWKV_N), f32),
                   jax.ShapeDtypeStruct((B, 1, RWKV_PROJ), f32)],
        scratch_shapes=[pltpu.VMEM((B, RW_NQ, QW, QW), f32), pltpu.VMEM((B, 8, RW_BLK), f32)],
        compiler_params=_cparams(("arbitrary",)),
    )(proj3, *[rp[k] for k in _RW_PKEYS])
    return y.reshape(B * L, RWKV_W), s, sh.reshape(B, RWKV_PROJ)


def _rwkv_step(proj, shift, s_t, l, buf, rp):
    B = proj.shape[0]
    N = RWKV_N
    shift_p = jnp.pad(shift, ((0, 0), (0, RW_BLK - RWKV_PROJ)))
    grid, wrap = _layer_grid(l, buf, (RWKV_H,))
    npk = len(_RW_PKEYS)
    vec_names = ('r', 'w', 'kf', 'v', 'kk', 'ka')

    def body(rw_ref, sh_ref, s_ref, *refs):
        pr = dict(zip(_RW_PKEYS, refs[:npk]))
        y_ref, so_ref = refs[-9:-7]
        vt = dict(zip(vec_names, refs[-7:-1]))
        yt = refs[-1]
        h = pl.program_id(len(grid) - 1)

        def update():
            @pl.when(h == 0)
            def _():
                r, lw, kf, v, kk, a, _ = _rwkv_mix(rw_ref[...], sh_ref[...], pr)
                for name, val in zip(vec_names, (r, jnp.exp(lw), kf, v, kk, kk * a)):
                    vt[name][...] = val.T

            hs = pl.ds(pl.multiple_of(h * N, N), N)
            s = s_ref[...]
            kk_h = vt['kk'][hs, :]
            sa = jnp.sum(s * (-kk_h)[None], axis=1, keepdims=True)
            s1 = s * vt['w'][hs, :][None] + sa * vt['ka'][hs, :][None] + vt['v'][hs, :][:, None, :] * vt['kf'][hs, :][None]
            so_ref[...] = s1
            yt[hs, :] = jnp.sum(s1 * vt['r'][hs, :][None], axis=1)

            @pl.when(h == RWKV_H - 1)
            def _():
                r, lw, kf, v, kk, a, g = _rwkv_mix(rw_ref[...], sh_ref[...], pr)
                y_ref[...] = _rwkv_out(yt[...].T, r, kf, v, g, pr).astype(bf16)

        if buf is not None:
            update()
        else:
            pl.when(pl.program_id(0) == l)(update)

            @pl.when(pl.program_id(0) != l)
            def _():
                so_ref[...] = jnp.zeros_like(so_ref)

    const = lambda d, ix, pk: (0, 0)
    in_specs = [pl.BlockSpec((pl.Element(B), pl.Element(RW_BLK)), wrap(lambda d, ix, pk: (0, OFF_RW))),
                pl.BlockSpec((B, RW_BLK), wrap(const)),
                pl.BlockSpec((None, None, N, N, B), wrap(lambda d, ix, pk: (l, pk[0], 0, 0, 0)))]
    in_specs += [pl.BlockSpec(rp[k].shape, wrap(const)) for k in _RW_PKEYS]
    args = [proj, shift_p, s_t] + [rp[k] for k in _RW_PKEYS]
    aliases = {}
    if buf is not None:
        in_specs.append(pl.BlockSpec(memory_space=pl.ANY))
        args.append(buf)
        aliases = {len(args) - 1: 1}
    return pl.pallas_call(
        body,
        grid=grid,
        in_specs=in_specs,
        out_specs=[pl.BlockSpec((B, RWKV_W), wrap(const)),
                   pl.BlockSpec((None, None, N, N, B), wrap(lambda d, ix, pk: (d, ix[0], 0, 0, 0)))],
        out_shape=[jax.ShapeDtypeStruct((B, RWKV_W), bf16),
                   jax.ShapeDtypeStruct((DEPTH, RWKV_H, N, N, B), f32)],
        scratch_shapes=[pltpu.VMEM((RWKV_W, B), f32)] * 7,
        input_output_aliases=aliases,
        compiler_params=_cparams(("arbitrary",) * len(grid)),
    )(*args)


S5_N = S5_G * S5_P
S5_KC = 256
S5_NKC = S5_W // S5_KC
S5_TILES = S5_N // LANE
S5_GB = S5_KC // S5_GC


def _s5_params(p):
    a_re, a_im = p['s5_a_re'], p['s5_a_im']
    dstep = jnp.exp(p['s5_log_dt'])[:, None]
    mag = jnp.exp(a_re * dstep)
    ab_re = mag * jnp.cos(a_im * dstep)
    ab_im = mag * jnp.sin(a_im * dstep)
    den = a_re * a_re + a_im * a_im
    n_re = ab_re - 1.0
    f_re = (n_re * a_re + ab_im * a_im) / den
    f_im = (ab_im * a_re - n_re * a_im) / den
    b_re, b_im = p['s5_b_re'], p['s5_b_im']
    bb_re = f_re[..., None] * b_re - f_im[..., None] * b_im
    bb_im = f_re[..., None] * b_im + f_im[..., None] * b_re
    eye = jnp.eye(S5_GB, dtype=f32)

    def in_map(bb):
        t = bb.reshape(S5_NKC, S5_GB, S5_P, S5_GC)
        return jnp.einsum('kgpc,gh->kgchp', t, eye).reshape(S5_NKC, S5_KC, S5_GB * S5_P)

    def out_map(cc):
        t = cc.reshape(S5_NKC, S5_GB, S5_GC, S5_P)
        return jnp.einsum('qgcp,gh->qgphc', t, eye).reshape(S5_NKC, S5_GB * S5_P, S5_KC)

    return dict(wb=jnp.concatenate([in_map(bb_re), in_map(bb_im)], axis=-1),
                wc_re=out_map(p['s5_c_re']), wc_im=out_map(p['s5_c_im']),
                ab_re_t=ab_re.reshape(S5_TILES // 8, 8, LANE), ab_im_t=ab_im.reshape(S5_TILES // 8, 8, LANE),
                ab_re=ab_re.reshape(1, S5_N), ab_im=ab_im.reshape(1, S5_N), d=p['s5_d'].reshape(1, S5_W))


def _s5_seq(proj, sp, B, L):
    Lc = min(L, 256)
    n = L // Lc
    pitch = Lc + 4
    lead = OFF_U % LANE
    base = OFF_U - lead
    width = S5_W + LANE
    nt4 = S5_TILES // 8
    tiles_kc = S5_TILES // S5_NKC

    def body(u_ref, wb_ref, wcr_ref, wci_ref, abr_ref, abi_ref, d_ref, y_ref, sr_ref, si_ref, xr, xi, cr, ci):
        c = pl.program_id(1)

        @pl.when(c == 0)
        def _():
            cr[...] = jnp.zeros_like(cr)
            ci[...] = jnp.zeros_like(ci)

        u = u_ref[:, lead:lead + S5_W]
        ub = u.astype(bf16)
        for kc in range(S5_NKC):
            bu = _dot(ub[:, kc * S5_KC:(kc + 1) * S5_KC], wb_ref[kc].astype(bf16))
            for j in range(tiles_kc):
                t = kc * tiles_kc + j
                xr[pl.ds(t * pitch, Lc), :] = bu[:, j * LANE:(j + 1) * LANE]
                xi[pl.ds(t * pitch, Lc), :] = bu[:, (tiles_kc + j) * LANE:(tiles_kc + j + 1) * LANE]

        abr = [abr_ref[g] for g in range(nt4)]
        abi = [abi_ref[g] for g in range(nt4)]

        def step(t, carry):
            out = []
            for g in range(nt4):
                s_r, s_i = carry[2 * g], carry[2 * g + 1]
                idx = pl.ds(g * 8 * pitch + t, 8, stride=pitch)
                n_r = abr[g] * s_r - abi[g] * s_i + xr[idx, :]
                n_i = abr[g] * s_i + abi[g] * s_r + xi[idx, :]
                xr[idx, :] = n_r
                xi[idx, :] = n_i
                out += [n_r, n_i]
            return tuple(out)

        init = []
        for g in range(nt4):
            init += [cr[g], ci[g]]
        fin = lax.fori_loop(0, Lc, step, tuple(init))
        for g in range(nt4):
            cr[g] = fin[2 * g]
            ci[g] = fin[2 * g + 1]

        for q in range(S5_NKC):
            lr = jnp.concatenate([xr[pl.ds((q * tiles_kc + j) * pitch, Lc), :] for j in range(tiles_kc)], axis=1)
            li = jnp.concatenate([xi[pl.ds((q * tiles_kc + j) * pitch, Lc), :] for j in range(tiles_kc)], axis=1)
            y = _dot(lr.astype(bf16), wcr_ref[q].astype(bf16)) - _dot(li.astype(bf16), wci_ref[q].astype(bf16))
            cs = slice(q * S5_KC, (q + 1) * S5_KC)
            y = y + d_ref[:, cs] * u[:, cs]
            y_ref[:, cs] = jax.nn.gelu(y).astype(bf16)

        @pl.when(c == n - 1)
        def _():
            sr_ref[...] = cr[...]
            si_ref[...] = ci[...]

    full = lambda shp: pl.BlockSpec(shp, lambda b, c: (0,) * len(shp))
    y, sr, si = pl.pallas_call(
        body,
        grid=(B, n),
        in_specs=[pl.BlockSpec((pl.Element(Lc), pl.Element(width)),
                               lambda b, c: (pl.multiple_of((b * n + c) * Lc, Lc), base)),
                  full(sp['wb'].shape), full(sp['wc_re'].shape), full(sp['wc_im'].shape),
                  full(sp['ab_re_t'].shape), full(sp['ab_im_t'].shape), full((1, S5_W))],
        out_specs=[pl.BlockSpec((Lc, S5_W), lambda b, c: (b * n + c, 0)),
                   pl.BlockSpec((None, nt4, 8, LANE), lambda b, c: (b, 0, 0, 0)),
                   pl.BlockSpec((None, nt4, 8, LANE), lambda b, c: (b, 0, 0, 0))],
        out_shape=[jax.ShapeDtypeStruct((B * L, S5_W), bf16),
                   jax.ShapeDtypeStruct((B, nt4, 8, LANE), f32),
                   jax.ShapeDtypeStruct((B, nt4, 8, LANE), f32)],
        scratch_shapes=[pltpu.VMEM((S5_TILES * pitch, LANE), f32), pltpu.VMEM((S5_TILES * pitch, LANE), f32),
                        pltpu.VMEM((nt4, 8, LANE), f32), pltpu.VMEM((nt4, 8, LANE), f32)],
        compiler_params=_cparams(("parallel", "arbitrary")),
    )(proj, sp['wb'], sp['wc_re'], sp['wc_im'], sp['ab_re_t'], sp['ab_im_t'], sp['d'])
    return y, sr.reshape(B, S5_G, S5_P), si.reshape(B, S5_G, S5_P)


def _s5_step(proj, x_re, x_im, sp):
    B = proj.shape[0]
    lead = OFF_U % LANE
    base = OFF_U - lead
    width = S5_W + LANE
    kw = S5_N // S5_NKC

    def body(u_ref, xr_ref, xi_ref, wb_ref, wcr_ref, wci_ref, abr_ref, abi_ref, d_ref, y_ref, sr_ref, si_ref):
        u = u_ref[:, lead:lead + S5_W]
        ub = u.astype(bf16)
        abr, abi = abr_ref[...], abi_ref[...]
        xr, xi = xr_ref[...], xi_ref[...]
        for kc in range(S5_NKC):
            bu = _dot(ub[:, kc * S5_KC:(kc + 1) * S5_KC], wb_ref[kc].astype(bf16))
            sl = slice(kc * kw, (kc + 1) * kw)
            n_r = abr[:, sl] * xr[:, sl] - abi[:, sl] * xi[:, sl] + bu[:, :kw]
            n_i = abr[:, sl] * xi[:, sl] + abi[:, sl] * xr[:, sl] + bu[:, kw:]
            sr_ref[:, sl] = n_r
            si_ref[:, sl] = n_i
            y = _dot(n_r.astype(bf16), wcr_ref[kc].astype(bf16)) - _dot(n_i.astype(bf16), wci_ref[kc].astype(bf16))
            cs = slice(kc * S5_KC, (kc + 1) * S5_KC)
            y = y + d_ref[:, cs] * u[:, cs]
            y_ref[:, cs] = jax.nn.gelu(y).astype(bf16)

    full = lambda shp: pl.BlockSpec(shp, lambda i: (0,) * len(shp))
    y, sr, si = pl.pallas_call(
        body,
        grid=(1,),
        in_specs=[pl.BlockSpec((pl.Element(B), pl.Element(width)), lambda i: (0, base)),
                  full((B, S5_N)), full((B, S5_N)),
                  full(sp['wb'].shape), full(sp['wc_re'].shape), full(sp['wc_im'].shape),
                  full((1, S5_N)), full((1, S5_N)), full((1, S5_W))],
        out_specs=[full((B, S5_W)), full((B, S5_N)), full((B, S5_N))],
        out_shape=[jax.ShapeDtypeStruct((B, S5_W), bf16),
                   jax.ShapeDtypeStruct((B, S5_N), f32), jax.ShapeDtypeStruct((B, S5_N), f32)],
        compiler_params=_cparams(("arbitrary",)),
    )(proj, x_re.reshape(B, S5_N), x_im.reshape(B, S5_N), sp['wb'], sp['wc_re'], sp['wc_im'],
      sp['ab_re'], sp['ab_im'], sp['d'])
    return y, sr.reshape(B, S5_G, S5_P), si.reshape(B, S5_G, S5_P)


MOE_RT = 256
MOE_RT_SMALL = 32
SLAB = D // LANE
SLAB_PITCH = SLAB + 4


def _moe_rt(T):
    return MOE_RT if T * TOP_K >= N_EXPERTS * MOE_RT else MOE_RT_SMALL
ROUTE_W = LANE


def _router(x, g, mod, w_r, b_r):
    T = x.shape[0]
    tm = min(mod.tm, 512)

    def body(x_ref, g_ref, sh_ref, sc_ref, w_ref, b_ref, h_ref, e_ref, p_ref):
        h2 = _rms(x_ref[...], g_ref[...]) * (1.0 + sc_ref[...]) + sh_ref[...]
        for s in range(SLAB):
            h_ref[:, s, :] = h2[:, s * LANE:(s + 1) * LANE]
        logits = jnp.dot(h2, w_ref[...], precision=lax.Precision.HIGHEST, preferred_element_type=f32) + b_ref[...]
        lane = lax.broadcasted_iota(i32, (tm, ROUTE_W), 1)
        ninf = jnp.float32(-jnp.inf)
        gl = jnp.where(lane < N_GROUPS, logits, ninf)
        gm = jnp.max(gl, axis=-1, keepdims=True)
        g_p = 1.0 / jnp.sum(jnp.exp(gl - gm), axis=-1, keepdims=True)
        g_idx = jnp.min(jnp.where(gl == gm, lane, ROUTE_W), axis=-1, keepdims=True)
        valid = (lane >= N_GROUPS) & (lane < N_GROUPS + N_EXPERTS) & (((lane - N_GROUPS) // EPG) == g_idx)
        el = jnp.where(valid, logits, ninf)
        ee = jnp.exp(el - jnp.max(el, axis=-1, keepdims=True))
        prob = jnp.where(valid, ee / jnp.sum(ee, axis=-1, keepdims=True), -1.0)
        p1 = jnp.max(prob, axis=-1, keepdims=True)
        i1 = jnp.min(jnp.where(prob == p1, lane, ROUTE_W), axis=-1, keepdims=True)
        prob2 = jnp.where(lane == i1, -1.0, prob)
        p2 = jnp.max(prob2, axis=-1, keepdims=True)
        i2 = jnp.min(jnp.where(prob2 == p2, lane, ROUTE_W), axis=-1, keepdims=True)
        den = p1 + p2
        e_ref[...] = jnp.where(lane == 0, i1 - N_GROUPS, jnp.where(lane == 1, i2 - N_GROUPS, 0))
        p_ref[...] = jnp.where(lane == 0, g_p * p1 / den, jnp.where(lane == 1, g_p * p2 / den, 0.0))

    return pl.pallas_call(
        body,
        grid=(T // tm,),
        in_specs=[pl.BlockSpec((tm, D), lambda m: (m, 0)),
                  pl.BlockSpec((1, D), lambda m: (0, 0)),
                  mod.row_spec(3, tm), mod.row_spec(4, tm),
                  pl.BlockSpec((D, ROUTE_W), lambda m: (0, 0)),
                  pl.BlockSpec((1, ROUTE_W), lambda m: (0, 0))],
        out_specs=[pl.BlockSpec((tm, SLAB, LANE), lambda m: (m, 0, 0)),
                   pl.BlockSpec((tm, ROUTE_W), lambda m: (m, 0)),
                   pl.BlockSpec((tm, ROUTE_W), lambda m: (m, 0))],
        out_shape=[jax.ShapeDtypeStruct((T, SLAB, LANE), f32), jax.ShapeDtypeStruct((T, ROUTE_W), i32),
                   jax.ShapeDtypeStruct((T, ROUTE_W), f32)],
        compiler_params=_cparams(("parallel",)),
    )(x, g.reshape(1, D), mod.arr, mod.arr, w_r, b_r)


TOK_BITS = 14


def _moe_plan(eid, T):
    A = T * TOP_K
    RT = _moe_rt(T)
    nt = A // RT + N_EXPERTS
    flat_e = eid[:, :TOP_K].reshape(-1)
    order = jnp.argsort(flat_e).astype(i32)
    counts = jnp.sum((flat_e[:, None] == jnp.arange(N_EXPERTS, dtype=i32)[None, :]).astype(i32), axis=0)
    start = jnp.cumsum(counts) - counts
    pcnt = (counts + RT - 1) // RT * RT
    pend = jnp.cumsum(pcnt)
    tile_e = jnp.minimum(jnp.sum((pend[None, :] <= (jnp.arange(nt, dtype=i32) * RT)[:, None]).astype(i32), axis=1),
                         N_EXPERTS - 1)
    off = (jnp.arange(nt, dtype=i32) * RT - (pend - pcnt)[tile_e])[:, None] + jnp.arange(RT, dtype=i32)[None, :]
    real = off < counts[tile_e][:, None]
    srt = start[tile_e][:, None] + jnp.minimum(off, counts[tile_e][:, None])
    a = order[jnp.minimum(srt, A - 1)]
    slot = jnp.arange(nt * RT, dtype=i32).reshape(nt, RT)
    tok = jnp.where(real, a // TOP_K, 0)
    dst = jnp.where(real, a, A + slot - srt)
    n_used = (pend[-1] // RT).astype(i32).reshape(1)
    return (tok | (dst << TOK_BITS)).reshape(-1), tile_e, n_used


def _experts(h2, packed, tile_e, n_used, l, w1, w3, w2):
    T = h2.shape[0]
    assert TOP_K == 2 and T < (1 << TOK_BITS)
    RT = _moe_rt(T)
    PITCH = SLAB_PITCH
    nt = tile_e.shape[0]
    prime_id = nt * RT
    out_rows = (prime_id + RT) // TOP_K
    HC, OC = 128, 256
    n_hc, n_oc = D_EXPERT // HC, D // OC
    g_per, s_per = RT // n_hc, RT // n_oc

    def body(tile_ref, nused_ref, slot_ref, h_hbm, h2d_hbm, w1_ref, w3_ref, w2_ref, o_hbm,
             x0, x1, o0, o1, w1b, w3b, w2b, gsem, ssem):
        j = pl.program_id(0)
        n_used = nused_ref[0]
        active = j < n_used

        def gather_row(x_ref, r, tok):
            return pltpu.make_async_copy(h_hbm.at[tok], x_ref.at[pl.ds(r * PITCH, SLAB), :], gsem)

        def scatter_row(o_ref, r, d):
            return pltpu.make_async_copy(o_ref.at[pl.ds(r * PITCH, SLAB), :],
                                         o_hbm.at[lax.shift_right_logical(d, 1), d & (TOP_K - 1)], ssem)

        def wait_rows(sem):
            pltpu.make_async_copy(h2d_hbm.at[pl.ds(0, RT * SLAB), :], x0.at[pl.ds(0, RT * SLAB), :], sem).wait()

        wait_gather = lambda: wait_rows(gsem)
        wait_scatter = lambda: wait_rows(ssem)

        tok_of = lambda s: s & ((1 << TOK_BITS) - 1)
        dst_of = lambda s: lax.shift_right_logical(s, TOK_BITS)

        @pl.when(j == 0)
        def _():
            o1[...] = jnp.zeros_like(o1)

            def one(r, _):
                gather_row(x0, r, tok_of(slot_ref[r])).start()
                return 0
            lax.fori_loop(0, RT, one, 0)

        @pl.when(active & ((j == 0) | (tile_ref[j] != tile_ref[jnp.maximum(j - 1, 0)])))
        def _():
            w1b[...] = w1_ref[...].astype(bf16)
            w3b[...] = w3_ref[...].astype(bf16)
            w2b[...] = w2_ref[...].astype(bf16)

        @pl.when(active & (j >= 1))
        def _():
            wait_scatter()

        def tile(x_cur, x_nxt, o_cur, o_prv):
            wait_gather()
            xb = jnp.concatenate([x_cur[pl.ds(s, RT, stride=PITCH), :] for s in range(SLAB)], axis=1).astype(bf16)
            nxt = jnp.minimum(j + 1, n_used - 1) * RT
            prv = jnp.maximum(j - 1, 0) * RT
            parts = []
            for c in range(n_hc):
                h1 = _dot(xb, w1b[:, c * HC:(c + 1) * HC])
                h3 = _dot(xb, w3b[:, c * HC:(c + 1) * HC])
                parts.append(((h1 * jax.nn.sigmoid(h1)) * h3).astype(bf16))
                for r in range(c * g_per, (c + 1) * g_per):
                    gather_row(x_nxt, r, tok_of(slot_ref[nxt + r])).start(priority=r % 2)
            hm = jnp.concatenate(parts, axis=1)
            for c in range(n_oc):
                res = _dot(hm, w2b[:, c * OC:(c + 1) * OC])
                for t in range(OC // LANE):
                    o_cur[pl.ds(c * (OC // LANE) + t, RT, stride=PITCH), :] = res[:, t * LANE:(t + 1) * LANE]
                for r in range(c * s_per, (c + 1) * s_per):
                    d = jnp.where(j == 0, prime_id + r, dst_of(slot_ref[prv + r]))
                    scatter_row(o_prv, r, d).start(priority=r % 2)

        def last(o_cur):
            wait_scatter()

            def one(r, _):
                scatter_row(o_cur, r, dst_of(slot_ref[j * RT + r])).start()
                return 0
            lax.fori_loop(0, RT, one, 0)
            wait_scatter()
            wait_gather()

        for par, (xc, xn, oc, op) in enumerate(((x0, x1, o0, o1), (x1, x0, o1, o0))):
            pl.when(active & (j % 2 == par))(lambda xc=xc, xn=xn, oc=oc, op=op: tile(xc, xn, oc, op))
        for par, oc in enumerate((o0, o1)):
            pl.when((j == n_used - 1) & (j % 2 == par))(lambda oc=oc: last(oc))

    grid_spec = pltpu.PrefetchScalarGridSpec(
        num_scalar_prefetch=3,
        grid=(nt,),
        in_specs=[pl.BlockSpec(memory_space=pl.ANY), pl.BlockSpec(memory_space=pl.ANY),
                  pl.BlockSpec((None, None, D, D_EXPERT), lambda j, te, *_: (l, te[j], 0, 0)),
                  pl.BlockSpec((None, None, D, D_EXPERT), lambda j, te, *_: (l, te[j], 0, 0)),
                  pl.BlockSpec((None, None, D_EXPERT, D), lambda j, te, *_: (l, te[j], 0, 0))],
        out_specs=pl.BlockSpec(memory_space=pl.ANY),
        scratch_shapes=[pltpu.VMEM((RT * PITCH, LANE), f32), pltpu.VMEM((RT * PITCH, LANE), f32),
                        pltpu.VMEM((RT * PITCH, LANE), f32), pltpu.VMEM((RT * PITCH, LANE), f32),
                        pltpu.VMEM((D, D_EXPERT), bf16), pltpu.VMEM((D, D_EXPERT), bf16),
                        pltpu.VMEM((D_EXPERT, D), bf16),
                        pltpu.SemaphoreType.DMA(()), pltpu.SemaphoreType.DMA(())],
    )
    return pl.pallas_call(
        body,
        grid_spec=grid_spec,
        out_shape=jax.ShapeDtypeStruct((out_rows, TOP_K, SLAB, LANE), f32),
        compiler_params=_cparams(("arbitrary",)),
    )(tile_e, n_used, packed, h2, h2.reshape(T * SLAB, LANE), w1, w3, w2)


def _moe_combine(x, o2, wgt, mod, norm_final):
    T = x.shape[0]
    tm = min(mod.tm, 512)

    def body(x_ref, o_ref, p_ref, m_ref, *rest):
        p = p_ref[...]
        row = lambda k: jnp.concatenate([o_ref[:, k, s, :] for s in range(SLAB)], axis=1)
        y = p[:, 0:1] * row(0)
        for k in range(1, TOP_K):
            y = y + p[:, k:k + 1] * row(k)
        xo = x_ref[...] + m_ref[...] * y
        if norm_final is None:
            rest[0][...] = xo
        else:
            rest[1][...] = _rms(xo, rest[0][...])

    ins = [x, o2, wgt, mod.arr]
    specs = [pl.BlockSpec((tm, D), lambda m: (m, 0)),
             pl.BlockSpec((tm, TOP_K, SLAB, LANE), lambda m: (m, 0, 0, 0)),
             pl.BlockSpec((tm, ROUTE_W), lambda m: (m, 0)), mod.row_spec(5, tm)]
    if norm_final is not None:
        ins.append(norm_final.reshape(1, D))
        specs.append(pl.BlockSpec((1, D), lambda m: (0, 0)))
    return pl.pallas_call(
        body,
        grid=(T // tm,),
        in_specs=specs,
        out_specs=pl.BlockSpec((tm, D), lambda m: (m, 0)),
        out_shape=jax.ShapeDtypeStruct((T, D), f32),
        compiler_params=_cparams(("parallel",)),
    )(*ins)


def _moe(x, mod, p, l, big, norm_final):
    pad = ROUTE_W - N_GROUPS - N_EXPERTS
    w_r = jnp.pad(jnp.concatenate([p['moe_w_group'], p['moe_w_router']], axis=1), ((0, 0), (0, pad)))
    b_r = jnp.pad(jnp.concatenate([p['moe_b_group'], p['moe_b_router']]), (0, pad)).reshape(1, ROUTE_W)
    h2, eid, wgt = _router(x, p['norm_ffn'], mod, w_r, b_r)
    packed, tile_e, n_used = _moe_plan(eid, x.shape[0])
    o2 = _experts(h2, packed, tile_e, n_used, l, big['moe_w1'], big['moe_w3'], big['moe_w2'])
    return _moe_combine(x, o2, wgt, mod, norm_final)


def _trunk(x, mods, states, pos0, layers, norm_final):
    B, L, _ = x.shape
    T = B * L
    tm = _row_tile(T)
    x = x.reshape(T, D)
    outs = ([], [], [], [], [])
    big = {k: layers[k] for k in _BIG}
    w_in_t = jnp.swapaxes(layers['w_in'], 1, 2)
    ret_all = rw_all = None
    for l in range(DEPTH):
        p = {name: arr[l] for name, arr in layers.items() if name not in _BIG + ('w_in',)}
        mod = _Mod(mods[l], L, tm)
        rp, sp = _rwkv_params(p), _s5_params(p)
        proj = _in_proj(x, layers['norm_mix'], l, mod, w_in_t)
        if states is None:
            y_ret, s_ret = _retention_seq(proj, B, L)
            y_rw, s_rw, shift = _rwkv_seq(proj, rp, B, L)
            y_s5, s_re, s_im = _s5_seq(proj, sp, B, L)
        else:
            st_ret, st_rw, st_shift, st_re, st_im = states
            y_ret, ret_all = _retention_step(proj, st_ret, l, ret_all, pos0)
            y_rw, rw_all = _rwkv_step(proj, st_shift[l], jnp.transpose(st_rw, (0, 2, 3, 4, 1)), l, rw_all, rp)
            s_ret = s_rw = None
            shift = proj[:, OFF_RW:OFF_RW + RWKV_PROJ]
            y_s5, s_re, s_im = _s5_step(proj, st_re[l], st_im[l], sp)
        z = _glu_proj(y_s5, l, big['s5_w_glu'])
        merged = _merge_proj(y_ret, y_rw, z, l, big['ret_w_o'], big['rwkv_w_o'], big['s5_w_o'], proj)
        x = _out_proj(merged, l, big['w_out'], x, mod)
        x = _moe(x, mod, p, l, big, norm_final if l == DEPTH - 1 else None)
        for lst, val in zip(outs, (s_ret, s_rw, shift, s_re, s_im)):
            lst.append(val)
    stacked = [jnp.stack(o) if o[0] is not None else None for o in outs]
    if states is not None:
        stacked[0], stacked[1] = ret_all, jnp.transpose(rw_all, (0, 4, 1, 2, 3))
    return x.reshape(B, L, D), stacked


_BIG = ('ret_w_o', 'rwkv_w_o', 's5_w_glu', 's5_w_o', 'w_out', 'moe_w1', 'moe_w3', 'moe_w2')


def kernel(x_prompt, x_sample, state_ret, state_rwkv, state_shift, state_s5_re, state_s5_im,
           c_prompt, c_sample, norm_mix, norm_ffn, w_ada, b_ada, w_in, ret_w_o, rwkv_mu, rwkv_w0,
           rwkv_w2, rwkv_a0, rwkv_a2, rwkv_g2, rwkv_k_k, rwkv_k_a, rwkv_r_k, rwkv_ln_w, rwkv_ln_b,
           rwkv_w_o, s5_a_re, s5_a_im, s5_b_re, s5_b_im, s5_c_re, s5_c_im, s5_d, s5_log_dt, s5_w_glu,
           s5_w_o, w_out, moe_w_group, moe_b_group, moe_w_router, moe_b_router, moe_w1, moe_w3, moe_w2,
           norm_final):
    layers = {
        'norm_mix': norm_mix, 'norm_ffn': norm_ffn, 'w_in': w_in,
        'ret_w_o': ret_w_o, 'rwkv_mu': rwkv_mu, 'rwkv_w0': rwkv_w0, 'rwkv_w2': rwkv_w2,
        'rwkv_a0': rwkv_a0, 'rwkv_a2': rwkv_a2, 'rwkv_g2': rwkv_g2, 'rwkv_k_k': rwkv_k_k,
        'rwkv_k_a': rwkv_k_a, 'rwkv_r_k': rwkv_r_k, 'rwkv_ln_w': rwkv_ln_w, 'rwkv_ln_b': rwkv_ln_b,
        'rwkv_w_o': rwkv_w_o, 's5_a_re': s5_a_re, 's5_a_im': s5_a_im, 's5_b_re': s5_b_re,
        's5_b_im': s5_b_im, 's5_c_re': s5_c_re, 's5_c_im': s5_c_im, 's5_d': s5_d,
        's5_log_dt': s5_log_dt, 's5_w_glu': s5_w_glu, 's5_w_o': s5_w_o, 'w_out': w_out,
        'moe_w_group': moe_w_group, 'moe_b_group': moe_b_group, 'moe_w_router': moe_w_router,
        'moe_b_router': moe_b_router, 'moe_w1': moe_w1, 'moe_w3': moe_w3, 'moe_w2': moe_w2,
    }
    Bp, Bs = x_prompt.shape[0], x_sample.shape[0]
    s_off = -(-Bp // 8) * 8
    c_all = jnp.concatenate([c_prompt, jnp.zeros((s_off - Bp, D), f32), c_sample], axis=0)
    mod_all = _adaln(c_all, w_ada, b_ada)
    y_prompt, (ret_p, rwkv_p, shift_p, s5re_p, s5im_p) = _trunk(
        x_prompt, mod_all[:, :Bp], None, 0.0, layers, norm_final)
    y_sample, (ret_s, rwkv_s, shift_s, s5re_s, s5im_s) = _trunk(
        x_sample, mod_all[:, s_off:s_off + Bs], (state_ret, state_rwkv, state_shift, state_s5_re, state_s5_im),
        float(PAST_LEN), layers, norm_final)
    return (y_prompt, y_sample, ret_p, ret_s, rwkv_p, rwkv_s, shift_p, shift_s, s5re_p, s5re_s, s5im_p, s5im_s)
```

```python
import jax
import jax.numpy as jnp
from jax import lax
from jax.experimental import pallas as pl
from jax.experimental.pallas import tpu as pltpu

f32 = jnp.float32
bf16 = jnp.bfloat16
i32 = jnp.int32

D = 2048
DEPTH = 2
PAST_LEN = 16384
RET_W, RET_H, RET_DK, RET_DV, RET_CHUNK = 1024, 4, 256, 256, 128
RET_GN_EPS = 1e-6
ROPE_BASE = 10000.0
RWKV_W, RWKV_N, RWKV_H = 1024, 64, 16
DECAY_LORA, AAA_LORA, GATE_LORA = 64, 64, 160
RWKV_PROJ = 3 * RWKV_W + DECAY_LORA + AAA_LORA + GATE_LORA
RWKV_LN_EPS = 64e-5
S5_W, S5_GC, S5_G, S5_P = 1024, 16, 64, 64
N_MOD = 6
RMS_EPS = 1e-6
N_GROUPS, EPG, N_EXPERTS, TOP_K, D_EXPERT = 4, 8, 32, 2, 512
IN_W = 4 * RET_W + RWKV_PROJ + S5_W + 3 * D
OFF_Q, OFF_K, OFF_V, OFF_G, OFF_RW = 0, 1024, 2048, 3072, 4096
OFF_U = OFF_RW + RWKV_PROJ
OFF_GATE = OFF_U + S5_W
LANE = 128
PROJ_W = ((IN_W + LANE - 1) // LANE) * LANE
VMEM_LIMIT = 56 * 1024 * 1024


def _cparams(sem):
    return pltpu.CompilerParams(dimension_semantics=sem, vmem_limit_bytes=VMEM_LIMIT)


def _dot(a, b):
    return jnp.dot(a, b, preferred_element_type=f32)


def _dot_nt(a, b):
    return lax.dot_general(a, b, (((1,), (1,)), ((), ())), preferred_element_type=f32)


def _dot_tn(a, b):
    return lax.dot_general(a, b, (((0,), (0,)), ((), ())), preferred_element_type=f32)


def _rms(x, g):
    return x * lax.rsqrt(jnp.mean(x * x, axis=-1, keepdims=True) + RMS_EPS) * g


def _head_norm(y, eps):
    mu = jnp.mean(y, axis=-1, keepdims=True)
    yc = y - mu
    return yc * lax.rsqrt(jnp.mean(yc * yc, axis=-1, keepdims=True) + eps)


def _row_tile(T):
    return 1024 if T >= 1024 else T


class _Mod:
    def __init__(self, mod, L, tm):
        self.L, self.tm = L, tm
        self.per_token = L == 1
        self.arr = mod if self.per_token else mod.reshape(mod.shape[0], 1, N_MOD * D)

    def spec(self, j, tn, col_of):
        nb = D // tn
        if self.per_token:
            return pl.BlockSpec((self.tm, tn), lambda m, n: (m, j * nb + col_of(n)))
        L, tm = self.L, self.tm
        return pl.BlockSpec((None, 1, tn), lambda m, n: ((m * tm) // L, 0, j * nb + col_of(n)))

    def row_spec(self, j, tm, remap=lambda m: m):
        if self.per_token:
            return pl.BlockSpec((tm, D), lambda m: (remap(m), j))
        L = self.L
        return pl.BlockSpec((None, 1, D), lambda m: ((remap(m) * tm) // L, 0, j))


def _fused_mm(x_ops, w_ops, e_ops, pre, post, *, grid, out_specs, out_shape, cache_shapes):
    nx, nw, ne = len(x_ops), len(w_ops), len(e_ops)
    n_out = len(out_shape)

    def body(*refs):
        x_refs = refs[:nx]
        w_refs = refs[nx:nx + nw]
        e_refs = refs[nx + nw:nx + nw + ne]
        o_refs = refs[nx + nw + ne:nx + nw + ne + n_out]
        caches = refs[nx + nw + ne + n_out:]
        if cache_shapes:
            @pl.when(pl.program_id(1) == 0)
            def _():
                for i in range(nx):
                    caches[i][...] = pre(i, x_refs[i], e_refs).astype(bf16)
            lhs = [c[...] for c in caches]
        else:
            lhs = [x[...] for x in x_refs]
        prods = [(_dot_nt if len(w_ops[j]) > 3 and w_ops[j][3] else _dot)(
            lhs[w_ops[j][2]], w_refs[j][...].astype(bf16)) for j in range(nw)]
        for o_ref, o in zip(o_refs, post(prods, e_refs)):
            o_ref[...] = o.astype(o_ref.dtype)

    return pl.pallas_call(
        body,
        grid=grid,
        in_specs=[s for _, s in x_ops] + [w[1] for w in w_ops] + [s for _, s in e_ops],
        out_specs=out_specs,
        out_shape=out_shape,
        scratch_shapes=[pltpu.VMEM(s, bf16) for s in cache_shapes],
        compiler_params=_cparams(("parallel", "arbitrary")),
    )(*[a for a, _ in x_ops], *[w[0] for w in w_ops], *[a for a, _ in e_ops])


def _adaln(c_all, w_ada, b_ada):
    R = c_all.shape[0]
    tn = 1024

    def pre(i, x_ref, e_refs):
        c = x_ref[...]
        return c * jax.nn.sigmoid(c)

    def post(prods, e_refs):
        return (prods[0] + e_refs[0][...],)

    (out,) = _fused_mm(
        [(c_all, pl.BlockSpec((R, D), lambda l, n: (0, 0)))],
        [(w_ada, pl.BlockSpec((None, D, tn), lambda l, n: (l, 0, n)), 0)],
        [(b_ada.reshape(DEPTH, 1, N_MOD * D), pl.BlockSpec((None, 1, tn), lambda l, n: (l, 0, n)))],
        pre, post,
        grid=(DEPTH, N_MOD * D // tn),
        out_specs=[pl.BlockSpec((None, R, tn), lambda l, n: (l, 0, n))],
        out_shape=[jax.ShapeDtypeStruct((DEPTH, R, N_MOD * D), f32)],
        cache_shapes=[(R, D)],
    )
    return out


def _in_proj(x, g, l, mod, w_in_t):
    T = x.shape[0]
    tm, tn = mod.tm, 1024
    tr = min(tm, 512)

    def norm_body(x_ref, g_ref, shift_ref, scale_ref, h_ref):
        h_ref[...] = (_rms(x_ref[...], g_ref[...]) * (1.0 + scale_ref[...]) + shift_ref[...]).astype(bf16)

    h = pl.pallas_call(
        norm_body,
        grid=(T // tr,),
        in_specs=[pl.BlockSpec((tr, D), lambda m: (m, 0)),
                  pl.BlockSpec((None, 1, D), lambda m: (l, 0, 0)),
                  mod.row_spec(0, tr), mod.row_spec(1, tr)],
        out_specs=pl.BlockSpec((tr, D), lambda m: (m, 0)),
        out_shape=jax.ShapeDtypeStruct((T, D), bf16),
        compiler_params=_cparams(("parallel",)),
    )(x, g.reshape(DEPTH, 1, D), mod.arr, mod.arr)

    def mm_body(h_ref, w_ref, o_ref, wb):
        @pl.when(pl.program_id(1) == 0)
        def _():
            wb[...] = w_ref[...].astype(bf16)

        col = pl.program_id(0) * tn + lax.broadcasted_iota(i32, (tm, tn), 1)
        o_ref[...] = jnp.where(col < IN_W, _dot_nt(h_ref[...], wb[...]), 0.0)

    return pl.pallas_call(
        mm_body,
        grid=(pl.cdiv(PROJ_W, tn), T // tm),
        in_specs=[pl.BlockSpec((tm, D), lambda n, m: (m, 0)),
                  pl.BlockSpec((None, tn, D), lambda n, m: (l, n, 0))],
        out_specs=pl.BlockSpec((tm, tn), lambda n, m: (m, n)),
        out_shape=jax.ShapeDtypeStruct((T, PROJ_W), f32),
        scratch_shapes=[pltpu.VMEM((tn, D), bf16)],
        compiler_params=_cparams(("parallel", "arbitrary")),
    )(h, w_in_t)


def _glu_proj(yg, l, w_glu):
    T = yg.shape[0]
    tm, tn = _row_tile(T), 512
    nb = S5_W // tn

    def post(prods, e_refs):
        return (prods[0] * jax.nn.sigmoid(prods[1]),)

    (out,) = _fused_mm(
        [(yg, pl.BlockSpec((tm, S5_W), lambda m, n: (m, 0)))],
        [(w_glu, pl.BlockSpec((None, S5_W, tn), lambda m, n: (l, 0, n)), 0),
         (w_glu, pl.BlockSpec((None, S5_W, tn), lambda m, n: (l, 0, nb + n)), 0)],
        [], None, post,
        grid=(T // tm, nb),
        out_specs=[pl.BlockSpec((tm, tn), lambda m, n: (m, n))],
        out_shape=[jax.ShapeDtypeStruct((T, S5_W), bf16)],
        cache_shapes=[],
    )
    return out


def _merge_proj(y_ret, y_rw, y_s5, l, w_ret, w_rw, w_s5, proj):
    T = y_ret.shape[0]
    tm, tn = _row_tile(T), 512
    lead = OFF_GATE % LANE
    base = OFF_GATE - lead

    def gate_spec(i):
        return pl.BlockSpec((pl.Element(tm), pl.Element(tn + LANE)),
                            lambda m, n: (pl.multiple_of(m * tm, tm), pl.multiple_of(base + i * D + n * tn, LANE)))

    def post(prods, e_refs):
        acc = None
        for p, e in zip(prods, e_refs):
            t = jax.nn.sigmoid(e[:, lead:lead + tn]) * p
            acc = t if acc is None else acc + t
        return (acc,)

    xspec = pl.BlockSpec((tm, RET_W), lambda m, n: (m, 0))
    wspec = pl.BlockSpec((None, RET_W, tn), lambda m, n: (l, 0, n))
    (out,) = _fused_mm(
        [(y_ret, xspec), (y_rw, xspec), (y_s5, xspec)],
        [(w_ret, wspec, 0), (w_rw, wspec, 1), (w_s5, wspec, 2)],
        [(proj, gate_spec(0)), (proj, gate_spec(1)), (proj, gate_spec(2))],
        None, post,
        grid=(T // tm, D // tn),
        out_specs=[pl.BlockSpec((tm, tn), lambda m, n: (m, n))],
        out_shape=[jax.ShapeDtypeStruct((T, D), bf16)],
        cache_shapes=[],
    )
    return out


def _out_proj(merged, l, w_out, x, mod):
    T = x.shape[0]
    tm, tn = mod.tm, 512

    def post(prods, e_refs):
        x_ref, m_ref = e_refs
        return (x_ref[...] + m_ref[...] * prods[0],)

    (out,) = _fused_mm(
        [(merged, pl.BlockSpec((tm, D), lambda m, n: (m, 0)))],
        [(w_out, pl.BlockSpec((None, D, tn), lambda m, n: (l, 0, n)), 0)],
        [(x, pl.BlockSpec((tm, tn), lambda m, n: (m, n))), (mod.arr, mod.spec(2, tn, lambda n: n))],
        None, post,
        grid=(T // tm, D // tn),
        out_specs=[pl.BlockSpec((tm, tn), lambda m, n: (m, n))],
        out_shape=[jax.ShapeDtypeStruct((T, D), f32)],
        cache_shapes=[],
    )
    return out


def _ret_consts(L, pos0):
    C = RET_CHUNK if L % RET_CHUNK == 0 else L
    H = RET_H
    log_g = jnp.log1p(-jnp.exp2(-5.0 - jnp.arange(H, dtype=f32)))
    i = jnp.arange(C, dtype=f32)
    diff = i[:, None] - i[None, :]
    causal = diff >= 0
    dmask = jnp.where(causal, jnp.exp(jnp.where(causal, diff, 0.0)[None] * log_g[:, None, None]), 0.0)
    kdec = jnp.exp((C - 1.0 - i)[:, None] * log_g[None, :])
    qdec = jnp.exp((i + 1.0)[:, None] * log_g[None, :])
    g_chunk = jnp.exp(C * log_g)
    half = RET_DK // 2
    inv = ROPE_BASE ** (-jnp.arange(half, dtype=f32) / half)
    pos = pos0 + jnp.arange(L, dtype=f32)
    ang = pos[:, None] * inv[None, :]
    return C, dmask, kdec, qdec, g_chunk, jnp.cos(ang), jnp.sin(ang)


def _rotary(x, cos, sin):
    half = RET_DK // 2
    x1, x2 = x[..., :half], x[..., half:]
    return jnp.concatenate([x1 * cos - x2 * sin, x1 * sin + x2 * cos], axis=-1)


def _retention_seq(proj, B, L):
    C, dmask, kdec, qdec, g_chunk, cos, sin = _ret_consts(L, 0.0)
    H, dk = RET_H, RET_DK
    n = L // C
    kdec_f = jnp.broadcast_to(kdec.T[:, :, None], (H, C, dk))
    qdec_f = jnp.broadcast_to(qdec.T[:, :, None], (H, C, dk))
    gch_f = jnp.broadcast_to(g_chunk[:, None, None], (H, 8, dk))
    proj3 = proj.reshape(B, L, PROJ_W)

    def body(q_ref, k_ref, v_ref, g_ref, cos_ref, sin_ref, dm_ref, kd_ref, qd_ref, gc_ref, y_ref, s_ref, st):
        c = pl.program_id(1)

        @pl.when(c == 0)
        def _():
            st[...] = jnp.zeros_like(st)

        cs, sn = cos_ref[...], sin_ref[...]
        for h in range(H):
            sl = slice(h * dk, (h + 1) * dk)
            q = _rotary(q_ref[:, sl], cs, sn)
            k = _rotary(k_ref[:, sl], cs, sn) * (dk ** -0.5)
            vb = v_ref[:, sl].astype(bf16)
            s0 = st[h]
            scores = _dot_nt(q.astype(bf16), k.astype(bf16)) * dm_ref[h]
            o = _dot(scores.astype(bf16), vb)
            o = o + _dot((q * qd_ref[h]).astype(bf16), s0.astype(bf16))
            kv = _dot_tn((k * kd_ref[h]).astype(bf16), vb)
            st[h] = s0 * gc_ref[h, 0:1, :] + kv
            g = g_ref[:, sl]
            y_ref[:, sl] = (g * jax.nn.sigmoid(g) * _head_norm(o, RET_GN_EPS)).astype(bf16)

        @pl.when(c == n - 1)
        def _():
            s_ref[...] = st[...]

    def seg(off):
        return pl.BlockSpec((None, C, RET_W), lambda b, c: (b, c, off // RET_W))

    const3 = lambda shp: pl.BlockSpec(shp, lambda b, c: (0, 0, 0))
    y, s = pl.pallas_call(
        body,
        grid=(B, n),
        in_specs=[seg(OFF_Q), seg(OFF_K), seg(OFF_V), seg(OFF_G),
                  pl.BlockSpec((C, dk // 2), lambda b, c: (c, 0)),
                  pl.BlockSpec((C, dk // 2), lambda b, c: (c, 0)),
                  const3((H, C, C)), const3((H, C, dk)), const3((H, C, dk)), const3((H, 8, dk))],
        out_specs=[pl.BlockSpec((None, C, RET_W), lambda b, c: (b, c, 0)),
                   pl.BlockSpec((None, H, dk, RET_DV), lambda b, c: (b, 0, 0, 0))],
        out_shape=[jax.ShapeDtypeStruct((B, L, RET_W), bf16),
                   jax.ShapeDtypeStruct((B, H, dk, RET_DV), f32)],
        scratch_shapes=[pltpu.VMEM((H, dk, RET_DV), f32)],
        compiler_params=_cparams(("parallel", "arbitrary")),
    )(proj3, proj3, proj3, proj3, cos, sin, dmask, kdec_f, qdec_f, gch_f)
    return y.reshape(B * L, RET_W), s


STEP_TB = 16


def _layer_grid(l, buf, inner):
    if buf is not None:
        return inner, (lambda fn: (lambda *ix: fn(l, ix, ix)))
    assert l == 0
    last = tuple(n - 1 for n in inner)

    def wrap(fn):
        def index_map(d, *ix):
            parked = tuple(jnp.where(d == l, i, z) for i, z in zip(ix, last))
            return fn(d, ix, parked)
        return index_map
    return (DEPTH,) + inner, wrap


def _retention_step(proj, s_all, l, buf, pos0):
    B = proj.shape[0]
    _, _, _, _, g_chunk, cos, sin = _ret_consts(1, pos0)
    H, dk = RET_H, RET_DK
    gch = jnp.broadcast_to(g_chunk[:, None, None], (H, 8, dk))
    tb = STEP_TB
    grid, wrap = _layer_grid(l, buf, (B // tb, H))

    def body(q_ref, k_ref, v_ref, g_ref, cos_ref, sin_ref, gc_ref, s_ref, *rest):
        y_ref, so_ref = rest[-2:]

        def update():
            cs, sn = cos_ref[...], sin_ref[...]
            q = _rotary(q_ref[...], cs, sn)
            k = _rotary(k_ref[...], cs, sn) * (dk ** -0.5)
            v = v_ref[...]
            s1 = s_ref[...] * gc_ref[0:1, :][None] + k[:, :, None] * v[:, None, :]
            so_ref[...] = s1
            o = jnp.sum(q[:, :, None] * s1, axis=1)
            g = g_ref[...]
            y_ref[...] = (g * jax.nn.sigmoid(g) * _head_norm(o, RET_GN_EPS)).astype(bf16)

        if buf is not None:
            update()
        else:
            pl.when(pl.program_id(0) == l)(update)

            @pl.when(pl.program_id(0) != l)
            def _():
                so_ref[...] = jnp.zeros_like(so_ref)

    def seg(off):
        return pl.BlockSpec((tb, dk), wrap(lambda d, ix, pk: (pk[0], off // dk + pk[1])))

    const = lambda d, ix, pk: (0, 0)
    in_specs = [seg(OFF_Q), seg(OFF_K), seg(OFF_V), seg(OFF_G),
                pl.BlockSpec((1, dk // 2), wrap(const)), pl.BlockSpec((1, dk // 2), wrap(const)),
                pl.BlockSpec((None, 8, dk), wrap(lambda d, ix, pk: (pk[1], 0, 0))),
                pl.BlockSpec((None, tb, None, dk, RET_DV), wrap(lambda d, ix, pk: (l, pk[0], pk[1], 0, 0)))]
    args = [proj, proj, proj, proj, cos, sin, gch, s_all]
    aliases = {}
    if buf is not None:
        in_specs.append(pl.BlockSpec(memory_space=pl.ANY))
        args.append(buf)
        aliases = {len(args) - 1: 1}
    return pl.pallas_call(
        body,
        grid=grid,
        in_specs=in_specs,
        out_specs=[pl.BlockSpec((tb, dk), wrap(lambda d, ix, pk: pk)),
                   pl.BlockSpec((None, tb, None, dk, RET_DV), wrap(lambda d, ix, pk: (d, ix[0], ix[1], 0, 0)))],
        out_shape=[jax.ShapeDtypeStruct((B, RET_W), bf16),
                   jax.ShapeDtypeStruct((DEPTH, B, H, dk, RET_DV), f32)],
        input_output_aliases=aliases,
        compiler_params=_cparams(("arbitrary",) * len(grid)),
    )(*args)


RW_C = 64
RW_Q = 4
RW_NQ = RWKV_H // RW_Q
RW_GROUP = 8
RW_BLK =((RWKV_PROJ + LANE - 1) // LANE) * LANE
_RW_PKEYS = ('mu', 'w0', 'w2', 'a0', 'a2', 'g2', 'k_k', 'k_a', 'r_k', 'ln_w', 'ln_b')


def _rwkv_params(p):
    return dict(
        mu=jnp.pad(p['rwkv_mu'], (0, RW_BLK - RWKV_PROJ)).reshape(1, RW_BLK),
        w0=p['rwkv_w0'].reshape(1, RWKV_W), w2=p['rwkv_w2'],
        a0=p['rwkv_a0'].reshape(1, RWKV_W), a2=p['rwkv_a2'], g2=p['rwkv_g2'],
        k_k=p['rwkv_k_k'].reshape(1, RWKV_W), k_a=p['rwkv_k_a'].reshape(1, RWKV_W),
        r_k=p['rwkv_r_k'].reshape(1, RWKV_W),
        ln_w=p['rwkv_ln_w'].reshape(1, RWKV_W), ln_b=p['rwkv_ln_b'].reshape(1, RWKV_W))


def _split_bf16(x, terms):
    out = []
    for _ in range(terms - 1):
        hi = x.astype(bf16)
        out.append(hi)
        x = x - hi.astype(f32)
    out.append(x.astype(bf16))
    return out


def _head_sum(x):
    QW = RW_Q * RWKV_N
    r = lax.broadcasted_iota(i32, (QW, QW), 0) // RWKV_N
    c = lax.broadcasted_iota(i32, (QW, QW), 1) // RWKV_N
    ones = (r == c).astype(bf16)
    parts = _split_bf16(x, 2)
    outs = []
    for q in range(RW_NQ):
        sl = slice(q * QW, (q + 1) * QW)
        outs.append(_dot(parts[0][:, sl], ones) + _dot(parts[1][:, sl], ones))
    return jnp.concatenate(outs, axis=1)


def _rwkv_mix(rw, prev, pr):
    m = rw + (prev - rw) * pr['mu'][...]
    r = m[:, 0:RWKV_W]
    k = m[:, RWKV_W:2 * RWKV_W]
    v = m[:, 2 * RWKV_W:3 * RWKV_W]
    o = 3 * RWKV_W
    xw = m[:, o:o + DECAY_LORA]
    xa = m[:, o + DECAY_LORA:o + DECAY_LORA + AAA_LORA]
    xg = m[:, o + DECAY_LORA + AAA_LORA:o + DECAY_LORA + AAA_LORA + GATE_LORA]
    w_log = -jax.nn.softplus(-(pr['w0'][...] + _dot(jnp.tanh(xw).astype(bf16), pr['w2'][...].astype(bf16)))) - 0.5
    lw = -jnp.exp(w_log)
    a = jax.nn.sigmoid(pr['a0'][...] + _dot(xa.astype(bf16), pr['a2'][...].astype(bf16)))
    g = _dot(jax.nn.sigmoid(xg).astype(bf16), pr['g2'][...].astype(bf16))
    kk = k * pr['k_k'][...]
    kk = kk / jnp.maximum(jnp.sqrt(_head_sum(kk * kk)), 1e-12)
    kf = k * (1.0 + (a - 1.0) * pr['k_a'][...])
    return r, lw, kf, v, kk, a, g


def _rwkv_out(y, r, kf, v, g, pr):
    yc = y - _head_sum(y) * (1.0 / RWKV_N)
    yn = yc * lax.rsqrt(_head_sum(yc * yc) * (1.0 / RWKV_N) + RWKV_LN_EPS)
    yn = yn * pr['ln_w'][...] + pr['ln_b'][...]
    yn = yn + _head_sum(r * kf * pr['r_k'][...]) * v
    return yn * g


def _rwkv_seq(proj, rp, B, L):
    C = RW_C
    assert L % C == 0 and C == RWKV_N
    n = L // C
    QW = RW_Q * RWKV_N
    BC = B * C
    proj3 = proj.reshape(B, L, PROJ_W)

    def body(rw_ref, *refs):
        pr = dict(zip(_RW_PKEYS, refs[:len(_RW_PKEYS)]))
        y_ref, s_ref, sh_ref, st, carry = refs[len(_RW_PKEYS):]
        c = pl.program_id(0)

        @pl.when(c == 0)
        def _():
            st[...] = jnp.zeros_like(st)
            carry[...] = jnp.zeros_like(carry)

        rw = rw_ref[...].reshape(BC, RW_BLK)
        rolled = pltpu.roll(rw, 1, 0)
        row = lax.broadcasted_iota(i32, (C, RW_BLK), 0)
        prev = jnp.concatenate(
            [jnp.where(row == 0, carry[b, 0:1, :], rolled[b * C:(b + 1) * C]) for b in range(B)], axis=0)
        for b in range(B):
            carry[b, 0:1, :] = rw[(b + 1) * C - 1:(b + 1) * C, :]
        r, lw, kf, v, kk, a, g = _rwkv_mix(rw, prev, pr)

        ti = lax.broadcasted_iota(i32, (BC, BC), 0)
        si = lax.broadcasted_iota(i32, (BC, BC), 1)
        tril = ((si <= ti) & ((si // C) == (ti // C))).astype(bf16)
        lg = sum(_dot(tril, part) for part in _split_bf16(lw, 3))
        lgc = jnp.concatenate(
            [jnp.broadcast_to(lg[(b + 1) * C - 1:(b + 1) * C, :], (C, RWKV_W)) for b in range(B)], axis=0)
        e_neg = jnp.exp(-lg)
        e_rem = jnp.exp(lgc - lg)
        at = kk * jnp.exp(lg - lw)
        ka = kk * a
        bt = ka * e_neg
        kt = kf * e_neg
        rt = r * jnp.exp(lg)
        bh = ka * e_rem
        kh = kf * e_rem
        gcr = jnp.exp(lgc)

        rr = lax.broadcasted_iota(i32, (RW_Q * C, QW), 0)
        ll = lax.broadcasted_iota(i32, (RW_Q * C, QW), 1)
        blockmask = (rr // C) == (ll // RWKV_N)
        tt = lax.broadcasted_iota(i32, (C, QW), 0)
        ss = lax.broadcasted_iota(i32, (C, QW), 1) % C
        strict = ss < tt
        incl = ss <= tt
        eye = (ss == tt).astype(f32)

        def bd(x):
            return jnp.where(blockmask, jnp.concatenate([x] * RW_Q, axis=0), 0.0).astype(bf16)

        ys = [[None] * RW_NQ for _ in range(B)]
        chains = [(b, q) for b in range(B) for q in range(RW_NQ)]
        for g0 in range(0, len(chains), RW_GROUP):
            grp = chains[g0:g0 + RW_GROUP]
            idx = [(slice(b * C, (b + 1) * C), slice(q * QW, (q + 1) * QW)) for b, q in grp]
            each = lambda fn: [fn(i) for i in range(len(grp))]
            vq = each(lambda i: v[idx[i]])
            ar = each(lambda i: jnp.concatenate([at[idx[i]], rt[idx[i]]], axis=0).astype(bf16))
            big = each(lambda i: _dot_nt(ar[i], jnp.concatenate([bd(bt[idx[i]]), bd(kt[idx[i]])], axis=0)))
            s0 = each(lambda i: st[grp[i]])
            asrs = each(lambda i: _dot_nt(ar[i], s0[i].astype(bf16)))
            nmat = each(lambda i: jnp.where(strict, big[i][:C, :QW], 0.0))
            akm = each(lambda i: jnp.where(strict, big[i][:C, QW:], 0.0).astype(bf16))
            rbk = each(lambda i: jnp.concatenate([jnp.where(incl, big[i][C:, :QW], 0.0),
                                                  jnp.where(incl, big[i][C:, QW:], 0.0)], axis=1).astype(bf16))
            tm = each(lambda i: eye - nmat[i])
            pw = each(lambda i: _dot(nmat[i].astype(bf16), bd(nmat[i])))
            lvl = 2
            while lvl < C:
                res = each(lambda i: _dot(jnp.concatenate([tm[i], pw[i]], axis=0).astype(bf16), bd(pw[i])))
                tm = each(lambda i: tm[i] + res[i][:C])
                pw = each(lambda i: res[i][C:])
                lvl *= 2
            vbd = each(lambda i: bd(vq[i]))
            rhs = each(lambda i: -(asrs[i][:C] + _dot(akm[i], vbd[i])))
            u = each(lambda i: _dot(tm[i].astype(bf16), bd(rhs[i])))
            y = each(lambda i: asrs[i][C:] + _dot(rbk[i], jnp.concatenate([bd(u[i]), vbd[i]], axis=0)))
            upd = each(lambda i: _dot_tn(jnp.concatenate([u[i], vq[i]], axis=0).astype(bf16),
                                         jnp.concatenate([bh[idx[i]], kh[idx[i]]], axis=0).astype(bf16)))
            for i, (b, q) in enumerate(grp):
                st[b, q] = s0[i] * gcr[b * C:b * C + 1, idx[i][1]] + jnp.where(blockmask, upd[i], 0.0)
                ys[b][q] = y[i]

        y = jnp.concatenate([jnp.concatenate(yb, axis=1) for yb in ys], axis=0)
        y_ref[...] = _rwkv_out(y, r, kf, v, g, pr).astype(bf16).reshape(B, C, RWKV_W)

        @pl.when(c == n - 1)
        def _():
            for b in range(B):
                for q in range(RW_NQ):
                    for h in range(RW_Q):
                        hs = slice(h * RWKV_N, (h + 1) * RWKV_N)
                        s_ref[b, q * RW_Q + h] = st[b, q, hs, hs]
                sh_ref[b] = rw[(b + 1) * C - 1:(b + 1) * C, 0:RWKV_PROJ]

    pspecs = [pl.BlockSpec(rp[k].shape, lambda c: (0, 0)) for k in _RW_PKEYS]
    y, s, sh = pl.pallas_call(
        body,
        grid=(n,),
        in_specs=[pl.BlockSpec((pl.Element(B), pl.Element(C), pl.Element(RW_BLK)),
                               lambda c: (0, pl.multiple_of(c * C, C), OFF_RW))] + pspecs,
        out_specs=[pl.BlockSpec((B, C, RWKV_W), lambda c: (0, c, 0)),
                   pl.BlockSpec((B, RWKV_H, RWKV_N, RWKV_N), lambda c: (0, 0, 0, 0)),
                   pl.BlockSpec((B, 1, RWKV_PROJ), lambda c: (0, 0, 0))],
        out_shape=[jax.ShapeDtypeStruct((B, L, RWKV_W), bf16),
                   jax.ShapeDtypeStruct((B, RWKV_H, RWKV_N, RWKV_N), f32),
                   jax.ShapeDtypeStruct((B, 1, RWKV_PROJ), f32)],
        scratch_shapes=[pltpu.VMEM((B, RW_NQ, QW, QW), f32), pltpu.VMEM((B, 8, RW_BLK), f32)],
        compiler_params=_cparams(("arbitrary",)),
    )(proj3, *[rp[k] for k in _RW_PKEYS])
    return y.reshape(B * L, RWKV_W), s, sh.reshape(B, RWKV_PROJ)


def _rwkv_step(proj, shift, s_t, l, buf, rp):
    B = proj.shape[0]
    N = RWKV_N
    shift_p = jnp.pad(shift, ((0, 0), (0, RW_BLK - RWKV_PROJ)))
    grid, wrap = _layer_grid(l, buf, (RWKV_H,))
    npk = len(_RW_PKEYS)
    vec_names = ('r', 'w', 'kf', 'v', 'kk', 'ka')

    def body(rw_ref, sh_ref, s_ref, *refs):
        pr = dict(zip(_RW_PKEYS, refs[:npk]))
        y_ref, so_ref = refs[-9:-7]
        vt = dict(zip(vec_names, refs[-7:-1]))
        yt = refs[-1]
        h = pl.program_id(len(grid) - 1)

        def update():
            @pl.when(h == 0)
            def _():
                r, lw, kf, v, kk, a, _ = _rwkv_mix(rw_ref[...], sh_ref[...], pr)
                for name, val in zip(vec_names, (r, jnp.exp(lw), kf, v, kk, kk * a)):
                    vt[name][...] = val.T

            hs = pl.ds(pl.multiple_of(h * N, N), N)
            s = s_ref[...]
            kk_h = vt['kk'][hs, :]
            sa = jnp.sum(s * (-kk_h)[None], axis=1, keepdims=True)
            s1 = s * vt['w'][hs, :][None] + sa * vt['ka'][hs, :][None] + vt['v'][hs, :][:, None, :] * vt['kf'][hs, :][None]
            so_ref[...] = s1
            yt[hs, :] = jnp.sum(s1 * vt['r'][hs, :][None], axis=1)

            @pl.when(h == RWKV_H - 1)
            def _():
                r, lw, kf, v, kk, a, g = _rwkv_mix(rw_ref[...], sh_ref[...], pr)
                y_ref[...] = _rwkv_out(yt[...].T, r, kf, v, g, pr).astype(bf16)

        if buf is not None:
            update()
        else:
            pl.when(pl.program_id(0) == l)(update)

            @pl.when(pl.program_id(0) != l)
            def _():
                so_ref[...] = jnp.zeros_like(so_ref)

    const = lambda d, ix, pk: (0, 0)
    in_specs = [pl.BlockSpec((pl.Element(B), pl.Element(RW_BLK)), wrap(lambda d, ix, pk: (0, OFF_RW))),
                pl.BlockSpec((B, RW_BLK), wrap(const)),
                pl.BlockSpec((None, None, N, N, B), wrap(lambda d, ix, pk: (l, pk[0], 0, 0, 0)))]
    in_specs += [pl.BlockSpec(rp[k].shape, wrap(const)) for k in _RW_PKEYS]
    args = [proj, shift_p, s_t] + [rp[k] for k in _RW_PKEYS]
    aliases = {}
    if buf is not None:
        in_specs.append(pl.BlockSpec(memory_space=pl.ANY))
        args.append(buf)
        aliases = {len(args) - 1: 1}
    return pl.pallas_call(
        body,
        grid=grid,
        in_specs=in_specs,
        out_specs=[pl.BlockSpec((B, RWKV_W), wrap(const)),
                   pl.BlockSpec((None, None, N, N, B), wrap(lambda d, ix, pk: (d, ix[0], 0, 0, 0)))],
        out_shape=[jax.ShapeDtypeStruct((B, RWKV_W), bf16),
                   jax.ShapeDtypeStruct((DEPTH, RWKV_H, N, N, B), f32)],
        scratch_shapes=[pltpu.VMEM((RWKV_W, B), f32)] * 7,
        input_output_aliases=aliases,
        compiler_params=_cparams(("arbitrary",) * len(grid)),
    )(*args)


S5_N = S5_G * S5_P
S5_KC = 256
S5_NKC = S5_W // S5_KC
S5_TILES = S5_N // LANE
S5_GB = S5_KC // S5_GC


def _s5_params(p):
    a_re, a_im = p['s5_a_re'], p['s5_a_im']
    dstep = jnp.exp(p['s5_log_dt'])[:, None]
    mag = jnp.exp(a_re * dstep)
    ab_re = mag * jnp.cos(a_im * dstep)
    ab_im = mag * jnp.sin(a_im * dstep)
    den = a_re * a_re + a_im * a_im
    n_re = ab_re - 1.0
    f_re = (n_re * a_re + ab_im * a_im) / den
    f_im = (ab_im * a_re - n_re * a_im) / den
    b_re, b_im = p['s5_b_re'], p['s5_b_im']
    bb_re = f_re[..., None] * b_re - f_im[..., None] * b_im
    bb_im = f_re[..., None] * b_im + f_im[..., None] * b_re
    eye = jnp.eye(S5_GB, dtype=f32)

    def in_map(bb):
        t = bb.reshape(S5_NKC, S5_GB, S5_P, S5_GC)
        return jnp.einsum('kgpc,gh->kgchp', t, eye).reshape(S5_NKC, S5_KC, S5_GB * S5_P)

    def out_map(cc):
        t = cc.reshape(S5_NKC, S5_GB, S5_GC, S5_P)
        return jnp.einsum('qgcp,gh->qgphc', t, eye).reshape(S5_NKC, S5_GB * S5_P, S5_KC)

    return dict(wb=jnp.concatenate([in_map(bb_re), in_map(bb_im)], axis=-1),
                wc_re=out_map(p['s5_c_re']), wc_im=out_map(p['s5_c_im']),
                ab_re_t=ab_re.reshape(S5_TILES // 8, 8, LANE), ab_im_t=ab_im.reshape(S5_TILES // 8, 8, LANE),
                ab_re=ab_re.reshape(1, S5_N), ab_im=ab_im.reshape(1, S5_N), d=p['s5_d'].reshape(1, S5_W))


def _s5_seq(proj, sp, B, L):
    Lc = min(L, 256)
    n = L // Lc
    pitch = Lc + 4
    lead = OFF_U % LANE
    base = OFF_U - lead
    width = S5_W + LANE
    nt4 = S5_TILES // 8
    tiles_kc = S5_TILES // S5_NKC

    def body(u_ref, wb_ref, wcr_ref, wci_ref, abr_ref, abi_ref, d_ref, y_ref, sr_ref, si_ref, xr, xi, cr, ci):
        c = pl.program_id(1)

        @pl.when(c == 0)
        def _():
            cr[...] = jnp.zeros_like(cr)
            ci[...] = jnp.zeros_like(ci)

        u = u_ref[:, lead:lead + S5_W]
        ub = u.astype(bf16)
        for kc in range(S5_NKC):
            bu = _dot(ub[:, kc * S5_KC:(kc + 1) * S5_KC], wb_ref[kc].astype(bf16))
            for j in range(tiles_kc):
                t = kc * tiles_kc + j
                xr[pl.ds(t * pitch, Lc), :] = bu[:, j * LANE:(j + 1) * LANE]
                xi[pl.ds(t * pitch, Lc), :] = bu[:, (tiles_kc + j) * LANE:(tiles_kc + j + 1) * LANE]

        abr = [abr_ref[g] for g in range(nt4)]
        abi = [abi_ref[g] for g in range(nt4)]

        def step(t, carry):
            out = []
            for g in range(nt4):
                s_r, s_i = carry[2 * g], carry[2 * g + 1]
                idx = pl.ds(g * 8 * pitch + t, 8, stride=pitch)
                n_r = abr[g] * s_r - abi[g] * s_i + xr[idx, :]
                n_i = abr[g] * s_i + abi[g] * s_r + xi[idx, :]
                xr[idx, :] = n_r
                xi[idx, :] = n_i
                out += [n_r, n_i]
            return tuple(out)

        init = []
        for g in range(nt4):
            init += [cr[g], ci[g]]
        fin = lax.fori_loop(0, Lc, step, tuple(init))
        for g in range(nt4):
            cr[g] = fin[2 * g]
            ci[g] = fin[2 * g + 1]

        for q in range(S5_NKC):
            lr = jnp.concatenate([xr[pl.ds((q * tiles_kc + j) * pitch, Lc), :] for j in range(tiles_kc)], axis=1)
            li = jnp.concatenate([xi[pl.ds((q * tiles_kc + j) * pitch, Lc), :] for j in range(tiles_kc)], axis=1)
            y = _dot(lr.astype(bf16), wcr_ref[q].astype(bf16)) - _dot(li.astype(bf16), wci_ref[q].astype(bf16))
            cs = slice(q * S5_KC, (q + 1) * S5_KC)
            y = y + d_ref[:, cs] * u[:, cs]
            y_ref[:, cs] = jax.nn.gelu(y).astype(bf16)

        @pl.when(c == n - 1)
        def _():
            sr_ref[...] = cr[...]
            si_ref[...] = ci[...]

    full = lambda shp: pl.BlockSpec(shp, lambda b, c: (0,) * len(shp))
    y, sr, si = pl.pallas_call(
        body,
        grid=(B, n),
        in_specs=[pl.BlockSpec((pl.Element(Lc), pl.Element(width)),
                               lambda b, c: (pl.multiple_of((b * n + c) * Lc, Lc), base)),
                  full(sp['wb'].shape), full(sp['wc_re'].shape), full(sp['wc_im'].shape),
                  full(sp['ab_re_t'].shape), full(sp['ab_im_t'].shape), full((1, S5_W))],
        out_specs=[pl.BlockSpec((Lc, S5_W), lambda b, c: (b * n + c, 0)),
                   pl.BlockSpec((None, nt4, 8, LANE), lambda b, c: (b, 0, 0, 0)),
                   pl.BlockSpec((None, nt4, 8, LANE), lambda b, c: (b, 0, 0, 0))],
        out_shape=[jax.ShapeDtypeStruct((B * L, S5_W), bf16),
                   jax.ShapeDtypeStruct((B, nt4, 8, LANE), f32),
                   jax.ShapeDtypeStruct((B, nt4, 8, LANE), f32)],
        scratch_shapes=[pltpu.VMEM((S5_TILES * pitch, LANE), f32), pltpu.VMEM((S5_TILES * pitch, LANE), f32),
                        pltpu.VMEM((nt4, 8, LANE), f32), pltpu.VMEM((nt4, 8, LANE), f32)],
        compiler_params=_cparams(("parallel", "arbitrary")),
    )(proj, sp['wb'], sp['wc_re'], sp['wc_im'], sp['ab_re_t'], sp['ab_im_t'], sp['d'])
    return y, sr.reshape(B, S5_G, S5_P), si.reshape(B, S5_G, S5_P)


def _s5_step(proj, x_re, x_im, sp):
    B = proj.shape[0]
    lead = OFF_U % LANE
    base = OFF_U - lead
    width = S5_W + LANE
    kw = S5_N // S5_NKC

    def body(u_ref, xr_ref, xi_ref, wb_ref, wcr_ref, wci_ref, abr_ref, abi_ref, d_ref, y_ref, sr_ref, si_ref):
        u = u_ref[:, lead:lead + S5_W]
        ub = u.astype(bf16)
        abr, abi = abr_ref[...], abi_ref[...]
        xr, xi = xr_ref[...], xi_ref[...]
        for kc in range(S5_NKC):
            bu = _dot(ub[:, kc * S5_KC:(kc + 1) * S5_KC], wb_ref[kc].astype(bf16))
            sl = slice(kc * kw, (kc + 1) * kw)
            n_r = abr[:, sl] * xr[:, sl] - abi[:, sl] * xi[:, sl] + bu[:, :kw]
            n_i = abr[:, sl] * xi[:, sl] + abi[:, sl] * xr[:, sl] + bu[:, kw:]
            sr_ref[:, sl] = n_r
            si_ref[:, sl] = n_i
            y = _dot(n_r.astype(bf16), wcr_ref[kc].astype(bf16)) - _dot(n_i.astype(bf16), wci_ref[kc].astype(bf16))
            cs = slice(kc * S5_KC, (kc + 1) * S5_KC)
            y = y + d_ref[:, cs] * u[:, cs]
            y_ref[:, cs] = jax.nn.gelu(y).astype(bf16)

    full = lambda shp: pl.BlockSpec(shp, lambda i: (0,) * len(shp))
    y, sr, si = pl.pallas_call(
        body,
        grid=(1,),
        in_specs=[pl.BlockSpec((pl.Element(B), pl.Element(width)), lambda i: (0, base)),
                  full((B, S5_N)), full((B, S5_N)),
                  full(sp['wb'].shape), full(sp['wc_re'].shape), full(sp['wc_im'].shape),
                  full((1, S5_N)), full((1, S5_N)), full((1, S5_W))],
        out_specs=[full((B, S5_W)), full((B, S5_N)), full((B, S5_N))],
        out_shape=[jax.ShapeDtypeStruct((B, S5_W), bf16),
                   jax.ShapeDtypeStruct((B, S5_N), f32), jax.ShapeDtypeStruct((B, S5_N), f32)],
        compiler_params=_cparams(("arbitrary",)),
    )(proj, x_re.reshape(B, S5_N), x_im.reshape(B, S5_N), sp['wb'], sp['wc_re'], sp['wc_im'],
      sp['ab_re'], sp['ab_im'], sp['d'])
    return y, sr.reshape(B, S5_G, S5_P), si.reshape(B, S5_G, S5_P)


MOE_RT = 256
MOE_RT_SMALL = 32
SLAB = D // LANE
SLAB_PITCH = SLAB + 4


def _moe_rt(T):
    return MOE_RT if T * TOP_K >= N_EXPERTS * MOE_RT else MOE_RT_SMALL
ROUTE_W = LANE


def _router(x, g, mod, w_r, b_r, row0, t_all, bufs):
    T = x.shape[0]
    tm = min(mod.tm, 512)
    nm = T // tm
    assert row0 % tm == 0
    if bufs is None:
        assert row0 == 0
        n_steps = pl.cdiv(t_all, tm)
    else:
        n_steps = nm
    clamp = lambda m: jnp.minimum(m, nm - 1)

    def body(x_ref, g_ref, sh_ref, sc_ref, w_ref, b_ref, *rest):
        h_ref, e_ref, p_ref = rest[-3:]
        if n_steps > nm:
            @pl.when(pl.program_id(0) >= nm)
            def _():
                h_ref[...] = jnp.zeros_like(h_ref)
                e_ref[...] = jnp.zeros_like(e_ref)
                p_ref[...] = jnp.zeros_like(p_ref)

            pl.when(pl.program_id(0) < nm)(lambda: route(x_ref, g_ref, sh_ref, sc_ref, w_ref, b_ref, *rest[-3:]))
        else:
            route(x_ref, g_ref, sh_ref, sc_ref, w_ref, b_ref, *rest[-3:])

    def route(x_ref, g_ref, sh_ref, sc_ref, w_ref, b_ref, h_ref, e_ref, p_ref):
        h2 = _rms(x_ref[...], g_ref[...]) * (1.0 + sc_ref[...]) + sh_ref[...]
        for s in range(SLAB):
            h_ref[:, s, :] = h2[:, s * LANE:(s + 1) * LANE]
        logits = jnp.dot(h2, w_ref[...], precision=lax.Precision.HIGHEST, preferred_element_type=f32) + b_ref[...]
        lane = lax.broadcasted_iota(i32, (tm, ROUTE_W), 1)
        ninf = jnp.float32(-jnp.inf)
        gl = jnp.where(lane < N_GROUPS, logits, ninf)
        gm = jnp.max(gl, axis=-1, keepdims=True)
        g_p = 1.0 / jnp.sum(jnp.exp(gl - gm), axis=-1, keepdims=True)
        g_idx = jnp.min(jnp.where(gl == gm, lane, ROUTE_W), axis=-1, keepdims=True)
        valid = (lane >= N_GROUPS) & (lane < N_GROUPS + N_EXPERTS) & (((lane - N_GROUPS) // EPG) == g_idx)
        el = jnp.where(valid, logits, ninf)
        ee = jnp.exp(el - jnp.max(el, axis=-1, keepdims=True))
        prob = jnp.where(valid, ee / jnp.sum(ee, axis=-1, keepdims=True), -1.0)
        p1 = jnp.max(prob, axis=-1, keepdims=True)
        i1 = jnp.min(jnp.where(prob == p1, lane, ROUTE_W), axis=-1, keepdims=True)
        prob2 = jnp.where(lane == i1, -1.0, prob)
        p2 = jnp.max(prob2, axis=-1, keepdims=True)
        i2 = jnp.min(jnp.where(prob2 == p2, lane, ROUTE_W), axis=-1, keepdims=True)
        den = p1 + p2
        e_ref[...] = jnp.where(lane == 0, i1 - N_GROUPS, jnp.where(lane == 1, i2 - N_GROUPS, 0))
        p_ref[...] = jnp.where(lane == 0, g_p * p1 / den, jnp.where(lane == 1, g_p * p2 / den, 0.0))

    in_specs = [pl.BlockSpec((tm, D), lambda m: (clamp(m), 0)),
                pl.BlockSpec((1, D), lambda m: (0, 0)),
                mod.row_spec(3, tm, clamp), mod.row_spec(4, tm, clamp),
                pl.BlockSpec((D, ROUTE_W), lambda m: (0, 0)),
                pl.BlockSpec((1, ROUTE_W), lambda m: (0, 0))]
    args = [x, g.reshape(1, D), mod.arr, mod.arr, w_r, b_r]
    aliases = {}
    if bufs is not None:
        in_specs += [pl.BlockSpec(memory_space=pl.ANY)] * 3
        aliases = {len(args) + i: i for i in range(3)}
        args += list(bufs)
    b0 = row0 // tm
    return pl.pallas_call(
        body,
        grid=(n_steps,),
        in_specs=in_specs,
        out_specs=[pl.BlockSpec((tm, SLAB, LANE), lambda m: (b0 + m, 0, 0)),
                   pl.BlockSpec((tm, ROUTE_W), lambda m: (b0 + m, 0)),
                   pl.BlockSpec((tm, ROUTE_W), lambda m: (b0 + m, 0))],
        out_shape=[jax.ShapeDtypeStruct((t_all, SLAB, LANE), f32), jax.ShapeDtypeStruct((t_all, ROUTE_W), i32),
                   jax.ShapeDtypeStruct((t_all, ROUTE_W), f32)],
        input_output_aliases=aliases,
        compiler_params=_cparams(("arbitrary",)),
    )(*args)


TOK_BITS = 14


def _moe_plan(eid, T):
    A = T * TOP_K
    RT = _moe_rt(T)
    nt = A // RT + N_EXPERTS
    flat_e = eid[:, :TOP_K].reshape(-1)
    order = jnp.argsort(flat_e).astype(i32)
    counts = jnp.sum((flat_e[:, None] == jnp.arange(N_EXPERTS, dtype=i32)[None, :]).astype(i32), axis=0)
    start = jnp.cumsum(counts) - counts
    pcnt = (counts + RT - 1) // RT * RT
    pend = jnp.cumsum(pcnt)
    tile_e = jnp.minimum(jnp.sum((pend[None, :] <= (jnp.arange(nt, dtype=i32) * RT)[:, None]).astype(i32), axis=1),
                         N_EXPERTS - 1)
    off = (jnp.arange(nt, dtype=i32) * RT - (pend - pcnt)[tile_e])[:, None] + jnp.arange(RT, dtype=i32)[None, :]
    real = off < counts[tile_e][:, None]
    srt = start[tile_e][:, None] + jnp.minimum(off, counts[tile_e][:, None])
    a = order[jnp.minimum(srt, A - 1)]
    slot = jnp.arange(nt * RT, dtype=i32).reshape(nt, RT)
    tok = jnp.where(real, a // TOP_K, 0)
    dst = jnp.where(real, a, A + slot - srt)
    n_used = (pend[-1] // RT).astype(i32).reshape(1)
    return (tok | (dst << TOK_BITS)).reshape(-1), tile_e, n_used


def _experts(h2, packed, tile_e, n_used, l, w1, w3, w2):
    T = h2.shape[0]
    assert TOP_K == 2 and T < (1 << TOK_BITS)
    RT = _moe_rt(T)
    PITCH = SLAB_PITCH
    nt = tile_e.shape[0]
    prime_id = nt * RT
    out_rows = (prime_id + RT) // TOP_K
    HC, OC = 128, 256
    n_hc, n_oc = D_EXPERT // HC, D // OC
    g_per, s_per = RT // n_hc, RT // n_oc

    def body(tile_ref, nused_ref, slot_ref, h_hbm, h2d_hbm, w1_ref, w3_ref, w2_ref, o_hbm,
             x0, x1, o0, o1, w1b, w3b, w2b, gsem, ssem):
        j = pl.program_id(0)
        n_used = nused_ref[0]
        active = j < n_used

        def gather_row(x_ref, r, tok):
            return pltpu.make_async_copy(h_hbm.at[tok], x_ref.at[pl.ds(r * PITCH, SLAB), :], gsem)

        def scatter_row(o_ref, r, d):
            return pltpu.make_async_copy(o_ref.at[pl.ds(r * PITCH, SLAB), :],
                                         o_hbm.at[lax.shift_right_logical(d, 1), d & (TOP_K - 1)], ssem)

        def wait_rows(sem):
            pltpu.make_async_copy(h2d_hbm.at[pl.ds(0, RT * SLAB), :], x0.at[pl.ds(0, RT * SLAB), :], sem).wait()

        wait_gather = lambda: wait_rows(gsem)
        wait_scatter = lambda: wait_rows(ssem)

        tok_of = lambda s: s & ((1 << TOK_BITS) - 1)
        dst_of = lambda s: lax.shift_right_logical(s, TOK_BITS)

        @pl.when(j == 0)
        def _():
            o1[...] = jnp.zeros_like(o1)

            def one(r, _):
                gather_row(x0, r, tok_of(slot_ref[r])).start()
                return 0
            lax.fori_loop(0, RT, one, 0)

        @pl.when(active & ((j == 0) | (tile_ref[j] != tile_ref[jnp.maximum(j - 1, 0)])))
        def _():
            w1b[...] = w1_ref[...].astype(bf16)
            w3b[...] = w3_ref[...].astype(bf16)
            w2b[...] = w2_ref[...].astype(bf16)

        @pl.when(active & (j >= 1))
        def _():
            wait_scatter()

        def tile(x_cur, x_nxt, o_cur, o_prv):
            wait_gather()
            xb = jnp.concatenate([x_cur[pl.ds(s, RT, stride=PITCH), :] for s in range(SLAB)], axis=1).astype(bf16)
            nxt = jnp.minimum(j + 1, n_used - 1) * RT
            prv = jnp.maximum(j - 1, 0) * RT
            parts = []
            for c in range(n_hc):
                h1 = _dot(xb, w1b[:, c * HC:(c + 1) * HC])
                h3 = _dot(xb, w3b[:, c * HC:(c + 1) * HC])
                parts.append(((h1 * jax.nn.sigmoid(h1)) * h3).astype(bf16))
                for r in range(c * g_per, (c + 1) * g_per):
                    gather_row(x_nxt, r, tok_of(slot_ref[nxt + r])).start(priority=r % 2)
            hm = jnp.concatenate(parts, axis=1)
            for c in range(n_oc):
                res = _dot(hm, w2b[:, c * OC:(c + 1) * OC])
                for t in range(OC // LANE):
                    o_cur[pl.ds(c * (OC // LANE) + t, RT, stride=PITCH), :] = res[:, t * LANE:(t + 1) * LANE]
                for r in range(c * s_per, (c + 1) * s_per):
                    d = jnp.where(j == 0, prime_id + r, dst_of(slot_ref[prv + r]))
                    scatter_row(o_prv, r, d).start(priority=r % 2)

        def last(o_cur):
            wait_scatter()

            def one(r, _):
                scatter_row(o_cur, r, dst_of(slot_ref[j * RT + r])).start()
                return 0
            lax.fori_loop(0, RT, one, 0)
            wait_scatter()
            wait_gather()

        for par, (xc, xn, oc, op) in enumerate(((x0, x1, o0, o1), (x1, x0, o1, o0))):
            pl.when(active & (j % 2 == par))(lambda xc=xc, xn=xn, oc=oc, op=op: tile(xc, xn, oc, op))
        for par, oc in enumerate((o0, o1)):
            pl.when((j == n_used - 1) & (j % 2 == par))(lambda oc=oc: last(oc))

    grid_spec = pltpu.PrefetchScalarGridSpec(
        num_scalar_prefetch=3,
        grid=(nt,),
        in_specs=[pl.BlockSpec(memory_space=pl.ANY), pl.BlockSpec(memory_space=pl.ANY),
                  pl.BlockSpec((None, None, D, D_EXPERT), lambda j, te, *_: (l, te[j], 0, 0)),
                  pl.BlockSpec((None, None, D, D_EXPERT), lambda j, te, *_: (l, te[j], 0, 0)),
                  pl.BlockSpec((None, None, D_EXPERT, D), lambda j, te, *_: (l, te[j], 0, 0))],
        out_specs=pl.BlockSpec(memory_space=pl.ANY),
        scratch_shapes=[pltpu.VMEM((RT * PITCH, LANE), f32), pltpu.VMEM((RT * PITCH, LANE), f32),
                        pltpu.VMEM((RT * PITCH, LANE), f32), pltpu.VMEM((RT * PITCH, LANE), f32),
                        pltpu.VMEM((D, D_EXPERT), bf16), pltpu.VMEM((D, D_EXPERT), bf16),
                        pltpu.VMEM((D_EXPERT, D), bf16),
                        pltpu.SemaphoreType.DMA(()), pltpu.SemaphoreType.DMA(())],
    )
    return pl.pallas_call(
        body,
        grid_spec=grid_spec,
        out_shape=jax.ShapeDtypeStruct((out_rows, TOP_K, SLAB, LANE), f32),
        compiler_params=_cparams(("arbitrary",)),
    )(tile_e, n_used, packed, h2, h2.reshape(T * SLAB, LANE), w1, w3, w2)


def _moe_combine(x, o2, wgt, mod, norm_final, row0):
    T = x.shape[0]
    tm = min(mod.tm, 512)
    b0 = row0 // tm

    def body(x_ref, o_ref, p_ref, m_ref, *rest):
        p = p_ref[...]
        row = lambda k: jnp.concatenate([o_ref[:, k, s, :] for s in range(SLAB)], axis=1)
        y = p[:, 0:1] * row(0)
        for k in range(1, TOP_K):
            y = y + p[:, k:k + 1] * row(k)
        xo = x_ref[...] + m_ref[...] * y
        if norm_final is None:
            rest[0][...] = xo
        else:
            rest[1][...] = _rms(xo, rest[0][...])

    ins = [x, o2, wgt, mod.arr]
    specs = [pl.BlockSpec((tm, D), lambda m: (m, 0)),
             pl.BlockSpec((tm, TOP_K, SLAB, LANE), lambda m: (b0 + m, 0, 0, 0)),
             pl.BlockSpec((tm, ROUTE_W), lambda m: (b0 + m, 0)), mod.row_spec(5, tm)]
    if norm_final is not None:
        ins.append(norm_final.reshape(1, D))
        specs.append(pl.BlockSpec((1, D), lambda m: (0, 0)))
    return pl.pallas_call(
        body,
        grid=(T // tm,),
        in_specs=specs,
        out_specs=pl.BlockSpec((tm, D), lambda m: (m, 0)),
        out_shape=jax.ShapeDtypeStruct((T, D), f32),
        compiler_params=_cparams(("parallel",)),
    )(*ins)


def _moe(groups, p, l, big, norm_final):
    pad = ROUTE_W - N_GROUPS - N_EXPERTS
    w_r = jnp.pad(jnp.concatenate([p['moe_w_group'], p['moe_w_router']], axis=1), ((0, 0), (0, pad)))
    b_r = jnp.pad(jnp.concatenate([p['moe_b_group'], p['moe_b_router']]), (0, pad)).reshape(1, ROUTE_W)
    t_all = sum(g.T for g in groups)
    bufs, row0 = None, 0
    for g in groups:
        g.row0 = row0
        bufs = _router(g.x, p['norm_ffn'], g.mod, w_r, b_r, row0, t_all, bufs)
        row0 += g.T
    h2, eid, wgt = bufs
    packed, tile_e, n_used = _moe_plan(eid, t_all)
    o2 = _experts(h2, packed, tile_e, n_used, l, big['moe_w1'], big['moe_w3'], big['moe_w2'])
    for g in groups:
        g.x = _moe_combine(g.x, o2, wgt, g.mod, norm_final, g.row0)


class _Group:
    def __init__(self, x, mods, states, pos0):
        self.B, self.L, _ = x.shape
        self.T = self.B * self.L
        self.tm = _row_tile(self.T)
        self.x = x.reshape(self.T, D)
        self.mods, self.states, self.pos0 = mods, states, pos0
        self.outs = ([], [], [], [], [])
        self.ret_all = self.rw_all = None

    def mix(self, l, p, rp, sp, layers, big, w_in_t):
        B, L = self.B, self.L
        self.mod = mod = _Mod(self.mods[l], L, self.tm)
        proj = _in_proj(self.x, layers['norm_mix'], l, mod, w_in_t)
        if self.states is None:
            y_ret, s_ret = _retention_seq(proj, B, L)
            y_rw, s_rw, shift = _rwkv_seq(proj, rp, B, L)
            y_s5, s_re, s_im = _s5_seq(proj, sp, B, L)
        else:
            st_ret, st_rw, st_shift, st_re, st_im = self.states
            y_ret, self.ret_all = _retention_step(proj, st_ret, l, self.ret_all, self.pos0)
            y_rw, self.rw_all = _rwkv_step(proj, st_shift[l], jnp.transpose(st_rw, (0, 2, 3, 4, 1)), l,
                                           self.rw_all, rp)
            s_ret = s_rw = None
            shift = proj[:, OFF_RW:OFF_RW + RWKV_PROJ]
            y_s5, s_re, s_im = _s5_step(proj, st_re[l], st_im[l], sp)
        z = _glu_proj(y_s5, l, big['s5_w_glu'])
        merged = _merge_proj(y_ret, y_rw, z, l, big['ret_w_o'], big['rwkv_w_o'], big['s5_w_o'], proj)
        self.x = _out_proj(merged, l, big['w_out'], self.x, mod)
        for lst, val in zip(self.outs, (s_ret, s_rw, shift, s_re, s_im)):
            lst.append(val)

    def results(self):
        stacked = [jnp.stack(o) if o[0] is not None else None for o in self.outs]
        if self.states is not None:
            stacked[0], stacked[1] = self.ret_all, jnp.transpose(self.rw_all, (0, 4, 1, 2, 3))
        return self.x.reshape(self.B, self.L, D), stacked


def _trunk(groups, layers, norm_final):
    big = {k: layers[k] for k in _BIG}
    w_in_t = jnp.swapaxes(layers['w_in'], 1, 2)
    for l in range(DEPTH):
        p = {name: arr[l] for name, arr in layers.items() if name not in _BIG + ('w_in',)}
        rp, sp = _rwkv_params(p), _s5_params(p)
        for g in groups:
            g.mix(l, p, rp, sp, layers, big, w_in_t)
        _moe(groups, p, l, big, norm_final if l == DEPTH - 1 else None)
    return [g.results() for g in groups]


_BIG = ('ret_w_o', 'rwkv_w_o', 's5_w_glu', 's5_w_o', 'w_out', 'moe_w1', 'moe_w3', 'moe_w2')


def kernel(x_prompt, x_sample, state_ret, state_rwkv, state_shift, state_s5_re, state_s5_im,
           c_prompt, c_sample, norm_mix, norm_ffn, w_ada, b_ada, w_in, ret_w_o, rwkv_mu, rwkv_w0,
           rwkv_w2, rwkv_a0, rwkv_a2, rwkv_g2, rwkv_k_k, rwkv_k_a, rwkv_r_k, rwkv_ln_w, rwkv_ln_b,
           rwkv_w_o, s5_a_re, s5_a_im, s5_b_re, s5_b_im, s5_c_re, s5_c_im, s5_d, s5_log_dt, s5_w_glu,
           s5_w_o, w_out, moe_w_group, moe_b_group, moe_w_router, moe_b_router, moe_w1, moe_w3, moe_w2,
           norm_final):
    layers = {
        'norm_mix': norm_mix, 'norm_ffn': norm_ffn, 'w_in': w_in,
        'ret_w_o': ret_w_o, 'rwkv_mu': rwkv_mu, 'rwkv_w0': rwkv_w0, 'rwkv_w2': rwkv_w2,
        'rwkv_a0': rwkv_a0, 'rwkv_a2': rwkv_a2, 'rwkv_g2': rwkv_g2, 'rwkv_k_k': rwkv_k_k,
        'rwkv_k_a': rwkv_k_a, 'rwkv_r_k': rwkv_r_k, 'rwkv_ln_w': rwkv_ln_w, 'rwkv_ln_b': rwkv_ln_b,
        'rwkv_w_o': rwkv_w_o, 's5_a_re': s5_a_re, 's5_a_im': s5_a_im, 's5_b_re': s5_b_re,
        's5_b_im': s5_b_im, 's5_c_re': s5_c_re, 's5_c_im': s5_c_im, 's5_d': s5_d,
        's5_log_dt': s5_log_dt, 's5_w_glu': s5_w_glu, 's5_w_o': s5_w_o, 'w_out': w_out,
        'moe_w_group': moe_w_group, 'moe_b_group': moe_b_group, 'moe_w_router': moe_w_router,
        'moe_b_router': moe_b_router, 'moe_w1': moe_w1, 'moe_w3': moe_w3, 'moe_w2': moe_w2,
    }
    Bp, Bs = x_prompt.shape[0], x_sample.shape[0]
    s_off = -(-Bp // 8) * 8
    c_all = jnp.concatenate([c_prompt, jnp.zeros((s_off - Bp, D), f32), c_sample], axis=0)
    mod_all = _adaln(c_all, w_ada, b_ada)
    prompt = _Group(x_prompt, mod_all[:, :Bp], None, 0.0)
    sample = _Group(x_sample, mod_all[:, s_off:s_off + Bs],
                    (state_ret, state_rwkv, state_shift, state_s5_re, state_s5_im), float(PAST_LEN))
    ((y_prompt, (ret_p, rwkv_p, shift_p, s5re_p, s5im_p)),
     (y_sample, (ret_s, rwkv_s, shift_s, s5re_s, s5im_s))) = _trunk([prompt, sample], layers, norm_final)
    return (y_prompt, y_sample, ret_p, ret_s, rwkv_p, rwkv_s, shift_p, shift_s, s5re_p, s5re_s, s5im_p, s5im_s)
```

```python
import jax
import jax.numpy as jnp
from jax import lax
from jax.experimental import pallas as pl
from jax.experimental.pallas import tpu as pltpu

f32 = jnp.float32
bf16 = jnp.bfloat16
i32 = jnp.int32

D = 2048
DEPTH = 2
PAST_LEN = 16384
RET_W, RET_H, RET_DK, RET_DV, RET_CHUNK = 1024, 4, 256, 256, 128
RET_GN_EPS = 1e-6
ROPE_BASE = 10000.0
RWKV_W, RWKV_N, RWKV_H = 1024, 64, 16
DECAY_LORA, AAA_LORA, GATE_LORA = 64, 64, 160
RWKV_PROJ = 3 * RWKV_W + DECAY_LORA + AAA_LORA + GATE_LORA
RWKV_LN_EPS = 64e-5
S5_W, S5_GC, S5_G, S5_P = 1024, 16, 64, 64
N_MOD = 6
RMS_EPS = 1e-6
N_GROUPS, EPG, N_EXPERTS, TOP_K, D_EXPERT = 4, 8, 32, 2, 512
IN_W = 4 * RET_W + RWKV_PROJ + S5_W + 3 * D
OFF_Q, OFF_K, OFF_V, OFF_G, OFF_RW = 0, 1024, 2048, 3072, 4096
OFF_U = OFF_RW + RWKV_PROJ
OFF_GATE = OFF_U + S5_W
LANE = 128
PROJ_W = ((IN_W + LANE - 1) // LANE) * LANE
VMEM_LIMIT = 56 * 1024 * 1024


def _cparams(sem):
    return pltpu.CompilerParams(dimension_semantics=sem, vmem_limit_bytes=VMEM_LIMIT)


def _dot(a, b):
    return jnp.dot(a, b, preferred_element_type=f32)


def _dot_nt(a, b):
    return lax.dot_general(a, b, (((1,), (1,)), ((), ())), preferred_element_type=f32)


def _dot_tn(a, b):
    return lax.dot_general(a, b, (((0,), (0,)), ((), ())), preferred_element_type=f32)


def _rms(x, g):
    return x * lax.rsqrt(jnp.mean(x * x, axis=-1, keepdims=True) + RMS_EPS) * g


def _head_norm(y, eps):
    mu = jnp.mean(y, axis=-1, keepdims=True)
    yc = y - mu
    return yc * lax.rsqrt(jnp.mean(yc * yc, axis=-1, keepdims=True) + eps)


def _row_tile(T):
    return 1024 if T >= 1024 else T


class _Mod:
    def __init__(self, mod, L, tm):
        self.L, self.tm = L, tm
        self.per_token = L == 1
        self.arr = mod if self.per_token else mod.reshape(mod.shape[0], 1, N_MOD * D)

    def spec(self, j, tn, col_of):
        nb = D // tn
        if self.per_token:
            return pl.BlockSpec((self.tm, tn), lambda m, n: (m, j * nb + col_of(n)))
        L, tm = self.L, self.tm
        return pl.BlockSpec((None, 1, tn), lambda m, n: ((m * tm) // L, 0, j * nb + col_of(n)))

    def row_spec(self, j, tm, remap=lambda m: m):
        if self.per_token:
            return pl.BlockSpec((tm, D), lambda m: (remap(m), j))
        L = self.L
        return pl.BlockSpec((None, 1, D), lambda m: ((remap(m) * tm) // L, 0, j))


def _fused_mm(x_ops, w_ops, e_ops, pre, post, *, grid, out_specs, out_shape, cache_shapes):
    nx, nw, ne = len(x_ops), len(w_ops), len(e_ops)
    n_out = len(out_shape)

    def body(*refs):
        x_refs = refs[:nx]
        w_refs = refs[nx:nx + nw]
        e_refs = refs[nx + nw:nx + nw + ne]
        o_refs = refs[nx + nw + ne:nx + nw + ne + n_out]
        caches = refs[nx + nw + ne + n_out:]
        if cache_shapes:
            @pl.when(pl.program_id(1) == 0)
            def _():
                for i in range(nx):
                    caches[i][...] = pre(i, x_refs[i], e_refs).astype(bf16)
            lhs = [c[...] for c in caches]
        else:
            lhs = [x[...] for x in x_refs]
        prods = [(_dot_nt if len(w_ops[j]) > 3 and w_ops[j][3] else _dot)(
            lhs[w_ops[j][2]], w_refs[j][...].astype(bf16)) for j in range(nw)]
        for o_ref, o in zip(o_refs, post(prods, e_refs)):
            o_ref[...] = o.astype(o_ref.dtype)

    return pl.pallas_call(
        body,
        grid=grid,
        in_specs=[s for _, s in x_ops] + [w[1] for w in w_ops] + [s for _, s in e_ops],
        out_specs=out_specs,
        out_shape=out_shape,
        scratch_shapes=[pltpu.VMEM(s, bf16) for s in cache_shapes],
        compiler_params=_cparams(("parallel", "arbitrary")),
    )(*[a for a, _ in x_ops], *[w[0] for w in w_ops], *[a for a, _ in e_ops])


def _adaln(c_all, w_ada, b_ada):
    R = c_all.shape[0]
    tn = 1024

    def pre(i, x_ref, e_refs):
        c = x_ref[...]
        return c * jax.nn.sigmoid(c)

    def post(prods, e_refs):
        return (prods[0] + e_refs[0][...],)

    (out,) = _fused_mm(
        [(c_all, pl.BlockSpec((R, D), lambda l, n: (0, 0)))],
        [(w_ada, pl.BlockSpec((None, D, tn), lambda l, n: (l, 0, n)), 0)],
        [(b_ada.reshape(DEPTH, 1, N_MOD * D), pl.BlockSpec((None, 1, tn), lambda l, n: (l, 0, n)))],
        pre, post,
        grid=(DEPTH, N_MOD * D // tn),
        out_specs=[pl.BlockSpec((None, R, tn), lambda l, n: (l, 0, n))],
        out_shape=[jax.ShapeDtypeStruct((DEPTH, R, N_MOD * D), f32)],
        cache_shapes=[(R, D)],
    )
    return out


def _in_proj(x, g, l, mod, w_in_t):
    T = x.shape[0]
    tm, tn = mod.tm, 1024
    tr = min(tm, 512)

    def norm_body(x_ref, g_ref, shift_ref, scale_ref, h_ref):
        h_ref[...] = (_rms(x_ref[...], g_ref[...]) * (1.0 + scale_ref[...]) + shift_ref[...]).astype(bf16)

    h = pl.pallas_call(
        norm_body,
        grid=(T // tr,),
        in_specs=[pl.BlockSpec((tr, D), lambda m: (m, 0)),
                  pl.BlockSpec((None, 1, D), lambda m: (l, 0, 0)),
                  mod.row_spec(0, tr), mod.row_spec(1, tr)],
        out_specs=pl.BlockSpec((tr, D), lambda m: (m, 0)),
        out_shape=jax.ShapeDtypeStruct((T, D), bf16),
        compiler_params=_cparams(("parallel",)),
    )(x, g.reshape(DEPTH, 1, D), mod.arr, mod.arr)

    def mm_body(h_ref, w_ref, o_ref, wb):
        @pl.when(pl.program_id(1) == 0)
        def _():
            wb[...] = w_ref[...].astype(bf16)

        col = pl.program_id(0) * tn + lax.broadcasted_iota(i32, (tm, tn), 1)
        o_ref[...] = jnp.where(col < IN_W, _dot_nt(h_ref[...], wb[...]), 0.0)

    return pl.pallas_call(
        mm_body,
        grid=(pl.cdiv(PROJ_W, tn), T // tm),
        in_specs=[pl.BlockSpec((tm, D), lambda n, m: (m, 0)),
                  pl.BlockSpec((None, tn, D), lambda n, m: (l, n, 0))],
        out_specs=pl.BlockSpec((tm, tn), lambda n, m: (m, n)),
        out_shape=jax.ShapeDtypeStruct((T, PROJ_W), f32),
        scratch_shapes=[pltpu.VMEM((tn, D), bf16)],
        compiler_params=_cparams(("parallel", "arbitrary")),
    )(h, w_in_t)


def _glu_proj(yg, l, w_glu):
    T = yg.shape[0]
    tm, tn = _row_tile(T), 512
    nb = S5_W // tn

    def post(prods, e_refs):
        return (prods[0] * jax.nn.sigmoid(prods[1]),)

    (out,) = _fused_mm(
        [(yg, pl.BlockSpec((tm, S5_W), lambda m, n: (m, 0)))],
        [(w_glu, pl.BlockSpec((None, S5_W, tn), lambda m, n: (l, 0, n)), 0),
         (w_glu, pl.BlockSpec((None, S5_W, tn), lambda m, n: (l, 0, nb + n)), 0)],
        [], None, post,
        grid=(T // tm, nb),
        out_specs=[pl.BlockSpec((tm, tn), lambda m, n: (m, n))],
        out_shape=[jax.ShapeDtypeStruct((T, S5_W), bf16)],
        cache_shapes=[],
    )
    return out


def _merge_proj(y_ret, y_rw, y_s5, l, w_ret, w_rw, w_s5, proj):
    T = y_ret.shape[0]
    tm, tn = _row_tile(T), 512
    lead = OFF_GATE % LANE
    base = OFF_GATE - lead

    def gate_spec(i):
        return pl.BlockSpec((pl.Element(tm), pl.Element(tn + LANE)),
                            lambda m, n: (pl.multiple_of(m * tm, tm), pl.multiple_of(base + i * D + n * tn, LANE)))

    def post(prods, e_refs):
        acc = None
        for p, e in zip(prods, e_refs):
            t = jax.nn.sigmoid(e[:, lead:lead + tn]) * p
            acc = t if acc is None else acc + t
        return (acc,)

    xspec = pl.BlockSpec((tm, RET_W), lambda m, n: (m, 0))
    wspec = pl.BlockSpec((None, RET_W, tn), lambda m, n: (l, 0, n))
    (out,) = _fused_mm(
        [(y_ret, xspec), (y_rw, xspec), (y_s5, xspec)],
        [(w_ret, wspec, 0), (w_rw, wspec, 1), (w_s5, wspec, 2)],
        [(proj, gate_spec(0)), (proj, gate_spec(1)), (proj, gate_spec(2))],
        None, post,
        grid=(T // tm, D // tn),
        out_specs=[pl.BlockSpec((tm, tn), lambda m, n: (m, n))],
        out_shape=[jax.ShapeDtypeStruct((T, D), bf16)],
        cache_shapes=[],
    )
    return out


def _out_proj(merged, l, w_out, x, mod):
    T = x.shape[0]
    tm, tn = mod.tm, 512

    def post(prods, e_refs):
        x_ref, m_ref = e_refs
        return (x_ref[...] + m_ref[...] * prods[0],)

    (out,) = _fused_mm(
        [(merged, pl.BlockSpec((tm, D), lambda m, n: (m, 0)))],
        [(w_out, pl.BlockSpec((None, D, tn), lambda m, n: (l, 0, n)), 0)],
        [(x, pl.BlockSpec((tm, tn), lambda m, n: (m, n))), (mod.arr, mod.spec(2, tn, lambda n: n))],
        None, post,
        grid=(T // tm, D // tn),
        out_specs=[pl.BlockSpec((tm, tn), lambda m, n: (m, n))],
        out_shape=[jax.ShapeDtypeStruct((T, D), f32)],
        cache_shapes=[],
    )
    return out


def _ret_consts(L, pos0):
    C = RET_CHUNK if L % RET_CHUNK == 0 else L
    H = RET_H
    log_g = jnp.log1p(-jnp.exp2(-5.0 - jnp.arange(H, dtype=f32)))
    i = jnp.arange(C, dtype=f32)
    diff = i[:, None] - i[None, :]
    causal = diff >= 0
    dmask = jnp.where(causal, jnp.exp(jnp.where(causal, diff, 0.0)[None] * log_g[:, None, None]), 0.0)
    kdec = jnp.exp((C - 1.0 - i)[:, None] * log_g[None, :])
    qdec = jnp.exp((i + 1.0)[:, None] * log_g[None, :])
    g_chunk = jnp.exp(C * log_g)
    half = RET_DK // 2
    inv = ROPE_BASE ** (-jnp.arange(half, dtype=f32) / half)
    pos = pos0 + jnp.arange(L, dtype=f32)
    ang = pos[:, None] * inv[None, :]
    return C, dmask, kdec, qdec, g_chunk, jnp.cos(ang), jnp.sin(ang)


def _rotary(x, cos, sin):
    half = RET_DK // 2
    x1, x2 = x[..., :half], x[..., half:]
    return jnp.concatenate([x1 * cos - x2 * sin, x1 * sin + x2 * cos], axis=-1)


def _retention_seq(proj, B, L):
    C, dmask, kdec, qdec, g_chunk, cos, sin = _ret_consts(L, 0.0)
    H, dk = RET_H, RET_DK
    n = L // C
    kdec_f = jnp.broadcast_to(kdec.T[:, :, None], (H, C, dk))
    qdec_f = jnp.broadcast_to(qdec.T[:, :, None], (H, C, dk))
    gch_f = jnp.broadcast_to(g_chunk[:, None, None], (H, 8, dk))
    proj3 = proj.reshape(B, L, PROJ_W)

    def body(q_ref, k_ref, v_ref, g_ref, cos_ref, sin_ref, dm_ref, kd_ref, qd_ref, gc_ref, y_ref, s_ref, st):
        c = pl.program_id(1)

        @pl.when(c == 0)
        def _():
            st[...] = jnp.zeros_like(st)

        cs, sn = cos_ref[...], sin_ref[...]
        for h in range(H):
            sl = slice(h * dk, (h + 1) * dk)
            q = _rotary(q_ref[:, sl], cs, sn)
            k = _rotary(k_ref[:, sl], cs, sn) * (dk ** -0.5)
            vb = v_ref[:, sl].astype(bf16)
            s0 = st[h]
            scores = _dot_nt(q.astype(bf16), k.astype(bf16)) * dm_ref[h]
            o = _dot(scores.astype(bf16), vb)
            o = o + _dot((q * qd_ref[h]).astype(bf16), s0.astype(bf16))
            kv = _dot_tn((k * kd_ref[h]).astype(bf16), vb)
            st[h] = s0 * gc_ref[h, 0:1, :] + kv
            g = g_ref[:, sl]
            y_ref[:, sl] = (g * jax.nn.sigmoid(g) * _head_norm(o, RET_GN_EPS)).astype(bf16)

        @pl.when(c == n - 1)
        def _():
            s_ref[...] = st[...]

    def seg(off):
        return pl.BlockSpec((None, C, RET_W), lambda b, c: (b, c, off // RET_W))

    const3 = lambda shp: pl.BlockSpec(shp, lambda b, c: (0, 0, 0))
    y, s = pl.pallas_call(
        body,
        grid=(B, n),
        in_specs=[seg(OFF_Q), seg(OFF_K), seg(OFF_V), seg(OFF_G),
                  pl.BlockSpec((C, dk // 2), lambda b, c: (c, 0)),
                  pl.BlockSpec((C, dk // 2), lambda b, c: (c, 0)),
                  const3((H, C, C)), const3((H, C, dk)), const3((H, C, dk)), const3((H, 8, dk))],
        out_specs=[pl.BlockSpec((None, C, RET_W), lambda b, c: (b, c, 0)),
                   pl.BlockSpec((None, H, dk, RET_DV), lambda b, c: (b, 0, 0, 0))],
        out_shape=[jax.ShapeDtypeStruct((B, L, RET_W), bf16),
                   jax.ShapeDtypeStruct((B, H, dk, RET_DV), f32)],
        scratch_shapes=[pltpu.VMEM((H, dk, RET_DV), f32)],
        compiler_params=_cparams(("parallel", "arbitrary")),
    )(proj3, proj3, proj3, proj3, cos, sin, dmask, kdec_f, qdec_f, gch_f)
    return y.reshape(B * L, RET_W), s


STEP_TB = 16


def _layer_grid(l, buf, inner):
    if buf is not None:
        return inner, (lambda fn: (lambda *ix: fn(l, ix, ix)))
    assert l == 0
    last = tuple(n - 1 for n in inner)

    def wrap(fn):
        def index_map(d, *ix):
            parked = tuple(jnp.where(d == l, i, z) for i, z in zip(ix, last))
            return fn(d, ix, parked)
        return index_map
    return (DEPTH,) + inner, wrap


def _retention_step(proj, s_all, l, buf, pos0):
    B = proj.shape[0]
    _, _, _, _, g_chunk, cos, sin = _ret_consts(1, pos0)
    H, dk = RET_H, RET_DK
    gch = jnp.broadcast_to(g_chunk[:, None, None], (H, 8, dk))
    tb = STEP_TB
    grid, wrap = _layer_grid(l, buf, (B // tb, H))

    def body(q_ref, k_ref, v_ref, g_ref, cos_ref, sin_ref, gc_ref, s_ref, *rest):
        y_ref, so_ref = rest[-2:]

        def update():
            cs, sn = cos_ref[...], sin_ref[...]
            q = _rotary(q_ref[...], cs, sn)
            k = _rotary(k_ref[...], cs, sn) * (dk ** -0.5)
            v = v_ref[...]
            s1 = s_ref[...] * gc_ref[0:1, :][None] + k[:, :, None] * v[:, None, :]
            so_ref[...] = s1
            o = jnp.sum(q[:, :, None] * s1, axis=1)
            g = g_ref[...]
            y_ref[...] = (g * jax.nn.sigmoid(g) * _head_norm(o, RET_GN_EPS)).astype(bf16)

        if buf is not None:
            update()
        else:
            pl.when(pl.program_id(0) == l)(update)

            @pl.when(pl.program_id(0) != l)
            def _():
                so_ref[...] = jnp.zeros_like(so_ref)

    def seg(off):
        return pl.BlockSpec((tb, dk), wrap(lambda d, ix, pk: (pk[0], off // dk + pk[1])))

    const = lambda d, ix, pk: (0, 0)
    in_specs = [seg(OFF_Q), seg(OFF_K), seg(OFF_V), seg(OFF_G),
                pl.BlockSpec((1, dk // 2), wrap(const)), pl.BlockSpec((1, dk // 2), wrap(const)),
                pl.BlockSpec((None, 8, dk), wrap(lambda d, ix, pk: (pk[1], 0, 0))),
                pl.BlockSpec((None, tb, None, dk, RET_DV), wrap(lambda d, ix, pk: (l, pk[0], pk[1], 0, 0)))]
    args = [proj, proj, proj, proj, cos, sin, gch, s_all]
    aliases = {}
    if buf is not None:
        in_specs.append(pl.BlockSpec(memory_space=pl.ANY))
        args.append(buf)
        aliases = {len(args) - 1: 1}
    return pl.pallas_call(
        body,
        grid=grid,
        in_specs=in_specs,
        out_specs=[pl.BlockSpec((tb, dk), wrap(lambda d, ix, pk: pk)),
                   pl.BlockSpec((None, tb, None, dk, RET_DV), wrap(lambda d, ix, pk: (d, ix[0], ix[1], 0, 0)))],
        out_shape=[jax.ShapeDtypeStruct((B, RET_W), bf16),
                   jax.ShapeDtypeStruct((DEPTH, B, H, dk, RET_DV), f32)],
        input_output_aliases=aliases,
        compiler_params=_cparams(("arbitrary",) * len(grid)),
    )(*args)


RW_C = 64
RW_Q = 4
RW_NQ = RWKV_H // RW_Q
RW_GROUP = 8
RW_BLK =((RWKV_PROJ + LANE - 1) // LANE) * LANE
_RW_PKEYS = ('mu', 'w0', 'w2', 'a0', 'a2', 'g2', 'k_k', 'k_a', 'r_k', 'ln_w', 'ln_b')


def _rwkv_params(p):
    return dict(
        mu=jnp.pad(p['rwkv_mu'], (0, RW_BLK - RWKV_PROJ)).reshape(1, RW_BLK),
        w0=p['rwkv_w0'].reshape(1, RWKV_W), w2=p['rwkv_w2'],
        a0=p['rwkv_a0'].reshape(1, RWKV_W), a2=p['rwkv_a2'], g2=p['rwkv_g2'],
        k_k=p['rwkv_k_k'].reshape(1, RWKV_W), k_a=p['rwkv_k_a'].reshape(1, RWKV_W),
        r_k=p['rwkv_r_k'].reshape(1, RWKV_W),
        ln_w=p['rwkv_ln_w'].reshape(1, RWKV_W), ln_b=p['rwkv_ln_b'].reshape(1, RWKV_W))


def _split_bf16(x, terms):
    out = []
    for _ in range(terms - 1):
        hi = x.astype(bf16)
        out.append(hi)
        x = x - hi.astype(f32)
    out.append(x.astype(bf16))
    return out


def _head_sum(x):
    QW = RW_Q * RWKV_N
    r = lax.broadcasted_iota(i32, (QW, QW), 0) // RWKV_N
    c = lax.broadcasted_iota(i32, (QW, QW), 1) // RWKV_N
    ones = (r == c).astype(bf16)
    parts = _split_bf16(x, 2)
    outs = []
    for q in range(RW_NQ):
        sl = slice(q * QW, (q + 1) * QW)
        outs.append(_dot(parts[0][:, sl], ones) + _dot(parts[1][:, sl], ones))
    return jnp.concatenate(outs, axis=1)


def _rwkv_mix(rw, prev, pr):
    m = rw + (prev - rw) * pr['mu'][...]
    r = m[:, 0:RWKV_W]
    k = m[:, RWKV_W:2 * RWKV_W]
    v = m[:, 2 * RWKV_W:3 * RWKV_W]
    o = 3 * RWKV_W
    xw = m[:, o:o + DECAY_LORA]
    xa = m[:, o + DECAY_LORA:o + DECAY_LORA + AAA_LORA]
    xg = m[:, o + DECAY_LORA + AAA_LORA:o + DECAY_LORA + AAA_LORA + GATE_LORA]
    w_log = -jax.nn.softplus(-(pr['w0'][...] + _dot(jnp.tanh(xw).astype(bf16), pr['w2'][...].astype(bf16)))) - 0.5
    lw = -jnp.exp(w_log)
    a = jax.nn.sigmoid(pr['a0'][...] + _dot(xa.astype(bf16), pr['a2'][...].astype(bf16)))
    g = _dot(jax.nn.sigmoid(xg).astype(bf16), pr['g2'][...].astype(bf16))
    kk = k * pr['k_k'][...]
    kk = kk / jnp.maximum(jnp.sqrt(_head_sum(kk * kk)), 1e-12)
    kf = k * (1.0 + (a - 1.0) * pr['k_a'][...])
    return r, lw, kf, v, kk, a, g


def _rwkv_out(y, r, kf, v, g, pr):
    yc = y - _head_sum(y) * (1.0 / RWKV_N)
    yn = yc * lax.rsqrt(_head_sum(yc * yc) * (1.0 / RWKV_N) + RWKV_LN_EPS)
    yn = yn * pr['ln_w'][...] + pr['ln_b'][...]
    yn = yn + _head_sum(r * kf * pr['r_k'][...]) * v
    return yn * g


def _rwkv_seq(proj, rp, B, L):
    C = RW_C
    assert L % C == 0 and C == RWKV_N
    n = L // C
    QW = RW_Q * RWKV_N
    BC = B * C
    proj3 = proj.reshape(B, L, PROJ_W)

    def body(rw_ref, *refs):
        pr = dict(zip(_RW_PKEYS, refs[:len(_RW_PKEYS)]))
        y_ref, s_ref, sh_ref, st, carry = refs[len(_RW_PKEYS):]
        c = pl.program_id(0)

        @pl.when(c == 0)
        def _():
            st[...] = jnp.zeros_like(st)
            carry[...] = jnp.zeros_like(carry)

        rw = rw_ref[...].reshape(BC, RW_BLK)
        rolled = pltpu.roll(rw, 1, 0)
        row = lax.broadcasted_iota(i32, (C, RW_BLK), 0)
        prev = jnp.concatenate(
            [jnp.where(row == 0, carry[b, 0:1, :], rolled[b * C:(b + 1) * C]) for b in range(B)], axis=0)
        for b in range(B):
            carry[b, 0:1, :] = rw[(b + 1) * C - 1:(b + 1) * C, :]
        r, lw, kf, v, kk, a, g = _rwkv_mix(rw, prev, pr)

        ti = lax.broadcasted_iota(i32, (BC, BC), 0)
        si = lax.broadcasted_iota(i32, (BC, BC), 1)
        tril = ((si <= ti) & ((si // C) == (ti // C))).astype(bf16)
        lg = sum(_dot(tril, part) for part in _split_bf16(lw, 3))
        lgc = jnp.concatenate(
            [jnp.broadcast_to(lg[(b + 1) * C - 1:(b + 1) * C, :], (C, RWKV_W)) for b in range(B)], axis=0)
        e_neg = jnp.exp(-lg)
        e_rem = jnp.exp(lgc - lg)
        at = kk * jnp.exp(lg - lw)
        ka = kk * a
        bt = ka * e_neg
        kt = kf * e_neg
        rt = r * jnp.exp(lg)
        bh = ka * e_rem
        kh = kf * e_rem
        gcr = jnp.exp(lgc)

        rr = lax.broadcasted_iota(i32, (RW_Q * C, QW), 0)
        ll = lax.broadcasted_iota(i32, (RW_Q * C, QW), 1)
        blockmask = (rr // C) == (ll // RWKV_N)
        tt = lax.broadcasted_iota(i32, (C, QW), 0)
        ss = lax.broadcasted_iota(i32, (C, QW), 1) % C
        strict = ss < tt
        incl = ss <= tt
        eye = (ss == tt).astype(f32)

        def bd(x):
            return jnp.where(blockmask, jnp.concatenate([x] * RW_Q, axis=0), 0.0).astype(bf16)

        ys = [[None] * RW_NQ for _ in range(B)]
        chains = [(b, q) for b in range(B) for q in range(RW_NQ)]
        for g0 in range(0, len(chains), RW_GROUP):
            grp = chains[g0:g0 + RW_GROUP]
            idx = [(slice(b * C, (b + 1) * C), slice(q * QW, (q + 1) * QW)) for b, q in grp]
            each = lambda fn: [fn(i) for i in range(len(grp))]
            vq = each(lambda i: v[idx[i]])
            ar = each(lambda i: jnp.concatenate([at[idx[i]], rt[idx[i]]], axis=0).astype(bf16))
            big = each(lambda i: _dot_nt(ar[i], jnp.concatenate([bd(bt[idx[i]]), bd(kt[idx[i]])], axis=0)))
            s0 = each(lambda i: st[grp[i]])
            asrs = each(lambda i: _dot_nt(ar[i], s0[i].astype(bf16)))
            nmat = each(lambda i: jnp.where(strict, big[i][:C, :QW], 0.0))
            akm = each(lambda i: jnp.where(strict, big[i][:C, QW:], 0.0).astype(bf16))
            rbk = each(lambda i: jnp.concatenate([jnp.where(incl, big[i][C:, :QW], 0.0),
                                                  jnp.where(incl, big[i][C:, QW:], 0.0)], axis=1).astype(bf16))
            tm = each(lambda i: eye - nmat[i])
            pw = each(lambda i: _dot(nmat[i].astype(bf16), bd(nmat[i])))
            lvl = 2
            while lvl < C:
                res = each(lambda i: _dot(jnp.concatenate([tm[i], pw[i]], axis=0).astype(bf16), bd(pw[i])))
                tm = each(lambda i: tm[i] + res[i][:C])
                pw = each(lambda i: res[i][C:])
                lvl *= 2
            vbd = each(lambda i: bd(vq[i]))
            rhs = each(lambda i: -(asrs[i][:C] + _dot(akm[i], vbd[i])))
            u = each(lambda i: _dot(tm[i].astype(bf16), bd(rhs[i])))
            y = each(lambda i: asrs[i][C:] + _dot(rbk[i], jnp.concatenate([bd(u[i]), vbd[i]], axis=0)))
            upd = each(lambda i: _dot_tn(jnp.concatenate([u[i], vq[i]], axis=0).astype(bf16),
                                         jnp.concatenate([bh[idx[i]], kh[idx[i]]], axis=0).astype(bf16)))
            for i, (b, q) in enumerate(grp):
                st[b, q] = s0[i] * gcr[b * C:b * C + 1, idx[i][1]] + jnp.where(blockmask, upd[i], 0.0)
                ys[b][q] = y[i]

        y = jnp.concatenate([jnp.concatenate(yb, axis=1) for yb in ys], axis=0)
        y_ref[...] = _rwkv_out(y, r, kf, v, g, pr).astype(bf16).reshape(B, C, RWKV_W)

        @pl.when(c == n - 1)
        def _():
            for b in range(B):
                for q in range(RW_NQ):
                    for h in range(RW_Q):
                        hs = slice(h * RWKV_N, (h + 1) * RWKV_N)
                        s_ref[b, q * RW_Q + h] = st[b, q, hs, hs]
                sh_ref[b] = rw[(b + 1) * C - 1:(b + 1) * C, 0:RWKV_PROJ]

    pspecs = [pl.BlockSpec(rp[k].shape, lambda c: (0, 0)) for k in _RW_PKEYS]
    y, s, sh = pl.pallas_call(
        body,
        grid=(n,),
        in_specs=[pl.BlockSpec((pl.Element(B), pl.Element(C), pl.Element(RW_BLK)),
                               lambda c: (0, pl.multiple_of(c * C, C), OFF_RW))] + pspecs,
        out_specs=[pl.BlockSpec((B, C, RWKV_W), lambda c: (0, c, 0)),
                   pl.BlockSpec((B, RWKV_H, RWKV_N, RWKV_N), lambda c: (0, 0, 0, 0)),
                   pl.BlockSpec((B, 1, RWKV_PROJ), lambda c: (0, 0, 0))],
        out_shape=[jax.ShapeDtypeStruct((B, L, RWKV_W), bf16),
                   jax.ShapeDtypeStruct((B, RWKV_H, RWKV_N, RWKV_N), f32),
                   jax.ShapeDtypeStruct((B, 1, RWKV_PROJ), f32)],
        scratch_shapes=[pltpu.VMEM((B, RW_NQ, QW, QW), f32), pltpu.VMEM((B, 8, RW_BLK), f32)],
        compiler_params=_cparams(("arbitrary",)),
    )(proj3, *[rp[k] for k in _RW_PKEYS])
    return y.reshape(B * L, RWKV_W), s, sh.reshape(B, RWKV_PROJ)


def _rwkv_step(proj, shift, s_t, l, buf, rp):
    B = proj.shape[0]
    N = RWKV_N
    shift_p = jnp.pad(shift, ((0, 0), (0, RW_BLK - RWKV_PROJ)))
    grid, wrap = _layer_grid(l, buf, (RWKV_H,))
    npk = len(_RW_PKEYS)
    vec_names = ('r', 'w', 'kf', 'v', 'kk', 'ka')

    def body(rw_ref, sh_ref, s_ref, *refs):
        pr = dict(zip(_RW_PKEYS, refs[:npk]))
        y_ref, so_ref = refs[-9:-7]
        vt = dict(zip(vec_names, refs[-7:-1]))
        yt = refs[-1]
        h = pl.program_id(len(grid) - 1)

        def update():
            @pl.when(h == 0)
            def _():
                r, lw, kf, v, kk, a, _ = _rwkv_mix(rw_ref[...], sh_ref[...], pr)
                for name, val in zip(vec_names, (r, jnp.exp(lw), kf, v, kk, kk * a)):
                    vt[name][...] = val.T

            hs = pl.ds(pl.multiple_of(h * N, N), N)
            s = s_ref[...]
            kk_h = vt['kk'][hs, :]
            sa = jnp.sum(s * (-kk_h)[None], axis=1, keepdims=True)
            s1 = s * vt['w'][hs, :][None] + sa * vt['ka'][hs, :][None] + vt['v'][hs, :][:, None, :] * vt['kf'][hs, :][None]
            so_ref[...] = s1
            yt[hs, :] = jnp.sum(s1 * vt['r'][hs, :][None], axis=1)

            @pl.when(h == RWKV_H - 1)
            def _():
                r, lw, kf, v, kk, a, g = _rwkv_mix(rw_ref[...], sh_ref[...], pr)
                y_ref[...] = _rwkv_out(yt[...].T, r, kf, v, g, pr).astype(bf16)

        if buf is not None:
            update()
        else:
            pl.when(pl.program_id(0) == l)(update)

            @pl.when(pl.program_id(0) != l)
            def _():
                so_ref[...] = jnp.zeros_like(so_ref)

    const = lambda d, ix, pk: (0, 0)
    in_specs = [pl.BlockSpec((pl.Element(B), pl.Element(RW_BLK)), wrap(lambda d, ix, pk: (0, OFF_RW))),
                pl.BlockSpec((B, RW_BLK), wrap(const)),
                pl.BlockSpec((None, None, N, N, B), wrap(lambda d, ix, pk: (l, pk[0], 0, 0, 0)))]
    in_specs += [pl.BlockSpec(rp[k].shape, wrap(const)) for k in _RW_PKEYS]
    args = [proj, shift_p, s_t] + [rp[k] for k in _RW_PKEYS]
    aliases = {}
    if buf is not None:
        in_specs.append(pl.BlockSpec(memory_space=pl.ANY))
        args.append(buf)
        aliases = {len(args) - 1: 1}
    return pl.pallas_call(
        body,
        grid=grid,
        in_specs=in_specs,
        out_specs=[pl.BlockSpec((B, RWKV_W), wrap(const)),
                   pl.BlockSpec((None, None, N, N, B), wrap(lambda d, ix, pk: (d, ix[0], 0, 0, 0)))],
        out_shape=[jax.ShapeDtypeStruct((B, RWKV_W), bf16),
                   jax.ShapeDtypeStruct((DEPTH, RWKV_H, N, N, B), f32)],
        scratch_shapes=[pltpu.VMEM((RWKV_W, B), f32)] * 7,
        input_output_aliases=aliases,
        compiler_params=_cparams(("arbitrary",) * len(grid)),
    )(*args)


S5_N = S5_G * S5_P
S5_KC = 256
S5_NKC = S5_W // S5_KC
S5_TILES = S5_N // LANE
S5_GB = S5_KC // S5_GC


def _s5_params(p):
    a_re, a_im = p['s5_a_re'], p['s5_a_im']
    dstep = jnp.exp(p['s5_log_dt'])[:, None]
    mag = jnp.exp(a_re * dstep)
    ab_re = mag * jnp.cos(a_im * dstep)
    ab_im = mag * jnp.sin(a_im * dstep)
    den = a_re * a_re + a_im * a_im
    n_re = ab_re - 1.0
    f_re = (n_re * a_re + ab_im * a_im) / den
    f_im = (ab_im * a_re - n_re * a_im) / den
    b_re, b_im = p['s5_b_re'], p['s5_b_im']
    bb_re = f_re[..., None] * b_re - f_im[..., None] * b_im
    bb_im = f_re[..., None] * b_im + f_im[..., None] * b_re
    eye = jnp.eye(S5_GB, dtype=f32)

    def in_map(bb):
        t = bb.reshape(S5_NKC, S5_GB, S5_P, S5_GC)
        return jnp.einsum('kgpc,gh->kgchp', t, eye).reshape(S5_NKC, S5_KC, S5_GB * S5_P)

    def out_map(cc):
        t = cc.reshape(S5_NKC, S5_GB, S5_GC, S5_P)
        return jnp.einsum('qgcp,gh->qgphc', t, eye).reshape(S5_NKC, S5_GB * S5_P, S5_KC)

    return dict(wb=jnp.concatenate([in_map(bb_re), in_map(bb_im)], axis=-1),
                wc_re=out_map(p['s5_c_re']), wc_im=out_map(p['s5_c_im']),
                ab_re_t=ab_re.reshape(S5_TILES // 8, 8, LANE), ab_im_t=ab_im.reshape(S5_TILES // 8, 8, LANE),
                ab_re=ab_re.reshape(1, S5_N), ab_im=ab_im.reshape(1, S5_N), d=p['s5_d'].reshape(1, S5_W))


def _s5_seq(proj, sp, B, L):
    Lc = min(L, 256)
    n = L // Lc
    pitch = Lc + 4
    lead = OFF_U % LANE
    base = OFF_U - lead
    width = S5_W + LANE
    nt4 = S5_TILES // 8
    tiles_kc = S5_TILES // S5_NKC

    def body(u_ref, wb_ref, wcr_ref, wci_ref, abr_ref, abi_ref, d_ref, y_ref, sr_ref, si_ref, xr, xi, cr, ci):
        c = pl.program_id(1)

        @pl.when(c == 0)
        def _():
            cr[...] = jnp.zeros_like(cr)
            ci[...] = jnp.zeros_like(ci)

        u = u_ref[:, lead:lead + S5_W]
        ub = u.astype(bf16)
        for kc in range(S5_NKC):
            bu = _dot(ub[:, kc * S5_KC:(kc + 1) * S5_KC], wb_ref[kc].astype(bf16))
            for j in range(tiles_kc):
                t = kc * tiles_kc + j
                xr[pl.ds(t * pitch, Lc), :] = bu[:, j * LANE:(j + 1) * LANE]
                xi[pl.ds(t * pitch, Lc), :] = bu[:, (tiles_kc + j) * LANE:(tiles_kc + j + 1) * LANE]

        abr = [abr_ref[g] for g in range(nt4)]
        abi = [abi_ref[g] for g in range(nt4)]

        def step(t, carry):
            out = []
            for g in range(nt4):
                s_r, s_i = carry[2 * g], carry[2 * g + 1]
                idx = pl.ds(g * 8 * pitch + t, 8, stride=pitch)
                n_r = abr[g] * s_r - abi[g] * s_i + xr[idx, :]
                n_i = abr[g] * s_i + abi[g] * s_r + xi[idx, :]
                xr[idx, :] = n_r
                xi[idx, :] = n_i
                out += [n_r, n_i]
            return tuple(out)

        init = []
        for g in range(nt4):
            init += [cr[g], ci[g]]
        fin = lax.fori_loop(0, Lc, step, tuple(init))
        for g in range(nt4):
            cr[g] = fin[2 * g]
            ci[g] = fin[2 * g + 1]

        for q in range(S5_NKC):
            lr = jnp.concatenate([xr[pl.ds((q * tiles_kc + j) * pitch, Lc), :] for j in range(tiles_kc)], axis=1)
            li = jnp.concatenate([xi[pl.ds((q * tiles_kc + j) * pitch, Lc), :] for j in range(tiles_kc)], axis=1)
            y = _dot(lr.astype(bf16), wcr_ref[q].astype(bf16)) - _dot(li.astype(bf16), wci_ref[q].astype(bf16))
            cs = slice(q * S5_KC, (q + 1) * S5_KC)
            y = y + d_ref[:, cs] * u[:, cs]
            y_ref[:, cs] = jax.nn.gelu(y).astype(bf16)

        @pl.when(c == n - 1)
        def _():
            sr_ref[...] = cr[...]
            si_ref[...] = ci[...]

    full = lambda shp: pl.BlockSpec(shp, lambda b, c: (0,) * len(shp))
    y, sr, si = pl.pallas_call(
        body,
        grid=(B, n),
        in_specs=[pl.BlockSpec((pl.Element(Lc), pl.Element(width)),
                               lambda b, c: (pl.multiple_of((b * n + c) * Lc, Lc), base)),
                  full(sp['wb'].shape), full(sp['wc_re'].shape), full(sp['wc_im'].shape),
                  full(sp['ab_re_t'].shape), full(sp['ab_im_t'].shape), full((1, S5_W))],
        out_specs=[pl.BlockSpec((Lc, S5_W), lambda b, c: (b * n + c, 0)),
                   pl.BlockSpec((None, nt4, 8, LANE), lambda b, c: (b, 0, 0, 0)),
                   pl.BlockSpec((None, nt4, 8, LANE), lambda b, c: (b, 0, 0, 0))],
        out_shape=[jax.ShapeDtypeStruct((B * L, S5_W), bf16),
                   jax.ShapeDtypeStruct((B, nt4, 8, LANE), f32),
                   jax.ShapeDtypeStruct((B, nt4, 8, LANE), f32)],
        scratch_shapes=[pltpu.VMEM((S5_TILES * pitch, LANE), f32), pltpu.VMEM((S5_TILES * pitch, LANE), f32),
                        pltpu.VMEM((nt4, 8, LANE), f32), pltpu.VMEM((nt4, 8, LANE), f32)],
        compiler_params=_cparams(("parallel", "arbitrary")),
    )(proj, sp['wb'], sp['wc_re'], sp['wc_im'], sp['ab_re_t'], sp['ab_im_t'], sp['d'])
    return y, sr.reshape(B, S5_G, S5_P), si.reshape(B, S5_G, S5_P)


def _s5_step(proj, x_re, x_im, sp):
    B = proj.shape[0]
    lead = OFF_U % LANE
    base = OFF_U - lead
    width = S5_W + LANE
    kw = S5_N // S5_NKC

    def body(u_ref, xr_ref, xi_ref, wb_ref, wcr_ref, wci_ref, abr_ref, abi_ref, d_ref, y_ref, sr_ref, si_ref):
        u = u_ref[:, lead:lead + S5_W]
        ub = u.astype(bf16)
        abr, abi = abr_ref[...], abi_ref[...]
        xr, xi = xr_ref[...], xi_ref[...]
        for kc in range(S5_NKC):
            bu = _dot(ub[:, kc * S5_KC:(kc + 1) * S5_KC], wb_ref[kc].astype(bf16))
            sl = slice(kc * kw, (kc + 1) * kw)
            n_r = abr[:, sl] * xr[:, sl] - abi[:, sl] * xi[:, sl] + bu[:, :kw]
            n_i = abr[:, sl] * xi[:, sl] + abi[:, sl] * xr[:, sl] + bu[:, kw:]
            sr_ref[:, sl] = n_r
            si_ref[:, sl] = n_i
            y = _dot(n_r.astype(bf16), wcr_ref[kc].astype(bf16)) - _dot(n_i.astype(bf16), wci_ref[kc].astype(bf16))
            cs = slice(kc * S5_KC, (kc + 1) * S5_KC)
            y = y + d_ref[:, cs] * u[:, cs]
            y_ref[:, cs] = jax.nn.gelu(y).astype(bf16)

    full = lambda shp: pl.BlockSpec(shp, lambda i: (0,) * len(shp))
    y, sr, si = pl.pallas_call(
        body,
        grid=(1,),
        in_specs=[pl.BlockSpec((pl.Element(B), pl.Element(width)), lambda i: (0, base)),
                  full((B, S5_N)), full((B, S5_N)),
                  full(sp['wb'].shape), full(sp['wc_re'].shape), full(sp['wc_im'].shape),
                  full((1, S5_N)), full((1, S5_N)), full((1, S5_W))],
        out_specs=[full((B, S5_W)), full((B, S5_N)), full((B, S5_N))],
        out_shape=[jax.ShapeDtypeStruct((B, S5_W), bf16),
                   jax.ShapeDtypeStruct((B, S5_N), f32), jax.ShapeDtypeStruct((B, S5_N), f32)],
        compiler_params=_cparams(("arbitrary",)),
    )(proj, x_re.reshape(B, S5_N), x_im.reshape(B, S5_N), sp['wb'], sp['wc_re'], sp['wc_im'],
      sp['ab_re'], sp['ab_im'], sp['d'])
    return y, sr.reshape(B, S5_G, S5_P), si.reshape(B, S5_G, S5_P)


MOE_RT = 256
MOE_RT_SMALL = 32
SLAB = D // LANE
SUB = 8
SLAB_PITCH = SLAB + SUB
H2_PITCH = SLAB + SUB
O2_PITCH = TOP_K * SLAB + SUB


def _moe_rt(T):
    return MOE_RT if T * TOP_K >= N_EXPERTS * MOE_RT else MOE_RT_SMALL
ROUTE_W = LANE


def _router(x, g, mod, w_r, b_r, row0, t_all, bufs):
    T = x.shape[0]
    tm = min(mod.tm, 512)
    nm = T // tm
    assert row0 % tm == 0
    if bufs is None:
        assert row0 == 0
        n_steps = pl.cdiv(t_all, tm)
    else:
        n_steps = nm
    clamp = lambda m: jnp.minimum(m, nm - 1)

    def body(x_ref, g_ref, sh_ref, sc_ref, w_ref, b_ref, *rest):
        h_ref, e_ref, p_ref = rest[-3:]
        if n_steps > nm:
            @pl.when(pl.program_id(0) >= nm)
            def _():
                h_ref[...] = jnp.zeros_like(h_ref)
                e_ref[...] = jnp.zeros_like(e_ref)
                p_ref[...] = jnp.zeros_like(p_ref)

            pl.when(pl.program_id(0) < nm)(lambda: route(x_ref, g_ref, sh_ref, sc_ref, w_ref, b_ref, *rest[-3:]))
        else:
            route(x_ref, g_ref, sh_ref, sc_ref, w_ref, b_ref, *rest[-3:])

    def route(x_ref, g_ref, sh_ref, sc_ref, w_ref, b_ref, h_ref, e_ref, p_ref):
        h2 = _rms(x_ref[...], g_ref[...]) * (1.0 + sc_ref[...]) + sh_ref[...]
        for s in range(SLAB):
            h_ref[pl.ds(s, tm, stride=H2_PITCH), :] = h2[:, s * LANE:(s + 1) * LANE]
        for s in range(SLAB, H2_PITCH):
            h_ref[pl.ds(s, tm, stride=H2_PITCH), :] = jnp.zeros((tm, LANE), f32)
        logits = jnp.dot(h2, w_ref[...], precision=lax.Precision.HIGHEST, preferred_element_type=f32) + b_ref[...]
        lane = lax.broadcasted_iota(i32, (tm, ROUTE_W), 1)
        ninf = jnp.float32(-jnp.inf)
        gl = jnp.where(lane < N_GROUPS, logits, ninf)
        gm = jnp.max(gl, axis=-1, keepdims=True)
        g_p = 1.0 / jnp.sum(jnp.exp(gl - gm), axis=-1, keepdims=True)
        g_idx = jnp.min(jnp.where(gl == gm, lane, ROUTE_W), axis=-1, keepdims=True)
        valid = (lane >= N_GROUPS) & (lane < N_GROUPS + N_EXPERTS) & (((lane - N_GROUPS) // EPG) == g_idx)
        el = jnp.where(valid, logits, ninf)
        ee = jnp.exp(el - jnp.max(el, axis=-1, keepdims=True))
        prob = jnp.where(valid, ee / jnp.sum(ee, axis=-1, keepdims=True), -1.0)
        p1 = jnp.max(prob, axis=-1, keepdims=True)
        i1 = jnp.min(jnp.where(prob == p1, lane, ROUTE_W), axis=-1, keepdims=True)
        prob2 = jnp.where(lane == i1, -1.0, prob)
        p2 = jnp.max(prob2, axis=-1, keepdims=True)
        i2 = jnp.min(jnp.where(prob2 == p2, lane, ROUTE_W), axis=-1, keepdims=True)
        den = p1 + p2
        e_ref[...] = jnp.where(lane == 0, i1 - N_GROUPS, jnp.where(lane == 1, i2 - N_GROUPS, 0))
        p_ref[...] = jnp.where(lane == 0, g_p * p1 / den, jnp.where(lane == 1, g_p * p2 / den, 0.0))

    in_specs = [pl.BlockSpec((tm, D), lambda m: (clamp(m), 0)),
                pl.BlockSpec((1, D), lambda m: (0, 0)),
                mod.row_spec(3, tm, clamp), mod.row_spec(4, tm, clamp),
                pl.BlockSpec((D, ROUTE_W), lambda m: (0, 0)),
                pl.BlockSpec((1, ROUTE_W), lambda m: (0, 0))]
    args = [x, g.reshape(1, D), mod.arr, mod.arr, w_r, b_r]
    aliases = {}
    if bufs is not None:
        in_specs += [pl.BlockSpec(memory_space=pl.ANY)] * 3
        aliases = {len(args) + i: i for i in range(3)}
        args += list(bufs)
    b0 = row0 // tm
    return pl.pallas_call(
        body,
        grid=(n_steps,),
        in_specs=in_specs,
        out_specs=[pl.BlockSpec((tm * H2_PITCH, LANE), lambda m: (b0 + m, 0)),
                   pl.BlockSpec((tm, ROUTE_W), lambda m: (b0 + m, 0)),
                   pl.BlockSpec((tm, ROUTE_W), lambda m: (b0 + m, 0))],
        out_shape=[jax.ShapeDtypeStruct((t_all * H2_PITCH, LANE), f32), jax.ShapeDtypeStruct((t_all, ROUTE_W), i32),
                   jax.ShapeDtypeStruct((t_all, ROUTE_W), f32)],
        input_output_aliases=aliases,
        compiler_params=_cparams(("arbitrary",)),
    )(*args)


TOK_BITS = 14


def _moe_plan(eid, T):
    A = T * TOP_K
    RT = _moe_rt(T)
    nt = A // RT + N_EXPERTS
    flat_e = eid[:, :TOP_K].reshape(-1)
    order = jnp.argsort(flat_e).astype(i32)
    counts = jnp.sum((flat_e[:, None] == jnp.arange(N_EXPERTS, dtype=i32)[None, :]).astype(i32), axis=0)
    start = jnp.cumsum(counts) - counts
    pcnt = (counts + RT - 1) // RT * RT
    pend = jnp.cumsum(pcnt)
    tile_e = jnp.minimum(jnp.sum((pend[None, :] <= (jnp.arange(nt, dtype=i32) * RT)[:, None]).astype(i32), axis=1),
                         N_EXPERTS - 1)
    off = (jnp.arange(nt, dtype=i32) * RT - (pend - pcnt)[tile_e])[:, None] + jnp.arange(RT, dtype=i32)[None, :]
    real = off < counts[tile_e][:, None]
    srt = start[tile_e][:, None] + jnp.minimum(off, counts[tile_e][:, None])
    a = order[jnp.minimum(srt, A - 1)]
    slot = jnp.arange(nt * RT, dtype=i32).reshape(nt, RT)
    tok = jnp.where(real, a // TOP_K, 0)
    dst = jnp.where(real, a, A + slot - srt)
    n_used = (pend[-1] // RT).astype(i32).reshape(1)
    return (tok | (dst << TOK_BITS)).reshape(-1), tile_e, n_used


def _experts(h2, packed, tile_e, n_used, l, w1, w3, w2):
    T = h2.shape[0] // H2_PITCH
    assert TOP_K == 2 and T < (1 << TOK_BITS)
    RT = _moe_rt(T)
    PITCH = SLAB_PITCH
    nt = tile_e.shape[0]
    prime_id = nt * RT
    out_rows = (prime_id + RT) // TOP_K
    HC, OC = 128, 256
    n_hc, n_oc = D_EXPERT // HC, D // OC
    g_per, s_per = RT // n_hc, RT // n_oc

    def body(tile_ref, nused_ref, slot_ref, h_hbm, w1_ref, w3_ref, w2_ref, o_hbm,
             x0, x1, o0, o1, w1b, w3b, w2b, gsem, ssem):
        j = pl.program_id(0)
        n_used = nused_ref[0]
        active = j < n_used

        def gather_row(x_ref, r, tok):
            src = pl.multiple_of(tok * H2_PITCH, SUB)
            return pltpu.make_async_copy(h_hbm.at[pl.ds(src, SLAB), :], x_ref.at[pl.ds(r * PITCH, SLAB), :], gsem)

        def scatter_row(o_ref, r, d):
            dst = pl.multiple_of(lax.shift_right_logical(d, 1) * O2_PITCH + (d & (TOP_K - 1)) * SLAB, SUB)
            return pltpu.make_async_copy(o_ref.at[pl.ds(r * PITCH, SLAB), :], o_hbm.at[pl.ds(dst, SLAB), :], ssem)

        def wait_rows(sem):
            pltpu.make_async_copy(h_hbm.at[pl.ds(0, RT * SLAB), :], x0.at[pl.ds(0, RT * SLAB), :], sem).wait()

        wait_gather = lambda: wait_rows(gsem)
        wait_scatter = lambda: wait_rows(ssem)

        tok_of = lambda s: s & ((1 << TOK_BITS) - 1)
        dst_of = lambda s: lax.shift_right_logical(s, TOK_BITS)

        @pl.when(j == 0)
        def _():
            o1[...] = jnp.zeros_like(o1)

            def one(r, _):
                gather_row(x0, r, tok_of(slot_ref[r])).start()
                return 0
            lax.fori_loop(0, RT, one, 0)

        @pl.when(active & ((j == 0) | (tile_ref[j] != tile_ref[jnp.maximum(j - 1, 0)])))
        def _():
            w1b[...] = w1_ref[...].astype(bf16)
            w3b[...] = w3_ref[...].astype(bf16)
            w2b[...] = w2_ref[...].astype(bf16)

        @pl.when(active & (j >= 1))
        def _():
            wait_scatter()

        def tile(x_cur, x_nxt, o_cur, o_prv):
            wait_gather()
            xb = jnp.concatenate([x_cur[pl.ds(s, RT, stride=PITCH), :] for s in range(SLAB)], axis=1).astype(bf16)
            nxt = jnp.minimum(j + 1, n_used - 1) * RT
            prv = jnp.maximum(j - 1, 0) * RT
            parts = []
            for c in range(n_hc):
                h1 = _dot(xb, w1b[:, c * HC:(c + 1) * HC])
                h3 = _dot(xb, w3b[:, c * HC:(c + 1) * HC])
                parts.append(((h1 * jax.nn.sigmoid(h1)) * h3).astype(bf16))
                for r in range(c * g_per, (c + 1) * g_per):
                    gather_row(x_nxt, r, tok_of(slot_ref[nxt + r])).start(priority=r % 2)
            hm = jnp.concatenate(parts, axis=1)
            for c in range(n_oc):
                res = _dot(hm, w2b[:, c * OC:(c + 1) * OC])
                for t in range(OC // LANE):
                    o_cur[pl.ds(c * (OC // LANE) + t, RT, stride=PITCH), :] = res[:, t * LANE:(t + 1) * LANE]
                for r in range(c * s_per, (c + 1) * s_per):
                    d = jnp.where(j == 0, prime_id + r, dst_of(slot_ref[prv + r]))
                    scatter_row(o_prv, r, d).start(priority=r % 2)

        def last(o_cur):
            wait_scatter()

            def one(r, _):
                scatter_row(o_cur, r, dst_of(slot_ref[j * RT + r])).start()
                return 0
            lax.fori_loop(0, RT, one, 0)
            wait_scatter()
            wait_gather()

        for par, (xc, xn, oc, op) in enumerate(((x0, x1, o0, o1), (x1, x0, o1, o0))):
            pl.when(active & (j % 2 == par))(lambda xc=xc, xn=xn, oc=oc, op=op: tile(xc, xn, oc, op))
        for par, oc in enumerate((o0, o1)):
            pl.when((j == n_used - 1) & (j % 2 == par))(lambda oc=oc: last(oc))

    grid_spec = pltpu.PrefetchScalarGridSpec(
        num_scalar_prefetch=3,
        grid=(nt,),
        in_specs=[pl.BlockSpec(memory_space=pl.ANY),
                  pl.BlockSpec((None, None, D, D_EXPERT), lambda j, te, *_: (l, te[j], 0, 0)),
                  pl.BlockSpec((None, None, D, D_EXPERT), lambda j, te, *_: (l, te[j], 0, 0)),
                  pl.BlockSpec((None, None, D_EXPERT, D), lambda j, te, *_: (l, te[j], 0, 0))],
        out_specs=pl.BlockSpec(memory_space=pl.ANY),
        scratch_shapes=[pltpu.VMEM((RT * PITCH, LANE), f32), pltpu.VMEM((RT * PITCH, LANE), f32),
                        pltpu.VMEM((RT * PITCH, LANE), f32), pltpu.VMEM((RT * PITCH, LANE), f32),
                        pltpu.VMEM((D, D_EXPERT), bf16), pltpu.VMEM((D, D_EXPERT), bf16),
                        pltpu.VMEM((D_EXPERT, D), bf16),
                        pltpu.SemaphoreType.DMA(()), pltpu.SemaphoreType.DMA(())],
    )
    return pl.pallas_call(
        body,
        grid_spec=grid_spec,
        out_shape=jax.ShapeDtypeStruct((out_rows * O2_PITCH, LANE), f32),
        compiler_params=_cparams(("arbitrary",)),
    )(tile_e, n_used, packed, h2, w1, w3, w2)


def _moe_combine(x, o2, wgt, mod, norm_final, row0):
    T = x.shape[0]
    tm = min(mod.tm, 512)
    b0 = row0 // tm

    def body(x_ref, o_ref, p_ref, m_ref, *rest):
        p = p_ref[...]
        row = lambda k: jnp.concatenate(
            [o_ref[pl.ds(k * SLAB + s, tm, stride=O2_PITCH), :] for s in range(SLAB)], axis=1)
        y = p[:, 0:1] * row(0)
        for k in range(1, TOP_K):
            y = y + p[:, k:k + 1] * row(k)
        xo = x_ref[...] + m_ref[...] * y
        if norm_final is None:
            rest[0][...] = xo
        else:
            rest[1][...] = _rms(xo, rest[0][...])

    ins = [x, o2, wgt, mod.arr]
    specs = [pl.BlockSpec((tm, D), lambda m: (m, 0)),
             pl.BlockSpec((tm * O2_PITCH, LANE), lambda m: (b0 + m, 0)),
             pl.BlockSpec((tm, ROUTE_W), lambda m: (b0 + m, 0)), mod.row_spec(5, tm)]
    if norm_final is not None:
        ins.append(norm_final.reshape(1, D))
        specs.append(pl.BlockSpec((1, D), lambda m: (0, 0)))
    return pl.pallas_call(
        body,
        grid=(T // tm,),
        in_specs=specs,
        out_specs=pl.BlockSpec((tm, D), lambda m: (m, 0)),
        out_shape=jax.ShapeDtypeStruct((T, D), f32),
        compiler_params=_cparams(("parallel",)),
    )(*ins)


def _moe(groups, p, l, big, norm_final):
    pad = ROUTE_W - N_GROUPS - N_EXPERTS
    w_r = jnp.pad(jnp.concatenate([p['moe_w_group'], p['moe_w_router']], axis=1), ((0, 0), (0, pad)))
    b_r = jnp.pad(jnp.concatenate([p['moe_b_group'], p['moe_b_router']]), (0, pad)).reshape(1, ROUTE_W)
    t_all = sum(g.T for g in groups)
    bufs, row0 = None, 0
    for g in groups:
        g.row0 = row0
        bufs = _router(g.x, p['norm_ffn'], g.mod, w_r, b_r, row0, t_all, bufs)
        row0 += g.T
    h2, eid, wgt = bufs
    packed, tile_e, n_used = _moe_plan(eid, t_all)
    o2 = _experts(h2, packed, tile_e, n_used, l, big['moe_w1'], big['moe_w3'], big['moe_w2'])
    for g in groups:
        g.x = _moe_combine(g.x, o2, wgt, g.mod, norm_final, g.row0)


class _Group:
    def __init__(self, x, mods, states, pos0):
        self.B, self.L, _ = x.shape
        self.T = self.B * self.L
        self.tm = _row_tile(self.T)
        self.x = x.reshape(self.T, D)
        self.mods, self.states, self.pos0 = mods, states, pos0
        self.outs = ([], [], [], [], [])
        self.ret_all = self.rw_all = None

    def mix(self, l, p, rp, sp, layers, big, w_in_t):
        B, L = self.B, self.L
        self.mod = mod = _Mod(self.mods[l], L, self.tm)
        proj = _in_proj(self.x, layers['norm_mix'], l, mod, w_in_t)
        if self.states is None:
            y_ret, s_ret = _retention_seq(proj, B, L)
            y_rw, s_rw, shift = _rwkv_seq(proj, rp, B, L)
            y_s5, s_re, s_im = _s5_seq(proj, sp, B, L)
        else:
            st_ret, st_rw, st_shift, st_re, st_im = self.states
            y_ret, self.ret_all = _retention_step(proj, st_ret, l, self.ret_all, self.pos0)
            y_rw, self.rw_all = _rwkv_step(proj, st_shift[l], jnp.transpose(st_rw, (0, 2, 3, 4, 1)), l,
                                           self.rw_all, rp)
            s_ret = s_rw = None
            shift = proj[:, OFF_RW:OFF_RW + RWKV_PROJ]
            y_s5, s_re, s_im = _s5_step(proj, st_re[l], st_im[l], sp)
        z = _glu_proj(y_s5, l, big['s5_w_glu'])
        merged = _merge_proj(y_ret, y_rw, z, l, big['ret_w_o'], big['rwkv_w_o'], big['s5_w_o'], proj)
        self.x = _out_proj(merged, l, big['w_out'], self.x, mod)
        for lst, val in zip(self.outs, (s_ret, s_rw, shift, s_re, s_im)):
            lst.append(val)

    def results(self):
        stacked = [jnp.stack(o) if o[0] is not None else None for o in self.outs]
        if self.states is not None:
            stacked[0], stacked[1] = self.ret_all, jnp.transpose(self.rw_all, (0, 4, 1, 2, 3))
        return self.x.reshape(self.B, self.L, D), stacked


def _trunk(groups, layers, norm_final):
    big = {k: layers[k] for k in _BIG}
    w_in_t = jnp.swapaxes(layers['w_in'], 1, 2)
    for l in range(DEPTH):
        p = {name: arr[l] for name, arr in layers.items() if name not in _BIG + ('w_in',)}
        rp, sp = _rwkv_params(p), _s5_params(p)
        for g in groups:
            g.mix(l, p, rp, sp, layers, big, w_in_t)
        _moe(groups, p, l, big, norm_final if l == DEPTH - 1 else None)
    return [g.results() for g in groups]


_BIG = ('ret_w_o', 'rwkv_w_o', 's5_w_glu', 's5_w_o', 'w_out', 'moe_w1', 'moe_w3', 'moe_w2')


def kernel(x_prompt, x_sample, state_ret, state_rwkv, state_shift, state_s5_re, state_s5_im,
           c_prompt, c_sample, norm_mix, norm_ffn, w_ada, b_ada, w_in, ret_w_o, rwkv_mu, rwkv_w0,
           rwkv_w2, rwkv_a0, rwkv_a2, rwkv_g2, rwkv_k_k, rwkv_k_a, rwkv_r_k, rwkv_ln_w, rwkv_ln_b,
           rwkv_w_o, s5_a_re, s5_a_im, s5_b_re, s5_b_im, s5_c_re, s5_c_im, s5_d, s5_log_dt, s5_w_glu,
           s5_w_o, w_out, moe_w_group, moe_b_group, moe_w_router, moe_b_router, moe_w1, moe_w3, moe_w2,
           norm_final):
    layers = {
        'norm_mix': norm_mix, 'norm_ffn': norm_ffn, 'w_in': w_in,
        'ret_w_o': ret_w_o, 'rwkv_mu': rwkv_mu, 'rwkv_w0': rwkv_w0, 'rwkv_w2': rwkv_w2,
        'rwkv_a0': rwkv_a0, 'rwkv_a2': rwkv_a2, 'rwkv_g2': rwkv_g2, 'rwkv_k_k': rwkv_k_k,
        'rwkv_k_a': rwkv_k_a, 'rwkv_r_k': rwkv_r_k, 'rwkv_ln_w': rwkv_ln_w, 'rwkv_ln_b': rwkv_ln_b,
        'rwkv_w_o': rwkv_w_o, 's5_a_re': s5_a_re, 's5_a_im': s5_a_im, 's5_b_re': s5_b_re,
        's5_b_im': s5_b_im, 's5_c_re': s5_c_re, 's5_c_im': s5_c_im, 's5_d': s5_d,
        's5_log_dt': s5_log_dt, 's5_w_glu': s5_w_glu, 's5_w_o': s5_w_o, 'w_out': w_out,
        'moe_w_group': moe_w_group, 'moe_b_group': moe_b_group, 'moe_w_router': moe_w_router,
        'moe_b_router': moe_b_router, 'moe_w1': moe_w1, 'moe_w3': moe_w3, 'moe_w2': moe_w2,
    }
    Bp, Bs = x_prompt.shape[0], x_sample.shape[0]
    s_off = -(-Bp // 8) * 8
    c_all = jnp.concatenate([c_prompt, jnp.zeros((s_off - Bp, D), f32), c_sample], axis=0)
    mod_all = _adaln(c_all, w_ada, b_ada)
    prompt = _Group(x_prompt, mod_all[:, :Bp], None, 0.0)
    sample = _Group(x_sample, mod_all[:, s_off:s_off + Bs],
                    (state_ret, state_rwkv, state_shift, state_s5_re, state_s5_im), float(PAST_LEN))
    ((y_prompt, (ret_p, rwkv_p, shift_p, s5re_p, s5im_p)),
     (y_sample, (ret_s, rwkv_s, shift_s, s5re_s, s5im_s))) = _trunk([prompt, sample], layers, norm_final)
    return (y_prompt, y_sample, ret_p, ret_s, rwkv_p, rwkv_s, shift_p, shift_s, s5re_p, s5re_s, s5im_p, s5im_s)
```

```python
import jax
import jax.numpy as jnp
from jax import lax
from jax.experimental import pallas as pl
from jax.experimental.pallas import tpu as pltpu

f32 = jnp.float32
bf16 = jnp.bfloat16
i32 = jnp.int32

D = 2048
DEPTH = 2
PAST_LEN = 16384
RET_W, RET_H, RET_DK, RET_DV, RET_CHUNK = 1024, 4, 256, 256, 128
RET_GN_EPS = 1e-6
ROPE_BASE = 10000.0
RWKV_W, RWKV_N, RWKV_H = 1024, 64, 16
DECAY_LORA, AAA_LORA, GATE_LORA = 64, 64, 160
RWKV_PROJ = 3 * RWKV_W + DECAY_LORA + AAA_LORA + GATE_LORA
RWKV_LN_EPS = 64e-5
S5_W, S5_GC, S5_G, S5_P = 1024, 16, 64, 64
N_MOD = 6
RMS_EPS = 1e-6
N_GROUPS, EPG, N_EXPERTS, TOP_K, D_EXPERT = 4, 8, 32, 2, 512
IN_W = 4 * RET_W + RWKV_PROJ + S5_W + 3 * D
OFF_Q, OFF_K, OFF_V, OFF_G, OFF_RW = 0, 1024, 2048, 3072, 4096
OFF_U = OFF_RW + RWKV_PROJ
OFF_GATE = OFF_U + S5_W
LANE = 128
PROJ_W = ((IN_W + LANE - 1) // LANE) * LANE
VMEM_LIMIT = 56 * 1024 * 1024


def _cparams(sem):
    return pltpu.CompilerParams(dimension_semantics=sem, vmem_limit_bytes=VMEM_LIMIT)


def _dot(a, b):
    return jnp.dot(a, b, preferred_element_type=f32)


def _dot_nt(a, b):
    return lax.dot_general(a, b, (((1,), (1,)), ((), ())), preferred_element_type=f32)


def _dot_tn(a, b):
    return lax.dot_general(a, b, (((0,), (0,)), ((), ())), preferred_element_type=f32)


def _rms(x, g):
    return x * lax.rsqrt(jnp.mean(x * x, axis=-1, keepdims=True) + RMS_EPS) * g


def _head_norm(y, eps):
    mu = jnp.mean(y, axis=-1, keepdims=True)
    yc = y - mu
    return yc * lax.rsqrt(jnp.mean(yc * yc, axis=-1, keepdims=True) + eps)


def _row_tile(T):
    return 1024 if T >= 1024 else T


class _Mod:
    def __init__(self, mod, L, tm):
        self.L, self.tm = L, tm
        self.per_token = L == 1
        self.arr = mod if self.per_token else mod.reshape(mod.shape[0], 1, N_MOD * D)

    def spec(self, j, tn, col_of):
        nb = D // tn
        if self.per_token:
            return pl.BlockSpec((self.tm, tn), lambda m, n: (m, j * nb + col_of(n)))
        L, tm = self.L, self.tm
        return pl.BlockSpec((None, 1, tn), lambda m, n: ((m * tm) // L, 0, j * nb + col_of(n)))

    def row_spec(self, j, tm, remap=lambda m: m):
        if self.per_token:
            return pl.BlockSpec((tm, D), lambda m: (remap(m), j))
        L = self.L
        return pl.BlockSpec((None, 1, D), lambda m: ((remap(m) * tm) // L, 0, j))


def _fused_mm(x_ops, w_ops, e_ops, pre, post, *, grid, out_specs, out_shape, cache_shapes):
    nx, nw, ne = len(x_ops), len(w_ops), len(e_ops)
    n_out = len(out_shape)

    def body(*refs):
        x_refs = refs[:nx]
        w_refs = refs[nx:nx + nw]
        e_refs = refs[nx + nw:nx + nw + ne]
        o_refs = refs[nx + nw + ne:nx + nw + ne + n_out]
        caches = refs[nx + nw + ne + n_out:]
        if cache_shapes:
            @pl.when(pl.program_id(1) == 0)
            def _():
                for i in range(nx):
                    caches[i][...] = pre(i, x_refs[i], e_refs).astype(bf16)
            lhs = [c[...] for c in caches]
        else:
            lhs = [x[...] for x in x_refs]
        prods = [(_dot_nt if len(w_ops[j]) > 3 and w_ops[j][3] else _dot)(
            lhs[w_ops[j][2]], w_refs[j][...].astype(bf16)) for j in range(nw)]
        for o_ref, o in zip(o_refs, post(prods, e_refs)):
            o_ref[...] = o.astype(o_ref.dtype)

    return pl.pallas_call(
        body,
        grid=grid,
        in_specs=[s for _, s in x_ops] + [w[1] for w in w_ops] + [s for _, s in e_ops],
        out_specs=out_specs,
        out_shape=out_shape,
        scratch_shapes=[pltpu.VMEM(s, bf16) for s in cache_shapes],
        compiler_params=_cparams(("parallel", "arbitrary")),
    )(*[a for a, _ in x_ops], *[w[0] for w in w_ops], *[a for a, _ in e_ops])


def _adaln(c_all, w_ada, b_ada):
    R = c_all.shape[0]
    tn = 1024

    def pre(i, x_ref, e_refs):
        c = x_ref[...]
        return c * jax.nn.sigmoid(c)

    def post(prods, e_refs):
        return (prods[0] + e_refs[0][...],)

    (out,) = _fused_mm(
        [(c_all, pl.BlockSpec((R, D), lambda l, n: (0, 0)))],
        [(w_ada, pl.BlockSpec((None, D, tn), lambda l, n: (l, 0, n)), 0)],
        [(b_ada.reshape(DEPTH, 1, N_MOD * D), pl.BlockSpec((None, 1, tn), lambda l, n: (l, 0, n)))],
        pre, post,
        grid=(DEPTH, N_MOD * D // tn),
        out_specs=[pl.BlockSpec((None, R, tn), lambda l, n: (l, 0, n))],
        out_shape=[jax.ShapeDtypeStruct((DEPTH, R, N_MOD * D), f32)],
        cache_shapes=[(R, D)],
    )
    return out


def _in_proj(x, g, l, mod, w_in_t):
    T = x.shape[0]
    tm, tn = mod.tm, 1024
    tr = min(tm, 512)

    def norm_body(x_ref, g_ref, shift_ref, scale_ref, h_ref):
        h_ref[...] = (_rms(x_ref[...], g_ref[...]) * (1.0 + scale_ref[...]) + shift_ref[...]).astype(bf16)

    h = pl.pallas_call(
        norm_body,
        grid=(T // tr,),
        in_specs=[pl.BlockSpec((tr, D), lambda m: (m, 0)),
                  pl.BlockSpec((None, 1, D), lambda m: (l, 0, 0)),
                  mod.row_spec(0, tr), mod.row_spec(1, tr)],
        out_specs=pl.BlockSpec((tr, D), lambda m: (m, 0)),
        out_shape=jax.ShapeDtypeStruct((T, D), bf16),
        compiler_params=_cparams(("parallel",)),
    )(x, g.reshape(DEPTH, 1, D), mod.arr, mod.arr)

    def mm_body(h_ref, w_ref, o_ref, wb):
        @pl.when(pl.program_id(1) == 0)
        def _():
            wb[...] = w_ref[...].astype(bf16)

        col = pl.program_id(0) * tn + lax.broadcasted_iota(i32, (tm, tn), 1)
        o_ref[...] = jnp.where(col < IN_W, _dot_nt(h_ref[...], wb[...]), 0.0)

    return pl.pallas_call(
        mm_body,
        grid=(pl.cdiv(PROJ_W, tn), T // tm),
        in_specs=[pl.BlockSpec((tm, D), lambda n, m: (m, 0)),
                  pl.BlockSpec((None, tn, D), lambda n, m: (l, n, 0))],
        out_specs=pl.BlockSpec((tm, tn), lambda n, m: (m, n)),
        out_shape=jax.ShapeDtypeStruct((T, PROJ_W), f32),
        scratch_shapes=[pltpu.VMEM((tn, D), bf16)],
        compiler_params=_cparams(("parallel", "arbitrary")),
    )(h, w_in_t)


def _glu_proj(yg, l, w_glu):
    T = yg.shape[0]
    tm, tn = _row_tile(T), 512
    nb = S5_W // tn

    def post(prods, e_refs):
        return (prods[0] * jax.nn.sigmoid(prods[1]),)

    (out,) = _fused_mm(
        [(yg, pl.BlockSpec((tm, S5_W), lambda m, n: (m, 0)))],
        [(w_glu, pl.BlockSpec((None, S5_W, tn), lambda m, n: (l, 0, n)), 0),
         (w_glu, pl.BlockSpec((None, S5_W, tn), lambda m, n: (l, 0, nb + n)), 0)],
        [], None, post,
        grid=(T // tm, nb),
        out_specs=[pl.BlockSpec((tm, tn), lambda m, n: (m, n))],
        out_shape=[jax.ShapeDtypeStruct((T, S5_W), bf16)],
        cache_shapes=[],
    )
    return out


def _merge_proj(y_ret, y_rw, y_s5, l, w_ret, w_rw, w_s5, proj):
    T = y_ret.shape[0]
    tm, tn = _row_tile(T), 512
    lead = OFF_GATE % LANE
    base = OFF_GATE - lead

    def gate_spec(i):
        return pl.BlockSpec((pl.Element(tm), pl.Element(tn + LANE)),
                            lambda m, n: (pl.multiple_of(m * tm, tm), pl.multiple_of(base + i * D + n * tn, LANE)))

    def post(prods, e_refs):
        acc = None
        for p, e in zip(prods, e_refs):
            t = jax.nn.sigmoid(e[:, lead:lead + tn]) * p
            acc = t if acc is None else acc + t
        return (acc,)

    xspec = pl.BlockSpec((tm, RET_W), lambda m, n: (m, 0))
    wspec = pl.BlockSpec((None, RET_W, tn), lambda m, n: (l, 0, n))
    (out,) = _fused_mm(
        [(y_ret, xspec), (y_rw, xspec), (y_s5, xspec)],
        [(w_ret, wspec, 0), (w_rw, wspec, 1), (w_s5, wspec, 2)],
        [(proj, gate_spec(0)), (proj, gate_spec(1)), (proj, gate_spec(2))],
        None, post,
        grid=(T // tm, D // tn),
        out_specs=[pl.BlockSpec((tm, tn), lambda m, n: (m, n))],
        out_shape=[jax.ShapeDtypeStruct((T, D), bf16)],
        cache_shapes=[],
    )
    return out


def _out_proj(merged, l, w_out, x, mod):
    T = x.shape[0]
    tm, tn = mod.tm, 512

    def post(prods, e_refs):
        x_ref, m_ref = e_refs
        return (x_ref[...] + m_ref[...] * prods[0],)

    (out,) = _fused_mm(
        [(merged, pl.BlockSpec((tm, D), lambda m, n: (m, 0)))],
        [(w_out, pl.BlockSpec((None, D, tn), lambda m, n: (l, 0, n)), 0)],
        [(x, pl.BlockSpec((tm, tn), lambda m, n: (m, n))), (mod.arr, mod.spec(2, tn, lambda n: n))],
        None, post,
        grid=(T // tm, D // tn),
        out_specs=[pl.BlockSpec((tm, tn), lambda m, n: (m, n))],
        out_shape=[jax.ShapeDtypeStruct((T, D), f32)],
        cache_shapes=[],
    )
    return out


def _ret_consts(L, pos0):
    C = RET_CHUNK if L % RET_CHUNK == 0 else L
    H = RET_H
    log_g = jnp.log1p(-jnp.exp2(-5.0 - jnp.arange(H, dtype=f32)))
    i = jnp.arange(C, dtype=f32)
    diff = i[:, None] - i[None, :]
    causal = diff >= 0
    dmask = jnp.where(causal, jnp.exp(jnp.where(causal, diff, 0.0)[None] * log_g[:, None, None]), 0.0)
    kdec = jnp.exp((C - 1.0 - i)[:, None] * log_g[None, :])
    qdec = jnp.exp((i + 1.0)[:, None] * log_g[None, :])
    g_chunk = jnp.exp(C * log_g)
    half = RET_DK // 2
    inv = ROPE_BASE ** (-jnp.arange(half, dtype=f32) / half)
    pos = pos0 + jnp.arange(L, dtype=f32)
    ang = pos[:, None] * inv[None, :]
    return C, dmask, kdec, qdec, g_chunk, jnp.cos(ang), jnp.sin(ang)


def _rotary(x, cos, sin):
    half = RET_DK // 2
    x1, x2 = x[..., :half], x[..., half:]
    return jnp.concatenate([x1 * cos - x2 * sin, x1 * sin + x2 * cos], axis=-1)


def _retention_seq(proj, B, L):
    C, dmask, kdec, qdec, g_chunk, cos, sin = _ret_consts(L, 0.0)
    H, dk = RET_H, RET_DK
    n = L // C
    kdec_f = jnp.broadcast_to(kdec.T[:, :, None], (H, C, dk))
    qdec_f = jnp.broadcast_to(qdec.T[:, :, None], (H, C, dk))
    gch_f = jnp.broadcast_to(g_chunk[:, None, None], (H, 8, dk))
    proj3 = proj.reshape(B, L, PROJ_W)

    def body(q_ref, k_ref, v_ref, g_ref, cos_ref, sin_ref, dm_ref, kd_ref, qd_ref, gc_ref, y_ref, s_ref, st):
        c = pl.program_id(1)

        @pl.when(c == 0)
        def _():
            st[...] = jnp.zeros_like(st)

        cs, sn = cos_ref[...], sin_ref[...]
        for h in range(H):
            sl = slice(h * dk, (h + 1) * dk)
            q = _rotary(q_ref[:, sl], cs, sn)
            k = _rotary(k_ref[:, sl], cs, sn) * (dk ** -0.5)
            vb = v_ref[:, sl].astype(bf16)
            s0 = st[h]
            scores = _dot_nt(q.astype(bf16), k.astype(bf16)) * dm_ref[h]
            o = _dot(scores.astype(bf16), vb)
            o = o + _dot((q * qd_ref[h]).astype(bf16), s0.astype(bf16))
            kv = _dot_tn((k * kd_ref[h]).astype(bf16), vb)
            st[h] = s0 * gc_ref[h, 0:1, :] + kv
            g = g_ref[:, sl]
            y_ref[:, sl] = (g * jax.nn.sigmoid(g) * _head_norm(o, RET_GN_EPS)).astype(bf16)

        @pl.when(c == n - 1)
        def _():
            s_ref[...] = st[...]

    def seg(off):
        return pl.BlockSpec((None, C, RET_W), lambda b, c: (b, c, off // RET_W))

    const3 = lambda shp: pl.BlockSpec(shp, lambda b, c: (0, 0, 0))
    y, s = pl.pallas_call(
        body,
        grid=(B, n),
        in_specs=[seg(OFF_Q), seg(OFF_K), seg(OFF_V), seg(OFF_G),
                  pl.BlockSpec((C, dk // 2), lambda b, c: (c, 0)),
                  pl.BlockSpec((C, dk // 2), lambda b, c: (c, 0)),
                  const3((H, C, C)), const3((H, C, dk)), const3((H, C, dk)), const3((H, 8, dk))],
        out_specs=[pl.BlockSpec((None, C, RET_W), lambda b, c: (b, c, 0)),
                   pl.BlockSpec((None, H, dk, RET_DV), lambda b, c: (b, 0, 0, 0))],
        out_shape=[jax.ShapeDtypeStruct((B, L, RET_W), bf16),
                   jax.ShapeDtypeStruct((B, H, dk, RET_DV), f32)],
        scratch_shapes=[pltpu.VMEM((H, dk, RET_DV), f32)],
        compiler_params=_cparams(("parallel", "arbitrary")),
    )(proj3, proj3, proj3, proj3, cos, sin, dmask, kdec_f, qdec_f, gch_f)
    return y.reshape(B * L, RET_W), s


STEP_TB = 16


def _layer_grid(l, buf, inner):
    if buf is not None:
        return inner, (lambda fn: (lambda *ix: fn(l, ix, ix)))
    assert l == 0
    last = tuple(n - 1 for n in inner)

    def wrap(fn):
        def index_map(d, *ix):
            parked = tuple(jnp.where(d == l, i, z) for i, z in zip(ix, last))
            return fn(d, ix, parked)
        return index_map
    return (DEPTH,) + inner, wrap


def _retention_step(proj, s_all, l, buf, pos0):
    B = proj.shape[0]
    _, _, _, _, g_chunk, cos, sin = _ret_consts(1, pos0)
    H, dk = RET_H, RET_DK
    gch = jnp.broadcast_to(g_chunk[:, None, None], (H, 8, dk))
    tb = STEP_TB
    grid, wrap = _layer_grid(l, buf, (B // tb, H))

    def body(q_ref, k_ref, v_ref, g_ref, cos_ref, sin_ref, gc_ref, s_ref, *rest):
        y_ref, so_ref = rest[-2:]

        def update():
            cs, sn = cos_ref[...], sin_ref[...]
            q = _rotary(q_ref[...], cs, sn)
            k = _rotary(k_ref[...], cs, sn) * (dk ** -0.5)
            v = v_ref[...]
            s1 = s_ref[...] * gc_ref[0:1, :][None] + k[:, :, None] * v[:, None, :]
            so_ref[...] = s1
            o = jnp.sum(q[:, :, None] * s1, axis=1)
            g = g_ref[...]
            y_ref[...] = (g * jax.nn.sigmoid(g) * _head_norm(o, RET_GN_EPS)).astype(bf16)

        if buf is not None:
            update()
        else:
            pl.when(pl.program_id(0) == l)(update)

            @pl.when(pl.program_id(0) != l)
            def _():
                so_ref[...] = jnp.zeros_like(so_ref)

    def seg(off):
        return pl.BlockSpec((tb, dk), wrap(lambda d, ix, pk: (pk[0], off // dk + pk[1])))

    const = lambda d, ix, pk: (0, 0)
    in_specs = [seg(OFF_Q), seg(OFF_K), seg(OFF_V), seg(OFF_G),
                pl.BlockSpec((1, dk // 2), wrap(const)), pl.BlockSpec((1, dk // 2), wrap(const)),
                pl.BlockSpec((None, 8, dk), wrap(lambda d, ix, pk: (pk[1], 0, 0))),
                pl.BlockSpec((None, tb, None, dk, RET_DV), wrap(lambda d, ix, pk: (l, pk[0], pk[1], 0, 0)))]
    args = [proj, proj, proj, proj, cos, sin, gch, s_all]
    aliases = {}
    if buf is not None:
        in_specs.append(pl.BlockSpec(memory_space=pl.ANY))
        args.append(buf)
        aliases = {len(args) - 1: 1}
    return pl.pallas_call(
        body,
        grid=grid,
        in_specs=in_specs,
        out_specs=[pl.BlockSpec((tb, dk), wrap(lambda d, ix, pk: pk)),
                   pl.BlockSpec((None, tb, None, dk, RET_DV), wrap(lambda d, ix, pk: (d, ix[0], ix[1], 0, 0)))],
        out_shape=[jax.ShapeDtypeStruct((B, RET_W), bf16),
                   jax.ShapeDtypeStruct((DEPTH, B, H, dk, RET_DV), f32)],
        input_output_aliases=aliases,
        compiler_params=_cparams(("arbitrary",) * len(grid)),
    )(*args)


RW_C = 64
RW_Q = 4
RW_NQ = RWKV_H // RW_Q
RW_GROUP = 8
RW_BLK =((RWKV_PROJ + LANE - 1) // LANE) * LANE
_RW_PKEYS = ('mu', 'w0', 'w2', 'a0', 'a2', 'g2', 'k_k', 'k_a', 'r_k', 'ln_w', 'ln_b')


def _rwkv_params(p):
    return dict(
        mu=jnp.pad(p['rwkv_mu'], (0, RW_BLK - RWKV_PROJ)).reshape(1, RW_BLK),
        w0=p['rwkv_w0'].reshape(1, RWKV_W), w2=p['rwkv_w2'],
        a0=p['rwkv_a0'].reshape(1, RWKV_W), a2=p['rwkv_a2'], g2=p['rwkv_g2'],
        k_k=p['rwkv_k_k'].reshape(1, RWKV_W), k_a=p['rwkv_k_a'].reshape(1, RWKV_W),
        r_k=p['rwkv_r_k'].reshape(1, RWKV_W),
        ln_w=p['rwkv_ln_w'].reshape(1, RWKV_W), ln_b=p['rwkv_ln_b'].reshape(1, RWKV_W))


def _split_bf16(x, terms):
    out = []
    for _ in range(terms - 1):
        hi = x.astype(bf16)
        out.append(hi)
        x = x - hi.astype(f32)
    out.append(x.astype(bf16))
    return out


def _head_sum(x):
    QW = RW_Q * RWKV_N
    r = lax.broadcasted_iota(i32, (QW, QW), 0) // RWKV_N
    c = lax.broadcasted_iota(i32, (QW, QW), 1) // RWKV_N
    ones = (r == c).astype(bf16)
    parts = _split_bf16(x, 2)
    outs = []
    for q in range(RW_NQ):
        sl = slice(q * QW, (q + 1) * QW)
        outs.append(_dot(parts[0][:, sl], ones) + _dot(parts[1][:, sl], ones))
    return jnp.concatenate(outs, axis=1)


def _rwkv_mix(rw, prev, pr):
    m = rw + (prev - rw) * pr['mu'][...]
    r = m[:, 0:RWKV_W]
    k = m[:, RWKV_W:2 * RWKV_W]
    v = m[:, 2 * RWKV_W:3 * RWKV_W]
    o = 3 * RWKV_W
    xw = m[:, o:o + DECAY_LORA]
    xa = m[:, o + DECAY_LORA:o + DECAY_LORA + AAA_LORA]
    xg = m[:, o + DECAY_LORA + AAA_LORA:o + DECAY_LORA + AAA_LORA + GATE_LORA]
    w_log = -jax.nn.softplus(-(pr['w0'][...] + _dot(jnp.tanh(xw).astype(bf16), pr['w2'][...].astype(bf16)))) - 0.5
    lw = -jnp.exp(w_log)
    a = jax.nn.sigmoid(pr['a0'][...] + _dot(xa.astype(bf16), pr['a2'][...].astype(bf16)))
    g = _dot(jax.nn.sigmoid(xg).astype(bf16), pr['g2'][...].astype(bf16))
    kk = k * pr['k_k'][...]
    kk = kk / jnp.maximum(jnp.sqrt(_head_sum(kk * kk)), 1e-12)
    kf = k * (1.0 + (a - 1.0) * pr['k_a'][...])
    return r, lw, kf, v, kk, a, g


def _rwkv_out(y, r, kf, v, g, pr):
    yc = y - _head_sum(y) * (1.0 / RWKV_N)
    yn = yc * lax.rsqrt(_head_sum(yc * yc) * (1.0 / RWKV_N) + RWKV_LN_EPS)
    yn = yn * pr['ln_w'][...] + pr['ln_b'][...]
    yn = yn + _head_sum(r * kf * pr['r_k'][...]) * v
    return yn * g


def _rwkv_seq(proj, rp, B, L):
    C = RW_C
    assert L % C == 0 and C == RWKV_N
    n = L // C
    QW = RW_Q * RWKV_N
    BC = B * C
    proj3 = proj.reshape(B, L, PROJ_W)

    def body(rw_ref, *refs):
        pr = dict(zip(_RW_PKEYS, refs[:len(_RW_PKEYS)]))
        y_ref, s_ref, sh_ref, st, carry = refs[len(_RW_PKEYS):]
        c = pl.program_id(0)

        @pl.when(c == 0)
        def _():
            st[...] = jnp.zeros_like(st)
            carry[...] = jnp.zeros_like(carry)

        rw = rw_ref[...].reshape(BC, RW_BLK)
        rolled = pltpu.roll(rw, 1, 0)
        row = lax.broadcasted_iota(i32, (C, RW_BLK), 0)
        prev = jnp.concatenate(
            [jnp.where(row == 0, carry[b, 0:1, :], rolled[b * C:(b + 1) * C]) for b in range(B)], axis=0)
        for b in range(B):
            carry[b, 0:1, :] = rw[(b + 1) * C - 1:(b + 1) * C, :]
        r, lw, kf, v, kk, a, g = _rwkv_mix(rw, prev, pr)

        ti = lax.broadcasted_iota(i32, (BC, BC), 0)
        si = lax.broadcasted_iota(i32, (BC, BC), 1)
        tril = ((si <= ti) & ((si // C) == (ti // C))).astype(bf16)
        lg = sum(_dot(tril, part) for part in _split_bf16(lw, 3))
        lgc = jnp.concatenate(
            [jnp.broadcast_to(lg[(b + 1) * C - 1:(b + 1) * C, :], (C, RWKV_W)) for b in range(B)], axis=0)
        e_neg = jnp.exp(-lg)
        e_rem = jnp.exp(lgc - lg)
        at = kk * jnp.exp(lg - lw)
        ka = kk * a
        bt = ka * e_neg
        kt = kf * e_neg
        rt = r * jnp.exp(lg)
        bh = ka * e_rem
        kh = kf * e_rem
        gcr = jnp.exp(lgc)

        rr = lax.broadcasted_iota(i32, (RW_Q * C, QW), 0)
        ll = lax.broadcasted_iota(i32, (RW_Q * C, QW), 1)
        blockmask = (rr // C) == (ll // RWKV_N)
        tt = lax.broadcasted_iota(i32, (C, QW), 0)
        ss = lax.broadcasted_iota(i32, (C, QW), 1) % C
        strict = ss < tt
        incl = ss <= tt
        eye = (ss == tt).astype(f32)

        def bd(x):
            return jnp.where(blockmask, jnp.concatenate([x] * RW_Q, axis=0), 0.0).astype(bf16)

        ys = [[None] * RW_NQ for _ in range(B)]
        chains = [(b, q) for b in range(B) for q in range(RW_NQ)]
        for g0 in range(0, len(chains), RW_GROUP):
            grp = chains[g0:g0 + RW_GROUP]
            idx = [(slice(b * C, (b + 1) * C), slice(q * QW, (q + 1) * QW)) for b, q in grp]
            each = lambda fn: [fn(i) for i in range(len(grp))]
            vq = each(lambda i: v[idx[i]])
            ar = each(lambda i: jnp.concatenate([at[idx[i]], rt[idx[i]]], axis=0).astype(bf16))
            big = each(lambda i: _dot_nt(ar[i], jnp.concatenate([bd(bt[idx[i]]), bd(kt[idx[i]])], axis=0)))
            s0 = each(lambda i: st[grp[i]])
            asrs = each(lambda i: _dot_nt(ar[i], s0[i].astype(bf16)))
            nmat = each(lambda i: jnp.where(strict, big[i][:C, :QW], 0.0))
            akm = each(lambda i: jnp.where(strict, big[i][:C, QW:], 0.0).astype(bf16))
            rbk = each(lambda i: jnp.concatenate([jnp.where(incl, big[i][C:, :QW], 0.0),
                                                  jnp.where(incl, big[i][C:, QW:], 0.0)], axis=1).astype(bf16))
            tm = each(lambda i: eye - nmat[i])
            pw = each(lambda i: _dot(nmat[i].astype(bf16), bd(nmat[i])))
            lvl = 2
            while lvl < C:
                res = each(lambda i: _dot(jnp.concatenate([tm[i], pw[i]], axis=0).astype(bf16), bd(pw[i])))
                tm = each(lambda i: tm[i] + res[i][:C])
                pw = each(lambda i: res[i][C:])
                lvl *= 2
            vbd = each(lambda i: bd(vq[i]))
            rhs = each(lambda i: -(asrs[i][:C] + _dot(akm[i], vbd[i])))
            u = each(lambda i: _dot(tm[i].astype(bf16), bd(rhs[i])))
            y = each(lambda i: asrs[i][C:] + _dot(rbk[i], jnp.concatenate([bd(u[i]), vbd[i]], axis=0)))
            upd = each(lambda i: _dot_tn(jnp.concatenate([u[i], vq[i]], axis=0).astype(bf16),
                                         jnp.concatenate([bh[idx[i]], kh[idx[i]]], axis=0).astype(bf16)))
            for i, (b, q) in enumerate(grp):
                st[b, q] = s0[i] * gcr[b * C:b * C + 1, idx[i][1]] + jnp.where(blockmask, upd[i], 0.0)
                ys[b][q] = y[i]

        y = jnp.concatenate([jnp.concatenate(yb, axis=1) for yb in ys], axis=0)
        y_ref[...] = _rwkv_out(y, r, kf, v, g, pr).astype(bf16).reshape(B, C, RWKV_W)

        @pl.when(c == n - 1)
        def _():
            for b in range(B):
                for q in range(RW_NQ):
                    for h in range(RW_Q):
                        hs = slice(h * RWKV_N, (h + 1) * RWKV_N)
                        s_ref[b, q * RW_Q + h] = st[b, q, hs, hs]
                sh_ref[b] = rw[(b + 1) * C - 1:(b + 1) * C, 0:RWKV_PROJ]

    pspecs = [pl.BlockSpec(rp[k].shape, lambda c: (0, 0)) for k in _RW_PKEYS]
    y, s, sh = pl.pallas_call(
        body,
        grid=(n,),
        in_specs=[pl.BlockSpec((pl.Element(B), pl.Element(C), pl.Element(RW_BLK)),
                               lambda c: (0, pl.multiple_of(c * C, C), OFF_RW))] + pspecs,
        out_specs=[pl.BlockSpec((B, C, RWKV_W), lambda c: (0, c, 0)),
                   pl.BlockSpec((B, RWKV_H, RWKV_N, RWKV_N), lambda c: (0, 0, 0, 0)),
                   pl.BlockSpec((B, 1, RWKV_PROJ), lambda c: (0, 0, 0))],
        out_shape=[jax.ShapeDtypeStruct((B, L, RWKV_W), bf16),
                   jax.ShapeDtypeStruct((B, RWKV_H, RWKV_N, RWKV_N), f32),
                   jax.ShapeDtypeStruct((B, 1, RWKV_PROJ), f32)],
        scratch_shapes=[pltpu.VMEM((B, RW_NQ, QW, QW), f32), pltpu.VMEM((B, 8, RW_BLK), f32)],
        compiler_params=_cparams(("arbitrary",)),
    )(proj3, *[rp[k] for k in _RW_PKEYS])
    return y.reshape(B * L, RWKV_W), s, sh.reshape(B, RWKV_PROJ)


def _rwkv_step(proj, shift, s_t, l, buf, rp):
    B = proj.shape[0]
    N = RWKV_N
    shift_p = jnp.pad(shift, ((0, 0), (0, RW_BLK - RWKV_PROJ)))
    grid, wrap = _layer_grid(l, buf, (RWKV_H,))
    npk = len(_RW_PKEYS)
    vec_names = ('r', 'w', 'kf', 'v', 'kk', 'ka')

    def body(rw_ref, sh_ref, s_ref, *refs):
        pr = dict(zip(_RW_PKEYS, refs[:npk]))
        y_ref, so_ref = refs[-9:-7]
        vt = dict(zip(vec_names, refs[-7:-1]))
        yt = refs[-1]
        h = pl.program_id(len(grid) - 1)

        def update():
            @pl.when(h == 0)
            def _():
                r, lw, kf, v, kk, a, _ = _rwkv_mix(rw_ref[...], sh_ref[...], pr)
                for name, val in zip(vec_names, (r, jnp.exp(lw), kf, v, kk, kk * a)):
                    vt[name][...] = val.T

            hs = pl.ds(pl.multiple_of(h * N, N), N)
            s = s_ref[...]
            kk_h = vt['kk'][hs, :]
            sa = jnp.sum(s * (-kk_h)[None], axis=1, keepdims=True)
            s1 = s * vt['w'][hs, :][None] + sa * vt['ka'][hs, :][None] + vt['v'][hs, :][:, None, :] * vt['kf'][hs, :][None]
            so_ref[...] = s1
            yt[hs, :] = jnp.sum(s1 * vt['r'][hs, :][None], axis=1)

            @pl.when(h == RWKV_H - 1)
            def _():
                r, lw, kf, v, kk, a, g = _rwkv_mix(rw_ref[...], sh_ref[...], pr)
                y_ref[...] = _rwkv_out(yt[...].T, r, kf, v, g, pr).astype(bf16)

        if buf is not None:
            update()
        else:
            pl.when(pl.program_id(0) == l)(update)

            @pl.when(pl.program_id(0) != l)
            def _():
                so_ref[...] = jnp.zeros_like(so_ref)

    const = lambda d, ix, pk: (0, 0)
    in_specs = [pl.BlockSpec((pl.Element(B), pl.Element(RW_BLK)), wrap(lambda d, ix, pk: (0, OFF_RW))),
                pl.BlockSpec((B, RW_BLK), wrap(const)),
                pl.BlockSpec((None, None, N, N, B), wrap(lambda d, ix, pk: (l, pk[0], 0, 0, 0)))]
    in_specs += [pl.BlockSpec(rp[k].shape, wrap(const)) for k in _RW_PKEYS]
    args = [proj, shift_p, s_t] + [rp[k] for k in _RW_PKEYS]
    aliases = {}
    if buf is not None:
        in_specs.append(pl.BlockSpec(memory_space=pl.ANY))
        args.append(buf)
        aliases = {len(args) - 1: 1}
    return pl.pallas_call(
        body,
        grid=grid,
        in_specs=in_specs,
        out_specs=[pl.BlockSpec((B, RWKV_W), wrap(const)),
                   pl.BlockSpec((None, None, N, N, B), wrap(lambda d, ix, pk: (d, ix[0], 0, 0, 0)))],
        out_shape=[jax.ShapeDtypeStruct((B, RWKV_W), bf16),
                   jax.ShapeDtypeStruct((DEPTH, RWKV_H, N, N, B), f32)],
        scratch_shapes=[pltpu.VMEM((RWKV_W, B), f32)] * 7,
        input_output_aliases=aliases,
        compiler_params=_cparams(("arbitrary",) * len(grid)),
    )(*args)


S5_N = S5_G * S5_P
S5_KC = 256
S5_NKC = S5_W // S5_KC
S5_TILES = S5_N // LANE
S5_GB = S5_KC // S5_GC


def _s5_params(p):
    a_re, a_im = p['s5_a_re'], p['s5_a_im']
    dstep = jnp.exp(p['s5_log_dt'])[:, None]
    mag = jnp.exp(a_re * dstep)
    ab_re = mag * jnp.cos(a_im * dstep)
    ab_im = mag * jnp.sin(a_im * dstep)
    den = a_re * a_re + a_im * a_im
    n_re = ab_re - 1.0
    f_re = (n_re * a_re + ab_im * a_im) / den
    f_im = (ab_im * a_re - n_re * a_im) / den
    b_re, b_im = p['s5_b_re'], p['s5_b_im']
    bb_re = f_re[..., None] * b_re - f_im[..., None] * b_im
    bb_im = f_re[..., None] * b_im + f_im[..., None] * b_re
    eye = jnp.eye(S5_GB, dtype=f32)

    def in_map(bb):
        t = bb.reshape(S5_NKC, S5_GB, S5_P, S5_GC)
        return jnp.einsum('kgpc,gh->kgchp', t, eye).reshape(S5_NKC, S5_KC, S5_GB * S5_P)

    def out_map(cc):
        t = cc.reshape(S5_NKC, S5_GB, S5_GC, S5_P)
        return jnp.einsum('qgcp,gh->qgphc', t, eye).reshape(S5_NKC, S5_GB * S5_P, S5_KC)

    return dict(wb=jnp.concatenate([in_map(bb_re), in_map(bb_im)], axis=-1),
                wc_re=out_map(p['s5_c_re']), wc_im=out_map(p['s5_c_im']),
                ab_re_t=ab_re.reshape(S5_TILES // 8, 8, LANE), ab_im_t=ab_im.reshape(S5_TILES // 8, 8, LANE),
                ab_re=ab_re.reshape(1, S5_N), ab_im=ab_im.reshape(1, S5_N), d=p['s5_d'].reshape(1, S5_W))


def _s5_seq(proj, sp, B, L):
    Lc = min(L, 256)
    n = L // Lc
    pitch = Lc + 4
    lead = OFF_U % LANE
    base = OFF_U - lead
    width = S5_W + LANE
    nt4 = S5_TILES // 8
    tiles_kc = S5_TILES // S5_NKC

    def body(u_ref, wb_ref, wcr_ref, wci_ref, abr_ref, abi_ref, d_ref, y_ref, sr_ref, si_ref, xr, xi, cr, ci):
        c = pl.program_id(1)

        @pl.when(c == 0)
        def _():
            cr[...] = jnp.zeros_like(cr)
            ci[...] = jnp.zeros_like(ci)

        u = u_ref[:, lead:lead + S5_W]
        ub = u.astype(bf16)
        for kc in range(S5_NKC):
            bu = _dot(ub[:, kc * S5_KC:(kc + 1) * S5_KC], wb_ref[kc].astype(bf16))
            for j in range(tiles_kc):
                t = kc * tiles_kc + j
                xr[pl.ds(t * pitch, Lc), :] = bu[:, j * LANE:(j + 1) * LANE]
                xi[pl.ds(t * pitch, Lc), :] = bu[:, (tiles_kc + j) * LANE:(tiles_kc + j + 1) * LANE]

        abr = [abr_ref[g] for g in range(nt4)]
        abi = [abi_ref[g] for g in range(nt4)]

        def step(t, carry):
            out = []
            for g in range(nt4):
                s_r, s_i = carry[2 * g], carry[2 * g + 1]
                idx = pl.ds(g * 8 * pitch + t, 8, stride=pitch)
                n_r = abr[g] * s_r - abi[g] * s_i + xr[idx, :]
                n_i = abr[g] * s_i + abi[g] * s_r + xi[idx, :]
                xr[idx, :] = n_r
                xi[idx, :] = n_i
                out += [n_r, n_i]
            return tuple(out)

        init = []
        for g in range(nt4):
            init += [cr[g], ci[g]]
        fin = lax.fori_loop(0, Lc, step, tuple(init), unroll=2)
        for g in range(nt4):
            cr[g] = fin[2 * g]
            ci[g] = fin[2 * g + 1]

        for q in range(S5_NKC):
            lr = jnp.concatenate([xr[pl.ds((q * tiles_kc + j) * pitch, Lc), :] for j in range(tiles_kc)], axis=1)
            li = jnp.concatenate([xi[pl.ds((q * tiles_kc + j) * pitch, Lc), :] for j in range(tiles_kc)], axis=1)
            y = _dot(lr.astype(bf16), wcr_ref[q].astype(bf16)) - _dot(li.astype(bf16), wci_ref[q].astype(bf16))
            cs = slice(q * S5_KC, (q + 1) * S5_KC)
            y = y + d_ref[:, cs] * u[:, cs]
            y_ref[:, cs] = jax.nn.gelu(y).astype(bf16)

        @pl.when(c == n - 1)
        def _():
            sr_ref[...] = cr[...]
            si_ref[...] = ci[...]

    full = lambda shp: pl.BlockSpec(shp, lambda b, c: (0,) * len(shp))
    y, sr, si = pl.pallas_call(
        body,
        grid=(B, n),
        in_specs=[pl.BlockSpec((pl.Element(Lc), pl.Element(width)),
                               lambda b, c: (pl.multiple_of((b * n + c) * Lc, Lc), base)),
                  full(sp['wb'].shape), full(sp['wc_re'].shape), full(sp['wc_im'].shape),
                  full(sp['ab_re_t'].shape), full(sp['ab_im_t'].shape), full((1, S5_W))],
        out_specs=[pl.BlockSpec((Lc, S5_W), lambda b, c: (b * n + c, 0)),
                   pl.BlockSpec((None, nt4, 8, LANE), lambda b, c: (b, 0, 0, 0)),
                   pl.BlockSpec((None, nt4, 8, LANE), lambda b, c: (b, 0, 0, 0))],
        out_shape=[jax.ShapeDtypeStruct((B * L, S5_W), bf16),
                   jax.ShapeDtypeStruct((B, nt4, 8, LANE), f32),
                   jax.ShapeDtypeStruct((B, nt4, 8, LANE), f32)],
        scratch_shapes=[pltpu.VMEM((S5_TILES * pitch, LANE), f32), pltpu.VMEM((S5_TILES * pitch, LANE), f32),
                        pltpu.VMEM((nt4, 8, LANE), f32), pltpu.VMEM((nt4, 8, LANE), f32)],
        compiler_params=_cparams(("parallel", "arbitrary")),
    )(proj, sp['wb'], sp['wc_re'], sp['wc_im'], sp['ab_re_t'], sp['ab_im_t'], sp['d'])
    return y, sr.reshape(B, S5_G, S5_P), si.reshape(B, S5_G, S5_P)


def _s5_step(proj, x_re, x_im, sp):
    B = proj.shape[0]
    lead = OFF_U % LANE
    base = OFF_U - lead
    width = S5_W + LANE
    kw = S5_N // S5_NKC

    def body(u_ref, xr_ref, xi_ref, wb_ref, wcr_ref, wci_ref, abr_ref, abi_ref, d_ref, y_ref, sr_ref, si_ref):
        u = u_ref[:, lead:lead + S5_W]
        ub = u.astype(bf16)
        abr, abi = abr_ref[...], abi_ref[...]
        xr, xi = xr_ref[...], xi_ref[...]
        for kc in range(S5_NKC):
            bu = _dot(ub[:, kc * S5_KC:(kc + 1) * S5_KC], wb_ref[kc].astype(bf16))
            sl = slice(kc * kw, (kc + 1) * kw)
            n_r = abr[:, sl] * xr[:, sl] - abi[:, sl] * xi[:, sl] + bu[:, :kw]
            n_i = abr[:, sl] * xi[:, sl] + abi[:, sl] * xr[:, sl] + bu[:, kw:]
            sr_ref[:, sl] = n_r
            si_ref[:, sl] = n_i
            y = _dot(n_r.astype(bf16), wcr_ref[kc].astype(bf16)) - _dot(n_i.astype(bf16), wci_ref[kc].astype(bf16))
            cs = slice(kc * S5_KC, (kc + 1) * S5_KC)
            y = y + d_ref[:, cs] * u[:, cs]
            y_ref[:, cs] = jax.nn.gelu(y).astype(bf16)

    full = lambda shp: pl.BlockSpec(shp, lambda i: (0,) * len(shp))
    y, sr, si = pl.pallas_call(
        body,
        grid=(1,),
        in_specs=[pl.BlockSpec((pl.Element(B), pl.Element(width)), lambda i: (0, base)),
                  full((B, S5_N)), full((B, S5_N)),
                  full(sp['wb'].shape), full(sp['wc_re'].shape), full(sp['wc_im'].shape),
                  full((1, S5_N)), full((1, S5_N)), full((1, S5_W))],
        out_specs=[full((B, S5_W)), full((B, S5_N)), full((B, S5_N))],
        out_shape=[jax.ShapeDtypeStruct((B, S5_W), bf16),
                   jax.ShapeDtypeStruct((B, S5_N), f32), jax.ShapeDtypeStruct((B, S5_N), f32)],
        compiler_params=_cparams(("arbitrary",)),
    )(proj, x_re.reshape(B, S5_N), x_im.reshape(B, S5_N), sp['wb'], sp['wc_re'], sp['wc_im'],
      sp['ab_re'], sp['ab_im'], sp['d'])
    return y, sr.reshape(B, S5_G, S5_P), si.reshape(B, S5_G, S5_P)


MOE_RT_SMALL, MOE_RT_MAX = 32, 512
SLAB = D // LANE
SUB = 8
SLAB_PITCH = SLAB + SUB
H2_PITCH = SLAB + SUB
O2_PITCH = TOP_K * SLAB + SUB


def _moe_rt(T):
    rt = int(1.1 * T * TOP_K / N_EXPERTS / 2)
    return min(max(-(-rt // 16) * 16, MOE_RT_SMALL), MOE_RT_MAX)
ROUTE_W = LANE


def _router(x, g, mod, w_r, b_r, row0, t_all, bufs):
    T = x.shape[0]
    tm = min(mod.tm, 512)
    nm = T // tm
    assert row0 % tm == 0
    if bufs is None:
        assert row0 == 0
        n_steps = pl.cdiv(t_all, tm)
    else:
        n_steps = nm
    clamp = lambda m: jnp.minimum(m, nm - 1)

    def body(x_ref, g_ref, sh_ref, sc_ref, w_ref, b_ref, *rest):
        h_ref, e_ref, p_ref = rest[-3:]
        if n_steps > nm:
            @pl.when(pl.program_id(0) >= nm)
            def _():
                h_ref[...] = jnp.zeros_like(h_ref)
                e_ref[...] = jnp.zeros_like(e_ref)
                p_ref[...] = jnp.zeros_like(p_ref)

            pl.when(pl.program_id(0) < nm)(lambda: route(x_ref, g_ref, sh_ref, sc_ref, w_ref, b_ref, *rest[-3:]))
        else:
            route(x_ref, g_ref, sh_ref, sc_ref, w_ref, b_ref, *rest[-3:])

    def route(x_ref, g_ref, sh_ref, sc_ref, w_ref, b_ref, h_ref, e_ref, p_ref):
        h2 = _rms(x_ref[...], g_ref[...]) * (1.0 + sc_ref[...]) + sh_ref[...]
        for s in range(SLAB):
            h_ref[pl.ds(s, tm, stride=H2_PITCH), :] = h2[:, s * LANE:(s + 1) * LANE]
        for s in range(SLAB, H2_PITCH):
            h_ref[pl.ds(s, tm, stride=H2_PITCH), :] = jnp.zeros((tm, LANE), f32)
        logits = jnp.dot(h2, w_ref[...], precision=lax.Precision.HIGHEST, preferred_element_type=f32) + b_ref[...]
        lane = lax.broadcasted_iota(i32, (tm, ROUTE_W), 1)
        ninf = jnp.float32(-jnp.inf)
        gl = jnp.where(lane < N_GROUPS, logits, ninf)
        gm = jnp.max(gl, axis=-1, keepdims=True)
        g_p = 1.0 / jnp.sum(jnp.exp(gl - gm), axis=-1, keepdims=True)
        g_idx = jnp.min(jnp.where(gl == gm, lane, ROUTE_W), axis=-1, keepdims=True)
        valid = (lane >= N_GROUPS) & (lane < N_GROUPS + N_EXPERTS) & (((lane - N_GROUPS) // EPG) == g_idx)
        el = jnp.where(valid, logits, ninf)
        ee = jnp.exp(el - jnp.max(el, axis=-1, keepdims=True))
        prob = jnp.where(valid, ee / jnp.sum(ee, axis=-1, keepdims=True), -1.0)
        p1 = jnp.max(prob, axis=-1, keepdims=True)
        i1 = jnp.min(jnp.where(prob == p1, lane, ROUTE_W), axis=-1, keepdims=True)
        prob2 = jnp.where(lane == i1, -1.0, prob)
        p2 = jnp.max(prob2, axis=-1, keepdims=True)
        i2 = jnp.min(jnp.where(prob2 == p2, lane, ROUTE_W), axis=-1, keepdims=True)
        den = p1 + p2
        e_ref[...] = jnp.where(lane == 0, i1 - N_GROUPS, jnp.where(lane == 1, i2 - N_GROUPS, 0))
        p_ref[...] = jnp.where(lane == 0, g_p * p1 / den, jnp.where(lane == 1, g_p * p2 / den, 0.0))

    in_specs = [pl.BlockSpec((tm, D), lambda m: (clamp(m), 0)),
                pl.BlockSpec((1, D), lambda m: (0, 0)),
                mod.row_spec(3, tm, clamp), mod.row_spec(4, tm, clamp),
                pl.BlockSpec((D, ROUTE_W), lambda m: (0, 0)),
                pl.BlockSpec((1, ROUTE_W), lambda m: (0, 0))]
    args = [x, g.reshape(1, D), mod.arr, mod.arr, w_r, b_r]
    aliases = {}
    if bufs is not None:
        in_specs += [pl.BlockSpec(memory_space=pl.ANY)] * 3
        aliases = {len(args) + i: i for i in range(3)}
        args += list(bufs)
    b0 = row0 // tm
    return pl.pallas_call(
        body,
        grid=(n_steps,),
        in_specs=in_specs,
        out_specs=[pl.BlockSpec((tm * H2_PITCH, LANE), lambda m: (b0 + m, 0)),
                   pl.BlockSpec((tm, ROUTE_W), lambda m: (b0 + m, 0)),
                   pl.BlockSpec((tm, ROUTE_W), lambda m: (b0 + m, 0))],
        out_shape=[jax.ShapeDtypeStruct((t_all * H2_PITCH, LANE), f32), jax.ShapeDtypeStruct((t_all, ROUTE_W), i32),
                   jax.ShapeDtypeStruct((t_all, ROUTE_W), f32)],
        input_output_aliases=aliases,
        compiler_params=_cparams(("arbitrary",)),
    )(*args)


TOK_BITS = 14


def _moe_plan(eid, T):
    A = T * TOP_K
    RT = _moe_rt(T)
    nt = pl.cdiv(A, RT) + N_EXPERTS
    flat_e = eid[:, :TOP_K].reshape(-1)
    order = jnp.argsort(flat_e).astype(i32)
    counts = jnp.sum((flat_e[:, None] == jnp.arange(N_EXPERTS, dtype=i32)[None, :]).astype(i32), axis=0)
    start = jnp.cumsum(counts) - counts
    pcnt = (counts + RT - 1) // RT * RT
    pend = jnp.cumsum(pcnt)
    tile_e = jnp.minimum(jnp.sum((pend[None, :] <= (jnp.arange(nt, dtype=i32) * RT)[:, None]).astype(i32), axis=1),
                         N_EXPERTS - 1)
    off = (jnp.arange(nt, dtype=i32) * RT - (pend - pcnt)[tile_e])[:, None] + jnp.arange(RT, dtype=i32)[None, :]
    real = off < counts[tile_e][:, None]
    srt = start[tile_e][:, None] + jnp.minimum(off, counts[tile_e][:, None])
    a = order[jnp.minimum(srt, A - 1)]
    slot = jnp.arange(nt * RT, dtype=i32).reshape(nt, RT)
    tok = jnp.where(real, a // TOP_K, 0)
    dst = jnp.where(real, a, A + slot - srt)
    n_used = (pend[-1] // RT).astype(i32).reshape(1)
    return (tok | (dst << TOK_BITS)).reshape(-1), tile_e, n_used


def _experts(h2, packed, tile_e, n_used, l, w1, w3, w2):
    T = h2.shape[0] // H2_PITCH
    assert TOP_K == 2 and T < (1 << TOK_BITS)
    RT = _moe_rt(T)
    PITCH = SLAB_PITCH
    nt = tile_e.shape[0]
    prime_id = nt * RT
    out_rows = (prime_id + RT) // TOP_K
    HC, OC = 128, 256
    n_hc, n_oc = D_EXPERT // HC, D // OC
    g_per, s_per = RT // n_hc, RT // n_oc

    def body(tile_ref, nused_ref, slot_ref, h_hbm, w1_ref, w3_ref, w2_ref, o_hbm,
             x0, x1, o0, o1, w1b, w3b, w2b, gsem, ssem):
        j = pl.program_id(0)
        n_used = nused_ref[0]
        active = j < n_used

        def gather_row(x_ref, r, tok):
            src = pl.multiple_of(tok * H2_PITCH, SUB)
            return pltpu.make_async_copy(h_hbm.at[pl.ds(src, SLAB), :], x_ref.at[pl.ds(r * PITCH, SLAB), :], gsem)

        def scatter_row(o_ref, r, d):
            dst = pl.multiple_of(lax.shift_right_logical(d, 1) * O2_PITCH + (d & (TOP_K - 1)) * SLAB, SUB)
            return pltpu.make_async_copy(o_ref.at[pl.ds(r * PITCH, SLAB), :], o_hbm.at[pl.ds(dst, SLAB), :], ssem)

        def wait_rows(sem):
            pltpu.make_async_copy(h_hbm.at[pl.ds(0, RT * SLAB), :], x0.at[pl.ds(0, RT * SLAB), :], sem).wait()

        wait_gather = lambda: wait_rows(gsem)
        wait_scatter = lambda: wait_rows(ssem)

        tok_of = lambda s: s & ((1 << TOK_BITS) - 1)
        dst_of = lambda s: lax.shift_right_logical(s, TOK_BITS)

        @pl.when(j == 0)
        def _():
            o1[...] = jnp.zeros_like(o1)

            def one(r, _):
                gather_row(x0, r, tok_of(slot_ref[r])).start()
                return 0
            lax.fori_loop(0, RT, one, 0)

        @pl.when(active & ((j == 0) | (tile_ref[j] != tile_ref[jnp.maximum(j - 1, 0)])))
        def _():
            w1b[...] = w1_ref[...].astype(bf16)
            w3b[...] = w3_ref[...].astype(bf16)
            w2b[...] = w2_ref[...].astype(bf16)

        @pl.when(active & (j >= 1))
        def _():
            wait_scatter()

        def tile(x_cur, x_nxt, o_cur, o_prv):
            wait_gather()
            xb = jnp.concatenate([x_cur[pl.ds(s, RT, stride=PITCH), :] for s in range(SLAB)], axis=1).astype(bf16)
            nxt = jnp.minimum(j + 1, n_used - 1) * RT
            prv = jnp.maximum(j - 1, 0) * RT
            parts = []
            for c in range(n_hc):
                h1 = _dot(xb, w1b[:, c * HC:(c + 1) * HC])
                h3 = _dot(xb, w3b[:, c * HC:(c + 1) * HC])
                parts.append(((h1 * jax.nn.sigmoid(h1)) * h3).astype(bf16))
                for r in range(c * g_per, (c + 1) * g_per):
                    gather_row(x_nxt, r, tok_of(slot_ref[nxt + r])).start(priority=r % 2)
            hm = jnp.concatenate(parts, axis=1)
            for c in range(n_oc):
                res = _dot(hm, w2b[:, c * OC:(c + 1) * OC])
                for t in range(OC // LANE):
                    o_cur[pl.ds(c * (OC // LANE) + t, RT, stride=PITCH), :] = res[:, t * LANE:(t + 1) * LANE]
                for r in range(c * s_per, (c + 1) * s_per):
                    d = jnp.where(j == 0, prime_id + r, dst_of(slot_ref[prv + r]))
                    scatter_row(o_prv, r, d).start(priority=r % 2)

        def last(o_cur):
            wait_scatter()

            def one(r, _):
                scatter_row(o_cur, r, dst_of(slot_ref[j * RT + r])).start()
                return 0
            lax.fori_loop(0, RT, one, 0)
            wait_scatter()
            wait_gather()

        for par, (xc, xn, oc, op) in enumerate(((x0, x1, o0, o1), (x1, x0, o1, o0))):
            pl.when(active & (j % 2 == par))(lambda xc=xc, xn=xn, oc=oc, op=op: tile(xc, xn, oc, op))
        for par, oc in enumerate((o0, o1)):
            pl.when((j == n_used - 1) & (j % 2 == par))(lambda oc=oc: last(oc))

    grid_spec = pltpu.PrefetchScalarGridSpec(
        num_scalar_prefetch=3,
        grid=(nt,),
        in_specs=[pl.BlockSpec(memory_space=pl.ANY),
                  pl.BlockSpec((None, None, D, D_EXPERT), lambda j, te, *_: (l, te[j], 0, 0)),
                  pl.BlockSpec((None, None, D, D_EXPERT), lambda j, te, *_: (l, te[j], 0, 0)),
                  pl.BlockSpec((None, None, D_EXPERT, D), lambda j, te, *_: (l, te[j], 0, 0))],
        out_specs=pl.BlockSpec(memory_space=pl.ANY),
        scratch_shapes=[pltpu.VMEM((RT * PITCH, LANE), f32), pltpu.VMEM((RT * PITCH, LANE), f32),
                        pltpu.VMEM((RT * PITCH, LANE), f32), pltpu.VMEM((RT * PITCH, LANE), f32),
                        pltpu.VMEM((D, D_EXPERT), bf16), pltpu.VMEM((D, D_EXPERT), bf16),
                        pltpu.VMEM((D_EXPERT, D), bf16),
                        pltpu.SemaphoreType.DMA(()), pltpu.SemaphoreType.DMA(())],
    )
    return pl.pallas_call(
        body,
        grid_spec=grid_spec,
        out_shape=jax.ShapeDtypeStruct((out_rows * O2_PITCH, LANE), f32),
        compiler_params=_cparams(("arbitrary",)),
    )(tile_e, n_used, packed, h2, w1, w3, w2)


def _moe_combine(x, o2, wgt, mod, norm_final, row0):
    T = x.shape[0]
    tm = min(mod.tm, 512)
    b0 = row0 // tm

    def body(x_ref, o_ref, p_ref, m_ref, *rest):
        p = p_ref[...]
        row = lambda k: jnp.concatenate(
            [o_ref[pl.ds(k * SLAB + s, tm, stride=O2_PITCH), :] for s in range(SLAB)], axis=1)
        y = p[:, 0:1] * row(0)
        for k in range(1, TOP_K):
            y = y + p[:, k:k + 1] * row(k)
        xo = x_ref[...] + m_ref[...] * y
        if norm_final is None:
            rest[0][...] = xo
        else:
            rest[1][...] = _rms(xo, rest[0][...])

    ins = [x, o2, wgt, mod.arr]
    specs = [pl.BlockSpec((tm, D), lambda m: (m, 0)),
             pl.BlockSpec((tm * O2_PITCH, LANE), lambda m: (b0 + m, 0)),
             pl.BlockSpec((tm, ROUTE_W), lambda m: (b0 + m, 0)), mod.row_spec(5, tm)]
    if norm_final is not None:
        ins.append(norm_final.reshape(1, D))
        specs.append(pl.BlockSpec((1, D), lambda m: (0, 0)))
    return pl.pallas_call(
        body,
        grid=(T // tm,),
        in_specs=specs,
        out_specs=pl.BlockSpec((tm, D), lambda m: (m, 0)),
        out_shape=jax.ShapeDtypeStruct((T, D), f32),
        compiler_params=_cparams(("parallel",)),
    )(*ins)


def _moe(groups, p, l, big, norm_final):
    pad = ROUTE_W - N_GROUPS - N_EXPERTS
    w_r = jnp.pad(jnp.concatenate([p['moe_w_group'], p['moe_w_router']], axis=1), ((0, 0), (0, pad)))
    b_r = jnp.pad(jnp.concatenate([p['moe_b_group'], p['moe_b_router']]), (0, pad)).reshape(1, ROUTE_W)
    t_all = sum(g.T for g in groups)
    bufs, row0 = None, 0
    for g in groups:
        g.row0 = row0
        bufs = _router(g.x, p['norm_ffn'], g.mod, w_r, b_r, row0, t_all, bufs)
        row0 += g.T
    h2, eid, wgt = bufs
    packed, tile_e, n_used = _moe_plan(eid, t_all)
    o2 = _experts(h2, packed, tile_e, n_used, l, big['moe_w1'], big['moe_w3'], big['moe_w2'])
    for g in groups:
        g.x = _moe_combine(g.x, o2, wgt, g.mod, norm_final, g.row0)


class _Group:
    def __init__(self, x, mods, states, pos0):
        self.B, self.L, _ = x.shape
        self.T = self.B * self.L
        self.tm = _row_tile(self.T)
        self.x = x.reshape(self.T, D)
        self.mods, self.states, self.pos0 = mods, states, pos0
        self.outs = ([], [], [], [], [])
        self.ret_all = self.rw_all = None

    def mix(self, l, p, rp, sp, layers, big, w_in_t):
        B, L = self.B, self.L
        self.mod = mod = _Mod(self.mods[l], L, self.tm)
        proj = _in_proj(self.x, layers['norm_mix'], l, mod, w_in_t)
        if self.states is None:
            y_ret, s_ret = _retention_seq(proj, B, L)
            y_rw, s_rw, shift = _rwkv_seq(proj, rp, B, L)
            y_s5, s_re, s_im = _s5_seq(proj, sp, B, L)
        else:
            st_ret, st_rw, st_shift, st_re, st_im = self.states
            y_ret, self.ret_all = _retention_step(proj, st_ret, l, self.ret_all, self.pos0)
            y_rw, self.rw_all = _rwkv_step(proj, st_shift[l], jnp.transpose(st_rw, (0, 2, 3, 4, 1)), l,
                                           self.rw_all, rp)
            s_ret = s_rw = None
            shift = proj[:, OFF_RW:OFF_RW + RWKV_PROJ]
            y_s5, s_re, s_im = _s5_step(proj, st_re[l], st_im[l], sp)
        z = _glu_proj(y_s5, l, big['s5_w_glu'])
        merged = _merge_proj(y_ret, y_rw, z, l, big['ret_w_o'], big['rwkv_w_o'], big['s5_w_o'], proj)
        self.x = _out_proj(merged, l, big['w_out'], self.x, mod)
        for lst, val in zip(self.outs, (s_ret, s_rw, shift, s_re, s_im)):
            lst.append(val)

    def results(self):
        stacked = [jnp.stack(o) if o[0] is not None else None for o in self.outs]
        if self.states is not None:
            stacked[0], stacked[1] = self.ret_all, jnp.transpose(self.rw_all, (0, 4, 1, 2, 3))
        return self.x.reshape(self.B, self.L, D), stacked


def _trunk(groups, layers, norm_final):
    big = {k: layers[k] for k in _BIG}
    w_in_t = jnp.swapaxes(layers['w_in'], 1, 2)
    for l in range(DEPTH):
        p = {name: arr[l] for name, arr in layers.items() if name not in _BIG + ('w_in',)}
        rp, sp = _rwkv_params(p), _s5_params(p)
        for g in groups:
            g.mix(l, p, rp, sp, layers, big, w_in_t)
        _moe(groups, p, l, big, norm_final if l == DEPTH - 1 else None)
    return [g.results() for g in groups]


_BIG = ('ret_w_o', 'rwkv_w_o', 's5_w_glu', 's5_w_o', 'w_out', 'moe_w1', 'moe_w3', 'moe_w2')


def kernel(x_prompt, x_sample, state_ret, state_rwkv, state_shift, state_s5_re, state_s5_im,
           c_prompt, c_sample, norm_mix, norm_ffn, w_ada, b_ada, w_in, ret_w_o, rwkv_mu, rwkv_w0,
           rwkv_w2, rwkv_a0, rwkv_a2, rwkv_g2, rwkv_k_k, rwkv_k_a, rwkv_r_k, rwkv_ln_w, rwkv_ln_b,
           rwkv_w_o, s5_a_re, s5_a_im, s5_b_re, s5_b_im, s5_c_re, s5_c_im, s5_d, s5_log_dt, s5_w_glu,
           s5_w_o, w_out, moe_w_group, moe_b_group, moe_w_router, moe_b_router, moe_w1, moe_w3, moe_w2,
           norm_final):
    layers = {
        'norm_mix': norm_mix, 'norm_ffn': norm_ffn, 'w_in': w_in,
        'ret_w_o': ret_w_o, 'rwkv_mu': rwkv_mu, 'rwkv_w0': rwkv_w0, 'rwkv_w2': rwkv_w2,
        'rwkv_a0': rwkv_a0, 'rwkv_a2': rwkv_a2, 'rwkv_g2': rwkv_g2, 'rwkv_k_k': rwkv_k_k,
        'rwkv_k_a': rwkv_k_a, 'rwkv_r_k': rwkv_r_k, 'rwkv_ln_w': rwkv_ln_w, 'rwkv_ln_b': rwkv_ln_b,
        'rwkv_w_o': rwkv_w_o, 's5_a_re': s5_a_re, 's5_a_im': s5_a_im, 's5_b_re': s5_b_re,
        's5_b_im': s5_b_im, 's5_c_re': s5_c_re, 's5_c_im': s5_c_im, 's5_d': s5_d,
        's5_log_dt': s5_log_dt, 's5_w_glu': s5_w_glu, 's5_w_o': s5_w_o, 'w_out': w_out,
        'moe_w_group': moe_w_group, 'moe_b_group': moe_b_group, 'moe_w_router': moe_w_router,
        'moe_b_router': moe_b_router, 'moe_w1': moe_w1, 'moe_w3': moe_w3, 'moe_w2': moe_w2,
    }
    Bp, Bs = x_prompt.shape[0], x_sample.shape[0]
    s_off = -(-Bp // 8) * 8
    c_all = jnp.concatenate([c_prompt, jnp.zeros((s_off - Bp, D), f32), c_sample], axis=0)
    mod_all = _adaln(c_all, w_ada, b_ada)
    prompt = _Group(x_prompt, mod_all[:, :Bp], None, 0.0)
    sample = _Group(x_sample, mod_all[:, s_off:s_off + Bs],
                    (state_ret, state_rwkv, state_shift, state_s5_re, state_s5_im), float(PAST_LEN))
    ((y_prompt, (ret_p, rwkv_p, shift_p, s5re_p, s5im_p)),
     (y_sample, (ret_s, rwkv_s, shift_s, s5re_s, s5im_s))) = _trunk([prompt, sample], layers, norm_final)
    return (y_prompt, y_sample, ret_p, ret_s, rwkv_p, rwkv_s, shift_p, shift_s, s5re_p, s5re_s, s5im_p, s5im_s)
```

```python
import jax
import jax.numpy as jnp
from jax import lax
from jax.experimental import pallas as pl
from jax.experimental.pallas import tpu as pltpu

f32 = jnp.float32
bf16 = jnp.bfloat16
i32 = jnp.int32

D = 2048
DEPTH = 2
PAST_LEN = 16384
RET_W, RET_H, RET_DK, RET_DV, RET_CHUNK = 1024, 4, 256, 256, 128
RET_GN_EPS = 1e-6
ROPE_BASE = 10000.0
RWKV_W, RWKV_N, RWKV_H = 1024, 64, 16
DECAY_LORA, AAA_LORA, GATE_LORA = 64, 64, 160
RWKV_PROJ = 3 * RWKV_W + DECAY_LORA + AAA_LORA + GATE_LORA
RWKV_LN_EPS = 64e-5
S5_W, S5_GC, S5_G, S5_P = 1024, 16, 64, 64
N_MOD = 6
RMS_EPS = 1e-6
N_GROUPS, EPG, N_EXPERTS, TOP_K, D_EXPERT = 4, 8, 32, 2, 512
IN_W = 4 * RET_W + RWKV_PROJ + S5_W + 3 * D
OFF_Q, OFF_K, OFF_V, OFF_G, OFF_RW = 0, 1024, 2048, 3072, 4096
OFF_U = OFF_RW + RWKV_PROJ
OFF_GATE = OFF_U + S5_W
LANE = 128
PROJ_W = ((IN_W + LANE - 1) // LANE) * LANE
VMEM_LIMIT = 56 * 1024 * 1024


def _cparams(sem):
    return pltpu.CompilerParams(dimension_semantics=sem, vmem_limit_bytes=VMEM_LIMIT)


def _dot(a, b):
    return jnp.dot(a, b, preferred_element_type=f32)


def _dot_nt(a, b):
    return lax.dot_general(a, b, (((1,), (1,)), ((), ())), preferred_element_type=f32)


def _dot_tn(a, b):
    return lax.dot_general(a, b, (((0,), (0,)), ((), ())), preferred_element_type=f32)


def _rms(x, g):
    return x * lax.rsqrt(jnp.mean(x * x, axis=-1, keepdims=True) + RMS_EPS) * g


def _head_norm(y, eps):
    mu = jnp.mean(y, axis=-1, keepdims=True)
    yc = y - mu
    return yc * lax.rsqrt(jnp.mean(yc * yc, axis=-1, keepdims=True) + eps)


def _row_tile(T):
    return 1024 if T >= 1024 else T


class _Mod:
    def __init__(self, mod, L, tm):
        self.L, self.tm = L, tm
        self.per_token = L == 1
        self.arr = mod if self.per_token else mod.reshape(mod.shape[0], 1, N_MOD * D)

    def spec(self, j, tn, col_of):
        nb = D // tn
        if self.per_token:
            return pl.BlockSpec((self.tm, tn), lambda m, n: (m, j * nb + col_of(n)))
        L, tm = self.L, self.tm
        return pl.BlockSpec((None, 1, tn), lambda m, n: ((m * tm) // L, 0, j * nb + col_of(n)))

    def row_spec(self, j, tm, remap=lambda m: m):
        if self.per_token:
            return pl.BlockSpec((tm, D), lambda m: (remap(m), j))
        L = self.L
        return pl.BlockSpec((None, 1, D), lambda m: ((remap(m) * tm) // L, 0, j))


def _fused_mm(x_ops, w_ops, e_ops, pre, post, *, grid, out_specs, out_shape, cache_shapes):
    nx, nw, ne = len(x_ops), len(w_ops), len(e_ops)
    n_out = len(out_shape)

    def body(*refs):
        x_refs = refs[:nx]
        w_refs = refs[nx:nx + nw]
        e_refs = refs[nx + nw:nx + nw + ne]
        o_refs = refs[nx + nw + ne:nx + nw + ne + n_out]
        caches = refs[nx + nw + ne + n_out:]
        if cache_shapes:
            @pl.when(pl.program_id(1) == 0)
            def _():
                for i in range(nx):
                    caches[i][...] = pre(i, x_refs[i], e_refs).astype(bf16)
            lhs = [c[...] for c in caches]
        else:
            lhs = [x[...] for x in x_refs]
        prods = [(_dot_nt if len(w_ops[j]) > 3 and w_ops[j][3] else _dot)(
            lhs[w_ops[j][2]], w_refs[j][...].astype(bf16)) for j in range(nw)]
        for o_ref, o in zip(o_refs, post(prods, e_refs)):
            o_ref[...] = o.astype(o_ref.dtype)

    return pl.pallas_call(
        body,
        grid=grid,
        in_specs=[s for _, s in x_ops] + [w[1] for w in w_ops] + [s for _, s in e_ops],
        out_specs=out_specs,
        out_shape=out_shape,
        scratch_shapes=[pltpu.VMEM(s, bf16) for s in cache_shapes],
        compiler_params=_cparams(("parallel", "arbitrary")),
    )(*[a for a, _ in x_ops], *[w[0] for w in w_ops], *[a for a, _ in e_ops])


def _adaln(c_all, w_ada, b_ada):
    R = c_all.shape[0]
    tn = 1024

    def pre(i, x_ref, e_refs):
        c = x_ref[...]
        return c * jax.nn.sigmoid(c)

    def post(prods, e_refs):
        return (prods[0] + e_refs[0][...],)

    (out,) = _fused_mm(
        [(c_all, pl.BlockSpec((R, D), lambda l, n: (0, 0)))],
        [(w_ada, pl.BlockSpec((None, D, tn), lambda l, n: (l, 0, n)), 0)],
        [(b_ada.reshape(DEPTH, 1, N_MOD * D), pl.BlockSpec((None, 1, tn), lambda l, n: (l, 0, n)))],
        pre, post,
        grid=(DEPTH, N_MOD * D // tn),
        out_specs=[pl.BlockSpec((None, R, tn), lambda l, n: (l, 0, n))],
        out_shape=[jax.ShapeDtypeStruct((DEPTH, R, N_MOD * D), f32)],
        cache_shapes=[(R, D)],
    )
    return out


def _in_proj(x, g, l, mod, w_in_t):
    T = x.shape[0]
    tm, tn = mod.tm, 1024
    tr = min(tm, 512)

    def norm_body(x_ref, g_ref, shift_ref, scale_ref, h_ref):
        h_ref[...] = (_rms(x_ref[...], g_ref[...]) * (1.0 + scale_ref[...]) + shift_ref[...]).astype(bf16)

    h = pl.pallas_call(
        norm_body,
        grid=(T // tr,),
        in_specs=[pl.BlockSpec((tr, D), lambda m: (m, 0)),
                  pl.BlockSpec((None, 1, D), lambda m: (l, 0, 0)),
                  mod.row_spec(0, tr), mod.row_spec(1, tr)],
        out_specs=pl.BlockSpec((tr, D), lambda m: (m, 0)),
        out_shape=jax.ShapeDtypeStruct((T, D), bf16),
        compiler_params=_cparams(("parallel",)),
    )(x, g.reshape(DEPTH, 1, D), mod.arr, mod.arr)

    def mm_body(h_ref, w_ref, o_ref, wb):
        @pl.when(pl.program_id(1) == 0)
        def _():
            wb[...] = w_ref[...].astype(bf16)

        col = pl.program_id(0) * tn + lax.broadcasted_iota(i32, (tm, tn), 1)
        o_ref[...] = jnp.where(col < IN_W, _dot_nt(h_ref[...], wb[...]), 0.0)

    return pl.pallas_call(
        mm_body,
        grid=(pl.cdiv(PROJ_W, tn), T // tm),
        in_specs=[pl.BlockSpec((tm, D), lambda n, m: (m, 0)),
                  pl.BlockSpec((None, tn, D), lambda n, m: (l, n, 0))],
        out_specs=pl.BlockSpec((tm, tn), lambda n, m: (m, n)),
        out_shape=jax.ShapeDtypeStruct((T, PROJ_W), f32),
        scratch_shapes=[pltpu.VMEM((tn, D), bf16)],
        compiler_params=_cparams(("parallel", "arbitrary")),
    )(h, w_in_t)


def _glu_proj(yg, l, w_glu):
    T = yg.shape[0]
    tm, tn = _row_tile(T), 512
    nb = S5_W // tn

    def post(prods, e_refs):
        return (prods[0] * jax.nn.sigmoid(prods[1]),)

    (out,) = _fused_mm(
        [(yg, pl.BlockSpec((tm, S5_W), lambda m, n: (m, 0)))],
        [(w_glu, pl.BlockSpec((None, S5_W, tn), lambda m, n: (l, 0, n)), 0),
         (w_glu, pl.BlockSpec((None, S5_W, tn), lambda m, n: (l, 0, nb + n)), 0)],
        [], None, post,
        grid=(T // tm, nb),
        out_specs=[pl.BlockSpec((tm, tn), lambda m, n: (m, n))],
        out_shape=[jax.ShapeDtypeStruct((T, S5_W), bf16)],
        cache_shapes=[],
    )
    return out


def _merge_proj(y_ret, y_rw, y_s5, l, w_ret, w_rw, w_s5, proj):
    T = y_ret.shape[0]
    tm, tn = _row_tile(T), 512
    lead = OFF_GATE % LANE
    base = OFF_GATE - lead

    def gate_spec(i):
        return pl.BlockSpec((pl.Element(tm), pl.Element(tn + LANE)),
                            lambda m, n: (pl.multiple_of(m * tm, tm), pl.multiple_of(base + i * D + n * tn, LANE)))

    def post(prods, e_refs):
        acc = None
        for p, e in zip(prods, e_refs):
            t = jax.nn.sigmoid(e[:, lead:lead + tn]) * p
            acc = t if acc is None else acc + t
        return (acc,)

    xspec = pl.BlockSpec((tm, RET_W), lambda m, n: (m, 0))
    wspec = pl.BlockSpec((None, RET_W, tn), lambda m, n: (l, 0, n))
    (out,) = _fused_mm(
        [(y_ret, xspec), (y_rw, xspec), (y_s5, xspec)],
        [(w_ret, wspec, 0), (w_rw, wspec, 1), (w_s5, wspec, 2)],
        [(proj, gate_spec(0)), (proj, gate_spec(1)), (proj, gate_spec(2))],
        None, post,
        grid=(T // tm, D // tn),
        out_specs=[pl.BlockSpec((tm, tn), lambda m, n: (m, n))],
        out_shape=[jax.ShapeDtypeStruct((T, D), bf16)],
        cache_shapes=[],
    )
    return out


def _out_proj(merged, l, w_out, x, mod):
    T = x.shape[0]
    tm, tn = mod.tm, 512

    def post(prods, e_refs):
        x_ref, m_ref = e_refs
        return (x_ref[...] + m_ref[...] * prods[0],)

    (out,) = _fused_mm(
        [(merged, pl.BlockSpec((tm, D), lambda m, n: (m, 0)))],
        [(w_out, pl.BlockSpec((None, D, tn), lambda m, n: (l, 0, n)), 0)],
        [(x, pl.BlockSpec((tm, tn), lambda m, n: (m, n))), (mod.arr, mod.spec(2, tn, lambda n: n))],
        None, post,
        grid=(T // tm, D // tn),
        out_specs=[pl.BlockSpec((tm, tn), lambda m, n: (m, n))],
        out_shape=[jax.ShapeDtypeStruct((T, D), f32)],
        cache_shapes=[],
    )
    return out


def _ret_consts(L, pos0):
    C = RET_CHUNK if L % RET_CHUNK == 0 else L
    H = RET_H
    log_g = jnp.log1p(-jnp.exp2(-5.0 - jnp.arange(H, dtype=f32)))
    i = jnp.arange(C, dtype=f32)
    diff = i[:, None] - i[None, :]
    causal = diff >= 0
    dmask = jnp.where(causal, jnp.exp(jnp.where(causal, diff, 0.0)[None] * log_g[:, None, None]), 0.0)
    kdec = jnp.exp((C - 1.0 - i)[:, None] * log_g[None, :])
    qdec = jnp.exp((i + 1.0)[:, None] * log_g[None, :])
    g_chunk = jnp.exp(C * log_g)
    half = RET_DK // 2
    inv = ROPE_BASE ** (-jnp.arange(half, dtype=f32) / half)
    pos = pos0 + jnp.arange(L, dtype=f32)
    ang = pos[:, None] * inv[None, :]
    return C, dmask, kdec, qdec, g_chunk, jnp.cos(ang), jnp.sin(ang)


def _rotary(x, cos, sin):
    half = RET_DK // 2
    x1, x2 = x[..., :half], x[..., half:]
    return jnp.concatenate([x1 * cos - x2 * sin, x1 * sin + x2 * cos], axis=-1)


def _retention_seq(proj, B, L):
    C, dmask, kdec, qdec, g_chunk, cos, sin = _ret_consts(L, 0.0)
    H, dk = RET_H, RET_DK
    n = L // C
    kdec_f = jnp.broadcast_to(kdec.T[:, :, None], (H, C, dk))
    qdec_f = jnp.broadcast_to(qdec.T[:, :, None], (H, C, dk))
    gch_f = jnp.broadcast_to(g_chunk[:, None, None], (H, 8, dk))
    proj3 = proj.reshape(B, L, PROJ_W)

    def body(q_ref, k_ref, v_ref, g_ref, cos_ref, sin_ref, dm_ref, kd_ref, qd_ref, gc_ref, y_ref, s_ref, st):
        c = pl.program_id(1)

        @pl.when(c == 0)
        def _():
            st[...] = jnp.zeros_like(st)

        cs, sn = cos_ref[...], sin_ref[...]
        hs = range(H)
        sl = [slice(h * dk, (h + 1) * dk) for h in hs]
        q = [_rotary(q_ref[:, sl[h]], cs, sn) for h in hs]
        k = [_rotary(k_ref[:, sl[h]], cs, sn) * (dk ** -0.5) for h in hs]
        vb = [v_ref[:, sl[h]].astype(bf16) for h in hs]
        s0 = [st[h] for h in hs]
        scores = [_dot_nt(q[h].astype(bf16), k[h].astype(bf16)) * dm_ref[h] for h in hs]
        cross = [_dot((q[h] * qd_ref[h]).astype(bf16), s0[h].astype(bf16)) for h in hs]
        kv = [_dot_tn((k[h] * kd_ref[h]).astype(bf16), vb[h]) for h in hs]
        o = [_dot(scores[h].astype(bf16), vb[h]) + cross[h] for h in hs]
        for h in hs:
            st[h] = s0[h] * gc_ref[h, 0:1, :] + kv[h]
            g = g_ref[:, sl[h]]
            y_ref[:, sl[h]] = (g * jax.nn.sigmoid(g) * _head_norm(o[h], RET_GN_EPS)).astype(bf16)

        @pl.when(c == n - 1)
        def _():
            s_ref[...] = st[...]

    def seg(off):
        return pl.BlockSpec((None, C, RET_W), lambda b, c: (b, c, off // RET_W))

    const3 = lambda shp: pl.BlockSpec(shp, lambda b, c: (0, 0, 0))
    y, s = pl.pallas_call(
        body,
        grid=(B, n),
        in_specs=[seg(OFF_Q), seg(OFF_K), seg(OFF_V), seg(OFF_G),
                  pl.BlockSpec((C, dk // 2), lambda b, c: (c, 0)),
                  pl.BlockSpec((C, dk // 2), lambda b, c: (c, 0)),
                  const3((H, C, C)), const3((H, C, dk)), const3((H, C, dk)), const3((H, 8, dk))],
        out_specs=[pl.BlockSpec((None, C, RET_W), lambda b, c: (b, c, 0)),
                   pl.BlockSpec((None, H, dk, RET_DV), lambda b, c: (b, 0, 0, 0))],
        out_shape=[jax.ShapeDtypeStruct((B, L, RET_W), bf16),
                   jax.ShapeDtypeStruct((B, H, dk, RET_DV), f32)],
        scratch_shapes=[pltpu.VMEM((H, dk, RET_DV), f32)],
        compiler_params=_cparams(("parallel", "arbitrary")),
    )(proj3, proj3, proj3, proj3, cos, sin, dmask, kdec_f, qdec_f, gch_f)
    return y.reshape(B * L, RET_W), s


STEP_TB = 16


def _layer_grid(l, buf, inner):
    if buf is not None:
        return inner, (lambda fn: (lambda *ix: fn(l, ix, ix)))
    assert l == 0
    last = tuple(n - 1 for n in inner)

    def wrap(fn):
        def index_map(d, *ix):
            parked = tuple(jnp.where(d == l, i, z) for i, z in zip(ix, last))
            return fn(d, ix, parked)
        return index_map
    return (DEPTH,) + inner, wrap


def _retention_step(proj, s_all, l, buf, pos0):
    B = proj.shape[0]
    _, _, _, _, g_chunk, cos, sin = _ret_consts(1, pos0)
    H, dk = RET_H, RET_DK
    gch = jnp.broadcast_to(g_chunk[:, None, None], (H, 8, dk))
    tb = STEP_TB
    grid, wrap = _layer_grid(l, buf, (B // tb, H))

    def body(q_ref, k_ref, v_ref, g_ref, cos_ref, sin_ref, gc_ref, s_ref, *rest):
        y_ref, so_ref = rest[-2:]

        def update():
            cs, sn = cos_ref[...], sin_ref[...]
            q = _rotary(q_ref[...], cs, sn)
            k = _rotary(k_ref[...], cs, sn) * (dk ** -0.5)
            v = v_ref[...]
            s1 = s_ref[...] * gc_ref[0:1, :][None] + k[:, :, None] * v[:, None, :]
            so_ref[...] = s1
            o = jnp.sum(q[:, :, None] * s1, axis=1)
            g = g_ref[...]
            y_ref[...] = (g * jax.nn.sigmoid(g) * _head_norm(o, RET_GN_EPS)).astype(bf16)

        if buf is not None:
            update()
        else:
            pl.when(pl.program_id(0) == l)(update)

            @pl.when(pl.program_id(0) != l)
            def _():
                so_ref[...] = jnp.zeros_like(so_ref)

    def seg(off):
        return pl.BlockSpec((tb, dk), wrap(lambda d, ix, pk: (pk[0], off // dk + pk[1])))

    const = lambda d, ix, pk: (0, 0)
    in_specs = [seg(OFF_Q), seg(OFF_K), seg(OFF_V), seg(OFF_G),
                pl.BlockSpec((1, dk // 2), wrap(const)), pl.BlockSpec((1, dk // 2), wrap(const)),
                pl.BlockSpec((None, 8, dk), wrap(lambda d, ix, pk: (pk[1], 0, 0))),
                pl.BlockSpec((None, tb, None, dk, RET_DV), wrap(lambda d, ix, pk: (l, pk[0], pk[1], 0, 0)))]
    args = [proj, proj, proj, proj, cos, sin, gch, s_all]
    aliases = {}
    if buf is not None:
        in_specs.append(pl.BlockSpec(memory_space=pl.ANY))
        args.append(buf)
        aliases = {len(args) - 1: 1}
    return pl.pallas_call(
        body,
        grid=grid,
        in_specs=in_specs,
        out_specs=[pl.BlockSpec((tb, dk), wrap(lambda d, ix, pk: pk)),
                   pl.BlockSpec((None, tb, None, dk, RET_DV), wrap(lambda d, ix, pk: (d, ix[0], ix[1], 0, 0)))],
        out_shape=[jax.ShapeDtypeStruct((B, RET_W), bf16),
                   jax.ShapeDtypeStruct((DEPTH, B, H, dk, RET_DV), f32)],
        input_output_aliases=aliases,
        compiler_params=_cparams(("arbitrary",) * len(grid)),
    )(*args)


RW_C = 64
RW_Q = 4
RW_NQ = RWKV_H // RW_Q
RW_GROUP = 8
RW_BLK =((RWKV_PROJ + LANE - 1) // LANE) * LANE
_RW_PKEYS = ('mu', 'w0', 'w2', 'a0', 'a2', 'g2', 'k_k', 'k_a', 'r_k', 'ln_w', 'ln_b')


def _rwkv_params(p):
    return dict(
        mu=jnp.pad(p['rwkv_mu'], (0, RW_BLK - RWKV_PROJ)).reshape(1, RW_BLK),
        w0=p['rwkv_w0'].reshape(1, RWKV_W), w2=p['rwkv_w2'],
        a0=p['rwkv_a0'].reshape(1, RWKV_W), a2=p['rwkv_a2'], g2=p['rwkv_g2'],
        k_k=p['rwkv_k_k'].reshape(1, RWKV_W), k_a=p['rwkv_k_a'].reshape(1, RWKV_W),
        r_k=p['rwkv_r_k'].reshape(1, RWKV_W),
        ln_w=p['rwkv_ln_w'].reshape(1, RWKV_W), ln_b=p['rwkv_ln_b'].reshape(1, RWKV_W))


def _split_bf16(x, terms):
    out = []
    for _ in range(terms - 1):
        hi = x.astype(bf16)
        out.append(hi)
        x = x - hi.astype(f32)
    out.append(x.astype(bf16))
    return out


def _head_sum(x):
    QW = RW_Q * RWKV_N
    r = lax.broadcasted_iota(i32, (QW, QW), 0) // RWKV_N
    c = lax.broadcasted_iota(i32, (QW, QW), 1) // RWKV_N
    ones = (r == c).astype(bf16)
    parts = _split_bf16(x, 2)
    outs = []
    for q in range(RW_NQ):
        sl = slice(q * QW, (q + 1) * QW)
        outs.append(_dot(parts[0][:, sl], ones) + _dot(parts[1][:, sl], ones))
    return jnp.concatenate(outs, axis=1)


def _rwkv_mix(rw, prev, pr):
    m = rw + (prev - rw) * pr['mu'][...]
    r = m[:, 0:RWKV_W]
    k = m[:, RWKV_W:2 * RWKV_W]
    v = m[:, 2 * RWKV_W:3 * RWKV_W]
    o = 3 * RWKV_W
    xw = m[:, o:o + DECAY_LORA]
    xa = m[:, o + DECAY_LORA:o + DECAY_LORA + AAA_LORA]
    xg = m[:, o + DECAY_LORA + AAA_LORA:o + DECAY_LORA + AAA_LORA + GATE_LORA]
    w_log = -jax.nn.softplus(-(pr['w0'][...] + _dot(jnp.tanh(xw).astype(bf16), pr['w2'][...].astype(bf16)))) - 0.5
    lw = -jnp.exp(w_log)
    a = jax.nn.sigmoid(pr['a0'][...] + _dot(xa.astype(bf16), pr['a2'][...].astype(bf16)))
    g = _dot(jax.nn.sigmoid(xg).astype(bf16), pr['g2'][...].astype(bf16))
    kk = k * pr['k_k'][...]
    kk = kk / jnp.maximum(jnp.sqrt(_head_sum(kk * kk)), 1e-12)
    kf = k * (1.0 + (a - 1.0) * pr['k_a'][...])
    return r, lw, kf, v, kk, a, g


def _rwkv_out(y, r, kf, v, g, pr):
    yc = y - _head_sum(y) * (1.0 / RWKV_N)
    yn = yc * lax.rsqrt(_head_sum(yc * yc) * (1.0 / RWKV_N) + RWKV_LN_EPS)
    yn = yn * pr['ln_w'][...] + pr['ln_b'][...]
    yn = yn + _head_sum(r * kf * pr['r_k'][...]) * v
    return yn * g


def _rwkv_seq(proj, rp, B, L):
    C = RW_C
    assert L % C == 0 and C == RWKV_N
    n = L // C
    QW = RW_Q * RWKV_N
    BC = B * C
    proj3 = proj.reshape(B, L, PROJ_W)

    def body(rw_ref, *refs):
        pr = dict(zip(_RW_PKEYS, refs[:len(_RW_PKEYS)]))
        y_ref, s_ref, sh_ref, st, carry = refs[len(_RW_PKEYS):]
        c = pl.program_id(0)

        @pl.when(c == 0)
        def _():
            st[...] = jnp.zeros_like(st)
            carry[...] = jnp.zeros_like(carry)

        rw = rw_ref[...].reshape(BC, RW_BLK)
        rolled = pltpu.roll(rw, 1, 0)
        row = lax.broadcasted_iota(i32, (C, RW_BLK), 0)
        prev = jnp.concatenate(
            [jnp.where(row == 0, carry[b, 0:1, :], rolled[b * C:(b + 1) * C]) for b in range(B)], axis=0)
        for b in range(B):
            carry[b, 0:1, :] = rw[(b + 1) * C - 1:(b + 1) * C, :]
        r, lw, kf, v, kk, a, g = _rwkv_mix(rw, prev, pr)

        ti = lax.broadcasted_iota(i32, (BC, BC), 0)
        si = lax.broadcasted_iota(i32, (BC, BC), 1)
        tril = ((si <= ti) & ((si // C) == (ti // C))).astype(bf16)
        lg = sum(_dot(tril, part) for part in _split_bf16(lw, 3))
        lgc = jnp.concatenate(
            [jnp.broadcast_to(lg[(b + 1) * C - 1:(b + 1) * C, :], (C, RWKV_W)) for b in range(B)], axis=0)
        e_neg = jnp.exp(-lg)
        e_rem = jnp.exp(lgc - lg)
        at = kk * jnp.exp(lg - lw)
        ka = kk * a
        bt = ka * e_neg
        kt = kf * e_neg
        rt = r * jnp.exp(lg)
        bh = ka * e_rem
        kh = kf * e_rem
        gcr = jnp.exp(lgc)

        rr = lax.broadcasted_iota(i32, (RW_Q * C, QW), 0)
        ll = lax.broadcasted_iota(i32, (RW_Q * C, QW), 1)
        blockmask = (rr // C) == (ll // RWKV_N)
        tt = lax.broadcasted_iota(i32, (C, QW), 0)
        ss = lax.broadcasted_iota(i32, (C, QW), 1) % C
        strict = ss < tt
        incl = ss <= tt
        eye = (ss == tt).astype(f32)

        def bd(x):
            return jnp.where(blockmask, jnp.concatenate([x] * RW_Q, axis=0), 0.0).astype(bf16)

        ys = [[None] * RW_NQ for _ in range(B)]
        chains = [(b, q) for b in range(B) for q in range(RW_NQ)]
        for g0 in range(0, len(chains), RW_GROUP):
            grp = chains[g0:g0 + RW_GROUP]
            idx = [(slice(b * C, (b + 1) * C), slice(q * QW, (q + 1) * QW)) for b, q in grp]
            each = lambda fn: [fn(i) for i in range(len(grp))]
            vq = each(lambda i: v[idx[i]])
            ar = each(lambda i: jnp.concatenate([at[idx[i]], rt[idx[i]]], axis=0).astype(bf16))
            big = each(lambda i: _dot_nt(ar[i], jnp.concatenate([bd(bt[idx[i]]), bd(kt[idx[i]])], axis=0)))
            s0 = each(lambda i: st[grp[i]])
            asrs = each(lambda i: _dot_nt(ar[i], s0[i].astype(bf16)))
            nmat = each(lambda i: jnp.where(strict, big[i][:C, :QW], 0.0))
            akm = each(lambda i: jnp.where(strict, big[i][:C, QW:], 0.0).astype(bf16))
            rbk = each(lambda i: jnp.concatenate([jnp.where(incl, big[i][C:, :QW], 0.0),
                                                  jnp.where(incl, big[i][C:, QW:], 0.0)], axis=1).astype(bf16))
            tm = each(lambda i: eye - nmat[i])
            pw = each(lambda i: _dot(nmat[i].astype(bf16), bd(nmat[i])))
            lvl = 2
            while lvl < C:
                res = each(lambda i: _dot(jnp.concatenate([tm[i], pw[i]], axis=0).astype(bf16), bd(pw[i])))
                tm = each(lambda i: tm[i] + res[i][:C])
                pw = each(lambda i: res[i][C:])
                lvl *= 2
            vbd = each(lambda i: bd(vq[i]))
            rhs = each(lambda i: -(asrs[i][:C] + _dot(akm[i], vbd[i])))
            u = each(lambda i: _dot(tm[i].astype(bf16), bd(rhs[i])))
            y = each(lambda i: asrs[i][C:] + _dot(rbk[i], jnp.concatenate([bd(u[i]), vbd[i]], axis=0)))
            upd = each(lambda i: _dot_tn(jnp.concatenate([u[i], vq[i]], axis=0).astype(bf16),
                                         jnp.concatenate([bh[idx[i]], kh[idx[i]]], axis=0).astype(bf16)))
            for i, (b, q) in enumerate(grp):
                st[b, q] = s0[i] * gcr[b * C:b * C + 1, idx[i][1]] + jnp.where(blockmask, upd[i], 0.0)
                ys[b][q] = y[i]

        y = jnp.concatenate([jnp.concatenate(yb, axis=1) for yb in ys], axis=0)
        y_ref[...] = _rwkv_out(y, r, kf, v, g, pr).astype(bf16).reshape(B, C, RWKV_W)

        @pl.when(c == n - 1)
        def _():
            for b in range(B):
                for q in range(RW_NQ):
                    for h in range(RW_Q):
                        hs = slice(h * RWKV_N, (h + 1) * RWKV_N)
                        s_ref[b, q * RW_Q + h] = st[b, q, hs, hs]
                sh_ref[b] = rw[(b + 1) * C - 1:(b + 1) * C, 0:RWKV_PROJ]

    pspecs = [pl.BlockSpec(rp[k].shape, lambda c: (0, 0)) for k in _RW_PKEYS]
    y, s, sh = pl.pallas_call(
        body,
        grid=(n,),
        in_specs=[pl.BlockSpec((pl.Element(B), pl.Element(C), pl.Element(RW_BLK)),
                               lambda c: (0, pl.multiple_of(c * C, C), OFF_RW))] + pspecs,
        out_specs=[pl.BlockSpec((B, C, RWKV_W), lambda c: (0, c, 0)),
                   pl.BlockSpec((B, RWKV_H, RWKV_N, RWKV_N), lambda c: (0, 0, 0, 0)),
                   pl.BlockSpec((B, 1, RWKV_PROJ), lambda c: (0, 0, 0))],
        out_shape=[jax.ShapeDtypeStruct((B, L, RWKV_W), bf16),
                   jax.ShapeDtypeStruct((B, RWKV_H, RWKV_N, RWKV_N), f32),
                   jax.ShapeDtypeStruct((B, 1, RWKV_PROJ), f32)],
        scratch_shapes=[pltpu.VMEM((B, RW_NQ, QW, QW), f32), pltpu.VMEM((B, 8, RW_BLK), f32)],
        compiler_params=_cparams(("arbitrary",)),
    )(proj3, *[rp[k] for k in _RW_PKEYS])
    return y.reshape(B * L, RWKV_W), s, sh.reshape(B, RWKV_PROJ)


def _rwkv_step(proj, shift, s_t, l, buf, rp):
    B = proj.shape[0]
    N = RWKV_N
    shift_p = jnp.pad(shift, ((0, 0), (0, RW_BLK - RWKV_PROJ)))
    grid, wrap = _layer_grid(l, buf, (RWKV_H,))
    npk = len(_RW_PKEYS)
    vec_names = ('r', 'w', 'kf', 'v', 'kk', 'ka')

    def body(rw_ref, sh_ref, s_ref, *refs):
        pr = dict(zip(_RW_PKEYS, refs[:npk]))
        y_ref, so_ref = refs[-9:-7]
        vt = dict(zip(vec_names, refs[-7:-1]))
        yt = refs[-1]
        h = pl.program_id(len(grid) - 1)

        def update():
            @pl.when(h == 0)
            def _():
                r, lw, kf, v, kk, a, _ = _rwkv_mix(rw_ref[...], sh_ref[...], pr)
                for name, val in zip(vec_names, (r, jnp.exp(lw), kf, v, kk, kk * a)):
                    vt[name][...] = val.T

            hs = pl.ds(pl.multiple_of(h * N, N), N)
            s = s_ref[...]
            kk_h = vt['kk'][hs, :]
            sa = jnp.sum(s * (-kk_h)[None], axis=1, keepdims=True)
            s1 = s * vt['w'][hs, :][None] + sa * vt['ka'][hs, :][None] + vt['v'][hs, :][:, None, :] * vt['kf'][hs, :][None]
            so_ref[...] = s1
            yt[hs, :] = jnp.sum(s1 * vt['r'][hs, :][None], axis=1)

            @pl.when(h == RWKV_H - 1)
            def _():
                r, lw, kf, v, kk, a, g = _rwkv_mix(rw_ref[...], sh_ref[...], pr)
                y_ref[...] = _rwkv_out(yt[...].T, r, kf, v, g, pr).astype(bf16)

        if buf is not None:
            update()
        else:
            pl.when(pl.program_id(0) == l)(update)

            @pl.when(pl.program_id(0) != l)
            def _():
                so_ref[...] = jnp.zeros_like(so_ref)

    const = lambda d, ix, pk: (0, 0)
    in_specs = [pl.BlockSpec((pl.Element(B), pl.Element(RW_BLK)), wrap(lambda d, ix, pk: (0, OFF_RW))),
                pl.BlockSpec((B, RW_BLK), wrap(const)),
                pl.BlockSpec((None, None, N, N, B), wrap(lambda d, ix, pk: (l, pk[0], 0, 0, 0)))]
    in_specs += [pl.BlockSpec(rp[k].shape, wrap(const)) for k in _RW_PKEYS]
    args = [proj, shift_p, s_t] + [rp[k] for k in _RW_PKEYS]
    aliases = {}
    if buf is not None:
        in_specs.append(pl.BlockSpec(memory_space=pl.ANY))
        args.append(buf)
        aliases = {len(args) - 1: 1}
    return pl.pallas_call(
        body,
        grid=grid,
        in_specs=in_specs,
        out_specs=[pl.BlockSpec((B, RWKV_W), wrap(const)),
                   pl.BlockSpec((None, None, N, N, B), wrap(lambda d, ix, pk: (d, ix[0], 0, 0, 0)))],
        out_shape=[jax.ShapeDtypeStruct((B, RWKV_W), bf16),
                   jax.ShapeDtypeStruct((DEPTH, RWKV_H, N, N, B), f32)],
        scratch_shapes=[pltpu.VMEM((RWKV_W, B), f32)] * 7,
        input_output_aliases=aliases,
        compiler_params=_cparams(("arbitrary",) * len(grid)),
    )(*args)


S5_N = S5_G * S5_P
S5_KC = 256
S5_NKC = S5_W // S5_KC
S5_TILES = S5_N // LANE
S5_GB = S5_KC // S5_GC


def _s5_params(p):
    a_re, a_im = p['s5_a_re'], p['s5_a_im']
    dstep = jnp.exp(p['s5_log_dt'])[:, None]
    mag = jnp.exp(a_re * dstep)
    ab_re = mag * jnp.cos(a_im * dstep)
    ab_im = mag * jnp.sin(a_im * dstep)
    den = a_re * a_re + a_im * a_im
    n_re = ab_re - 1.0
    f_re = (n_re * a_re + ab_im * a_im) / den
    f_im = (ab_im * a_re - n_re * a_im) / den
    b_re, b_im = p['s5_b_re'], p['s5_b_im']
    bb_re = f_re[..., None] * b_re - f_im[..., None] * b_im
    bb_im = f_re[..., None] * b_im + f_im[..., None] * b_re
    eye = jnp.eye(S5_GB, dtype=f32)

    def in_map(bb):
        t = bb.reshape(S5_NKC, S5_GB, S5_P, S5_GC)
        return jnp.einsum('kgpc,gh->kgchp', t, eye).reshape(S5_NKC, S5_KC, S5_GB * S5_P)

    def out_map(cc):
        t = cc.reshape(S5_NKC, S5_GB, S5_GC, S5_P)
        return jnp.einsum('qgcp,gh->qgphc', t, eye).reshape(S5_NKC, S5_GB * S5_P, S5_KC)

    return dict(wb=jnp.concatenate([in_map(bb_re), in_map(bb_im)], axis=-1),
                wc_re=out_map(p['s5_c_re']), wc_im=out_map(p['s5_c_im']),
                ab_re_t=ab_re.reshape(S5_TILES // 8, 8, LANE), ab_im_t=ab_im.reshape(S5_TILES // 8, 8, LANE),
                ab_re=ab_re.reshape(1, S5_N), ab_im=ab_im.reshape(1, S5_N), d=p['s5_d'].reshape(1, S5_W))


def _s5_seq(proj, sp, B, L):
    Lc = min(L, 256)
    n = L // Lc
    pitch = Lc + 4
    lead = OFF_U % LANE
    base = OFF_U - lead
    width = S5_W + LANE
    nt4 = S5_TILES // 8
    tiles_kc = S5_TILES // S5_NKC

    def body(u_ref, wb_ref, wcr_ref, wci_ref, abr_ref, abi_ref, d_ref, y_ref, sr_ref, si_ref, xr, xi, cr, ci):
        c = pl.program_id(1)

        @pl.when(c == 0)
        def _():
            cr[...] = jnp.zeros_like(cr)
            ci[...] = jnp.zeros_like(ci)

        u = u_ref[:, lead:lead + S5_W]
        ub = u.astype(bf16)
        for kc in range(S5_NKC):
            bu = _dot(ub[:, kc * S5_KC:(kc + 1) * S5_KC], wb_ref[kc].astype(bf16))
            for j in range(tiles_kc):
                t = kc * tiles_kc + j
                xr[pl.ds(t * pitch, Lc), :] = bu[:, j * LANE:(j + 1) * LANE]
                xi[pl.ds(t * pitch, Lc), :] = bu[:, (tiles_kc + j) * LANE:(tiles_kc + j + 1) * LANE]

        abr = [abr_ref[g] for g in range(nt4)]
        abi = [abi_ref[g] for g in range(nt4)]

        def step(t, carry):
            out = []
            for g in range(nt4):
                s_r, s_i = carry[2 * g], carry[2 * g + 1]
                idx = pl.ds(g * 8 * pitch + t, 8, stride=pitch)
                n_r = abr[g] * s_r - abi[g] * s_i + xr[idx, :]
                n_i = abr[g] * s_i + abi[g] * s_r + xi[idx, :]
                xr[idx, :] = n_r
                xi[idx, :] = n_i
                out += [n_r, n_i]
            return tuple(out)

        init = []
        for g in range(nt4):
            init += [cr[g], ci[g]]
        fin = lax.fori_loop(0, Lc, step, tuple(init), unroll=2)
        for g in range(nt4):
            cr[g] = fin[2 * g]
            ci[g] = fin[2 * g + 1]

        for q in range(S5_NKC):
            lr = jnp.concatenate([xr[pl.ds((q * tiles_kc + j) * pitch, Lc), :] for j in range(tiles_kc)], axis=1)
            li = jnp.concatenate([xi[pl.ds((q * tiles_kc + j) * pitch, Lc), :] for j in range(tiles_kc)], axis=1)
            y = _dot(lr.astype(bf16), wcr_ref[q].astype(bf16)) - _dot(li.astype(bf16), wci_ref[q].astype(bf16))
            cs = slice(q * S5_KC, (q + 1) * S5_KC)
            y = y + d_ref[:, cs] * u[:, cs]
            y_ref[:, cs] = jax.nn.gelu(y).astype(bf16)

        @pl.when(c == n - 1)
        def _():
            sr_ref[...] = cr[...]
            si_ref[...] = ci[...]

    full = lambda shp: pl.BlockSpec(shp, lambda b, c: (0,) * len(shp))
    y, sr, si = pl.pallas_call(
        body,
        grid=(B, n),
        in_specs=[pl.BlockSpec((pl.Element(Lc), pl.Element(width)),
                               lambda b, c: (pl.multiple_of((b * n + c) * Lc, Lc), base)),
                  full(sp['wb'].shape), full(sp['wc_re'].shape), full(sp['wc_im'].shape),
                  full(sp['ab_re_t'].shape), full(sp['ab_im_t'].shape), full((1, S5_W))],
        out_specs=[pl.BlockSpec((Lc, S5_W), lambda b, c: (b * n + c, 0)),
                   pl.BlockSpec((None, nt4, 8, LANE), lambda b, c: (b, 0, 0, 0)),
                   pl.BlockSpec((None, nt4, 8, LANE), lambda b, c: (b, 0, 0, 0))],
        out_shape=[jax.ShapeDtypeStruct((B * L, S5_W), bf16),
                   jax.ShapeDtypeStruct((B, nt4, 8, LANE), f32),
                   jax.ShapeDtypeStruct((B, nt4, 8, LANE), f32)],
        scratch_shapes=[pltpu.VMEM((S5_TILES * pitch, LANE), f32), pltpu.VMEM((S5_TILES * pitch, LANE), f32),
                        pltpu.VMEM((nt4, 8, LANE), f32), pltpu.VMEM((nt4, 8, LANE), f32)],
        compiler_params=_cparams(("parallel", "arbitrary")),
    )(proj, sp['wb'], sp['wc_re'], sp['wc_im'], sp['ab_re_t'], sp['ab_im_t'], sp['d'])
    return y, sr.reshape(B, S5_G, S5_P), si.reshape(B, S5_G, S5_P)


def _s5_step(proj, x_re, x_im, sp):
    B = proj.shape[0]
    lead = OFF_U % LANE
    base = OFF_U - lead
    width = S5_W + LANE
    kw = S5_N // S5_NKC

    def body(u_ref, xr_ref, xi_ref, wb_ref, wcr_ref, wci_ref, abr_ref, abi_ref, d_ref, y_ref, sr_ref, si_ref):
        u = u_ref[:, lead:lead + S5_W]
        ub = u.astype(bf16)
        abr, abi = abr_ref[...], abi_ref[...]
        xr, xi = xr_ref[...], xi_ref[...]
        for kc in range(S5_NKC):
            bu = _dot(ub[:, kc * S5_KC:(kc + 1) * S5_KC], wb_ref[kc].astype(bf16))
            sl = slice(kc * kw, (kc + 1) * kw)
            n_r = abr[:, sl] * xr[:, sl] - abi[:, sl] * xi[:, sl] + bu[:, :kw]
            n_i = abr[:, sl] * xi[:, sl] + abi[:, sl] * xr[:, sl] + bu[:, kw:]
            sr_ref[:, sl] = n_r
            si_ref[:, sl] = n_i
            y = _dot(n_r.astype(bf16), wcr_ref[kc].astype(bf16)) - _dot(n_i.astype(bf16), wci_ref[kc].astype(bf16))
            cs = slice(kc * S5_KC, (kc + 1) * S5_KC)
            y = y + d_ref[:, cs] * u[:, cs]
            y_ref[:, cs] = jax.nn.gelu(y).astype(bf16)

    full = lambda shp: pl.BlockSpec(shp, lambda i: (0,) * len(shp))
    y, sr, si = pl.pallas_call(
        body,
        grid=(1,),
        in_specs=[pl.BlockSpec((pl.Element(B), pl.Element(width)), lambda i: (0, base)),
                  full((B, S5_N)), full((B, S5_N)),
                  full(sp['wb'].shape), full(sp['wc_re'].shape), full(sp['wc_im'].shape),
                  full((1, S5_N)), full((1, S5_N)), full((1, S5_W))],
        out_specs=[full((B, S5_W)), full((B, S5_N)), full((B, S5_N))],
        out_shape=[jax.ShapeDtypeStruct((B, S5_W), bf16),
                   jax.ShapeDtypeStruct((B, S5_N), f32), jax.ShapeDtypeStruct((B, S5_N), f32)],
        compiler_params=_cparams(("arbitrary",)),
    )(proj, x_re.reshape(B, S5_N), x_im.reshape(B, S5_N), sp['wb'], sp['wc_re'], sp['wc_im'],
      sp['ab_re'], sp['ab_im'], sp['d'])
    return y, sr.reshape(B, S5_G, S5_P), si.reshape(B, S5_G, S5_P)


MOE_RT, MOE_RT_SMALL = 256, 32
SLAB = D // LANE
SUB = 8
SLAB_PITCH = SLAB + SUB
H2_PITCH = SLAB + SUB
O2_PITCH = TOP_K * SLAB + SUB


def _moe_rt(T):
    return MOE_RT if T * TOP_K >= N_EXPERTS * MOE_RT else MOE_RT_SMALL
ROUTE_W = LANE


def _router(x, g, mod, w_r, b_r, row0, t_all, bufs):
    T = x.shape[0]
    tm = min(mod.tm, 512)
    nm = T // tm
    assert row0 % tm == 0
    if bufs is None:
        assert row0 == 0
        n_steps = pl.cdiv(t_all, tm)
    else:
        n_steps = nm
    clamp = lambda m: jnp.minimum(m, nm - 1)

    def body(x_ref, g_ref, sh_ref, sc_ref, w_ref, b_ref, *rest):
        h_ref, e_ref, p_ref = rest[-3:]
        if n_steps > nm:
            @pl.when(pl.program_id(0) >= nm)
            def _():
                h_ref[...] = jnp.zeros_like(h_ref)
                e_ref[...] = jnp.zeros_like(e_ref)
                p_ref[...] = jnp.zeros_like(p_ref)

            pl.when(pl.program_id(0) < nm)(lambda: route(x_ref, g_ref, sh_ref, sc_ref, w_ref, b_ref, *rest[-3:]))
        else:
            route(x_ref, g_ref, sh_ref, sc_ref, w_ref, b_ref, *rest[-3:])

    def route(x_ref, g_ref, sh_ref, sc_ref, w_ref, b_ref, h_ref, e_ref, p_ref):
        h2 = _rms(x_ref[...], g_ref[...]) * (1.0 + sc_ref[...]) + sh_ref[...]
        for s in range(SLAB):
            h_ref[pl.ds(s, tm, stride=H2_PITCH), :] = h2[:, s * LANE:(s + 1) * LANE]
        for s in range(SLAB, H2_PITCH):
            h_ref[pl.ds(s, tm, stride=H2_PITCH), :] = jnp.zeros((tm, LANE), f32)
        logits = jnp.dot(h2, w_ref[...], precision=lax.Precision.HIGHEST, preferred_element_type=f32) + b_ref[...]
        lane = lax.broadcasted_iota(i32, (tm, ROUTE_W), 1)
        ninf = jnp.float32(-jnp.inf)
        gl = jnp.where(lane < N_GROUPS, logits, ninf)
        gm = jnp.max(gl, axis=-1, keepdims=True)
        g_p = 1.0 / jnp.sum(jnp.exp(gl - gm), axis=-1, keepdims=True)
        g_idx = jnp.min(jnp.where(gl == gm, lane, ROUTE_W), axis=-1, keepdims=True)
        valid = (lane >= N_GROUPS) & (lane < N_GROUPS + N_EXPERTS) & (((lane - N_GROUPS) // EPG) == g_idx)
        el = jnp.where(valid, logits, ninf)
        ee = jnp.exp(el - jnp.max(el, axis=-1, keepdims=True))
        prob = jnp.where(valid, ee / jnp.sum(ee, axis=-1, keepdims=True), -1.0)
        p1 = jnp.max(prob, axis=-1, keepdims=True)
        i1 = jnp.min(jnp.where(prob == p1, lane, ROUTE_W), axis=-1, keepdims=True)
        prob2 = jnp.where(lane == i1, -1.0, prob)
        p2 = jnp.max(prob2, axis=-1, keepdims=True)
        i2 = jnp.min(jnp.where(prob2 == p2, lane, ROUTE_W), axis=-1, keepdims=True)
        den = p1 + p2
        e_ref[...] = jnp.where(lane == 0, i1 - N_GROUPS, jnp.where(lane == 1, i2 - N_GROUPS, 0))
        p_ref[...] = jnp.where(lane == 0, g_p * p1 / den, jnp.where(lane == 1, g_p * p2 / den, 0.0))

    in_specs = [pl.BlockSpec((tm, D), lambda m: (clamp(m), 0)),
                pl.BlockSpec((1, D), lambda m: (0, 0)),
                mod.row_spec(3, tm, clamp), mod.row_spec(4, tm, clamp),
                pl.BlockSpec((D, ROUTE_W), lambda m: (0, 0)),
                pl.BlockSpec((1, ROUTE_W), lambda m: (0, 0))]
    args = [x, g.reshape(1, D), mod.arr, mod.arr, w_r, b_r]
    aliases = {}
    if bufs is not None:
        in_specs += [pl.BlockSpec(memory_space=pl.ANY)] * 3
        aliases = {len(args) + i: i for i in range(3)}
        args += list(bufs)
    b0 = row0 // tm
    return pl.pallas_call(
        body,
        grid=(n_steps,),
        in_specs=in_specs,
        out_specs=[pl.BlockSpec((tm * H2_PITCH, LANE), lambda m: (b0 + m, 0)),
                   pl.BlockSpec((tm, ROUTE_W), lambda m: (b0 + m, 0)),
                   pl.BlockSpec((tm, ROUTE_W), lambda m: (b0 + m, 0))],
        out_shape=[jax.ShapeDtypeStruct((t_all * H2_PITCH, LANE), f32), jax.ShapeDtypeStruct((t_all, ROUTE_W), i32),
                   jax.ShapeDtypeStruct((t_all, ROUTE_W), f32)],
        input_output_aliases=aliases,
        compiler_params=_cparams(("arbitrary",)),
    )(*args)


TOK_BITS = 14


def _moe_plan(eid, T):
    A = T * TOP_K
    RT = _moe_rt(T)
    nt = pl.cdiv(A, RT) + N_EXPERTS
    flat_e = eid[:, :TOP_K].reshape(-1)
    order = jnp.argsort(flat_e).astype(i32)
    counts = jnp.sum((flat_e[:, None] == jnp.arange(N_EXPERTS, dtype=i32)[None, :]).astype(i32), axis=0)
    start = jnp.cumsum(counts) - counts
    pcnt = (counts + RT - 1) // RT * RT
    pend = jnp.cumsum(pcnt)
    tile_e = jnp.minimum(jnp.sum((pend[None, :] <= (jnp.arange(nt, dtype=i32) * RT)[:, None]).astype(i32), axis=1),
                         N_EXPERTS - 1)
    off = (jnp.arange(nt, dtype=i32) * RT - (pend - pcnt)[tile_e])[:, None] + jnp.arange(RT, dtype=i32)[None, :]
    real = off < counts[tile_e][:, None]
    srt = start[tile_e][:, None] + jnp.minimum(off, counts[tile_e][:, None])
    a = order[jnp.minimum(srt, A - 1)]
    slot = jnp.arange(nt * RT, dtype=i32).reshape(nt, RT)
    tok = jnp.where(real, a // TOP_K, 0)
    dst = jnp.where(real, a, A + slot - srt)
    n_used = (pend[-1] // RT).astype(i32).reshape(1)
    eidx = jnp.arange(N_EXPERTS, dtype=i32)
    in_use = counts > 0
    rank = jnp.cumsum(in_use.astype(i32)) - 1
    after = lax.cummin(jnp.where(in_use, eidx, N_EXPERTS), reverse=True)
    nxt = jnp.concatenate([after[1:], jnp.full((1,), N_EXPERTS, i32)])
    nxt = jnp.where(nxt < N_EXPERTS, nxt, -1)
    tile_info = jnp.concatenate([tile_e, (rank % 2)[tile_e], nxt[tile_e]]).astype(i32)
    return (tok | (dst << TOK_BITS)).reshape(-1), tile_info, n_used


def _experts(h2, packed, tile_e, n_used, l, w1, w3, w2):
    T = h2.shape[0] // H2_PITCH
    assert TOP_K == 2 and T < (1 << TOK_BITS)
    RT = _moe_rt(T)
    PITCH = SLAB_PITCH
    nt = tile_e.shape[0] // 3
    prime_id = nt * RT
    out_rows = (prime_id + RT) // TOP_K
    HC, OC = 128, 256
    n_hc, n_oc = D_EXPERT // HC, D // OC
    g_per, s_per = RT // n_hc, RT // n_oc

    def body(tile_ref, nused_ref, slot_ref, h_hbm, w1_hbm, w3_hbm, w2_hbm, o_hbm,
             x0, x1, o0, o1, w1b, w3b, w2b, wf1, wf3, wf2, gsem, ssem, wsem):
        j = pl.program_id(0)
        n_used = nused_ref[0]
        active = j < n_used

        def gather_row(x_ref, r, tok):
            src = pl.multiple_of(tok * H2_PITCH, SUB)
            return pltpu.make_async_copy(h_hbm.at[pl.ds(src, SLAB), :], x_ref.at[pl.ds(r * PITCH, SLAB), :], gsem)

        def scatter_row(o_ref, r, d):
            dst = pl.multiple_of(lax.shift_right_logical(d, 1) * O2_PITCH + (d & (TOP_K - 1)) * SLAB, SUB)
            return pltpu.make_async_copy(o_ref.at[pl.ds(r * PITCH, SLAB), :], o_hbm.at[pl.ds(dst, SLAB), :], ssem)

        def wait_rows(sem):
            pltpu.make_async_copy(h_hbm.at[pl.ds(0, RT * SLAB), :], x0.at[pl.ds(0, RT * SLAB), :], sem).wait()

        wait_gather = lambda: wait_rows(gsem)
        wait_scatter = lambda: wait_rows(ssem)

        tok_of = lambda s: s & ((1 << TOK_BITS) - 1)
        dst_of = lambda s: lax.shift_right_logical(s, TOK_BITS)

        @pl.when(j == 0)
        def _():
            o1[...] = jnp.zeros_like(o1)

            def one(r, _):
                gather_row(x0, r, tok_of(slot_ref[r])).start()
                return 0
            lax.fori_loop(0, RT, one, 0)

        def fetch(e, s):
            return [pltpu.make_async_copy(w_hbm.at[l, e], wf.at[s], wsem.at[s])
                    for w_hbm, wf in ((w1_hbm, wf1), (w3_hbm, wf3), (w2_hbm, wf2))]

        @pl.when(j == 0)
        def _():
            for c in fetch(tile_ref[0], 0):
                c.start()

        @pl.when(active & ((j == 0) | (tile_ref[j] != tile_ref[jnp.maximum(j - 1, 0)])))
        def _():
            ws, e_nxt = tile_ref[nt + j], tile_ref[2 * nt + j]
            for c in fetch(tile_ref[j], ws):
                c.wait()
            w1b[...] = wf1[ws].astype(bf16)
            w3b[...] = wf3[ws].astype(bf16)
            w2b[...] = wf2[ws].astype(bf16)

            @pl.when(e_nxt >= 0)
            def _():
                for c in fetch(e_nxt, 1 - ws):
                    c.start()

        @pl.when(active & (j >= 1))
        def _():
            wait_scatter()

        def tile(x_cur, x_nxt, o_cur, o_prv):
            wait_gather()
            xb = jnp.concatenate([x_cur[pl.ds(s, RT, stride=PITCH), :] for s in range(SLAB)], axis=1).astype(bf16)
            nxt = jnp.minimum(j + 1, n_used - 1) * RT
            prv = jnp.maximum(j - 1, 0) * RT
            parts = []
            for c in range(n_hc):
                h1 = _dot(xb, w1b[:, c * HC:(c + 1) * HC])
                h3 = _dot(xb, w3b[:, c * HC:(c + 1) * HC])
                parts.append(((h1 * jax.nn.sigmoid(h1)) * h3).astype(bf16))
                for r in range(c * g_per, (c + 1) * g_per):
                    gather_row(x_nxt, r, tok_of(slot_ref[nxt + r])).start(priority=r % 2)
            hm = jnp.concatenate(parts, axis=1)
            for c in range(n_oc):
                res = _dot(hm, w2b[:, c * OC:(c + 1) * OC])
                for t in range(OC // LANE):
                    o_cur[pl.ds(c * (OC // LANE) + t, RT, stride=PITCH), :] = res[:, t * LANE:(t + 1) * LANE]
                for r in range(c * s_per, (c + 1) * s_per):
                    d = jnp.where(j == 0, prime_id + r, dst_of(slot_ref[prv + r]))
                    scatter_row(o_prv, r, d).start(priority=r % 2)

        def last(o_cur):
            wait_scatter()

            def one(r, _):
                scatter_row(o_cur, r, dst_of(slot_ref[j * RT + r])).start()
                return 0
            lax.fori_loop(0, RT, one, 0)
            wait_scatter()
            wait_gather()

        for par, (xc, xn, oc, op) in enumerate(((x0, x1, o0, o1), (x1, x0, o1, o0))):
            pl.when(active & (j % 2 == par))(lambda xc=xc, xn=xn, oc=oc, op=op: tile(xc, xn, oc, op))
        for par, oc in enumerate((o0, o1)):
            pl.when((j == n_used - 1) & (j % 2 == par))(lambda oc=oc: last(oc))

    grid_spec = pltpu.PrefetchScalarGridSpec(
        num_scalar_prefetch=3,
        grid=(nt,),
        in_specs=[pl.BlockSpec(memory_space=pl.ANY)] * 4,
        out_specs=pl.BlockSpec(memory_space=pl.ANY),
        scratch_shapes=[pltpu.VMEM((RT * PITCH, LANE), f32), pltpu.VMEM((RT * PITCH, LANE), f32),
                        pltpu.VMEM((RT * PITCH, LANE), f32), pltpu.VMEM((RT * PITCH, LANE), f32),
                        pltpu.VMEM((D, D_EXPERT), bf16), pltpu.VMEM((D, D_EXPERT), bf16),
                        pltpu.VMEM((D_EXPERT, D), bf16),
                        pltpu.VMEM((2, D, D_EXPERT), f32), pltpu.VMEM((2, D, D_EXPERT), f32),
                        pltpu.VMEM((2, D_EXPERT, D), f32),
                        pltpu.SemaphoreType.DMA(()), pltpu.SemaphoreType.DMA(()), pltpu.SemaphoreType.DMA((2,))],
    )
    return pl.pallas_call(
        body,
        grid_spec=grid_spec,
        out_shape=jax.ShapeDtypeStruct((out_rows * O2_PITCH, LANE), f32),
        compiler_params=_cparams(("arbitrary",)),
    )(tile_e, n_used, packed, h2, w1, w3, w2)


def _moe_combine(x, o2, wgt, mod, norm_final, row0):
    T = x.shape[0]
    tm = min(mod.tm, 512)
    b0 = row0 // tm

    def body(x_ref, o_ref, p_ref, m_ref, *rest):
        p = p_ref[...]
        row = lambda k: jnp.concatenate(
            [o_ref[pl.ds(k * SLAB + s, tm, stride=O2_PITCH), :] for s in range(SLAB)], axis=1)
        y = p[:, 0:1] * row(0)
        for k in range(1, TOP_K):
            y = y + p[:, k:k + 1] * row(k)
        xo = x_ref[...] + m_ref[...] * y
        if norm_final is None:
            rest[0][...] = xo
        else:
            rest[1][...] = _rms(xo, rest[0][...])

    ins = [x, o2, wgt, mod.arr]
    specs = [pl.BlockSpec((tm, D), lambda m: (m, 0)),
             pl.BlockSpec((tm * O2_PITCH, LANE), lambda m: (b0 + m, 0)),
             pl.BlockSpec((tm, ROUTE_W), lambda m: (b0 + m, 0)), mod.row_spec(5, tm)]
    if norm_final is not None:
        ins.append(norm_final.reshape(1, D))
        specs.append(pl.BlockSpec((1, D), lambda m: (0, 0)))
    return pl.pallas_call(
        body,
        grid=(T // tm,),
        in_specs=specs,
        out_specs=pl.BlockSpec((tm, D), lambda m: (m, 0)),
        out_shape=jax.ShapeDtypeStruct((T, D), f32),
        compiler_params=_cparams(("parallel",)),
    )(*ins)


def _moe(groups, p, l, big, norm_final):
    pad = ROUTE_W - N_GROUPS - N_EXPERTS
    w_r = jnp.pad(jnp.concatenate([p['moe_w_group'], p['moe_w_router']], axis=1), ((0, 0), (0, pad)))
    b_r = jnp.pad(jnp.concatenate([p['moe_b_group'], p['moe_b_router']]), (0, pad)).reshape(1, ROUTE_W)
    t_all = sum(g.T for g in groups)
    bufs, row0 = None, 0
    for g in groups:
        g.row0 = row0
        bufs = _router(g.x, p['norm_ffn'], g.mod, w_r, b_r, row0, t_all, bufs)
        row0 += g.T
    h2, eid, wgt = bufs
    packed, tile_e, n_used = _moe_plan(eid, t_all)
    o2 = _experts(h2, packed, tile_e, n_used, l, big['moe_w1'], big['moe_w3'], big['moe_w2'])
    for g in groups:
        g.x = _moe_combine(g.x, o2, wgt, g.mod, norm_final, g.row0)


class _Group:
    def __init__(self, x, mods, states, pos0):
        self.B, self.L, _ = x.shape
        self.T = self.B * self.L
        self.tm = _row_tile(self.T)
        self.x = x.reshape(self.T, D)
        self.mods, self.states, self.pos0 = mods, states, pos0
        self.outs = ([], [], [], [], [])
        self.ret_all = self.rw_all = None

    def mix(self, l, p, rp, sp, layers, big, w_in_t):
        B, L = self.B, self.L
        self.mod = mod = _Mod(self.mods[l], L, self.tm)
        proj = _in_proj(self.x, layers['norm_mix'], l, mod, w_in_t)
        if self.states is None:
            y_ret, s_ret = _retention_seq(proj, B, L)
            y_rw, s_rw, shift = _rwkv_seq(proj, rp, B, L)
            y_s5, s_re, s_im = _s5_seq(proj, sp, B, L)
        else:
            st_ret, st_rw, st_shift, st_re, st_im = self.states
            y_ret, self.ret_all = _retention_step(proj, st_ret, l, self.ret_all, self.pos0)
            y_rw, self.rw_all = _rwkv_step(proj, st_shift[l], jnp.transpose(st_rw, (0, 2, 3, 4, 1)), l,
                                           self.rw_all, rp)
            s_ret = s_rw = None
            shift = proj[:, OFF_RW:OFF_RW + RWKV_PROJ]
            y_s5, s_re, s_im = _s5_step(proj, st_re[l], st_im[l], sp)
        z = _glu_proj(y_s5, l, big['s5_w_glu'])
        merged = _merge_proj(y_ret, y_rw, z, l, big['ret_w_o'], big['rwkv_w_o'], big['s5_w_o'], proj)
        self.x = _out_proj(merged, l, big['w_out'], self.x, mod)
        for lst, val in zip(self.outs, (s_ret, s_rw, shift, s_re, s_im)):
            lst.append(val)

    def results(self):
        stacked = [jnp.stack(o) if o[0] is not None else None for o in self.outs]
        if self.states is not None:
            stacked[0], stacked[1] = self.ret_all, jnp.transpose(self.rw_all, (0, 4, 1, 2, 3))
        return self.x.reshape(self.B, self.L, D), stacked


def _trunk(groups, layers, norm_final):
    big = {k: layers[k] for k in _BIG}
    w_in_t = jnp.swapaxes(layers['w_in'], 1, 2)
    for l in range(DEPTH):
        p = {name: arr[l] for name, arr in layers.items() if name not in _BIG + ('w_in',)}
        rp, sp = _rwkv_params(p), _s5_params(p)
        for g in groups:
            g.mix(l, p, rp, sp, layers, big, w_in_t)
        _moe(groups, p, l, big, norm_final if l == DEPTH - 1 else None)
    return [g.results() for g in groups]


_BIG = ('ret_w_o', 'rwkv_w_o', 's5_w_glu', 's5_w_o', 'w_out', 'moe_w1', 'moe_w3', 'moe_w2')


def kernel(x_prompt, x_sample, state_ret, state_rwkv, state_shift, state_s5_re, state_s5_im,
           c_prompt, c_sample, norm_mix, norm_ffn, w_ada, b_ada, w_in, ret_w_o, rwkv_mu, rwkv_w0,
           rwkv_w2, rwkv_a0, rwkv_a2, rwkv_g2, rwkv_k_k, rwkv_k_a, rwkv_r_k, rwkv_ln_w, rwkv_ln_b,
           rwkv_w_o, s5_a_re, s5_a_im, s5_b_re, s5_b_im, s5_c_re, s5_c_im, s5_d, s5_log_dt, s5_w_glu,
           s5_w_o, w_out, moe_w_group, moe_b_group, moe_w_router, moe_b_router, moe_w1, moe_w3, moe_w2,
           norm_final):
    layers = {
        'norm_mix': norm_mix, 'norm_ffn': norm_ffn, 'w_in': w_in,
        'ret_w_o': ret_w_o, 'rwkv_mu': rwkv_mu, 'rwkv_w0': rwkv_w0, 'rwkv_w2': rwkv_w2,
        'rwkv_a0': rwkv_a0, 'rwkv_a2': rwkv_a2, 'rwkv_g2': rwkv_g2, 'rwkv_k_k': rwkv_k_k,
        'rwkv_k_a': rwkv_k_a, 'rwkv_r_k': rwkv_r_k, 'rwkv_ln_w': rwkv_ln_w, 'rwkv_ln_b': rwkv_ln_b,
        'rwkv_w_o': rwkv_w_o, 's5_a_re': s5_a_re, 's5_a_im': s5_a_im, 's5_b_re': s5_b_re,
        's5_b_im': s5_b_im, 's5_c_re': s5_c_re, 's5_c_im': s5_c_im, 's5_d': s5_d,
        's5_log_dt': s5_log_dt, 's5_w_glu': s5_w_glu, 's5_w_o': s5_w_o, 'w_out': w_out,
        'moe_w_group': moe_w_group, 'moe_b_group': moe_b_group, 'moe_w_router': moe_w_router,
        'moe_b_router': moe_b_router, 'moe_w1': moe_w1, 'moe_w3': moe_w3, 'moe_w2': moe_w2,
    }
    Bp, Bs = x_prompt.shape[0], x_sample.shape[0]
    s_off = -(-Bp // 8) * 8
    c_all = jnp.concatenate([c_prompt, jnp.zeros((s_off - Bp, D), f32), c_sample], axis=0)
    mod_all = _adaln(c_all, w_ada, b_ada)
    prompt = _Group(x_prompt, mod_all[:, :Bp], None, 0.0)
    sample = _Group(x_sample, mod_all[:, s_off:s_off + Bs],
                    (state_ret, state_rwkv, state_shift, state_s5_re, state_s5_im), float(PAST_LEN))
    ((y_prompt, (ret_p, rwkv_p, shift_p, s5re_p, s5im_p)),
     (y_sample, (ret_s, rwkv_s, shift_s, s5re_s, s5im_s))) = _trunk([prompt, sample], layers, norm_final)
    return (y_prompt, y_sample, ret_p, ret_s, rwkv_p, rwkv_s, shift_p, shift_s, s5re_p, s5re_s, s5im_p, s5im_s)
```

```python
import jax
import jax.numpy as jnp
from jax import lax
from jax.experimental import pallas as pl
from jax.experimental.pallas import tpu as pltpu

f32 = jnp.float32
bf16 = jnp.bfloat16
i32 = jnp.int32

D = 2048
DEPTH = 2
PAST_LEN = 16384
RET_W, RET_H, RET_DK, RET_DV, RET_CHUNK = 1024, 4, 256, 256, 128
RET_GN_EPS = 1e-6
ROPE_BASE = 10000.0
RWKV_W, RWKV_N, RWKV_H = 1024, 64, 16
DECAY_LORA, AAA_LORA, GATE_LORA = 64, 64, 160
RWKV_PROJ = 3 * RWKV_W + DECAY_LORA + AAA_LORA + GATE_LORA
RWKV_LN_EPS = 64e-5
S5_W, S5_GC, S5_G, S5_P = 1024, 16, 64, 64
N_MOD = 6
RMS_EPS = 1e-6
N_GROUPS, EPG, N_EXPERTS, TOP_K, D_EXPERT = 4, 8, 32, 2, 512
IN_W = 4 * RET_W + RWKV_PROJ + S5_W + 3 * D
OFF_Q, OFF_K, OFF_V, OFF_G, OFF_RW = 0, 1024, 2048, 3072, 4096
OFF_U = OFF_RW + RWKV_PROJ
OFF_GATE = OFF_U + S5_W
LANE = 128
PROJ_W = ((IN_W + LANE - 1) // LANE) * LANE
VMEM_LIMIT = 56 * 1024 * 1024


def _cparams(sem):
    return pltpu.CompilerParams(dimension_semantics=sem, vmem_limit_bytes=VMEM_LIMIT)


def _dot(a, b):
    return jnp.dot(a, b, preferred_element_type=f32)


def _dot_nt(a, b):
    return lax.dot_general(a, b, (((1,), (1,)), ((), ())), preferred_element_type=f32)


def _dot_tn(a, b):
    return lax.dot_general(a, b, (((0,), (0,)), ((), ())), preferred_element_type=f32)


def _rms(x, g):
    return x * lax.rsqrt(jnp.mean(x * x, axis=-1, keepdims=True) + RMS_EPS) * g


def _head_norm(y, eps):
    mu = jnp.mean(y, axis=-1, keepdims=True)
    yc = y - mu
    return yc * lax.rsqrt(jnp.mean(yc * yc, axis=-1, keepdims=True) + eps)


def _row_tile(T):
    return 1024 if T >= 1024 else T


class _Mod:
    def __init__(self, mod, L, tm):
        self.L, self.tm = L, tm
        self.per_token = L == 1
        self.arr = mod if self.per_token else mod.reshape(mod.shape[0], 1, N_MOD * D)

    def spec(self, j, tn, col_of):
        nb = D // tn
        if self.per_token:
            return pl.BlockSpec((self.tm, tn), lambda m, n: (m, j * nb + col_of(n)))
        L, tm = self.L, self.tm
        return pl.BlockSpec((None, 1, tn), lambda m, n: ((m * tm) // L, 0, j * nb + col_of(n)))

    def row_spec(self, j, tm, remap=lambda m: m):
        if self.per_token:
            return pl.BlockSpec((tm, D), lambda m: (remap(m), j))
        L = self.L
        return pl.BlockSpec((None, 1, D), lambda m: ((remap(m) * tm) // L, 0, j))


def _fused_mm(x_ops, w_ops, e_ops, pre, post, *, grid, out_specs, out_shape, cache_shapes):
    nx, nw, ne = len(x_ops), len(w_ops), len(e_ops)
    n_out = len(out_shape)

    def body(*refs):
        x_refs = refs[:nx]
        w_refs = refs[nx:nx + nw]
        e_refs = refs[nx + nw:nx + nw + ne]
        o_refs = refs[nx + nw + ne:nx + nw + ne + n_out]
        caches = refs[nx + nw + ne + n_out:]
        if cache_shapes:
            @pl.when(pl.program_id(1) == 0)
            def _():
                for i in range(nx):
                    caches[i][...] = pre(i, x_refs[i], e_refs).astype(bf16)
            lhs = [c[...] for c in caches]
        else:
            lhs = [x[...] for x in x_refs]
        prods = [(_dot_nt if len(w_ops[j]) > 3 and w_ops[j][3] else _dot)(
            lhs[w_ops[j][2]], w_refs[j][...].astype(bf16)) for j in range(nw)]
        for o_ref, o in zip(o_refs, post(prods, e_refs)):
            o_ref[...] = o.astype(o_ref.dtype)

    return pl.pallas_call(
        body,
        grid=grid,
        in_specs=[s for _, s in x_ops] + [w[1] for w in w_ops] + [s for _, s in e_ops],
        out_specs=out_specs,
        out_shape=out_shape,
        scratch_shapes=[pltpu.VMEM(s, bf16) for s in cache_shapes],
        compiler_params=_cparams(("parallel", "arbitrary")),
    )(*[a for a, _ in x_ops], *[w[0] for w in w_ops], *[a for a, _ in e_ops])


def _adaln(c_all, w_ada, b_ada):
    R = c_all.shape[0]
    tn = 1024

    def pre(i, x_ref, e_refs):
        c = x_ref[...]
        return c * jax.nn.sigmoid(c)

    def post(prods, e_refs):
        return (prods[0] + e_refs[0][...],)

    (out,) = _fused_mm(
        [(c_all, pl.BlockSpec((R, D), lambda l, n: (0, 0)))],
        [(w_ada, pl.BlockSpec((None, D, tn), lambda l, n: (l, 0, n)), 0)],
        [(b_ada.reshape(DEPTH, 1, N_MOD * D), pl.BlockSpec((None, 1, tn), lambda l, n: (l, 0, n)))],
        pre, post,
        grid=(DEPTH, N_MOD * D // tn),
        out_specs=[pl.BlockSpec((None, R, tn), lambda l, n: (l, 0, n))],
        out_shape=[jax.ShapeDtypeStruct((DEPTH, R, N_MOD * D), f32)],
        cache_shapes=[(R, D)],
    )
    return out


def _in_proj(x, g, l, mod, w_in_t):
    T = x.shape[0]
    tm, tn = mod.tm, 1024
    tr = min(tm, 512)

    def norm_body(x_ref, g_ref, shift_ref, scale_ref, h_ref):
        h_ref[...] = (_rms(x_ref[...], g_ref[...]) * (1.0 + scale_ref[...]) + shift_ref[...]).astype(bf16)

    h = pl.pallas_call(
        norm_body,
        grid=(T // tr,),
        in_specs=[pl.BlockSpec((tr, D), lambda m: (m, 0)),
                  pl.BlockSpec((None, 1, D), lambda m: (l, 0, 0)),
                  mod.row_spec(0, tr), mod.row_spec(1, tr)],
        out_specs=pl.BlockSpec((tr, D), lambda m: (m, 0)),
        out_shape=jax.ShapeDtypeStruct((T, D), bf16),
        compiler_params=_cparams(("parallel",)),
    )(x, g.reshape(DEPTH, 1, D), mod.arr, mod.arr)

    def mm_body(h_ref, w_ref, o_ref, wb):
        @pl.when(pl.program_id(1) == 0)
        def _():
            wb[...] = w_ref[...].astype(bf16)

        col = pl.program_id(0) * tn + lax.broadcasted_iota(i32, (tm, tn), 1)
        o_ref[...] = jnp.where(col < IN_W, _dot_nt(h_ref[...], wb[...]), 0.0)

    return pl.pallas_call(
        mm_body,
        grid=(pl.cdiv(PROJ_W, tn), T // tm),
        in_specs=[pl.BlockSpec((tm, D), lambda n, m: (m, 0)),
                  pl.BlockSpec((None, tn, D), lambda n, m: (l, n, 0))],
        out_specs=pl.BlockSpec((tm, tn), lambda n, m: (m, n)),
        out_shape=jax.ShapeDtypeStruct((T, PROJ_W), f32),
        scratch_shapes=[pltpu.VMEM((tn, D), bf16)],
        compiler_params=_cparams(("parallel", "arbitrary")),
    )(h, w_in_t)


def _glu_proj(yg, l, w_glu):
    T = yg.shape[0]
    tm, tn = _row_tile(T), 512
    nb = S5_W // tn

    def post(prods, e_refs):
        return (prods[0] * jax.nn.sigmoid(prods[1]),)

    (out,) = _fused_mm(
        [(yg, pl.BlockSpec((tm, S5_W), lambda m, n: (m, 0)))],
        [(w_glu, pl.BlockSpec((None, S5_W, tn), lambda m, n: (l, 0, n)), 0),
         (w_glu, pl.BlockSpec((None, S5_W, tn), lambda m, n: (l, 0, nb + n)), 0)],
        [], None, post,
        grid=(T // tm, nb),
        out_specs=[pl.BlockSpec((tm, tn), lambda m, n: (m, n))],
        out_shape=[jax.ShapeDtypeStruct((T, S5_W), bf16)],
        cache_shapes=[],
    )
    return out


def _merge_proj(y_ret, y_rw, y_s5, l, w_ret, w_rw, w_s5, proj):
    T = y_ret.shape[0]
    tm, tn = _row_tile(T), 512
    lead = OFF_GATE % LANE
    base = OFF_GATE - lead

    def gate_spec(i):
        return pl.BlockSpec((pl.Element(tm), pl.Element(tn + LANE)),
                            lambda m, n: (pl.multiple_of(m * tm, tm), pl.multiple_of(base + i * D + n * tn, LANE)))

    def post(prods, e_refs):
        acc = None
        for p, e in zip(prods, e_refs):
            t = jax.nn.sigmoid(e[:, lead:lead + tn]) * p
            acc = t if acc is None else acc + t
        return (acc,)

    xspec = pl.BlockSpec((tm, RET_W), lambda m, n: (m, 0))
    wspec = pl.BlockSpec((None, RET_W, tn), lambda m, n: (l, 0, n))
    (out,) = _fused_mm(
        [(y_ret, xspec), (y_rw, xspec), (y_s5, xspec)],
        [(w_ret, wspec, 0), (w_rw, wspec, 1), (w_s5, wspec, 2)],
        [(proj, gate_spec(0)), (proj, gate_spec(1)), (proj, gate_spec(2))],
        None, post,
        grid=(T // tm, D // tn),
        out_specs=[pl.BlockSpec((tm, tn), lambda m, n: (m, n))],
        out_shape=[jax.ShapeDtypeStruct((T, D), bf16)],
        cache_shapes=[],
    )
    return out


def _out_proj(merged, l, w_out, x, mod):
    T = x.shape[0]
    tm, tn = mod.tm, 512

    def post(prods, e_refs):
        x_ref, m_ref = e_refs
        return (x_ref[...] + m_ref[...] * prods[0],)

    (out,) = _fused_mm(
        [(merged, pl.BlockSpec((tm, D), lambda m, n: (m, 0)))],
        [(w_out, pl.BlockSpec((None, D, tn), lambda m, n: (l, 0, n)), 0)],
        [(x, pl.BlockSpec((tm, tn), lambda m, n: (m, n))), (mod.arr, mod.spec(2, tn, lambda n: n))],
        None, post,
        grid=(T // tm, D // tn),
        out_specs=[pl.BlockSpec((tm, tn), lambda m, n: (m, n))],
        out_shape=[jax.ShapeDtypeStruct((T, D), f32)],
        cache_shapes=[],
    )
    return out


def _ret_consts(L, pos0):
    C = RET_CHUNK if L % RET_CHUNK == 0 else L
    H = RET_H
    log_g = jnp.log1p(-jnp.exp2(-5.0 - jnp.arange(H, dtype=f32)))
    i = jnp.arange(C, dtype=f32)
    diff = i[:, None] - i[None, :]
    causal = diff >= 0
    dmask = jnp.where(causal, jnp.exp(jnp.where(causal, diff, 0.0)[None] * log_g[:, None, None]), 0.0)
    kdec = jnp.exp((C - 1.0 - i)[:, None] * log_g[None, :])
    qdec = jnp.exp((i + 1.0)[:, None] * log_g[None, :])
    g_chunk = jnp.exp(C * log_g)
    half = RET_DK // 2
    inv = ROPE_BASE ** (-jnp.arange(half, dtype=f32) / half)
    pos = pos0 + jnp.arange(L, dtype=f32)
    ang = pos[:, None] * inv[None, :]
    return C, dmask, kdec, qdec, g_chunk, jnp.cos(ang), jnp.sin(ang)


def _rotary(x, cos, sin):
    half = RET_DK // 2
    x1, x2 = x[..., :half], x[..., half:]
    return jnp.concatenate([x1 * cos - x2 * sin, x1 * sin + x2 * cos], axis=-1)


def _retention_seq(proj, B, L):
    C, dmask, kdec, qdec, g_chunk, cos, sin = _ret_consts(L, 0.0)
    H, dk = RET_H, RET_DK
    n = L // C
    kdec_f = jnp.broadcast_to(kdec.T[:, :, None], (H, C, dk))
    qdec_f = jnp.broadcast_to(qdec.T[:, :, None], (H, C, dk))
    gch_f = jnp.broadcast_to(g_chunk[:, None, None], (H, 8, dk))
    proj3 = proj.reshape(B, L, PROJ_W)

    def body(q_ref, k_ref, v_ref, g_ref, cos_ref, sin_ref, dm_ref, kd_ref, qd_ref, gc_ref, y_ref, s_ref, st):
        c = pl.program_id(1)

        @pl.when(c == 0)
        def _():
            st[...] = jnp.zeros_like(st)

        cs, sn = cos_ref[...], sin_ref[...]
        hs = range(H)
        sl = [slice(h * dk, (h + 1) * dk) for h in hs]
        q = [_rotary(q_ref[:, sl[h]], cs, sn) for h in hs]
        k = [_rotary(k_ref[:, sl[h]], cs, sn) * (dk ** -0.5) for h in hs]
        vb = [v_ref[:, sl[h]].astype(bf16) for h in hs]
        s0 = [st[h] for h in hs]
        scores = [_dot_nt(q[h].astype(bf16), k[h].astype(bf16)) * dm_ref[h] for h in hs]
        cross = [_dot((q[h] * qd_ref[h]).astype(bf16), s0[h].astype(bf16)) for h in hs]
        kv = [_dot_tn((k[h] * kd_ref[h]).astype(bf16), vb[h]) for h in hs]
        o = [_dot(scores[h].astype(bf16), vb[h]) + cross[h] for h in hs]
        for h in hs:
            st[h] = s0[h] * gc_ref[h, 0:1, :] + kv[h]
            g = g_ref[:, sl[h]]
            y_ref[:, sl[h]] = (g * jax.nn.sigmoid(g) * _head_norm(o[h], RET_GN_EPS)).astype(bf16)

        @pl.when(c == n - 1)
        def _():
            s_ref[...] = st[...]

    def seg(off):
        return pl.BlockSpec((None, C, RET_W), lambda b, c: (b, c, off // RET_W))

    const3 = lambda shp: pl.BlockSpec(shp, lambda b, c: (0, 0, 0))
    y, s = pl.pallas_call(
        body,
        grid=(B, n),
        in_specs=[seg(OFF_Q), seg(OFF_K), seg(OFF_V), seg(OFF_G),
                  pl.BlockSpec((C, dk // 2), lambda b, c: (c, 0)),
                  pl.BlockSpec((C, dk // 2), lambda b, c: (c, 0)),
                  const3((H, C, C)), const3((H, C, dk)), const3((H, C, dk)), const3((H, 8, dk))],
        out_specs=[pl.BlockSpec((None, C, RET_W), lambda b, c: (b, c, 0)),
                   pl.BlockSpec((None, H, dk, RET_DV), lambda b, c: (b, 0, 0, 0))],
        out_shape=[jax.ShapeDtypeStruct((B, L, RET_W), bf16),
                   jax.ShapeDtypeStruct((B, H, dk, RET_DV), f32)],
        scratch_shapes=[pltpu.VMEM((H, dk, RET_DV), f32)],
        compiler_params=_cparams(("parallel", "arbitrary")),
    )(proj3, proj3, proj3, proj3, cos, sin, dmask, kdec_f, qdec_f, gch_f)
    return y.reshape(B * L, RET_W), s


STEP_TB = 16


def _layer_grid(l, buf, inner):
    if buf is not None:
        return inner, (lambda fn: (lambda *ix: fn(l, ix, ix)))
    assert l == 0
    last = tuple(n - 1 for n in inner)

    def wrap(fn):
        def index_map(d, *ix):
            parked = tuple(jnp.where(d == l, i, z) for i, z in zip(ix, last))
            return fn(d, ix, parked)
        return index_map
    return (DEPTH,) + inner, wrap


def _retention_step(proj, s_all, l, buf, pos0):
    B = proj.shape[0]
    _, _, _, _, g_chunk, cos, sin = _ret_consts(1, pos0)
    H, dk = RET_H, RET_DK
    gch = jnp.broadcast_to(g_chunk[:, None, None], (H, 8, dk))
    tb = STEP_TB
    grid, wrap = _layer_grid(l, buf, (B // tb, H))

    def body(q_ref, k_ref, v_ref, g_ref, cos_ref, sin_ref, gc_ref, s_ref, *rest):
        y_ref, so_ref = rest[-2:]

        def update():
            cs, sn = cos_ref[...], sin_ref[...]
            q = _rotary(q_ref[...], cs, sn)
            k = _rotary(k_ref[...], cs, sn) * (dk ** -0.5)
            v = v_ref[...]
            s1 = s_ref[...] * gc_ref[0:1, :][None] + k[:, :, None] * v[:, None, :]
            so_ref[...] = s1
            o = jnp.sum(q[:, :, None] * s1, axis=1)
            g = g_ref[...]
            y_ref[...] = (g * jax.nn.sigmoid(g) * _head_norm(o, RET_GN_EPS)).astype(bf16)

        if buf is not None:
            update()
        else:
            pl.when(pl.program_id(0) == l)(update)

            @pl.when(pl.program_id(0) != l)
            def _():
                so_ref[...] = jnp.zeros_like(so_ref)

    def seg(off):
        return pl.BlockSpec((tb, dk), wrap(lambda d, ix, pk: (pk[0], off // dk + pk[1])))

    const = lambda d, ix, pk: (0, 0)
    in_specs = [seg(OFF_Q), seg(OFF_K), seg(OFF_V), seg(OFF_G),
                pl.BlockSpec((1, dk // 2), wrap(const)), pl.BlockSpec((1, dk // 2), wrap(const)),
                pl.BlockSpec((None, 8, dk), wrap(lambda d, ix, pk: (pk[1], 0, 0))),
                pl.BlockSpec((None, tb, None, dk, RET_DV), wrap(lambda d, ix, pk: (l, pk[0], pk[1], 0, 0)))]
    args = [proj, proj, proj, proj, cos, sin, gch, s_all]
    aliases = {}
    if buf is not None:
        in_specs.append(pl.BlockSpec(memory_space=pl.ANY))
        args.append(buf)
        aliases = {len(args) - 1: 1}
    return pl.pallas_call(
        body,
        grid=grid,
        in_specs=in_specs,
        out_specs=[pl.BlockSpec((tb, dk), wrap(lambda d, ix, pk: pk)),
                   pl.BlockSpec((None, tb, None, dk, RET_DV), wrap(lambda d, ix, pk: (d, ix[0], ix[1], 0, 0)))],
        out_shape=[jax.ShapeDtypeStruct((B, RET_W), bf16),
                   jax.ShapeDtypeStruct((DEPTH, B, H, dk, RET_DV), f32)],
        input_output_aliases=aliases,
        compiler_params=_cparams(("arbitrary",) * len(grid)),
    )(*args)


RW_C = 64
RW_Q = 4
RW_NQ = RWKV_H // RW_Q
RW_GROUP = 8
RW_BLK =((RWKV_PROJ + LANE - 1) // LANE) * LANE
_RW_PKEYS = ('mu', 'w0', 'w2', 'a0', 'a2', 'g2', 'k_k', 'k_a', 'r_k', 'ln_w', 'ln_b')


def _rwkv_params(p):
    return dict(
        mu=jnp.pad(p['rwkv_mu'], (0, RW_BLK - RWKV_PROJ)).reshape(1, RW_BLK),
        w0=p['rwkv_w0'].reshape(1, RWKV_W), w2=p['rwkv_w2'],
        a0=p['rwkv_a0'].reshape(1, RWKV_W), a2=p['rwkv_a2'], g2=p['rwkv_g2'],
        k_k=p['rwkv_k_k'].reshape(1, RWKV_W), k_a=p['rwkv_k_a'].reshape(1, RWKV_W),
        r_k=p['rwkv_r_k'].reshape(1, RWKV_W),
        ln_w=p['rwkv_ln_w'].reshape(1, RWKV_W), ln_b=p['rwkv_ln_b'].reshape(1, RWKV_W))


def _split_bf16(x, terms):
    out = []
    for _ in range(terms - 1):
        hi = x.astype(bf16)
        out.append(hi)
        x = x - hi.astype(f32)
    out.append(x.astype(bf16))
    return out


def _head_sum(x):
    QW = RW_Q * RWKV_N
    r = lax.broadcasted_iota(i32, (QW, QW), 0) // RWKV_N
    c = lax.broadcasted_iota(i32, (QW, QW), 1) // RWKV_N
    ones = (r == c).astype(bf16)
    parts = _split_bf16(x, 2)
    outs = []
    for q in range(RW_NQ):
        sl = slice(q * QW, (q + 1) * QW)
        outs.append(_dot(parts[0][:, sl], ones) + _dot(parts[1][:, sl], ones))
    return jnp.concatenate(outs, axis=1)


def _rwkv_mix(rw, prev, pr):
    m = rw + (prev - rw) * pr['mu'][...]
    r = m[:, 0:RWKV_W]
    k = m[:, RWKV_W:2 * RWKV_W]
    v = m[:, 2 * RWKV_W:3 * RWKV_W]
    o = 3 * RWKV_W
    xw = m[:, o:o + DECAY_LORA]
    xa = m[:, o + DECAY_LORA:o + DECAY_LORA + AAA_LORA]
    xg = m[:, o + DECAY_LORA + AAA_LORA:o + DECAY_LORA + AAA_LORA + GATE_LORA]
    w_log = -jax.nn.softplus(-(pr['w0'][...] + _dot(jnp.tanh(xw).astype(bf16), pr['w2'][...].astype(bf16)))) - 0.5
    lw = -jnp.exp(w_log)
    a = jax.nn.sigmoid(pr['a0'][...] + _dot(xa.astype(bf16), pr['a2'][...].astype(bf16)))
    g = _dot(jax.nn.sigmoid(xg).astype(bf16), pr['g2'][...].astype(bf16))
    kk = k * pr['k_k'][...]
    kk = kk / jnp.maximum(jnp.sqrt(_head_sum(kk * kk)), 1e-12)
    kf = k * (1.0 + (a - 1.0) * pr['k_a'][...])
    return r, lw, kf, v, kk, a, g


def _rwkv_out(y, r, kf, v, g, pr):
    yc = y - _head_sum(y) * (1.0 / RWKV_N)
    yn = yc * lax.rsqrt(_head_sum(yc * yc) * (1.0 / RWKV_N) + RWKV_LN_EPS)
    yn = yn * pr['ln_w'][...] + pr['ln_b'][...]
    yn = yn + _head_sum(r * kf * pr['r_k'][...]) * v
    return yn * g


def _rwkv_seq(proj, rp, B, L):
    C = RW_C
    assert L % C == 0 and C == RWKV_N
    n = L // C
    QW = RW_Q * RWKV_N
    BC = B * C
    proj3 = proj.reshape(B, L, PROJ_W)

    def body(rw_ref, *refs):
        pr = dict(zip(_RW_PKEYS, refs[:len(_RW_PKEYS)]))
        y_ref, s_ref, sh_ref, st, carry = refs[len(_RW_PKEYS):]
        c = pl.program_id(0)

        @pl.when(c == 0)
        def _():
            st[...] = jnp.zeros_like(st)
            carry[...] = jnp.zeros_like(carry)

        rw = rw_ref[...].reshape(BC, RW_BLK)
        rolled = pltpu.roll(rw, 1, 0)
        row = lax.broadcasted_iota(i32, (C, RW_BLK), 0)
        prev = jnp.concatenate(
            [jnp.where(row == 0, carry[b, 0:1, :], rolled[b * C:(b + 1) * C]) for b in range(B)], axis=0)
        for b in range(B):
            carry[b, 0:1, :] = rw[(b + 1) * C - 1:(b + 1) * C, :]
        r, lw, kf, v, kk, a, g = _rwkv_mix(rw, prev, pr)

        ti = lax.broadcasted_iota(i32, (BC, BC), 0)
        si = lax.broadcasted_iota(i32, (BC, BC), 1)
        tril = ((si <= ti) & ((si // C) == (ti // C))).astype(bf16)
        lg = sum(_dot(tril, part) for part in _split_bf16(lw, 3))
        lgc = jnp.concatenate(
            [jnp.broadcast_to(lg[(b + 1) * C - 1:(b + 1) * C, :], (C, RWKV_W)) for b in range(B)], axis=0)
        e_neg = jnp.exp(-lg)
        e_rem = jnp.exp(lgc - lg)
        at = kk * jnp.exp(lg - lw)
        ka = kk * a
        bt = ka * e_neg
        kt = kf * e_neg
        rt = r * jnp.exp(lg)
        bh = ka * e_rem
        kh = kf * e_rem
        gcr = jnp.exp(lgc)

        rr = lax.broadcasted_iota(i32, (RW_Q * C, QW), 0)
        ll = lax.broadcasted_iota(i32, (RW_Q * C, QW), 1)
        blockmask = (rr // C) == (ll // RWKV_N)
        tt = lax.broadcasted_iota(i32, (C, QW), 0)
        ss = lax.broadcasted_iota(i32, (C, QW), 1) % C
        strict = ss < tt
        incl = ss <= tt
        eye = (ss == tt).astype(f32)

        def bd(x):
            return jnp.where(blockmask, jnp.concatenate([x] * RW_Q, axis=0), 0.0).astype(bf16)

        ys = [[None] * RW_NQ for _ in range(B)]
        chains = [(b, q) for b in range(B) for q in range(RW_NQ)]
        for g0 in range(0, len(chains), RW_GROUP):
            grp = chains[g0:g0 + RW_GROUP]
            idx = [(slice(b * C, (b + 1) * C), slice(q * QW, (q + 1) * QW)) for b, q in grp]
            each = lambda fn: [fn(i) for i in range(len(grp))]
            vq = each(lambda i: v[idx[i]])
            ar = each(lambda i: jnp.concatenate([at[idx[i]], rt[idx[i]]], axis=0).astype(bf16))
            big = each(lambda i: _dot_nt(ar[i], jnp.concatenate([bd(bt[idx[i]]), bd(kt[idx[i]])], axis=0)))
            s0 = each(lambda i: st[grp[i]])
            asrs = each(lambda i: _dot_nt(ar[i], s0[i].astype(bf16)))
            nmat = each(lambda i: jnp.where(strict, big[i][:C, :QW], 0.0))
            akm = each(lambda i: jnp.where(strict, big[i][:C, QW:], 0.0).astype(bf16))
            rbk = each(lambda i: jnp.concatenate([jnp.where(incl, big[i][C:, :QW], 0.0),
                                                  jnp.where(incl, big[i][C:, QW:], 0.0)], axis=1).astype(bf16))
            tm = each(lambda i: eye - nmat[i])
            pw = each(lambda i: _dot(nmat[i].astype(bf16), bd(nmat[i])))
            lvl = 2
            while lvl < C:
                res = each(lambda i: _dot(jnp.concatenate([tm[i], pw[i]], axis=0).astype(bf16), bd(pw[i])))
                tm = each(lambda i: tm[i] + res[i][:C])
                pw = each(lambda i: res[i][C:])
                lvl *= 2
            vbd = each(lambda i: bd(vq[i]))
            rhs = each(lambda i: -(asrs[i][:C] + _dot(akm[i], vbd[i])))
            u = each(lambda i: _dot(tm[i].astype(bf16), bd(rhs[i])))
            y = each(lambda i: asrs[i][C:] + _dot(rbk[i], jnp.concatenate([bd(u[i]), vbd[i]], axis=0)))
            upd = each(lambda i: _dot_tn(jnp.concatenate([u[i], vq[i]], axis=0).astype(bf16),
                                         jnp.concatenate([bh[idx[i]], kh[idx[i]]], axis=0).astype(bf16)))
            for i, (b, q) in enumerate(grp):
                st[b, q] = s0[i] * gcr[b * C:b * C + 1, idx[i][1]] + jnp.where(blockmask, upd[i], 0.0)
                ys[b][q] = y[i]

        y = jnp.concatenate([jnp.concatenate(yb, axis=1) for yb in ys], axis=0)
        y_ref[...] = _rwkv_out(y, r, kf, v, g, pr).astype(bf16).reshape(B, C, RWKV_W)

        @pl.when(c == n - 1)
        def _():
            for b in range(B):
                for q in range(RW_NQ):
                    for h in range(RW_Q):
                        hs = slice(h * RWKV_N, (h + 1) * RWKV_N)
                        s_ref[b, q * RW_Q + h] = st[b, q, hs, hs]
                sh_ref[b] = rw[(b + 1) * C - 1:(b + 1) * C, 0:RWKV_PROJ]

    pspecs = [pl.BlockSpec(rp[k].shape, lambda c: (0, 0)) for k in _RW_PKEYS]
    y, s, sh = pl.pallas_call(
        body,
        grid=(n,),
        in_specs=[pl.BlockSpec((pl.Element(B), pl.Element(C), pl.Element(RW_BLK)),
                               lambda c: (0, pl.multiple_of(c * C, C), OFF_RW))] + pspecs,
        out_specs=[pl.BlockSpec((B, C, RWKV_W), lambda c: (0, c, 0)),
                   pl.BlockSpec((B, RWKV_H, RWKV_N, RWKV_N), lambda c: (0, 0, 0, 0)),
                   pl.BlockSpec((B, 1, RWKV_PROJ), lambda c: (0, 0, 0))],
        out_shape=[jax.ShapeDtypeStruct((B, L, RWKV_W), bf16),
                   jax.ShapeDtypeStruct((B, RWKV_H, RWKV_N, RWKV_N), f32),
                   jax.ShapeDtypeStruct((B, 1, RWKV_PROJ), f32)],
        scratch_shapes=[pltpu.VMEM((B, RW_NQ, QW, QW), f32), pltpu.VMEM((B, 8, RW_BLK), f32)],
        compiler_params=_cparams(("arbitrary",)),
    )(proj3, *[rp[k] for k in _RW_PKEYS])
    return y.reshape(B * L, RWKV_W), s, sh.reshape(B, RWKV_PROJ)


def _rwkv_step(proj, shift, s_t, l, buf, rp):
    B = proj.shape[0]
    N = RWKV_N
    shift_p = jnp.pad(shift, ((0, 0), (0, RW_BLK - RWKV_PROJ)))
    grid, wrap = _layer_grid(l, buf, (RWKV_H,))
    npk = len(_RW_PKEYS)
    vec_names = ('r', 'w', 'kf', 'v', 'kk', 'ka')

    def body(rw_ref, sh_ref, s_ref, *refs):
        pr = dict(zip(_RW_PKEYS, refs[:npk]))
        y_ref, so_ref = refs[-9:-7]
        vt = dict(zip(vec_names, refs[-7:-1]))
        yt = refs[-1]
        h = pl.program_id(len(grid) - 1)

        def update():
            @pl.when(h == 0)
            def _():
                r, lw, kf, v, kk, a, _ = _rwkv_mix(rw_ref[...], sh_ref[...], pr)
                for name, val in zip(vec_names, (r, jnp.exp(lw), kf, v, kk, kk * a)):
                    vt[name][...] = val.T

            hs = pl.ds(pl.multiple_of(h * N, N), N)
            s = s_ref[...]
            kk_h = vt['kk'][hs, :]
            sa = jnp.sum(s * (-kk_h)[None], axis=1, keepdims=True)
            s1 = s * vt['w'][hs, :][None] + sa * vt['ka'][hs, :][None] + vt['v'][hs, :][:, None, :] * vt['kf'][hs, :][None]
            so_ref[...] = s1
            yt[hs, :] = jnp.sum(s1 * vt['r'][hs, :][None], axis=1)

            @pl.when(h == RWKV_H - 1)
            def _():
                r, lw, kf, v, kk, a, g = _rwkv_mix(rw_ref[...], sh_ref[...], pr)
                y_ref[...] = _rwkv_out(yt[...].T, r, kf, v, g, pr).astype(bf16)

        if buf is not None:
            update()
        else:
            pl.when(pl.program_id(0) == l)(update)

            @pl.when(pl.program_id(0) != l)
            def _():
                so_ref[...] = jnp.zeros_like(so_ref)

    const = lambda d, ix, pk: (0, 0)
    in_specs = [pl.BlockSpec((pl.Element(B), pl.Element(RW_BLK)), wrap(lambda d, ix, pk: (0, OFF_RW))),
                pl.BlockSpec((B, RW_BLK), wrap(const)),
                pl.BlockSpec((None, None, N, N, B), wrap(lambda d, ix, pk: (l, pk[0], 0, 0, 0)))]
    in_specs += [pl.BlockSpec(rp[k].shape, wrap(const)) for k in _RW_PKEYS]
    args = [proj, shift_p, s_t] + [rp[k] for k in _RW_PKEYS]
    aliases = {}
    if buf is not None:
        in_specs.append(pl.BlockSpec(memory_space=pl.ANY))
        args.append(buf)
        aliases = {len(args) - 1: 1}
    return pl.pallas_call(
        body,
        grid=grid,
        in_specs=in_specs,
        out_specs=[pl.BlockSpec((B, RWKV_W), wrap(const)),
                   pl.BlockSpec((None, None, N, N, B), wrap(lambda d, ix, pk: (d, ix[0], 0, 0, 0)))],
        out_shape=[jax.ShapeDtypeStruct((B, RWKV_W), bf16),
                   jax.ShapeDtypeStruct((DEPTH, RWKV_H, N, N, B), f32)],
        scratch_shapes=[pltpu.VMEM((RWKV_W, B), f32)] * 7,
        input_output_aliases=aliases,
        compiler_params=_cparams(("arbitrary",) * len(grid)),
    )(*args)


S5_N = S5_G * S5_P
S5_KC = 256
S5_NKC = S5_W // S5_KC
S5_TILES = S5_N // LANE
S5_GB = S5_KC // S5_GC


def _s5_params(p):
    a_re, a_im = p['s5_a_re'], p['s5_a_im']
    dstep = jnp.exp(p['s5_log_dt'])[:, None]
    mag = jnp.exp(a_re * dstep)
    ab_re = mag * jnp.cos(a_im * dstep)
    ab_im = mag * jnp.sin(a_im * dstep)
    den = a_re * a_re + a_im * a_im
    n_re = ab_re - 1.0
    f_re = (n_re * a_re + ab_im * a_im) / den
    f_im = (ab_im * a_re - n_re * a_im) / den
    b_re, b_im = p['s5_b_re'], p['s5_b_im']
    bb_re = f_re[..., None] * b_re - f_im[..., None] * b_im
    bb_im = f_re[..., None] * b_im + f_im[..., None] * b_re
    eye = jnp.eye(S5_GB, dtype=f32)

    def in_map(bb):
        t = bb.reshape(S5_NKC, S5_GB, S5_P, S5_GC)
        return jnp.einsum('kgpc,gh->kgchp', t, eye).reshape(S5_NKC, S5_KC, S5_GB * S5_P)

    def out_map(cc):
        t = cc.reshape(S5_NKC, S5_GB, S5_GC, S5_P)
        return jnp.einsum('qgcp,gh->qgphc', t, eye).reshape(S5_NKC, S5_GB * S5_P, S5_KC)

    return dict(wb=jnp.concatenate([in_map(bb_re), in_map(bb_im)], axis=-1),
                wc_re=out_map(p['s5_c_re']), wc_im=out_map(p['s5_c_im']),
                ab_re_t=ab_re.reshape(S5_TILES // 8, 8, LANE), ab_im_t=ab_im.reshape(S5_TILES // 8, 8, LANE),
                ab_re=ab_re.reshape(1, S5_N), ab_im=ab_im.reshape(1, S5_N), d=p['s5_d'].reshape(1, S5_W))


def _s5_seq(proj, sp, B, L):
    Lc = min(L, 256)
    n = L // Lc
    pitch = Lc + 4
    lead = OFF_U % LANE
    base = OFF_U - lead
    width = S5_W + LANE
    nt4 = S5_TILES // 8
    tiles_kc = S5_TILES // S5_NKC

    def body(u_ref, wb_ref, wcr_ref, wci_ref, abr_ref, abi_ref, d_ref, y_ref, sr_ref, si_ref, xr, xi, cr, ci):
        c = pl.program_id(1)

        @pl.when(c == 0)
        def _():
            cr[...] = jnp.zeros_like(cr)
            ci[...] = jnp.zeros_like(ci)

        u = u_ref[:, lead:lead + S5_W]
        ub = u.astype(bf16)
        for kc in range(S5_NKC):
            bu = _dot(ub[:, kc * S5_KC:(kc + 1) * S5_KC], wb_ref[kc].astype(bf16))
            for j in range(tiles_kc):
                t = kc * tiles_kc + j
                xr[pl.ds(t * pitch, Lc), :] = bu[:, j * LANE:(j + 1) * LANE]
                xi[pl.ds(t * pitch, Lc), :] = bu[:, (tiles_kc + j) * LANE:(tiles_kc + j + 1) * LANE]

        abr = [abr_ref[g] for g in range(nt4)]
        abi = [abi_ref[g] for g in range(nt4)]

        def step(t, carry):
            out = []
            for g in range(nt4):
                s_r, s_i = carry[2 * g], carry[2 * g + 1]
                idx = pl.ds(g * 8 * pitch + t, 8, stride=pitch)
                n_r = abr[g] * s_r - abi[g] * s_i + xr[idx, :]
                n_i = abr[g] * s_i + abi[g] * s_r + xi[idx, :]
                xr[idx, :] = n_r
                xi[idx, :] = n_i
                out += [n_r, n_i]
            return tuple(out)

        init = []
        for g in range(nt4):
            init += [cr[g], ci[g]]
        fin = lax.fori_loop(0, Lc, step, tuple(init), unroll=2)
        for g in range(nt4):
            cr[g] = fin[2 * g]
            ci[g] = fin[2 * g + 1]

        for q in range(S5_NKC):
            lr = jnp.concatenate([xr[pl.ds((q * tiles_kc + j) * pitch, Lc), :] for j in range(tiles_kc)], axis=1)
            li = jnp.concatenate([xi[pl.ds((q * tiles_kc + j) * pitch, Lc), :] for j in range(tiles_kc)], axis=1)
            y = _dot(lr.astype(bf16), wcr_ref[q].astype(bf16)) - _dot(li.astype(bf16), wci_ref[q].astype(bf16))
            cs = slice(q * S5_KC, (q + 1) * S5_KC)
            y = y + d_ref[:, cs] * u[:, cs]
            y_ref[:, cs] = jax.nn.gelu(y).astype(bf16)

        @pl.when(c == n - 1)
        def _():
            sr_ref[...] = cr[...]
            si_ref[...] = ci[...]

    full = lambda shp: pl.BlockSpec(shp, lambda b, c: (0,) * len(shp))
    y, sr, si = pl.pallas_call(
        body,
        grid=(B, n),
        in_specs=[pl.BlockSpec((pl.Element(Lc), pl.Element(width)),
                               lambda b, c: (pl.multiple_of((b * n + c) * Lc, Lc), base)),
                  full(sp['wb'].shape), full(sp['wc_re'].shape), full(sp['wc_im'].shape),
                  full(sp['ab_re_t'].shape), full(sp['ab_im_t'].shape), full((1, S5_W))],
        out_specs=[pl.BlockSpec((Lc, S5_W), lambda b, c: (b * n + c, 0)),
                   pl.BlockSpec((None, nt4, 8, LANE), lambda b, c: (b, 0, 0, 0)),
                   pl.BlockSpec((None, nt4, 8, LANE), lambda b, c: (b, 0, 0, 0))],
        out_shape=[jax.ShapeDtypeStruct((B * L, S5_W), bf16),
                   jax.ShapeDtypeStruct((B, nt4, 8, LANE), f32),
                   jax.ShapeDtypeStruct((B, nt4, 8, LANE), f32)],
        scratch_shapes=[pltpu.VMEM((S5_TILES * pitch, LANE), f32), pltpu.VMEM((S5_TILES * pitch, LANE), f32),
                        pltpu.VMEM((nt4, 8, LANE), f32), pltpu.VMEM((nt4, 8, LANE), f32)],
        compiler_params=_cparams(("parallel", "arbitrary")),
    )(proj, sp['wb'], sp['wc_re'], sp['wc_im'], sp['ab_re_t'], sp['ab_im_t'], sp['d'])
    return y, sr.reshape(B, S5_G, S5_P), si.reshape(B, S5_G, S5_P)


def _s5_step(proj, x_re, x_im, sp):
    B = proj.shape[0]
    lead = OFF_U % LANE
    base = OFF_U - lead
    width = S5_W + LANE
    kw = S5_N // S5_NKC

    def body(u_ref, xr_ref, xi_ref, wb_ref, wcr_ref, wci_ref, abr_ref, abi_ref, d_ref, y_ref, sr_ref, si_ref):
        u = u_ref[:, lead:lead + S5_W]
        ub = u.astype(bf16)
        abr, abi = abr_ref[...], abi_ref[...]
        xr, xi = xr_ref[...], xi_ref[...]
        for kc in range(S5_NKC):
            bu = _dot(ub[:, kc * S5_KC:(kc + 1) * S5_KC], wb_ref[kc].astype(bf16))
            sl = slice(kc * kw, (kc + 1) * kw)
            n_r = abr[:, sl] * xr[:, sl] - abi[:, sl] * xi[:, sl] + bu[:, :kw]
            n_i = abr[:, sl] * xi[:, sl] + abi[:, sl] * xr[:, sl] + bu[:, kw:]
            sr_ref[:, sl] = n_r
            si_ref[:, sl] = n_i
            y = _dot(n_r.astype(bf16), wcr_ref[kc].astype(bf16)) - _dot(n_i.astype(bf16), wci_ref[kc].astype(bf16))
            cs = slice(kc * S5_KC, (kc + 1) * S5_KC)
            y = y + d_ref[:, cs] * u[:, cs]
            y_ref[:, cs] = jax.nn.gelu(y).astype(bf16)

    full = lambda shp: pl.BlockSpec(shp, lambda i: (0,) * len(shp))
    y, sr, si = pl.pallas_call(
        body,
        grid=(1,),
        in_specs=[pl.BlockSpec((pl.Element(B), pl.Element(width)), lambda i: (0, base)),
                  full((B, S5_N)), full((B, S5_N)),
                  full(sp['wb'].shape), full(sp['wc_re'].shape), full(sp['wc_im'].shape),
                  full((1, S5_N)), full((1, S5_N)), full((1, S5_W))],
        out_specs=[full((B, S5_W)), full((B, S5_N)), full((B, S5_N))],
        out_shape=[jax.ShapeDtypeStruct((B, S5_W), bf16),
                   jax.ShapeDtypeStruct((B, S5_N), f32), jax.ShapeDtypeStruct((B, S5_N), f32)],
        compiler_params=_cparams(("arbitrary",)),
    )(proj, x_re.reshape(B, S5_N), x_im.reshape(B, S5_N), sp['wb'], sp['wc_re'], sp['wc_im'],
      sp['ab_re'], sp['ab_im'], sp['d'])
    return y, sr.reshape(B, S5_G, S5_P), si.reshape(B, S5_G, S5_P)


MOE_RT, MOE_RT_SMALL = 256, 32
SLAB = D // LANE
SUB = 8
SLAB_PITCH = SLAB + SUB
H2_PITCH = SLAB + SUB
O2_PITCH = TOP_K * SLAB + SUB


def _moe_rt(T):
    return MOE_RT if T * TOP_K >= N_EXPERTS * MOE_RT else MOE_RT_SMALL
ROUTE_W = LANE


def _router(x, g, mod, w_r, b_r, row0, t_all, bufs):
    T = x.shape[0]
    tm = min(mod.tm, 512)
    nm = T // tm
    assert row0 % tm == 0
    if bufs is None:
        assert row0 == 0
        n_steps = pl.cdiv(t_all, tm)
    else:
        n_steps = nm
    clamp = lambda m: jnp.minimum(m, nm - 1)

    def body(x_ref, g_ref, sh_ref, sc_ref, w_ref, b_ref, *rest):
        h_ref, e_ref, p_ref = rest[-3:]
        if n_steps > nm:
            @pl.when(pl.program_id(0) >= nm)
            def _():
                h_ref[...] = jnp.zeros_like(h_ref)
                e_ref[...] = jnp.zeros_like(e_ref)
                p_ref[...] = jnp.zeros_like(p_ref)

            pl.when(pl.program_id(0) < nm)(lambda: route(x_ref, g_ref, sh_ref, sc_ref, w_ref, b_ref, *rest[-3:]))
        else:
            route(x_ref, g_ref, sh_ref, sc_ref, w_ref, b_ref, *rest[-3:])

    def route(x_ref, g_ref, sh_ref, sc_ref, w_ref, b_ref, h_ref, e_ref, p_ref):
        h2 = _rms(x_ref[...], g_ref[...]) * (1.0 + sc_ref[...]) + sh_ref[...]
        for s in range(SLAB):
            h_ref[pl.ds(s, tm, stride=H2_PITCH), :] = h2[:, s * LANE:(s + 1) * LANE]
        for s in range(SLAB, H2_PITCH):
            h_ref[pl.ds(s, tm, stride=H2_PITCH), :] = jnp.zeros((tm, LANE), f32)
        logits = jnp.dot(h2, w_ref[...], precision=lax.Precision.HIGHEST, preferred_element_type=f32) + b_ref[...]
        lane = lax.broadcasted_iota(i32, (tm, ROUTE_W), 1)
        ninf = jnp.float32(-jnp.inf)
        gl = jnp.where(lane < N_GROUPS, logits, ninf)
        gm = jnp.max(gl, axis=-1, keepdims=True)
        g_p = 1.0 / jnp.sum(jnp.exp(gl - gm), axis=-1, keepdims=True)
        g_idx = jnp.min(jnp.where(gl == gm, lane, ROUTE_W), axis=-1, keepdims=True)
        valid = (lane >= N_GROUPS) & (lane < N_GROUPS + N_EXPERTS) & (((lane - N_GROUPS) // EPG) == g_idx)
        el = jnp.where(valid, logits, ninf)
        ee = jnp.exp(el - jnp.max(el, axis=-1, keepdims=True))
        prob = jnp.where(valid, ee / jnp.sum(ee, axis=-1, keepdims=True), -1.0)
        p1 = jnp.max(prob, axis=-1, keepdims=True)
        i1 = jnp.min(jnp.where(prob == p1, lane, ROUTE_W), axis=-1, keepdims=True)
        prob2 = jnp.where(lane == i1, -1.0, prob)
        p2 = jnp.max(prob2, axis=-1, keepdims=True)
        i2 = jnp.min(jnp.where(prob2 == p2, lane, ROUTE_W), axis=-1, keepdims=True)
        den = p1 + p2
        e_ref[...] = jnp.where(lane == 0, i1 - N_GROUPS, jnp.where(lane == 1, i2 - N_GROUPS, 0))
        p_ref[...] = jnp.where(lane == 0, g_p * p1 / den, jnp.where(lane == 1, g_p * p2 / den, 0.0))

    in_specs = [pl.BlockSpec((tm, D), lambda m: (clamp(m), 0)),
                pl.BlockSpec((1, D), lambda m: (0, 0)),
                mod.row_spec(3, tm, clamp), mod.row_spec(4, tm, clamp),
                pl.BlockSpec((D, ROUTE_W), lambda m: (0, 0)),
                pl.BlockSpec((1, ROUTE_W), lambda m: (0, 0))]
    args = [x, g.reshape(1, D), mod.arr, mod.arr, w_r, b_r]
    aliases = {}
    if bufs is not None:
        in_specs += [pl.BlockSpec(memory_space=pl.ANY)] * 3
        aliases = {len(args) + i: i for i in range(3)}
        args += list(bufs)
    b0 = row0 // tm
    return pl.pallas_call(
        body,
        grid=(n_steps,),
        in_specs=in_specs,
        out_specs=[pl.BlockSpec((tm * H2_PITCH, LANE), lambda m: (b0 + m, 0)),
                   pl.BlockSpec((tm, ROUTE_W), lambda m: (b0 + m, 0)),
                   pl.BlockSpec((tm, ROUTE_W), lambda m: (b0 + m, 0))],
        out_shape=[jax.ShapeDtypeStruct((t_all * H2_PITCH, LANE), f32), jax.ShapeDtypeStruct((t_all, ROUTE_W), i32),
                   jax.ShapeDtypeStruct((t_all, ROUTE_W), f32)],
        input_output_aliases=aliases,
        compiler_params=_cparams(("arbitrary",)),
    )(*args)


TOK_BITS = 14


def _moe_plan(eid, T):
    A = T * TOP_K
    RT = _moe_rt(T)
    nt = pl.cdiv(A, RT) + N_EXPERTS
    flat_e = eid[:, :TOP_K].reshape(-1)
    order = jnp.argsort(flat_e).astype(i32)
    counts = jnp.sum((flat_e[:, None] == jnp.arange(N_EXPERTS, dtype=i32)[None, :]).astype(i32), axis=0)
    start = jnp.cumsum(counts) - counts
    pcnt = (counts + RT - 1) // RT * RT
    pend = jnp.cumsum(pcnt)
    tile_e = jnp.minimum(jnp.sum((pend[None, :] <= (jnp.arange(nt, dtype=i32) * RT)[:, None]).astype(i32), axis=1),
                         N_EXPERTS - 1)
    off = (jnp.arange(nt, dtype=i32) * RT - (pend - pcnt)[tile_e])[:, None] + jnp.arange(RT, dtype=i32)[None, :]
    real = off < counts[tile_e][:, None]
    srt = start[tile_e][:, None] + jnp.minimum(off, counts[tile_e][:, None])
    a = order[jnp.minimum(srt, A - 1)]
    slot = jnp.arange(nt * RT, dtype=i32).reshape(nt, RT)
    tok = jnp.where(real, a // TOP_K, 0)
    dst = jnp.where(real, a, A + slot - srt)
    n_used = (pend[-1] // RT).astype(i32).reshape(1)
    eidx = jnp.arange(N_EXPERTS, dtype=i32)
    in_use = counts > 0
    rank = jnp.cumsum(in_use.astype(i32)) - 1
    after = lax.cummin(jnp.where(in_use, eidx, N_EXPERTS), reverse=True)
    nxt = jnp.concatenate([after[1:], jnp.full((1,), N_EXPERTS, i32)])
    nxt = jnp.where(nxt < N_EXPERTS, nxt, -1)
    tile_info = jnp.concatenate([tile_e, (rank % 2)[tile_e], nxt[tile_e]]).astype(i32)
    return (tok | (dst << TOK_BITS)).reshape(-1), tile_info, n_used


def _experts(h2, packed, tile_e, n_used, l, w1, w3, w2):
    T = h2.shape[0] // H2_PITCH
    assert TOP_K == 2 and T < (1 << TOK_BITS)
    RT = _moe_rt(T)
    PITCH = SLAB_PITCH
    nt = tile_e.shape[0] // 3
    prime_id = nt * RT
    out_rows = (prime_id + RT) // TOP_K
    RING = 3
    HC, OC = 128, 256
    n_hc, n_oc = D_EXPERT // HC, D // OC
    g_per, s_per = RT // n_hc, RT // n_oc

    def body(tile_ref, nused_ref, slot_ref, h_hbm, w1_hbm, w3_hbm, w2_hbm, o_hbm,
             x0, x1, x2, o0, o1, o2b, w1b, w3b, w2b, wf1, wf3, wf2, gsem, ssem, wsem):
        j = pl.program_id(0)
        n_used = nused_ref[0]
        active = j < n_used
        xs, os_ = (x0, x1, x2), (o0, o1, o2b)

        def gather_row(b, r, tok):
            src = pl.multiple_of(tok * H2_PITCH, SUB)
            return pltpu.make_async_copy(h_hbm.at[pl.ds(src, SLAB), :], xs[b].at[pl.ds(r * PITCH, SLAB), :],
                                         gsem.at[b])

        def scatter_row(b, r, d):
            dst = pl.multiple_of(lax.shift_right_logical(d, 1) * O2_PITCH + (d & (TOP_K - 1)) * SLAB, SUB)
            return pltpu.make_async_copy(os_[b].at[pl.ds(r * PITCH, SLAB), :], o_hbm.at[pl.ds(dst, SLAB), :],
                                         ssem.at[b])

        def wait_rows(sem):
            pltpu.make_async_copy(h_hbm.at[pl.ds(0, RT * SLAB), :], x0.at[pl.ds(0, RT * SLAB), :], sem).wait()

        tok_of = lambda s: s & ((1 << TOK_BITS) - 1)
        dst_of = lambda s: lax.shift_right_logical(s, TOK_BITS)
        last_tile = n_used - 1

        @pl.when(j == 0)
        def _():
            o2b[...] = jnp.zeros_like(o2b)
            for b in range(RING - 1):
                base = jnp.minimum(b, last_tile) * RT

                def one(r, _, b=b, base=base):
                    gather_row(b, r, tok_of(slot_ref[base + r])).start()
                    return 0
                lax.fori_loop(0, RT, one, 0)

        def fetch(e, s):
            return [pltpu.make_async_copy(w_hbm.at[l, e], wf.at[s], wsem.at[s])
                    for w_hbm, wf in ((w1_hbm, wf1), (w3_hbm, wf3), (w2_hbm, wf2))]

        @pl.when(j == 0)
        def _():
            for c in fetch(tile_ref[0], 0):
                c.start()

        @pl.when(active & ((j == 0) | (tile_ref[j] != tile_ref[jnp.maximum(j - 1, 0)])))
        def _():
            ws, e_nxt = tile_ref[nt + j], tile_ref[2 * nt + j]
            for c in fetch(tile_ref[j], ws):
                c.wait()
            w1b[...] = wf1[ws].astype(bf16)
            w3b[...] = wf3[ws].astype(bf16)
            w2b[...] = wf2[ws].astype(bf16)

            @pl.when(e_nxt >= 0)
            def _():
                for c in fetch(e_nxt, 1 - ws):
                    c.start()

        def tile(cur):
            nxt_b = prv_b = (cur + RING - 1) % RING

            @pl.when(j >= RING - 1)
            def _():
                wait_rows(ssem.at[cur])

            wait_rows(gsem.at[cur])
            xb = jnp.concatenate([xs[cur][pl.ds(s, RT, stride=PITCH), :] for s in range(SLAB)],
                                 axis=1).astype(bf16)
            nxt = jnp.minimum(j + RING - 1, last_tile) * RT
            prv = jnp.maximum(j - 1, 0) * RT
            parts = []
            for c in range(n_hc):
                h1 = _dot(xb, w1b[:, c * HC:(c + 1) * HC])
                h3 = _dot(xb, w3b[:, c * HC:(c + 1) * HC])
                parts.append(((h1 * jax.nn.sigmoid(h1)) * h3).astype(bf16))
                for r in range(c * g_per, (c + 1) * g_per):
                    gather_row(nxt_b, r, tok_of(slot_ref[nxt + r])).start(priority=r % 2)
            hm = jnp.concatenate(parts, axis=1)
            for c in range(n_oc):
                res = _dot(hm, w2b[:, c * OC:(c + 1) * OC])
                for t in range(OC // LANE):
                    os_[cur][pl.ds(c * (OC // LANE) + t, RT, stride=PITCH), :] = res[:, t * LANE:(t + 1) * LANE]
                for r in range(c * s_per, (c + 1) * s_per):
                    d = jnp.where(j == 0, prime_id + r, dst_of(slot_ref[prv + r]))
                    scatter_row(prv_b, r, d).start(priority=r % 2)

        def last(cur):
            def one(r, _):
                scatter_row(cur, r, dst_of(slot_ref[j * RT + r])).start()
                return 0
            lax.fori_loop(0, RT, one, 0)
            wait_rows(ssem.at[cur])
            for back in range(1, RING):
                pl.when(j >= back - 1)(lambda b=(cur - back) % RING: wait_rows(ssem.at[b]))
            for ahead in range(1, RING):
                wait_rows(gsem.at[(cur + ahead) % RING])

        for cur in range(RING):
            pl.when(active & (j % RING == cur))(lambda cur=cur: tile(cur))
        for cur in range(RING):
            pl.when((j == last_tile) & (j % RING == cur))(lambda cur=cur: last(cur))

    grid_spec = pltpu.PrefetchScalarGridSpec(
        num_scalar_prefetch=3,
        grid=(nt,),
        in_specs=[pl.BlockSpec(memory_space=pl.ANY)] * 4,
        out_specs=pl.BlockSpec(memory_space=pl.ANY),
        scratch_shapes=[pltpu.VMEM((RT * PITCH, LANE), f32)] * (2 * RING) + [
                        pltpu.VMEM((D, D_EXPERT), bf16), pltpu.VMEM((D, D_EXPERT), bf16),
                        pltpu.VMEM((D_EXPERT, D), bf16),
                        pltpu.VMEM((2, D, D_EXPERT), f32), pltpu.VMEM((2, D, D_EXPERT), f32),
                        pltpu.VMEM((2, D_EXPERT, D), f32),
                        pltpu.SemaphoreType.DMA((RING,)), pltpu.SemaphoreType.DMA((RING,)),
                        pltpu.SemaphoreType.DMA((2,))],
    )
    return pl.pallas_call(
        body,
        grid_spec=grid_spec,
        out_shape=jax.ShapeDtypeStruct((out_rows * O2_PITCH, LANE), f32),
        compiler_params=_cparams(("arbitrary",)),
    )(tile_e, n_used, packed, h2, w1, w3, w2)


def _moe_combine(x, o2, wgt, mod, norm_final, row0):
    T = x.shape[0]
    tm = min(mod.tm, 512)
    b0 = row0 // tm

    def body(x_ref, o_ref, p_ref, m_ref, *rest):
        p = p_ref[...]
        row = lambda k: jnp.concatenate(
            [o_ref[pl.ds(k * SLAB + s, tm, stride=O2_PITCH), :] for s in range(SLAB)], axis=1)
        y = p[:, 0:1] * row(0)
        for k in range(1, TOP_K):
            y = y + p[:, k:k + 1] * row(k)
        xo = x_ref[...] + m_ref[...] * y
        if norm_final is None:
            rest[0][...] = xo
        else:
            rest[1][...] = _rms(xo, rest[0][...])

    ins = [x, o2, wgt, mod.arr]
    specs = [pl.BlockSpec((tm, D), lambda m: (m, 0)),
             pl.BlockSpec((tm * O2_PITCH, LANE), lambda m: (b0 + m, 0)),
             pl.BlockSpec((tm, ROUTE_W), lambda m: (b0 + m, 0)), mod.row_spec(5, tm)]
    if norm_final is not None:
        ins.append(norm_final.reshape(1, D))
        specs.append(pl.BlockSpec((1, D), lambda m: (0, 0)))
    return pl.pallas_call(
        body,
        grid=(T // tm,),
        in_specs=specs,
        out_specs=pl.BlockSpec((tm, D), lambda m: (m, 0)),
        out_shape=jax.ShapeDtypeStruct((T, D), f32),
        compiler_params=_cparams(("parallel",)),
    )(*ins)


def _moe(groups, p, l, big, norm_final):
    pad = ROUTE_W - N_GROUPS - N_EXPERTS
    w_r = jnp.pad(jnp.concatenate([p['moe_w_group'], p['moe_w_router']], axis=1), ((0, 0), (0, pad)))
    b_r = jnp.pad(jnp.concatenate([p['moe_b_group'], p['moe_b_router']]), (0, pad)).reshape(1, ROUTE_W)
    t_all = sum(g.T for g in groups)
    bufs, row0 = None, 0
    for g in groups:
        g.row0 = row0
        bufs = _router(g.x, p['norm_ffn'], g.mod, w_r, b_r, row0, t_all, bufs)
        row0 += g.T
    h2, eid, wgt = bufs
    packed, tile_e, n_used = _moe_plan(eid, t_all)
    o2 = _experts(h2, packed, tile_e, n_used, l, big['moe_w1'], big['moe_w3'], big['moe_w2'])
    for g in groups:
        g.x = _moe_combine(g.x, o2, wgt, g.mod, norm_final, g.row0)


class _Group:
    def __init__(self, x, mods, states, pos0):
        self.B, self.L, _ = x.shape
        self.T = self.B * self.L
        self.tm = _row_tile(self.T)
        self.x = x.reshape(self.T, D)
        self.mods, self.states, self.pos0 = mods, states, pos0
        self.outs = ([], [], [], [], [])
        self.ret_all = self.rw_all = None

    def mix(self, l, p, rp, sp, layers, big, w_in_t):
        B, L = self.B, self.L
        self.mod = mod = _Mod(self.mods[l], L, self.tm)
        proj = _in_proj(self.x, layers['norm_mix'], l, mod, w_in_t)
        if self.states is None:
            y_ret, s_ret = _retention_seq(proj, B, L)
            y_rw, s_rw, shift = _rwkv_seq(proj, rp, B, L)
            y_s5, s_re, s_im = _s5_seq(proj, sp, B, L)
        else:
            st_ret, st_rw, st_shift, st_re, st_im = self.states
            y_ret, self.ret_all = _retention_step(proj, st_ret, l, self.ret_all, self.pos0)
            y_rw, self.rw_all = _rwkv_step(proj, st_shift[l], jnp.transpose(st_rw, (0, 2, 3, 4, 1)), l,
                                           self.rw_all, rp)
            s_ret = s_rw = None
            shift = proj[:, OFF_RW:OFF_RW + RWKV_PROJ]
            y_s5, s_re, s_im = _s5_step(proj, st_re[l], st_im[l], sp)
        z = _glu_proj(y_s5, l, big['s5_w_glu'])
        merged = _merge_proj(y_ret, y_rw, z, l, big['ret_w_o'], big['rwkv_w_o'], big['s5_w_o'], proj)
        self.x = _out_proj(merged, l, big['w_out'], self.x, mod)
        for lst, val in zip(self.outs, (s_ret, s_rw, shift, s_re, s_im)):
            lst.append(val)

    def results(self):
        stacked = [jnp.stack(o) if o[0] is not None else None for o in self.outs]
        if self.states is not None:
            stacked[0], stacked[1] = self.ret_all, jnp.transpose(self.rw_all, (0, 4, 1, 2, 3))
        return self.x.reshape(self.B, self.L, D), stacked


def _trunk(groups, layers, norm_final):
    big = {k: layers[k] for k in _BIG}
    w_in_t = jnp.swapaxes(layers['w_in'], 1, 2)
    for l in range(DEPTH):
        p = {name: arr[l] for name, arr in layers.items() if name not in _BIG + ('w_in',)}
        rp, sp = _rwkv_params(p), _s5_params(p)
        for g in groups:
            g.mix(l, p, rp, sp, layers, big, w_in_t)
        _moe(groups, p, l, big, norm_final if l == DEPTH - 1 else None)
    return [g.results() for g in groups]


_BIG = ('ret_w_o', 'rwkv_w_o', 's5_w_glu', 's5_w_o', 'w_out', 'moe_w1', 'moe_w3', 'moe_w2')


def kernel(x_prompt, x_sample, state_ret, state_rwkv, state_shift, state_s5_re, state_s5_im,
           c_prompt, c_sample, norm_mix, norm_ffn, w_ada, b_ada, w_in, ret_w_o, rwkv_mu, rwkv_w0,
           rwkv_w2, rwkv_a0, rwkv_a2, rwkv_g2, rwkv_k_k, rwkv_k_a, rwkv_r_k, rwkv_ln_w, rwkv_ln_b,
           rwkv_w_o, s5_a_re, s5_a_im, s5_b_re, s5_b_im, s5_c_re, s5_c_im, s5_d, s5_log_dt, s5_w_glu,
           s5_w_o, w_out, moe_w_group, moe_b_group, moe_w_router, moe_b_router, moe_w1, moe_w3, moe_w2,
           norm_final):
    layers = {
        'norm_mix': norm_mix, 'norm_ffn': norm_ffn, 'w_in': w_in,
        'ret_w_o': ret_w_o, 'rwkv_mu': rwkv_mu, 'rwkv_w0': rwkv_w0, 'rwkv_w2': rwkv_w2,
        'rwkv_a0': rwkv_a0, 'rwkv_a2': rwkv_a2, 'rwkv_g2': rwkv_g2, 'rwkv_k_k': rwkv_k_k,
        'rwkv_k_a': rwkv_k_a, 'rwkv_r_k': rwkv_r_k, 'rwkv_ln_w': rwkv_ln_w, 'rwkv_ln_b': rwkv_ln_b,
        'rwkv_w_o': rwkv_w_o, 's5_a_re': s5_a_re, 's5_a_im': s5_a_im, 's5_b_re': s5_b_re,
        's5_b_im': s5_b_im, 's5_c_re': s5_c_re, 's5_c_im': s5_c_im, 's5_d': s5_d,
        's5_log_dt': s5_log_dt, 's5_w_glu': s5_w_glu, 's5_w_o': s5_w_o, 'w_out': w_out,
        'moe_w_group': moe_w_group, 'moe_b_group': moe_b_group, 'moe_w_router': moe_w_router,
        'moe_b_router': moe_b_router, 'moe_w1': moe_w1, 'moe_w3': moe_w3, 'moe_w2': moe_w2,
    }
    Bp, Bs = x_prompt.shape[0], x_sample.shape[0]
    s_off = -(-Bp // 8) * 8
    c_all = jnp.concatenate([c_prompt, jnp.zeros((s_off - Bp, D), f32), c_sample], axis=0)
    mod_all = _adaln(c_all, w_ada, b_ada)
    prompt = _Group(x_prompt, mod_all[:, :Bp], None, 0.0)
    sample = _Group(x_sample, mod_all[:, s_off:s_off + Bs],
                    (state_ret, state_rwkv, state_shift, state_s5_re, state_s5_im), float(PAST_LEN))
    ((y_prompt, (ret_p, rwkv_p, shift_p, s5re_p, s5im_p)),
     (y_sample, (ret_s, rwkv_s, shift_s, s5re_s, s5im_s))) = _trunk([prompt, sample], layers, norm_final)
    return (y_prompt, y_sample, ret_p, ret_s, rwkv_p, rwkv_s, shift_p, shift_s, s5re_p, s5re_s, s5im_p, s5im_s)
```

```python
import jax
import jax.numpy as jnp
from jax import lax
from jax.experimental import pallas as pl
from jax.experimental.pallas import tpu as pltpu

f32 = jnp.float32
bf16 = jnp.bfloat16
i32 = jnp.int32

D = 2048
DEPTH = 2
PAST_LEN = 16384
RET_W, RET_H, RET_DK, RET_DV, RET_CHUNK = 1024, 4, 256, 256, 128
RET_GN_EPS = 1e-6
ROPE_BASE = 10000.0
RWKV_W, RWKV_N, RWKV_H = 1024, 64, 16
DECAY_LORA, AAA_LORA, GATE_LORA = 64, 64, 160
RWKV_PROJ = 3 * RWKV_W + DECAY_LORA + AAA_LORA + GATE_LORA
RWKV_LN_EPS = 64e-5
S5_W, S5_GC, S5_G, S5_P = 1024, 16, 64, 64
N_MOD = 6
RMS_EPS = 1e-6
N_GROUPS, EPG, N_EXPERTS, TOP_K, D_EXPERT = 4, 8, 32, 2, 512
IN_W = 4 * RET_W + RWKV_PROJ + S5_W + 3 * D
OFF_Q, OFF_K, OFF_V, OFF_G, OFF_RW = 0, 1024, 2048, 3072, 4096
OFF_U = OFF_RW + RWKV_PROJ
OFF_GATE = OFF_U + S5_W
LANE = 128
PROJ_W = ((IN_W + LANE - 1) // LANE) * LANE
VMEM_LIMIT = 56 * 1024 * 1024


def _cparams(sem):
    return pltpu.CompilerParams(dimension_semantics=sem, vmem_limit_bytes=VMEM_LIMIT)


def _dot(a, b):
    return jnp.dot(a, b, preferred_element_type=f32)


def _dot_nt(a, b):
    return lax.dot_general(a, b, (((1,), (1,)), ((), ())), preferred_element_type=f32)


def _dot_tn(a, b):
    return lax.dot_general(a, b, (((0,), (0,)), ((), ())), preferred_element_type=f32)


def _rms(x, g):
    return x * lax.rsqrt(jnp.mean(x * x, axis=-1, keepdims=True) + RMS_EPS) * g


def _head_norm(y, eps):
    mu = jnp.mean(y, axis=-1, keepdims=True)
    yc = y - mu
    return yc * lax.rsqrt(jnp.mean(yc * yc, axis=-1, keepdims=True) + eps)


def _row_tile(T):
    return 1024 if T >= 1024 else T


class _Mod:
    def __init__(self, mod, L, tm):
        self.L, self.tm = L, tm
        self.per_token = L == 1
        self.arr = mod if self.per_token else mod.reshape(mod.shape[0], 1, N_MOD * D)

    def spec(self, j, tn, col_of):
        nb = D // tn
        if self.per_token:
            return pl.BlockSpec((self.tm, tn), lambda m, n: (m, j * nb + col_of(n)))
        L, tm = self.L, self.tm
        return pl.BlockSpec((None, 1, tn), lambda m, n: ((m * tm) // L, 0, j * nb + col_of(n)))

    def row_spec(self, j, tm, remap=lambda m: m):
        if self.per_token:
            return pl.BlockSpec((tm, D), lambda m: (remap(m), j))
        L = self.L
        return pl.BlockSpec((None, 1, D), lambda m: ((remap(m) * tm) // L, 0, j))


def _fused_mm(x_ops, w_ops, e_ops, pre, post, *, grid, out_specs, out_shape, cache_shapes):
    nx, nw, ne = len(x_ops), len(w_ops), len(e_ops)
    n_out = len(out_shape)

    def body(*refs):
        x_refs = refs[:nx]
        w_refs = refs[nx:nx + nw]
        e_refs = refs[nx + nw:nx + nw + ne]
        o_refs = refs[nx + nw + ne:nx + nw + ne + n_out]
        caches = refs[nx + nw + ne + n_out:]
        if cache_shapes:
            @pl.when(pl.program_id(1) == 0)
            def _():
                for i in range(nx):
                    caches[i][...] = pre(i, x_refs[i], e_refs).astype(bf16)
            lhs = [c[...] for c in caches]
        else:
            lhs = [x[...] for x in x_refs]
        prods = [(_dot_nt if len(w_ops[j]) > 3 and w_ops[j][3] else _dot)(
            lhs[w_ops[j][2]], w_refs[j][...].astype(bf16)) for j in range(nw)]
        for o_ref, o in zip(o_refs, post(prods, e_refs)):
            o_ref[...] = o.astype(o_ref.dtype)

    return pl.pallas_call(
        body,
        grid=grid,
        in_specs=[s for _, s in x_ops] + [w[1] for w in w_ops] + [s for _, s in e_ops],
        out_specs=out_specs,
        out_shape=out_shape,
        scratch_shapes=[pltpu.VMEM(s, bf16) for s in cache_shapes],
        compiler_params=_cparams(("parallel", "arbitrary")),
    )(*[a for a, _ in x_ops], *[w[0] for w in w_ops], *[a for a, _ in e_ops])


def _adaln(c_all, w_ada, b_ada):
    R = c_all.shape[0]
    tn = 1024

    def pre(i, x_ref, e_refs):
        c = x_ref[...]
        return c * jax.nn.sigmoid(c)

    def post(prods, e_refs):
        return (prods[0] + e_refs[0][...],)

    (out,) = _fused_mm(
        [(c_all, pl.BlockSpec((R, D), lambda l, n: (0, 0)))],
        [(w_ada, pl.BlockSpec((None, D, tn), lambda l, n: (l, 0, n)), 0)],
        [(b_ada.reshape(DEPTH, 1, N_MOD * D), pl.BlockSpec((None, 1, tn), lambda l, n: (l, 0, n)))],
        pre, post,
        grid=(DEPTH, N_MOD * D // tn),
        out_specs=[pl.BlockSpec((None, R, tn), lambda l, n: (l, 0, n))],
        out_shape=[jax.ShapeDtypeStruct((DEPTH, R, N_MOD * D), f32)],
        cache_shapes=[(R, D)],
    )
    return out


def _in_proj(x, g, l, mod, w_in_t):
    T = x.shape[0]
    tm, tn = mod.tm, 1024
    tr = min(tm, 512)

    def norm_body(x_ref, g_ref, shift_ref, scale_ref, h_ref):
        h_ref[...] = (_rms(x_ref[...], g_ref[...]) * (1.0 + scale_ref[...]) + shift_ref[...]).astype(bf16)

    h = pl.pallas_call(
        norm_body,
        grid=(T // tr,),
        in_specs=[pl.BlockSpec((tr, D), lambda m: (m, 0)),
                  pl.BlockSpec((None, 1, D), lambda m: (l, 0, 0)),
                  mod.row_spec(0, tr), mod.row_spec(1, tr)],
        out_specs=pl.BlockSpec((tr, D), lambda m: (m, 0)),
        out_shape=jax.ShapeDtypeStruct((T, D), bf16),
        compiler_params=_cparams(("parallel",)),
    )(x, g.reshape(DEPTH, 1, D), mod.arr, mod.arr)

    def mm_body(h_ref, w_ref, o_ref, wb):
        @pl.when(pl.program_id(1) == 0)
        def _():
            wb[...] = w_ref[...].astype(bf16)

        col = pl.program_id(0) * tn + lax.broadcasted_iota(i32, (tm, tn), 1)
        o_ref[...] = jnp.where(col < IN_W, _dot_nt(h_ref[...], wb[...]), 0.0)

    return pl.pallas_call(
        mm_body,
        grid=(pl.cdiv(PROJ_W, tn), T // tm),
        in_specs=[pl.BlockSpec((tm, D), lambda n, m: (m, 0)),
                  pl.BlockSpec((None, tn, D), lambda n, m: (l, n, 0))],
        out_specs=pl.BlockSpec((tm, tn), lambda n, m: (m, n)),
        out_shape=jax.ShapeDtypeStruct((T, PROJ_W), f32),
        scratch_shapes=[pltpu.VMEM((tn, D), bf16)],
        compiler_params=_cparams(("parallel", "arbitrary")),
    )(h, w_in_t)


def _glu_proj(yg, l, w_glu):
    T = yg.shape[0]
    tm, tn = _row_tile(T), 512
    nb = S5_W // tn

    def post(prods, e_refs):
        return (prods[0] * jax.nn.sigmoid(prods[1]),)

    (out,) = _fused_mm(
        [(yg, pl.BlockSpec((tm, S5_W), lambda m, n: (m, 0)))],
        [(w_glu, pl.BlockSpec((None, S5_W, tn), lambda m, n: (l, 0, n)), 0),
         (w_glu, pl.BlockSpec((None, S5_W, tn), lambda m, n: (l, 0, nb + n)), 0)],
        [], None, post,
        grid=(T // tm, nb),
        out_specs=[pl.BlockSpec((tm, tn), lambda m, n: (m, n))],
        out_shape=[jax.ShapeDtypeStruct((T, S5_W), bf16)],
        cache_shapes=[],
    )
    return out


def _merge_proj(y_ret, y_rw, y_s5, l, w_ret, w_rw, w_s5, proj):
    T = y_ret.shape[0]
    tm, tn = _row_tile(T), 512
    lead = OFF_GATE % LANE
    base = OFF_GATE - lead

    def gate_spec(i):
        return pl.BlockSpec((pl.Element(tm), pl.Element(tn + LANE)),
                            lambda m, n: (pl.multiple_of(m * tm, tm), pl.multiple_of(base + i * D + n * tn, LANE)))

    def post(prods, e_refs):
        acc = None
        for p, e in zip(prods, e_refs):
            t = jax.nn.sigmoid(e[:, lead:lead + tn]) * p
            acc = t if acc is None else acc + t
        return (acc,)

    xspec = pl.BlockSpec((tm, RET_W), lambda m, n: (m, 0))
    wspec = pl.BlockSpec((None, RET_W, tn), lambda m, n: (l, 0, n))
    (out,) = _fused_mm(
        [(y_ret, xspec), (y_rw, xspec), (y_s5, xspec)],
        [(w_ret, wspec, 0), (w_rw, wspec, 1), (w_s5, wspec, 2)],
        [(proj, gate_spec(0)), (proj, gate_spec(1)), (proj, gate_spec(2))],
        None, post,
        grid=(T // tm, D // tn),
        out_specs=[pl.BlockSpec((tm, tn), lambda m, n: (m, n))],
        out_shape=[jax.ShapeDtypeStruct((T, D), bf16)],
        cache_shapes=[],
    )
    return out


def _out_proj(merged, l, w_out, x, mod):
    T = x.shape[0]
    tm, tn = mod.tm, 512

    def post(prods, e_refs):
        x_ref, m_ref = e_refs
        return (x_ref[...] + m_ref[...] * prods[0],)

    (out,) = _fused_mm(
        [(merged, pl.BlockSpec((tm, D), lambda m, n: (m, 0)))],
        [(w_out, pl.BlockSpec((None, D, tn), lambda m, n: (l, 0, n)), 0)],
        [(x, pl.BlockSpec((tm, tn), lambda m, n: (m, n))), (mod.arr, mod.spec(2, tn, lambda n: n))],
        None, post,
        grid=(T // tm, D // tn),
        out_specs=[pl.BlockSpec((tm, tn), lambda m, n: (m, n))],
        out_shape=[jax.ShapeDtypeStruct((T, D), f32)],
        cache_shapes=[],
    )
    return out


def _ret_consts(L, pos0):
    C = RET_CHUNK if L % RET_CHUNK == 0 else L
    H = RET_H
    log_g = jnp.log1p(-jnp.exp2(-5.0 - jnp.arange(H, dtype=f32)))
    i = jnp.arange(C, dtype=f32)
    diff = i[:, None] - i[None, :]
    causal = diff >= 0
    dmask = jnp.where(causal, jnp.exp(jnp.where(causal, diff, 0.0)[None] * log_g[:, None, None]), 0.0)
    kdec = jnp.exp((C - 1.0 - i)[:, None] * log_g[None, :])
    qdec = jnp.exp((i + 1.0)[:, None] * log_g[None, :])
    g_chunk = jnp.exp(C * log_g)
    half = RET_DK // 2
    inv = ROPE_BASE ** (-jnp.arange(half, dtype=f32) / half)
    pos = pos0 + jnp.arange(L, dtype=f32)
    ang = pos[:, None] * inv[None, :]
    return C, dmask, kdec, qdec, g_chunk, jnp.cos(ang), jnp.sin(ang)


def _rotary(x, cos, sin):
    half = RET_DK // 2
    x1, x2 = x[..., :half], x[..., half:]
    return jnp.concatenate([x1 * cos - x2 * sin, x1 * sin + x2 * cos], axis=-1)


def _retention_seq(proj, B, L):
    C, dmask, kdec, qdec, g_chunk, cos, sin = _ret_consts(L, 0.0)
    H, dk = RET_H, RET_DK
    n = L // C
    kdec_f = jnp.broadcast_to(kdec.T[:, :, None], (H, C, dk))
    qdec_f = jnp.broadcast_to(qdec.T[:, :, None], (H, C, dk))
    gch_f = jnp.broadcast_to(g_chunk[:, None, None], (H, 8, dk))
    proj3 = proj.reshape(B, L, PROJ_W)

    def body(q_ref, k_ref, v_ref, g_ref, cos_ref, sin_ref, dm_ref, kd_ref, qd_ref, gc_ref, y_ref, s_ref, st):
        c = pl.program_id(1)

        @pl.when(c == 0)
        def _():
            st[...] = jnp.zeros_like(st)

        cs, sn = cos_ref[...], sin_ref[...]
        hs = range(H)
        sl = [slice(h * dk, (h + 1) * dk) for h in hs]
        q = [_rotary(q_ref[:, sl[h]], cs, sn) for h in hs]
        k = [_rotary(k_ref[:, sl[h]], cs, sn) * (dk ** -0.5) for h in hs]
        vb = [v_ref[:, sl[h]].astype(bf16) for h in hs]
        s0 = [st[h] for h in hs]
        scores = [_dot_nt(q[h].astype(bf16), k[h].astype(bf16)) * dm_ref[h] for h in hs]
        cross = [_dot((q[h] * qd_ref[h]).astype(bf16), s0[h].astype(bf16)) for h in hs]
        kv = [_dot_tn((k[h] * kd_ref[h]).astype(bf16), vb[h]) for h in hs]
        o = [_dot(scores[h].astype(bf16), vb[h]) + cross[h] for h in hs]
        for h in hs:
            st[h] = s0[h] * gc_ref[h, 0:1, :] + kv[h]
            g = g_ref[:, sl[h]]
            y_ref[:, sl[h]] = (g * jax.nn.sigmoid(g) * _head_norm(o[h], RET_GN_EPS)).astype(bf16)

        @pl.when(c == n - 1)
        def _():
            s_ref[...] = st[...]

    def seg(off):
        return pl.BlockSpec((None, C, RET_W), lambda b, c: (b, c, off // RET_W))

    const3 = lambda shp: pl.BlockSpec(shp, lambda b, c: (0, 0, 0))
    y, s = pl.pallas_call(
        body,
        grid=(B, n),
        in_specs=[seg(OFF_Q), seg(OFF_K), seg(OFF_V), seg(OFF_G),
                  pl.BlockSpec((C, dk // 2), lambda b, c: (c, 0)),
                  pl.BlockSpec((C, dk // 2), lambda b, c: (c, 0)),
                  const3((H, C, C)), const3((H, C, dk)), const3((H, C, dk)), const3((H, 8, dk))],
        out_specs=[pl.BlockSpec((None, C, RET_W), lambda b, c: (b, c, 0)),
                   pl.BlockSpec((None, H, dk, RET_DV), lambda b, c: (b, 0, 0, 0))],
        out_shape=[jax.ShapeDtypeStruct((B, L, RET_W), bf16),
                   jax.ShapeDtypeStruct((B, H, dk, RET_DV), f32)],
        scratch_shapes=[pltpu.VMEM((H, dk, RET_DV), f32)],
        compiler_params=_cparams(("parallel", "arbitrary")),
    )(proj3, proj3, proj3, proj3, cos, sin, dmask, kdec_f, qdec_f, gch_f)
    return y.reshape(B * L, RET_W), s


STEP_TB = 16


def _layer_grid(l, buf, inner):
    if buf is not None:
        return inner, (lambda fn: (lambda *ix: fn(l, ix, ix)))
    assert l == 0
    last = tuple(n - 1 for n in inner)

    def wrap(fn):
        def index_map(d, *ix):
            parked = tuple(jnp.where(d == l, i, z) for i, z in zip(ix, last))
            return fn(d, ix, parked)
        return index_map
    return (DEPTH,) + inner, wrap


def _retention_step(proj, s_all, l, buf, pos0):
    B = proj.shape[0]
    _, _, _, _, g_chunk, cos, sin = _ret_consts(1, pos0)
    H, dk = RET_H, RET_DK
    gch = jnp.broadcast_to(g_chunk[:, None, None], (H, 8, dk))
    tb = STEP_TB
    grid, wrap = _layer_grid(l, buf, (B // tb, H))

    def body(q_ref, k_ref, v_ref, g_ref, cos_ref, sin_ref, gc_ref, s_ref, *rest):
        y_ref, so_ref = rest[-2:]

        def update():
            cs, sn = cos_ref[...], sin_ref[...]
            q = _rotary(q_ref[...], cs, sn)
            k = _rotary(k_ref[...], cs, sn) * (dk ** -0.5)
            v = v_ref[...]
            s1 = s_ref[...] * gc_ref[0:1, :][None] + k[:, :, None] * v[:, None, :]
            so_ref[...] = s1
            o = jnp.sum(q[:, :, None] * s1, axis=1)
            g = g_ref[...]
            y_ref[...] = (g * jax.nn.sigmoid(g) * _head_norm(o, RET_GN_EPS)).astype(bf16)

        if buf is not None:
            update()
        else:
            pl.when(pl.program_id(0) == l)(update)

            @pl.when(pl.program_id(0) != l)
            def _():
                so_ref[...] = jnp.zeros_like(so_ref)

    def seg(off):
        return pl.BlockSpec((tb, dk), wrap(lambda d, ix, pk: (pk[0], off // dk + pk[1])))

    const = lambda d, ix, pk: (0, 0)
    in_specs = [seg(OFF_Q), seg(OFF_K), seg(OFF_V), seg(OFF_G),
                pl.BlockSpec((1, dk // 2), wrap(const)), pl.BlockSpec((1, dk // 2), wrap(const)),
                pl.BlockSpec((None, 8, dk), wrap(lambda d, ix, pk: (pk[1], 0, 0))),
                pl.BlockSpec((None, tb, None, dk, RET_DV), wrap(lambda d, ix, pk: (l, pk[0], pk[1], 0, 0)))]
    args = [proj, proj, proj, proj, cos, sin, gch, s_all]
    aliases = {}
    if buf is not None:
        in_specs.append(pl.BlockSpec(memory_space=pl.ANY))
        args.append(buf)
        aliases = {len(args) - 1: 1}
    return pl.pallas_call(
        body,
        grid=grid,
        in_specs=in_specs,
        out_specs=[pl.BlockSpec((tb, dk), wrap(lambda d, ix, pk: pk)),
                   pl.BlockSpec((None, tb, None, dk, RET_DV), wrap(lambda d, ix, pk: (d, ix[0], ix[1], 0, 0)))],
        out_shape=[jax.ShapeDtypeStruct((B, RET_W), bf16),
                   jax.ShapeDtypeStruct((DEPTH, B, H, dk, RET_DV), f32)],
        input_output_aliases=aliases,
        compiler_params=_cparams(("arbitrary",) * len(grid)),
    )(*args)


RW_C = 64
RW_Q = 4
RW_NQ = RWKV_H // RW_Q
RW_GROUP = 8
RW_BLK =((RWKV_PROJ + LANE - 1) // LANE) * LANE
_RW_PKEYS = ('mu', 'w0', 'w2', 'a0', 'a2', 'g2', 'k_k', 'k_a', 'r_k', 'ln_w', 'ln_b')


def _rwkv_params(p):
    return dict(
        mu=jnp.pad(p['rwkv_mu'], (0, RW_BLK - RWKV_PROJ)).reshape(1, RW_BLK),
        w0=p['rwkv_w0'].reshape(1, RWKV_W), w2=p['rwkv_w2'],
        a0=p['rwkv_a0'].reshape(1, RWKV_W), a2=p['rwkv_a2'], g2=p['rwkv_g2'],
        k_k=p['rwkv_k_k'].reshape(1, RWKV_W), k_a=p['rwkv_k_a'].reshape(1, RWKV_W),
        r_k=p['rwkv_r_k'].reshape(1, RWKV_W),
        ln_w=p['rwkv_ln_w'].reshape(1, RWKV_W), ln_b=p['rwkv_ln_b'].reshape(1, RWKV_W))


def _split_bf16(x, terms):
    out = []
    for _ in range(terms - 1):
        hi = x.astype(bf16)
        out.append(hi)
        x = x - hi.astype(f32)
    out.append(x.astype(bf16))
    return out


def _head_sum(x):
    QW = RW_Q * RWKV_N
    r = lax.broadcasted_iota(i32, (QW, QW), 0) // RWKV_N
    c = lax.broadcasted_iota(i32, (QW, QW), 1) // RWKV_N
    ones = (r == c).astype(bf16)
    parts = _split_bf16(x, 2)
    outs = []
    for q in range(RW_NQ):
        sl = slice(q * QW, (q + 1) * QW)
        outs.append(_dot(parts[0][:, sl], ones) + _dot(parts[1][:, sl], ones))
    return jnp.concatenate(outs, axis=1)


def _rwkv_mix(rw, prev, pr):
    m = rw + (prev - rw) * pr['mu'][...]
    r = m[:, 0:RWKV_W]
    k = m[:, RWKV_W:2 * RWKV_W]
    v = m[:, 2 * RWKV_W:3 * RWKV_W]
    o = 3 * RWKV_W
    xw = m[:, o:o + DECAY_LORA]
    xa = m[:, o + DECAY_LORA:o + DECAY_LORA + AAA_LORA]
    xg = m[:, o + DECAY_LORA + AAA_LORA:o + DECAY_LORA + AAA_LORA + GATE_LORA]
    w_log = -jax.nn.softplus(-(pr['w0'][...] + _dot(jnp.tanh(xw).astype(bf16), pr['w2'][...].astype(bf16)))) - 0.5
    lw = -jnp.exp(w_log)
    a = jax.nn.sigmoid(pr['a0'][...] + _dot(xa.astype(bf16), pr['a2'][...].astype(bf16)))
    g = _dot(jax.nn.sigmoid(xg).astype(bf16), pr['g2'][...].astype(bf16))
    kk = k * pr['k_k'][...]
    kk = kk / jnp.maximum(jnp.sqrt(_head_sum(kk * kk)), 1e-12)
    kf = k * (1.0 + (a - 1.0) * pr['k_a'][...])
    return r, lw, kf, v, kk, a, g


def _rwkv_out(y, r, kf, v, g, pr):
    yc = y - _head_sum(y) * (1.0 / RWKV_N)
    yn = yc * lax.rsqrt(_head_sum(yc * yc) * (1.0 / RWKV_N) + RWKV_LN_EPS)
    yn = yn * pr['ln_w'][...] + pr['ln_b'][...]
    yn = yn + _head_sum(r * kf * pr['r_k'][...]) * v
    return yn * g


def _rwkv_seq(proj, rp, B, L):
    C = RW_C
    assert L % C == 0 and C == RWKV_N
    n = L // C
    QW = RW_Q * RWKV_N
    BC = B * C
    proj3 = proj.reshape(B, L, PROJ_W)

    def body(rw_ref, *refs):
        pr = dict(zip(_RW_PKEYS, refs[:len(_RW_PKEYS)]))
        y_ref, s_ref, sh_ref, st, carry = refs[len(_RW_PKEYS):]
        c = pl.program_id(0)

        @pl.when(c == 0)
        def _():
            st[...] = jnp.zeros_like(st)
            carry[...] = jnp.zeros_like(carry)

        rw = rw_ref[...].reshape(BC, RW_BLK)
        rolled = pltpu.roll(rw, 1, 0)
        row = lax.broadcasted_iota(i32, (C, RW_BLK), 0)
        prev = jnp.concatenate(
            [jnp.where(row == 0, carry[b, 0:1, :], rolled[b * C:(b + 1) * C]) for b in range(B)], axis=0)
        for b in range(B):
            carry[b, 0:1, :] = rw[(b + 1) * C - 1:(b + 1) * C, :]
        r, lw, kf, v, kk, a, g = _rwkv_mix(rw, prev, pr)

        ti = lax.broadcasted_iota(i32, (BC, BC), 0)
        si = lax.broadcasted_iota(i32, (BC, BC), 1)
        tril = ((si <= ti) & ((si // C) == (ti // C))).astype(bf16)
        lg = sum(_dot(tril, part) for part in _split_bf16(lw, 3))
        lgc = jnp.concatenate(
            [jnp.broadcast_to(lg[(b + 1) * C - 1:(b + 1) * C, :], (C, RWKV_W)) for b in range(B)], axis=0)
        e_neg = jnp.exp(-lg)
        e_rem = jnp.exp(lgc - lg)
        at = kk * jnp.exp(lg - lw)
        ka = kk * a
        bt = ka * e_neg
        kt = kf * e_neg
        rt = r * jnp.exp(lg)
        bh = ka * e_rem
        kh = kf * e_rem
        gcr = jnp.exp(lgc)

        rr = lax.broadcasted_iota(i32, (RW_Q * C, QW), 0)
        ll = lax.broadcasted_iota(i32, (RW_Q * C, QW), 1)
        blockmask = (rr // C) == (ll // RWKV_N)
        tt = lax.broadcasted_iota(i32, (C, QW), 0)
        ss = lax.broadcasted_iota(i32, (C, QW), 1) % C
        strict = ss < tt
        incl = ss <= tt
        eye = (ss == tt).astype(f32)

        def bd(x):
            return jnp.where(blockmask, jnp.concatenate([x] * RW_Q, axis=0), 0.0).astype(bf16)

        ys = [[None] * RW_NQ for _ in range(B)]
        chains = [(b, q) for b in range(B) for q in range(RW_NQ)]
        for g0 in range(0, len(chains), RW_GROUP):
            grp = chains[g0:g0 + RW_GROUP]
            idx = [(slice(b * C, (b + 1) * C), slice(q * QW, (q + 1) * QW)) for b, q in grp]
            each = lambda fn: [fn(i) for i in range(len(grp))]
            vq = each(lambda i: v[idx[i]])
            ar = each(lambda i: jnp.concatenate([at[idx[i]], rt[idx[i]]], axis=0).astype(bf16))
            big = each(lambda i: _dot_nt(ar[i], jnp.concatenate([bd(bt[idx[i]]), bd(kt[idx[i]])], axis=0)))
            s0 = each(lambda i: st[grp[i]])
            asrs = each(lambda i: _dot_nt(ar[i], s0[i].astype(bf16)))
            nmat = each(lambda i: jnp.where(strict, big[i][:C, :QW], 0.0))
            akm = each(lambda i: jnp.where(strict, big[i][:C, QW:], 0.0).astype(bf16))
            rbk = each(lambda i: jnp.concatenate([jnp.where(incl, big[i][C:, :QW], 0.0),
                                                  jnp.where(incl, big[i][C:, QW:], 0.0)], axis=1).astype(bf16))
            tm = each(lambda i: eye - nmat[i])
            pw = each(lambda i: _dot(nmat[i].astype(bf16), bd(nmat[i])))
            lvl = 2
            while lvl < C:
                res = each(lambda i: _dot(jnp.concatenate([tm[i], pw[i]], axis=0).astype(bf16), bd(pw[i])))
                tm = each(lambda i: tm[i] + res[i][:C])
                pw = each(lambda i: res[i][C:])
                lvl *= 2
            vbd = each(lambda i: bd(vq[i]))
            rhs = each(lambda i: -(asrs[i][:C] + _dot(akm[i], vbd[i])))
            u = each(lambda i: _dot(tm[i].astype(bf16), bd(rhs[i])))
            y = each(lambda i: asrs[i][C:] + _dot(rbk[i], jnp.concatenate([bd(u[i]), vbd[i]], axis=0)))
            upd = each(lambda i: _dot_tn(jnp.concatenate([u[i], vq[i]], axis=0).astype(bf16),
                                         jnp.concatenate([bh[idx[i]], kh[idx[i]]], axis=0).astype(bf16)))
            for i, (b, q) in enumerate(grp):
                st[b, q] = s0[i] * gcr[b * C:b * C + 1, idx[i][1]] + jnp.where(blockmask, upd[i], 0.0)
                ys[b][q] = y[i]

        y = jnp.concatenate([jnp.concatenate(yb, axis=1) for yb in ys], axis=0)
        y_ref[...] = _rwkv_out(y, r, kf, v, g, pr).astype(bf16).reshape(B, C, RWKV_W)

        @pl.when(c == n - 1)
        def _():
            for b in range(B):
                for q in range(RW_NQ):
                    for h in range(RW_Q):
                        hs = slice(h * RWKV_N, (h + 1) * RWKV_N)
                        s_ref[b, q * RW_Q + h] = st[b, q, hs, hs]
                sh_ref[b] = rw[(b + 1) * C - 1:(b + 1) * C, 0:RWKV_PROJ]

    pspecs = [pl.BlockSpec(rp[k].shape, lambda c: (0, 0)) for k in _RW_PKEYS]
    y, s, sh = pl.pallas_call(
        body,
        grid=(n,),
        in_specs=[pl.BlockSpec((pl.Element(B), pl.Element(C), pl.Element(RW_BLK)),
                               lambda c: (0, pl.multiple_of(c * C, C), OFF_RW))] + pspecs,
        out_specs=[pl.BlockSpec((B, C, RWKV_W), lambda c: (0, c, 0)),
                   pl.BlockSpec((B, RWKV_H, RWKV_N, RWKV_N), lambda c: (0, 0, 0, 0)),
                   pl.BlockSpec((B, 1, RWKV_PROJ), lambda c: (0, 0, 0))],
        out_shape=[jax.ShapeDtypeStruct((B, L, RWKV_W), bf16),
                   jax.ShapeDtypeStruct((B, RWKV_H, RWKV_N, RWKV_N), f32),
                   jax.ShapeDtypeStruct((B, 1, RWKV_PROJ), f32)],
        scratch_shapes=[pltpu.VMEM((B, RW_NQ, QW, QW), f32), pltpu.VMEM((B, 8, RW_BLK), f32)],
        compiler_params=_cparams(("arbitrary",)),
    )(proj3, *[rp[k] for k in _RW_PKEYS])
    return y.reshape(B * L, RWKV_W), s, sh.reshape(B, RWKV_PROJ)


def _rwkv_step(proj, shift, s_t, l, buf, rp):
    B = proj.shape[0]
    N = RWKV_N
    shift_p = jnp.pad(shift, ((0, 0), (0, RW_BLK - RWKV_PROJ)))
    grid, wrap = _layer_grid(l, buf, (RWKV_H,))
    npk = len(_RW_PKEYS)
    vec_names = ('r', 'w', 'kf', 'v', 'kk', 'ka')

    def body(rw_ref, sh_ref, s_ref, *refs):
        pr = dict(zip(_RW_PKEYS, refs[:npk]))
        y_ref, so_ref = refs[-9:-7]
        vt = dict(zip(vec_names, refs[-7:-1]))
        yt = refs[-1]
        h = pl.program_id(len(grid) - 1)

        def update():
            @pl.when(h == 0)
            def _():
                r, lw, kf, v, kk, a, _ = _rwkv_mix(rw_ref[...], sh_ref[...], pr)
                for name, val in zip(vec_names, (r, jnp.exp(lw), kf, v, kk, kk * a)):
                    vt[name][...] = val.T

            hs = pl.ds(pl.multiple_of(h * N, N), N)
            s = s_ref[...]
            kk_h = vt['kk'][hs, :]
            sa = jnp.sum(s * (-kk_h)[None], axis=1, keepdims=True)
            s1 = s * vt['w'][hs, :][None] + sa * vt['ka'][hs, :][None] + vt['v'][hs, :][:, None, :] * vt['kf'][hs, :][None]
            so_ref[...] = s1
            yt[hs, :] = jnp.sum(s1 * vt['r'][hs, :][None], axis=1)

            @pl.when(h == RWKV_H - 1)
            def _():
                r, lw, kf, v, kk, a, g = _rwkv_mix(rw_ref[...], sh_ref[...], pr)
                y_ref[...] = _rwkv_out(yt[...].T, r, kf, v, g, pr).astype(bf16)

        if buf is not None:
            update()
        else:
            pl.when(pl.program_id(0) == l)(update)

            @pl.when(pl.program_id(0) != l)
            def _():
                so_ref[...] = jnp.zeros_like(so_ref)

    const = lambda d, ix, pk: (0, 0)
    in_specs = [pl.BlockSpec((pl.Element(B), pl.Element(RW_BLK)), wrap(lambda d, ix, pk: (0, OFF_RW))),
                pl.BlockSpec((B, RW_BLK), wrap(const)),
                pl.BlockSpec((None, None, N, N, B), wrap(lambda d, ix, pk: (l, pk[0], 0, 0, 0)))]
    in_specs += [pl.BlockSpec(rp[k].shape, wrap(const)) for k in _RW_PKEYS]
    args = [proj, shift_p, s_t] + [rp[k] for k in _RW_PKEYS]
    aliases = {}
    if buf is not None:
        in_specs.append(pl.BlockSpec(memory_space=pl.ANY))
        args.append(buf)
        aliases = {len(args) - 1: 1}
    return pl.pallas_call(
        body,
        grid=grid,
        in_specs=in_specs,
        out_specs=[pl.BlockSpec((B, RWKV_W), wrap(const)),
                   pl.BlockSpec((None, None, N, N, B), wrap(lambda d, ix, pk: (d, ix[0], 0, 0, 0)))],
        out_shape=[jax.ShapeDtypeStruct((B, RWKV_W), bf16),
                   jax.ShapeDtypeStruct((DEPTH, RWKV_H, N, N, B), f32)],
        scratch_shapes=[pltpu.VMEM((RWKV_W, B), f32)] * 7,
        input_output_aliases=aliases,
        compiler_params=_cparams(("arbitrary",) * len(grid)),
    )(*args)


S5_N = S5_G * S5_P
S5_KC = 256
S5_NKC = S5_W // S5_KC
S5_TILES = S5_N // LANE
S5_GB = S5_KC // S5_GC


def _s5_params(p):
    a_re, a_im = p['s5_a_re'], p['s5_a_im']
    dstep = jnp.exp(p['s5_log_dt'])[:, None]
    mag = jnp.exp(a_re * dstep)
    ab_re = mag * jnp.cos(a_im * dstep)
    ab_im = mag * jnp.sin(a_im * dstep)
    den = a_re * a_re + a_im * a_im
    n_re = ab_re - 1.0
    f_re = (n_re * a_re + ab_im * a_im) / den
    f_im = (ab_im * a_re - n_re * a_im) / den
    b_re, b_im = p['s5_b_re'], p['s5_b_im']
    bb_re = f_re[..., None] * b_re - f_im[..., None] * b_im
    bb_im = f_re[..., None] * b_im + f_im[..., None] * b_re
    eye = jnp.eye(S5_GB, dtype=f32)

    def in_map(bb):
        t = bb.reshape(S5_NKC, S5_GB, S5_P, S5_GC)
        return jnp.einsum('kgpc,gh->kgchp', t, eye).reshape(S5_NKC, S5_KC, S5_GB * S5_P)

    def out_map(cc):
        t = cc.reshape(S5_NKC, S5_GB, S5_GC, S5_P)
        return jnp.einsum('qgcp,gh->qgphc', t, eye).reshape(S5_NKC, S5_GB * S5_P, S5_KC)

    return dict(wb=jnp.concatenate([in_map(bb_re), in_map(bb_im)], axis=-1),
                wc_re=out_map(p['s5_c_re']), wc_im=out_map(p['s5_c_im']),
                ab_re_t=ab_re.reshape(S5_TILES // 8, 8, LANE), ab_im_t=ab_im.reshape(S5_TILES // 8, 8, LANE),
                ab_re=ab_re.reshape(1, S5_N), ab_im=ab_im.reshape(1, S5_N), d=p['s5_d'].reshape(1, S5_W))


def _s5_seq(proj, sp, B, L):
    Lc = min(L, 256)
    n = L // Lc
    pitch = Lc + 4
    lead = OFF_U % LANE
    base = OFF_U - lead
    width = S5_W + LANE
    nt4 = S5_TILES // 8
    tiles_kc = S5_TILES // S5_NKC

    def body(u_ref, wb_ref, wcr_ref, wci_ref, abr_ref, abi_ref, d_ref, y_ref, sr_ref, si_ref, xr, xi, cr, ci):
        c = pl.program_id(1)

        @pl.when(c == 0)
        def _():
            cr[...] = jnp.zeros_like(cr)
            ci[...] = jnp.zeros_like(ci)

        u = u_ref[:, lead:lead + S5_W]
        ub = u.astype(bf16)
        for kc in range(S5_NKC):
            bu = _dot(ub[:, kc * S5_KC:(kc + 1) * S5_KC], wb_ref[kc].astype(bf16))
            for j in range(tiles_kc):
                t = kc * tiles_kc + j
                xr[pl.ds(t * pitch, Lc), :] = bu[:, j * LANE:(j + 1) * LANE]
                xi[pl.ds(t * pitch, Lc), :] = bu[:, (tiles_kc + j) * LANE:(tiles_kc + j + 1) * LANE]

        abr = [abr_ref[g] for g in range(nt4)]
        abi = [abi_ref[g] for g in range(nt4)]

        def step(t, carry):
            out = []
            for g in range(nt4):
                s_r, s_i = carry[2 * g], carry[2 * g + 1]
                idx = pl.ds(g * 8 * pitch + t, 8, stride=pitch)
                n_r = abr[g] * s_r - abi[g] * s_i + xr[idx, :]
                n_i = abr[g] * s_i + abi[g] * s_r + xi[idx, :]
                xr[idx, :] = n_r
                xi[idx, :] = n_i
                out += [n_r, n_i]
            return tuple(out)

        init = []
        for g in range(nt4):
            init += [cr[g], ci[g]]
        fin = lax.fori_loop(0, Lc, step, tuple(init), unroll=2)
        for g in range(nt4):
            cr[g] = fin[2 * g]
            ci[g] = fin[2 * g + 1]

        for q in range(S5_NKC):
            lr = jnp.concatenate([xr[pl.ds((q * tiles_kc + j) * pitch, Lc), :] for j in range(tiles_kc)], axis=1)
            li = jnp.concatenate([xi[pl.ds((q * tiles_kc + j) * pitch, Lc), :] for j in range(tiles_kc)], axis=1)
            y = _dot(lr.astype(bf16), wcr_ref[q].astype(bf16)) - _dot(li.astype(bf16), wci_ref[q].astype(bf16))
            cs = slice(q * S5_KC, (q + 1) * S5_KC)
            y = y + d_ref[:, cs] * u[:, cs]
            y_ref[:, cs] = jax.nn.gelu(y).astype(bf16)

        @pl.when(c == n - 1)
        def _():
            sr_ref[...] = cr[...]
            si_ref[...] = ci[...]

    full = lambda shp: pl.BlockSpec(shp, lambda b, c: (0,) * len(shp))
    y, sr, si = pl.pallas_call(
        body,
        grid=(B, n),
        in_specs=[pl.BlockSpec((pl.Element(Lc), pl.Element(width)),
                               lambda b, c: (pl.multiple_of((b * n + c) * Lc, Lc), base)),
                  full(sp['wb'].shape), full(sp['wc_re'].shape), full(sp['wc_im'].shape),
                  full(sp['ab_re_t'].shape), full(sp['ab_im_t'].shape), full((1, S5_W))],
        out_specs=[pl.BlockSpec((Lc, S5_W), lambda b, c: (b * n + c, 0)),
                   pl.BlockSpec((None, nt4, 8, LANE), lambda b, c: (b, 0, 0, 0)),
                   pl.BlockSpec((None, nt4, 8, LANE), lambda b, c: (b, 0, 0, 0))],
        out_shape=[jax.ShapeDtypeStruct((B * L, S5_W), bf16),
                   jax.ShapeDtypeStruct((B, nt4, 8, LANE), f32),
                   jax.ShapeDtypeStruct((B, nt4, 8, LANE), f32)],
        scratch_shapes=[pltpu.VMEM((S5_TILES * pitch, LANE), f32), pltpu.VMEM((S5_TILES * pitch, LANE), f32),
                        pltpu.VMEM((nt4, 8, LANE), f32), pltpu.VMEM((nt4, 8, LANE), f32)],
        compiler_params=_cparams(("parallel", "arbitrary")),
    )(proj, sp['wb'], sp['wc_re'], sp['wc_im'], sp['ab_re_t'], sp['ab_im_t'], sp['d'])
    return y, sr.reshape(B, S5_G, S5_P), si.reshape(B, S5_G, S5_P)


def _s5_step(proj, x_re, x_im, sp):
    B = proj.shape[0]
    lead = OFF_U % LANE
    base = OFF_U - lead
    width = S5_W + LANE
    kw = S5_N // S5_NKC

    def body(u_ref, xr_ref, xi_ref, wb_ref, wcr_ref, wci_ref, abr_ref, abi_ref, d_ref, y_ref, sr_ref, si_ref):
        u = u_ref[:, lead:lead + S5_W]
        ub = u.astype(bf16)
        abr, abi = abr_ref[...], abi_ref[...]
        xr, xi = xr_ref[...], xi_ref[...]
        for kc in range(S5_NKC):
            bu = _dot(ub[:, kc * S5_KC:(kc + 1) * S5_KC], wb_ref[kc].astype(bf16))
            sl = slice(kc * kw, (kc + 1) * kw)
            n_r = abr[:, sl] * xr[:, sl] - abi[:, sl] * xi[:, sl] + bu[:, :kw]
            n_i = abr[:, sl] * xi[:, sl] + abi[:, sl] * xr[:, sl] + bu[:, kw:]
            sr_ref[:, sl] = n_r
            si_ref[:, sl] = n_i
            y = _dot(n_r.astype(bf16), wcr_ref[kc].astype(bf16)) - _dot(n_i.astype(bf16), wci_ref[kc].astype(bf16))
            cs = slice(kc * S5_KC, (kc + 1) * S5_KC)
            y = y + d_ref[:, cs] * u[:, cs]
            y_ref[:, cs] = jax.nn.gelu(y).astype(bf16)

    full = lambda shp: pl.BlockSpec(shp, lambda i: (0,) * len(shp))
    y, sr, si = pl.pallas_call(
        body,
        grid=(1,),
        in_specs=[pl.BlockSpec((pl.Element(B), pl.Element(width)), lambda i: (0, base)),
                  full((B, S5_N)), full((B, S5_N)),
                  full(sp['wb'].shape), full(sp['wc_re'].shape), full(sp['wc_im'].shape),
                  full((1, S5_N)), full((1, S5_N)), full((1, S5_W))],
        out_specs=[full((B, S5_W)), full((B, S5_N)), full((B, S5_N))],
        out_shape=[jax.ShapeDtypeStruct((B, S5_W), bf16),
                   jax.ShapeDtypeStruct((B, S5_N), f32), jax.ShapeDtypeStruct((B, S5_N), f32)],
        compiler_params=_cparams(("arbitrary",)),
    )(proj, x_re.reshape(B, S5_N), x_im.reshape(B, S5_N), sp['wb'], sp['wc_re'], sp['wc_im'],
      sp['ab_re'], sp['ab_im'], sp['d'])
    return y, sr.reshape(B, S5_G, S5_P), si.reshape(B, S5_G, S5_P)


MOE_RT, MOE_RT_SMALL = 128, 32
SLAB = D // LANE
SUB = 8
SLAB_PITCH = SLAB + SUB
H2_PITCH = SLAB + SUB
O2_PITCH = TOP_K * SLAB + SUB


def _moe_rt(T):
    return MOE_RT if T * TOP_K >= N_EXPERTS * MOE_RT else MOE_RT_SMALL
ROUTE_W = LANE


def _router(x, g, mod, w_r, b_r, row0, t_all, bufs):
    T = x.shape[0]
    tm = min(mod.tm, 512)
    nm = T // tm
    assert row0 % tm == 0
    if bufs is None:
        assert row0 == 0
        n_steps = pl.cdiv(t_all, tm)
    else:
        n_steps = nm
    clamp = lambda m: jnp.minimum(m, nm - 1)

    def body(x_ref, g_ref, sh_ref, sc_ref, w_ref, b_ref, *rest):
        h_ref, e_ref, p_ref = rest[-3:]
        if n_steps > nm:
            @pl.when(pl.program_id(0) >= nm)
            def _():
                h_ref[...] = jnp.zeros_like(h_ref)
                e_ref[...] = jnp.zeros_like(e_ref)
                p_ref[...] = jnp.zeros_like(p_ref)

            pl.when(pl.program_id(0) < nm)(lambda: route(x_ref, g_ref, sh_ref, sc_ref, w_ref, b_ref, *rest[-3:]))
        else:
            route(x_ref, g_ref, sh_ref, sc_ref, w_ref, b_ref, *rest[-3:])

    def route(x_ref, g_ref, sh_ref, sc_ref, w_ref, b_ref, h_ref, e_ref, p_ref):
        h2 = _rms(x_ref[...], g_ref[...]) * (1.0 + sc_ref[...]) + sh_ref[...]
        for s in range(SLAB):
            h_ref[pl.ds(s, tm, stride=H2_PITCH), :] = h2[:, s * LANE:(s + 1) * LANE]
        for s in range(SLAB, H2_PITCH):
            h_ref[pl.ds(s, tm, stride=H2_PITCH), :] = jnp.zeros((tm, LANE), f32)
        logits = _dot(h2.astype(bf16), w_ref[...].astype(bf16)) + b_ref[...]
        lane = lax.broadcasted_iota(i32, (tm, ROUTE_W), 1)
        ninf = jnp.float32(-jnp.inf)
        gl = jnp.where(lane < N_GROUPS, logits, ninf)
        gm = jnp.max(gl, axis=-1, keepdims=True)
        g_p = 1.0 / jnp.sum(jnp.exp(gl - gm), axis=-1, keepdims=True)
        g_idx = jnp.min(jnp.where(gl == gm, lane, ROUTE_W), axis=-1, keepdims=True)
        valid = (lane >= N_GROUPS) & (lane < N_GROUPS + N_EXPERTS) & (((lane - N_GROUPS) // EPG) == g_idx)
        el = jnp.where(valid, logits, ninf)
        ee = jnp.exp(el - jnp.max(el, axis=-1, keepdims=True))
        prob = jnp.where(valid, ee / jnp.sum(ee, axis=-1, keepdims=True), -1.0)
        p1 = jnp.max(prob, axis=-1, keepdims=True)
        i1 = jnp.min(jnp.where(prob == p1, lane, ROUTE_W), axis=-1, keepdims=True)
        prob2 = jnp.where(lane == i1, -1.0, prob)
        p2 = jnp.max(prob2, axis=-1, keepdims=True)
        i2 = jnp.min(jnp.where(prob2 == p2, lane, ROUTE_W), axis=-1, keepdims=True)
        den = p1 + p2
        e_ref[...] = jnp.where(lane == 0, i1 - N_GROUPS, jnp.where(lane == 1, i2 - N_GROUPS, 0))
        p_ref[...] = jnp.where(lane == 0, g_p * p1 / den, jnp.where(lane == 1, g_p * p2 / den, 0.0))

    in_specs = [pl.BlockSpec((tm, D), lambda m: (clamp(m), 0)),
                pl.BlockSpec((1, D), lambda m: (0, 0)),
                mod.row_spec(3, tm, clamp), mod.row_spec(4, tm, clamp),
                pl.BlockSpec((D, ROUTE_W), lambda m: (0, 0)),
                pl.BlockSpec((1, ROUTE_W), lambda m: (0, 0))]
    args = [x, g.reshape(1, D), mod.arr, mod.arr, w_r, b_r]
    aliases = {}
    if bufs is not None:
        in_specs += [pl.BlockSpec(memory_space=pl.ANY)] * 3
        aliases = {len(args) + i: i for i in range(3)}
        args += list(bufs)
    b0 = row0 // tm
    return pl.pallas_call(
        body,
        grid=(n_steps,),
        in_specs=in_specs,
        out_specs=[pl.BlockSpec((tm * H2_PITCH, LANE), lambda m: (b0 + m, 0)),
                   pl.BlockSpec((tm, ROUTE_W), lambda m: (b0 + m, 0)),
                   pl.BlockSpec((tm, ROUTE_W), lambda m: (b0 + m, 0))],
        out_shape=[jax.ShapeDtypeStruct((t_all * H2_PITCH, LANE), f32), jax.ShapeDtypeStruct((t_all, ROUTE_W), i32),
                   jax.ShapeDtypeStruct((t_all, ROUTE_W), f32)],
        input_output_aliases=aliases,
        compiler_params=_cparams(("arbitrary",)),
    )(*args)


TOK_BITS = 14


def _moe_plan(eid, T):
    A = T * TOP_K
    RT = _moe_rt(T)
    nt = pl.cdiv(A, RT) + N_EXPERTS
    flat_e = eid[:, :TOP_K].reshape(-1)
    order = jnp.argsort(flat_e).astype(i32)
    counts = jnp.sum((flat_e[:, None] == jnp.arange(N_EXPERTS, dtype=i32)[None, :]).astype(i32), axis=0)
    start = jnp.cumsum(counts) - counts
    pcnt = (counts + RT - 1) // RT * RT
    pend = jnp.cumsum(pcnt)
    tile_e = jnp.minimum(jnp.sum((pend[None, :] <= (jnp.arange(nt, dtype=i32) * RT)[:, None]).astype(i32), axis=1),
                         N_EXPERTS - 1)
    off = (jnp.arange(nt, dtype=i32) * RT - (pend - pcnt)[tile_e])[:, None] + jnp.arange(RT, dtype=i32)[None, :]
    real = off < counts[tile_e][:, None]
    srt = start[tile_e][:, None] + jnp.minimum(off, counts[tile_e][:, None])
    a = order[jnp.minimum(srt, A - 1)]
    slot = jnp.arange(nt * RT, dtype=i32).reshape(nt, RT)
    tok = jnp.where(real, a // TOP_K, 0)
    dst = jnp.where(real, a, A + slot - srt)
    n_used = (pend[-1] // RT).astype(i32).reshape(1)
    eidx = jnp.arange(N_EXPERTS, dtype=i32)
    in_use = counts > 0
    rank = jnp.cumsum(in_use.astype(i32)) - 1
    after = lax.cummin(jnp.where(in_use, eidx, N_EXPERTS), reverse=True)
    nxt = jnp.concatenate([after[1:], jnp.full((1,), N_EXPERTS, i32)])
    nxt = jnp.where(nxt < N_EXPERTS, nxt, -1)
    tile_info = jnp.concatenate([tile_e, (rank % 2)[tile_e], nxt[tile_e]]).astype(i32)
    return (tok | (dst << TOK_BITS)).reshape(-1), tile_info, n_used


def _experts(h2, packed, tile_e, n_used, l, w1, w3, w2):
    T = h2.shape[0] // H2_PITCH
    assert TOP_K == 2 and T < (1 << TOK_BITS)
    RT = _moe_rt(T)
    PITCH = SLAB_PITCH
    nt = tile_e.shape[0] // 3
    prime_id = nt * RT
    out_rows = (prime_id + RT) // TOP_K
    RING = 4
    HC, OC = 128, 256
    n_hc, n_oc = D_EXPERT // HC, D // OC
    g_per, s_per = RT // n_hc, RT // n_oc

    def body(tile_ref, nused_ref, slot_ref, h_hbm, w1_hbm, w3_hbm, w2_hbm, o_hbm, *scratch):
        xs, os_ = scratch[:RING], scratch[RING:2 * RING]
        w1b, w3b, w2b, wf1, wf3, wf2, gsem, ssem, wsem = scratch[2 * RING:]
        x0 = xs[0]
        j = pl.program_id(0)
        n_used = nused_ref[0]
        active = j < n_used

        def gather_row(b, r, tok):
            src = pl.multiple_of(tok * H2_PITCH, SUB)
            return pltpu.make_async_copy(h_hbm.at[pl.ds(src, SLAB), :], xs[b].at[pl.ds(r * PITCH, SLAB), :],
                                         gsem.at[b])

        def scatter_row(b, r, d):
            dst = pl.multiple_of(lax.shift_right_logical(d, 1) * O2_PITCH + (d & (TOP_K - 1)) * SLAB, SUB)
            return pltpu.make_async_copy(os_[b].at[pl.ds(r * PITCH, SLAB), :], o_hbm.at[pl.ds(dst, SLAB), :],
                                         ssem.at[b])

        def wait_rows(sem):
            pltpu.make_async_copy(h_hbm.at[pl.ds(0, RT * SLAB), :], x0.at[pl.ds(0, RT * SLAB), :], sem).wait()

        tok_of = lambda s: s & ((1 << TOK_BITS) - 1)
        dst_of = lambda s: lax.shift_right_logical(s, TOK_BITS)
        last_tile = n_used - 1

        @pl.when(j == 0)
        def _():
            os_[RING - 1][...] = jnp.zeros_like(os_[RING - 1])
            for b in range(RING - 1):
                base = jnp.minimum(b, last_tile) * RT

                def one(r, _, b=b, base=base):
                    gather_row(b, r, tok_of(slot_ref[base + r])).start()
                    return 0
                lax.fori_loop(0, RT, one, 0)

        def fetch(e, s):
            return [pltpu.make_async_copy(w_hbm.at[l, e], wf.at[s], wsem.at[s])
                    for w_hbm, wf in ((w1_hbm, wf1), (w3_hbm, wf3), (w2_hbm, wf2))]

        @pl.when(j == 0)
        def _():
            for c in fetch(tile_ref[0], 0):
                c.start()

        @pl.when(active & ((j == 0) | (tile_ref[j] != tile_ref[jnp.maximum(j - 1, 0)])))
        def _():
            ws, e_nxt = tile_ref[nt + j], tile_ref[2 * nt + j]
            for c in fetch(tile_ref[j], ws):
                c.wait()
            w1b[...] = wf1[ws].astype(bf16)
            w3b[...] = wf3[ws].astype(bf16)
            w2b[...] = wf2[ws].astype(bf16)

            @pl.when(e_nxt >= 0)
            def _():
                for c in fetch(e_nxt, 1 - ws):
                    c.start()

        def tile(cur):
            nxt_b = prv_b = (cur + RING - 1) % RING

            @pl.when(j >= RING - 1)
            def _():
                wait_rows(ssem.at[cur])

            wait_rows(gsem.at[cur])
            xb = jnp.concatenate([xs[cur][pl.ds(s, RT, stride=PITCH), :] for s in range(SLAB)],
                                 axis=1).astype(bf16)
            nxt = jnp.minimum(j + RING - 1, last_tile) * RT
            prv = jnp.maximum(j - 1, 0) * RT
            parts = []
            for c in range(n_hc):
                h1 = _dot(xb, w1b[:, c * HC:(c + 1) * HC])
                h3 = _dot(xb, w3b[:, c * HC:(c + 1) * HC])
                parts.append(((h1 * jax.nn.sigmoid(h1)) * h3).astype(bf16))
                for r in range(c * g_per, (c + 1) * g_per):
                    gather_row(nxt_b, r, tok_of(slot_ref[nxt + r])).start(priority=r % 2)
            hm = jnp.concatenate(parts, axis=1)
            for c in range(n_oc):
                res = _dot(hm, w2b[:, c * OC:(c + 1) * OC])
                for t in range(OC // LANE):
                    os_[cur][pl.ds(c * (OC // LANE) + t, RT, stride=PITCH), :] = res[:, t * LANE:(t + 1) * LANE]
                for r in range(c * s_per, (c + 1) * s_per):
                    d = jnp.where(j == 0, prime_id + r, dst_of(slot_ref[prv + r]))
                    scatter_row(prv_b, r, d).start(priority=r % 2)

        def last(cur):
            def one(r, _):
                scatter_row(cur, r, dst_of(slot_ref[j * RT + r])).start()
                return 0
            lax.fori_loop(0, RT, one, 0)
            wait_rows(ssem.at[cur])
            for back in range(1, RING):
                pl.when(j >= back - 1)(lambda b=(cur - back) % RING: wait_rows(ssem.at[b]))
            for ahead in range(1, RING):
                wait_rows(gsem.at[(cur + ahead) % RING])

        for cur in range(RING):
            pl.when(active & (j % RING == cur))(lambda cur=cur: tile(cur))
        for cur in range(RING):
            pl.when((j == last_tile) & (j % RING == cur))(lambda cur=cur: last(cur))

    grid_spec = pltpu.PrefetchScalarGridSpec(
        num_scalar_prefetch=3,
        grid=(nt,),
        in_specs=[pl.BlockSpec(memory_space=pl.ANY)] * 4,
        out_specs=pl.BlockSpec(memory_space=pl.ANY),
        scratch_shapes=[pltpu.VMEM((RT * PITCH, LANE), f32)] * (2 * RING) + [
                        pltpu.VMEM((D, D_EXPERT), bf16), pltpu.VMEM((D, D_EXPERT), bf16),
                        pltpu.VMEM((D_EXPERT, D), bf16),
                        pltpu.VMEM((2, D, D_EXPERT), f32), pltpu.VMEM((2, D, D_EXPERT), f32),
                        pltpu.VMEM((2, D_EXPERT, D), f32),
                        pltpu.SemaphoreType.DMA((RING,)), pltpu.SemaphoreType.DMA((RING,)),
                        pltpu.SemaphoreType.DMA((2,))],
    )
    return pl.pallas_call(
        body,
        grid_spec=grid_spec,
        out_shape=jax.ShapeDtypeStruct((out_rows * O2_PITCH, LANE), f32),
        compiler_params=_cparams(("arbitrary",)),
    )(tile_e, n_used, packed, h2, w1, w3, w2)


def _moe_combine(x, o2, wgt, mod, norm_final, row0):
    T = x.shape[0]
    tm = min(mod.tm, 512)
    b0 = row0 // tm

    def body(x_ref, o_ref, p_ref, m_ref, *rest):
        p = p_ref[...]
        row = lambda k: jnp.concatenate(
            [o_ref[pl.ds(k * SLAB + s, tm, stride=O2_PITCH), :] for s in range(SLAB)], axis=1)
        y = p[:, 0:1] * row(0)
        for k in range(1, TOP_K):
            y = y + p[:, k:k + 1] * row(k)
        xo = x_ref[...] + m_ref[...] * y
        if norm_final is None:
            rest[0][...] = xo
        else:
            rest[1][...] = _rms(xo, rest[0][...])

    ins = [x, o2, wgt, mod.arr]
    specs = [pl.BlockSpec((tm, D), lambda m: (m, 0)),
             pl.BlockSpec((tm * O2_PITCH, LANE), lambda m: (b0 + m, 0)),
             pl.BlockSpec((tm, ROUTE_W), lambda m: (b0 + m, 0)), mod.row_spec(5, tm)]
    if norm_final is not None:
        ins.append(norm_final.reshape(1, D))
        specs.append(pl.BlockSpec((1, D), lambda m: (0, 0)))
    return pl.pallas_call(
        body,
        grid=(T // tm,),
        in_specs=specs,
        out_specs=pl.BlockSpec((tm, D), lambda m: (m, 0)),
        out_shape=jax.ShapeDtypeStruct((T, D), f32),
        compiler_params=_cparams(("parallel",)),
    )(*ins)


def _moe(groups, p, l, big, norm_final):
    pad = ROUTE_W - N_GROUPS - N_EXPERTS
    w_r = jnp.pad(jnp.concatenate([p['moe_w_group'], p['moe_w_router']], axis=1), ((0, 0), (0, pad)))
    b_r = jnp.pad(jnp.concatenate([p['moe_b_group'], p['moe_b_router']]), (0, pad)).reshape(1, ROUTE_W)
    t_all = sum(g.T for g in groups)
    bufs, row0 = None, 0
    for g in groups:
        g.row0 = row0
        bufs = _router(g.x, p['norm_ffn'], g.mod, w_r, b_r, row0, t_all, bufs)
        row0 += g.T
    h2, eid, wgt = bufs
    packed, tile_e, n_used = _moe_plan(eid, t_all)
    o2 = _experts(h2, packed, tile_e, n_used, l, big['moe_w1'], big['moe_w3'], big['moe_w2'])
    for g in groups:
        g.x = _moe_combine(g.x, o2, wgt, g.mod, norm_final, g.row0)


class _Group:
    def __init__(self, x, mods, states, pos0):
        self.B, self.L, _ = x.shape
        self.T = self.B * self.L
        self.tm = _row_tile(self.T)
        self.x = x.reshape(self.T, D)
        self.mods, self.states, self.pos0 = mods, states, pos0
        self.outs = ([], [], [], [], [])
        self.ret_all = self.rw_all = None

    def mix(self, l, p, rp, sp, layers, big, w_in_t):
        B, L = self.B, self.L
        self.mod = mod = _Mod(self.mods[l], L, self.tm)
        proj = _in_proj(self.x, layers['norm_mix'], l, mod, w_in_t)
        if self.states is None:
            y_ret, s_ret = _retention_seq(proj, B, L)
            y_rw, s_rw, shift = _rwkv_seq(proj, rp, B, L)
            y_s5, s_re, s_im = _s5_seq(proj, sp, B, L)
        else:
            st_ret, st_rw, st_shift, st_re, st_im = self.states
            y_ret, self.ret_all = _retention_step(proj, st_ret, l, self.ret_all, self.pos0)
            y_rw, self.rw_all = _rwkv_step(proj, st_shift[l], jnp.transpose(st_rw, (0, 2, 3, 4, 1)), l,
                                           self.rw_all, rp)
            s_ret = s_rw = None
            shift = proj[:, OFF_RW:OFF_RW + RWKV_PROJ]
            y_s5, s_re, s_im = _s5_step(proj, st_re[l], st_im[l], sp)
        z = _glu_proj(y_s5, l, big['s5_w_glu'])
        merged = _merge_proj(y_ret, y_rw, z, l, big['ret_w_o'], big['rwkv_w_o'], big['s5_w_o'], proj)
        self.x = _out_proj(merged, l, big['w_out'], self.x, mod)
        for lst, val in zip(self.outs, (s_ret, s_rw, shift, s_re, s_im)):
            lst.append(val)

    def results(self):
        stacked = [jnp.stack(o) if o[0] is not None else None for o in self.outs]
        if self.states is not None:
            stacked[0], stacked[1] = self.ret_all, jnp.transpose(self.rw_all, (0, 4, 1, 2, 3))
        return self.x.reshape(self.B, self.L, D), stacked


def _trunk(groups, layers, norm_final):
    big = {k: layers[k] for k in _BIG}
    w_in_t = jnp.swapaxes(layers['w_in'], 1, 2)
    for l in range(DEPTH):
        p = {name: arr[l] for name, arr in layers.items() if name not in _BIG + ('w_in',)}
        rp, sp = _rwkv_params(p), _s5_params(p)
        for g in groups:
            g.mix(l, p, rp, sp, layers, big, w_in_t)
        _moe(groups, p, l, big, norm_final if l == DEPTH - 1 else None)
    return [g.results() for g in groups]


_BIG = ('ret_w_o', 'rwkv_w_o', 's5_w_glu', 's5_w_o', 'w_out', 'moe_w1', 'moe_w3', 'moe_w2')


def kernel(x_prompt, x_sample, state_ret, state_rwkv, state_shift, state_s5_re, state_s5_im,
           c_prompt, c_sample, norm_mix, norm_ffn, w_ada, b_ada, w_in, ret_w_o, rwkv_mu, rwkv_w0,
           rwkv_w2, rwkv_a0, rwkv_a2, rwkv_g2, rwkv_k_k, rwkv_k_a, rwkv_r_k, rwkv_ln_w, rwkv_ln_b,
           rwkv_w_o, s5_a_re, s5_a_im, s5_b_re, s5_b_im, s5_c_re, s5_c_im, s5_d, s5_log_dt, s5_w_glu,
           s5_w_o, w_out, moe_w_group, moe_b_group, moe_w_router, moe_b_router, moe_w1, moe_w3, moe_w2,
           norm_final):
    layers = {
        'norm_mix': norm_mix, 'norm_ffn': norm_ffn, 'w_in': w_in,
        'ret_w_o': ret_w_o, 'rwkv_mu': rwkv_mu, 'rwkv_w0': rwkv_w0, 'rwkv_w2': rwkv_w2,
        'rwkv_a0': rwkv_a0, 'rwkv_a2': rwkv_a2, 'rwkv_g2': rwkv_g2, 'rwkv_k_k': rwkv_k_k,
        'rwkv_k_a': rwkv_k_a, 'rwkv_r_k': rwkv_r_k, 'rwkv_ln_w': rwkv_ln_w, 'rwkv_ln_b': rwkv_ln_b,
        'rwkv_w_o': rwkv_w_o, 's5_a_re': s5_a_re, 's5_a_im': s5_a_im, 's5_b_re': s5_b_re,
        's5_b_im': s5_b_im, 's5_c_re': s5_c_re, 's5_c_im': s5_c_im, 's5_d': s5_d,
        's5_log_dt': s5_log_dt, 's5_w_glu': s5_w_glu, 's5_w_o': s5_w_o, 'w_out': w_out,
        'moe_w_group': moe_w_group, 'moe_b_group': moe_b_group, 'moe_w_router': moe_w_router,
        'moe_b_router': moe_b_router, 'moe_w1': moe_w1, 'moe_w3': moe_w3, 'moe_w2': moe_w2,
    }
    Bp, Bs = x_prompt.shape[0], x_sample.shape[0]
    s_off = -(-Bp // 8) * 8
    c_all = jnp.concatenate([c_prompt, jnp.zeros((s_off - Bp, D), f32), c_sample], axis=0)
    mod_all = _adaln(c_all, w_ada, b_ada)
    prompt = _Group(x_prompt, mod_all[:, :Bp], None, 0.0)
    sample = _Group(x_sample, mod_all[:, s_off:s_off + Bs],
                    (state_ret, state_rwkv, state_shift, state_s5_re, state_s5_im), float(PAST_LEN))
    ((y_prompt, (ret_p, rwkv_p, shift_p, s5re_p, s5im_p)),
     (y_sample, (ret_s, rwkv_s, shift_s, s5re_s, s5im_s))) = _trunk([prompt, sample], layers, norm_final)
    return (y_prompt, y_sample, ret_p, ret_s, rwkv_p, rwkv_s, shift_p, shift_s, s5re_p, s5re_s, s5im_p, s5im_s)
```

```python
import jax
import jax.numpy as jnp
from jax import lax
from jax.experimental import pallas as pl
from jax.experimental.pallas import tpu as pltpu

f32 = jnp.float32
bf16 = jnp.bfloat16
i32 = jnp.int32

D = 2048
DEPTH = 2
PAST_LEN = 16384
RET_W, RET_H, RET_DK, RET_DV, RET_CHUNK = 1024, 4, 256, 256, 128
RET_GN_EPS = 1e-6
ROPE_BASE = 10000.0
RWKV_W, RWKV_N, RWKV_H = 1024, 64, 16
DECAY_LORA, AAA_LORA, GATE_LORA = 64, 64, 160
RWKV_PROJ = 3 * RWKV_W + DECAY_LORA + AAA_LORA + GATE_LORA
RWKV_LN_EPS = 64e-5
S5_W, S5_GC, S5_G, S5_P = 1024, 16, 64, 64
N_MOD = 6
RMS_EPS = 1e-6
N_GROUPS, EPG, N_EXPERTS, TOP_K, D_EXPERT = 4, 8, 32, 2, 512
IN_W = 4 * RET_W + RWKV_PROJ + S5_W + 3 * D
OFF_Q, OFF_K, OFF_V, OFF_G, OFF_RW = 0, 1024, 2048, 3072, 4096
OFF_U = OFF_RW + RWKV_PROJ
OFF_GATE = OFF_U + S5_W
LANE = 128
PROJ_W = ((IN_W + LANE - 1) // LANE) * LANE
VMEM_LIMIT = 56 * 1024 * 1024


def _cparams(sem):
    return pltpu.CompilerParams(dimension_semantics=sem, vmem_limit_bytes=VMEM_LIMIT)


def _dot(a, b):
    return jnp.dot(a, b, preferred_element_type=f32)


def _dot_nt(a, b):
    return lax.dot_general(a, b, (((1,), (1,)), ((), ())), preferred_element_type=f32)


def _dot_tn(a, b):
    return lax.dot_general(a, b, (((0,), (0,)), ((), ())), preferred_element_type=f32)


def _rms(x, g):
    return x * lax.rsqrt(jnp.mean(x * x, axis=-1, keepdims=True) + RMS_EPS) * g


def _head_norm(y, eps):
    mu = jnp.mean(y, axis=-1, keepdims=True)
    yc = y - mu
    return yc * lax.rsqrt(jnp.mean(yc * yc, axis=-1, keepdims=True) + eps)


def _row_tile(T):
    return 1024 if T >= 1024 else T


class _Mod:
    def __init__(self, mod, L, tm):
        self.L, self.tm = L, tm
        self.per_token = L == 1
        self.arr = mod if self.per_token else mod.reshape(mod.shape[0], 1, N_MOD * D)

    def spec(self, j, tn, col_of):
        nb = D // tn
        if self.per_token:
            return pl.BlockSpec((self.tm, tn), lambda m, n: (m, j * nb + col_of(n)))
        L, tm = self.L, self.tm
        return pl.BlockSpec((None, 1, tn), lambda m, n: ((m * tm) // L, 0, j * nb + col_of(n)))

    def row_spec(self, j, tm, remap=lambda m: m):
        if self.per_token:
            return pl.BlockSpec((tm, D), lambda m: (remap(m), j))
        L = self.L
        return pl.BlockSpec((None, 1, D), lambda m: ((remap(m) * tm) // L, 0, j))


def _fused_mm(x_ops, w_ops, e_ops, pre, post, *, grid, out_specs, out_shape, cache_shapes):
    nx, nw, ne = len(x_ops), len(w_ops), len(e_ops)
    n_out = len(out_shape)

    def body(*refs):
        x_refs = refs[:nx]
        w_refs = refs[nx:nx + nw]
        e_refs = refs[nx + nw:nx + nw + ne]
        o_refs = refs[nx + nw + ne:nx + nw + ne + n_out]
        caches = refs[nx + nw + ne + n_out:]
        if cache_shapes:
            @pl.when(pl.program_id(1) == 0)
            def _():
                for i in range(nx):
                    caches[i][...] = pre(i, x_refs[i], e_refs).astype(bf16)
            lhs = [c[...] for c in caches]
        else:
            lhs = [x[...] for x in x_refs]
        prods = [(_dot_nt if len(w_ops[j]) > 3 and w_ops[j][3] else _dot)(
            lhs[w_ops[j][2]], w_refs[j][...].astype(bf16)) for j in range(nw)]
        for o_ref, o in zip(o_refs, post(prods, e_refs)):
            o_ref[...] = o.astype(o_ref.dtype)

    return pl.pallas_call(
        body,
        grid=grid,
        in_specs=[s for _, s in x_ops] + [w[1] for w in w_ops] + [s for _, s in e_ops],
        out_specs=out_specs,
        out_shape=out_shape,
        scratch_shapes=[pltpu.VMEM(s, bf16) for s in cache_shapes],
        compiler_params=_cparams(("parallel", "arbitrary")),
    )(*[a for a, _ in x_ops], *[w[0] for w in w_ops], *[a for a, _ in e_ops])


def _adaln(c_all, w_ada, b_ada):
    R = c_all.shape[0]
    tn = 1024

    def pre(i, x_ref, e_refs):
        c = x_ref[...]
        return c * jax.nn.sigmoid(c)

    def post(prods, e_refs):
        return (prods[0] + e_refs[0][...],)

    (out,) = _fused_mm(
        [(c_all, pl.BlockSpec((R, D), lambda l, n: (0, 0)))],
        [(w_ada, pl.BlockSpec((None, D, tn), lambda l, n: (l, 0, n)), 0)],
        [(b_ada.reshape(DEPTH, 1, N_MOD * D), pl.BlockSpec((None, 1, tn), lambda l, n: (l, 0, n)))],
        pre, post,
        grid=(DEPTH, N_MOD * D // tn),
        out_specs=[pl.BlockSpec((None, R, tn), lambda l, n: (l, 0, n))],
        out_shape=[jax.ShapeDtypeStruct((DEPTH, R, N_MOD * D), f32)],
        cache_shapes=[(R, D)],
    )
    return out


def _in_proj(x, g, l, mod, w_in_t):
    T = x.shape[0]
    tm, tn = mod.tm, 1024
    tr = min(tm, 512)

    def norm_body(x_ref, g_ref, shift_ref, scale_ref, h_ref):
        h_ref[...] = (_rms(x_ref[...], g_ref[...]) * (1.0 + scale_ref[...]) + shift_ref[...]).astype(bf16)

    h = pl.pallas_call(
        norm_body,
        grid=(T // tr,),
        in_specs=[pl.BlockSpec((tr, D), lambda m: (m, 0)),
                  pl.BlockSpec((None, 1, D), lambda m: (l, 0, 0)),
                  mod.row_spec(0, tr), mod.row_spec(1, tr)],
        out_specs=pl.BlockSpec((tr, D), lambda m: (m, 0)),
        out_shape=jax.ShapeDtypeStruct((T, D), bf16),
        compiler_params=_cparams(("parallel",)),
    )(x, g.reshape(DEPTH, 1, D), mod.arr, mod.arr)

    def mm_body(h_ref, w_ref, o_ref, wb):
        @pl.when(pl.program_id(1) == 0)
        def _():
            wb[...] = w_ref[...].astype(bf16)

        col = pl.program_id(0) * tn + lax.broadcasted_iota(i32, (tm, tn), 1)
        o_ref[...] = jnp.where(col < IN_W, _dot_nt(h_ref[...], wb[...]), 0.0)

    return pl.pallas_call(
        mm_body,
        grid=(pl.cdiv(PROJ_W, tn), T // tm),
        in_specs=[pl.BlockSpec((tm, D), lambda n, m: (m, 0)),
                  pl.BlockSpec((None, tn, D), lambda n, m: (l, n, 0))],
        out_specs=pl.BlockSpec((tm, tn), lambda n, m: (m, n)),
        out_shape=jax.ShapeDtypeStruct((T, PROJ_W), f32),
        scratch_shapes=[pltpu.VMEM((tn, D), bf16)],
        compiler_params=_cparams(("parallel", "arbitrary")),
    )(h, w_in_t)


def _glu_proj(yg, l, w_glu):
    T = yg.shape[0]
    tm, tn = _row_tile(T), 512
    nb = S5_W // tn

    def post(prods, e_refs):
        return (prods[0] * jax.nn.sigmoid(prods[1]),)

    (out,) = _fused_mm(
        [(yg, pl.BlockSpec((tm, S5_W), lambda m, n: (m, 0)))],
        [(w_glu, pl.BlockSpec((None, S5_W, tn), lambda m, n: (l, 0, n)), 0),
         (w_glu, pl.BlockSpec((None, S5_W, tn), lambda m, n: (l, 0, nb + n)), 0)],
        [], None, post,
        grid=(T // tm, nb),
        out_specs=[pl.BlockSpec((tm, tn), lambda m, n: (m, n))],
        out_shape=[jax.ShapeDtypeStruct((T, S5_W), bf16)],
        cache_shapes=[],
    )
    return out


def _merge_proj(y_ret, y_rw, y_s5, l, w_ret, w_rw, w_s5, proj):
    T = y_ret.shape[0]
    tm, tn = _row_tile(T), 512
    lead = OFF_GATE % LANE
    base = OFF_GATE - lead

    def gate_spec(i):
        return pl.BlockSpec((pl.Element(tm), pl.Element(tn + LANE)),
                            lambda m, n: (pl.multiple_of(m * tm, tm), pl.multiple_of(base + i * D + n * tn, LANE)))

    def post(prods, e_refs):
        acc = None
        for p, e in zip(prods, e_refs):
            t = jax.nn.sigmoid(e[:, lead:lead + tn]) * p
            acc = t if acc is None else acc + t
        return (acc,)

    xspec = pl.BlockSpec((tm, RET_W), lambda m, n: (m, 0))
    wspec = pl.BlockSpec((None, RET_W, tn), lambda m, n: (l, 0, n))
    (out,) = _fused_mm(
        [(y_ret, xspec), (y_rw, xspec), (y_s5, xspec)],
        [(w_ret, wspec, 0), (w_rw, wspec, 1), (w_s5, wspec, 2)],
        [(proj, gate_spec(0)), (proj, gate_spec(1)), (proj, gate_spec(2))],
        None, post,
        grid=(T // tm, D // tn),
        out_specs=[pl.BlockSpec((tm, tn), lambda m, n: (m, n))],
        out_shape=[jax.ShapeDtypeStruct((T, D), bf16)],
        cache_shapes=[],
    )
    return out


def _out_proj(merged, l, w_out, x, mod):
    T = x.shape[0]
    tm, tn = mod.tm, 512

    def post(prods, e_refs):
        x_ref, m_ref = e_refs
        return (x_ref[...] + m_ref[...] * prods[0],)

    (out,) = _fused_mm(
        [(merged, pl.BlockSpec((tm, D), lambda m, n: (m, 0)))],
        [(w_out, pl.BlockSpec((None, D, tn), lambda m, n: (l, 0, n)), 0)],
        [(x, pl.BlockSpec((tm, tn), lambda m, n: (m, n))), (mod.arr, mod.spec(2, tn, lambda n: n))],
        None, post,
        grid=(T // tm, D // tn),
        out_specs=[pl.BlockSpec((tm, tn), lambda m, n: (m, n))],
        out_shape=[jax.ShapeDtypeStruct((T, D), f32)],
        cache_shapes=[],
    )
    return out


def _ret_consts(L, pos0):
    C = RET_CHUNK if L % RET_CHUNK == 0 else L
    H = RET_H
    log_g = jnp.log1p(-jnp.exp2(-5.0 - jnp.arange(H, dtype=f32)))
    i = jnp.arange(C, dtype=f32)
    diff = i[:, None] - i[None, :]
    causal = diff >= 0
    dmask = jnp.where(causal, jnp.exp(jnp.where(causal, diff, 0.0)[None] * log_g[:, None, None]), 0.0)
    kdec = jnp.exp((C - 1.0 - i)[:, None] * log_g[None, :])
    qdec = jnp.exp((i + 1.0)[:, None] * log_g[None, :])
    g_chunk = jnp.exp(C * log_g)
    half = RET_DK // 2
    inv = ROPE_BASE ** (-jnp.arange(half, dtype=f32) / half)
    pos = pos0 + jnp.arange(L, dtype=f32)
    ang = pos[:, None] * inv[None, :]
    return C, dmask, kdec, qdec, g_chunk, jnp.cos(ang), jnp.sin(ang)


def _rotary(x, cos, sin):
    half = RET_DK // 2
    x1, x2 = x[..., :half], x[..., half:]
    return jnp.concatenate([x1 * cos - x2 * sin, x1 * sin + x2 * cos], axis=-1)


def _retention_seq(proj, B, L):
    C, dmask, kdec, qdec, g_chunk, cos, sin = _ret_consts(L, 0.0)
    H, dk = RET_H, RET_DK
    n = L // C
    kdec_f = jnp.broadcast_to(kdec.T[:, :, None], (H, C, dk))
    qdec_f = jnp.broadcast_to(qdec.T[:, :, None], (H, C, dk))
    gch_f = jnp.broadcast_to(g_chunk[:, None, None], (H, 8, dk))
    proj3 = proj.reshape(B, L, PROJ_W)

    def body(q_ref, k_ref, v_ref, g_ref, cos_ref, sin_ref, dm_ref, kd_ref, qd_ref, gc_ref, y_ref, s_ref, st):
        c = pl.program_id(1)

        @pl.when(c == 0)
        def _():
            st[...] = jnp.zeros_like(st)

        cs, sn = cos_ref[...], sin_ref[...]
        hs = range(H)
        sl = [slice(h * dk, (h + 1) * dk) for h in hs]
        q = [_rotary(q_ref[:, sl[h]], cs, sn) for h in hs]
        k = [_rotary(k_ref[:, sl[h]], cs, sn) * (dk ** -0.5) for h in hs]
        vb = [v_ref[:, sl[h]].astype(bf16) for h in hs]
        s0 = [st[h] for h in hs]
        scores = [_dot_nt(q[h].astype(bf16), k[h].astype(bf16)) * dm_ref[h] for h in hs]
        cross = [_dot((q[h] * qd_ref[h]).astype(bf16), s0[h].astype(bf16)) for h in hs]
        kv = [_dot_tn((k[h] * kd_ref[h]).astype(bf16), vb[h]) for h in hs]
        o = [_dot(scores[h].astype(bf16), vb[h]) + cross[h] for h in hs]
        for h in hs:
            st[h] = s0[h] * gc_ref[h, 0:1, :] + kv[h]
            g = g_ref[:, sl[h]]
            y_ref[:, sl[h]] = (g * jax.nn.sigmoid(g) * _head_norm(o[h], RET_GN_EPS)).astype(bf16)

        @pl.when(c == n - 1)
        def _():
            s_ref[...] = st[...]

    def seg(off):
        return pl.BlockSpec((None, C, RET_W), lambda b, c: (b, c, off // RET_W))

    const3 = lambda shp: pl.BlockSpec(shp, lambda b, c: (0, 0, 0))
    y, s = pl.pallas_call(
        body,
        grid=(B, n),
        in_specs=[seg(OFF_Q), seg(OFF_K), seg(OFF_V), seg(OFF_G),
                  pl.BlockSpec((C, dk // 2), lambda b, c: (c, 0)),
                  pl.BlockSpec((C, dk // 2), lambda b, c: (c, 0)),
                  const3((H, C, C)), const3((H, C, dk)), const3((H, C, dk)), const3((H, 8, dk))],
        out_specs=[pl.BlockSpec((None, C, RET_W), lambda b, c: (b, c, 0)),
                   pl.BlockSpec((None, H, dk, RET_DV), lambda b, c: (b, 0, 0, 0))],
        out_shape=[jax.ShapeDtypeStruct((B, L, RET_W), bf16),
                   jax.ShapeDtypeStruct((B, H, dk, RET_DV), f32)],
        scratch_shapes=[pltpu.VMEM((H, dk, RET_DV), f32)],
        compiler_params=_cparams(("parallel", "arbitrary")),
    )(proj3, proj3, proj3, proj3, cos, sin, dmask, kdec_f, qdec_f, gch_f)
    return y.reshape(B * L, RET_W), s


STEP_TB = 16


def _layer_grid(l, buf, inner):
    if buf is not None:
        return inner, (lambda fn: (lambda *ix: fn(l, ix, ix)))
    assert l == 0
    last = tuple(n - 1 for n in inner)

    def wrap(fn):
        def index_map(d, *ix):
            parked = tuple(jnp.where(d == l, i, z) for i, z in zip(ix, last))
            return fn(d, ix, parked)
        return index_map
    return (DEPTH,) + inner, wrap


def _retention_step(proj, s_all, l, buf, pos0):
    B = proj.shape[0]
    _, _, _, _, g_chunk, cos, sin = _ret_consts(1, pos0)
    H, dk = RET_H, RET_DK
    gch = jnp.broadcast_to(g_chunk[:, None, None], (H, 8, dk))
    tb = STEP_TB
    grid, wrap = _layer_grid(l, buf, (B // tb, H))

    def body(q_ref, k_ref, v_ref, g_ref, cos_ref, sin_ref, gc_ref, s_ref, *rest):
        y_ref, so_ref = rest[-2:]

        def update():
            cs, sn = cos_ref[...], sin_ref[...]
            q = _rotary(q_ref[...], cs, sn)
            k = _rotary(k_ref[...], cs, sn) * (dk ** -0.5)
            v = v_ref[...]
            s1 = s_ref[...] * gc_ref[0:1, :][None] + k[:, :, None] * v[:, None, :]
            so_ref[...] = s1
            o = jnp.sum(q[:, :, None] * s1, axis=1)
            g = g_ref[...]
            y_ref[...] = (g * jax.nn.sigmoid(g) * _head_norm(o, RET_GN_EPS)).astype(bf16)

        if buf is not None:
            update()
        else:
            pl.when(pl.program_id(0) == l)(update)

            @pl.when(pl.program_id(0) != l)
            def _():
                so_ref[...] = jnp.zeros_like(so_ref)

    def seg(off):
        return pl.BlockSpec((tb, dk), wrap(lambda d, ix, pk: (pk[0], off // dk + pk[1])))

    const = lambda d, ix, pk: (0, 0)
    in_specs = [seg(OFF_Q), seg(OFF_K), seg(OFF_V), seg(OFF_G),
                pl.BlockSpec((1, dk // 2), wrap(const)), pl.BlockSpec((1, dk // 2), wrap(const)),
                pl.BlockSpec((None, 8, dk), wrap(lambda d, ix, pk: (pk[1], 0, 0))),
                pl.BlockSpec((None, tb, None, dk, RET_DV), wrap(lambda d, ix, pk: (l, pk[0], pk[1], 0, 0)))]
    args = [proj, proj, proj, proj, cos, sin, gch, s_all]
    aliases = {}
    if buf is not None:
        in_specs.append(pl.BlockSpec(memory_space=pl.ANY))
        args.append(buf)
        aliases = {len(args) - 1: 1}
    return pl.pallas_call(
        body,
        grid=grid,
        in_specs=in_specs,
        out_specs=[pl.BlockSpec((tb, dk), wrap(lambda d, ix, pk: pk)),
                   pl.BlockSpec((None, tb, None, dk, RET_DV), wrap(lambda d, ix, pk: (d, ix[0], ix[1], 0, 0)))],
        out_shape=[jax.ShapeDtypeStruct((B, RET_W), bf16),
                   jax.ShapeDtypeStruct((DEPTH, B, H, dk, RET_DV), f32)],
        input_output_aliases=aliases,
        compiler_params=_cparams(("arbitrary",) * len(grid)),
    )(*args)


RW_C = 64
RW_Q = 4
RW_NQ = RWKV_H // RW_Q
RW_GROUP = 8
RW_BLK =((RWKV_PROJ + LANE - 1) // LANE) * LANE
_RW_PKEYS = ('mu', 'w0', 'w2', 'a0', 'a2', 'g2', 'k_k', 'k_a', 'r_k', 'ln_w', 'ln_b')


def _rwkv_params(p):
    return dict(
        mu=jnp.pad(p['rwkv_mu'], (0, RW_BLK - RWKV_PROJ)).reshape(1, RW_BLK),
        w0=p['rwkv_w0'].reshape(1, RWKV_W), w2=p['rwkv_w2'],
        a0=p['rwkv_a0'].reshape(1, RWKV_W), a2=p['rwkv_a2'], g2=p['rwkv_g2'],
        k_k=p['rwkv_k_k'].reshape(1, RWKV_W), k_a=p['rwkv_k_a'].reshape(1, RWKV_W),
        r_k=p['rwkv_r_k'].reshape(1, RWKV_W),
        ln_w=p['rwkv_ln_w'].reshape(1, RWKV_W), ln_b=p['rwkv_ln_b'].reshape(1, RWKV_W))


def _split_bf16(x, terms):
    out = []
    for _ in range(terms - 1):
        hi = x.astype(bf16)
        out.append(hi)
        x = x - hi.astype(f32)
    out.append(x.astype(bf16))
    return out


def _head_sum(x):
    QW = RW_Q * RWKV_N
    r = lax.broadcasted_iota(i32, (QW, QW), 0) // RWKV_N
    c = lax.broadcasted_iota(i32, (QW, QW), 1) // RWKV_N
    ones = (r == c).astype(bf16)
    parts = _split_bf16(x, 2)
    outs = []
    for q in range(RW_NQ):
        sl = slice(q * QW, (q + 1) * QW)
        outs.append(_dot(parts[0][:, sl], ones) + _dot(parts[1][:, sl], ones))
    return jnp.concatenate(outs, axis=1)


def _rwkv_mix(rw, prev, pr):
    m = rw + (prev - rw) * pr['mu'][...]
    r = m[:, 0:RWKV_W]
    k = m[:, RWKV_W:2 * RWKV_W]
    v = m[:, 2 * RWKV_W:3 * RWKV_W]
    o = 3 * RWKV_W
    xw = m[:, o:o + DECAY_LORA]
    xa = m[:, o + DECAY_LORA:o + DECAY_LORA + AAA_LORA]
    xg = m[:, o + DECAY_LORA + AAA_LORA:o + DECAY_LORA + AAA_LORA + GATE_LORA]
    w_log = -jax.nn.softplus(-(pr['w0'][...] + _dot(jnp.tanh(xw).astype(bf16), pr['w2'][...].astype(bf16)))) - 0.5
    lw = -jnp.exp(w_log)
    a = jax.nn.sigmoid(pr['a0'][...] + _dot(xa.astype(bf16), pr['a2'][...].astype(bf16)))
    g = _dot(jax.nn.sigmoid(xg).astype(bf16), pr['g2'][...].astype(bf16))
    kk = k * pr['k_k'][...]
    kk = kk / jnp.maximum(jnp.sqrt(_head_sum(kk * kk)), 1e-12)
    kf = k * (1.0 + (a - 1.0) * pr['k_a'][...])
    return r, lw, kf, v, kk, a, g


def _rwkv_out(y, r, kf, v, g, pr):
    yc = y - _head_sum(y) * (1.0 / RWKV_N)
    yn = yc * lax.rsqrt(_head_sum(yc * yc) * (1.0 / RWKV_N) + RWKV_LN_EPS)
    yn = yn * pr['ln_w'][...] + pr['ln_b'][...]
    yn = yn + _head_sum(r * kf * pr['r_k'][...]) * v
    return yn * g


def _rwkv_seq(proj, rp, B, L):
    C = RW_C
    assert L % C == 0 and C == RWKV_N
    n = L // C
    QW = RW_Q * RWKV_N
    BC = B * C
    proj3 = proj.reshape(B, L, PROJ_W)

    def body(rw_ref, *refs):
        pr = dict(zip(_RW_PKEYS, refs[:len(_RW_PKEYS)]))
        y_ref, s_ref, sh_ref, st, carry = refs[len(_RW_PKEYS):]
        c = pl.program_id(0)

        @pl.when(c == 0)
        def _():
            st[...] = jnp.zeros_like(st)
            carry[...] = jnp.zeros_like(carry)

        rw = rw_ref[...].reshape(BC, RW_BLK)
        rolled = pltpu.roll(rw, 1, 0)
        row = lax.broadcasted_iota(i32, (C, RW_BLK), 0)
        prev = jnp.concatenate(
            [jnp.where(row == 0, carry[b, 0:1, :], rolled[b * C:(b + 1) * C]) for b in range(B)], axis=0)
        for b in range(B):
            carry[b, 0:1, :] = rw[(b + 1) * C - 1:(b + 1) * C, :]
        r, lw, kf, v, kk, a, g = _rwkv_mix(rw, prev, pr)

        ti = lax.broadcasted_iota(i32, (BC, BC), 0)
        si = lax.broadcasted_iota(i32, (BC, BC), 1)
        tril = ((si <= ti) & ((si // C) == (ti // C))).astype(bf16)
        lg = sum(_dot(tril, part) for part in _split_bf16(lw, 3))
        lgc = jnp.concatenate(
            [jnp.broadcast_to(lg[(b + 1) * C - 1:(b + 1) * C, :], (C, RWKV_W)) for b in range(B)], axis=0)
        e_neg = jnp.exp(-lg)
        e_rem = jnp.exp(lgc - lg)
        at = kk * jnp.exp(lg - lw)
        ka = kk * a
        bt = ka * e_neg
        kt = kf * e_neg
        rt = r * jnp.exp(lg)
        bh = ka * e_rem
        kh = kf * e_rem
        gcr = jnp.exp(lgc)

        rr = lax.broadcasted_iota(i32, (RW_Q * C, QW), 0)
        ll = lax.broadcasted_iota(i32, (RW_Q * C, QW), 1)
        blockmask = (rr // C) == (ll // RWKV_N)
        tt = lax.broadcasted_iota(i32, (C, QW), 0)
        ss = lax.broadcasted_iota(i32, (C, QW), 1) % C
        strict = ss < tt
        incl = ss <= tt
        eye = (ss == tt).astype(f32)

        def bd(x):
            return jnp.where(blockmask, jnp.concatenate([x] * RW_Q, axis=0), 0.0).astype(bf16)

        ys = [[None] * RW_NQ for _ in range(B)]
        chains = [(b, q) for b in range(B) for q in range(RW_NQ)]
        for g0 in range(0, len(chains), RW_GROUP):
            grp = chains[g0:g0 + RW_GROUP]
            idx = [(slice(b * C, (b + 1) * C), slice(q * QW, (q + 1) * QW)) for b, q in grp]
            each = lambda fn: [fn(i) for i in range(len(grp))]
            vq = each(lambda i: v[idx[i]])
            ar = each(lambda i: jnp.concatenate([at[idx[i]], rt[idx[i]]], axis=0).astype(bf16))
            big = each(lambda i: _dot_nt(ar[i], jnp.concatenate([bd(bt[idx[i]]), bd(kt[idx[i]])], axis=0)))
            s0 = each(lambda i: st[grp[i]])
            asrs = each(lambda i: _dot_nt(ar[i], s0[i].astype(bf16)))
            nmat = each(lambda i: jnp.where(strict, big[i][:C, :QW], 0.0))
            akm = each(lambda i: jnp.where(strict, big[i][:C, QW:], 0.0).astype(bf16))
            rbk = each(lambda i: jnp.concatenate([jnp.where(incl, big[i][C:, :QW], 0.0),
                                                  jnp.where(incl, big[i][C:, QW:], 0.0)], axis=1).astype(bf16))
            tm = each(lambda i: eye - nmat[i])
            pw = each(lambda i: _dot(nmat[i].astype(bf16), bd(nmat[i])))
            lvl = 2
            while lvl < C:
                res = each(lambda i: _dot(jnp.concatenate([tm[i], pw[i]], axis=0).astype(bf16), bd(pw[i])))
                tm = each(lambda i: tm[i] + res[i][:C])
                pw = each(lambda i: res[i][C:])
                lvl *= 2
            vbd = each(lambda i: bd(vq[i]))
            rhs = each(lambda i: -(asrs[i][:C] + _dot(akm[i], vbd[i])))
            u = each(lambda i: _dot(tm[i].astype(bf16), bd(rhs[i])))
            y = each(lambda i: asrs[i][C:] + _dot(rbk[i], jnp.concatenate([bd(u[i]), vbd[i]], axis=0)))
            upd = each(lambda i: _dot_tn(jnp.concatenate([u[i], vq[i]], axis=0).astype(bf16),
                                         jnp.concatenate([bh[idx[i]], kh[idx[i]]], axis=0).astype(bf16)))
            for i, (b, q) in enumerate(grp):
                st[b, q] = s0[i] * gcr[b * C:b * C + 1, idx[i][1]] + jnp.where(blockmask, upd[i], 0.0)
                ys[b][q] = y[i]

        y = jnp.concatenate([jnp.concatenate(yb, axis=1) for yb in ys], axis=0)
        y_ref[...] = _rwkv_out(y, r, kf, v, g, pr).astype(bf16).reshape(B, C, RWKV_W)

        @pl.when(c == n - 1)
        def _():
            for b in range(B):
                for q in range(RW_NQ):
                    for h in range(RW_Q):
                        hs = slice(h * RWKV_N, (h + 1) * RWKV_N)
                        s_ref[b, q * RW_Q + h] = st[b, q, hs, hs]
                sh_ref[b] = rw[(b + 1) * C - 1:(b + 1) * C, 0:RWKV_PROJ]

    pspecs = [pl.BlockSpec(rp[k].shape, lambda c: (0, 0)) for k in _RW_PKEYS]
    y, s, sh = pl.pallas_call(
        body,
        grid=(n,),
        in_specs=[pl.BlockSpec((pl.Element(B), pl.Element(C), pl.Element(RW_BLK)),
                               lambda c: (0, pl.multiple_of(c * C, C), OFF_RW))] + pspecs,
        out_specs=[pl.BlockSpec((B, C, RWKV_W), lambda c: (0, c, 0)),
                   pl.BlockSpec((B, RWKV_H, RWKV_N, RWKV_N), lambda c: (0, 0, 0, 0)),
                   pl.BlockSpec((B, 1, RWKV_PROJ), lambda c: (0, 0, 0))],
        out_shape=[jax.ShapeDtypeStruct((B, L, RWKV_W), bf16),
                   jax.ShapeDtypeStruct((B, RWKV_H, RWKV_N, RWKV_N), f32),
                   jax.ShapeDtypeStruct((B, 1, RWKV_PROJ), f32)],
        scratch_shapes=[pltpu.VMEM((B, RW_NQ, QW, QW), f32), pltpu.VMEM((B, 8, RW_BLK), f32)],
        compiler_params=_cparams(("arbitrary",)),
    )(proj3, *[rp[k] for k in _RW_PKEYS])
    return y.reshape(B * L, RWKV_W), s, sh.reshape(B, RWKV_PROJ)


def _rwkv_step(proj, shift, s_t, l, buf, rp):
    B = proj.shape[0]
    N = RWKV_N
    shift_p = jnp.pad(shift, ((0, 0), (0, RW_BLK - RWKV_PROJ)))
    grid, wrap = _layer_grid(l, buf, (RWKV_H,))
    npk = len(_RW_PKEYS)
    vec_names = ('r', 'w', 'kf', 'v', 'kk', 'ka')

    def body(rw_ref, sh_ref, s_ref, *refs):
        pr = dict(zip(_RW_PKEYS, refs[:npk]))
        y_ref, so_ref = refs[-9:-7]
        vt = dict(zip(vec_names, refs[-7:-1]))
        yt = refs[-1]
        h = pl.program_id(len(grid) - 1)

        def update():
            @pl.when(h == 0)
            def _():
                r, lw, kf, v, kk, a, _ = _rwkv_mix(rw_ref[...], sh_ref[...], pr)
                for name, val in zip(vec_names, (r, jnp.exp(lw), kf, v, kk, kk * a)):
                    vt[name][...] = val.T

            hs = pl.ds(pl.multiple_of(h * N, N), N)
            s = s_ref[...]
            kk_h = vt['kk'][hs, :]
            sa = jnp.sum(s * (-kk_h)[None], axis=1, keepdims=True)
            s1 = s * vt['w'][hs, :][None] + sa * vt['ka'][hs, :][None] + vt['v'][hs, :][:, None, :] * vt['kf'][hs, :][None]
            so_ref[...] = s1
            yt[hs, :] = jnp.sum(s1 * vt['r'][hs, :][None], axis=1)

            @pl.when(h == RWKV_H - 1)
            def _():
                r, lw, kf, v, kk, a, g = _rwkv_mix(rw_ref[...], sh_ref[...], pr)
                y_ref[...] = _rwkv_out(yt[...].T, r, kf, v, g, pr).astype(bf16)

        if buf is not None:
            update()
        else:
            pl.when(pl.program_id(0) == l)(update)

            @pl.when(pl.program_id(0) != l)
            def _():
                so_ref[...] = jnp.zeros_like(so_ref)

    const = lambda d, ix, pk: (0, 0)
    in_specs = [pl.BlockSpec((pl.Element(B), pl.Element(RW_BLK)), wrap(lambda d, ix, pk: (0, OFF_RW))),
                pl.BlockSpec((B, RW_BLK), wrap(const)),
                pl.BlockSpec((None, None, N, N, B), wrap(lambda d, ix, pk: (l, pk[0], 0, 0, 0)))]
    in_specs += [pl.BlockSpec(rp[k].shape, wrap(const)) for k in _RW_PKEYS]
    args = [proj, shift_p, s_t] + [rp[k] for k in _RW_PKEYS]
    aliases = {}
    if buf is not None:
        in_specs.append(pl.BlockSpec(memory_space=pl.ANY))
        args.append(buf)
        aliases = {len(args) - 1: 1}
    return pl.pallas_call(
        body,
        grid=grid,
        in_specs=in_specs,
        out_specs=[pl.BlockSpec((B, RWKV_W), wrap(const)),
                   pl.BlockSpec((None, None, N, N, B), wrap(lambda d, ix, pk: (d, ix[0], 0, 0, 0)))],
        out_shape=[jax.ShapeDtypeStruct((B, RWKV_W), bf16),
                   jax.ShapeDtypeStruct((DEPTH, RWKV_H, N, N, B), f32)],
        scratch_shapes=[pltpu.VMEM((RWKV_W, B), f32)] * 7,
        input_output_aliases=aliases,
        compiler_params=_cparams(("arbitrary",) * len(grid)),
    )(*args)


S5_N = S5_G * S5_P
S5_KC = 256
S5_NKC = S5_W // S5_KC
S5_TILES = S5_N // LANE
S5_GB = S5_KC // S5_GC


def _s5_params(p):
    a_re, a_im = p['s5_a_re'], p['s5_a_im']
    dstep = jnp.exp(p['s5_log_dt'])[:, None]
    mag = jnp.exp(a_re * dstep)
    ab_re = mag * jnp.cos(a_im * dstep)
    ab_im = mag * jnp.sin(a_im * dstep)
    den = a_re * a_re + a_im * a_im
    n_re = ab_re - 1.0
    f_re = (n_re * a_re + ab_im * a_im) / den
    f_im = (ab_im * a_re - n_re * a_im) / den
    b_re, b_im = p['s5_b_re'], p['s5_b_im']
    bb_re = f_re[..., None] * b_re - f_im[..., None] * b_im
    bb_im = f_re[..., None] * b_im + f_im[..., None] * b_re
    eye = jnp.eye(S5_GB, dtype=f32)

    def in_map(bb):
        t = bb.reshape(S5_NKC, S5_GB, S5_P, S5_GC)
        return jnp.einsum('kgpc,gh->kgchp', t, eye).reshape(S5_NKC, S5_KC, S5_GB * S5_P)

    def out_map(cc):
        t = cc.reshape(S5_NKC, S5_GB, S5_GC, S5_P)
        return jnp.einsum('qgcp,gh->qgphc', t, eye).reshape(S5_NKC, S5_GB * S5_P, S5_KC)

    return dict(wb=jnp.concatenate([in_map(bb_re), in_map(bb_im)], axis=-1),
                wc_re=out_map(p['s5_c_re']), wc_im=out_map(p['s5_c_im']),
                ab_re_t=ab_re.reshape(S5_TILES // 8, 8, LANE), ab_im_t=ab_im.reshape(S5_TILES // 8, 8, LANE),
                ab_re=ab_re.reshape(1, S5_N), ab_im=ab_im.reshape(1, S5_N), d=p['s5_d'].reshape(1, S5_W))


def _s5_seq(proj, sp, B, L):
    Lc = min(L, 256)
    n = L // Lc
    pitch = Lc + 4
    lead = OFF_U % LANE
    base = OFF_U - lead
    width = S5_W + LANE
    nt4 = S5_TILES // 8
    tiles_kc = S5_TILES // S5_NKC

    def body(u_ref, wb_ref, wcr_ref, wci_ref, abr_ref, abi_ref, d_ref, y_ref, sr_ref, si_ref, xr, xi, cr, ci):
        c = pl.program_id(1)

        @pl.when(c == 0)
        def _():
            cr[...] = jnp.zeros_like(cr)
            ci[...] = jnp.zeros_like(ci)

        u = u_ref[:, lead:lead + S5_W]
        ub = u.astype(bf16)
        for kc in range(S5_NKC):
            bu = _dot(ub[:, kc * S5_KC:(kc + 1) * S5_KC], wb_ref[kc].astype(bf16))
            for j in range(tiles_kc):
                t = kc * tiles_kc + j
                xr[pl.ds(t * pitch, Lc), :] = bu[:, j * LANE:(j + 1) * LANE]
                xi[pl.ds(t * pitch, Lc), :] = bu[:, (tiles_kc + j) * LANE:(tiles_kc + j + 1) * LANE]

        abr = [abr_ref[g] for g in range(nt4)]
        abi = [abi_ref[g] for g in range(nt4)]

        def step(t, carry):
            out = []
            for g in range(nt4):
                s_r, s_i = carry[2 * g], carry[2 * g + 1]
                idx = pl.ds(g * 8 * pitch + t, 8, stride=pitch)
                n_r = abr[g] * s_r - abi[g] * s_i + xr[idx, :]
                n_i = abr[g] * s_i + abi[g] * s_r + xi[idx, :]
                xr[idx, :] = n_r
                xi[idx, :] = n_i
                out += [n_r, n_i]
            return tuple(out)

        init = []
        for g in range(nt4):
            init += [cr[g], ci[g]]
        fin = lax.fori_loop(0, Lc, step, tuple(init), unroll=2)
        for g in range(nt4):
            cr[g] = fin[2 * g]
            ci[g] = fin[2 * g + 1]

        for q in range(S5_NKC):
            lr = jnp.concatenate([xr[pl.ds((q * tiles_kc + j) * pitch, Lc), :] for j in range(tiles_kc)], axis=1)
            li = jnp.concatenate([xi[pl.ds((q * tiles_kc + j) * pitch, Lc), :] for j in range(tiles_kc)], axis=1)
            y = _dot(lr.astype(bf16), wcr_ref[q].astype(bf16)) - _dot(li.astype(bf16), wci_ref[q].astype(bf16))
            cs = slice(q * S5_KC, (q + 1) * S5_KC)
            y = y + d_ref[:, cs] * u[:, cs]
            y_ref[:, cs] = jax.nn.gelu(y).astype(bf16)

        @pl.when(c == n - 1)
        def _():
            sr_ref[...] = cr[...]
            si_ref[...] = ci[...]

    full = lambda shp: pl.BlockSpec(shp, lambda b, c: (0,) * len(shp))
    y, sr, si = pl.pallas_call(
        body,
        grid=(B, n),
        in_specs=[pl.BlockSpec((pl.Element(Lc), pl.Element(width)),
                               lambda b, c: (pl.multiple_of((b * n + c) * Lc, Lc), base)),
                  full(sp['wb'].shape), full(sp['wc_re'].shape), full(sp['wc_im'].shape),
                  full(sp['ab_re_t'].shape), full(sp['ab_im_t'].shape), full((1, S5_W))],
        out_specs=[pl.BlockSpec((Lc, S5_W), lambda b, c: (b * n + c, 0)),
                   pl.BlockSpec((None, nt4, 8, LANE), lambda b, c: (b, 0, 0, 0)),
                   pl.BlockSpec((None, nt4, 8, LANE), lambda b, c: (b, 0, 0, 0))],
        out_shape=[jax.ShapeDtypeStruct((B * L, S5_W), bf16),
                   jax.ShapeDtypeStruct((B, nt4, 8, LANE), f32),
                   jax.ShapeDtypeStruct((B, nt4, 8, LANE), f32)],
        scratch_shapes=[pltpu.VMEM((S5_TILES * pitch, LANE), f32), pltpu.VMEM((S5_TILES * pitch, LANE), f32),
                        pltpu.VMEM((nt4, 8, LANE), f32), pltpu.VMEM((nt4, 8, LANE), f32)],
        compiler_params=_cparams(("parallel", "arbitrary")),
    )(proj, sp['wb'], sp['wc_re'], sp['wc_im'], sp['ab_re_t'], sp['ab_im_t'], sp['d'])
    return y, sr.reshape(B, S5_G, S5_P), si.reshape(B, S5_G, S5_P)


def _s5_step(proj, x_re, x_im, sp):
    B = proj.shape[0]
    lead = OFF_U % LANE
    base = OFF_U - lead
    width = S5_W + LANE
    kw = S5_N // S5_NKC

    def body(u_ref, xr_ref, xi_ref, wb_ref, wcr_ref, wci_ref, abr_ref, abi_ref, d_ref, y_ref, sr_ref, si_ref):
        u = u_ref[:, lead:lead + S5_W]
        ub = u.astype(bf16)
        abr, abi = abr_ref[...], abi_ref[...]
        xr, xi = xr_ref[...], xi_ref[...]
        for kc in range(S5_NKC):
            bu = _dot(ub[:, kc * S5_KC:(kc + 1) * S5_KC], wb_ref[kc].astype(bf16))
            sl = slice(kc * kw, (kc + 1) * kw)
            n_r = abr[:, sl] * xr[:, sl] - abi[:, sl] * xi[:, sl] + bu[:, :kw]
            n_i = abr[:, sl] * xi[:, sl] + abi[:, sl] * xr[:, sl] + bu[:, kw:]
            sr_ref[:, sl] = n_r
            si_ref[:, sl] = n_i
            y = _dot(n_r.astype(bf16), wcr_ref[kc].astype(bf16)) - _dot(n_i.astype(bf16), wci_ref[kc].astype(bf16))
            cs = slice(kc * S5_KC, (kc + 1) * S5_KC)
            y = y + d_ref[:, cs] * u[:, cs]
            y_ref[:, cs] = jax.nn.gelu(y).astype(bf16)

    full = lambda shp: pl.BlockSpec(shp, lambda i: (0,) * len(shp))
    y, sr, si = pl.pallas_call(
        body,
        grid=(1,),
        in_specs=[pl.BlockSpec((pl.Element(B), pl.Element(width)), lambda i: (0, base)),
                  full((B, S5_N)), full((B, S5_N)),
                  full(sp['wb'].shape), full(sp['wc_re'].shape), full(sp['wc_im'].shape),
                  full((1, S5_N)), full((1, S5_N)), full((1, S5_W))],
        out_specs=[full((B, S5_W)), full((B, S5_N)), full((B, S5_N))],
        out_shape=[jax.ShapeDtypeStruct((B, S5_W), bf16),
                   jax.ShapeDtypeStruct((B, S5_N), f32), jax.ShapeDtypeStruct((B, S5_N), f32)],
        compiler_params=_cparams(("arbitrary",)),
    )(proj, x_re.reshape(B, S5_N), x_im.reshape(B, S5_N), sp['wb'], sp['wc_re'], sp['wc_im'],
      sp['ab_re'], sp['ab_im'], sp['d'])
    return y, sr.reshape(B, S5_G, S5_P), si.reshape(B, S5_G, S5_P)


MOE_RT, MOE_RT_SMALL = 128, 32
SLAB = D // LANE
SUB = 8
SLAB_PITCH = SLAB + SUB
H2_PITCH = SLAB + SUB
O2_PITCH = TOP_K * SLAB + SUB


def _moe_rt(T):
    return MOE_RT if T * TOP_K >= N_EXPERTS * MOE_RT else MOE_RT_SMALL
ROUTE_W = LANE


def _router(x, g, mod, w_r, b_r, row0, t_all, bufs):
    T = x.shape[0]
    tm = min(mod.tm, 512)
    nm = T // tm
    assert row0 % tm == 0
    if bufs is None:
        assert row0 == 0
        n_steps = pl.cdiv(t_all, tm)
    else:
        n_steps = nm
    clamp = lambda m: jnp.minimum(m, nm - 1)

    def body(x_ref, g_ref, sh_ref, sc_ref, w_ref, b_ref, *rest):
        h_ref, e_ref, p_ref = rest[-3:]
        if n_steps > nm:
            @pl.when(pl.program_id(0) >= nm)
            def _():
                h_ref[...] = jnp.zeros_like(h_ref)
                e_ref[...] = jnp.zeros_like(e_ref)
                p_ref[...] = jnp.zeros_like(p_ref)

            pl.when(pl.program_id(0) < nm)(lambda: route(x_ref, g_ref, sh_ref, sc_ref, w_ref, b_ref, *rest[-3:]))
        else:
            route(x_ref, g_ref, sh_ref, sc_ref, w_ref, b_ref, *rest[-3:])

    def route(x_ref, g_ref, sh_ref, sc_ref, w_ref, b_ref, h_ref, e_ref, p_ref):
        h2 = _rms(x_ref[...], g_ref[...]) * (1.0 + sc_ref[...]) + sh_ref[...]
        for s in range(SLAB):
            h_ref[pl.ds(s, tm, stride=H2_PITCH), :] = h2[:, s * LANE:(s + 1) * LANE]
        for s in range(SLAB, H2_PITCH):
            h_ref[pl.ds(s, tm, stride=H2_PITCH), :] = jnp.zeros((tm, LANE), f32)
        logits = _dot(h2.astype(bf16), w_ref[...].astype(bf16)) + b_ref[...]
        lane = lax.broadcasted_iota(i32, (tm, ROUTE_W), 1)
        ninf = jnp.float32(-jnp.inf)
        gl = jnp.where(lane < N_GROUPS, logits, ninf)
        gm = jnp.max(gl, axis=-1, keepdims=True)
        g_p = 1.0 / jnp.sum(jnp.exp(gl - gm), axis=-1, keepdims=True)
        g_idx = jnp.min(jnp.where(gl == gm, lane, ROUTE_W), axis=-1, keepdims=True)
        valid = (lane >= N_GROUPS) & (lane < N_GROUPS + N_EXPERTS) & (((lane - N_GROUPS) // EPG) == g_idx)
        el = jnp.where(valid, logits, ninf)
        ee = jnp.exp(el - jnp.max(el, axis=-1, keepdims=True))
        prob = jnp.where(valid, ee / jnp.sum(ee, axis=-1, keepdims=True), -1.0)
        p1 = jnp.max(prob, axis=-1, keepdims=True)
        i1 = jnp.min(jnp.where(prob == p1, lane, ROUTE_W), axis=-1, keepdims=True)
        prob2 = jnp.where(lane == i1, -1.0, prob)
        p2 = jnp.max(prob2, axis=-1, keepdims=True)
        i2 = jnp.min(jnp.where(prob2 == p2, lane, ROUTE_W), axis=-1, keepdims=True)
        den = p1 + p2
        e_ref[...] = jnp.where(lane == 0, i1 - N_GROUPS, jnp.where(lane == 1, i2 - N_GROUPS, 0))
        p_ref[...] = jnp.where(lane == 0, g_p * p1 / den, jnp.where(lane == 1, g_p * p2 / den, 0.0))

    in_specs = [pl.BlockSpec((tm, D), lambda m: (clamp(m), 0)),
                pl.BlockSpec((1, D), lambda m: (0, 0)),
                mod.row_spec(3, tm, clamp), mod.row_spec(4, tm, clamp),
                pl.BlockSpec((D, ROUTE_W), lambda m: (0, 0)),
                pl.BlockSpec((1, ROUTE_W), lambda m: (0, 0))]
    args = [x, g.reshape(1, D), mod.arr, mod.arr, w_r, b_r]
    aliases = {}
    if bufs is not None:
        in_specs += [pl.BlockSpec(memory_space=pl.ANY)] * 3
        aliases = {len(args) + i: i for i in range(3)}
        args += list(bufs)
    b0 = row0 // tm
    return pl.pallas_call(
        body,
        grid=(n_steps,),
        in_specs=in_specs,
        out_specs=[pl.BlockSpec((tm * H2_PITCH, LANE), lambda m: (b0 + m, 0)),
                   pl.BlockSpec((tm, ROUTE_W), lambda m: (b0 + m, 0)),
                   pl.BlockSpec((tm, ROUTE_W), lambda m: (b0 + m, 0))],
        out_shape=[jax.ShapeDtypeStruct((t_all * H2_PITCH, LANE), f32), jax.ShapeDtypeStruct((t_all, ROUTE_W), i32),
                   jax.ShapeDtypeStruct((t_all, ROUTE_W), f32)],
        input_output_aliases=aliases,
        compiler_params=_cparams(("arbitrary",)),
    )(*args)


TOK_BITS = 14


def _moe_plan(eid, T):
    A = T * TOP_K
    RT = _moe_rt(T)
    nt = pl.cdiv(A, RT) + N_EXPERTS
    flat_e = eid[:, :TOP_K].reshape(-1)
    order = jnp.argsort(flat_e).astype(i32)
    counts = jnp.sum((flat_e[:, None] == jnp.arange(N_EXPERTS, dtype=i32)[None, :]).astype(i32), axis=0)
    start = jnp.cumsum(counts) - counts
    pcnt = (counts + RT - 1) // RT * RT
    pend = jnp.cumsum(pcnt)
    tile_e = jnp.minimum(jnp.sum((pend[None, :] <= (jnp.arange(nt, dtype=i32) * RT)[:, None]).astype(i32), axis=1),
                         N_EXPERTS - 1)
    off = (jnp.arange(nt, dtype=i32) * RT - (pend - pcnt)[tile_e])[:, None] + jnp.arange(RT, dtype=i32)[None, :]
    real = off < counts[tile_e][:, None]
    srt = start[tile_e][:, None] + jnp.minimum(off, counts[tile_e][:, None])
    a = order[jnp.minimum(srt, A - 1)]
    slot = jnp.arange(nt * RT, dtype=i32).reshape(nt, RT)
    tok = jnp.where(real, a // TOP_K, 0)
    dst = jnp.where(real, a, A + slot - srt)
    n_used = (pend[-1] // RT).astype(i32).reshape(1)
    eidx = jnp.arange(N_EXPERTS, dtype=i32)
    in_use = counts > 0
    rank = jnp.cumsum(in_use.astype(i32)) - 1
    later = (eidx[None, :] > eidx[:, None]) & in_use[None, :]
    nxt = jnp.min(jnp.where(later, eidx[None, :], N_EXPERTS), axis=1)
    nxt = jnp.where(nxt < N_EXPERTS, nxt, -1)
    tile_info = jnp.concatenate([tile_e, (rank % 2)[tile_e], nxt[tile_e]]).astype(i32)
    return (tok | (dst << TOK_BITS)).reshape(-1), tile_info, n_used


def _experts(h2, packed, tile_e, n_used, l, w1, w3, w2):
    T = h2.shape[0] // H2_PITCH
    assert TOP_K == 2 and T < (1 << TOK_BITS)
    RT = _moe_rt(T)
    PITCH = SLAB_PITCH
    nt = tile_e.shape[0] // 3
    prime_id = nt * RT
    out_rows = (prime_id + RT) // TOP_K
    RING = 4
    HC, OC = 128, 256
    n_hc, n_oc = D_EXPERT // HC, D // OC
    g_per, s_per = RT // n_hc, RT // n_oc

    def body(tile_ref, nused_ref, slot_ref, h_hbm, w1_hbm, w3_hbm, w2_hbm, o_hbm, *scratch):
        xs, os_ = scratch[:RING], scratch[RING:2 * RING]
        w1b, w3b, w2b, wf1, wf3, wf2, gsem, ssem, wsem = scratch[2 * RING:]
        x0 = xs[0]
        j = pl.program_id(0)
        n_used = nused_ref[0]
        active = j < n_used

        def gather_row(b, r, tok):
            src = pl.multiple_of(tok * H2_PITCH, SUB)
            return pltpu.make_async_copy(h_hbm.at[pl.ds(src, SLAB), :], xs[b].at[pl.ds(r * PITCH, SLAB), :],
                                         gsem.at[b])

        def scatter_row(b, r, d):
            dst = pl.multiple_of(lax.shift_right_logical(d, 1) * O2_PITCH + (d & (TOP_K - 1)) * SLAB, SUB)
            return pltpu.make_async_copy(os_[b].at[pl.ds(r * PITCH, SLAB), :], o_hbm.at[pl.ds(dst, SLAB), :],
                                         ssem.at[b])

        def wait_rows(sem):
            pltpu.make_async_copy(h_hbm.at[pl.ds(0, RT * SLAB), :], x0.at[pl.ds(0, RT * SLAB), :], sem).wait()

        tok_of = lambda s: s & ((1 << TOK_BITS) - 1)
        dst_of = lambda s: lax.shift_right_logical(s, TOK_BITS)
        last_tile = n_used - 1

        @pl.when(j == 0)
        def _():
            os_[RING - 1][...] = jnp.zeros_like(os_[RING - 1])
            for b in range(RING - 1):
                base = jnp.minimum(b, last_tile) * RT

                def one(r, _, b=b, base=base):
                    gather_row(b, r, tok_of(slot_ref[base + r])).start()
                    return 0
                lax.fori_loop(0, RT, one, 0)

        def fetch(e, s):
            return [pltpu.make_async_copy(w_hbm.at[l, e], wf.at[s], wsem.at[s])
                    for w_hbm, wf in ((w1_hbm, wf1), (w3_hbm, wf3), (w2_hbm, wf2))]

        @pl.when(j == 0)
        def _():
            for c in fetch(tile_ref[0], 0):
                c.start()

        @pl.when(active & ((j == 0) | (tile_ref[j] != tile_ref[jnp.maximum(j - 1, 0)])))
        def _():
            ws, e_nxt = tile_ref[nt + j], tile_ref[2 * nt + j]
            for c in fetch(tile_ref[j], ws):
                c.wait()
            w1b[...] = wf1[ws].astype(bf16)
            w3b[...] = wf3[ws].astype(bf16)
            w2b[...] = wf2[ws].astype(bf16)

            @pl.when(e_nxt >= 0)
            def _():
                for c in fetch(e_nxt, 1 - ws):
                    c.start()

        def tile(cur):
            nxt_b = prv_b = (cur + RING - 1) % RING

            @pl.when(j >= RING - 1)
            def _():
                wait_rows(ssem.at[cur])

            wait_rows(gsem.at[cur])
            xb = jnp.concatenate([xs[cur][pl.ds(s, RT, stride=PITCH), :] for s in range(SLAB)],
                                 axis=1).astype(bf16)
            nxt = jnp.minimum(j + RING - 1, last_tile) * RT
            prv = jnp.maximum(j - 1, 0) * RT
            parts = []
            for c in range(n_hc):
                h1 = _dot(xb, w1b[:, c * HC:(c + 1) * HC])
                h3 = _dot(xb, w3b[:, c * HC:(c + 1) * HC])
                parts.append(((h1 * jax.nn.sigmoid(h1)) * h3).astype(bf16))
                for r in range(c * g_per, (c + 1) * g_per):
                    gather_row(nxt_b, r, tok_of(slot_ref[nxt + r])).start(priority=r % 2)
            hm = jnp.concatenate(parts, axis=1)
            for c in range(n_oc):
                res = _dot(hm, w2b[:, c * OC:(c + 1) * OC])
                for t in range(OC // LANE):
                    os_[cur][pl.ds(c * (OC // LANE) + t, RT, stride=PITCH), :] = res[:, t * LANE:(t + 1) * LANE]
                for r in range(c * s_per, (c + 1) * s_per):
                    d = jnp.where(j == 0, prime_id + r, dst_of(slot_ref[prv + r]))
                    scatter_row(prv_b, r, d).start(priority=r % 2)

        def last(cur):
            def one(r, _):
                scatter_row(cur, r, dst_of(slot_ref[j * RT + r])).start()
                return 0
            lax.fori_loop(0, RT, one, 0)
            wait_rows(ssem.at[cur])
            for back in range(1, RING):
                pl.when(j >= back - 1)(lambda b=(cur - back) % RING: wait_rows(ssem.at[b]))
            for ahead in range(1, RING):
                wait_rows(gsem.at[(cur + ahead) % RING])

        for cur in range(RING):
            pl.when(active & (j % RING == cur))(lambda cur=cur: tile(cur))
        for cur in range(RING):
            pl.when((j == last_tile) & (j % RING == cur))(lambda cur=cur: last(cur))

    grid_spec = pltpu.PrefetchScalarGridSpec(
        num_scalar_prefetch=3,
        grid=(nt,),
        in_specs=[pl.BlockSpec(memory_space=pl.ANY)] * 4,
        out_specs=pl.BlockSpec(memory_space=pl.ANY),
        scratch_shapes=[pltpu.VMEM((RT * PITCH, LANE), f32)] * (2 * RING) + [
                        pltpu.VMEM((D, D_EXPERT), bf16), pltpu.VMEM((D, D_EXPERT), bf16),
                        pltpu.VMEM((D_EXPERT, D), bf16),
                        pltpu.VMEM((2, D, D_EXPERT), f32), pltpu.VMEM((2, D, D_EXPERT), f32),
                        pltpu.VMEM((2, D_EXPERT, D), f32),
                        pltpu.SemaphoreType.DMA((RING,)), pltpu.SemaphoreType.DMA((RING,)),
                        pltpu.SemaphoreType.DMA((2,))],
    )
    return pl.pallas_call(
        body,
        grid_spec=grid_spec,
        out_shape=jax.ShapeDtypeStruct((out_rows * O2_PITCH, LANE), f32),
        compiler_params=_cparams(("arbitrary",)),
    )(tile_e, n_used, packed, h2, w1, w3, w2)


def _moe_combine(x, o2, wgt, mod, norm_final, row0):
    T = x.shape[0]
    tm = min(mod.tm, 512)
    b0 = row0 // tm

    def body(x_ref, o_ref, p_ref, m_ref, *rest):
        p = p_ref[...]
        row = lambda k: jnp.concatenate(
            [o_ref[pl.ds(k * SLAB + s, tm, stride=O2_PITCH), :] for s in range(SLAB)], axis=1)
        y = p[:, 0:1] * row(0)
        for k in range(1, TOP_K):
            y = y + p[:, k:k + 1] * row(k)
        xo = x_ref[...] + m_ref[...] * y
        if norm_final is None:
            rest[0][...] = xo
        else:
            rest[1][...] = _rms(xo, rest[0][...])

    ins = [x, o2, wgt, mod.arr]
    specs = [pl.BlockSpec((tm, D), lambda m: (m, 0)),
             pl.BlockSpec((tm * O2_PITCH, LANE), lambda m: (b0 + m, 0)),
             pl.BlockSpec((tm, ROUTE_W), lambda m: (b0 + m, 0)), mod.row_spec(5, tm)]
    if norm_final is not None:
        ins.append(norm_final.reshape(1, D))
        specs.append(pl.BlockSpec((1, D), lambda m: (0, 0)))
    return pl.pallas_call(
        body,
        grid=(T // tm,),
        in_specs=specs,
        out_specs=pl.BlockSpec((tm, D), lambda m: (m, 0)),
        out_shape=jax.ShapeDtypeStruct((T, D), f32),
        compiler_params=_cparams(("parallel",)),
    )(*ins)


def _moe(groups, p, l, big, norm_final):
    pad = ROUTE_W - N_GROUPS - N_EXPERTS
    w_r = jnp.pad(jnp.concatenate([p['moe_w_group'], p['moe_w_router']], axis=1), ((0, 0), (0, pad)))
    b_r = jnp.pad(jnp.concatenate([p['moe_b_group'], p['moe_b_router']]), (0, pad)).reshape(1, ROUTE_W)
    t_all = sum(g.T for g in groups)
    bufs, row0 = None, 0
    for g in groups:
        g.row0 = row0
        bufs = _router(g.x, p['norm_ffn'], g.mod, w_r, b_r, row0, t_all, bufs)
        row0 += g.T
    h2, eid, wgt = bufs
    packed, tile_e, n_used = _moe_plan(eid, t_all)
    o2 = _experts(h2, packed, tile_e, n_used, l, big['moe_w1'], big['moe_w3'], big['moe_w2'])
    for g in groups:
        g.x = _moe_combine(g.x, o2, wgt, g.mod, norm_final, g.row0)


class _Group:
    def __init__(self, x, mods, states, pos0):
        self.B, self.L, _ = x.shape
        self.T = self.B * self.L
        self.tm = _row_tile(self.T)
        self.x = x.reshape(self.T, D)
        self.mods, self.states, self.pos0 = mods, states, pos0
        self.outs = ([], [], [], [], [])
        self.ret_all = self.rw_all = None

    def mix(self, l, p, rp, sp, layers, big, w_in_t):
        B, L = self.B, self.L
        self.mod = mod = _Mod(self.mods[l], L, self.tm)
        proj = _in_proj(self.x, layers['norm_mix'], l, mod, w_in_t)
        if self.states is None:
            y_ret, s_ret = _retention_seq(proj, B, L)
            y_rw, s_rw, shift = _rwkv_seq(proj, rp, B, L)
            y_s5, s_re, s_im = _s5_seq(proj, sp, B, L)
        else:
            st_ret, st_rw, st_shift, st_re, st_im = self.states
            y_ret, self.ret_all = _retention_step(proj, st_ret, l, self.ret_all, self.pos0)
            y_rw, self.rw_all = _rwkv_step(proj, st_shift[l], jnp.transpose(st_rw, (0, 2, 3, 4, 1)), l,
                                           self.rw_all, rp)
            s_ret = s_rw = None
            shift = proj[:, OFF_RW:OFF_RW + RWKV_PROJ]
            y_s5, s_re, s_im = _s5_step(proj, st_re[l], st_im[l], sp)
        z = _glu_proj(y_s5, l, big['s5_w_glu'])
        merged = _merge_proj(y_ret, y_rw, z, l, big['ret_w_o'], big['rwkv_w_o'], big['s5_w_o'], proj)
        self.x = _out_proj(merged, l, big['w_out'], self.x, mod)
        for lst, val in zip(self.outs, (s_ret, s_rw, shift, s_re, s_im)):
            lst.append(val)

    def results(self):
        stacked = [jnp.stack(o) if o[0] is not None else None for o in self.outs]
        if self.states is not None:
            stacked[0], stacked[1] = self.ret_all, jnp.transpose(self.rw_all, (0, 4, 1, 2, 3))
        return self.x.reshape(self.B, self.L, D), stacked


def _trunk(groups, layers, norm_final):
    big = {k: layers[k] for k in _BIG}
    w_in_t = jnp.swapaxes(layers['w_in'], 1, 2)
    for l in range(DEPTH):
        p = {name: arr[l] for name, arr in layers.items() if name not in _BIG + ('w_in',)}
        rp, sp = _rwkv_params(p), _s5_params(p)
        for g in groups:
            g.mix(l, p, rp, sp, layers, big, w_in_t)
        _moe(groups, p, l, big, norm_final if l == DEPTH - 1 else None)
    return [g.results() for g in groups]


_BIG = ('ret_w_o', 'rwkv_w_o', 's5_w_glu', 's5_w_o', 'w_out', 'moe_w1', 'moe_w3', 'moe_w2')


def kernel(x_prompt, x_sample, state_ret, state_rwkv, state_shift, state_s5_re, state_s5_im,
           c_prompt, c_sample, norm_mix, norm_ffn, w_ada, b_ada, w_in, ret_w_o, rwkv_mu, rwkv_w0,
           rwkv_w2, rwkv_a0, rwkv_a2, rwkv_g2, rwkv_k_k, rwkv_k_a, rwkv_r_k, rwkv_ln_w, rwkv_ln_b,
           rwkv_w_o, s5_a_re, s5_a_im, s5_b_re, s5_b_im, s5_c_re, s5_c_im, s5_d, s5_log_dt, s5_w_glu,
           s5_w_o, w_out, moe_w_group, moe_b_group, moe_w_router, moe_b_router, moe_w1, moe_w3, moe_w2,
           norm_final):
    layers = {
        'norm_mix': norm_mix, 'norm_ffn': norm_ffn, 'w_in': w_in,
        'ret_w_o': ret_w_o, 'rwkv_mu': rwkv_mu, 'rwkv_w0': rwkv_w0, 'rwkv_w2': rwkv_w2,
        'rwkv_a0': rwkv_a0, 'rwkv_a2': rwkv_a2, 'rwkv_g2': rwkv_g2, 'rwkv_k_k': rwkv_k_k,
        'rwkv_k_a': rwkv_k_a, 'rwkv_r_k': rwkv_r_k, 'rwkv_ln_w': rwkv_ln_w, 'rwkv_ln_b': rwkv_ln_b,
        'rwkv_w_o': rwkv_w_o, 's5_a_re': s5_a_re, 's5_a_im': s5_a_im, 's5_b_re': s5_b_re,
        's5_b_im': s5_b_im, 's5_c_re': s5_c_re, 's5_c_im': s5_c_im, 's5_d': s5_d,
        's5_log_dt': s5_log_dt, 's5_w_glu': s5_w_glu, 's5_w_o': s5_w_o, 'w_out': w_out,
        'moe_w_group': moe_w_group, 'moe_b_group': moe_b_group, 'moe_w_router': moe_w_router,
        'moe_b_router': moe_b_router, 'moe_w1': moe_w1, 'moe_w3': moe_w3, 'moe_w2': moe_w2,
    }
    Bp, Bs = x_prompt.shape[0], x_sample.shape[0]
    s_off = -(-Bp // 8) * 8
    c_all = jnp.concatenate([c_prompt, jnp.zeros((s_off - Bp, D), f32), c_sample], axis=0)
    mod_all = _adaln(c_all, w_ada, b_ada)
    prompt = _Group(x_prompt, mod_all[:, :Bp], None, 0.0)
    sample = _Group(x_sample, mod_all[:, s_off:s_off + Bs],
                    (state_ret, state_rwkv, state_shift, state_s5_re, state_s5_im), float(PAST_LEN))
    ((y_prompt, (ret_p, rwkv_p, shift_p, s5re_p, s5im_p)),
     (y_sample, (ret_s, rwkv_s, shift_s, s5re_s, s5im_s))) = _trunk([prompt, sample], layers, norm_final)
    return (y_prompt, y_sample, ret_p, ret_s, rwkv_p, rwkv_s, shift_p, shift_s, s5re_p, s5re_s, s5im_p, s5im_s)
```

```python
import jax
import jax.numpy as jnp
from jax import lax
from jax.experimental import pallas as pl
from jax.experimental.pallas import tpu as pltpu

f32 = jnp.float32
bf16 = jnp.bfloat16
i32 = jnp.int32

D = 2048
DEPTH = 2
PAST_LEN = 16384
RET_W, RET_H, RET_DK, RET_DV, RET_CHUNK = 1024, 4, 256, 256, 128
RET_GN_EPS = 1e-6
ROPE_BASE = 10000.0
RWKV_W, RWKV_N, RWKV_H = 1024, 64, 16
DECAY_LORA, AAA_LORA, GATE_LORA = 64, 64, 160
RWKV_PROJ = 3 * RWKV_W + DECAY_LORA + AAA_LORA + GATE_LORA
RWKV_LN_EPS = 64e-5
S5_W, S5_GC, S5_G, S5_P = 1024, 16, 64, 64
N_MOD = 6
RMS_EPS = 1e-6
N_GROUPS, EPG, N_EXPERTS, TOP_K, D_EXPERT = 4, 8, 32, 2, 512
IN_W = 4 * RET_W + RWKV_PROJ + S5_W + 3 * D
OFF_Q, OFF_K, OFF_V, OFF_G, OFF_RW = 0, 1024, 2048, 3072, 4096
OFF_U = OFF_RW + RWKV_PROJ
OFF_GATE = OFF_U + S5_W
LANE = 128
PROJ_W = ((IN_W + LANE - 1) // LANE) * LANE
VMEM_LIMIT = 56 * 1024 * 1024


def _cparams(sem):
    return pltpu.CompilerParams(dimension_semantics=sem, vmem_limit_bytes=VMEM_LIMIT)


def _dot(a, b):
    return jnp.dot(a, b, preferred_element_type=f32)


def _dot_nt(a, b):
    return lax.dot_general(a, b, (((1,), (1,)), ((), ())), preferred_element_type=f32)


def _dot_tn(a, b):
    return lax.dot_general(a, b, (((0,), (0,)), ((), ())), preferred_element_type=f32)


def _rms(x, g):
    return x * lax.rsqrt(jnp.mean(x * x, axis=-1, keepdims=True) + RMS_EPS) * g


def _head_norm(y, eps):
    mu = jnp.mean(y, axis=-1, keepdims=True)
    yc = y - mu
    return yc * lax.rsqrt(jnp.mean(yc * yc, axis=-1, keepdims=True) + eps)


def _row_tile(T):
    return 1024 if T >= 1024 else T


class _Mod:
    def __init__(self, mod, L, tm):
        self.L, self.tm = L, tm
        self.per_token = L == 1
        self.arr = mod if self.per_token else mod.reshape(mod.shape[0], 1, N_MOD * D)

    def spec(self, j, tn, col_of):
        nb = D // tn
        if self.per_token:
            return pl.BlockSpec((self.tm, tn), lambda m, n: (m, j * nb + col_of(n)))
        L, tm = self.L, self.tm
        return pl.BlockSpec((None, 1, tn), lambda m, n: ((m * tm) // L, 0, j * nb + col_of(n)))

    def row_spec(self, j, tm, remap=lambda m: m):
        if self.per_token:
            return pl.BlockSpec((tm, D), lambda m: (remap(m), j))
        L = self.L
        return pl.BlockSpec((None, 1, D), lambda m: ((remap(m) * tm) // L, 0, j))


def _fused_mm(x_ops, w_ops, e_ops, pre, post, *, grid, out_specs, out_shape, cache_shapes):
    nx, nw, ne = len(x_ops), len(w_ops), len(e_ops)
    n_out = len(out_shape)

    def body(*refs):
        x_refs = refs[:nx]
        w_refs = refs[nx:nx + nw]
        e_refs = refs[nx + nw:nx + nw + ne]
        o_refs = refs[nx + nw + ne:nx + nw + ne + n_out]
        caches = refs[nx + nw + ne + n_out:]
        if cache_shapes:
            @pl.when(pl.program_id(1) == 0)
            def _():
                for i in range(nx):
                    caches[i][...] = pre(i, x_refs[i], e_refs).astype(bf16)
            lhs = [c[...] for c in caches]
        else:
            lhs = [x[...] for x in x_refs]
        prods = [(_dot_nt if len(w_ops[j]) > 3 and w_ops[j][3] else _dot)(
            lhs[w_ops[j][2]], w_refs[j][...].astype(bf16)) for j in range(nw)]
        for o_ref, o in zip(o_refs, post(prods, e_refs)):
            o_ref[...] = o.astype(o_ref.dtype)

    return pl.pallas_call(
        body,
        grid=grid,
        in_specs=[s for _, s in x_ops] + [w[1] for w in w_ops] + [s for _, s in e_ops],
        out_specs=out_specs,
        out_shape=out_shape,
        scratch_shapes=[pltpu.VMEM(s, bf16) for s in cache_shapes],
        compiler_params=_cparams(("parallel", "arbitrary")),
    )(*[a for a, _ in x_ops], *[w[0] for w in w_ops], *[a for a, _ in e_ops])


def _adaln(c_all, w_ada, b_ada):
    R = c_all.shape[0]
    tn = 1024

    def pre(i, x_ref, e_refs):
        c = x_ref[...]
        return c * jax.nn.sigmoid(c)

    def post(prods, e_refs):
        return (prods[0] + e_refs[0][...],)

    (out,) = _fused_mm(
        [(c_all, pl.BlockSpec((R, D), lambda l, n: (0, 0)))],
        [(w_ada, pl.BlockSpec((None, D, tn), lambda l, n: (l, 0, n)), 0)],
        [(b_ada.reshape(DEPTH, 1, N_MOD * D), pl.BlockSpec((None, 1, tn), lambda l, n: (l, 0, n)))],
        pre, post,
        grid=(DEPTH, N_MOD * D // tn),
        out_specs=[pl.BlockSpec((None, R, tn), lambda l, n: (l, 0, n))],
        out_shape=[jax.ShapeDtypeStruct((DEPTH, R, N_MOD * D), f32)],
        cache_shapes=[(R, D)],
    )
    return out


def _in_proj(x, g, l, mod, w_in_t):
    T = x.shape[0]
    tm, tn = mod.tm, 1024
    tr = min(tm, 512)

    def norm_body(x_ref, g_ref, shift_ref, scale_ref, h_ref):
        h_ref[...] = (_rms(x_ref[...], g_ref[...]) * (1.0 + scale_ref[...]) + shift_ref[...]).astype(bf16)

    h = pl.pallas_call(
        norm_body,
        grid=(T // tr,),
        in_specs=[pl.BlockSpec((tr, D), lambda m: (m, 0)),
                  pl.BlockSpec((None, 1, D), lambda m: (l, 0, 0)),
                  mod.row_spec(0, tr), mod.row_spec(1, tr)],
        out_specs=pl.BlockSpec((tr, D), lambda m: (m, 0)),
        out_shape=jax.ShapeDtypeStruct((T, D), bf16),
        compiler_params=_cparams(("parallel",)),
    )(x, g.reshape(DEPTH, 1, D), mod.arr, mod.arr)

    def mm_body(h_ref, w_ref, o_ref, wb):
        @pl.when(pl.program_id(1) == 0)
        def _():
            wb[...] = w_ref[...].astype(bf16)

        col = pl.program_id(0) * tn + lax.broadcasted_iota(i32, (tm, tn), 1)
        o_ref[...] = jnp.where(col < IN_W, _dot_nt(h_ref[...], wb[...]), 0.0)

    return pl.pallas_call(
        mm_body,
        grid=(pl.cdiv(PROJ_W, tn), T // tm),
        in_specs=[pl.BlockSpec((tm, D), lambda n, m: (m, 0)),
                  pl.BlockSpec((None, tn, D), lambda n, m: (l, n, 0))],
        out_specs=pl.BlockSpec((tm, tn), lambda n, m: (m, n)),
        out_shape=jax.ShapeDtypeStruct((T, PROJ_W), f32),
        scratch_shapes=[pltpu.VMEM((tn, D), bf16)],
        compiler_params=_cparams(("parallel", "arbitrary")),
    )(h, w_in_t)


def _glu_proj(yg, l, w_glu):
    T = yg.shape[0]
    tm, tn = _row_tile(T), 512
    nb = S5_W // tn

    def post(prods, e_refs):
        return (prods[0] * jax.nn.sigmoid(prods[1]),)

    (out,) = _fused_mm(
        [(yg, pl.BlockSpec((tm, S5_W), lambda m, n: (m, 0)))],
        [(w_glu, pl.BlockSpec((None, S5_W, tn), lambda m, n: (l, 0, n)), 0),
         (w_glu, pl.BlockSpec((None, S5_W, tn), lambda m, n: (l, 0, nb + n)), 0)],
        [], None, post,
        grid=(T // tm, nb),
        out_specs=[pl.BlockSpec((tm, tn), lambda m, n: (m, n))],
        out_shape=[jax.ShapeDtypeStruct((T, S5_W), bf16)],
        cache_shapes=[],
    )
    return out


def _merge_proj(y_ret, y_rw, y_s5, l, w_ret, w_rw, w_s5, proj):
    T = y_ret.shape[0]
    tm, tn = _row_tile(T), 512
    lead = OFF_GATE % LANE
    base = OFF_GATE - lead

    def gate_spec(i):
        return pl.BlockSpec((pl.Element(tm), pl.Element(tn + LANE)),
                            lambda m, n: (pl.multiple_of(m * tm, tm), pl.multiple_of(base + i * D + n * tn, LANE)))

    def post(prods, e_refs):
        acc = None
        for p, e in zip(prods, e_refs):
            t = jax.nn.sigmoid(e[:, lead:lead + tn]) * p
            acc = t if acc is None else acc + t
        return (acc,)

    xspec = pl.BlockSpec((tm, RET_W), lambda m, n: (m, 0))
    wspec = pl.BlockSpec((None, RET_W, tn), lambda m, n: (l, 0, n))
    (out,) = _fused_mm(
        [(y_ret, xspec), (y_rw, xspec), (y_s5, xspec)],
        [(w_ret, wspec, 0), (w_rw, wspec, 1), (w_s5, wspec, 2)],
        [(proj, gate_spec(0)), (proj, gate_spec(1)), (proj, gate_spec(2))],
        None, post,
        grid=(T // tm, D // tn),
        out_specs=[pl.BlockSpec((tm, tn), lambda m, n: (m, n))],
        out_shape=[jax.ShapeDtypeStruct((T, D), bf16)],
        cache_shapes=[],
    )
    return out


def _out_proj(merged, l, w_out, x, mod):
    T = x.shape[0]
    tm, tn = mod.tm, 512

    def post(prods, e_refs):
        x_ref, m_ref = e_refs
        return (x_ref[...] + m_ref[...] * prods[0],)

    (out,) = _fused_mm(
        [(merged, pl.BlockSpec((tm, D), lambda m, n: (m, 0)))],
        [(w_out, pl.BlockSpec((None, D, tn), lambda m, n: (l, 0, n)), 0)],
        [(x, pl.BlockSpec((tm, tn), lambda m, n: (m, n))), (mod.arr, mod.spec(2, tn, lambda n: n))],
        None, post,
        grid=(T // tm, D // tn),
        out_specs=[pl.BlockSpec((tm, tn), lambda m, n: (m, n))],
        out_shape=[jax.ShapeDtypeStruct((T, D), f32)],
        cache_shapes=[],
    )
    return out


def _ret_consts(L, pos0):
    C = RET_CHUNK if L % RET_CHUNK == 0 else L
    H = RET_H
    log_g = jnp.log1p(-jnp.exp2(-5.0 - jnp.arange(H, dtype=f32)))
    i = jnp.arange(C, dtype=f32)
    diff = i[:, None] - i[None, :]
    causal = diff >= 0
    dmask = jnp.where(causal, jnp.exp(jnp.where(causal, diff, 0.0)[None] * log_g[:, None, None]), 0.0)
    kdec = jnp.exp((C - 1.0 - i)[:, None] * log_g[None, :])
    qdec = jnp.exp((i + 1.0)[:, None] * log_g[None, :])
    g_chunk = jnp.exp(C * log_g)
    half = RET_DK // 2
    inv = ROPE_BASE ** (-jnp.arange(half, dtype=f32) / half)
    pos = pos0 + jnp.arange(L, dtype=f32)
    ang = pos[:, None] * inv[None, :]
    return C, dmask, kdec, qdec, g_chunk, jnp.cos(ang), jnp.sin(ang)


def _rotary(x, cos, sin):
    half = RET_DK // 2
    x1, x2 = x[..., :half], x[..., half:]
    return jnp.concatenate([x1 * cos - x2 * sin, x1 * sin + x2 * cos], axis=-1)


def _retention_seq(proj, B, L):
    C, dmask, kdec, qdec, g_chunk, cos, sin = _ret_consts(L, 0.0)
    H, dk = RET_H, RET_DK
    n = L // C
    kdec_f = jnp.broadcast_to(kdec.T[:, :, None], (H, C, dk))
    qdec_f = jnp.broadcast_to(qdec.T[:, :, None], (H, C, dk))
    gch_f = jnp.broadcast_to(g_chunk[:, None, None], (H, 8, dk))
    proj3 = proj.reshape(B, L, PROJ_W)

    def body(q_ref, k_ref, v_ref, g_ref, cos_ref, sin_ref, dm_ref, kd_ref, qd_ref, gc_ref, y_ref, s_ref, st):
        c = pl.program_id(1)

        @pl.when(c == 0)
        def _():
            st[...] = jnp.zeros_like(st)

        cs, sn = cos_ref[...], sin_ref[...]
        hs = range(H)
        sl = [slice(h * dk, (h + 1) * dk) for h in hs]
        q = [_rotary(q_ref[:, sl[h]], cs, sn) for h in hs]
        k = [_rotary(k_ref[:, sl[h]], cs, sn) * (dk ** -0.5) for h in hs]
        vb = [v_ref[:, sl[h]].astype(bf16) for h in hs]
        s0 = [st[h] for h in hs]
        scores = [_dot_nt(q[h].astype(bf16), k[h].astype(bf16)) * dm_ref[h] for h in hs]
        cross = [_dot((q[h] * qd_ref[h]).astype(bf16), s0[h].astype(bf16)) for h in hs]
        kv = [_dot_tn((k[h] * kd_ref[h]).astype(bf16), vb[h]) for h in hs]
        o = [_dot(scores[h].astype(bf16), vb[h]) + cross[h] for h in hs]
        for h in hs:
            st[h] = s0[h] * gc_ref[h, 0:1, :] + kv[h]
            g = g_ref[:, sl[h]]
            y_ref[:, sl[h]] = (g * jax.nn.sigmoid(g) * _head_norm(o[h], RET_GN_EPS)).astype(bf16)

        @pl.when(c == n - 1)
        def _():
            s_ref[...] = st[...]

    def seg(off):
        return pl.BlockSpec((None, C, RET_W), lambda b, c: (b, c, off // RET_W))

    const3 = lambda shp: pl.BlockSpec(shp, lambda b, c: (0, 0, 0))
    y, s = pl.pallas_call(
        body,
        grid=(B, n),
        in_specs=[seg(OFF_Q), seg(OFF_K), seg(OFF_V), seg(OFF_G),
                  pl.BlockSpec((C, dk // 2), lambda b, c: (c, 0)),
                  pl.BlockSpec((C, dk // 2), lambda b, c: (c, 0)),
                  const3((H, C, C)), const3((H, C, dk)), const3((H, C, dk)), const3((H, 8, dk))],
        out_specs=[pl.BlockSpec((None, C, RET_W), lambda b, c: (b, c, 0)),
                   pl.BlockSpec((None, H, dk, RET_DV), lambda b, c: (b, 0, 0, 0))],
        out_shape=[jax.ShapeDtypeStruct((B, L, RET_W), bf16),
                   jax.ShapeDtypeStruct((B, H, dk, RET_DV), f32)],
        scratch_shapes=[pltpu.VMEM((H, dk, RET_DV), f32)],
        compiler_params=_cparams(("parallel", "arbitrary")),
    )(proj3, proj3, proj3, proj3, cos, sin, dmask, kdec_f, qdec_f, gch_f)
    return y.reshape(B * L, RET_W), s


STEP_TB = 16


def _layer_grid(l, buf, inner):
    if buf is not None:
        return inner, (lambda fn: (lambda *ix: fn(l, ix, ix)))
    assert l == 0
    last = tuple(n - 1 for n in inner)

    def wrap(fn):
        def index_map(d, *ix):
            parked = tuple(jnp.where(d == l, i, z) for i, z in zip(ix, last))
            return fn(d, ix, parked)
        return index_map
    return (DEPTH,) + inner, wrap


def _retention_step(proj, s_all, l, buf, pos0):
    B = proj.shape[0]
    _, _, _, _, g_chunk, cos, sin = _ret_consts(1, pos0)
    H, dk = RET_H, RET_DK
    gch = jnp.broadcast_to(g_chunk[:, None, None], (H, 8, dk))
    tb = STEP_TB
    grid, wrap = _layer_grid(l, buf, (B // tb, H))

    def body(q_ref, k_ref, v_ref, g_ref, cos_ref, sin_ref, gc_ref, s_ref, *rest):
        y_ref, so_ref = rest[-2:]

        def update():
            cs, sn = cos_ref[...], sin_ref[...]
            q = _rotary(q_ref[...], cs, sn)
            k = _rotary(k_ref[...], cs, sn) * (dk ** -0.5)
            v = v_ref[...]
            s1 = s_ref[...] * gc_ref[0:1, :][None] + k[:, :, None] * v[:, None, :]
            so_ref[...] = s1
            o = jnp.sum(q[:, :, None] * s1, axis=1)
            g = g_ref[...]
            y_ref[...] = (g * jax.nn.sigmoid(g) * _head_norm(o, RET_GN_EPS)).astype(bf16)

        if buf is not None:
            update()
        else:
            pl.when(pl.program_id(0) == l)(update)

            @pl.when(pl.program_id(0) != l)
            def _():
                so_ref[...] = jnp.zeros_like(so_ref)

    def seg(off):
        return pl.BlockSpec((tb, dk), wrap(lambda d, ix, pk: (pk[0], off // dk + pk[1])))

    const = lambda d, ix, pk: (0, 0)
    in_specs = [seg(OFF_Q), seg(OFF_K), seg(OFF_V), seg(OFF_G),
                pl.BlockSpec((1, dk // 2), wrap(const)), pl.BlockSpec((1, dk // 2), wrap(const)),
                pl.BlockSpec((None, 8, dk), wrap(lambda d, ix, pk: (pk[1], 0, 0))),
                pl.BlockSpec((None, tb, None, dk, RET_DV), wrap(lambda d, ix, pk: (l, pk[0], pk[1], 0, 0)))]
    args = [proj, proj, proj, proj, cos, sin, gch, s_all]
    aliases = {}
    if buf is not None:
        in_specs.append(pl.BlockSpec(memory_space=pl.ANY))
        args.append(buf)
        aliases = {len(args) - 1: 1}
    return pl.pallas_call(
        body,
        grid=grid,
        in_specs=in_specs,
        out_specs=[pl.BlockSpec((tb, dk), wrap(lambda d, ix, pk: pk)),
                   pl.BlockSpec((None, tb, None, dk, RET_DV), wrap(lambda d, ix, pk: (d, ix[0], ix[1], 0, 0)))],
        out_shape=[jax.ShapeDtypeStruct((B, RET_W), bf16),
                   jax.ShapeDtypeStruct((DEPTH, B, H, dk, RET_DV), f32)],
        input_output_aliases=aliases,
        compiler_params=_cparams(("arbitrary",) * len(grid)),
    )(*args)


RW_C = 64
RW_Q = 4
RW_NQ = RWKV_H // RW_Q
RW_GROUP = 8
RW_BLK =((RWKV_PROJ + LANE - 1) // LANE) * LANE
_RW_PKEYS = ('mu', 'w0', 'w2', 'a0', 'a2', 'g2', 'k_k', 'k_a', 'r_k', 'ln_w', 'ln_b')


def _rwkv_params(p):
    return dict(
        mu=jnp.pad(p['rwkv_mu'], (0, RW_BLK - RWKV_PROJ)).reshape(1, RW_BLK),
        w0=p['rwkv_w0'].reshape(1, RWKV_W), w2=p['rwkv_w2'],
        a0=p['rwkv_a0'].reshape(1, RWKV_W), a2=p['rwkv_a2'], g2=p['rwkv_g2'],
        k_k=p['rwkv_k_k'].reshape(1, RWKV_W), k_a=p['rwkv_k_a'].reshape(1, RWKV_W),
        r_k=p['rwkv_r_k'].reshape(1, RWKV_W),
        ln_w=p['rwkv_ln_w'].reshape(1, RWKV_W), ln_b=p['rwkv_ln_b'].reshape(1, RWKV_W))


def _split_bf16(x, terms):
    out = []
    for _ in range(terms - 1):
        hi = x.astype(bf16)
        out.append(hi)
        x = x - hi.astype(f32)
    out.append(x.astype(bf16))
    return out


def _head_sum(x):
    QW = RW_Q * RWKV_N
    r = lax.broadcasted_iota(i32, (QW, QW), 0) // RWKV_N
    c = lax.broadcasted_iota(i32, (QW, QW), 1) // RWKV_N
    ones = (r == c).astype(bf16)
    parts = _split_bf16(x, 2)
    outs = []
    for q in range(RW_NQ):
        sl = slice(q * QW, (q + 1) * QW)
        outs.append(_dot(parts[0][:, sl], ones) + _dot(parts[1][:, sl], ones))
    return jnp.concatenate(outs, axis=1)


def _rwkv_mix(rw, prev, pr):
    m = rw + (prev - rw) * pr['mu'][...]
    r = m[:, 0:RWKV_W]
    k = m[:, RWKV_W:2 * RWKV_W]
    v = m[:, 2 * RWKV_W:3 * RWKV_W]
    o = 3 * RWKV_W
    xw = m[:, o:o + DECAY_LORA]
    xa = m[:, o + DECAY_LORA:o + DECAY_LORA + AAA_LORA]
    xg = m[:, o + DECAY_LORA + AAA_LORA:o + DECAY_LORA + AAA_LORA + GATE_LORA]
    w_log = -jax.nn.softplus(-(pr['w0'][...] + _dot(jnp.tanh(xw).astype(bf16), pr['w2'][...].astype(bf16)))) - 0.5
    lw = -jnp.exp(w_log)
    a = jax.nn.sigmoid(pr['a0'][...] + _dot(xa.astype(bf16), pr['a2'][...].astype(bf16)))
    g = _dot(jax.nn.sigmoid(xg).astype(bf16), pr['g2'][...].astype(bf16))
    kk = k * pr['k_k'][...]
    kk = kk / jnp.maximum(jnp.sqrt(_head_sum(kk * kk)), 1e-12)
    kf = k * (1.0 + (a - 1.0) * pr['k_a'][...])
    return r, lw, kf, v, kk, a, g


def _rwkv_out(y, r, kf, v, g, pr):
    yc = y - _head_sum(y) * (1.0 / RWKV_N)
    yn = yc * lax.rsqrt(_head_sum(yc * yc) * (1.0 / RWKV_N) + RWKV_LN_EPS)
    yn = yn * pr['ln_w'][...] + pr['ln_b'][...]
    yn = yn + _head_sum(r * kf * pr['r_k'][...]) * v
    return yn * g


def _rwkv_seq(proj, rp, B, L):
    C = RW_C
    assert L % C == 0 and C == RWKV_N
    n = L // C
    QW = RW_Q * RWKV_N
    BC = B * C
    proj3 = proj.reshape(B, L, PROJ_W)

    def body(rw_ref, *refs):
        pr = dict(zip(_RW_PKEYS, refs[:len(_RW_PKEYS)]))
        y_ref, s_ref, sh_ref, st, carry = refs[len(_RW_PKEYS):]
        c = pl.program_id(0)

        @pl.when(c == 0)
        def _():
            st[...] = jnp.zeros_like(st)
            carry[...] = jnp.zeros_like(carry)

        rw = rw_ref[...].reshape(BC, RW_BLK)
        rolled = pltpu.roll(rw, 1, 0)
        row = lax.broadcasted_iota(i32, (C, RW_BLK), 0)
        prev = jnp.concatenate(
            [jnp.where(row == 0, carry[b, 0:1, :], rolled[b * C:(b + 1) * C]) for b in range(B)], axis=0)
        for b in range(B):
            carry[b, 0:1, :] = rw[(b + 1) * C - 1:(b + 1) * C, :]
        r, lw, kf, v, kk, a, g = _rwkv_mix(rw, prev, pr)

        ti = lax.broadcasted_iota(i32, (BC, BC), 0)
        si = lax.broadcasted_iota(i32, (BC, BC), 1)
        tril = ((si <= ti) & ((si // C) == (ti // C))).astype(bf16)
        lg = sum(_dot(tril, part) for part in _split_bf16(lw, 3))
        lgc = jnp.concatenate(
            [jnp.broadcast_to(lg[(b + 1) * C - 1:(b + 1) * C, :], (C, RWKV_W)) for b in range(B)], axis=0)
        e_neg = jnp.exp(-lg)
        e_rem = jnp.exp(lgc - lg)
        at = kk * jnp.exp(lg - lw)
        ka = kk * a
        bt = ka * e_neg
        kt = kf * e_neg
        rt = r * jnp.exp(lg)
        bh = ka * e_rem
        kh = kf * e_rem
        gcr = jnp.exp(lgc)

        rr = lax.broadcasted_iota(i32, (RW_Q * C, QW), 0)
        ll = lax.broadcasted_iota(i32, (RW_Q * C, QW), 1)
        blockmask = (rr // C) == (ll // RWKV_N)
        tt = lax.broadcasted_iota(i32, (C, QW), 0)
        ss = lax.broadcasted_iota(i32, (C, QW), 1) % C
        strict = ss < tt
        incl = ss <= tt
        eye = (ss == tt).astype(f32)

        def bd(x):
            return jnp.where(blockmask, jnp.concatenate([x] * RW_Q, axis=0), 0.0).astype(bf16)

        ys = [[None] * RW_NQ for _ in range(B)]
        chains = [(b, q) for b in range(B) for q in range(RW_NQ)]
        for g0 in range(0, len(chains), RW_GROUP):
            grp = chains[g0:g0 + RW_GROUP]
            idx = [(slice(b * C, (b + 1) * C), slice(q * QW, (q + 1) * QW)) for b, q in grp]
            each = lambda fn: [fn(i) for i in range(len(grp))]
            vq = each(lambda i: v[idx[i]])
            ar = each(lambda i: jnp.concatenate([at[idx[i]], rt[idx[i]]], axis=0).astype(bf16))
            big = each(lambda i: _dot_nt(ar[i], jnp.concatenate([bd(bt[idx[i]]), bd(kt[idx[i]])], axis=0)))
            s0 = each(lambda i: st[grp[i]])
            asrs = each(lambda i: _dot_nt(ar[i], s0[i].astype(bf16)))
            nmat = each(lambda i: jnp.where(strict, big[i][:C, :QW], 0.0))
            akm = each(lambda i: jnp.where(strict, big[i][:C, QW:], 0.0).astype(bf16))
            rbk = each(lambda i: jnp.concatenate([jnp.where(incl, big[i][C:, :QW], 0.0),
                                                  jnp.where(incl, big[i][C:, QW:], 0.0)], axis=1).astype(bf16))
            tm = each(lambda i: eye - nmat[i])
            pw = each(lambda i: _dot(nmat[i].astype(bf16), bd(nmat[i])))
            lvl = 2
            while lvl < C:
                res = each(lambda i: _dot(jnp.concatenate([tm[i], pw[i]], axis=0).astype(bf16), bd(pw[i])))
                tm = each(lambda i: tm[i] + res[i][:C])
                pw = each(lambda i: res[i][C:])
                lvl *= 2
            vbd = each(lambda i: bd(vq[i]))
            rhs = each(lambda i: -(asrs[i][:C] + _dot(akm[i], vbd[i])))
            u = each(lambda i: _dot(tm[i].astype(bf16), bd(rhs[i])))
            y = each(lambda i: asrs[i][C:] + _dot(rbk[i], jnp.concatenate([bd(u[i]), vbd[i]], axis=0)))
            upd = each(lambda i: _dot_tn(jnp.concatenate([u[i], vq[i]], axis=0).astype(bf16),
                                         jnp.concatenate([bh[idx[i]], kh[idx[i]]], axis=0).astype(bf16)))
            for i, (b, q) in enumerate(grp):
                st[b, q] = s0[i] * gcr[b * C:b * C + 1, idx[i][1]] + jnp.where(blockmask, upd[i], 0.0)
                ys[b][q] = y[i]

        y = jnp.concatenate([jnp.concatenate(yb, axis=1) for yb in ys], axis=0)
        y_ref[...] = _rwkv_out(y, r, kf, v, g, pr).astype(bf16).reshape(B, C, RWKV_W)

        @pl.when(c == n - 1)
        def _():
            for b in range(B):
                for q in range(RW_NQ):
                    for h in range(RW_Q):
                        hs = slice(h * RWKV_N, (h + 1) * RWKV_N)
                        s_ref[b, q * RW_Q + h] = st[b, q, hs, hs]
                sh_ref[b] = rw[(b + 1) * C - 1:(b + 1) * C, 0:RWKV_PROJ]

    pspecs = [pl.BlockSpec(rp[k].shape, lambda c: (0, 0)) for k in _RW_PKEYS]
    y, s, sh = pl.pallas_call(
        body,
        grid=(n,),
        in_specs=[pl.BlockSpec((pl.Element(B), pl.Element(C), pl.Element(RW_BLK)),
                               lambda c: (0, pl.multiple_of(c * C, C), OFF_RW))] + pspecs,
        out_specs=[pl.BlockSpec((B, C, RWKV_W), lambda c: (0, c, 0)),
                   pl.BlockSpec((B, RWKV_H, RWKV_N, RWKV_N), lambda c: (0, 0, 0, 0)),
                   pl.BlockSpec((B, 1, RWKV_PROJ), lambda c: (0, 0, 0))],
        out_shape=[jax.ShapeDtypeStruct((B, L, RWKV_W), bf16),
                   jax.ShapeDtypeStruct((B, RWKV_H, RWKV_N, RWKV_N), f32),
                   jax.ShapeDtypeStruct((B, 1, RWKV_PROJ), f32)],
        scratch_shapes=[pltpu.VMEM((B, RW_NQ, QW, QW), f32), pltpu.VMEM((B, 8, RW_BLK), f32)],
        compiler_params=_cparams(("arbitrary",)),
    )(proj3, *[rp[k] for k in _RW_PKEYS])
    return y.reshape(B * L, RWKV_W), s, sh.reshape(B, RWKV_PROJ)


def _rwkv_step(proj, shift, s_t, l, buf, rp):
    B = proj.shape[0]
    N = RWKV_N
    shift_p = jnp.pad(shift, ((0, 0), (0, RW_BLK - RWKV_PROJ)))
    grid, wrap = _layer_grid(l, buf, (RWKV_H,))
    npk = len(_RW_PKEYS)
    vec_names = ('r', 'w', 'kf', 'v', 'kk', 'ka')

    def body(rw_ref, sh_ref, s_ref, *refs):
        pr = dict(zip(_RW_PKEYS, refs[:npk]))
        y_ref, so_ref = refs[-9:-7]
        vt = dict(zip(vec_names, refs[-7:-1]))
        yt = refs[-1]
        h = pl.program_id(len(grid) - 1)

        def update():
            @pl.when(h == 0)
            def _():
                r, lw, kf, v, kk, a, _ = _rwkv_mix(rw_ref[...], sh_ref[...], pr)
                for name, val in zip(vec_names, (r, jnp.exp(lw), kf, v, kk, kk * a)):
                    vt[name][...] = val.T

            hs = pl.ds(pl.multiple_of(h * N, N), N)
            s = s_ref[...]
            kk_h = vt['kk'][hs, :]
            sa = jnp.sum(s * (-kk_h)[None], axis=1, keepdims=True)
            s1 = s * vt['w'][hs, :][None] + sa * vt['ka'][hs, :][None] + vt['v'][hs, :][:, None, :] * vt['kf'][hs, :][None]
            so_ref[...] = s1
            yt[hs, :] = jnp.sum(s1 * vt['r'][hs, :][None], axis=1)

            @pl.when(h == RWKV_H - 1)
            def _():
                r, lw, kf, v, kk, a, g = _rwkv_mix(rw_ref[...], sh_ref[...], pr)
                y_ref[...] = _rwkv_out(yt[...].T, r, kf, v, g, pr).astype(bf16)

        if buf is not None:
            update()
        else:
            pl.when(pl.program_id(0) == l)(update)

            @pl.when(pl.program_id(0) != l)
            def _():
                so_ref[...] = jnp.zeros_like(so_ref)

    const = lambda d, ix, pk: (0, 0)
    in_specs = [pl.BlockSpec((pl.Element(B), pl.Element(RW_BLK)), wrap(lambda d, ix, pk: (0, OFF_RW))),
                pl.BlockSpec((B, RW_BLK), wrap(const)),
                pl.BlockSpec((None, None, N, N, B), wrap(lambda d, ix, pk: (l, pk[0], 0, 0, 0)))]
    in_specs += [pl.BlockSpec(rp[k].shape, wrap(const)) for k in _RW_PKEYS]
    args = [proj, shift_p, s_t] + [rp[k] for k in _RW_PKEYS]
    aliases = {}
    if buf is not None:
        in_specs.append(pl.BlockSpec(memory_space=pl.ANY))
        args.append(buf)
        aliases = {len(args) - 1: 1}
    return pl.pallas_call(
        body,
        grid=grid,
        in_specs=in_specs,
        out_specs=[pl.BlockSpec((B, RWKV_W), wrap(const)),
                   pl.BlockSpec((None, None, N, N, B), wrap(lambda d, ix, pk: (d, ix[0], 0, 0, 0)))],
        out_shape=[jax.ShapeDtypeStruct((B, RWKV_W), bf16),
                   jax.ShapeDtypeStruct((DEPTH, RWKV_H, N, N, B), f32)],
        scratch_shapes=[pltpu.VMEM((RWKV_W, B), f32)] * 7,
        input_output_aliases=aliases,
        compiler_params=_cparams(("arbitrary",) * len(grid)),
    )(*args)


S5_N = S5_G * S5_P
S5_KC = 256
S5_NKC = S5_W // S5_KC
S5_TILES = S5_N // LANE
S5_GB = S5_KC // S5_GC


def _s5_params(p):
    a_re, a_im = p['s5_a_re'], p['s5_a_im']
    dstep = jnp.exp(p['s5_log_dt'])[:, None]
    mag = jnp.exp(a_re * dstep)
    ab_re = mag * jnp.cos(a_im * dstep)
    ab_im = mag * jnp.sin(a_im * dstep)
    den = a_re * a_re + a_im * a_im
    n_re = ab_re - 1.0
    f_re = (n_re * a_re + ab_im * a_im) / den
    f_im = (ab_im * a_re - n_re * a_im) / den
    b_re, b_im = p['s5_b_re'], p['s5_b_im']
    bb_re = f_re[..., None] * b_re - f_im[..., None] * b_im
    bb_im = f_re[..., None] * b_im + f_im[..., None] * b_re
    eye = jnp.eye(S5_GB, dtype=f32)

    def in_map(bb):
        t = bb.reshape(S5_NKC, S5_GB, S5_P, S5_GC)
        return jnp.einsum('kgpc,gh->kgchp', t, eye).reshape(S5_NKC, S5_KC, S5_GB * S5_P)

    def out_map(cc):
        t = cc.reshape(S5_NKC, S5_GB, S5_GC, S5_P)
        return jnp.einsum('qgcp,gh->qgphc', t, eye).reshape(S5_NKC, S5_GB * S5_P, S5_KC)

    return dict(wb=jnp.concatenate([in_map(bb_re), in_map(bb_im)], axis=-1),
                wc_re=out_map(p['s5_c_re']), wc_im=out_map(p['s5_c_im']),
                ab_re_t=ab_re.reshape(S5_TILES // 8, 8, LANE), ab_im_t=ab_im.reshape(S5_TILES // 8, 8, LANE),
                ab_re=ab_re.reshape(1, S5_N), ab_im=ab_im.reshape(1, S5_N), d=p['s5_d'].reshape(1, S5_W))


def _s5_seq(proj, sp, B, L):
    Lc = min(L, 256)
    n = L // Lc
    pitch = Lc + 4
    lead = OFF_U % LANE
    base = OFF_U - lead
    width = S5_W + LANE
    nt4 = S5_TILES // 8
    tiles_kc = S5_TILES // S5_NKC

    def body(u_ref, wb_ref, wcr_ref, wci_ref, abr_ref, abi_ref, d_ref, y_ref, sr_ref, si_ref, xr, xi, cr, ci):
        c = pl.program_id(1)

        @pl.when(c == 0)
        def _():
            cr[...] = jnp.zeros_like(cr)
            ci[...] = jnp.zeros_like(ci)

        u = u_ref[:, lead:lead + S5_W]
        ub = u.astype(bf16)
        for kc in range(S5_NKC):
            bu = _dot(ub[:, kc * S5_KC:(kc + 1) * S5_KC], wb_ref[kc].astype(bf16))
            for j in range(tiles_kc):
                t = kc * tiles_kc + j
                xr[pl.ds(t * pitch, Lc), :] = bu[:, j * LANE:(j + 1) * LANE]
                xi[pl.ds(t * pitch, Lc), :] = bu[:, (tiles_kc + j) * LANE:(tiles_kc + j + 1) * LANE]

        abr = [abr_ref[g] for g in range(nt4)]
        abi = [abi_ref[g] for g in range(nt4)]

        def step(t, carry):
            out = []
            for g in range(nt4):
                s_r, s_i = carry[2 * g], carry[2 * g + 1]
                idx = pl.ds(g * 8 * pitch + t, 8, stride=pitch)
                n_r = abr[g] * s_r - abi[g] * s_i + xr[idx, :]
                n_i = abr[g] * s_i + abi[g] * s_r + xi[idx, :]
                xr[idx, :] = n_r
                xi[idx, :] = n_i
                out += [n_r, n_i]
            return tuple(out)

        init = []
        for g in range(nt4):
            init += [cr[g], ci[g]]
        fin = lax.fori_loop(0, Lc, step, tuple(init), unroll=2)
        for g in range(nt4):
            cr[g] = fin[2 * g]
            ci[g] = fin[2 * g + 1]

        for q in range(S5_NKC):
            lr = jnp.concatenate([xr[pl.ds((q * tiles_kc + j) * pitch, Lc), :] for j in range(tiles_kc)], axis=1)
            li = jnp.concatenate([xi[pl.ds((q * tiles_kc + j) * pitch, Lc), :] for j in range(tiles_kc)], axis=1)
            y = _dot(lr.astype(bf16), wcr_ref[q].astype(bf16)) - _dot(li.astype(bf16), wci_ref[q].astype(bf16))
            cs = slice(q * S5_KC, (q + 1) * S5_KC)
            y = y + d_ref[:, cs] * u[:, cs]
            y_ref[:, cs] = jax.nn.gelu(y).astype(bf16)

        @pl.when(c == n - 1)
        def _():
            sr_ref[...] = cr[...]
            si_ref[...] = ci[...]

    full = lambda shp: pl.BlockSpec(shp, lambda b, c: (0,) * len(shp))
    y, sr, si = pl.pallas_call(
        body,
        grid=(B, n),
        in_specs=[pl.BlockSpec((pl.Element(Lc), pl.Element(width)),
                               lambda b, c: (pl.multiple_of((b * n + c) * Lc, Lc), base)),
                  full(sp['wb'].shape), full(sp['wc_re'].shape), full(sp['wc_im'].shape),
                  full(sp['ab_re_t'].shape), full(sp['ab_im_t'].shape), full((1, S5_W))],
        out_specs=[pl.BlockSpec((Lc, S5_W), lambda b, c: (b * n + c, 0)),
                   pl.BlockSpec((None, nt4, 8, LANE), lambda b, c: (b, 0, 0, 0)),
                   pl.BlockSpec((None, nt4, 8, LANE), lambda b, c: (b, 0, 0, 0))],
        out_shape=[jax.ShapeDtypeStruct((B * L, S5_W), bf16),
                   jax.ShapeDtypeStruct((B, nt4, 8, LANE), f32),
                   jax.ShapeDtypeStruct((B, nt4, 8, LANE), f32)],
        scratch_shapes=[pltpu.VMEM((S5_TILES * pitch, LANE), f32), pltpu.VMEM((S5_TILES * pitch, LANE), f32),
                        pltpu.VMEM((nt4, 8, LANE), f32), pltpu.VMEM((nt4, 8, LANE), f32)],
        compiler_params=_cparams(("parallel", "arbitrary")),
    )(proj, sp['wb'], sp['wc_re'], sp['wc_im'], sp['ab_re_t'], sp['ab_im_t'], sp['d'])
    return y, sr.reshape(B, S5_G, S5_P), si.reshape(B, S5_G, S5_P)


def _s5_step(proj, x_re, x_im, sp):
    B = proj.shape[0]
    lead = OFF_U % LANE
    base = OFF_U - lead
    width = S5_W + LANE
    kw = S5_N // S5_NKC

    def body(u_ref, xr_ref, xi_ref, wb_ref, wcr_ref, wci_ref, abr_ref, abi_ref, d_ref, y_ref, sr_ref, si_ref):
        u = u_ref[:, lead:lead + S5_W]
        ub = u.astype(bf16)
        abr, abi = abr_ref[...], abi_ref[...]
        xr, xi = xr_ref[...], xi_ref[...]
        for kc in range(S5_NKC):
            bu = _dot(ub[:, kc * S5_KC:(kc + 1) * S5_KC], wb_ref[kc].astype(bf16))
            sl = slice(kc * kw, (kc + 1) * kw)
            n_r = abr[:, sl] * xr[:, sl] - abi[:, sl] * xi[:, sl] + bu[:, :kw]
            n_i = abr[:, sl] * xi[:, sl] + abi[:, sl] * xr[:, sl] + bu[:, kw:]
            sr_ref[:, sl] = n_r
            si_ref[:, sl] = n_i
            y = _dot(n_r.astype(bf16), wcr_ref[kc].astype(bf16)) - _dot(n_i.astype(bf16), wci_ref[kc].astype(bf16))
            cs = slice(kc * S5_KC, (kc + 1) * S5_KC)
            y = y + d_ref[:, cs] * u[:, cs]
            y_ref[:, cs] = jax.nn.gelu(y).astype(bf16)

    full = lambda shp: pl.BlockSpec(shp, lambda i: (0,) * len(shp))
    y, sr, si = pl.pallas_call(
        body,
        grid=(1,),
        in_specs=[pl.BlockSpec((pl.Element(B), pl.Element(width)), lambda i: (0, base)),
                  full((B, S5_N)), full((B, S5_N)),
                  full(sp['wb'].shape), full(sp['wc_re'].shape), full(sp['wc_im'].shape),
                  full((1, S5_N)), full((1, S5_N)), full((1, S5_W))],
        out_specs=[full((B, S5_W)), full((B, S5_N)), full((B, S5_N))],
        out_shape=[jax.ShapeDtypeStruct((B, S5_W), bf16),
                   jax.ShapeDtypeStruct((B, S5_N), f32), jax.ShapeDtypeStruct((B, S5_N), f32)],
        compiler_params=_cparams(("arbitrary",)),
    )(proj, x_re.reshape(B, S5_N), x_im.reshape(B, S5_N), sp['wb'], sp['wc_re'], sp['wc_im'],
      sp['ab_re'], sp['ab_im'], sp['d'])
    return y, sr.reshape(B, S5_G, S5_P), si.reshape(B, S5_G, S5_P)


MOE_RT, MOE_RT_SMALL = 128, 32
SLAB = D // LANE
SUB = 8
SLAB_PITCH = SLAB + SUB
H2_PITCH = SLAB + SUB
O2_PITCH = TOP_K * SLAB + SUB


def _moe_rt(T):
    return MOE_RT if T * TOP_K >= N_EXPERTS * MOE_RT else MOE_RT_SMALL
ROUTE_W = LANE


def _router(x, g, mod, w_r, b_r, row0, t_all, bufs):
    T = x.shape[0]
    tm = min(mod.tm, 512)
    nm = T // tm
    assert row0 % tm == 0
    if bufs is None:
        assert row0 == 0
        n_steps = pl.cdiv(t_all, tm)
    else:
        n_steps = nm
    clamp = lambda m: jnp.minimum(m, nm - 1)

    def body(x_ref, g_ref, sh_ref, sc_ref, w_ref, b_ref, *rest):
        h_ref, e_ref, p_ref = rest[-3:]
        if n_steps > nm:
            @pl.when(pl.program_id(0) >= nm)
            def _():
                h_ref[...] = jnp.zeros_like(h_ref)
                e_ref[...] = jnp.zeros_like(e_ref)
                p_ref[...] = jnp.zeros_like(p_ref)

            pl.when(pl.program_id(0) < nm)(lambda: route(x_ref, g_ref, sh_ref, sc_ref, w_ref, b_ref, *rest[-3:]))
        else:
            route(x_ref, g_ref, sh_ref, sc_ref, w_ref, b_ref, *rest[-3:])

    def route(x_ref, g_ref, sh_ref, sc_ref, w_ref, b_ref, h_ref, e_ref, p_ref):
        h2 = _rms(x_ref[...], g_ref[...]) * (1.0 + sc_ref[...]) + sh_ref[...]
        for s in range(SLAB):
            h_ref[pl.ds(s, tm, stride=H2_PITCH), :] = h2[:, s * LANE:(s + 1) * LANE]
        for s in range(SLAB, H2_PITCH):
            h_ref[pl.ds(s, tm, stride=H2_PITCH), :] = jnp.zeros((tm, LANE), f32)
        logits = _dot(h2.astype(bf16), w_ref[...].astype(bf16)) + b_ref[...]
        lane = lax.broadcasted_iota(i32, (tm, ROUTE_W), 1)
        ninf = jnp.float32(-jnp.inf)
        gl = jnp.where(lane < N_GROUPS, logits, ninf)
        gm = jnp.max(gl, axis=-1, keepdims=True)
        g_p = 1.0 / jnp.sum(jnp.exp(gl - gm), axis=-1, keepdims=True)
        g_idx = jnp.min(jnp.where(gl == gm, lane, ROUTE_W), axis=-1, keepdims=True)
        valid = (lane >= N_GROUPS) & (lane < N_GROUPS + N_EXPERTS) & (((lane - N_GROUPS) // EPG) == g_idx)
        el = jnp.where(valid, logits, ninf)
        ee = jnp.exp(el - jnp.max(el, axis=-1, keepdims=True))
        prob = jnp.where(valid, ee / jnp.sum(ee, axis=-1, keepdims=True), -1.0)
        p1 = jnp.max(prob, axis=-1, keepdims=True)
        i1 = jnp.min(jnp.where(prob == p1, lane, ROUTE_W), axis=-1, keepdims=True)
        prob2 = jnp.where(lane == i1, -1.0, prob)
        p2 = jnp.max(prob2, axis=-1, keepdims=True)
        i2 = jnp.min(jnp.where(prob2 == p2, lane, ROUTE_W), axis=-1, keepdims=True)
        den = p1 + p2
        e_ref[...] = jnp.where(lane == 0, i1 - N_GROUPS, jnp.where(lane == 1, i2 - N_GROUPS, 0))
        p_ref[...] = jnp.where(lane == 0, g_p * p1 / den, jnp.where(lane == 1, g_p * p2 / den, 0.0))

    in_specs = [pl.BlockSpec((tm, D), lambda m: (clamp(m), 0)),
                pl.BlockSpec((1, D), lambda m: (0, 0)),
                mod.row_spec(3, tm, clamp), mod.row_spec(4, tm, clamp),
                pl.BlockSpec((D, ROUTE_W), lambda m: (0, 0)),
                pl.BlockSpec((1, ROUTE_W), lambda m: (0, 0))]
    args = [x, g.reshape(1, D), mod.arr, mod.arr, w_r, b_r]
    aliases = {}
    if bufs is not None:
        in_specs += [pl.BlockSpec(memory_space=pl.ANY)] * 3
        aliases = {len(args) + i: i for i in range(3)}
        args += list(bufs)
    b0 = row0 // tm
    return pl.pallas_call(
        body,
        grid=(n_steps,),
        in_specs=in_specs,
        out_specs=[pl.BlockSpec((tm * H2_PITCH, LANE), lambda m: (b0 + m, 0)),
                   pl.BlockSpec((tm, ROUTE_W), lambda m: (b0 + m, 0)),
                   pl.BlockSpec((tm, ROUTE_W), lambda m: (b0 + m, 0))],
        out_shape=[jax.ShapeDtypeStruct((t_all * H2_PITCH, LANE), f32), jax.ShapeDtypeStruct((t_all, ROUTE_W), i32),
                   jax.ShapeDtypeStruct((t_all, ROUTE_W), f32)],
        input_output_aliases=aliases,
        compiler_params=_cparams(("arbitrary",)),
    )(*args)


TOK_BITS = 14


def _moe_plan(eid, T):
    A = T * TOP_K
    RT = _moe_rt(T)
    nt = pl.cdiv(A, RT) + N_EXPERTS
    flat_e = eid[:, :TOP_K].reshape(-1)
    order = jnp.argsort(flat_e).astype(i32)
    counts = jnp.sum((flat_e[:, None] == jnp.arange(N_EXPERTS, dtype=i32)[None, :]).astype(i32), axis=0)
    start = jnp.cumsum(counts) - counts
    pcnt = (counts + RT - 1) // RT * RT
    pend = jnp.cumsum(pcnt)
    tile_e = jnp.minimum(jnp.sum((pend[None, :] <= (jnp.arange(nt, dtype=i32) * RT)[:, None]).astype(i32), axis=1),
                         N_EXPERTS - 1)
    eidx = jnp.arange(N_EXPERTS, dtype=i32)
    is_e = tile_e[:, None] == eidx[None, :]
    per_tile = lambda tbl: jnp.sum(jnp.where(is_e, tbl[None, :], 0), axis=1)
    t_cnt, t_start = per_tile(counts), per_tile(start)
    off = (jnp.arange(nt, dtype=i32) * RT - per_tile(pend - pcnt))[:, None] + jnp.arange(RT, dtype=i32)[None, :]
    real = off < t_cnt[:, None]
    srt = t_start[:, None] + jnp.minimum(off, t_cnt[:, None])
    a = order[jnp.minimum(srt, A - 1)]
    slot = jnp.arange(nt * RT, dtype=i32).reshape(nt, RT)
    tok = jnp.where(real, a // TOP_K, 0)
    dst = jnp.where(real, a, A + slot - srt)
    n_used = (pend[-1] // RT).astype(i32).reshape(1)
    in_use = counts > 0
    rank = jnp.cumsum(in_use.astype(i32)) - 1
    later = (eidx[None, :] > eidx[:, None]) & in_use[None, :]
    nxt = jnp.min(jnp.where(later, eidx[None, :], N_EXPERTS), axis=1)
    nxt = jnp.where(nxt < N_EXPERTS, nxt, -1)
    tile_info = jnp.concatenate([tile_e, per_tile(rank % 2), per_tile(nxt)]).astype(i32)
    return (tok | (dst << TOK_BITS)).reshape(-1), tile_info, n_used


def _experts(h2, packed, tile_e, n_used, l, w1, w3, w2):
    T = h2.shape[0] // H2_PITCH
    assert TOP_K == 2 and T < (1 << TOK_BITS)
    RT = _moe_rt(T)
    PITCH = SLAB_PITCH
    nt = tile_e.shape[0] // 3
    prime_id = nt * RT
    out_rows = (prime_id + RT) // TOP_K
    RING = 4
    HC, OC = 128, 256
    n_hc, n_oc = D_EXPERT // HC, D // OC
    g_per, s_per = RT // n_hc, RT // n_oc

    def body(tile_ref, nused_ref, slot_ref, h_hbm, w1_hbm, w3_hbm, w2_hbm, o_hbm, *scratch):
        xs, os_ = scratch[:RING], scratch[RING:2 * RING]
        w1b, w3b, w2b, wf1, wf3, wf2, gsem, ssem, wsem = scratch[2 * RING:]
        x0 = xs[0]
        j = pl.program_id(0)
        n_used = nused_ref[0]
        active = j < n_used

        def gather_row(b, r, tok):
            src = pl.multiple_of(tok * H2_PITCH, SUB)
            return pltpu.make_async_copy(h_hbm.at[pl.ds(src, SLAB), :], xs[b].at[pl.ds(r * PITCH, SLAB), :],
                                         gsem.at[b])

        def scatter_row(b, r, d):
            dst = pl.multiple_of(lax.shift_right_logical(d, 1) * O2_PITCH + (d & (TOP_K - 1)) * SLAB, SUB)
            return pltpu.make_async_copy(os_[b].at[pl.ds(r * PITCH, SLAB), :], o_hbm.at[pl.ds(dst, SLAB), :],
                                         ssem.at[b])

        def wait_rows(sem):
            pltpu.make_async_copy(h_hbm.at[pl.ds(0, RT * SLAB), :], x0.at[pl.ds(0, RT * SLAB), :], sem).wait()

        tok_of = lambda s: s & ((1 << TOK_BITS) - 1)
        dst_of = lambda s: lax.shift_right_logical(s, TOK_BITS)
        last_tile = n_used - 1

        @pl.when(j == 0)
        def _():
            os_[RING - 1][...] = jnp.zeros_like(os_[RING - 1])
            for b in range(RING - 1):
                base = jnp.minimum(b, last_tile) * RT

                def one(r, _, b=b, base=base):
                    gather_row(b, r, tok_of(slot_ref[base + r])).start()
                    return 0
                lax.fori_loop(0, RT, one, 0)

        def fetch(e, s):
            return [pltpu.make_async_copy(w_hbm.at[l, e], wf.at[s], wsem.at[s])
                    for w_hbm, wf in ((w1_hbm, wf1), (w3_hbm, wf3), (w2_hbm, wf2))]

        @pl.when(j == 0)
        def _():
            for c in fetch(tile_ref[0], 0):
                c.start()

        @pl.when(active & ((j == 0) | (tile_ref[j] != tile_ref[jnp.maximum(j - 1, 0)])))
        def _():
            ws, e_nxt = tile_ref[nt + j], tile_ref[2 * nt + j]
            for c in fetch(tile_ref[j], ws):
                c.wait()
            w1b[...] = wf1[ws].astype(bf16)
            w3b[...] = wf3[ws].astype(bf16)
            w2b[...] = wf2[ws].astype(bf16)

            @pl.when(e_nxt >= 0)
            def _():
                for c in fetch(e_nxt, 1 - ws):
                    c.start()

        def tile(cur):
            nxt_b = prv_b = (cur + RING - 1) % RING

            @pl.when(j >= RING - 1)
            def _():
                wait_rows(ssem.at[cur])

            wait_rows(gsem.at[cur])
            xb = jnp.concatenate([xs[cur][pl.ds(s, RT, stride=PITCH), :] for s in range(SLAB)],
                                 axis=1).astype(bf16)
            nxt = jnp.minimum(j + RING - 1, last_tile) * RT
            prv = jnp.maximum(j - 1, 0) * RT
            parts = []
            for c in range(n_hc):
                h1 = _dot(xb, w1b[:, c * HC:(c + 1) * HC])
                h3 = _dot(xb, w3b[:, c * HC:(c + 1) * HC])
                parts.append(((h1 * jax.nn.sigmoid(h1)) * h3).astype(bf16))
                for r in range(c * g_per, (c + 1) * g_per):
                    gather_row(nxt_b, r, tok_of(slot_ref[nxt + r])).start(priority=r % 2)
            hm = jnp.concatenate(parts, axis=1)
            for c in range(n_oc):
                res = _dot(hm, w2b[:, c * OC:(c + 1) * OC])
                for t in range(OC // LANE):
                    os_[cur][pl.ds(c * (OC // LANE) + t, RT, stride=PITCH), :] = res[:, t * LANE:(t + 1) * LANE]
                for r in range(c * s_per, (c + 1) * s_per):
                    d = jnp.where(j == 0, prime_id + r, dst_of(slot_ref[prv + r]))
                    scatter_row(prv_b, r, d).start(priority=r % 2)

        def last(cur):
            def one(r, _):
                scatter_row(cur, r, dst_of(slot_ref[j * RT + r])).start()
                return 0
            lax.fori_loop(0, RT, one, 0)
            wait_rows(ssem.at[cur])
            for back in range(1, RING):
                pl.when(j >= back - 1)(lambda b=(cur - back) % RING: wait_rows(ssem.at[b]))
            for ahead in range(1, RING):
                wait_rows(gsem.at[(cur + ahead) % RING])

        for cur in range(RING):
            pl.when(active & (j % RING == cur))(lambda cur=cur: tile(cur))
        for cur in range(RING):
            pl.when((j == last_tile) & (j % RING == cur))(lambda cur=cur: last(cur))

    grid_spec = pltpu.PrefetchScalarGridSpec(
        num_scalar_prefetch=3,
        grid=(nt,),
        in_specs=[pl.BlockSpec(memory_space=pl.ANY)] * 4,
        out_specs=pl.BlockSpec(memory_space=pl.ANY),
        scratch_shapes=[pltpu.VMEM((RT * PITCH, LANE), f32)] * (2 * RING) + [
                        pltpu.VMEM((D, D_EXPERT), bf16), pltpu.VMEM((D, D_EXPERT), bf16),
                        pltpu.VMEM((D_EXPERT, D), bf16),
                        pltpu.VMEM((2, D, D_EXPERT), f32), pltpu.VMEM((2, D, D_EXPERT), f32),
                        pltpu.VMEM((2, D_EXPERT, D), f32),
                        pltpu.SemaphoreType.DMA((RING,)), pltpu.SemaphoreType.DMA((RING,)),
                        pltpu.SemaphoreType.DMA((2,))],
    )
    return pl.pallas_call(
        body,
        grid_spec=grid_spec,
        out_shape=jax.ShapeDtypeStruct((out_rows * O2_PITCH, LANE), f32),
        compiler_params=_cparams(("arbitrary",)),
    )(tile_e, n_used, packed, h2, w1, w3, w2)


def _moe_combine(x, o2, wgt, mod, norm_final, row0):
    T = x.shape[0]
    tm = min(mod.tm, 512)
    b0 = row0 // tm

    def body(x_ref, o_ref, p_ref, m_ref, *rest):
        p = p_ref[...]
        row = lambda k: jnp.concatenate(
            [o_ref[pl.ds(k * SLAB + s, tm, stride=O2_PITCH), :] for s in range(SLAB)], axis=1)
        y = p[:, 0:1] * row(0)
        for k in range(1, TOP_K):
            y = y + p[:, k:k + 1] * row(k)
        xo = x_ref[...] + m_ref[...] * y
        if norm_final is None:
            rest[0][...] = xo
        else:
            rest[1][...] = _rms(xo, rest[0][...])

    ins = [x, o2, wgt, mod.arr]
    specs = [pl.BlockSpec((tm, D), lambda m: (m, 0)),
             pl.BlockSpec((tm * O2_PITCH, LANE), lambda m: (b0 + m, 0)),
             pl.BlockSpec((tm, ROUTE_W), lambda m: (b0 + m, 0)), mod.row_spec(5, tm)]
    if norm_final is not None:
        ins.append(norm_final.reshape(1, D))
        specs.append(pl.BlockSpec((1, D), lambda m: (0, 0)))
    return pl.pallas_call(
        body,
        grid=(T // tm,),
        in_specs=specs,
        out_specs=pl.BlockSpec((tm, D), lambda m: (m, 0)),
        out_shape=jax.ShapeDtypeStruct((T, D), f32),
        compiler_params=_cparams(("parallel",)),
    )(*ins)


def _moe(groups, p, l, big, norm_final):
    pad = ROUTE_W - N_GROUPS - N_EXPERTS
    w_r = jnp.pad(jnp.concatenate([p['moe_w_group'], p['moe_w_router']], axis=1), ((0, 0), (0, pad)))
    b_r = jnp.pad(jnp.concatenate([p['moe_b_group'], p['moe_b_router']]), (0, pad)).reshape(1, ROUTE_W)
    t_all = sum(g.T for g in groups)
    bufs, row0 = None, 0
    for g in groups:
        g.row0 = row0
        bufs = _router(g.x, p['norm_ffn'], g.mod, w_r, b_r, row0, t_all, bufs)
        row0 += g.T
    h2, eid, wgt = bufs
    packed, tile_e, n_used = _moe_plan(eid, t_all)
    o2 = _experts(h2, packed, tile_e, n_used, l, big['moe_w1'], big['moe_w3'], big['moe_w2'])
    for g in groups:
        g.x = _moe_combine(g.x, o2, wgt, g.mod, norm_final, g.row0)


class _Group:
    def __init__(self, x, mods, states, pos0):
        self.B, self.L, _ = x.shape
        self.T = self.B * self.L
        self.tm = _row_tile(self.T)
        self.x = x.reshape(self.T, D)
        self.mods, self.states, self.pos0 = mods, states, pos0
        self.outs = ([], [], [], [], [])
        self.ret_all = self.rw_all = None

    def mix(self, l, p, rp, sp, layers, big, w_in_t):
        B, L = self.B, self.L
        self.mod = mod = _Mod(self.mods[l], L, self.tm)
        proj = _in_proj(self.x, layers['norm_mix'], l, mod, w_in_t)
        if self.states is None:
            y_ret, s_ret = _retention_seq(proj, B, L)
            y_rw, s_rw, shift = _rwkv_seq(proj, rp, B, L)
            y_s5, s_re, s_im = _s5_seq(proj, sp, B, L)
        else:
            st_ret, st_rw, st_shift, st_re, st_im = self.states
            y_ret, self.ret_all = _retention_step(proj, st_ret, l, self.ret_all, self.pos0)
            y_rw, self.rw_all = _rwkv_step(proj, st_shift[l], jnp.transpose(st_rw, (0, 2, 3, 4, 1)), l,
                                           self.rw_all, rp)
            s_ret = s_rw = None
            shift = proj[:, OFF_RW:OFF_RW + RWKV_PROJ]
            y_s5, s_re, s_im = _s5_step(proj, st_re[l], st_im[l], sp)
        z = _glu_proj(y_s5, l, big['s5_w_glu'])
        merged = _merge_proj(y_ret, y_rw, z, l, big['ret_w_o'], big['rwkv_w_o'], big['s5_w_o'], proj)
        self.x = _out_proj(merged, l, big['w_out'], self.x, mod)
        for lst, val in zip(self.outs, (s_ret, s_rw, shift, s_re, s_im)):
            lst.append(val)

    def results(self):
        stacked = [jnp.stack(o) if o[0] is not None else None for o in self.outs]
        if self.states is not None:
            stacked[0], stacked[1] = self.ret_all, jnp.transpose(self.rw_all, (0, 4, 1, 2, 3))
        return self.x.reshape(self.B, self.L, D), stacked


def _trunk(groups, layers, norm_final):
    big = {k: layers[k] for k in _BIG}
    w_in_t = jnp.swapaxes(layers['w_in'], 1, 2)
    for l in range(DEPTH):
        p = {name: arr[l] for name, arr in layers.items() if name not in _BIG + ('w_in',)}
        rp, sp = _rwkv_params(p), _s5_params(p)
        for g in groups:
            g.mix(l, p, rp, sp, layers, big, w_in_t)
        _moe(groups, p, l, big, norm_final if l == DEPTH - 1 else None)
    return [g.results() for g in groups]


_BIG = ('ret_w_o', 'rwkv_w_o', 's5_w_glu', 's5_w_o', 'w_out', 'moe_w1', 'moe_w3', 'moe_w2')


def kernel(x_prompt, x_sample, state_ret, state_rwkv, state_shift, state_s5_re, state_s5_im,
           c_prompt, c_sample, norm_mix, norm_ffn, w_ada, b_ada, w_in, ret_w_o, rwkv_mu, rwkv_w0,
           rwkv_w2, rwkv_a0, rwkv_a2, rwkv_g2, rwkv_k_k, rwkv_k_a, rwkv_r_k, rwkv_ln_w, rwkv_ln_b,
           rwkv_w_o, s5_a_re, s5_a_im, s5_b_re, s5_b_im, s5_c_re, s5_c_im, s5_d, s5_log_dt, s5_w_glu,
           s5_w_o, w_out, moe_w_group, moe_b_group, moe_w_router, moe_b_router, moe_w1, moe_w3, moe_w2,
           norm_final):
    layers = {
        'norm_mix': norm_mix, 'norm_ffn': norm_ffn, 'w_in': w_in,
        'ret_w_o': ret_w_o, 'rwkv_mu': rwkv_mu, 'rwkv_w0': rwkv_w0, 'rwkv_w2': rwkv_w2,
        'rwkv_a0': rwkv_a0, 'rwkv_a2': rwkv_a2, 'rwkv_g2': rwkv_g2, 'rwkv_k_k': rwkv_k_k,
        'rwkv_k_a': rwkv_k_a, 'rwkv_r_k': rwkv_r_k, 'rwkv_ln_w': rwkv_ln_w, 'rwkv_ln_b': rwkv_ln_b,
        'rwkv_w_o': rwkv_w_o, 's5_a_re': s5_a_re, 's5_a_im': s5_a_im, 's5_b_re': s5_b_re,
        's5_b_im': s5_b_im, 's5_c_re': s5_c_re, 's5_c_im': s5_c_im, 's5_d': s5_d,
        's5_log_dt': s5_log_dt, 's5_w_glu': s5_w_glu, 's5_w_o': s5_w_o, 'w_out': w_out,
        'moe_w_group': moe_w_group, 'moe_b_group': moe_b_group, 'moe_w_router': moe_w_router,
        'moe_b_router': moe_b_router, 'moe_w1': moe_w1, 'moe_w3': moe_w3, 'moe_w2': moe_w2,
    }
    Bp, Bs = x_prompt.shape[0], x_sample.shape[0]
    s_off = -(-Bp // 8) * 8
    c_all = jnp.concatenate([c_prompt, jnp.zeros((s_off - Bp, D), f32), c_sample], axis=0)
    mod_all = _adaln(c_all, w_ada, b_ada)
    prompt = _Group(x_prompt, mod_all[:, :Bp], None, 0.0)
    sample = _Group(x_sample, mod_all[:, s_off:s_off + Bs],
                    (state_ret, state_rwkv, state_shift, state_s5_re, state_s5_im), float(PAST_LEN))
    ((y_prompt, (ret_p, rwkv_p, shift_p, s5re_p, s5im_p)),
     (y_sample, (ret_s, rwkv_s, shift_s, s5re_s, s5im_s))) = _trunk([prompt, sample], layers, norm_final)
    return (y_prompt, y_sample, ret_p, ret_s, rwkv_p, rwkv_s, shift_p, shift_s, s5re_p, s5re_s, s5im_p, s5im_s)
```

```python
import jax
import jax.numpy as jnp
from jax import lax
from jax.experimental import pallas as pl
from jax.experimental.pallas import tpu as pltpu

f32 = jnp.float32
bf16 = jnp.bfloat16
i32 = jnp.int32

D = 2048
DEPTH = 2
PAST_LEN = 16384
RET_W, RET_H, RET_DK, RET_DV, RET_CHUNK = 1024, 4, 256, 256, 128
RET_GN_EPS = 1e-6
ROPE_BASE = 10000.0
RWKV_W, RWKV_N, RWKV_H = 1024, 64, 16
DECAY_LORA, AAA_LORA, GATE_LORA = 64, 64, 160
RWKV_PROJ = 3 * RWKV_W + DECAY_LORA + AAA_LORA + GATE_LORA
RWKV_LN_EPS = 64e-5
S5_W, S5_GC, S5_G, S5_P = 1024, 16, 64, 64
N_MOD = 6
RMS_EPS = 1e-6
N_GROUPS, EPG, N_EXPERTS, TOP_K, D_EXPERT = 4, 8, 32, 2, 512
IN_W = 4 * RET_W + RWKV_PROJ + S5_W + 3 * D
OFF_Q, OFF_K, OFF_V, OFF_G, OFF_RW = 0, 1024, 2048, 3072, 4096
OFF_U = OFF_RW + RWKV_PROJ
OFF_GATE = OFF_U + S5_W
LANE = 128
PROJ_W = ((IN_W + LANE - 1) // LANE) * LANE
VMEM_LIMIT = 56 * 1024 * 1024


def _cparams(sem):
    return pltpu.CompilerParams(dimension_semantics=sem, vmem_limit_bytes=VMEM_LIMIT)


def _dot(a, b):
    return jnp.dot(a, b, preferred_element_type=f32)


def _dot_nt(a, b):
    return lax.dot_general(a, b, (((1,), (1,)), ((), ())), preferred_element_type=f32)


def _dot_tn(a, b):
    return lax.dot_general(a, b, (((0,), (0,)), ((), ())), preferred_element_type=f32)


def _rms(x, g):
    return x * lax.rsqrt(jnp.mean(x * x, axis=-1, keepdims=True) + RMS_EPS) * g


def _head_norm(y, eps):
    mu = jnp.mean(y, axis=-1, keepdims=True)
    yc = y - mu
    return yc * lax.rsqrt(jnp.mean(yc * yc, axis=-1, keepdims=True) + eps)


def _row_tile(T):
    return 1024 if T >= 1024 else T


class _Mod:
    def __init__(self, mod, L, tm):
        self.L, self.tm = L, tm
        self.per_token = L == 1
        self.arr = mod if self.per_token else mod.reshape(mod.shape[0], 1, N_MOD * D)

    def spec(self, j, tn, col_of):
        nb = D // tn
        if self.per_token:
            return pl.BlockSpec((self.tm, tn), lambda m, n: (m, j * nb + col_of(n)))
        L, tm = self.L, self.tm
        return pl.BlockSpec((None, 1, tn), lambda m, n: ((m * tm) // L, 0, j * nb + col_of(n)))

    def row_spec(self, j, tm, remap=lambda m: m):
        if self.per_token:
            return pl.BlockSpec((tm, D), lambda m: (remap(m), j))
        L = self.L
        return pl.BlockSpec((None, 1, D), lambda m: ((remap(m) * tm) // L, 0, j))


def _fused_mm(x_ops, w_ops, e_ops, pre, post, *, grid, out_specs, out_shape, cache_shapes):
    nx, nw, ne = len(x_ops), len(w_ops), len(e_ops)
    n_out = len(out_shape)

    def body(*refs):
        x_refs = refs[:nx]
        w_refs = refs[nx:nx + nw]
        e_refs = refs[nx + nw:nx + nw + ne]
        o_refs = refs[nx + nw + ne:nx + nw + ne + n_out]
        caches = refs[nx + nw + ne + n_out:]
        if cache_shapes:
            @pl.when(pl.program_id(1) == 0)
            def _():
                for i in range(nx):
                    caches[i][...] = pre(i, x_refs[i], e_refs).astype(bf16)
            lhs = [c[...] for c in caches]
        else:
            lhs = [x[...] for x in x_refs]
        prods = [(_dot_nt if len(w_ops[j]) > 3 and w_ops[j][3] else _dot)(
            lhs[w_ops[j][2]], w_refs[j][...].astype(bf16)) for j in range(nw)]
        for o_ref, o in zip(o_refs, post(prods, e_refs)):
            o_ref[...] = o.astype(o_ref.dtype)

    return pl.pallas_call(
        body,
        grid=grid,
        in_specs=[s for _, s in x_ops] + [w[1] for w in w_ops] + [s for _, s in e_ops],
        out_specs=out_specs,
        out_shape=out_shape,
        scratch_shapes=[pltpu.VMEM(s, bf16) for s in cache_shapes],
        compiler_params=_cparams(("parallel", "arbitrary")),
    )(*[a for a, _ in x_ops], *[w[0] for w in w_ops], *[a for a, _ in e_ops])


def _adaln(c_all, w_ada, b_ada):
    R = c_all.shape[0]
    tn = 1024

    def pre(i, x_ref, e_refs):
        c = x_ref[...]
        return c * jax.nn.sigmoid(c)

    def post(prods, e_refs):
        return (prods[0] + e_refs[0][...],)

    (out,) = _fused_mm(
        [(c_all, pl.BlockSpec((R, D), lambda l, n: (0, 0)))],
        [(w_ada, pl.BlockSpec((None, D, tn), lambda l, n: (l, 0, n)), 0)],
        [(b_ada.reshape(DEPTH, 1, N_MOD * D), pl.BlockSpec((None, 1, tn), lambda l, n: (l, 0, n)))],
        pre, post,
        grid=(DEPTH, N_MOD * D // tn),
        out_specs=[pl.BlockSpec((None, R, tn), lambda l, n: (l, 0, n))],
        out_shape=[jax.ShapeDtypeStruct((DEPTH, R, N_MOD * D), f32)],
        cache_shapes=[(R, D)],
    )
    return out


def _in_proj(x, g, l, mod, w_in_t):
    T = x.shape[0]
    tm, tn = mod.tm, 1024
    tr = min(tm, 512)

    def norm_body(x_ref, g_ref, shift_ref, scale_ref, h_ref):
        h_ref[...] = (_rms(x_ref[...], g_ref[...]) * (1.0 + scale_ref[...]) + shift_ref[...]).astype(bf16)

    h = pl.pallas_call(
        norm_body,
        grid=(T // tr,),
        in_specs=[pl.BlockSpec((tr, D), lambda m: (m, 0)),
                  pl.BlockSpec((None, 1, D), lambda m: (l, 0, 0)),
                  mod.row_spec(0, tr), mod.row_spec(1, tr)],
        out_specs=pl.BlockSpec((tr, D), lambda m: (m, 0)),
        out_shape=jax.ShapeDtypeStruct((T, D), bf16),
        compiler_params=_cparams(("parallel",)),
    )(x, g.reshape(DEPTH, 1, D), mod.arr, mod.arr)

    def mm_body(h_ref, w_ref, o_ref, wb):
        @pl.when(pl.program_id(1) == 0)
        def _():
            wb[...] = w_ref[...].astype(bf16)

        col = pl.program_id(0) * tn + lax.broadcasted_iota(i32, (tm, tn), 1)
        o_ref[...] = jnp.where(col < IN_W, _dot_nt(h_ref[...], wb[...]), 0.0)

    return pl.pallas_call(
        mm_body,
        grid=(pl.cdiv(PROJ_W, tn), T // tm),
        in_specs=[pl.BlockSpec((tm, D), lambda n, m: (m, 0)),
                  pl.BlockSpec((None, tn, D), lambda n, m: (l, n, 0))],
        out_specs=pl.BlockSpec((tm, tn), lambda n, m: (m, n)),
        out_shape=jax.ShapeDtypeStruct((T, PROJ_W), f32),
        scratch_shapes=[pltpu.VMEM((tn, D), bf16)],
        compiler_params=_cparams(("parallel", "arbitrary")),
    )(h, w_in_t)


def _glu_proj(yg, l, w_glu):
    T = yg.shape[0]
    tm, tn = _row_tile(T), 512
    nb = S5_W // tn

    def post(prods, e_refs):
        return (prods[0] * jax.nn.sigmoid(prods[1]),)

    (out,) = _fused_mm(
        [(yg, pl.BlockSpec((tm, S5_W), lambda m, n: (m, 0)))],
        [(w_glu, pl.BlockSpec((None, S5_W, tn), lambda m, n: (l, 0, n)), 0),
         (w_glu, pl.BlockSpec((None, S5_W, tn), lambda m, n: (l, 0, nb + n)), 0)],
        [], None, post,
        grid=(T // tm, nb),
        out_specs=[pl.BlockSpec((tm, tn), lambda m, n: (m, n))],
        out_shape=[jax.ShapeDtypeStruct((T, S5_W), bf16)],
        cache_shapes=[],
    )
    return out


def _merge_proj(y_ret, y_rw, y_s5, l, w_ret, w_rw, w_s5, proj):
    T = y_ret.shape[0]
    tm, tn = _row_tile(T), 512
    lead = OFF_GATE % LANE
    base = OFF_GATE - lead

    def gate_spec(i):
        return pl.BlockSpec((pl.Element(tm), pl.Element(tn + LANE)),
                            lambda m, n: (pl.multiple_of(m * tm, tm), pl.multiple_of(base + i * D + n * tn, LANE)))

    def post(prods, e_refs):
        acc = None
        for p, e in zip(prods, e_refs):
            t = jax.nn.sigmoid(e[:, lead:lead + tn]) * p
            acc = t if acc is None else acc + t
        return (acc,)

    xspec = pl.BlockSpec((tm, RET_W), lambda m, n: (m, 0))
    wspec = pl.BlockSpec((None, RET_W, tn), lambda m, n: (l, 0, n))
    (out,) = _fused_mm(
        [(y_ret, xspec), (y_rw, xspec), (y_s5, xspec)],
        [(w_ret, wspec, 0), (w_rw, wspec, 1), (w_s5, wspec, 2)],
        [(proj, gate_spec(0)), (proj, gate_spec(1)), (proj, gate_spec(2))],
        None, post,
        grid=(T // tm, D // tn),
        out_specs=[pl.BlockSpec((tm, tn), lambda m, n: (m, n))],
        out_shape=[jax.ShapeDtypeStruct((T, D), bf16)],
        cache_shapes=[],
    )
    return out


def _out_proj(merged, l, w_out, x, mod):
    T = x.shape[0]
    tm, tn = mod.tm, 512

    def post(prods, e_refs):
        x_ref, m_ref = e_refs
        return (x_ref[...] + m_ref[...] * prods[0],)

    (out,) = _fused_mm(
        [(merged, pl.BlockSpec((tm, D), lambda m, n: (m, 0)))],
        [(w_out, pl.BlockSpec((None, D, tn), lambda m, n: (l, 0, n)), 0)],
        [(x, pl.BlockSpec((tm, tn), lambda m, n: (m, n))), (mod.arr, mod.spec(2, tn, lambda n: n))],
        None, post,
        grid=(T // tm, D // tn),
        out_specs=[pl.BlockSpec((tm, tn), lambda m, n: (m, n))],
        out_shape=[jax.ShapeDtypeStruct((T, D), f32)],
        cache_shapes=[],
    )
    return out


def _ret_consts(L, pos0):
    C = RET_CHUNK if L % RET_CHUNK == 0 else L
    H = RET_H
    log_g = jnp.log1p(-jnp.exp2(-5.0 - jnp.arange(H, dtype=f32)))
    i = jnp.arange(C, dtype=f32)
    diff = i[:, None] - i[None, :]
    causal = diff >= 0
    dmask = jnp.where(causal, jnp.exp(jnp.where(causal, diff, 0.0)[None] * log_g[:, None, None]), 0.0)
    kdec = jnp.exp((C - 1.0 - i)[:, None] * log_g[None, :])
    qdec = jnp.exp((i + 1.0)[:, None] * log_g[None, :])
    g_chunk = jnp.exp(C * log_g)
    half = RET_DK // 2
    inv = ROPE_BASE ** (-jnp.arange(half, dtype=f32) / half)
    pos = pos0 + jnp.arange(L, dtype=f32)
    ang = pos[:, None] * inv[None, :]
    return C, dmask, kdec, qdec, g_chunk, jnp.cos(ang), jnp.sin(ang)


def _rotary(x, cos, sin):
    half = RET_DK // 2
    x1, x2 = x[..., :half], x[..., half:]
    return jnp.concatenate([x1 * cos - x2 * sin, x1 * sin + x2 * cos], axis=-1)


def _retention_seq(proj, B, L):
    C, dmask, kdec, qdec, g_chunk, cos, sin = _ret_consts(L, 0.0)
    H, dk = RET_H, RET_DK
    n = L // C
    kdec_f = jnp.broadcast_to(kdec.T[:, :, None], (H, C, dk))
    qdec_f = jnp.broadcast_to(qdec.T[:, :, None], (H, C, dk))
    gch_f = jnp.broadcast_to(g_chunk[:, None, None], (H, 8, dk))
    proj3 = proj.reshape(B, L, PROJ_W)

    def body(q_ref, k_ref, v_ref, g_ref, cos_ref, sin_ref, dm_ref, kd_ref, qd_ref, gc_ref, y_ref, s_ref, st):
        c = pl.program_id(1)

        @pl.when(c == 0)
        def _():
            st[...] = jnp.zeros_like(st)

        cs, sn = cos_ref[...], sin_ref[...]
        hs = range(H)
        sl = [slice(h * dk, (h + 1) * dk) for h in hs]
        q = [_rotary(q_ref[:, sl[h]], cs, sn) for h in hs]
        k = [_rotary(k_ref[:, sl[h]], cs, sn) * (dk ** -0.5) for h in hs]
        vb = [v_ref[:, sl[h]].astype(bf16) for h in hs]
        s0 = [st[h] for h in hs]
        scores = [_dot_nt(q[h].astype(bf16), k[h].astype(bf16)) * dm_ref[h] for h in hs]
        cross = [_dot((q[h] * qd_ref[h]).astype(bf16), s0[h].astype(bf16)) for h in hs]
        kv = [_dot_tn((k[h] * kd_ref[h]).astype(bf16), vb[h]) for h in hs]
        o = [_dot(scores[h].astype(bf16), vb[h]) + cross[h] for h in hs]
        for h in hs:
            st[h] = s0[h] * gc_ref[h, 0:1, :] + kv[h]
            g = g_ref[:, sl[h]]
            y_ref[:, sl[h]] = (g * jax.nn.sigmoid(g) * _head_norm(o[h], RET_GN_EPS)).astype(bf16)

        @pl.when(c == n - 1)
        def _():
            s_ref[...] = st[...]

    def seg(off):
        return pl.BlockSpec((None, C, RET_W), lambda b, c: (b, c, off // RET_W))

    const3 = lambda shp: pl.BlockSpec(shp, lambda b, c: (0, 0, 0))
    y, s = pl.pallas_call(
        body,
        grid=(B, n),
        in_specs=[seg(OFF_Q), seg(OFF_K), seg(OFF_V), seg(OFF_G),
                  pl.BlockSpec((C, dk // 2), lambda b, c: (c, 0)),
                  pl.BlockSpec((C, dk // 2), lambda b, c: (c, 0)),
                  const3((H, C, C)), const3((H, C, dk)), const3((H, C, dk)), const3((H, 8, dk))],
        out_specs=[pl.BlockSpec((None, C, RET_W), lambda b, c: (b, c, 0)),
                   pl.BlockSpec((None, H, dk, RET_DV), lambda b, c: (b, 0, 0, 0))],
        out_shape=[jax.ShapeDtypeStruct((B, L, RET_W), bf16),
                   jax.ShapeDtypeStruct((B, H, dk, RET_DV), f32)],
        scratch_shapes=[pltpu.VMEM((H, dk, RET_DV), f32)],
        compiler_params=_cparams(("parallel", "arbitrary")),
    )(proj3, proj3, proj3, proj3, cos, sin, dmask, kdec_f, qdec_f, gch_f)
    return y.reshape(B * L, RET_W), s


STEP_TB = 16


def _layer_grid(l, buf, inner):
    if buf is not None:
        return inner, (lambda fn: (lambda *ix: fn(l, ix, ix)))
    assert l == 0
    last = tuple(n - 1 for n in inner)

    def wrap(fn):
        def index_map(d, *ix):
            parked = tuple(jnp.where(d == l, i, z) for i, z in zip(ix, last))
            return fn(d, ix, parked)
        return index_map
    return (DEPTH,) + inner, wrap


def _retention_step(proj, s_all, l, buf, pos0):
    B = proj.shape[0]
    _, _, _, _, g_chunk, cos, sin = _ret_consts(1, pos0)
    H, dk = RET_H, RET_DK
    gch = jnp.broadcast_to(g_chunk[:, None, None], (H, 8, dk))
    tb = STEP_TB
    grid, wrap = _layer_grid(l, buf, (B // tb, H))

    def body(q_ref, k_ref, v_ref, g_ref, cos_ref, sin_ref, gc_ref, s_ref, *rest):
        y_ref, so_ref = rest[-2:]

        def update():
            cs, sn = cos_ref[...], sin_ref[...]
            q = _rotary(q_ref[...], cs, sn)
            k = _rotary(k_ref[...], cs, sn) * (dk ** -0.5)
            v = v_ref[...]
            s1 = s_ref[...] * gc_ref[0:1, :][None] + k[:, :, None] * v[:, None, :]
            so_ref[...] = s1
            o = jnp.sum(q[:, :, None] * s1, axis=1)
            g = g_ref[...]
            y_ref[...] = (g * jax.nn.sigmoid(g) * _head_norm(o, RET_GN_EPS)).astype(bf16)

        if buf is not None:
            update()
        else:
            pl.when(pl.program_id(0) == l)(update)

            @pl.when(pl.program_id(0) != l)
            def _():
                so_ref[...] = jnp.zeros_like(so_ref)

    def seg(off):
        return pl.BlockSpec((tb, dk), wrap(lambda d, ix, pk: (pk[0], off // dk + pk[1])))

    const = lambda d, ix, pk: (0, 0)
    in_specs = [seg(OFF_Q), seg(OFF_K), seg(OFF_V), seg(OFF_G),
                pl.BlockSpec((1, dk // 2), wrap(const)), pl.BlockSpec((1, dk // 2), wrap(const)),
                pl.BlockSpec((None, 8, dk), wrap(lambda d, ix, pk: (pk[1], 0, 0))),
                pl.BlockSpec((None, tb, None, dk, RET_DV), wrap(lambda d, ix, pk: (l, pk[0], pk[1], 0, 0)))]
    args = [proj, proj, proj, proj, cos, sin, gch, s_all]
    aliases = {}
    if buf is not None:
        in_specs.append(pl.BlockSpec(memory_space=pl.ANY))
        args.append(buf)
        aliases = {len(args) - 1: 1}
    return pl.pallas_call(
        body,
        grid=grid,
        in_specs=in_specs,
        out_specs=[pl.BlockSpec((tb, dk), wrap(lambda d, ix, pk: pk)),
                   pl.BlockSpec((None, tb, None, dk, RET_DV), wrap(lambda d, ix, pk: (d, ix[0], ix[1], 0, 0)))],
        out_shape=[jax.ShapeDtypeStruct((B, RET_W), bf16),
                   jax.ShapeDtypeStruct((DEPTH, B, H, dk, RET_DV), f32)],
        input_output_aliases=aliases,
        compiler_params=_cparams(("arbitrary",) * len(grid)),
    )(*args)


RW_C = 64
RW_Q = 4
RW_NQ = RWKV_H // RW_Q
RW_GROUP = 16
RW_BLK =((RWKV_PROJ + LANE - 1) // LANE) * LANE
_RW_PKEYS = ('mu', 'w0', 'w2', 'a0', 'a2', 'g2', 'k_k', 'k_a', 'r_k', 'ln_w', 'ln_b')


def _rwkv_params(p):
    return dict(
        mu=jnp.pad(p['rwkv_mu'], (0, RW_BLK - RWKV_PROJ)).reshape(1, RW_BLK),
        w0=p['rwkv_w0'].reshape(1, RWKV_W), w2=p['rwkv_w2'],
        a0=p['rwkv_a0'].reshape(1, RWKV_W), a2=p['rwkv_a2'], g2=p['rwkv_g2'],
        k_k=p['rwkv_k_k'].reshape(1, RWKV_W), k_a=p['rwkv_k_a'].reshape(1, RWKV_W),
        r_k=p['rwkv_r_k'].reshape(1, RWKV_W),
        ln_w=p['rwkv_ln_w'].reshape(1, RWKV_W), ln_b=p['rwkv_ln_b'].reshape(1, RWKV_W))


def _split_bf16(x, terms):
    out = []
    for _ in range(terms - 1):
        hi = x.astype(bf16)
        out.append(hi)
        x = x - hi.astype(f32)
    out.append(x.astype(bf16))
    return out


def _head_sum(x):
    QW = RW_Q * RWKV_N
    r = lax.broadcasted_iota(i32, (QW, QW), 0) // RWKV_N
    c = lax.broadcasted_iota(i32, (QW, QW), 1) // RWKV_N
    ones = (r == c).astype(bf16)
    parts = _split_bf16(x, 2)
    outs = []
    for q in range(RW_NQ):
        sl = slice(q * QW, (q + 1) * QW)
        outs.append(_dot(parts[0][:, sl], ones) + _dot(parts[1][:, sl], ones))
    return jnp.concatenate(outs, axis=1)


def _rwkv_mix(rw, prev, pr):
    m = rw + (prev - rw) * pr['mu'][...]
    r = m[:, 0:RWKV_W]
    k = m[:, RWKV_W:2 * RWKV_W]
    v = m[:, 2 * RWKV_W:3 * RWKV_W]
    o = 3 * RWKV_W
    xw = m[:, o:o + DECAY_LORA]
    xa = m[:, o + DECAY_LORA:o + DECAY_LORA + AAA_LORA]
    xg = m[:, o + DECAY_LORA + AAA_LORA:o + DECAY_LORA + AAA_LORA + GATE_LORA]
    w_log = -jax.nn.softplus(-(pr['w0'][...] + _dot(jnp.tanh(xw).astype(bf16), pr['w2'][...].astype(bf16)))) - 0.5
    lw = -jnp.exp(w_log)
    a = jax.nn.sigmoid(pr['a0'][...] + _dot(xa.astype(bf16), pr['a2'][...].astype(bf16)))
    g = _dot(jax.nn.sigmoid(xg).astype(bf16), pr['g2'][...].astype(bf16))
    kk = k * pr['k_k'][...]
    kk = kk / jnp.maximum(jnp.sqrt(_head_sum(kk * kk)), 1e-12)
    kf = k * (1.0 + (a - 1.0) * pr['k_a'][...])
    return r, lw, kf, v, kk, a, g


def _rwkv_out(y, r, kf, v, g, pr):
    yc = y - _head_sum(y) * (1.0 / RWKV_N)
    yn = yc * lax.rsqrt(_head_sum(yc * yc) * (1.0 / RWKV_N) + RWKV_LN_EPS)
    yn = yn * pr['ln_w'][...] + pr['ln_b'][...]
    yn = yn + _head_sum(r * kf * pr['r_k'][...]) * v
    return yn * g


def _rwkv_seq(proj, rp, B, L):
    C = RW_C
    assert L % C == 0 and C == RWKV_N
    n = L // C
    QW = RW_Q * RWKV_N
    BC = B * C
    proj3 = proj.reshape(B, L, PROJ_W)

    def body(rw_ref, *refs):
        pr = dict(zip(_RW_PKEYS, refs[:len(_RW_PKEYS)]))
        y_ref, s_ref, sh_ref, st, carry = refs[len(_RW_PKEYS):]
        c = pl.program_id(0)

        @pl.when(c == 0)
        def _():
            st[...] = jnp.zeros_like(st)
            carry[...] = jnp.zeros_like(carry)

        rw = rw_ref[...].reshape(BC, RW_BLK)
        rolled = pltpu.roll(rw, 1, 0)
        row = lax.broadcasted_iota(i32, (C, RW_BLK), 0)
        prev = jnp.concatenate(
            [jnp.where(row == 0, carry[b, 0:1, :], rolled[b * C:(b + 1) * C]) for b in range(B)], axis=0)
        for b in range(B):
            carry[b, 0:1, :] = rw[(b + 1) * C - 1:(b + 1) * C, :]
        r, lw, kf, v, kk, a, g = _rwkv_mix(rw, prev, pr)

        ti = lax.broadcasted_iota(i32, (BC, BC), 0)
        si = lax.broadcasted_iota(i32, (BC, BC), 1)
        tril = ((si <= ti) & ((si // C) == (ti // C))).astype(bf16)
        lg = sum(_dot(tril, part) for part in _split_bf16(lw, 3))
        lgc = jnp.concatenate(
            [jnp.broadcast_to(lg[(b + 1) * C - 1:(b + 1) * C, :], (C, RWKV_W)) for b in range(B)], axis=0)
        e_neg = jnp.exp(-lg)
        e_rem = jnp.exp(lgc - lg)
        at = kk * jnp.exp(lg - lw)
        ka = kk * a
        bt = ka * e_neg
        kt = kf * e_neg
        rt = r * jnp.exp(lg)
        bh = ka * e_rem
        kh = kf * e_rem
        gcr = jnp.exp(lgc)

        rr = lax.broadcasted_iota(i32, (RW_Q * C, QW), 0)
        ll = lax.broadcasted_iota(i32, (RW_Q * C, QW), 1)
        blockmask = (rr // C) == (ll // RWKV_N)
        tt = lax.broadcasted_iota(i32, (C, QW), 0)
        ss = lax.broadcasted_iota(i32, (C, QW), 1) % C
        strict = ss < tt
        incl = ss <= tt
        eye = (ss == tt).astype(f32)

        def bd(x):
            return jnp.where(blockmask, jnp.concatenate([x] * RW_Q, axis=0), 0.0).astype(bf16)

        ys = [[None] * RW_NQ for _ in range(B)]
        chains = [(b, q) for b in range(B) for q in range(RW_NQ)]
        for g0 in range(0, len(chains), RW_GROUP):
            grp = chains[g0:g0 + RW_GROUP]
            idx = [(slice(b * C, (b + 1) * C), slice(q * QW, (q + 1) * QW)) for b, q in grp]
            each = lambda fn: [fn(i) for i in range(len(grp))]
            vq = each(lambda i: v[idx[i]])
            ar = each(lambda i: jnp.concatenate([at[idx[i]], rt[idx[i]]], axis=0).astype(bf16))
            big = each(lambda i: _dot_nt(ar[i], jnp.concatenate([bd(bt[idx[i]]), bd(kt[idx[i]])], axis=0)))
            s0 = each(lambda i: st[grp[i]])
            asrs = each(lambda i: _dot_nt(ar[i], s0[i].astype(bf16)))
            nmat = each(lambda i: jnp.where(strict, big[i][:C, :QW], 0.0))
            akm = each(lambda i: jnp.where(strict, big[i][:C, QW:], 0.0).astype(bf16))
            rbk = each(lambda i: jnp.concatenate([jnp.where(incl, big[i][C:, :QW], 0.0),
                                                  jnp.where(incl, big[i][C:, QW:], 0.0)], axis=1).astype(bf16))
            tm = each(lambda i: eye - nmat[i])
            pw = each(lambda i: _dot(nmat[i].astype(bf16), bd(nmat[i])))
            lvl = 2
            while lvl < C:
                res = each(lambda i: _dot(jnp.concatenate([tm[i], pw[i]], axis=0).astype(bf16), bd(pw[i])))
                tm = each(lambda i: tm[i] + res[i][:C])
                pw = each(lambda i: res[i][C:])
                lvl *= 2
            vbd = each(lambda i: bd(vq[i]))
            rhs = each(lambda i: -(asrs[i][:C] + _dot(akm[i], vbd[i])))
            u = each(lambda i: _dot(tm[i].astype(bf16), bd(rhs[i])))
            y = each(lambda i: asrs[i][C:] + _dot(rbk[i], jnp.concatenate([bd(u[i]), vbd[i]], axis=0)))
            upd = each(lambda i: _dot_tn(jnp.concatenate([u[i], vq[i]], axis=0).astype(bf16),
                                         jnp.concatenate([bh[idx[i]], kh[idx[i]]], axis=0).astype(bf16)))
            for i, (b, q) in enumerate(grp):
                st[b, q] = s0[i] * gcr[b * C:b * C + 1, idx[i][1]] + jnp.where(blockmask, upd[i], 0.0)
                ys[b][q] = y[i]

        y = jnp.concatenate([jnp.concatenate(yb, axis=1) for yb in ys], axis=0)
        y_ref[...] = _rwkv_out(y, r, kf, v, g, pr).astype(bf16).reshape(B, C, RWKV_W)

        @pl.when(c == n - 1)
        def _():
            for b in range(B):
                for q in range(RW_NQ):
                    for h in range(RW_Q):
                        hs = slice(h * RWKV_N, (h + 1) * RWKV_N)
                        s_ref[b, q * RW_Q + h] = st[b, q, hs, hs]
                sh_ref[b] = rw[(b + 1) * C - 1:(b + 1) * C, 0:RWKV_PROJ]

    pspecs = [pl.BlockSpec(rp[k].shape, lambda c: (0, 0)) for k in _RW_PKEYS]
    y, s, sh = pl.pallas_call(
        body,
        grid=(n,),
        in_specs=[pl.BlockSpec((pl.Element(B), pl.Element(C), pl.Element(RW_BLK)),
                               lambda c: (0, pl.multiple_of(c * C, C), OFF_RW))] + pspecs,
        out_specs=[pl.BlockSpec((B, C, RWKV_W), lambda c: (0, c, 0)),
                   pl.BlockSpec((B, RWKV_H, RWKV_N, RWKV_N), lambda c: (0, 0, 0, 0)),
                   pl.BlockSpec((B, 1, RWKV_PROJ), lambda c: (0, 0, 0))],
        out_shape=[jax.ShapeDtypeStruct((B, L, RWKV_W), bf16),
                   jax.ShapeDtypeStruct((B, RWKV_H, RWKV_N, RWKV_N), f32),
                   jax.ShapeDtypeStruct((B, 1, RWKV_PROJ), f32)],
        scratch_shapes=[pltpu.VMEM((B, RW_NQ, QW, QW), f32), pltpu.VMEM((B, 8, RW_BLK), f32)],
        compiler_params=_cparams(("arbitrary",)),
    )(proj3, *[rp[k] for k in _RW_PKEYS])
    return y.reshape(B * L, RWKV_W), s, sh.reshape(B, RWKV_PROJ)


def _rwkv_step(proj, shift, s_t, l, buf, rp):
    B = proj.shape[0]
    N = RWKV_N
    shift_p = jnp.pad(shift, ((0, 0), (0, RW_BLK - RWKV_PROJ)))
    grid, wrap = _layer_grid(l, buf, (RWKV_H,))
    npk = len(_RW_PKEYS)
    vec_names = ('r', 'w', 'kf', 'v', 'kk', 'ka')

    def body(rw_ref, sh_ref, s_ref, *refs):
        pr = dict(zip(_RW_PKEYS, refs[:npk]))
        y_ref, so_ref = refs[-9:-7]
        vt = dict(zip(vec_names, refs[-7:-1]))
        yt = refs[-1]
        h = pl.program_id(len(grid) - 1)

        def update():
            @pl.when(h == 0)
            def _():
                r, lw, kf, v, kk, a, _ = _rwkv_mix(rw_ref[...], sh_ref[...], pr)
                for name, val in zip(vec_names, (r, jnp.exp(lw), kf, v, kk, kk * a)):
                    vt[name][...] = val.T

            hs = pl.ds(pl.multiple_of(h * N, N), N)
            s = s_ref[...]
            kk_h = vt['kk'][hs, :]
            sa = jnp.sum(s * (-kk_h)[None], axis=1, keepdims=True)
            s1 = s * vt['w'][hs, :][None] + sa * vt['ka'][hs, :][None] + vt['v'][hs, :][:, None, :] * vt['kf'][hs, :][None]
            so_ref[...] = s1
            yt[hs, :] = jnp.sum(s1 * vt['r'][hs, :][None], axis=1)

            @pl.when(h == RWKV_H - 1)
            def _():
                r, lw, kf, v, kk, a, g = _rwkv_mix(rw_ref[...], sh_ref[...], pr)
                y_ref[...] = _rwkv_out(yt[...].T, r, kf, v, g, pr).astype(bf16)

        if buf is not None:
            update()
        else:
            pl.when(pl.program_id(0) == l)(update)

            @pl.when(pl.program_id(0) != l)
            def _():
                so_ref[...] = jnp.zeros_like(so_ref)

    const = lambda d, ix, pk: (0, 0)
    in_specs = [pl.BlockSpec((pl.Element(B), pl.Element(RW_BLK)), wrap(lambda d, ix, pk: (0, OFF_RW))),
                pl.BlockSpec((B, RW_BLK), wrap(const)),
                pl.BlockSpec((None, None, N, N, B), wrap(lambda d, ix, pk: (l, pk[0], 0, 0, 0)))]
    in_specs += [pl.BlockSpec(rp[k].shape, wrap(const)) for k in _RW_PKEYS]
    args = [proj, shift_p, s_t] + [rp[k] for k in _RW_PKEYS]
    aliases = {}
    if buf is not None:
        in_specs.append(pl.BlockSpec(memory_space=pl.ANY))
        args.append(buf)
        aliases = {len(args) - 1: 1}
    return pl.pallas_call(
        body,
        grid=grid,
        in_specs=in_specs,
        out_specs=[pl.BlockSpec((B, RWKV_W), wrap(const)),
                   pl.BlockSpec((None, None, N, N, B), wrap(lambda d, ix, pk: (d, ix[0], 0, 0, 0)))],
        out_shape=[jax.ShapeDtypeStruct((B, RWKV_W), bf16),
                   jax.ShapeDtypeStruct((DEPTH, RWKV_H, N, N, B), f32)],
        scratch_shapes=[pltpu.VMEM((RWKV_W, B), f32)] * 7,
        input_output_aliases=aliases,
        compiler_params=_cparams(("arbitrary",) * len(grid)),
    )(*args)


S5_N = S5_G * S5_P
S5_KC = 256
S5_NKC = S5_W // S5_KC
S5_TILES = S5_N // LANE
S5_GB = S5_KC // S5_GC


def _s5_params(p):
    a_re, a_im = p['s5_a_re'], p['s5_a_im']
    dstep = jnp.exp(p['s5_log_dt'])[:, None]
    mag = jnp.exp(a_re * dstep)
    ab_re = mag * jnp.cos(a_im * dstep)
    ab_im = mag * jnp.sin(a_im * dstep)
    den = a_re * a_re + a_im * a_im
    n_re = ab_re - 1.0
    f_re = (n_re * a_re + ab_im * a_im) / den
    f_im = (ab_im * a_re - n_re * a_im) / den
    b_re, b_im = p['s5_b_re'], p['s5_b_im']
    bb_re = f_re[..., None] * b_re - f_im[..., None] * b_im
    bb_im = f_re[..., None] * b_im + f_im[..., None] * b_re
    eye = jnp.eye(S5_GB, dtype=f32)

    def in_map(bb):
        t = bb.reshape(S5_NKC, S5_GB, S5_P, S5_GC)
        return jnp.einsum('kgpc,gh->kgchp', t, eye).reshape(S5_NKC, S5_KC, S5_GB * S5_P)

    def out_map(cc):
        t = cc.reshape(S5_NKC, S5_GB, S5_GC, S5_P)
        return jnp.einsum('qgcp,gh->qgphc', t, eye).reshape(S5_NKC, S5_GB * S5_P, S5_KC)

    return dict(wb=jnp.concatenate([in_map(bb_re), in_map(bb_im)], axis=-1),
                wc_re=out_map(p['s5_c_re']), wc_im=out_map(p['s5_c_im']),
                ab_re_t=ab_re.reshape(S5_TILES // 8, 8, LANE), ab_im_t=ab_im.reshape(S5_TILES // 8, 8, LANE),
                ab_re=ab_re.reshape(1, S5_N), ab_im=ab_im.reshape(1, S5_N), d=p['s5_d'].reshape(1, S5_W))


def _s5_seq(proj, sp, B, L):
    Lc = min(L, 256)
    n = L // Lc
    pitch = Lc + 4
    lead = OFF_U % LANE
    base = OFF_U - lead
    width = S5_W + LANE
    nt4 = S5_TILES // 8
    tiles_kc = S5_TILES // S5_NKC

    def body(u_ref, wb_ref, wcr_ref, wci_ref, abr_ref, abi_ref, d_ref, y_ref, sr_ref, si_ref, xr, xi, cr, ci):
        c = pl.program_id(1)

        @pl.when(c == 0)
        def _():
            cr[...] = jnp.zeros_like(cr)
            ci[...] = jnp.zeros_like(ci)

        u = u_ref[:, lead:lead + S5_W]
        ub = u.astype(bf16)
        for kc in range(S5_NKC):
            bu = _dot(ub[:, kc * S5_KC:(kc + 1) * S5_KC], wb_ref[kc].astype(bf16))
            for j in range(tiles_kc):
                t = kc * tiles_kc + j
                xr[pl.ds(t * pitch, Lc), :] = bu[:, j * LANE:(j + 1) * LANE]
                xi[pl.ds(t * pitch, Lc), :] = bu[:, (tiles_kc + j) * LANE:(tiles_kc + j + 1) * LANE]

        abr = [abr_ref[g] for g in range(nt4)]
        abi = [abi_ref[g] for g in range(nt4)]

        def step(t, carry):
            out = []
            for g in range(nt4):
                s_r, s_i = carry[2 * g], carry[2 * g + 1]
                idx = pl.ds(g * 8 * pitch + t, 8, stride=pitch)
                n_r = abr[g] * s_r - abi[g] * s_i + xr[idx, :]
                n_i = abr[g] * s_i + abi[g] * s_r + xi[idx, :]
                xr[idx, :] = n_r
                xi[idx, :] = n_i
                out += [n_r, n_i]
            return tuple(out)

        init = []
        for g in range(nt4):
            init += [cr[g], ci[g]]
        fin = lax.fori_loop(0, Lc, step, tuple(init), unroll=2)
        for g in range(nt4):
            cr[g] = fin[2 * g]
            ci[g] = fin[2 * g + 1]

        for q in range(S5_NKC):
            lr = jnp.concatenate([xr[pl.ds((q * tiles_kc + j) * pitch, Lc), :] for j in range(tiles_kc)], axis=1)
            li = jnp.concatenate([xi[pl.ds((q * tiles_kc + j) * pitch, Lc), :] for j in range(tiles_kc)], axis=1)
            y = _dot(lr.astype(bf16), wcr_ref[q].astype(bf16)) - _dot(li.astype(bf16), wci_ref[q].astype(bf16))
            cs = slice(q * S5_KC, (q + 1) * S5_KC)
            y = y + d_ref[:, cs] * u[:, cs]
            y_ref[:, cs] = jax.nn.gelu(y).astype(bf16)

        @pl.when(c == n - 1)
        def _():
            sr_ref[...] = cr[...]
            si_ref[...] = ci[...]

    full = lambda shp: pl.BlockSpec(shp, lambda b, c: (0,) * len(shp))
    y, sr, si = pl.pallas_call(
        body,
        grid=(B, n),
        in_specs=[pl.BlockSpec((pl.Element(Lc), pl.Element(width)),
                               lambda b, c: (pl.multiple_of((b * n + c) * Lc, Lc), base)),
                  full(sp['wb'].shape), full(sp['wc_re'].shape), full(sp['wc_im'].shape),
                  full(sp['ab_re_t'].shape), full(sp['ab_im_t'].shape), full((1, S5_W))],
        out_specs=[pl.BlockSpec((Lc, S5_W), lambda b, c: (b * n + c, 0)),
                   pl.BlockSpec((None, nt4, 8, LANE), lambda b, c: (b, 0, 0, 0)),
                   pl.BlockSpec((None, nt4, 8, LANE), lambda b, c: (b, 0, 0, 0))],
        out_shape=[jax.ShapeDtypeStruct((B * L, S5_W), bf16),
                   jax.ShapeDtypeStruct((B, nt4, 8, LANE), f32),
                   jax.ShapeDtypeStruct((B, nt4, 8, LANE), f32)],
        scratch_shapes=[pltpu.VMEM((S5_TILES * pitch, LANE), f32), pltpu.VMEM((S5_TILES * pitch, LANE), f32),
                        pltpu.VMEM((nt4, 8, LANE), f32), pltpu.VMEM((nt4, 8, LANE), f32)],
        compiler_params=_cparams(("parallel", "arbitrary")),
    )(proj, sp['wb'], sp['wc_re'], sp['wc_im'], sp['ab_re_t'], sp['ab_im_t'], sp['d'])
    return y, sr.reshape(B, S5_G, S5_P), si.reshape(B, S5_G, S5_P)


def _s5_step(proj, x_re, x_im, sp):
    B = proj.shape[0]
    lead = OFF_U % LANE
    base = OFF_U - lead
    width = S5_W + LANE
    kw = S5_N // S5_NKC

    def body(u_ref, xr_ref, xi_ref, wb_ref, wcr_ref, wci_ref, abr_ref, abi_ref, d_ref, y_ref, sr_ref, si_ref):
        u = u_ref[:, lead:lead + S5_W]
        ub = u.astype(bf16)
        abr, abi = abr_ref[...], abi_ref[...]
        xr, xi = xr_ref[...], xi_ref[...]
        for kc in range(S5_NKC):
            bu = _dot(ub[:, kc * S5_KC:(kc + 1) * S5_KC], wb_ref[kc].astype(bf16))
            sl = slice(kc * kw, (kc + 1) * kw)
            n_r = abr[:, sl] * xr[:, sl] - abi[:, sl] * xi[:, sl] + bu[:, :kw]
            n_i = abr[:, sl] * xi[:, sl] + abi[:, sl] * xr[:, sl] + bu[:, kw:]
            sr_ref[:, sl] = n_r
            si_ref[:, sl] = n_i
            y = _dot(n_r.astype(bf16), wcr_ref[kc].astype(bf16)) - _dot(n_i.astype(bf16), wci_ref[kc].astype(bf16))
            cs = slice(kc * S5_KC, (kc + 1) * S5_KC)
            y = y + d_ref[:, cs] * u[:, cs]
            y_ref[:, cs] = jax.nn.gelu(y).astype(bf16)

    full = lambda shp: pl.BlockSpec(shp, lambda i: (0,) * len(shp))
    y, sr, si = pl.pallas_call(
        body,
        grid=(1,),
        in_specs=[pl.BlockSpec((pl.Element(B), pl.Element(width)), lambda i: (0, base)),
                  full((B, S5_N)), full((B, S5_N)),
                  full(sp['wb'].shape), full(sp['wc_re'].shape), full(sp['wc_im'].shape),
                  full((1, S5_N)), full((1, S5_N)), full((1, S5_W))],
        out_specs=[full((B, S5_W)), full((B, S5_N)), full((B, S5_N))],
        out_shape=[jax.ShapeDtypeStruct((B, S5_W), bf16),
                   jax.ShapeDtypeStruct((B, S5_N), f32), jax.ShapeDtypeStruct((B, S5_N), f32)],
        compiler_params=_cparams(("arbitrary",)),
    )(proj, x_re.reshape(B, S5_N), x_im.reshape(B, S5_N), sp['wb'], sp['wc_re'], sp['wc_im'],
      sp['ab_re'], sp['ab_im'], sp['d'])
    return y, sr.reshape(B, S5_G, S5_P), si.reshape(B, S5_G, S5_P)


MOE_RT, MOE_RT_SMALL = 128, 32
SLAB = D // LANE
SUB = 8
SLAB_PITCH = SLAB + SUB
H2_PITCH = SLAB + SUB
O2_PITCH = TOP_K * SLAB + SUB


def _moe_rt(T):
    return MOE_RT if T * TOP_K >= N_EXPERTS * MOE_RT else MOE_RT_SMALL
ROUTE_W = LANE


def _router(x, g, mod, w_r, b_r, row0, t_all, bufs):
    T = x.shape[0]
    tm = min(mod.tm, 512)
    nm = T // tm
    assert row0 % tm == 0
    if bufs is None:
        assert row0 == 0
        n_steps = pl.cdiv(t_all, tm)
    else:
        n_steps = nm
    clamp = lambda m: jnp.minimum(m, nm - 1)

    def body(x_ref, g_ref, sh_ref, sc_ref, w_ref, b_ref, *rest):
        h_ref, e_ref, p_ref = rest[-3:]
        if n_steps > nm:
            @pl.when(pl.program_id(0) >= nm)
            def _():
                h_ref[...] = jnp.zeros_like(h_ref)
                e_ref[...] = jnp.zeros_like(e_ref)
                p_ref[...] = jnp.zeros_like(p_ref)

            pl.when(pl.program_id(0) < nm)(lambda: route(x_ref, g_ref, sh_ref, sc_ref, w_ref, b_ref, *rest[-3:]))
        else:
            route(x_ref, g_ref, sh_ref, sc_ref, w_ref, b_ref, *rest[-3:])

    def route(x_ref, g_ref, sh_ref, sc_ref, w_ref, b_ref, h_ref, e_ref, p_ref):
        h2 = _rms(x_ref[...], g_ref[...]) * (1.0 + sc_ref[...]) + sh_ref[...]
        for s in range(SLAB):
            h_ref[pl.ds(s, tm, stride=H2_PITCH), :] = h2[:, s * LANE:(s + 1) * LANE]
        for s in range(SLAB, H2_PITCH):
            h_ref[pl.ds(s, tm, stride=H2_PITCH), :] = jnp.zeros((tm, LANE), f32)
        logits = _dot(h2.astype(bf16), w_ref[...].astype(bf16)) + b_ref[...]
        lane = lax.broadcasted_iota(i32, (tm, ROUTE_W), 1)
        ninf = jnp.float32(-jnp.inf)
        gl = jnp.where(lane < N_GROUPS, logits, ninf)
        gm = jnp.max(gl, axis=-1, keepdims=True)
        g_p = 1.0 / jnp.sum(jnp.exp(gl - gm), axis=-1, keepdims=True)
        g_idx = jnp.min(jnp.where(gl == gm, lane, ROUTE_W), axis=-1, keepdims=True)
        valid = (lane >= N_GROUPS) & (lane < N_GROUPS + N_EXPERTS) & (((lane - N_GROUPS) // EPG) == g_idx)
        el = jnp.where(valid, logits, ninf)
        ee = jnp.exp(el - jnp.max(el, axis=-1, keepdims=True))
        prob = jnp.where(valid, ee / jnp.sum(ee, axis=-1, keepdims=True), -1.0)
        p1 = jnp.max(prob, axis=-1, keepdims=True)
        i1 = jnp.min(jnp.where(prob == p1, lane, ROUTE_W), axis=-1, keepdims=True)
        prob2 = jnp.where(lane == i1, -1.0, prob)
        p2 = jnp.max(prob2, axis=-1, keepdims=True)
        i2 = jnp.min(jnp.where(prob2 == p2, lane, ROUTE_W), axis=-1, keepdims=True)
        den = p1 + p2
        e_ref[...] = jnp.where(lane == 0, i1 - N_GROUPS, jnp.where(lane == 1, i2 - N_GROUPS, 0))
        p_ref[...] = jnp.where(lane == 0, g_p * p1 / den, jnp.where(lane == 1, g_p * p2 / den, 0.0))

    in_specs = [pl.BlockSpec((tm, D), lambda m: (clamp(m), 0)),
                pl.BlockSpec((1, D), lambda m: (0, 0)),
                mod.row_spec(3, tm, clamp), mod.row_spec(4, tm, clamp),
                pl.BlockSpec((D, ROUTE_W), lambda m: (0, 0)),
                pl.BlockSpec((1, ROUTE_W), lambda m: (0, 0))]
    args = [x, g.reshape(1, D), mod.arr, mod.arr, w_r, b_r]
    aliases = {}
    if bufs is not None:
        in_specs += [pl.BlockSpec(memory_space=pl.ANY)] * 3
        aliases = {len(args) + i: i for i in range(3)}
        args += list(bufs)
    b0 = row0 // tm
    return pl.pallas_call(
        body,
        grid=(n_steps,),
        in_specs=in_specs,
        out_specs=[pl.BlockSpec((tm * H2_PITCH, LANE), lambda m: (b0 + m, 0)),
                   pl.BlockSpec((tm, ROUTE_W), lambda m: (b0 + m, 0)),
                   pl.BlockSpec((tm, ROUTE_W), lambda m: (b0 + m, 0))],
        out_shape=[jax.ShapeDtypeStruct((t_all * H2_PITCH, LANE), f32), jax.ShapeDtypeStruct((t_all, ROUTE_W), i32),
                   jax.ShapeDtypeStruct((t_all, ROUTE_W), f32)],
        input_output_aliases=aliases,
        compiler_params=_cparams(("arbitrary",)),
    )(*args)


TOK_BITS = 14


def _moe_plan(eid, T):
    A = T * TOP_K
    RT = _moe_rt(T)
    nt = pl.cdiv(A, RT) + N_EXPERTS
    flat_e = eid[:, :TOP_K].reshape(-1)
    order = jnp.argsort(flat_e).astype(i32)
    counts = jnp.sum((flat_e[:, None] == jnp.arange(N_EXPERTS, dtype=i32)[None, :]).astype(i32), axis=0)
    start = jnp.cumsum(counts) - counts
    pcnt = (counts + RT - 1) // RT * RT
    pend = jnp.cumsum(pcnt)
    tile_e = jnp.minimum(jnp.sum((pend[None, :] <= (jnp.arange(nt, dtype=i32) * RT)[:, None]).astype(i32), axis=1),
                         N_EXPERTS - 1)
    eidx = jnp.arange(N_EXPERTS, dtype=i32)
    is_e = tile_e[:, None] == eidx[None, :]
    per_tile = lambda tbl: jnp.sum(jnp.where(is_e, tbl[None, :], 0), axis=1)
    t_cnt, t_start = per_tile(counts), per_tile(start)
    off = (jnp.arange(nt, dtype=i32) * RT - per_tile(pend - pcnt))[:, None] + jnp.arange(RT, dtype=i32)[None, :]
    real = off < t_cnt[:, None]
    srt = t_start[:, None] + jnp.minimum(off, t_cnt[:, None])
    a = order[jnp.minimum(srt, A - 1)]
    slot = jnp.arange(nt * RT, dtype=i32).reshape(nt, RT)
    tok = jnp.where(real, a // TOP_K, 0)
    dst = jnp.where(real, a, A + slot - srt)
    n_used = (pend[-1] // RT).astype(i32).reshape(1)
    in_use = counts > 0
    rank = jnp.cumsum(in_use.astype(i32)) - 1
    later = (eidx[None, :] > eidx[:, None]) & in_use[None, :]
    nxt = jnp.min(jnp.where(later, eidx[None, :], N_EXPERTS), axis=1)
    nxt = jnp.where(nxt < N_EXPERTS, nxt, -1)
    tile_info = jnp.concatenate([tile_e, per_tile(rank % 2), per_tile(nxt)]).astype(i32)
    return (tok | (dst << TOK_BITS)).reshape(-1), tile_info, n_used


def _experts(h2, packed, tile_e, n_used, l, w1, w3, w2):
    T = h2.shape[0] // H2_PITCH
    assert TOP_K == 2 and T < (1 << TOK_BITS)
    RT = _moe_rt(T)
    PITCH = SLAB_PITCH
    nt = tile_e.shape[0] // 3
    prime_id = nt * RT
    out_rows = (prime_id + RT) // TOP_K
    RING = 4
    HC, OC = 128, 256
    n_hc, n_oc = D_EXPERT // HC, D // OC
    g_per, s_per = RT // n_hc, RT // n_oc

    def body(tile_ref, nused_ref, slot_ref, h_hbm, w1_hbm, w3_hbm, w2_hbm, o_hbm, *scratch):
        xs, os_ = scratch[:RING], scratch[RING:2 * RING]
        w1b, w3b, w2b, wf1, wf3, wf2, gsem, ssem, wsem = scratch[2 * RING:]
        x0 = xs[0]
        j = pl.program_id(0)
        n_used = nused_ref[0]
        active = j < n_used

        def gather_row(b, r, tok):
            src = pl.multiple_of(tok * H2_PITCH, SUB)
            return pltpu.make_async_copy(h_hbm.at[pl.ds(src, SLAB), :], xs[b].at[pl.ds(r * PITCH, SLAB), :],
                                         gsem.at[b])

        def scatter_row(b, r, d):
            dst = pl.multiple_of(lax.shift_right_logical(d, 1) * O2_PITCH + (d & (TOP_K - 1)) * SLAB, SUB)
            return pltpu.make_async_copy(os_[b].at[pl.ds(r * PITCH, SLAB), :], o_hbm.at[pl.ds(dst, SLAB), :],
                                         ssem.at[b])

        def wait_rows(sem):
            pltpu.make_async_copy(h_hbm.at[pl.ds(0, RT * SLAB), :], x0.at[pl.ds(0, RT * SLAB), :], sem).wait()

        tok_of = lambda s: s & ((1 << TOK_BITS) - 1)
        dst_of = lambda s: lax.shift_right_logical(s, TOK_BITS)
        last_tile = n_used - 1

        @pl.when(j == 0)
        def _():
            os_[RING - 1][...] = jnp.zeros_like(os_[RING - 1])
            for b in range(RING - 1):
                base = jnp.minimum(b, last_tile) * RT

                def one(r, _, b=b, base=base):
                    gather_row(b, r, tok_of(slot_ref[base + r])).start()
                    return 0
                lax.fori_loop(0, RT, one, 0)

        def fetch(e, s):
            return [pltpu.make_async_copy(w_hbm.at[l, e], wf.at[s], wsem.at[s])
                    for w_hbm, wf in ((w1_hbm, wf1), (w3_hbm, wf3), (w2_hbm, wf2))]

        @pl.when(j == 0)
        def _():
            for c in fetch(tile_ref[0], 0):
                c.start()

        @pl.when(active & ((j == 0) | (tile_ref[j] != tile_ref[jnp.maximum(j - 1, 0)])))
        def _():
            ws, e_nxt = tile_ref[nt + j], tile_ref[2 * nt + j]
            for c in fetch(tile_ref[j], ws):
                c.wait()
            w1b[...] = wf1[ws].astype(bf16)
            w3b[...] = wf3[ws].astype(bf16)
            w2b[...] = wf2[ws].astype(bf16)

            @pl.when(e_nxt >= 0)
            def _():
                for c in fetch(e_nxt, 1 - ws):
                    c.start()

        def tile(cur):
            nxt_b = prv_b = (cur + RING - 1) % RING

            @pl.when(j >= RING - 1)
            def _():
                wait_rows(ssem.at[cur])

            wait_rows(gsem.at[cur])
            xb = jnp.concatenate([xs[cur][pl.ds(s, RT, stride=PITCH), :] for s in range(SLAB)],
                                 axis=1).astype(bf16)
            nxt = jnp.minimum(j + RING - 1, last_tile) * RT
            prv = jnp.maximum(j - 1, 0) * RT
            parts = []
            for c in range(n_hc):
                h1 = _dot(xb, w1b[:, c * HC:(c + 1) * HC])
                h3 = _dot(xb, w3b[:, c * HC:(c + 1) * HC])
                parts.append(((h1 * jax.nn.sigmoid(h1)) * h3).astype(bf16))
                for r in range(c * g_per, (c + 1) * g_per):
                    gather_row(nxt_b, r, tok_of(slot_ref[nxt + r])).start(priority=r % 2)
            hm = jnp.concatenate(parts, axis=1)
            for c in range(n_oc):
                res = _dot(hm, w2b[:, c * OC:(c + 1) * OC])
                for t in range(OC // LANE):
                    os_[cur][pl.ds(c * (OC // LANE) + t, RT, stride=PITCH), :] = res[:, t * LANE:(t + 1) * LANE]
                for r in range(c * s_per, (c + 1) * s_per):
                    d = jnp.where(j == 0, prime_id + r, dst_of(slot_ref[prv + r]))
                    scatter_row(prv_b, r, d).start(priority=r % 2)

        def last(cur):
            def one(r, _):
                scatter_row(cur, r, dst_of(slot_ref[j * RT + r])).start()
                return 0
            lax.fori_loop(0, RT, one, 0)
            wait_rows(ssem.at[cur])
            for back in range(1, RING):
                pl.when(j >= back - 1)(lambda b=(cur - back) % RING: wait_rows(ssem.at[b]))
            for ahead in range(1, RING):
                wait_rows(gsem.at[(cur + ahead) % RING])

        for cur in range(RING):
            pl.when(active & (j % RING == cur))(lambda cur=cur: tile(cur))
        for cur in range(RING):
            pl.when((j == last_tile) & (j % RING == cur))(lambda cur=cur: last(cur))

    grid_spec = pltpu.PrefetchScalarGridSpec(
        num_scalar_prefetch=3,
        grid=(nt,),
        in_specs=[pl.BlockSpec(memory_space=pl.ANY)] * 4,
        out_specs=pl.BlockSpec(memory_space=pl.ANY),
        scratch_shapes=[pltpu.VMEM((RT * PITCH, LANE), f32)] * (2 * RING) + [
                        pltpu.VMEM((D, D_EXPERT), bf16), pltpu.VMEM((D, D_EXPERT), bf16),
                        pltpu.VMEM((D_EXPERT, D), bf16),
                        pltpu.VMEM((2, D, D_EXPERT), f32), pltpu.VMEM((2, D, D_EXPERT), f32),
                        pltpu.VMEM((2, D_EXPERT, D), f32),
                        pltpu.SemaphoreType.DMA((RING,)), pltpu.SemaphoreType.DMA((RING,)),
                        pltpu.SemaphoreType.DMA((2,))],
    )
    return pl.pallas_call(
        body,
        grid_spec=grid_spec,
        out_shape=jax.ShapeDtypeStruct((out_rows * O2_PITCH, LANE), f32),
        compiler_params=_cparams(("arbitrary",)),
    )(tile_e, n_used, packed, h2, w1, w3, w2)


def _moe_combine(x, o2, wgt, mod, norm_final, row0):
    T = x.shape[0]
    tm = min(mod.tm, 512)
    b0 = row0 // tm

    def body(x_ref, o_ref, p_ref, m_ref, *rest):
        p = p_ref[...]
        row = lambda k: jnp.concatenate(
            [o_ref[pl.ds(k * SLAB + s, tm, stride=O2_PITCH), :] for s in range(SLAB)], axis=1)
        y = p[:, 0:1] * row(0)
        for k in range(1, TOP_K):
            y = y + p[:, k:k + 1] * row(k)
        xo = x_ref[...] + m_ref[...] * y
        if norm_final is None:
            rest[0][...] = xo
        else:
            rest[1][...] = _rms(xo, rest[0][...])

    ins = [x, o2, wgt, mod.arr]
    specs = [pl.BlockSpec((tm, D), lambda m: (m, 0)),
             pl.BlockSpec((tm * O2_PITCH, LANE), lambda m: (b0 + m, 0)),
             pl.BlockSpec((tm, ROUTE_W), lambda m: (b0 + m, 0)), mod.row_spec(5, tm)]
    if norm_final is not None:
        ins.append(norm_final.reshape(1, D))
        specs.append(pl.BlockSpec((1, D), lambda m: (0, 0)))
    return pl.pallas_call(
        body,
        grid=(T // tm,),
        in_specs=specs,
        out_specs=pl.BlockSpec((tm, D), lambda m: (m, 0)),
        out_shape=jax.ShapeDtypeStruct((T, D), f32),
        compiler_params=_cparams(("parallel",)),
    )(*ins)


def _moe(groups, p, l, big, norm_final):
    pad = ROUTE_W - N_GROUPS - N_EXPERTS
    w_r = jnp.pad(jnp.concatenate([p['moe_w_group'], p['moe_w_router']], axis=1), ((0, 0), (0, pad)))
    b_r = jnp.pad(jnp.concatenate([p['moe_b_group'], p['moe_b_router']]), (0, pad)).reshape(1, ROUTE_W)
    t_all = sum(g.T for g in groups)
    bufs, row0 = None, 0
    for g in groups:
        g.row0 = row0
        bufs = _router(g.x, p['norm_ffn'], g.mod, w_r, b_r, row0, t_all, bufs)
        row0 += g.T
    h2, eid, wgt = bufs
    packed, tile_e, n_used = _moe_plan(eid, t_all)
    o2 = _experts(h2, packed, tile_e, n_used, l, big['moe_w1'], big['moe_w3'], big['moe_w2'])
    for g in groups:
        g.x = _moe_combine(g.x, o2, wgt, g.mod, norm_final, g.row0)


class _Group:
    def __init__(self, x, mods, states, pos0):
        self.B, self.L, _ = x.shape
        self.T = self.B * self.L
        self.tm = _row_tile(self.T)
        self.x = x.reshape(self.T, D)
        self.mods, self.states, self.pos0 = mods, states, pos0
        self.outs = ([], [], [], [], [])
        self.ret_all = self.rw_all = None

    def mix(self, l, p, rp, sp, layers, big, w_in_t):
        B, L = self.B, self.L
        self.mod = mod = _Mod(self.mods[l], L, self.tm)
        proj = _in_proj(self.x, layers['norm_mix'], l, mod, w_in_t)
        if self.states is None:
            y_ret, s_ret = _retention_seq(proj, B, L)
            y_rw, s_rw, shift = _rwkv_seq(proj, rp, B, L)
            y_s5, s_re, s_im = _s5_seq(proj, sp, B, L)
        else:
            st_ret, st_rw, st_shift, st_re, st_im = self.states
            y_ret, self.ret_all = _retention_step(proj, st_ret, l, self.ret_all, self.pos0)
            y_rw, self.rw_all = _rwkv_step(proj, st_shift[l], jnp.transpose(st_rw, (0, 2, 3, 4, 1)), l,
                                           self.rw_all, rp)
            s_ret = s_rw = None
            shift = proj[:, OFF_RW:OFF_RW + RWKV_PROJ]
            y_s5, s_re, s_im = _s5_step(proj, st_re[l], st_im[l], sp)
        z = _glu_proj(y_s5, l, big['s5_w_glu'])
        merged = _merge_proj(y_ret, y_rw, z, l, big['ret_w_o'], big['rwkv_w_o'], big['s5_w_o'], proj)
        self.x = _out_proj(merged, l, big['w_out'], self.x, mod)
        for lst, val in zip(self.outs, (s_ret, s_rw, shift, s_re, s_im)):
            lst.append(val)

    def results(self):
        stacked = [jnp.stack(o) if o[0] is not None else None for o in self.outs]
        if self.states is not None:
            stacked[0], stacked[1] = self.ret_all, jnp.transpose(self.rw_all, (0, 4, 1, 2, 3))
        return self.x.reshape(self.B, self.L, D), stacked


def _trunk(groups, layers, norm_final):
    big = {k: layers[k] for k in _BIG}
    w_in_t = jnp.swapaxes(layers['w_in'], 1, 2)
    for l in range(DEPTH):
        p = {name: arr[l] for name, arr in layers.items() if name not in _BIG + ('w_in',)}
        rp, sp = _rwkv_params(p), _s5_params(p)
        for g in groups:
            g.mix(l, p, rp, sp, layers, big, w_in_t)
        _moe(groups, p, l, big, norm_final if l == DEPTH - 1 else None)
    return [g.results() for g in groups]


_BIG = ('ret_w_o', 'rwkv_w_o', 's5_w_glu', 's5_w_o', 'w_out', 'moe_w1', 'moe_w3', 'moe_w2')


def kernel(x_prompt, x_sample, state_ret, state_rwkv, state_shift, state_s5_re, state_s5_im,
           c_prompt, c_sample, norm_mix, norm_ffn, w_ada, b_ada, w_in, ret_w_o, rwkv_mu, rwkv_w0,
           rwkv_w2, rwkv_a0, rwkv_a2, rwkv_g2, rwkv_k_k, rwkv_k_a, rwkv_r_k, rwkv_ln_w, rwkv_ln_b,
           rwkv_w_o, s5_a_re, s5_a_im, s5_b_re, s5_b_im, s5_c_re, s5_c_im, s5_d, s5_log_dt, s5_w_glu,
           s5_w_o, w_out, moe_w_group, moe_b_group, moe_w_router, moe_b_router, moe_w1, moe_w3, moe_w2,
           norm_final):
    layers = {
        'norm_mix': norm_mix, 'norm_ffn': norm_ffn, 'w_in': w_in,
        'ret_w_o': ret_w_o, 'rwkv_mu': rwkv_mu, 'rwkv_w0': rwkv_w0, 'rwkv_w2': rwkv_w2,
        'rwkv_a0': rwkv_a0, 'rwkv_a2': rwkv_a2, 'rwkv_g2': rwkv_g2, 'rwkv_k_k': rwkv_k_k,
        'rwkv_k_a': rwkv_k_a, 'rwkv_r_k': rwkv_r_k, 'rwkv_ln_w': rwkv_ln_w, 'rwkv_ln_b': rwkv_ln_b,
        'rwkv_w_o': rwkv_w_o, 's5_a_re': s5_a_re, 's5_a_im': s5_a_im, 's5_b_re': s5_b_re,
        's5_b_im': s5_b_im, 's5_c_re': s5_c_re, 's5_c_im': s5_c_im, 's5_d': s5_d,
        's5_log_dt': s5_log_dt, 's5_w_glu': s5_w_glu, 's5_w_o': s5_w_o, 'w_out': w_out,
        'moe_w_group': moe_w_group, 'moe_b_group': moe_b_group, 'moe_w_router': moe_w_router,
        'moe_b_router': moe_b_router, 'moe_w1': moe_w1, 'moe_w3': moe_w3, 'moe_w2': moe_w2,
    }
    Bp, Bs = x_prompt.shape[0], x_sample.shape[0]
    s_off = -(-Bp // 8) * 8
    c_all = jnp.concatenate([c_prompt, jnp.zeros((s_off - Bp, D), f32), c_sample], axis=0)
    mod_all = _adaln(c_all, w_ada, b_ada)
    prompt = _Group(x_prompt, mod_all[:, :Bp], None, 0.0)
    sample = _Group(x_sample, mod_all[:, s_off:s_off + Bs],
                    (state_ret, state_rwkv, state_shift, state_s5_re, state_s5_im), float(PAST_LEN))
    ((y_prompt, (ret_p, rwkv_p, shift_p, s5re_p, s5im_p)),
     (y_sample, (ret_s, rwkv_s, shift_s, s5re_s, s5im_s))) = _trunk([prompt, sample], layers, norm_final)
    return (y_prompt, y_sample, ret_p, ret_s, rwkv_p, rwkv_s, shift_p, shift_s, s5re_p, s5re_s, s5im_p, s5im_s)
```

```python
import jax
import jax.numpy as jnp
from jax import lax
from jax.experimental import pallas as pl
from jax.experimental.pallas import tpu as pltpu

f32 = jnp.float32
bf16 = jnp.bfloat16
i32 = jnp.int32

D = 2048
DEPTH = 2
PAST_LEN = 16384
RET_W, RET_H, RET_DK, RET_DV, RET_CHUNK = 1024, 4, 256, 256, 128
RET_GN_EPS = 1e-6
ROPE_BASE = 10000.0
RWKV_W, RWKV_N, RWKV_H = 1024, 64, 16
DECAY_LORA, AAA_LORA, GATE_LORA = 64, 64, 160
RWKV_PROJ = 3 * RWKV_W + DECAY_LORA + AAA_LORA + GATE_LORA
RWKV_LN_EPS = 64e-5
S5_W, S5_GC, S5_G, S5_P = 1024, 16, 64, 64
N_MOD = 6
RMS_EPS = 1e-6
N_GROUPS, EPG, N_EXPERTS, TOP_K, D_EXPERT = 4, 8, 32, 2, 512
IN_W = 4 * RET_W + RWKV_PROJ + S5_W + 3 * D
OFF_Q, OFF_K, OFF_V, OFF_G, OFF_RW = 0, 1024, 2048, 3072, 4096
OFF_U = OFF_RW + RWKV_PROJ
OFF_GATE = OFF_U + S5_W
LANE = 128
PROJ_W = ((IN_W + LANE - 1) // LANE) * LANE
VMEM_LIMIT = 56 * 1024 * 1024


def _cparams(sem):
    return pltpu.CompilerParams(dimension_semantics=sem, vmem_limit_bytes=VMEM_LIMIT)


def _dot(a, b):
    return jnp.dot(a, b, preferred_element_type=f32)


def _dot_nt(a, b):
    return lax.dot_general(a, b, (((1,), (1,)), ((), ())), preferred_element_type=f32)


def _dot_tn(a, b):
    return lax.dot_general(a, b, (((0,), (0,)), ((), ())), preferred_element_type=f32)


def _rms(x, g):
    return x * lax.rsqrt(jnp.mean(x * x, axis=-1, keepdims=True) + RMS_EPS) * g


def _head_norm(y, eps):
    mu = jnp.mean(y, axis=-1, keepdims=True)
    yc = y - mu
    return yc * lax.rsqrt(jnp.mean(yc * yc, axis=-1, keepdims=True) + eps)


def _row_tile(T):
    return 1024 if T >= 1024 else T


class _Mod:
    def __init__(self, mod, L, tm):
        self.L, self.tm = L, tm
        self.per_token = L == 1
        self.arr = mod if self.per_token else mod.reshape(mod.shape[0], 1, N_MOD * D)

    def spec(self, j, tn, col_of):
        nb = D // tn
        if self.per_token:
            return pl.BlockSpec((self.tm, tn), lambda m, n: (m, j * nb + col_of(n)))
        L, tm = self.L, self.tm
        return pl.BlockSpec((None, 1, tn), lambda m, n: ((m * tm) // L, 0, j * nb + col_of(n)))

    def row_spec(self, j, tm, remap=lambda m: m):
        if self.per_token:
            return pl.BlockSpec((tm, D), lambda m: (remap(m), j))
        L = self.L
        return pl.BlockSpec((None, 1, D), lambda m: ((remap(m) * tm) // L, 0, j))


def _fused_mm(x_ops, w_ops, e_ops, pre, post, *, grid, out_specs, out_shape, cache_shapes):
    nx, nw, ne = len(x_ops), len(w_ops), len(e_ops)
    n_out = len(out_shape)

    def body(*refs):
        x_refs = refs[:nx]
        w_refs = refs[nx:nx + nw]
        e_refs = refs[nx + nw:nx + nw + ne]
        o_refs = refs[nx + nw + ne:nx + nw + ne + n_out]
        caches = refs[nx + nw + ne + n_out:]
        if cache_shapes:
            @pl.when(pl.program_id(1) == 0)
            def _():
                for i in range(nx):
                    caches[i][...] = pre(i, x_refs[i], e_refs).astype(bf16)
            lhs = [c[...] for c in caches]
        else:
            lhs = [x[...] for x in x_refs]
        prods = [(_dot_nt if len(w_ops[j]) > 3 and w_ops[j][3] else _dot)(
            lhs[w_ops[j][2]], w_refs[j][...].astype(bf16)) for j in range(nw)]
        for o_ref, o in zip(o_refs, post(prods, e_refs)):
            o_ref[...] = o.astype(o_ref.dtype)

    return pl.pallas_call(
        body,
        grid=grid,
        in_specs=[s for _, s in x_ops] + [w[1] for w in w_ops] + [s for _, s in e_ops],
        out_specs=out_specs,
        out_shape=out_shape,
        scratch_shapes=[pltpu.VMEM(s, bf16) for s in cache_shapes],
        compiler_params=_cparams(("parallel", "arbitrary")),
    )(*[a for a, _ in x_ops], *[w[0] for w in w_ops], *[a for a, _ in e_ops])


def _adaln(c_all, w_ada, b_ada):
    R = c_all.shape[0]
    tn = 1024

    def pre(i, x_ref, e_refs):
        c = x_ref[...]
        return c * jax.nn.sigmoid(c)

    def post(prods, e_refs):
        return (prods[0] + e_refs[0][...],)

    (out,) = _fused_mm(
        [(c_all, pl.BlockSpec((R, D), lambda l, n: (0, 0)))],
        [(w_ada, pl.BlockSpec((None, D, tn), lambda l, n: (l, 0, n)), 0)],
        [(b_ada.reshape(DEPTH, 1, N_MOD * D), pl.BlockSpec((None, 1, tn), lambda l, n: (l, 0, n)))],
        pre, post,
        grid=(DEPTH, N_MOD * D // tn),
        out_specs=[pl.BlockSpec((None, R, tn), lambda l, n: (l, 0, n))],
        out_shape=[jax.ShapeDtypeStruct((DEPTH, R, N_MOD * D), f32)],
        cache_shapes=[(R, D)],
    )
    return out


def _in_proj(x, g, l, mod, w_in_t):
    T = x.shape[0]
    tm, tn = mod.tm, 1024
    tr = min(tm, 512)

    def norm_body(x_ref, g_ref, shift_ref, scale_ref, h_ref):
        h_ref[...] = (_rms(x_ref[...], g_ref[...]) * (1.0 + scale_ref[...]) + shift_ref[...]).astype(bf16)

    h = pl.pallas_call(
        norm_body,
        grid=(T // tr,),
        in_specs=[pl.BlockSpec((tr, D), lambda m: (m, 0)),
                  pl.BlockSpec((None, 1, D), lambda m: (l, 0, 0)),
                  mod.row_spec(0, tr), mod.row_spec(1, tr)],
        out_specs=pl.BlockSpec((tr, D), lambda m: (m, 0)),
        out_shape=jax.ShapeDtypeStruct((T, D), bf16),
        compiler_params=_cparams(("parallel",)),
    )(x, g.reshape(DEPTH, 1, D), mod.arr, mod.arr)

    def mm_body(h_ref, w_ref, o_ref, wb):
        @pl.when(pl.program_id(1) == 0)
        def _():
            wb[...] = w_ref[...].astype(bf16)

        col = pl.program_id(0) * tn + lax.broadcasted_iota(i32, (tm, tn), 1)
        o_ref[...] = jnp.where(col < IN_W, _dot_nt(h_ref[...], wb[...]), 0.0)

    return pl.pallas_call(
        mm_body,
        grid=(pl.cdiv(PROJ_W, tn), T // tm),
        in_specs=[pl.BlockSpec((tm, D), lambda n, m: (m, 0)),
                  pl.BlockSpec((None, tn, D), lambda n, m: (l, n, 0))],
        out_specs=pl.BlockSpec((tm, tn), lambda n, m: (m, n)),
        out_shape=jax.ShapeDtypeStruct((T, PROJ_W), f32),
        scratch_shapes=[pltpu.VMEM((tn, D), bf16)],
        compiler_params=_cparams(("parallel", "arbitrary")),
    )(h, w_in_t)


def _glu_proj(yg, l, w_glu):
    T = yg.shape[0]
    tm, tn = _row_tile(T), 512
    nb = S5_W // tn

    def post(prods, e_refs):
        return (prods[0] * jax.nn.sigmoid(prods[1]),)

    (out,) = _fused_mm(
        [(yg, pl.BlockSpec((tm, S5_W), lambda m, n: (m, 0)))],
        [(w_glu, pl.BlockSpec((None, S5_W, tn), lambda m, n: (l, 0, n)), 0),
         (w_glu, pl.BlockSpec((None, S5_W, tn), lambda m, n: (l, 0, nb + n)), 0)],
        [], None, post,
        grid=(T // tm, nb),
        out_specs=[pl.BlockSpec((tm, tn), lambda m, n: (m, n))],
        out_shape=[jax.ShapeDtypeStruct((T, S5_W), bf16)],
        cache_shapes=[],
    )
    return out


def _merge_proj(y_ret, y_rw, y_s5, l, w_ret, w_rw, w_s5, proj):
    T = y_ret.shape[0]
    tm, tn = _row_tile(T), 512
    lead = OFF_GATE % LANE
    base = OFF_GATE - lead

    def gate_spec(i):
        return pl.BlockSpec((pl.Element(tm), pl.Element(tn + LANE)),
                            lambda m, n: (pl.multiple_of(m * tm, tm), pl.multiple_of(base + i * D + n * tn, LANE)))

    def post(prods, e_refs):
        acc = None
        for p, e in zip(prods, e_refs):
            t = jax.nn.sigmoid(e[:, lead:lead + tn]) * p
            acc = t if acc is None else acc + t
        return (acc,)

    xspec = pl.BlockSpec((tm, RET_W), lambda m, n: (m, 0))
    wspec = pl.BlockSpec((None, RET_W, tn), lambda m, n: (l, 0, n))
    (out,) = _fused_mm(
        [(y_ret, xspec), (y_rw, xspec), (y_s5, xspec)],
        [(w_ret, wspec, 0), (w_rw, wspec, 1), (w_s5, wspec, 2)],
        [(proj, gate_spec(0)), (proj, gate_spec(1)), (proj, gate_spec(2))],
        None, post,
        grid=(T // tm, D // tn),
        out_specs=[pl.BlockSpec((tm, tn), lambda m, n: (m, n))],
        out_shape=[jax.ShapeDtypeStruct((T, D), bf16)],
        cache_shapes=[],
    )
    return out


def _out_proj(merged, l, w_out, x, mod):
    T = x.shape[0]
    tm, tn = mod.tm, 512

    def post(prods, e_refs):
        x_ref, m_ref = e_refs
        return (x_ref[...] + m_ref[...] * prods[0],)

    (out,) = _fused_mm(
        [(merged, pl.BlockSpec((tm, D), lambda m, n: (m, 0)))],
        [(w_out, pl.BlockSpec((None, D, tn), lambda m, n: (l, 0, n)), 0)],
        [(x, pl.BlockSpec((tm, tn), lambda m, n: (m, n))), (mod.arr, mod.spec(2, tn, lambda n: n))],
        None, post,
        grid=(T // tm, D // tn),
        out_specs=[pl.BlockSpec((tm, tn), lambda m, n: (m, n))],
        out_shape=[jax.ShapeDtypeStruct((T, D), f32)],
        cache_shapes=[],
    )
    return out


def _ret_consts(L, pos0):
    C = RET_CHUNK if L % RET_CHUNK == 0 else L
    H = RET_H
    log_g = jnp.log1p(-jnp.exp2(-5.0 - jnp.arange(H, dtype=f32)))
    i = jnp.arange(C, dtype=f32)
    diff = i[:, None] - i[None, :]
    causal = diff >= 0
    dmask = jnp.where(causal, jnp.exp(jnp.where(causal, diff, 0.0)[None] * log_g[:, None, None]), 0.0)
    kdec = jnp.exp((C - 1.0 - i)[:, None] * log_g[None, :])
    qdec = jnp.exp((i + 1.0)[:, None] * log_g[None, :])
    g_chunk = jnp.exp(C * log_g)
    half = RET_DK // 2
    inv = ROPE_BASE ** (-jnp.arange(half, dtype=f32) / half)
    pos = pos0 + jnp.arange(L, dtype=f32)
    ang = pos[:, None] * inv[None, :]
    return C, dmask, kdec, qdec, g_chunk, jnp.cos(ang), jnp.sin(ang)


def _rotary(x, cos, sin):
    half = RET_DK // 2
    x1, x2 = x[..., :half], x[..., half:]
    return jnp.concatenate([x1 * cos - x2 * sin, x1 * sin + x2 * cos], axis=-1)


def _retention_seq(proj, B, L):
    C, dmask, kdec, qdec, g_chunk, cos, sin = _ret_consts(L, 0.0)
    H, dk = RET_H, RET_DK
    n = L // C
    kdec_f = jnp.broadcast_to(kdec.T[:, :, None], (H, C, dk))
    qdec_f = jnp.broadcast_to(qdec.T[:, :, None], (H, C, dk))
    gch_f = jnp.broadcast_to(g_chunk[:, None, None], (H, 8, dk))
    proj3 = proj.reshape(B, L, PROJ_W)

    def body(q_ref, k_ref, v_ref, g_ref, cos_ref, sin_ref, dm_ref, kd_ref, qd_ref, gc_ref, y_ref, s_ref, st):
        c = pl.program_id(1)

        @pl.when(c == 0)
        def _():
            st[...] = jnp.zeros_like(st)

        cs, sn = cos_ref[...], sin_ref[...]
        hs = range(H)
        sl = [slice(h * dk, (h + 1) * dk) for h in hs]
        q = [_rotary(q_ref[:, sl[h]], cs, sn) for h in hs]
        k = [_rotary(k_ref[:, sl[h]], cs, sn) * (dk ** -0.5) for h in hs]
        vb = [v_ref[:, sl[h]].astype(bf16) for h in hs]
        s0 = [st[h] for h in hs]
        scores = [_dot_nt(q[h].astype(bf16), k[h].astype(bf16)) * dm_ref[h] for h in hs]
        cross = [_dot((q[h] * qd_ref[h]).astype(bf16), s0[h].astype(bf16)) for h in hs]
        kv = [_dot_tn((k[h] * kd_ref[h]).astype(bf16), vb[h]) for h in hs]
        o = [_dot(scores[h].astype(bf16), vb[h]) + cross[h] for h in hs]
        for h in hs:
            st[h] = s0[h] * gc_ref[h, 0:1, :] + kv[h]
            g = g_ref[:, sl[h]]
            y_ref[:, sl[h]] = (g * jax.nn.sigmoid(g) * _head_norm(o[h], RET_GN_EPS)).astype(bf16)

        @pl.when(c == n - 1)
        def _():
            s_ref[...] = st[...]

    def seg(off):
        return pl.BlockSpec((None, C, RET_W), lambda b, c: (b, c, off // RET_W))

    const3 = lambda shp: pl.BlockSpec(shp, lambda b, c: (0, 0, 0))
    y, s = pl.pallas_call(
        body,
        grid=(B, n),
        in_specs=[seg(OFF_Q), seg(OFF_K), seg(OFF_V), seg(OFF_G),
                  pl.BlockSpec((C, dk // 2), lambda b, c: (c, 0)),
                  pl.BlockSpec((C, dk // 2), lambda b, c: (c, 0)),
                  const3((H, C, C)), const3((H, C, dk)), const3((H, C, dk)), const3((H, 8, dk))],
        out_specs=[pl.BlockSpec((None, C, RET_W), lambda b, c: (b, c, 0)),
                   pl.BlockSpec((None, H, dk, RET_DV), lambda b, c: (b, 0, 0, 0))],
        out_shape=[jax.ShapeDtypeStruct((B, L, RET_W), bf16),
                   jax.ShapeDtypeStruct((B, H, dk, RET_DV), f32)],
        scratch_shapes=[pltpu.VMEM((H, dk, RET_DV), f32)],
        compiler_params=_cparams(("parallel", "arbitrary")),
    )(proj3, proj3, proj3, proj3, cos, sin, dmask, kdec_f, qdec_f, gch_f)
    return y.reshape(B * L, RET_W), s


STEP_TB = 16


def _layer_grid(l, buf, inner):
    if buf is not None:
        return inner, (lambda fn: (lambda *ix: fn(l, ix, ix)))
    assert l == 0
    last = tuple(n - 1 for n in inner)

    def wrap(fn):
        def index_map(d, *ix):
            parked = tuple(jnp.where(d == l, i, z) for i, z in zip(ix, last))
            return fn(d, ix, parked)
        return index_map
    return (DEPTH,) + inner, wrap


def _retention_step(proj, s_all, l, buf, pos0):
    B = proj.shape[0]
    _, _, _, _, g_chunk, cos, sin = _ret_consts(1, pos0)
    H, dk = RET_H, RET_DK
    gch = jnp.broadcast_to(g_chunk[:, None, None], (H, 8, dk))
    tb = STEP_TB
    grid, wrap = _layer_grid(l, buf, (B // tb, H))

    def body(q_ref, k_ref, v_ref, g_ref, cos_ref, sin_ref, gc_ref, s_ref, *rest):
        y_ref, so_ref = rest[-2:]

        def update():
            cs, sn = cos_ref[...], sin_ref[...]
            q = _rotary(q_ref[...], cs, sn)
            k = _rotary(k_ref[...], cs, sn) * (dk ** -0.5)
            v = v_ref[...]
            s1 = s_ref[...] * gc_ref[0:1, :][None] + k[:, :, None] * v[:, None, :]
            so_ref[...] = s1
            o = jnp.sum(q[:, :, None] * s1, axis=1)
            g = g_ref[...]
            y_ref[...] = (g * jax.nn.sigmoid(g) * _head_norm(o, RET_GN_EPS)).astype(bf16)

        if buf is not None:
            update()
        else:
            pl.when(pl.program_id(0) == l)(update)

            @pl.when(pl.program_id(0) != l)
            def _():
                so_ref[...] = jnp.zeros_like(so_ref)

    def seg(off):
        return pl.BlockSpec((tb, dk), wrap(lambda d, ix, pk: (pk[0], off // dk + pk[1])))

    const = lambda d, ix, pk: (0, 0)
    in_specs = [seg(OFF_Q), seg(OFF_K), seg(OFF_V), seg(OFF_G),
                pl.BlockSpec((1, dk // 2), wrap(const)), pl.BlockSpec((1, dk // 2), wrap(const)),
                pl.BlockSpec((None, 8, dk), wrap(lambda d, ix, pk: (pk[1], 0, 0))),
                pl.BlockSpec((None, tb, None, dk, RET_DV), wrap(lambda d, ix, pk: (l, pk[0], pk[1], 0, 0)))]
    args = [proj, proj, proj, proj, cos, sin, gch, s_all]
    aliases = {}
    if buf is not None:
        in_specs.append(pl.BlockSpec(memory_space=pl.ANY))
        args.append(buf)
        aliases = {len(args) - 1: 1}
    return pl.pallas_call(
        body,
        grid=grid,
        in_specs=in_specs,
        out_specs=[pl.BlockSpec((tb, dk), wrap(lambda d, ix, pk: pk)),
                   pl.BlockSpec((None, tb, None, dk, RET_DV), wrap(lambda d, ix, pk: (d, ix[0], ix[1], 0, 0)))],
        out_shape=[jax.ShapeDtypeStruct((B, RET_W), bf16),
                   jax.ShapeDtypeStruct((DEPTH, B, H, dk, RET_DV), f32)],
        input_output_aliases=aliases,
        compiler_params=_cparams(("arbitrary",) * len(grid)),
    )(*args)


RW_C = 64
RW_Q = 4
RW_NQ = RWKV_H // RW_Q
RW_GROUP = 16
RW_BLK =((RWKV_PROJ + LANE - 1) // LANE) * LANE
_RW_PKEYS = ('mu', 'w0', 'w2', 'a0', 'a2', 'g2', 'k_k', 'k_a', 'r_k', 'ln_w', 'ln_b')


def _rwkv_params(p):
    return dict(
        mu=jnp.pad(p['rwkv_mu'], (0, RW_BLK - RWKV_PROJ)).reshape(1, RW_BLK),
        w0=p['rwkv_w0'].reshape(1, RWKV_W), w2=p['rwkv_w2'],
        a0=p['rwkv_a0'].reshape(1, RWKV_W), a2=p['rwkv_a2'], g2=p['rwkv_g2'],
        k_k=p['rwkv_k_k'].reshape(1, RWKV_W), k_a=p['rwkv_k_a'].reshape(1, RWKV_W),
        r_k=p['rwkv_r_k'].reshape(1, RWKV_W),
        ln_w=p['rwkv_ln_w'].reshape(1, RWKV_W), ln_b=p['rwkv_ln_b'].reshape(1, RWKV_W))


def _split_bf16(x, terms):
    out = []
    for _ in range(terms - 1):
        hi = x.astype(bf16)
        out.append(hi)
        x = x - hi.astype(f32)
    out.append(x.astype(bf16))
    return out


def _head_sum(x):
    QW = RW_Q * RWKV_N
    r = lax.broadcasted_iota(i32, (QW, QW), 0) // RWKV_N
    c = lax.broadcasted_iota(i32, (QW, QW), 1) // RWKV_N
    ones = (r == c).astype(bf16)
    parts = _split_bf16(x, 2)
    outs = []
    for q in range(RW_NQ):
        sl = slice(q * QW, (q + 1) * QW)
        outs.append(_dot(parts[0][:, sl], ones) + _dot(parts[1][:, sl], ones))
    return jnp.concatenate(outs, axis=1)


def _rwkv_mix(rw, prev, pr):
    m = rw + (prev - rw) * pr['mu'][...]
    r = m[:, 0:RWKV_W]
    k = m[:, RWKV_W:2 * RWKV_W]
    v = m[:, 2 * RWKV_W:3 * RWKV_W]
    o = 3 * RWKV_W
    xw = m[:, o:o + DECAY_LORA]
    xa = m[:, o + DECAY_LORA:o + DECAY_LORA + AAA_LORA]
    xg = m[:, o + DECAY_LORA + AAA_LORA:o + DECAY_LORA + AAA_LORA + GATE_LORA]
    w_log = -jax.nn.softplus(-(pr['w0'][...] + _dot(jnp.tanh(xw).astype(bf16), pr['w2'][...].astype(bf16)))) - 0.5
    lw = -jnp.exp(w_log)
    a = jax.nn.sigmoid(pr['a0'][...] + _dot(xa.astype(bf16), pr['a2'][...].astype(bf16)))
    g = _dot(jax.nn.sigmoid(xg).astype(bf16), pr['g2'][...].astype(bf16))
    kk = k * pr['k_k'][...]
    kk = kk / jnp.maximum(jnp.sqrt(_head_sum(kk * kk)), 1e-12)
    kf = k * (1.0 + (a - 1.0) * pr['k_a'][...])
    return r, lw, kf, v, kk, a, g


def _rwkv_out(y, r, kf, v, g, pr):
    yc = y - _head_sum(y) * (1.0 / RWKV_N)
    yn = yc * lax.rsqrt(_head_sum(yc * yc) * (1.0 / RWKV_N) + RWKV_LN_EPS)
    yn = yn * pr['ln_w'][...] + pr['ln_b'][...]
    yn = yn + _head_sum(r * kf * pr['r_k'][...]) * v
    return yn * g


def _rwkv_seq(proj, rp, B, L):
    C = RW_C
    assert L % C == 0 and C == RWKV_N
    n = L // C
    QW = RW_Q * RWKV_N
    BC = B * C
    proj3 = proj.reshape(B, L, PROJ_W)

    def body(rw_ref, *refs):
        pr = dict(zip(_RW_PKEYS, refs[:len(_RW_PKEYS)]))
        y_ref, s_ref, sh_ref, st, carry = refs[len(_RW_PKEYS):]
        c = pl.program_id(0)

        @pl.when(c == 0)
        def _():
            st[...] = jnp.zeros_like(st)
            carry[...] = jnp.zeros_like(carry)

        rw = rw_ref[...].reshape(BC, RW_BLK)
        rolled = pltpu.roll(rw, 1, 0)
        row = lax.broadcasted_iota(i32, (C, RW_BLK), 0)
        prev = jnp.concatenate(
            [jnp.where(row == 0, carry[b, 0:1, :], rolled[b * C:(b + 1) * C]) for b in range(B)], axis=0)
        for b in range(B):
            carry[b, 0:1, :] = rw[(b + 1) * C - 1:(b + 1) * C, :]
        r, lw, kf, v, kk, a, g = _rwkv_mix(rw, prev, pr)

        ti = lax.broadcasted_iota(i32, (BC, BC), 0)
        si = lax.broadcasted_iota(i32, (BC, BC), 1)
        tril = ((si <= ti) & ((si // C) == (ti // C))).astype(bf16)
        lg = sum(_dot(tril, part) for part in _split_bf16(lw, 3))
        lgc = jnp.concatenate(
            [jnp.broadcast_to(lg[(b + 1) * C - 1:(b + 1) * C, :], (C, RWKV_W)) for b in range(B)], axis=0)
        e_neg = jnp.exp(-lg)
        e_rem = jnp.exp(lgc - lg)
        at = kk * jnp.exp(lg - lw)
        ka = kk * a
        bt = ka * e_neg
        kt = kf * e_neg
        rt = r * jnp.exp(lg)
        bh = ka * e_rem
        kh = kf * e_rem
        gcr = jnp.exp(lgc)

        rr = lax.broadcasted_iota(i32, (RW_Q * C, QW), 0)
        ll = lax.broadcasted_iota(i32, (RW_Q * C, QW), 1)
        blockmask = (rr // C) == (ll // RWKV_N)
        tt = lax.broadcasted_iota(i32, (C, QW), 0)
        ss = lax.broadcasted_iota(i32, (C, QW), 1) % C
        strict = ss < tt
        incl = ss <= tt
        eye = (ss == tt).astype(f32)

        def bd(x):
            return jnp.where(blockmask, jnp.concatenate([x] * RW_Q, axis=0), 0.0).astype(bf16)

        ys = [[None] * RW_NQ for _ in range(B)]
        chains = [(b, q) for b in range(B) for q in range(RW_NQ)]
        for g0 in range(0, len(chains), RW_GROUP):
            grp = chains[g0:g0 + RW_GROUP]
            idx = [(slice(b * C, (b + 1) * C), slice(q * QW, (q + 1) * QW)) for b, q in grp]
            each = lambda fn: [fn(i) for i in range(len(grp))]
            vq = each(lambda i: v[idx[i]])
            ar = each(lambda i: jnp.concatenate([at[idx[i]], rt[idx[i]]], axis=0).astype(bf16))
            big = each(lambda i: _dot_nt(ar[i], jnp.concatenate([bd(bt[idx[i]]), bd(kt[idx[i]])], axis=0)))
            s0 = each(lambda i: st[grp[i]])
            asrs = each(lambda i: _dot_nt(ar[i], s0[i].astype(bf16)))
            nmat = each(lambda i: jnp.where(strict, big[i][:C, :QW], 0.0))
            akm = each(lambda i: jnp.where(strict, big[i][:C, QW:], 0.0).astype(bf16))
            rbk = each(lambda i: jnp.concatenate([jnp.where(incl, big[i][C:, :QW], 0.0),
                                                  jnp.where(incl, big[i][C:, QW:], 0.0)], axis=1).astype(bf16))
            tm = each(lambda i: eye - nmat[i])
            pw = each(lambda i: _dot(nmat[i].astype(bf16), bd(nmat[i])))
            lvl = 2
            while lvl < C:
                res = each(lambda i: _dot(jnp.concatenate([tm[i], pw[i]], axis=0).astype(bf16), bd(pw[i])))
                tm = each(lambda i: tm[i] + res[i][:C])
                pw = each(lambda i: res[i][C:])
                lvl *= 2
            vbd = each(lambda i: bd(vq[i]))
            rhs = each(lambda i: -(asrs[i][:C] + _dot(akm[i], vbd[i])))
            u = each(lambda i: _dot(tm[i].astype(bf16), bd(rhs[i])))
            y = each(lambda i: asrs[i][C:] + _dot(rbk[i], jnp.concatenate([bd(u[i]), vbd[i]], axis=0)))
            upd = each(lambda i: _dot_tn(jnp.concatenate([u[i], vq[i]], axis=0).astype(bf16),
                                         jnp.concatenate([bh[idx[i]], kh[idx[i]]], axis=0).astype(bf16)))
            for i, (b, q) in enumerate(grp):
                st[b, q] = s0[i] * gcr[b * C:b * C + 1, idx[i][1]] + jnp.where(blockmask, upd[i], 0.0)
                ys[b][q] = y[i]

        y = jnp.concatenate([jnp.concatenate(yb, axis=1) for yb in ys], axis=0)
        y_ref[...] = _rwkv_out(y, r, kf, v, g, pr).astype(bf16).reshape(B, C, RWKV_W)

        @pl.when(c == n - 1)
        def _():
            for b in range(B):
                for q in range(RW_NQ):
                    for h in range(RW_Q):
                        hs = slice(h * RWKV_N, (h + 1) * RWKV_N)
                        s_ref[b, q * RW_Q + h] = st[b, q, hs, hs]
                sh_ref[b] = rw[(b + 1) * C - 1:(b + 1) * C, 0:RWKV_PROJ]

    pspecs = [pl.BlockSpec(rp[k].shape, lambda c: (0, 0)) for k in _RW_PKEYS]
    y, s, sh = pl.pallas_call(
        body,
        grid=(n,),
        in_specs=[pl.BlockSpec((pl.Element(B), pl.Element(C), pl.Element(RW_BLK)),
                               lambda c: (0, pl.multiple_of(c * C, C), OFF_RW))] + pspecs,
        out_specs=[pl.BlockSpec((B, C, RWKV_W), lambda c: (0, c, 0)),
                   pl.BlockSpec((B, RWKV_H, RWKV_N, RWKV_N), lambda c: (0, 0, 0, 0)),
                   pl.BlockSpec((B, 1, RWKV_PROJ), lambda c: (0, 0, 0))],
        out_shape=[jax.ShapeDtypeStruct((B, L, RWKV_W), bf16),
                   jax.ShapeDtypeStruct((B, RWKV_H, RWKV_N, RWKV_N), f32),
                   jax.ShapeDtypeStruct((B, 1, RWKV_PROJ), f32)],
        scratch_shapes=[pltpu.VMEM((B, RW_NQ, QW, QW), f32), pltpu.VMEM((B, 8, RW_BLK), f32)],
        compiler_params=_cparams(("arbitrary",)),
    )(proj3, *[rp[k] for k in _RW_PKEYS])
    return y.reshape(B * L, RWKV_W), s, sh.reshape(B, RWKV_PROJ)


def _rwkv_step(proj, shift, s_t, l, buf, rp):
    B = proj.shape[0]
    N = RWKV_N
    shift_p = jnp.pad(shift, ((0, 0), (0, RW_BLK - RWKV_PROJ)))
    grid, wrap = _layer_grid(l, buf, (RWKV_H,))
    npk = len(_RW_PKEYS)
    vec_names = ('r', 'w', 'kf', 'v', 'kk', 'ka')

    def body(rw_ref, sh_ref, s_ref, *refs):
        pr = dict(zip(_RW_PKEYS, refs[:npk]))
        y_ref, so_ref = refs[-9:-7]
        vt = dict(zip(vec_names, refs[-7:-1]))
        yt = refs[-1]
        h = pl.program_id(len(grid) - 1)

        def update():
            @pl.when(h == 0)
            def _():
                r, lw, kf, v, kk, a, _ = _rwkv_mix(rw_ref[...], sh_ref[...], pr)
                for name, val in zip(vec_names, (r, jnp.exp(lw), kf, v, kk, kk * a)):
                    vt[name][...] = val.T

            hs = pl.ds(pl.multiple_of(h * N, N), N)
            s = s_ref[...]
            kk_h = vt['kk'][hs, :]
            sa = jnp.sum(s * (-kk_h)[None], axis=1, keepdims=True)
            s1 = s * vt['w'][hs, :][None] + sa * vt['ka'][hs, :][None] + vt['v'][hs, :][:, None, :] * vt['kf'][hs, :][None]
            so_ref[...] = s1
            yt[hs, :] = jnp.sum(s1 * vt['r'][hs, :][None], axis=1)

            @pl.when(h == RWKV_H - 1)
            def _():
                r, lw, kf, v, kk, a, g = _rwkv_mix(rw_ref[...], sh_ref[...], pr)
                y_ref[...] = _rwkv_out(yt[...].T, r, kf, v, g, pr).astype(bf16)

        if buf is not None:
            update()
        else:
            pl.when(pl.program_id(0) == l)(update)

            @pl.when(pl.program_id(0) != l)
            def _():
                so_ref[...] = jnp.zeros_like(so_ref)

    const = lambda d, ix, pk: (0, 0)
    in_specs = [pl.BlockSpec((pl.Element(B), pl.Element(RW_BLK)), wrap(lambda d, ix, pk: (0, OFF_RW))),
                pl.BlockSpec((B, RW_BLK), wrap(const)),
                pl.BlockSpec((None, None, N, N, B), wrap(lambda d, ix, pk: (l, pk[0], 0, 0, 0)))]
    in_specs += [pl.BlockSpec(rp[k].shape, wrap(const)) for k in _RW_PKEYS]
    args = [proj, shift_p, s_t] + [rp[k] for k in _RW_PKEYS]
    aliases = {}
    if buf is not None:
        in_specs.append(pl.BlockSpec(memory_space=pl.ANY))
        args.append(buf)
        aliases = {len(args) - 1: 1}
    return pl.pallas_call(
        body,
        grid=grid,
        in_specs=in_specs,
        out_specs=[pl.BlockSpec((B, RWKV_W), wrap(const)),
                   pl.BlockSpec((None, None, N, N, B), wrap(lambda d, ix, pk: (d, ix[0], 0, 0, 0)))],
        out_shape=[jax.ShapeDtypeStruct((B, RWKV_W), bf16),
                   jax.ShapeDtypeStruct((DEPTH, RWKV_H, N, N, B), f32)],
        scratch_shapes=[pltpu.VMEM((RWKV_W, B), f32)] * 7,
        input_output_aliases=aliases,
        compiler_params=_cparams(("arbitrary",) * len(grid)),
    )(*args)


S5_N = S5_G * S5_P
S5_KC = 256
S5_NKC = S5_W // S5_KC
S5_TILES = S5_N // LANE
S5_GB = S5_KC // S5_GC


def _s5_params(p):
    a_re, a_im = p['s5_a_re'], p['s5_a_im']
    dstep = jnp.exp(p['s5_log_dt'])[:, None]
    mag = jnp.exp(a_re * dstep)
    ab_re = mag * jnp.cos(a_im * dstep)
    ab_im = mag * jnp.sin(a_im * dstep)
    den = a_re * a_re + a_im * a_im
    n_re = ab_re - 1.0
    f_re = (n_re * a_re + ab_im * a_im) / den
    f_im = (ab_im * a_re - n_re * a_im) / den
    b_re, b_im = p['s5_b_re'], p['s5_b_im']
    bb_re = f_re[..., None] * b_re - f_im[..., None] * b_im
    bb_im = f_re[..., None] * b_im + f_im[..., None] * b_re
    eye = jnp.eye(S5_GB, dtype=f32)

    def in_map(bb):
        t = bb.reshape(S5_NKC, S5_GB, S5_P, S5_GC)
        return jnp.einsum('kgpc,gh->kgchp', t, eye).reshape(S5_NKC, S5_KC, S5_GB * S5_P)

    def out_map(cc):
        t = cc.reshape(S5_NKC, S5_GB, S5_GC, S5_P)
        return jnp.einsum('qgcp,gh->qgphc', t, eye).reshape(S5_NKC, S5_GB * S5_P, S5_KC)

    return dict(wb=jnp.concatenate([in_map(bb_re), in_map(bb_im)], axis=-1),
                wc_re=out_map(p['s5_c_re']), wc_im=out_map(p['s5_c_im']),
                ab_re_t=ab_re.reshape(S5_TILES // 8, 8, LANE), ab_im_t=ab_im.reshape(S5_TILES // 8, 8, LANE),
                ab_re=ab_re.reshape(1, S5_N), ab_im=ab_im.reshape(1, S5_N), d=p['s5_d'].reshape(1, S5_W))


def _s5_seq(proj, sp, B, L):
    Lc = min(L, 512)
    n = L // Lc
    pitch = Lc + 4
    lead = OFF_U % LANE
    base = OFF_U - lead
    width = S5_W + LANE
    nt4 = S5_TILES // 8
    tiles_kc = S5_TILES // S5_NKC

    def body(u_ref, wb_ref, wcr_ref, wci_ref, abr_ref, abi_ref, d_ref, y_ref, sr_ref, si_ref, xr, xi, cr, ci):
        c = pl.program_id(1)

        @pl.when(c == 0)
        def _():
            cr[...] = jnp.zeros_like(cr)
            ci[...] = jnp.zeros_like(ci)

        u = u_ref[:, lead:lead + S5_W]
        ub = u.astype(bf16)
        for kc in range(S5_NKC):
            bu = _dot(ub[:, kc * S5_KC:(kc + 1) * S5_KC], wb_ref[kc].astype(bf16))
            for j in range(tiles_kc):
                t = kc * tiles_kc + j
                xr[pl.ds(t * pitch, Lc), :] = bu[:, j * LANE:(j + 1) * LANE]
                xi[pl.ds(t * pitch, Lc), :] = bu[:, (tiles_kc + j) * LANE:(tiles_kc + j + 1) * LANE]

        abr = [abr_ref[g] for g in range(nt4)]
        abi = [abi_ref[g] for g in range(nt4)]

        def step(t, carry):
            out = []
            for g in range(nt4):
                s_r, s_i = carry[2 * g], carry[2 * g + 1]
                idx = pl.ds(g * 8 * pitch + t, 8, stride=pitch)
                n_r = abr[g] * s_r - abi[g] * s_i + xr[idx, :]
                n_i = abr[g] * s_i + abi[g] * s_r + xi[idx, :]
                xr[idx, :] = n_r
                xi[idx, :] = n_i
                out += [n_r, n_i]
            return tuple(out)

        init = []
        for g in range(nt4):
            init += [cr[g], ci[g]]
        fin = lax.fori_loop(0, Lc, step, tuple(init), unroll=2)
        for g in range(nt4):
            cr[g] = fin[2 * g]
            ci[g] = fin[2 * g + 1]

        for q in range(S5_NKC):
            lr = jnp.concatenate([xr[pl.ds((q * tiles_kc + j) * pitch, Lc), :] for j in range(tiles_kc)], axis=1)
            li = jnp.concatenate([xi[pl.ds((q * tiles_kc + j) * pitch, Lc), :] for j in range(tiles_kc)], axis=1)
            y = _dot(lr.astype(bf16), wcr_ref[q].astype(bf16)) - _dot(li.astype(bf16), wci_ref[q].astype(bf16))
            cs = slice(q * S5_KC, (q + 1) * S5_KC)
            y = y + d_ref[:, cs] * u[:, cs]
            y_ref[:, cs] = jax.nn.gelu(y).astype(bf16)

        @pl.when(c == n - 1)
        def _():
            sr_ref[...] = cr[...]
            si_ref[...] = ci[...]

    full = lambda shp: pl.BlockSpec(shp, lambda b, c: (0,) * len(shp))
    y, sr, si = pl.pallas_call(
        body,
        grid=(B, n),
        in_specs=[pl.BlockSpec((pl.Element(Lc), pl.Element(width)),
                               lambda b, c: (pl.multiple_of((b * n + c) * Lc, Lc), base)),
                  full(sp['wb'].shape), full(sp['wc_re'].shape), full(sp['wc_im'].shape),
                  full(sp['ab_re_t'].shape), full(sp['ab_im_t'].shape), full((1, S5_W))],
        out_specs=[pl.BlockSpec((Lc, S5_W), lambda b, c: (b * n + c, 0)),
                   pl.BlockSpec((None, nt4, 8, LANE), lambda b, c: (b, 0, 0, 0)),
                   pl.BlockSpec((None, nt4, 8, LANE), lambda b, c: (b, 0, 0, 0))],
        out_shape=[jax.ShapeDtypeStruct((B * L, S5_W), bf16),
                   jax.ShapeDtypeStruct((B, nt4, 8, LANE), f32),
                   jax.ShapeDtypeStruct((B, nt4, 8, LANE), f32)],
        scratch_shapes=[pltpu.VMEM((S5_TILES * pitch, LANE), f32), pltpu.VMEM((S5_TILES * pitch, LANE), f32),
                        pltpu.VMEM((nt4, 8, LANE), f32), pltpu.VMEM((nt4, 8, LANE), f32)],
        compiler_params=_cparams(("parallel", "arbitrary")),
    )(proj, sp['wb'], sp['wc_re'], sp['wc_im'], sp['ab_re_t'], sp['ab_im_t'], sp['d'])
    return y, sr.reshape(B, S5_G, S5_P), si.reshape(B, S5_G, S5_P)


def _s5_step(proj, x_re, x_im, sp):
    B = proj.shape[0]
    lead = OFF_U % LANE
    base = OFF_U - lead
    width = S5_W + LANE
    kw = S5_N // S5_NKC

    def body(u_ref, xr_ref, xi_ref, wb_ref, wcr_ref, wci_ref, abr_ref, abi_ref, d_ref, y_ref, sr_ref, si_ref):
        u = u_ref[:, lead:lead + S5_W]
        ub = u.astype(bf16)
        abr, abi = abr_ref[...], abi_ref[...]
        xr, xi = xr_ref[...], xi_ref[...]
        for kc in range(S5_NKC):
            bu = _dot(ub[:, kc * S5_KC:(kc + 1) * S5_KC], wb_ref[kc].astype(bf16))
            sl = slice(kc * kw, (kc + 1) * kw)
            n_r = abr[:, sl] * xr[:, sl] - abi[:, sl] * xi[:, sl] + bu[:, :kw]
            n_i = abr[:, sl] * xi[:, sl] + abi[:, sl] * xr[:, sl] + bu[:, kw:]
            sr_ref[:, sl] = n_r
            si_ref[:, sl] = n_i
            y = _dot(n_r.astype(bf16), wcr_ref[kc].astype(bf16)) - _dot(n_i.astype(bf16), wci_ref[kc].astype(bf16))
            cs = slice(kc * S5_KC, (kc + 1) * S5_KC)
            y = y + d_ref[:, cs] * u[:, cs]
            y_ref[:, cs] = jax.nn.gelu(y).astype(bf16)

    full = lambda shp: pl.BlockSpec(shp, lambda i: (0,) * len(shp))
    y, sr, si = pl.pallas_call(
        body,
        grid=(1,),
        in_specs=[pl.BlockSpec((pl.Element(B), pl.Element(width)), lambda i: (0, base)),
                  full((B, S5_N)), full((B, S5_N)),
                  full(sp['wb'].shape), full(sp['wc_re'].shape), full(sp['wc_im'].shape),
                  full((1, S5_N)), full((1, S5_N)), full((1, S5_W))],
        out_specs=[full((B, S5_W)), full((B, S5_N)), full((B, S5_N))],
        out_shape=[jax.ShapeDtypeStruct((B, S5_W), bf16),
                   jax.ShapeDtypeStruct((B, S5_N), f32), jax.ShapeDtypeStruct((B, S5_N), f32)],
        compiler_params=_cparams(("arbitrary",)),
    )(proj, x_re.reshape(B, S5_N), x_im.reshape(B, S5_N), sp['wb'], sp['wc_re'], sp['wc_im'],
      sp['ab_re'], sp['ab_im'], sp['d'])
    return y, sr.reshape(B, S5_G, S5_P), si.reshape(B, S5_G, S5_P)


MOE_RT, MOE_RT_SMALL = 128, 32
SLAB = D // LANE
SUB = 8
SLAB_PITCH = SLAB + SUB
H2_PITCH = SLAB + SUB
O2_PITCH = TOP_K * SLAB + SUB


def _moe_rt(T):
    return MOE_RT if T * TOP_K >= N_EXPERTS * MOE_RT else MOE_RT_SMALL
ROUTE_W = LANE


def _router(x, g, mod, w_r, b_r, row0, t_all, bufs):
    T = x.shape[0]
    tm = min(mod.tm, 512)
    nm = T // tm
    assert row0 % tm == 0
    if bufs is None:
        assert row0 == 0
        n_steps = pl.cdiv(t_all, tm)
    else:
        n_steps = nm
    clamp = lambda m: jnp.minimum(m, nm - 1)

    def body(x_ref, g_ref, sh_ref, sc_ref, w_ref, b_ref, *rest):
        h_ref, e_ref, p_ref = rest[-3:]
        if n_steps > nm:
            @pl.when(pl.program_id(0) >= nm)
            def _():
                h_ref[...] = jnp.zeros_like(h_ref)
                e_ref[...] = jnp.zeros_like(e_ref)
                p_ref[...] = jnp.zeros_like(p_ref)

            pl.when(pl.program_id(0) < nm)(lambda: route(x_ref, g_ref, sh_ref, sc_ref, w_ref, b_ref, *rest[-3:]))
        else:
            route(x_ref, g_ref, sh_ref, sc_ref, w_ref, b_ref, *rest[-3:])

    def route(x_ref, g_ref, sh_ref, sc_ref, w_ref, b_ref, h_ref, e_ref, p_ref):
        h2 = _rms(x_ref[...], g_ref[...]) * (1.0 + sc_ref[...]) + sh_ref[...]
        for s in range(SLAB):
            h_ref[pl.ds(s, tm, stride=H2_PITCH), :] = h2[:, s * LANE:(s + 1) * LANE]
        for s in range(SLAB, H2_PITCH):
            h_ref[pl.ds(s, tm, stride=H2_PITCH), :] = jnp.zeros((tm, LANE), f32)
        logits = _dot(h2.astype(bf16), w_ref[...].astype(bf16)) + b_ref[...]
        lane = lax.broadcasted_iota(i32, (tm, ROUTE_W), 1)
        ninf = jnp.float32(-jnp.inf)
        gl = jnp.where(lane < N_GROUPS, logits, ninf)
        gm = jnp.max(gl, axis=-1, keepdims=True)
        g_p = 1.0 / jnp.sum(jnp.exp(gl - gm), axis=-1, keepdims=True)
        g_idx = jnp.min(jnp.where(gl == gm, lane, ROUTE_W), axis=-1, keepdims=True)
        valid = (lane >= N_GROUPS) & (lane < N_GROUPS + N_EXPERTS) & (((lane - N_GROUPS) // EPG) == g_idx)
        el = jnp.where(valid, logits, ninf)
        ee = jnp.exp(el - jnp.max(el, axis=-1, keepdims=True))
        prob = jnp.where(valid, ee / jnp.sum(ee, axis=-1, keepdims=True), -1.0)
        p1 = jnp.max(prob, axis=-1, keepdims=True)
        i1 = jnp.min(jnp.where(prob == p1, lane, ROUTE_W), axis=-1, keepdims=True)
        prob2 = jnp.where(lane == i1, -1.0, prob)
        p2 = jnp.max(prob2, axis=-1, keepdims=True)
        i2 = jnp.min(jnp.where(prob2 == p2, lane, ROUTE_W), axis=-1, keepdims=True)
        den = p1 + p2
        e_ref[...] = jnp.where(lane == 0, i1 - N_GROUPS, jnp.where(lane == 1, i2 - N_GROUPS, 0))
        p_ref[...] = jnp.where(lane == 0, g_p * p1 / den, jnp.where(lane == 1, g_p * p2 / den, 0.0))

    in_specs = [pl.BlockSpec((tm, D), lambda m: (clamp(m), 0)),
                pl.BlockSpec((1, D), lambda m: (0, 0)),
                mod.row_spec(3, tm, clamp), mod.row_spec(4, tm, clamp),
                pl.BlockSpec((D, ROUTE_W), lambda m: (0, 0)),
                pl.BlockSpec((1, ROUTE_W), lambda m: (0, 0))]
    args = [x, g.reshape(1, D), mod.arr, mod.arr, w_r, b_r]
    aliases = {}
    if bufs is not None:
        in_specs += [pl.BlockSpec(memory_space=pl.ANY)] * 3
        aliases = {len(args) + i: i for i in range(3)}
        args += list(bufs)
    b0 = row0 // tm
    return pl.pallas_call(
        body,
        grid=(n_steps,),
        in_specs=in_specs,
        out_specs=[pl.BlockSpec((tm * H2_PITCH, LANE), lambda m: (b0 + m, 0)),
                   pl.BlockSpec((tm, ROUTE_W), lambda m: (b0 + m, 0)),
                   pl.BlockSpec((tm, ROUTE_W), lambda m: (b0 + m, 0))],
        out_shape=[jax.ShapeDtypeStruct((t_all * H2_PITCH, LANE), f32), jax.ShapeDtypeStruct((t_all, ROUTE_W), i32),
                   jax.ShapeDtypeStruct((t_all, ROUTE_W), f32)],
        input_output_aliases=aliases,
        compiler_params=_cparams(("arbitrary",)),
    )(*args)


TOK_BITS = 14


def _moe_plan(eid, T):
    A = T * TOP_K
    RT = _moe_rt(T)
    nt = pl.cdiv(A, RT) + N_EXPERTS
    flat_e = eid[:, :TOP_K].reshape(-1)
    order = jnp.argsort(flat_e).astype(i32)
    counts = jnp.sum((flat_e[:, None] == jnp.arange(N_EXPERTS, dtype=i32)[None, :]).astype(i32), axis=0)
    start = jnp.cumsum(counts) - counts
    pcnt = (counts + RT - 1) // RT * RT
    pend = jnp.cumsum(pcnt)
    tile_e = jnp.minimum(jnp.sum((pend[None, :] <= (jnp.arange(nt, dtype=i32) * RT)[:, None]).astype(i32), axis=1),
                         N_EXPERTS - 1)
    eidx = jnp.arange(N_EXPERTS, dtype=i32)
    is_e = tile_e[:, None] == eidx[None, :]
    per_tile = lambda tbl: jnp.sum(jnp.where(is_e, tbl[None, :], 0), axis=1)
    t_cnt, t_start = per_tile(counts), per_tile(start)
    off = (jnp.arange(nt, dtype=i32) * RT - per_tile(pend - pcnt))[:, None] + jnp.arange(RT, dtype=i32)[None, :]
    real = off < t_cnt[:, None]
    srt = t_start[:, None] + jnp.minimum(off, t_cnt[:, None])
    a = order[jnp.minimum(srt, A - 1)]
    slot = jnp.arange(nt * RT, dtype=i32).reshape(nt, RT)
    tok = jnp.where(real, a // TOP_K, 0)
    dst = jnp.where(real, a, A + slot - srt)
    n_used = (pend[-1] // RT).astype(i32).reshape(1)
    in_use = counts > 0
    rank = jnp.cumsum(in_use.astype(i32)) - 1
    later = (eidx[None, :] > eidx[:, None]) & in_use[None, :]
    nxt = jnp.min(jnp.where(later, eidx[None, :], N_EXPERTS), axis=1)
    nxt = jnp.where(nxt < N_EXPERTS, nxt, -1)
    tile_info = jnp.concatenate([tile_e, per_tile(rank % 2), per_tile(nxt)]).astype(i32)
    return (tok | (dst << TOK_BITS)).reshape(-1), tile_info, n_used


def _experts(h2, packed, tile_e, n_used, l, w1, w3, w2):
    T = h2.shape[0] // H2_PITCH
    assert TOP_K == 2 and T < (1 << TOK_BITS)
    RT = _moe_rt(T)
    PITCH = SLAB_PITCH
    nt = tile_e.shape[0] // 3
    prime_id = nt * RT
    out_rows = (prime_id + RT) // TOP_K
    RING = 4
    HC, OC = 128, 256
    n_hc, n_oc = D_EXPERT // HC, D // OC
    g_per, s_per = RT // n_hc, RT // n_oc

    def body(tile_ref, nused_ref, slot_ref, h_hbm, w1_hbm, w3_hbm, w2_hbm, o_hbm, *scratch):
        xs, os_ = scratch[:RING], scratch[RING:2 * RING]
        w1b, w3b, w2b, wf1, wf3, wf2, gsem, ssem, wsem = scratch[2 * RING:]
        x0 = xs[0]
        j = pl.program_id(0)
        n_used = nused_ref[0]
        active = j < n_used

        def gather_row(b, r, tok):
            src = pl.multiple_of(tok * H2_PITCH, SUB)
            return pltpu.make_async_copy(h_hbm.at[pl.ds(src, SLAB), :], xs[b].at[pl.ds(r * PITCH, SLAB), :],
                                         gsem.at[b])

        def scatter_row(b, r, d):
            dst = pl.multiple_of(lax.shift_right_logical(d, 1) * O2_PITCH + (d & (TOP_K - 1)) * SLAB, SUB)
            return pltpu.make_async_copy(os_[b].at[pl.ds(r * PITCH, SLAB), :], o_hbm.at[pl.ds(dst, SLAB), :],
                                         ssem.at[b])

        def wait_rows(sem):
            pltpu.make_async_copy(h_hbm.at[pl.ds(0, RT * SLAB), :], x0.at[pl.ds(0, RT * SLAB), :], sem).wait()

        tok_of = lambda s: s & ((1 << TOK_BITS) - 1)
        dst_of = lambda s: lax.shift_right_logical(s, TOK_BITS)
        last_tile = n_used - 1

        @pl.when(j == 0)
        def _():
            os_[RING - 1][...] = jnp.zeros_like(os_[RING - 1])
            for b in range(RING - 1):
                base = jnp.minimum(b, last_tile) * RT

                def one(r, _, b=b, base=base):
                    gather_row(b, r, tok_of(slot_ref[base + r])).start()
                    return 0
                lax.fori_loop(0, RT, one, 0)

        def fetch(e, s):
            return [pltpu.make_async_copy(w_hbm.at[l, e], wf.at[s], wsem.at[s])
                    for w_hbm, wf in ((w1_hbm, wf1), (w3_hbm, wf3), (w2_hbm, wf2))]

        @pl.when(j == 0)
        def _():
            for c in fetch(tile_ref[0], 0):
                c.start()

        @pl.when(active & ((j == 0) | (tile_ref[j] != tile_ref[jnp.maximum(j - 1, 0)])))
        def _():
            ws, e_nxt = tile_ref[nt + j], tile_ref[2 * nt + j]
            for c in fetch(tile_ref[j], ws):
                c.wait()
            w1b[...] = wf1[ws].astype(bf16)
            w3b[...] = wf3[ws].astype(bf16)
            w2b[...] = wf2[ws].astype(bf16)

            @pl.when(e_nxt >= 0)
            def _():
                for c in fetch(e_nxt, 1 - ws):
                    c.start()

        def tile(cur):
            nxt_b = prv_b = (cur + RING - 1) % RING

            @pl.when(j >= RING - 1)
            def _():
                wait_rows(ssem.at[cur])

            wait_rows(gsem.at[cur])
            xb = jnp.concatenate([xs[cur][pl.ds(s, RT, stride=PITCH), :] for s in range(SLAB)],
                                 axis=1).astype(bf16)
            nxt = jnp.minimum(j + RING - 1, last_tile) * RT
            prv = jnp.maximum(j - 1, 0) * RT
            parts = []
            for c in range(n_hc):
                h1 = _dot(xb, w1b[:, c * HC:(c + 1) * HC])
                h3 = _dot(xb, w3b[:, c * HC:(c + 1) * HC])
                parts.append(((h1 * jax.nn.sigmoid(h1)) * h3).astype(bf16))
                for r in range(c * g_per, (c + 1) * g_per):
                    gather_row(nxt_b, r, tok_of(slot_ref[nxt + r])).start(priority=r % 2)
            hm = jnp.concatenate(parts, axis=1)
            for c in range(n_oc):
                res = _dot(hm, w2b[:, c * OC:(c + 1) * OC])
                for t in range(OC // LANE):
                    os_[cur][pl.ds(c * (OC // LANE) + t, RT, stride=PITCH), :] = res[:, t * LANE:(t + 1) * LANE]
                for r in range(c * s_per, (c + 1) * s_per):
                    d = jnp.where(j == 0, prime_id + r, dst_of(slot_ref[prv + r]))
                    scatter_row(prv_b, r, d).start(priority=r % 2)

        def last(cur):
            def one(r, _):
                scatter_row(cur, r, dst_of(slot_ref[j * RT + r])).start()
                return 0
            lax.fori_loop(0, RT, one, 0)
            wait_rows(ssem.at[cur])
            for back in range(1, RING):
                pl.when(j >= back - 1)(lambda b=(cur - back) % RING: wait_rows(ssem.at[b]))
            for ahead in range(1, RING):
                wait_rows(gsem.at[(cur + ahead) % RING])

        for cur in range(RING):
            pl.when(active & (j % RING == cur))(lambda cur=cur: tile(cur))
        for cur in range(RING):
            pl.when((j == last_tile) & (j % RING == cur))(lambda cur=cur: last(cur))

    grid_spec = pltpu.PrefetchScalarGridSpec(
        num_scalar_prefetch=3,
        grid=(nt,),
        in_specs=[pl.BlockSpec(memory_space=pl.ANY)] * 4,
        out_specs=pl.BlockSpec(memory_space=pl.ANY),
        scratch_shapes=[pltpu.VMEM((RT * PITCH, LANE), f32)] * (2 * RING) + [
                        pltpu.VMEM((D, D_EXPERT), bf16), pltpu.VMEM((D, D_EXPERT), bf16),
                        pltpu.VMEM((D_EXPERT, D), bf16),
                        pltpu.VMEM((2, D, D_EXPERT), f32), pltpu.VMEM((2, D, D_EXPERT), f32),
                        pltpu.VMEM((2, D_EXPERT, D), f32),
                        pltpu.SemaphoreType.DMA((RING,)), pltpu.SemaphoreType.DMA((RING,)),
                        pltpu.SemaphoreType.DMA((2,))],
    )
    return pl.pallas_call(
        body,
        grid_spec=grid_spec,
        out_shape=jax.ShapeDtypeStruct((out_rows * O2_PITCH, LANE), f32),
        compiler_params=_cparams(("arbitrary",)),
    )(tile_e, n_used, packed, h2, w1, w3, w2)


def _moe_combine(x, o2, wgt, mod, norm_final, row0):
    T = x.shape[0]
    tm = min(mod.tm, 512)
    b0 = row0 // tm

    def body(x_ref, o_ref, p_ref, m_ref, *rest):
        p = p_ref[...]
        row = lambda k: jnp.concatenate(
            [o_ref[pl.ds(k * SLAB + s, tm, stride=O2_PITCH), :] for s in range(SLAB)], axis=1)
        y = p[:, 0:1] * row(0)
        for k in range(1, TOP_K):
            y = y + p[:, k:k + 1] * row(k)
        xo = x_ref[...] + m_ref[...] * y
        if norm_final is None:
            rest[0][...] = xo
        else:
            rest[1][...] = _rms(xo, rest[0][...])

    ins = [x, o2, wgt, mod.arr]
    specs = [pl.BlockSpec((tm, D), lambda m: (m, 0)),
             pl.BlockSpec((tm * O2_PITCH, LANE), lambda m: (b0 + m, 0)),
             pl.BlockSpec((tm, ROUTE_W), lambda m: (b0 + m, 0)), mod.row_spec(5, tm)]
    if norm_final is not None:
        ins.append(norm_final.reshape(1, D))
        specs.append(pl.BlockSpec((1, D), lambda m: (0, 0)))
    return pl.pallas_call(
        body,
        grid=(T // tm,),
        in_specs=specs,
        out_specs=pl.BlockSpec((tm, D), lambda m: (m, 0)),
        out_shape=jax.ShapeDtypeStruct((T, D), f32),
        compiler_params=_cparams(("parallel",)),
    )(*ins)


def _moe(groups, p, l, big, norm_final):
    pad = ROUTE_W - N_GROUPS - N_EXPERTS
    w_r = jnp.pad(jnp.concatenate([p['moe_w_group'], p['moe_w_router']], axis=1), ((0, 0), (0, pad)))
    b_r = jnp.pad(jnp.concatenate([p['moe_b_group'], p['moe_b_router']]), (0, pad)).reshape(1, ROUTE_W)
    t_all = sum(g.T for g in groups)
    bufs, row0 = None, 0
    for g in groups:
        g.row0 = row0
        bufs = _router(g.x, p['norm_ffn'], g.mod, w_r, b_r, row0, t_all, bufs)
        row0 += g.T
    h2, eid, wgt = bufs
    packed, tile_e, n_used = _moe_plan(eid, t_all)
    o2 = _experts(h2, packed, tile_e, n_used, l, big['moe_w1'], big['moe_w3'], big['moe_w2'])
    for g in groups:
        g.x = _moe_combine(g.x, o2, wgt, g.mod, norm_final, g.row0)


class _Group:
    def __init__(self, x, mods, states, pos0):
        self.B, self.L, _ = x.shape
        self.T = self.B * self.L
        self.tm = _row_tile(self.T)
        self.x = x.reshape(self.T, D)
        self.mods, self.states, self.pos0 = mods, states, pos0
        self.outs = ([], [], [], [], [])
        self.ret_all = self.rw_all = None

    def mix(self, l, p, rp, sp, layers, big, w_in_t):
        B, L = self.B, self.L
        self.mod = mod = _Mod(self.mods[l], L, self.tm)
        proj = _in_proj(self.x, layers['norm_mix'], l, mod, w_in_t)
        if self.states is None:
            y_ret, s_ret = _retention_seq(proj, B, L)
            y_rw, s_rw, shift = _rwkv_seq(proj, rp, B, L)
            y_s5, s_re, s_im = _s5_seq(proj, sp, B, L)
        else:
            st_ret, st_rw, st_shift, st_re, st_im = self.states
            y_ret, self.ret_all = _retention_step(proj, st_ret, l, self.ret_all, self.pos0)
            y_rw, self.rw_all = _rwkv_step(proj, st_shift[l], jnp.transpose(st_rw, (0, 2, 3, 4, 1)), l,
                                           self.rw_all, rp)
            s_ret = s_rw = None
            shift = proj[:, OFF_RW:OFF_RW + RWKV_PROJ]
            y_s5, s_re, s_im = _s5_step(proj, st_re[l], st_im[l], sp)
        z = _glu_proj(y_s5, l, big['s5_w_glu'])
        merged = _merge_proj(y_ret, y_rw, z, l, big['ret_w_o'], big['rwkv_w_o'], big['s5_w_o'], proj)
        self.x = _out_proj(merged, l, big['w_out'], self.x, mod)
        for lst, val in zip(self.outs, (s_ret, s_rw, shift, s_re, s_im)):
            lst.append(val)

    def results(self):
        stacked = [jnp.stack(o) if o[0] is not None else None for o in self.outs]
        if self.states is not None:
            stacked[0], stacked[1] = self.ret_all, jnp.transpose(self.rw_all, (0, 4, 1, 2, 3))
        return self.x.reshape(self.B, self.L, D), stacked


def _trunk(groups, layers, norm_final):
    big = {k: layers[k] for k in _BIG}
    w_in_t = jnp.swapaxes(layers['w_in'], 1, 2)
    for l in range(DEPTH):
        p = {name: arr[l] for name, arr in layers.items() if name not in _BIG + ('w_in',)}
        rp, sp = _rwkv_params(p), _s5_params(p)
        for g in groups:
            g.mix(l, p, rp, sp, layers, big, w_in_t)
        _moe(groups, p, l, big, norm_final if l == DEPTH - 1 else None)
    return [g.results() for g in groups]


_BIG = ('ret_w_o', 'rwkv_w_o', 's5_w_glu', 's5_w_o', 'w_out', 'moe_w1', 'moe_w3', 'moe_w2')


def kernel(x_prompt, x_sample, state_ret, state_rwkv, state_shift, state_s5_re, state_s5_im,
           c_prompt, c_sample, norm_mix, norm_ffn, w_ada, b_ada, w_in, ret_w_o, rwkv_mu, rwkv_w0,
           rwkv_w2, rwkv_a0, rwkv_a2, rwkv_g2, rwkv_k_k, rwkv_k_a, rwkv_r_k, rwkv_ln_w, rwkv_ln_b,
           rwkv_w_o, s5_a_re, s5_a_im, s5_b_re, s5_b_im, s5_c_re, s5_c_im, s5_d, s5_log_dt, s5_w_glu,
           s5_w_o, w_out, moe_w_group, moe_b_group, moe_w_router, moe_b_router, moe_w1, moe_w3, moe_w2,
           norm_final):
    layers = {
        'norm_mix': norm_mix, 'norm_ffn': norm_ffn, 'w_in': w_in,
        'ret_w_o': ret_w_o, 'rwkv_mu': rwkv_mu, 'rwkv_w0': rwkv_w0, 'rwkv_w2': rwkv_w2,
        'rwkv_a0': rwkv_a0, 'rwkv_a2': rwkv_a2, 'rwkv_g2': rwkv_g2, 'rwkv_k_k': rwkv_k_k,
        'rwkv_k_a': rwkv_k_a, 'rwkv_r_k': rwkv_r_k, 'rwkv_ln_w': rwkv_ln_w, 'rwkv_ln_b': rwkv_ln_b,
        'rwkv_w_o': rwkv_w_o, 's5_a_re': s5_a_re, 's5_a_im': s5_a_im, 's5_b_re': s5_b_re,
        's5_b_im': s5_b_im, 's5_c_re': s5_c_re, 's5_c_im': s5_c_im, 's5_d': s5_d,
        's5_log_dt': s5_log_dt, 's5_w_glu': s5_w_glu, 's5_w_o': s5_w_o, 'w_out': w_out,
        'moe_w_group': moe_w_group, 'moe_b_group': moe_b_group, 'moe_w_router': moe_w_router,
        'moe_b_router': moe_b_router, 'moe_w1': moe_w1, 'moe_w3': moe_w3, 'moe_w2': moe_w2,
    }
    Bp, Bs = x_prompt.shape[0], x_sample.shape[0]
    s_off = -(-Bp // 8) * 8
    c_all = jnp.concatenate([c_prompt, jnp.zeros((s_off - Bp, D), f32), c_sample], axis=0)
    mod_all = _adaln(c_all, w_ada, b_ada)
    prompt = _Group(x_prompt, mod_all[:, :Bp], None, 0.0)
    sample = _Group(x_sample, mod_all[:, s_off:s_off + Bs],
                    (state_ret, state_rwkv, state_shift, state_s5_re, state_s5_im), float(PAST_LEN))
    ((y_prompt, (ret_p, rwkv_p, shift_p, s5re_p, s5im_p)),
     (y_sample, (ret_s, rwkv_s, shift_s, s5re_s, s5im_s))) = _trunk([prompt, sample], layers, norm_final)
    return (y_prompt, y_sample, ret_p, ret_s, rwkv_p, rwkv_s, shift_p, shift_s, s5re_p, s5re_s, s5im_p, s5im_s)
```
